```python
import math
import jax, jax.numpy as jnp
from jax import lax
import numpy as np

D_MODEL = 1024
BATCH = 2
SEQ = 8192
DEPTH = 2

HEAD_DIM = 64
BLOCK = 128
EPS = 1e-6
NEG = -1e30
A_HEADS = 4
A_QK_DIM = 2 * HEAD_DIM
A_V_DIM = 2 * HEAD_DIM
B_HEADS = 8
B_KV_HEADS = 2
B_HALF_WINDOW = 128
C_PATTERNS = ((128, 1), (512, 4), (2048, 16))
C_HEADS_PER_GROUP = 4
C_HEADS = C_HEADS_PER_GROUP * len(C_PATTERNS)
D_HEADS = 12
D_Q_LORA = 384
D_KV_LORA = 256
D_NOPE = 64
D_ROPE = 32
D_V = 64
ROPE_THETA = 10000.0
NUM_BUCKETS = 32
MAX_DISTANCE = 1024
N_BIAS_HEADS = A_HEADS + B_HEADS
D_FF = -(-8 * D_MODEL // (3 * 256)) * 256
AB_IN = A_HEADS * (2 * A_QK_DIM + A_V_DIM) + (B_HEADS + 2 * B_KV_HEADS) * HEAD_DIM
AB_OUT = A_HEADS * A_V_DIM + B_HEADS * HEAD_DIM
CD_IN = 3 * C_HEADS * HEAD_DIM + D_Q_LORA + D_KV_LORA + D_ROPE
CD_OUT = C_HEADS_PER_GROUP * HEAD_DIM + D_HEADS * D_V

kernel_name = 'hybrid_diff_swa_dilated_mla_encoder'


def rmsnorm(x, g):
    xf = x.astype(jnp.float32)
    y = xf * lax.rsqrt(jnp.mean(xf * xf, axis=-1, keepdims=True) + EPS)
    return (y * g.astype(jnp.float32)).astype(x.dtype)


def t5_bucket(rel):
    nb = NUM_BUCKETS // 2
    max_exact = nb // 2
    ret = jnp.where(rel > 0, nb, 0)
    n = jnp.abs(rel)
    nf = jnp.maximum(n, 1).astype(jnp.float32)
    large = max_exact + (jnp.log(nf / max_exact) / math.log(MAX_DISTANCE / max_exact)
                         * (nb - max_exact)).astype(jnp.int32)
    large = jnp.minimum(large, nb - 1)
    return ret + jnp.where(n < max_exact, n, large)


def heads(t, n):
    B, S, _ = t.shape
    return t.reshape(B, S, n, -1).transpose(0, 2, 1, 3)


def merge_heads(t):
    B, n, S, d = t.shape
    return t.transpose(0, 2, 1, 3).reshape(B, S, n * d)


def rope(t, pos):
    half = D_ROPE // 2
    inv = ROPE_THETA ** (-jnp.arange(half, dtype=jnp.float32) / half)
    ang = pos.astype(jnp.float32)[:, None] * inv[None, :]
    cos, sin = jnp.cos(ang)[:, None, :], jnp.sin(ang)[:, None, :]
    tf = t.astype(jnp.float32)
    t1, t2 = tf[..., :half], tf[..., half:]
    return jnp.concatenate([t1 * cos - t2 * sin, t2 * cos + t1 * sin], axis=-1).astype(t.dtype)


def diff_attention(q1, q2, k1, k2, v, lam, bias_tab):
    B, H, S, d = q1.shape
    scale = d ** -0.5
    kpos = jnp.arange(S)

    def block(b):
        qs = b * BLOCK
        q1b = lax.dynamic_slice_in_dim(q1, qs, BLOCK, axis=2)
        q2b = lax.dynamic_slice_in_dim(q2, qs, BLOCK, axis=2)
        qpos = qs + jnp.arange(BLOCK)
        bias = bias_tab[t5_bucket(kpos[None, :] - qpos[:, None])]
        bias = bias.transpose(2, 0, 1).astype(jnp.float32)
        s1 = jnp.einsum('bhqd,bhkd->bhqk', q1b, k1).astype(jnp.float32) * scale + bias
        s2 = jnp.einsum('bhqd,bhkd->bhqk', q2b, k2).astype(jnp.float32) * scale + bias
        p = jax.nn.softmax(s1, axis=-1) - lam * jax.nn.softmax(s2, axis=-1)
        return jnp.einsum('bhqk,bhkd->bhqd', p.astype(v.dtype), v)

    out = lax.map(block, jnp.arange(S // BLOCK))
    return out.transpose(1, 2, 0, 3, 4).reshape(B, H, S, v.shape[-1])


def dense_attention(q, k, v, scale):
    B, H, S, _ = q.shape

    def block(b):
        qb = lax.dynamic_slice_in_dim(q, b * BLOCK, BLOCK, axis=2)
        s = jnp.einsum('bhqd,bhkd->bhqk', qb, k).astype(jnp.float32) * scale
        p = jax.nn.softmax(s, axis=-1).astype(v.dtype)
        return jnp.einsum('bhqk,bhkd->bhqd', p, v)

    out = lax.map(block, jnp.arange(S // BLOCK))
    return out.transpose(1, 2, 0, 3, 4).reshape(B, H, S, v.shape[-1])


def banded_attention(q, k, v, half_window, bias_tab, dilation, sink=None):
    B, H, L, dq = q.shape
    G = k.shape[1]
    rep = H // G
    dv = v.shape[-1]
    W = half_window
    nb = -(-L // BLOCK)
    Lp = nb * BLOCK
    span = BLOCK + 2 * W
    qp = jnp.pad(q, ((0, 0), (0, 0), (0, Lp - L), (0, 0)))
    kp = jnp.pad(k, ((0, 0), (0, 0), (W, Lp - L + W), (0, 0)))
    vp = jnp.pad(v, ((0, 0), (0, 0), (W, Lp - L + W), (0, 0)))
    idx = jnp.arange(nb)[:, None] * BLOCK + jnp.arange(span)[None, :]
    kb = kp[:, :, idx]
    vb = vp[:, :, idx]
    qb = qp.reshape(B, G, rep, nb, BLOCK, dq)
    s = jnp.einsum('bgrnqd,bgnkd->bgrnqk', qb, kb).astype(jnp.float32) * dq ** -0.5
    rel = jnp.arange(span)[None, :] - W - jnp.arange(BLOCK)[:, None]
    bias = bias_tab[t5_bucket(rel * dilation)].astype(jnp.float32)
    bias = bias.transpose(2, 0, 1).reshape(G, rep, 1, BLOCK, span)
    kpos = idx - W
    valid = ((jnp.abs(rel) <= W)[None]
             & (kpos >= 0)[:, None, :] & (kpos < L)[:, None, :])
    s = jnp.where(valid, s + bias, NEG)
    m = jnp.max(s, axis=-1, keepdims=True)
    if sink is not None:
        sk = sink.astype(jnp.float32).reshape(G, rep, 1, 1, 1)
        m = jnp.maximum(m, sk)
    e = jnp.exp(s - m)
    denom = jnp.sum(e, axis=-1, keepdims=True)
    if sink is not None:
        denom = denom + jnp.exp(sk - m)
    o = jnp.einsum('bgrnqk,bgnkd->bgrnqd', (e / denom).astype(v.dtype), vb)
    lse = (m + jnp.log(denom))[..., 0]
    o = o.reshape(B, H, Lp, dv)[:, :, :L]
    lse = lse.reshape(B, H, Lp)[:, :, :L]
    return o, lse


def dilated_group(q, k, v, window, dilation, bias_tab):
    B, H, S, d = q.shape
    L = S // dilation

    def to_strided(t):
        return t.reshape(B, H, L, dilation, d).transpose(0, 3, 1, 2, 4).reshape(B * dilation, H, L, d)

    o, lse = banded_attention(to_strided(q), to_strided(k), to_strided(v),
                              window // (2 * dilation), bias_tab, dilation)
    o = o.reshape(B, dilation, H, L, d).transpose(0, 2, 3, 1, 4).reshape(B, H, S, d)
    lse = lse.reshape(B, dilation, H, L).transpose(0, 2, 3, 1).reshape(B, H, S)
    return o, lse


def mixer_ab(h, w_in, lam_q1, lam_k1, lam_q2, lam_k2, subln, sink, w_o, bias_table, lambda_init):
    proj = h @ w_in
    o1 = A_HEADS * A_QK_DIM
    o2 = o1 + A_HEADS * A_QK_DIM
    o3 = o2 + A_HEADS * A_V_DIM
    o4 = o3 + B_HEADS * HEAD_DIM
    o5 = o4 + B_KV_HEADS * HEAD_DIM
    qa = heads(proj[..., :o1], A_HEADS)
    ka = heads(proj[..., o1:o2], A_HEADS)
    va = heads(proj[..., o2:o3], A_HEADS)
    lam = (jnp.exp(jnp.sum(lam_q1.astype(jnp.float32) * lam_k1.astype(jnp.float32)))
           - jnp.exp(jnp.sum(lam_q2.astype(jnp.float32) * lam_k2.astype(jnp.float32)))
           + lambda_init)
    oa = diff_attention(qa[..., :HEAD_DIM], qa[..., HEAD_DIM:], ka[..., :HEAD_DIM], ka[..., HEAD_DIM:],
                        va, lam, bias_table[:, :A_HEADS])
    oa = rmsnorm(oa, subln) * (1.0 - lambda_init)
    qb = heads(proj[..., o3:o4], B_HEADS)
    kb = heads(proj[..., o4:o5], B_KV_HEADS)
    vb = heads(proj[..., o5:], B_KV_HEADS)
    ob, _ = banded_attention(qb, kb, vb, B_HALF_WINDOW, bias_table[:, A_HEADS:A_HEADS + B_HEADS], 1, sink)
    return jnp.concatenate([merge_heads(oa), merge_heads(ob)], axis=-1) @ w_o


def mixer_cd(h, w_in, q_norm, w_q_b, kv_norm, w_kv_b, w_o, bias_table, pos):
    B, S, _ = h.shape
    proj = h @ w_in
    cw = C_HEADS * HEAD_DIM
    o1, o2, o3 = cw, 2 * cw, 3 * cw
    o4 = o3 + D_Q_LORA
    o5 = o4 + D_KV_LORA
    qc = heads(proj[..., :o1], C_HEADS)
    kc = heads(proj[..., o1:o2], C_HEADS)
    vc = heads(proj[..., o2:o3], C_HEADS)
    outs, lses = [], []
    for g, (window, dilation) in enumerate(C_PATTERNS):
        sl = slice(g * C_HEADS_PER_GROUP, (g + 1) * C_HEADS_PER_GROUP)
        o_g, lse_g = dilated_group(qc[:, sl], kc[:, sl], vc[:, sl], window, dilation, bias_table[:, sl])
        outs.append(o_g)
        lses.append(lse_g)
    wts = jax.nn.softmax(jnp.stack(lses, axis=0), axis=0)
    oc = jnp.sum(wts[..., None] * jnp.stack(outs, axis=0).astype(jnp.float32), axis=0).astype(h.dtype)
    cq = rmsnorm(proj[..., o3:o4], q_norm)
    q = (cq @ w_q_b).reshape(B, S, D_HEADS, D_NOPE + D_ROPE)
    q_pe = rope(q[..., D_NOPE:], pos)
    ckv = rmsnorm(proj[..., o4:o5], kv_norm)
    kv = (ckv @ w_kv_b).reshape(B, S, D_HEADS, D_NOPE + D_V)
    k_pe = rope(proj[..., o5:].reshape(B, S, 1, D_ROPE), pos)
    qd = jnp.concatenate([q[..., :D_NOPE], q_pe], axis=-1).transpose(0, 2, 1, 3)
    kd = jnp.concatenate([kv[..., :D_NOPE], jnp.broadcast_to(k_pe, (B, S, D_HEADS, D_ROPE))],
                         axis=-1).transpose(0, 2, 1, 3)
    vd = kv[..., D_NOPE:].transpose(0, 2, 1, 3)
    od = dense_attention(qd, kd, vd, (D_NOPE + D_ROPE) ** -0.5)
    return jnp.concatenate([merge_heads(oc), merge_heads(od)], axis=-1) @ w_o


def swiglu(h, w_gate, w_up, w_down):
    return (jax.nn.silu(h @ w_gate) * (h @ w_up)) @ w_down


def setup_inputs(seed: int = 0) -> dict:
    key = jax.random.key(seed)
    ks = jax.random.split(key, 24)
    f = jnp.float32
    ne = (DEPTH + 1) // 2
    no = DEPTH // 2

    def w(k, shape, fan_in):
        return jax.random.normal(k, shape, f) * fan_in ** -0.5

    def gain(k, shape):
        return 1.0 + 0.05 * jax.random.normal(k, shape, f)

    return {
        'x': jax.random.normal(ks[0], (BATCH, SEQ, D_MODEL), f),
        'bias_table': 0.3 * jax.random.normal(ks[1], (NUM_BUCKETS, N_BIAS_HEADS), f),
        'attn_norm': gain(ks[2], (DEPTH, D_MODEL)),
        'ffn_norm': gain(ks[3], (DEPTH, D_MODEL)),
        'final_norm': gain(ks[4], (D_MODEL,)),
        'ab_w_in': w(ks[5], (ne, D_MODEL, AB_IN), D_MODEL),
        'ab_lambda_q1': 0.1 * jax.random.normal(ks[6], (ne, HEAD_DIM), f),
        'ab_lambda_k1': 0.1 * jax.random.normal(ks[7], (ne, HEAD_DIM), f),
        'ab_lambda_q2': 0.1 * jax.random.normal(ks[8], (ne, HEAD_DIM), f),
        'ab_lambda_k2': 0.1 * jax.random.normal(ks[9], (ne, HEAD_DIM), f),
        'ab_subln': gain(ks[10], (ne, A_V_DIM)),
        'ab_sink': 0.5 * jax.random.normal(ks[11], (ne, B_HEADS), f),
        'ab_w_o': w(ks[12], (ne, AB_OUT, D_MODEL), AB_OUT),
        'cd_w_in': w(ks[13], (no, D_MODEL, CD_IN), D_MODEL),
        'cd_q_norm': gain(ks[14], (no, D_Q_LORA)),
        'cd_w_q_b': w(ks[15], (no, D_Q_LORA, D_HEADS * (D_NOPE + D_ROPE)), D_Q_LORA),
        'cd_kv_norm': gain(ks[16], (no, D_KV_LORA)),
        'cd_w_kv_b': w(ks[17], (no, D_KV_LORA, D_HEADS * (D_NOPE + D_V)), D_KV_LORA),
        'cd_w_o': w(ks[18], (no, CD_OUT, D_MODEL), CD_OUT),
        'ffn_w_gate': w(ks[19], (DEPTH, D_MODEL, D_FF), D_MODEL),
        'ffn_w_up': w(ks[20], (DEPTH, D_MODEL, D_FF), D_MODEL),
        'ffn_w_down': w(ks[21], (DEPTH, D_FF, D_MODEL), D_FF),
    }


def reference(x, bias_table, attn_norm, ffn_norm, final_norm, ab_w_in, ab_lambda_q1, ab_lambda_k1,
              ab_lambda_q2, ab_lambda_k2, ab_subln, ab_sink, ab_w_o, cd_w_in, cd_q_norm, cd_w_q_b,
              cd_kv_norm, cd_w_kv_b, cd_w_o, ffn_w_gate, ffn_w_up, ffn_w_down):
    pos = jnp.arange(x.shape[1])
    h = x
    for layer in range(DEPTH):
        j = layer // 2
        hn = rmsnorm(h, attn_norm[layer])
        if layer % 2 == 0:
            lambda_init = 0.8 - 0.6 * math.exp(-0.3 * layer)
            mix = mixer_ab(hn, ab_w_in[j], ab_lambda_q1[j], ab_lambda_k1[j], ab_lambda_q2[j],
                           ab_lambda_k2[j], ab_subln[j], ab_sink[j], ab_w_o[j], bias_table, lambda_init)
        else:
            mix = mixer_cd(hn, cd_w_in[j], cd_q_norm[j], cd_w_q_b[j], cd_kv_norm[j], cd_w_kv_b[j],
                           cd_w_o[j], bias_table, pos)
        h = h + mix
        h = h + swiglu(rmsnorm(h, ffn_norm[layer]), ffn_w_gate[layer], ffn_w_up[layer], ffn_w_down[layer])
    return rmsnorm(h, final_norm)
```

```python
import functools
import math

import numpy as np
import jax
import jax.numpy as jnp
from jax import lax
from jax.experimental import pallas as pl
from jax.experimental.pallas import tpu as pltpu

F32 = jnp.float32
BF16 = jnp.bfloat16

D_MODEL = 1024
HEAD_DIM = 64
EPS = 1e-6
NEG = -1e30
LOG2E = math.log2(math.e)
LN2 = math.log(2.0)

A_HEADS = 4
A_QK_DIM = 2 * HEAD_DIM
A_V_DIM = 2 * HEAD_DIM
B_HEADS = 8
B_KV_HEADS = 2
B_HALF_WINDOW = 128
C_PATTERNS = ((128, 1), (512, 4), (2048, 16))
C_HEADS_PER_GROUP = 4
C_HEADS = C_HEADS_PER_GROUP * len(C_PATTERNS)
D_HEADS = 12
D_Q_LORA = 384
D_KV_LORA = 256
D_NOPE = 64
D_ROPE = 32
D_V = 64
ROPE_THETA = 10000.0
NUM_BUCKETS = 32
MAX_DISTANCE = 1024
D_FF = 2816
AB_IN = 2304
CD_C_IN = 3 * C_HEADS * HEAD_DIM

LANE = 128
VMEM_LIMIT = 48 * 1024 * 1024

ROW_TILE = 512
FFN_ROW_TILE = 1024
FFN_COL_TILE = 256
DENSE_TILE = 512
BIAS_REACH = 3
V_ROWS_D = 80


def _bucket_thresholds():
    nb = NUM_BUCKETS // 2
    max_exact = nb // 2
    n = np.arange(1, 4 * MAX_DISTANCE)
    large = max_exact + (np.log(n.astype(np.float32) / np.float32(max_exact))
                         / np.float32(math.log(MAX_DISTANCE / max_exact))
                         * np.float32(nb - max_exact)).astype(np.int32)
    mag = np.where(n < max_exact, n, np.minimum(large, nb - 1))
    return tuple(int(n[np.argmax(mag >= k)]) for k in range(1, nb))


BUCKET_THRESHOLDS = _bucket_thresholds()
assert BUCKET_THRESHOLDS[-1] <= (BIAS_REACH - 1) * DENSE_TILE + 1


def _params(*sem):
    return pltpu.CompilerParams(dimension_semantics=sem, vmem_limit_bytes=VMEM_LIMIT)


def _rms(x, g):
    return x * lax.rsqrt(jnp.mean(x * x, axis=-1, keepdims=True) + EPS) * g


def _norm_proj_kernel(x_ref, g_ref, w_ref, cs_ref, o_ref):
    xn = _rms(x_ref[...], g_ref[...]).astype(BF16)
    y = jnp.dot(xn, w_ref[...], preferred_element_type=F32)
    o_ref[...] = (y * cs_ref[...]).astype(o_ref.dtype)


def norm_proj(x, g, w, colscale, name):
    M, K = x.shape
    N = w.shape[1]
    tm = ROW_TILE
    return pl.pallas_call(
        _norm_proj_kernel,
        grid=(M // tm,),
        in_specs=[pl.BlockSpec((tm, K), lambda i: (i, 0)),
                  pl.BlockSpec((1, K), lambda i: (0, 0)),
                  pl.BlockSpec((K, N), lambda i: (0, 0)),
                  pl.BlockSpec((1, N), lambda i: (0, 0))],
        out_specs=pl.BlockSpec((tm, N), lambda i: (i, 0)),
        out_shape=jax.ShapeDtypeStruct((M, N), BF16),
        compiler_params=_params("parallel"),
        name=name,
    )(x, g.reshape(1, K), w, colscale.reshape(1, N))


def _bias_kernel(tab_ref, o_ref, *, off0, off_step, row_coef, col_coef, dil, half_window, head0):
    v = pl.program_id(0)
    hcol = head0 + pl.program_id(1)
    R, C = o_ref.shape[-2:]
    row = lax.broadcasted_iota(jnp.int32, (R, C), 0)
    col = lax.broadcasted_iota(jnp.int32, (R, C), 1)
    rel = off0 + v * off_step + row_coef * row + col_coef * col
    dist = rel * dil
    n = jnp.abs(dist)
    nb = NUM_BUCKETS // 2
    vneg = jnp.full((R, C), tab_ref[0, hcol], F32)
    vpos = jnp.full((R, C), tab_ref[nb, hcol], F32)
    for k, thr in enumerate(BUCKET_THRESHOLDS, start=1):
        ge = n >= thr
        vneg = jnp.where(ge, tab_ref[k, hcol], vneg)
        vpos = jnp.where(ge, tab_ref[nb + k, hcol], vpos)
    val = jnp.where(dist > 0, vpos, vneg) * LOG2E
    if half_window is not None:
        val = jnp.where(jnp.abs(rel) <= half_window, val, NEG)
    o_ref[0, 0] = val


def bias_tiles(table, *, nvar, nheads, head0, rows, cols, off0, off_step, row_coef, col_coef,
               dil, half_window, name):
    kern = functools.partial(_bias_kernel, off0=off0, off_step=off_step, row_coef=row_coef,
                             col_coef=col_coef, dil=dil, half_window=half_window, head0=head0)
    return pl.pallas_call(
        kern,
        grid=(nvar, nheads),
        in_specs=[pl.BlockSpec(memory_space=pltpu.SMEM)],
        out_specs=pl.BlockSpec((1, 1, rows, cols), lambda v, h: (v, h, 0, 0)),
        out_shape=jax.ShapeDtypeStruct((nvar, nheads, rows, cols), F32),
        compiler_params=_params("parallel", "parallel"),
        name=name,
    )(table)


def _attn_a_kernel(q1_ref, q2_ref, k_ref, v_ref, bias_ref, lq1_ref, lk1_ref, lq2_ref, lk2_ref,
                   subln_ref, o_ref, m1, m2, l1, l2, acc1, acc2, *, tile, nk, lambda_init):
    qi = pl.program_id(2)
    for m_ref, l_ref, acc_ref in ((m1, l1, acc1), (m2, l2, acc2)):
        m_ref[...] = jnp.full(m_ref.shape, NEG, F32)
        l_ref[...] = jnp.zeros(l_ref.shape, F32)
        acc_ref[...] = jnp.zeros(acc_ref.shape, F32)
    q1 = q1_ref[0, 0, 0]
    q2 = q2_ref[0, 0, 0]

    def body(kc, carry):
        kblk = k_ref[0, pl.ds(pl.multiple_of(kc * tile, tile), tile), :]
        vblk = v_ref[0, 0, kc]
        bt = bias_ref[jnp.clip(kc - qi, -BIAS_REACH, BIAS_REACH) + BIAS_REACH, 0]
        for q, m_ref, l_ref, acc_ref in ((q1, m1, l1, acc1), (q2, m2, l2, acc2)):
            s = jnp.dot(kblk, q, preferred_element_type=F32) + bt
            m_old = m_ref[...]
            m_new = jnp.maximum(m_old, jnp.max(s, axis=0, keepdims=True))
            alpha = jnp.exp2(m_old - m_new)
            p = jnp.exp2(s - m_new)
            l_ref[...] = alpha * l_ref[...] + jnp.sum(p, axis=0, keepdims=True)
            acc_ref[...] = alpha * acc_ref[...] + jnp.dot(vblk, p.astype(BF16),
                                                          preferred_element_type=F32)
            m_ref[...] = m_new
        return carry

    lax.fori_loop(0, nk, body, 0)

    lam = (jnp.exp(jnp.sum(lq1_ref[...] * lk1_ref[...], axis=-1, keepdims=True))
           - jnp.exp(jnp.sum(lq2_ref[...] * lk2_ref[...], axis=-1, keepdims=True)) + lambda_init)
    o = acc1[...] / l1[...] - lam * (acc2[...] / l2[...])
    ms = jnp.mean(o * o, axis=0, keepdims=True)
    y = o * lax.rsqrt(ms + EPS) * subln_ref[...] * (1.0 - lambda_init)
    o_ref[0] = y.T.astype(o_ref.dtype)


def attn_a(q1t, q2t, proj, vt, bias, lq1, lk1, lq2, lk2, subln, lambda_init):
    B, S, _ = proj.shape
    T = DENSE_TILE
    nq = nk = S // T
    k_block0 = (A_HEADS * A_QK_DIM) // LANE
    kern = functools.partial(_attn_a_kernel, tile=T, nk=nk, lambda_init=lambda_init)
    vec = lambda n: pl.BlockSpec((1, n), lambda b, h, i: (0, 0))
    return pl.pallas_call(
        kern,
        grid=(B, A_HEADS, nq),
        in_specs=[pl.BlockSpec((1, 1, 1, LANE, T), lambda b, h, i: (b, h, i, 0, 0)),
                  pl.BlockSpec((1, 1, 1, LANE, T), lambda b, h, i: (b, h, i, 0, 0)),
                  pl.BlockSpec((1, S, LANE), lambda b, h, i: (b, 0, k_block0 + h)),
                  pl.BlockSpec((1, 1, nk, LANE, T), lambda b, h, i: (b, h, 0, 0, 0)),
                  pl.BlockSpec((2 * BIAS_REACH + 1, 1, T, T), lambda b, h, i: (0, h, 0, 0)),
                  vec(HEAD_DIM), vec(HEAD_DIM), vec(HEAD_DIM), vec(HEAD_DIM),
                  pl.BlockSpec((A_V_DIM, 1), lambda b, h, i: (0, 0))],
        out_specs=pl.BlockSpec((1, T, LANE), lambda b, h, i: (b, i, h)),
        out_shape=jax.ShapeDtypeStruct((B, S, A_HEADS * A_V_DIM), BF16),
        scratch_shapes=[pltpu.VMEM((1, T), F32)] * 4 + [pltpu.VMEM((A_V_DIM, T), F32)] * 2,
        compiler_params=_params("parallel", "parallel", "arbitrary"),
        name="attn_a",
    )(q1t, q2t, proj, vt, bias, lq1.reshape(1, -1), lk1.reshape(1, -1), lq2.reshape(1, -1),
      lk2.reshape(1, -1), subln.reshape(-1, 1))


def _attn_d_kernel(q_ref, k_ref, v_ref, o_ref, m_s, acc_s, *, tile, nk):
    m_s[...] = jnp.full(m_s.shape, NEG, F32)
    acc_s[...] = jnp.zeros(acc_s.shape, F32)

    def body(kc, carry):
        start = pl.multiple_of(kc * tile, tile)
        for hh in range(2):
            kblk = k_ref[0, pl.ds(start, tile), hh * LANE:(hh + 1) * LANE]
            s = jnp.dot(kblk, q_ref[0, hh, 0], preferred_element_type=F32)
            m_old = m_s[hh]
            m_new = jnp.maximum(m_old, jnp.max(s, axis=0, keepdims=True))
            alpha = jnp.exp2(m_old - m_new)
            p = jnp.exp2(s - m_new).astype(BF16)
            acc_s[hh] = alpha * acc_s[hh] + jnp.dot(v_ref[0, hh, kc], p, preferred_element_type=F32)
            m_s[hh] = m_new
        return carry

    lax.fori_loop(0, nk, body, 0)
    outs = []
    for hh in range(2):
        acc = acc_s[hh]
        outs.append(acc[:D_V] / acc[D_V:D_V + 1])
    o_ref[0] = jnp.concatenate(outs, axis=0).T.astype(o_ref.dtype)


def attn_d(qt, k, vt):
    B, S, _ = k.shape
    T = DENSE_TILE
    nq = nk = S // T
    kern = functools.partial(_attn_d_kernel, tile=T, nk=nk)
    return pl.pallas_call(
        kern,
        grid=(B, D_HEADS // 2, nq),
        in_specs=[pl.BlockSpec((1, 2, 1, LANE, T), lambda b, h, i: (b, h, i, 0, 0)),
                  pl.BlockSpec((1, S, 2 * LANE), lambda b, h, i: (b, 0, h)),
                  pl.BlockSpec((1, 2, nk, V_ROWS_D, T), lambda b, h, i: (b, h, 0, 0, 0))],
        out_specs=pl.BlockSpec((1, T, LANE), lambda b, h, i: (b, i, h)),
        out_shape=jax.ShapeDtypeStruct((B, S, D_HEADS * D_V), BF16),
        scratch_shapes=[pltpu.VMEM((2, 1, T), F32), pltpu.VMEM((2, V_ROWS_D, T), F32)],
        compiler_params=_params("parallel", "parallel", "arbitrary"),
        name="attn_d",
    )(qt, k, vt)


def _banded_kernel(*refs, tq, span, half_window, seq_len, nqb, nkb, head_of, has_sink, has_lse):
    it = iter(refs)
    q_ref, k_ref, v_ref, bias_ref = next(it), next(it), next(it), next(it)
    sink_ref = next(it) if has_sink else None
    o_ref = next(it)
    lse_ref = next(it) if has_lse else None

    t = pl.program_id(2)
    start = jnp.clip(t * tq - half_window, 0, seq_len - span)
    start = pl.multiple_of(start, half_window)
    kwin = k_ref[0, pl.ds(start, span), :]
    vwin = v_ref[0, pl.ds(start, span), :]
    lane = lax.broadcasted_iota(jnp.int32, (tq, LANE), 1)
    low = lane < HEAD_DIM
    for c in range(nqb):
        q2 = q_ref[0, :, c * LANE:(c + 1) * LANE]
        kc = c if nkb == nqb else 0
        kb = kwin[:, kc * LANE:(kc + 1) * LANE]
        vb = vwin[:, kc * LANE:(kc + 1) * LANE]
        o_half, lse_half = [], []
        for half in range(2):
            hidx = head_of(c, half)
            qm = jnp.where(low if half == 0 else jnp.logical_not(low), q2, jnp.zeros_like(q2))
            s = lax.dot_general(qm, kb, (((1,), (1,)), ((), ())), preferred_element_type=F32)
            bt = bias_ref[0, hidx]
            s = jnp.where(bt > 0.5 * NEG, s + bt, NEG)
            m = jnp.max(s, axis=-1, keepdims=True)
            if has_sink:
                sk = sink_ref[:, hidx:hidx + 1] * LOG2E
                m = jnp.maximum(m, sk)
            e = jnp.exp2(s - m)
            denom = jnp.sum(e, axis=-1, keepdims=True)
            if has_sink:
                denom = denom + jnp.exp2(sk - m)
            o_half.append(jnp.dot(e.astype(BF16), vb, preferred_element_type=F32) / denom)
            if has_lse:
                lse_half.append(LN2 * m + jnp.log(denom))
        o_ref[0, :, c * LANE:(c + 1) * LANE] = jnp.where(low, o_half[0], o_half[1]).astype(o_ref.dtype)
        if has_lse:
            lse_ref[0, :, c * LANE:(c + 1) * LANE] = jnp.where(low, lse_half[0], lse_half[1])


def banded(proj, bias, sink, *, dil, tq, half_window, nqb, nkb, q_idx, k_idx, v_idx, head_of,
           out_dtype, has_lse, name):
    B, S, N = proj.shape
    L = S // dil
    span = tq + 2 * half_window
    nq = L // tq
    assert L % tq == 0 and L >= span and nq >= 2
    view = proj.reshape(B, L, dil * N)
    OW = nqb * LANE
    has_sink = sink is not None
    kern = functools.partial(_banded_kernel, tq=tq, span=span, half_window=half_window, seq_len=L,
                             nqb=nqb, nkb=nkb, head_of=head_of, has_sink=has_sink, has_lse=has_lse)
    nh = bias.shape[1]

    def variant(t):
        return jnp.where(t == 0, 0, jnp.where(t == nq - 1, 2, 1))

    in_specs = [pl.BlockSpec((1, tq, OW), lambda b, r, t: (b, t, q_idx(r))),
                pl.BlockSpec((1, L, nkb * LANE), lambda b, r, t: (b, 0, k_idx(r))),
                pl.BlockSpec((1, L, nkb * LANE), lambda b, r, t: (b, 0, v_idx(r))),
                pl.BlockSpec((1, nh, tq, span), lambda b, r, t: (variant(t), 0, 0, 0))]
    args = [view, view, view, bias]
    if has_sink:
        in_specs.append(pl.BlockSpec((1, sink.shape[-1]), lambda b, r, t: (0, 0)))
        args.append(sink.reshape(1, -1))
    out_spec = pl.BlockSpec((1, tq, OW), lambda b, r, t: (b, t, r))
    out_shapes = [jax.ShapeDtypeStruct((B, L, dil * OW), out_dtype)]
    out_specs = [out_spec]
    if has_lse:
        out_shapes.append(jax.ShapeDtypeStruct((B, L, dil * OW), F32))
        out_specs.append(out_spec)
    outs = pl.pallas_call(
        kern,
        grid=(B, dil, nq),
        in_specs=in_specs,
        out_specs=out_specs,
        out_shape=out_shapes,
        compiler_params=_params("parallel", "parallel", "arbitrary"),
        name=name,
    )(*args)
    return [o.reshape(B, S, OW) for o in outs]


def _out_proj_ab_kernel(h_ref, a_ref, b_ref, wa_ref, wb_ref, o_ref):
    mix = jnp.dot(a_ref[...], wa_ref[...], preferred_element_type=F32)
    mix = mix + jnp.dot(b_ref[...], wb_ref[...], preferred_element_type=F32)
    o_ref[...] = h_ref[...] + mix


def _out_proj_cd_kernel(h_ref, o0, o1, o2, s0, s1, s2, d_ref, wc_ref, wd_ref, o_ref):
    l0, l1, l2 = s0[...], s1[...], s2[...]
    mx = jnp.maximum(jnp.maximum(l0, l1), l2)
    e0, e1, e2 = jnp.exp(l0 - mx), jnp.exp(l1 - mx), jnp.exp(l2 - mx)
    oc = (e0 * o0[...] + e1 * o1[...] + e2 * o2[...]) / (e0 + e1 + e2)
    mix = jnp.dot(oc.astype(BF16), wc_ref[...], preferred_element_type=F32)
    mix = mix + jnp.dot(d_ref[...], wd_ref[...], preferred_element_type=F32)
    o_ref[...] = h_ref[...] + mix


def _row_call(kern, row_args, full_args, name):
    M = row_args[0].shape[0]
    tm = ROW_TILE
    in_specs = [pl.BlockSpec((tm, a.shape[1]), lambda i: (i, 0)) for a in row_args]
    in_specs += [pl.BlockSpec(a.shape, lambda i: (0, 0)) for a in full_args]
    return pl.pallas_call(
        kern,
        grid=(M // tm,),
        in_specs=in_specs,
        out_specs=pl.BlockSpec((tm, D_MODEL), lambda i: (i, 0)),
        out_shape=jax.ShapeDtypeStruct((M, D_MODEL), F32),
        compiler_params=_params("parallel"),
        name=name,
    )(*row_args, *full_args)


def _ffn_kernel(x_ref, g_ref, wg_ref, wu_ref, wd_ref, fg_ref, o_ref, xn_s, acc_s, *, final_norm):
    f = pl.program_id(1)

    @pl.when(f == 0)
    def _():
        xn_s[...] = _rms(x_ref[...], g_ref[...]).astype(BF16)
        acc_s[...] = jnp.zeros(acc_s.shape, F32)

    xn = xn_s[...]
    gate = jnp.dot(xn, wg_ref[...], preferred_element_type=F32)
    up = jnp.dot(xn, wu_ref[...], preferred_element_type=F32)
    mid = (gate / (1.0 + jnp.exp(-gate)) * up).astype(BF16)
    acc_s[...] += jnp.dot(mid, wd_ref[...], preferred_element_type=F32)

    @pl.when(f == pl.num_programs(1) - 1)
    def _():
        y = x_ref[...] + acc_s[...]
        if final_norm:
            y = _rms(y, fg_ref[...])
        o_ref[...] = y


def ffn(x, g, wg, wu, wd, fg, final_norm, name):
    M, K = x.shape
    tm, tf = FFN_ROW_TILE, FFN_COL_TILE
    kern = functools.partial(_ffn_kernel, final_norm=final_norm)
    return pl.pallas_call(
        kern,
        grid=(M // tm, D_FF // tf),
        in_specs=[pl.BlockSpec((tm, K), lambda i, f: (i, 0)),
                  pl.BlockSpec((1, K), lambda i, f: (0, 0)),
                  pl.BlockSpec((K, tf), lambda i, f: (0, f)),
                  pl.BlockSpec((K, tf), lambda i, f: (0, f)),
                  pl.BlockSpec((tf, K), lambda i, f: (f, 0)),
                  pl.BlockSpec((1, K), lambda i, f: (0, 0))],
        out_specs=pl.BlockSpec((tm, K), lambda i, f: (i, 0)),
        out_shape=jax.ShapeDtypeStruct((M, K), F32),
        scratch_shapes=[pltpu.VMEM((tm, K), BF16), pltpu.VMEM((tm, K), F32)],
        compiler_params=_params("parallel", "arbitrary"),
        name=name,
    )(x, g.reshape(1, K), wg, wu, wd, fg.reshape(1, K))


def _trig_kernel(ang_ref, cos_ref, sin_ref):
    a = ang_ref[...]
    cos_ref[...] = jnp.cos(a)
    sin_ref[...] = jnp.sin(a)


def rope_tables(seq_len):
    half = D_ROPE // 2
    inv = ROPE_THETA ** (-jnp.arange(half, dtype=F32) / half)
    ang = jnp.arange(seq_len).astype(F32)[:, None] * inv[None, :]
    dense = ang.reshape(seq_len * half // LANE, LANE)
    spec = pl.BlockSpec(dense.shape, lambda: (0, 0))
    cos, sin = pl.pallas_call(
        _trig_kernel,
        in_specs=[spec],
        out_specs=[spec, spec],
        out_shape=[jax.ShapeDtypeStruct(dense.shape, F32)] * 2,
        name="rope_trig",
    )(dense)
    cos, sin = cos.reshape(seq_len, half), sin.reshape(seq_len, half)
    pad = LANE - D_NOPE - D_ROPE
    cos_l = jnp.concatenate([jnp.ones((seq_len, D_NOPE), F32), cos, cos, jnp.ones((seq_len, pad), F32)], axis=1)
    sin_l = jnp.concatenate([jnp.zeros((seq_len, D_NOPE), F32), sin, sin, jnp.zeros((seq_len, pad), F32)], axis=1)
    return cos_l, sin_l


def _prep_d_kernel(x_ref, g_ref, wa_ref, qn_ref, kvn_ref, wq_ref, wkv_ref, cos_ref, sin_ref,
                   q_ref, k_ref, v_ref, *, qscale):
    xn = _rms(x_ref[...], g_ref[...]).astype(BF16)
    lat = jnp.dot(xn, wa_ref[...], preferred_element_type=F32)
    cq = _rms(lat[:, :D_Q_LORA], qn_ref[...]).astype(BF16)
    ckv = _rms(lat[:, D_Q_LORA:D_Q_LORA + D_KV_LORA], kvn_ref[...]).astype(BF16)
    o_pe = D_Q_LORA + D_KV_LORA
    cos, sin = cos_ref[...], sin_ref[...]
    kpe = lat[:, o_pe:o_pe + LANE] * cos + lat[:, o_pe + LANE:o_pe + 2 * LANE] * sin
    qq = jnp.dot(cq, wq_ref[...], preferred_element_type=F32)
    kv = jnp.dot(ckv, wkv_ref[...], preferred_element_type=F32)
    kw = D_HEADS * LANE
    for h in range(D_HEADS):
        qh = qq[:, h * LANE:(h + 1) * LANE] * cos + qq[:, kw + h * LANE:kw + (h + 1) * LANE] * sin
        q_ref[:, h * LANE:(h + 1) * LANE] = (qh * qscale).astype(BF16)
        k_ref[:, h * LANE:(h + 1) * LANE] = (kv[:, h * LANE:(h + 1) * LANE] + kpe).astype(BF16)
    v_ref[...] = kv[:, kw:].astype(BF16)


def prep_d(x, g, wa, qn, kvn, wq, wkv, cos_l, sin_l, seq_len):
    M, K = x.shape
    tm = ROW_TILE
    nrow = seq_len // tm
    kw = D_HEADS * LANE
    kern = functools.partial(_prep_d_kernel, qscale=(D_NOPE + D_ROPE) ** -0.5 * LOG2E)
    full = lambda a: pl.BlockSpec(a.shape, lambda i: (0, 0))
    qn2, kvn2, g2 = qn.reshape(1, -1), kvn.reshape(1, -1), g.reshape(1, K)
    return pl.pallas_call(
        kern,
        grid=(M // tm,),
        in_specs=[pl.BlockSpec((tm, K), lambda i: (i, 0)), full(g2), full(wa), full(qn2), full(kvn2),
                  full(wq), full(wkv),
                  pl.BlockSpec((tm, LANE), lambda i: (i % nrow, 0)),
                  pl.BlockSpec((tm, LANE), lambda i: (i % nrow, 0))],
        out_specs=[pl.BlockSpec((tm, kw), lambda i: (i, 0)),
                   pl.BlockSpec((tm, kw), lambda i: (i, 0)),
                   pl.BlockSpec((tm, D_HEADS * D_V), lambda i: (i, 0))],
        out_shape=[jax.ShapeDtypeStruct((M, kw), BF16), jax.ShapeDtypeStruct((M, kw), BF16),
                   jax.ShapeDtypeStruct((M, D_HEADS * D_V), BF16)],
        compiler_params=_params("parallel"),
        name="prep_d",
    )(x, g2, wa, qn2, kvn2, wq, wkv, cos_l, sin_l)


B_HEAD_ORDER = (0, 4, 1, 5, 2, 6, 3, 7)


def _head_cols(order, base):
    return np.concatenate([np.arange(base + h * HEAD_DIM, base + (h + 1) * HEAD_DIM) for h in order])


def _to_tiles_t(t, tile):
    B, S, H, d = t.shape
    return t.reshape(B, S // tile, tile, H, d).transpose(0, 3, 1, 4, 2)


def _rot_partner_cols(w):
    half = D_ROPE // 2
    return jnp.concatenate([-w[..., half:], w[..., :half]], axis=-1)


def kernel(x, bias_table, attn_norm, ffn_norm, final_norm, ab_w_in, ab_lambda_q1, ab_lambda_k1,
           ab_lambda_q2, ab_lambda_k2, ab_subln, ab_sink, ab_w_o, cd_w_in, cd_q_norm, cd_w_q_b,
           cd_kv_norm, cd_w_kv_b, cd_w_o, ffn_w_gate, ffn_w_up, ffn_w_down):
    B, S, _ = x.shape
    M = B * S
    T = DENSE_TILE
    h = x.reshape(M, D_MODEL)
    qk_scale = HEAD_DIM ** -0.5 * LOG2E

    o3 = A_HEADS * (2 * A_QK_DIM + A_V_DIM)
    cols0 = np.concatenate([np.arange(o3), _head_cols(B_HEAD_ORDER, o3),
                            np.arange(o3 + B_HEADS * HEAD_DIM, AB_IN)])
    w0 = ab_w_in[0][:, cols0].astype(BF16)
    cs0 = np.ones((AB_IN,), np.float32)
    cs0[:A_HEADS * A_QK_DIM] = qk_scale
    cs0[o3:o3 + B_HEADS * HEAD_DIM] = qk_scale
    proj0 = norm_proj(h, attn_norm[0], w0, jnp.asarray(cs0), "proj0").reshape(B, S, AB_IN)

    qa = proj0[..., :A_HEADS * A_QK_DIM].reshape(B, S, A_HEADS, 2, HEAD_DIM)
    zeros = jnp.zeros_like(qa[..., 0, :])
    q1t = _to_tiles_t(jnp.concatenate([qa[..., 0, :], zeros], axis=-1), T)
    q2t = _to_tiles_t(jnp.concatenate([zeros, qa[..., 1, :]], axis=-1), T)
    va = proj0[..., 2 * A_HEADS * A_QK_DIM:o3].reshape(B, S, A_HEADS, A_V_DIM)
    vat = _to_tiles_t(va, T)
    bias_a = bias_tiles(bias_table, nvar=2 * BIAS_REACH + 1, nheads=A_HEADS, head0=0, rows=T, cols=T,
                        off0=-BIAS_REACH * T, off_step=T, row_coef=1, col_coef=-1, dil=1,
                        half_window=None, name="bias_a")
    oa = attn_a(q1t, q2t, proj0, vat, bias_a, ab_lambda_q1[0], ab_lambda_k1[0], ab_lambda_q2[0],
                ab_lambda_k2[0], ab_subln[0], 0.8 - 0.6 * math.exp(-0.3 * 0))

    tq_b = 256
    bias_b = bias_tiles(bias_table, nvar=3, nheads=B_HEADS, head0=A_HEADS, rows=tq_b,
                        cols=tq_b + 2 * B_HALF_WINDOW, off0=0, off_step=-B_HALF_WINDOW, row_coef=-1,
                        col_coef=1, dil=1, half_window=B_HALF_WINDOW, name="bias_b")
    qb0 = o3 // LANE
    (ob,) = banded(proj0, bias_b, ab_sink[0], dil=1, tq=tq_b, half_window=B_HALF_WINDOW, nqb=4, nkb=1,
                   q_idx=lambda r: qb0 // 4, k_idx=lambda r: qb0 + 4, v_idx=lambda r: qb0 + 5,
                   head_of=lambda c, half: c + 4 * half, out_dtype=BF16, has_lse=False, name="attn_b")

    wo = ab_w_o[0]
    wo_a = wo[:A_HEADS * A_V_DIM].astype(BF16)
    wo_b = wo[_head_cols(B_HEAD_ORDER, A_HEADS * A_V_DIM)].astype(BF16)
    h = _row_call(_out_proj_ab_kernel, [h, oa.reshape(M, -1), ob.reshape(M, -1)], [wo_a, wo_b], "out_proj0")
    h = ffn(h, ffn_norm[0], ffn_w_gate[0].astype(BF16), ffn_w_up[0].astype(BF16),
            ffn_w_down[0].astype(BF16), final_norm, False, "ffn0")

    w1 = cd_w_in[0]
    cs1 = np.ones((CD_C_IN,), np.float32)
    cs1[:C_HEADS * HEAD_DIM] = qk_scale
    proj1 = norm_proj(h, attn_norm[1], w1[:, :CD_C_IN].astype(BF16), jnp.asarray(cs1),
                      "proj1").reshape(B, S, CD_C_IN)

    nblk = CD_C_IN // LANE
    oc, lses = [], []
    for g, (window, dil) in enumerate(C_PATTERNS):
        hw = window // (2 * dil)
        tq_c = 128
        bias_c = bias_tiles(bias_table, nvar=3, nheads=C_HEADS_PER_GROUP, head0=g * C_HEADS_PER_GROUP,
                            rows=tq_c, cols=tq_c + 2 * hw, off0=0, off_step=-hw, row_coef=-1, col_coef=1,
                            dil=dil, half_window=hw, name=f"bias_c{g}")
        half_blocks = nblk // 2
        o_g, lse_g = banded(proj1, bias_c, None, dil=dil, tq=tq_c, half_window=hw, nqb=2, nkb=2,
                            q_idx=lambda r, g=g: r * half_blocks + g,
                            k_idx=lambda r, g=g: r * half_blocks + 3 + g,
                            v_idx=lambda r, g=g: r * half_blocks + 6 + g,
                            head_of=lambda c, half: 2 * c + half, out_dtype=F32, has_lse=True,
                            name=f"attn_c{g}")
        oc.append(o_g.reshape(M, -1))
        lses.append(lse_g.reshape(M, -1))

    o_q, o_kv = CD_C_IN + D_Q_LORA, CD_C_IN + D_Q_LORA + D_KV_LORA
    w_pe = w1[:, o_kv:]
    lane_pad = lambda w: jnp.pad(w, ((0, 0), (D_NOPE, LANE - D_NOPE - D_ROPE)))
    wa = jnp.concatenate([w1[:, CD_C_IN:o_kv], lane_pad(w_pe), lane_pad(_rot_partner_cols(w_pe))],
                         axis=1).astype(BF16)
    wq3 = cd_w_q_b[0].reshape(D_Q_LORA, D_HEADS, D_NOPE + D_ROPE)
    zpad = jnp.zeros((D_Q_LORA, D_HEADS, LANE - D_NOPE - D_ROPE), F32)
    wq_main = jnp.concatenate([wq3, zpad], axis=-1)
    wq_rot = jnp.concatenate([jnp.zeros_like(wq3[..., :D_NOPE]), _rot_partner_cols(wq3[..., D_NOPE:]), zpad],
                             axis=-1)
    wq = jnp.concatenate([wq_main.reshape(D_Q_LORA, -1), wq_rot.reshape(D_Q_LORA, -1)], axis=1).astype(BF16)
    wkv3 = cd_w_kv_b[0].reshape(D_KV_LORA, D_HEADS, D_NOPE + D_V)
    wk = jnp.pad(wkv3[..., :D_NOPE], ((0, 0), (0, 0), (0, LANE - D_NOPE))).reshape(D_KV_LORA, -1)
    wv = wkv3[..., D_NOPE:].reshape(D_KV_LORA, -1)
    wkv = jnp.concatenate([wk, wv], axis=1).astype(BF16)
    cos_l, sin_l = rope_tables(S)
    qd, kd, vd = prep_d(h, attn_norm[1], wa, cd_q_norm[0], cd_kv_norm[0], wq, wkv, cos_l, sin_l, S)

    qdt = _to_tiles_t(qd.reshape(B, S, D_HEADS, LANE), T)
    vd4 = vd.reshape(B, S, D_HEADS, D_V)
    vd_ext = jnp.concatenate([vd4, jnp.ones((B, S, D_HEADS, 1), BF16),
                              jnp.zeros((B, S, D_HEADS, V_ROWS_D - D_V - 1), BF16)], axis=-1)
    vdt = _to_tiles_t(vd_ext, T)
    od = attn_d(qdt, kd.reshape(B, S, D_HEADS * LANE), vdt)

    wo1 = cd_w_o[0]
    wo_c = wo1[:C_HEADS_PER_GROUP * HEAD_DIM].astype(BF16)
    wo_d = wo1[C_HEADS_PER_GROUP * HEAD_DIM:].astype(BF16)
    h = _row_call(_out_proj_cd_kernel, [h, *oc, *lses, od.reshape(M, -1)], [wo_c, wo_d], "out_proj1")
    h = ffn(h, ffn_norm[1], ffn_w_gate[1].astype(BF16), ffn_w_up[1].astype(BF16),
            ffn_w_down[1].astype(BF16), final_norm, True, "ffn1")
    return h.reshape(B, S, D_MODEL)
```

```python
import functools
import math

import numpy as np
import jax
import jax.numpy as jnp
from jax import lax
from jax.experimental import pallas as pl
from jax.experimental.pallas import tpu as pltpu

F32 = jnp.float32
BF16 = jnp.bfloat16

D_MODEL = 1024
HEAD_DIM = 64
EPS = 1e-6
NEG = -1e30
LOG2E = math.log2(math.e)
LN2 = math.log(2.0)

A_HEADS = 4
A_QK_DIM = 2 * HEAD_DIM
A_V_DIM = 2 * HEAD_DIM
B_HEADS = 8
B_KV_HEADS = 2
B_HALF_WINDOW = 128
C_PATTERNS = ((128, 1), (512, 4), (2048, 16))
C_HEADS_PER_GROUP = 4
C_HEADS = C_HEADS_PER_GROUP * len(C_PATTERNS)
D_HEADS = 12
D_Q_LORA = 384
D_KV_LORA = 256
D_NOPE = 64
D_ROPE = 32
D_V = 64
ROPE_THETA = 10000.0
NUM_BUCKETS = 32
MAX_DISTANCE = 1024
D_FF = 2816
AB_IN = 2304
CD_C_IN = 3 * C_HEADS * HEAD_DIM

LANE = 128
VMEM_LIMIT = 48 * 1024 * 1024

ROW_TILE = 512
FFN_ROW_TILE = 1024
FFN_COL_TILE = 256
DENSE_TILE = 512
BIAS_REACH = 3
V_ROWS_D = 80
V_ROWS_A = 144


def _bucket_thresholds():
    nb = NUM_BUCKETS // 2
    max_exact = nb // 2
    n = np.arange(1, 4 * MAX_DISTANCE)
    large = max_exact + (np.log(n.astype(np.float32) / np.float32(max_exact))
                         / np.float32(math.log(MAX_DISTANCE / max_exact))
                         * np.float32(nb - max_exact)).astype(np.int32)
    mag = np.where(n < max_exact, n, np.minimum(large, nb - 1))
    return tuple(int(n[np.argmax(mag >= k)]) for k in range(1, nb))


BUCKET_THRESHOLDS = _bucket_thresholds()
assert BUCKET_THRESHOLDS[-1] <= (BIAS_REACH - 1) * DENSE_TILE + 1


def _params(*sem):
    return pltpu.CompilerParams(dimension_semantics=sem, vmem_limit_bytes=VMEM_LIMIT)


def _rms(x, g):
    return x * lax.rsqrt(jnp.mean(x * x, axis=-1, keepdims=True) + EPS) * g


def _norm_proj_kernel(x_ref, g_ref, w_ref, cs_ref, o_ref):
    xn = _rms(x_ref[...], g_ref[...]).astype(BF16)
    y = jnp.dot(xn, w_ref[...], preferred_element_type=F32)
    o_ref[...] = (y * cs_ref[...]).astype(o_ref.dtype)


def norm_proj(x, g, w, colscale, name):
    M, K = x.shape
    N = w.shape[1]
    tm = ROW_TILE
    return pl.pallas_call(
        _norm_proj_kernel,
        grid=(M // tm,),
        in_specs=[pl.BlockSpec((tm, K), lambda i: (i, 0)),
                  pl.BlockSpec((1, K), lambda i: (0, 0)),
                  pl.BlockSpec((K, N), lambda i: (0, 0)),
                  pl.BlockSpec((1, N), lambda i: (0, 0))],
        out_specs=pl.BlockSpec((tm, N), lambda i: (i, 0)),
        out_shape=jax.ShapeDtypeStruct((M, N), BF16),
        compiler_params=_params("parallel"),
        name=name,
    )(x, g.reshape(1, K), w, colscale.reshape(1, N))


def _bias_kernel(tab_ref, o_ref, *, off0, off_step, row_coef, col_coef, dil, half_window, head0):
    v = pl.program_id(0)
    hcol = head0 + pl.program_id(1)
    R, C = o_ref.shape[-2:]
    row = lax.broadcasted_iota(jnp.int32, (R, C), 0)
    col = lax.broadcasted_iota(jnp.int32, (R, C), 1)
    rel = off0 + v * off_step + row_coef * row + col_coef * col
    dist = rel * dil
    n = jnp.abs(dist)
    nb = NUM_BUCKETS // 2
    vneg = jnp.full((R, C), tab_ref[0, hcol], F32)
    vpos = jnp.full((R, C), tab_ref[nb, hcol], F32)
    for k, thr in enumerate(BUCKET_THRESHOLDS, start=1):
        ge = n >= thr
        vneg = jnp.where(ge, tab_ref[k, hcol], vneg)
        vpos = jnp.where(ge, tab_ref[nb + k, hcol], vpos)
    val = jnp.where(dist > 0, vpos, vneg) * LOG2E
    if half_window is not None:
        val = jnp.where(jnp.abs(rel) <= half_window, val, NEG)
    o_ref[0, 0] = val


def bias_tiles(table, *, nvar, nheads, head0, rows, cols, off0, off_step, row_coef, col_coef,
               dil, half_window, name):
    kern = functools.partial(_bias_kernel, off0=off0, off_step=off_step, row_coef=row_coef,
                             col_coef=col_coef, dil=dil, half_window=half_window, head0=head0)
    return pl.pallas_call(
        kern,
        grid=(nvar, nheads),
        in_specs=[pl.BlockSpec(memory_space=pltpu.SMEM)],
        out_specs=pl.BlockSpec((1, 1, rows, cols), lambda v, h: (v, h, 0, 0)),
        out_shape=jax.ShapeDtypeStruct((nvar, nheads, rows, cols), F32),
        compiler_params=_params("parallel", "parallel"),
        name=name,
    )(table)


def _attn_a_kernel(q_ref, k_ref, v_ref, bias_ref, lq1_ref, lk1_ref, lq2_ref, lk2_ref, subln_ref, o_ref,
                   m_s, acc_s, s_a, s_b, cm_a, cm_b, *, tile, nk, lambda_init):
    qi = pl.program_id(2)
    m_s[...] = jnp.full(m_s.shape, NEG, F32)
    acc_s[...] = jnp.zeros(acc_s.shape, F32)

    def scores(kc, s_dst, cm_dst):
        kblk = k_ref[0, pl.ds(pl.multiple_of(kc * tile, tile), tile), :]
        bt = bias_ref[jnp.clip(kc - qi, -BIAS_REACH, BIAS_REACH) + BIAS_REACH, 0]
        for j in range(2):
            s = jnp.dot(kblk, q_ref[0, 0, j, 0], preferred_element_type=F32) + bt
            s_dst[j] = s
            cm_dst[j] = jnp.max(s, axis=0, keepdims=True)

    def consume(kc, s_src, cm_src):
        vblk = v_ref[0, 0, kc]
        for j in range(2):
            m_old = m_s[j]
            m_new = jnp.maximum(m_old, cm_src[j])
            alpha = jnp.exp2(m_old - m_new)
            p = jnp.exp2((s_src[j] - m_new).astype(BF16))
            acc_s[j] = alpha * acc_s[j] + jnp.dot(vblk, p, preferred_element_type=F32)
            m_s[j] = m_new

    def stage(kc, cur, nxt):
        scores(kc + 1, *nxt)
        consume(kc, *cur)

    buf_a, buf_b = (s_a, cm_a), (s_b, cm_b)
    scores(0, *buf_a)

    def body(j, carry):
        stage(2 * j, buf_a, buf_b)
        stage(2 * j + 1, buf_b, buf_a)
        return carry

    lax.fori_loop(0, nk // 2 - 1, body, 0)
    stage(nk - 2, buf_a, buf_b)
    consume(nk - 1, *buf_b)

    lam = (jnp.exp(jnp.sum(lq1_ref[...] * lk1_ref[...], axis=-1, keepdims=True))
           - jnp.exp(jnp.sum(lq2_ref[...] * lk2_ref[...], axis=-1, keepdims=True)) + lambda_init)
    a1, a2 = acc_s[0], acc_s[1]
    o = (a1[:A_V_DIM] / a1[A_V_DIM:A_V_DIM + 1]
         - lam * (a2[:A_V_DIM] / a2[A_V_DIM:A_V_DIM + 1]))
    ms = jnp.mean(o * o, axis=0, keepdims=True)
    y = o * lax.rsqrt(ms + EPS) * subln_ref[...] * (1.0 - lambda_init)
    o_ref[0] = y.T.astype(o_ref.dtype)


def attn_a(qt, proj, vt, bias, lq1, lk1, lq2, lk2, subln, lambda_init):
    B, S, _ = proj.shape
    T = DENSE_TILE
    nq = nk = S // T
    assert nk % 2 == 0 and nk >= 4
    k_block0 = (A_HEADS * A_QK_DIM) // LANE
    kern = functools.partial(_attn_a_kernel, tile=T, nk=nk, lambda_init=lambda_init)
    vec = lambda n: pl.BlockSpec((1, n), lambda b, h, i: (0, 0))
    return pl.pallas_call(
        kern,
        grid=(B, A_HEADS, nq),
        in_specs=[pl.BlockSpec((1, 1, 2, 1, LANE, T), lambda b, h, i: (b, h, 0, i, 0, 0)),
                  pl.BlockSpec((1, S, LANE), lambda b, h, i: (b, 0, k_block0 + h)),
                  pl.BlockSpec((1, 1, nk, V_ROWS_A, T), lambda b, h, i: (b, h, 0, 0, 0)),
                  pl.BlockSpec((2 * BIAS_REACH + 1, 1, T, T), lambda b, h, i: (0, h, 0, 0)),
                  vec(HEAD_DIM), vec(HEAD_DIM), vec(HEAD_DIM), vec(HEAD_DIM),
                  pl.BlockSpec((A_V_DIM, 1), lambda b, h, i: (0, 0))],
        out_specs=pl.BlockSpec((1, T, LANE), lambda b, h, i: (b, i, h)),
        out_shape=jax.ShapeDtypeStruct((B, S, A_HEADS * A_V_DIM), BF16),
        scratch_shapes=[pltpu.VMEM((2, 1, T), F32), pltpu.VMEM((2, V_ROWS_A, T), F32),
                        pltpu.VMEM((2, T, T), F32), pltpu.VMEM((2, T, T), F32),
                        pltpu.VMEM((2, 1, T), F32), pltpu.VMEM((2, 1, T), F32)],
        compiler_params=_params("parallel", "parallel", "arbitrary"),
        name="attn_a",
    )(qt, proj, vt, bias, lq1.reshape(1, -1), lk1.reshape(1, -1), lq2.reshape(1, -1),
      lk2.reshape(1, -1), subln.reshape(-1, 1))


def _attn_d_kernel(q_ref, k_ref, v_ref, o_ref, m_s, acc_s, s_a, s_b, cm_a, cm_b, *, tile, nk):
    m_s[...] = jnp.full(m_s.shape, NEG, F32)
    acc_s[...] = jnp.zeros(acc_s.shape, F32)

    def scores(kc, s_dst, cm_dst):
        start = pl.multiple_of(kc * tile, tile)
        for hh in range(2):
            kblk = k_ref[0, pl.ds(start, tile), hh * LANE:(hh + 1) * LANE]
            s = jnp.dot(kblk, q_ref[0, hh, 0], preferred_element_type=F32)
            s_dst[hh] = s
            cm_dst[hh] = jnp.max(s, axis=0, keepdims=True)

    def consume(kc, s_src, cm_src):
        for hh in range(2):
            m_old = m_s[hh]
            m_new = jnp.maximum(m_old, cm_src[hh])
            alpha = jnp.exp2(m_old - m_new)
            p = jnp.exp2((s_src[hh] - m_new).astype(BF16))
            acc_s[hh] = alpha * acc_s[hh] + jnp.dot(v_ref[0, hh, kc], p, preferred_element_type=F32)
            m_s[hh] = m_new

    def stage(kc, cur, nxt):
        scores(kc + 1, *nxt)
        consume(kc, *cur)

    buf_a, buf_b = (s_a, cm_a), (s_b, cm_b)
    scores(0, *buf_a)

    def body(j, carry):
        stage(2 * j, buf_a, buf_b)
        stage(2 * j + 1, buf_b, buf_a)
        return carry

    lax.fori_loop(0, nk // 2 - 1, body, 0)
    stage(nk - 2, buf_a, buf_b)
    consume(nk - 1, *buf_b)
    outs = []
    for hh in range(2):
        acc = acc_s[hh]
        outs.append(acc[:D_V] / acc[D_V:D_V + 1])
    o_ref[0] = jnp.concatenate(outs, axis=0).T.astype(o_ref.dtype)


def attn_d(qt, k, vt):
    B, S, _ = k.shape
    T = DENSE_TILE
    nq = nk = S // T
    kern = functools.partial(_attn_d_kernel, tile=T, nk=nk)
    return pl.pallas_call(
        kern,
        grid=(B, D_HEADS // 2, nq),
        in_specs=[pl.BlockSpec((1, 2, 1, LANE, T), lambda b, h, i: (b, h, i, 0, 0)),
                  pl.BlockSpec((1, S, 2 * LANE), lambda b, h, i: (b, 0, h)),
                  pl.BlockSpec((1, 2, nk, V_ROWS_D, T), lambda b, h, i: (b, h, 0, 0, 0))],
        out_specs=pl.BlockSpec((1, T, LANE), lambda b, h, i: (b, i, h)),
        out_shape=jax.ShapeDtypeStruct((B, S, D_HEADS * D_V), BF16),
        scratch_shapes=[pltpu.VMEM((2, 1, T), F32), pltpu.VMEM((2, V_ROWS_D, T), F32),
                        pltpu.VMEM((2, T, T), F32), pltpu.VMEM((2, T, T), F32),
                        pltpu.VMEM((2, 1, T), F32), pltpu.VMEM((2, 1, T), F32)],
        compiler_params=_params("parallel", "parallel", "arbitrary"),
        name="attn_d",
    )(qt, k, vt)


def _banded_kernel(*refs, tq, span, half_window, seq_len, nqb, nkb, head_of, has_sink, has_lse):
    it = iter(refs)
    q_ref, k_ref, v_ref, bias_ref = next(it), next(it), next(it), next(it)
    sink_ref = next(it) if has_sink else None
    o_ref = next(it)
    lse_ref = next(it) if has_lse else None

    t = pl.program_id(2)
    start = jnp.clip(t * tq - half_window, 0, seq_len - span)
    start = pl.multiple_of(start, half_window)
    kwin = k_ref[0, pl.ds(start, span), :]
    vwin = v_ref[0, pl.ds(start, span), :]
    lane = lax.broadcasted_iota(jnp.int32, (tq, LANE), 1)
    low = lane < HEAD_DIM
    for c in range(nqb):
        q2 = q_ref[0, :, c * LANE:(c + 1) * LANE]
        kc = c if nkb == nqb else 0
        kb = kwin[:, kc * LANE:(kc + 1) * LANE]
        vb = vwin[:, kc * LANE:(kc + 1) * LANE]
        o_half, lse_half = [], []
        for half in range(2):
            hidx = head_of(c, half)
            qm = jnp.where(low if half == 0 else jnp.logical_not(low), q2, jnp.zeros_like(q2))
            s = lax.dot_general(qm, kb, (((1,), (1,)), ((), ())), preferred_element_type=F32)
            bt = bias_ref[0, hidx]
            s = jnp.where(bt > 0.5 * NEG, s + bt, NEG)
            m = jnp.max(s, axis=-1, keepdims=True)
            if has_sink:
                sk = sink_ref[:, hidx:hidx + 1] * LOG2E
                m = jnp.maximum(m, sk)
            e = jnp.exp2(s - m)
            denom = jnp.sum(e, axis=-1, keepdims=True)
            if has_sink:
                denom = denom + jnp.exp2(sk - m)
            o_half.append(jnp.dot(e.astype(BF16), vb, preferred_element_type=F32) / denom)
            if has_lse:
                lse_half.append(LN2 * m + jnp.log(denom))
        o_ref[0, :, c * LANE:(c + 1) * LANE] = jnp.where(low, o_half[0], o_half[1]).astype(o_ref.dtype)
        if has_lse:
            lse_ref[0, :, c * LANE:(c + 1) * LANE] = jnp.where(low, lse_half[0], lse_half[1])


def banded(proj, bias, sink, *, dil, tq, half_window, nqb, nkb, q_idx, k_idx, v_idx, head_of,
           out_dtype, has_lse, name):
    B, S, N = proj.shape
    L = S // dil
    span = tq + 2 * half_window
    nq = L // tq
    assert L % tq == 0 and L >= span and nq >= 2
    view = proj.reshape(B, L, dil * N)
    OW = nqb * LANE
    has_sink = sink is not None
    kern = functools.partial(_banded_kernel, tq=tq, span=span, half_window=half_window, seq_len=L,
                             nqb=nqb, nkb=nkb, head_of=head_of, has_sink=has_sink, has_lse=has_lse)
    nh = bias.shape[1]

    def variant(t):
        return jnp.where(t == 0, 0, jnp.where(t == nq - 1, 2, 1))

    in_specs = [pl.BlockSpec((1, tq, OW), lambda b, r, t: (b, t, q_idx(r))),
                pl.BlockSpec((1, L, nkb * LANE), lambda b, r, t: (b, 0, k_idx(r))),
                pl.BlockSpec((1, L, nkb * LANE), lambda b, r, t: (b, 0, v_idx(r))),
                pl.BlockSpec((1, nh, tq, span), lambda b, r, t: (variant(t), 0, 0, 0))]
    args = [view, view, view, bias]
    if has_sink:
        in_specs.append(pl.BlockSpec((1, sink.shape[-1]), lambda b, r, t: (0, 0)))
        args.append(sink.reshape(1, -1))
    out_spec = pl.BlockSpec((1, tq, OW), lambda b, r, t: (b, t, r))
    out_shapes = [jax.ShapeDtypeStruct((B, L, dil * OW), out_dtype)]
    out_specs = [out_spec]
    if has_lse:
        out_shapes.append(jax.ShapeDtypeStruct((B, L, dil * OW), F32))
        out_specs.append(out_spec)
    outs = pl.pallas_call(
        kern,
        grid=(B, dil, nq),
        in_specs=in_specs,
        out_specs=out_specs,
        out_shape=out_shapes,
        compiler_params=_params("parallel", "parallel", "arbitrary"),
        name=name,
    )(*args)
    return [o.reshape(B, S, OW) for o in outs]


def _out_proj_ab_kernel(h_ref, a_ref, b_ref, wa_ref, wb_ref, o_ref):
    mix = jnp.dot(a_ref[...], wa_ref[...], preferred_element_type=F32)
    mix = mix + jnp.dot(b_ref[...], wb_ref[...], preferred_element_type=F32)
    o_ref[...] = h_ref[...] + mix


def _out_proj_cd_kernel(h_ref, o0, o1, o2, s0, s1, s2, d_ref, wc_ref, wd_ref, o_ref):
    l0, l1, l2 = s0[...], s1[...], s2[...]
    mx = jnp.maximum(jnp.maximum(l0, l1), l2)
    e0, e1, e2 = jnp.exp(l0 - mx), jnp.exp(l1 - mx), jnp.exp(l2 - mx)
    oc = (e0 * o0[...] + e1 * o1[...] + e2 * o2[...]) / (e0 + e1 + e2)
    mix = jnp.dot(oc.astype(BF16), wc_ref[...], preferred_element_type=F32)
    mix = mix + jnp.dot(d_ref[...], wd_ref[...], preferred_element_type=F32)
    o_ref[...] = h_ref[...] + mix


def _row_call(kern, row_args, full_args, name):
    M = row_args[0].shape[0]
    tm = ROW_TILE
    in_specs = [pl.BlockSpec((tm, a.shape[1]), lambda i: (i, 0)) for a in row_args]
    in_specs += [pl.BlockSpec(a.shape, lambda i: (0, 0)) for a in full_args]
    return pl.pallas_call(
        kern,
        grid=(M // tm,),
        in_specs=in_specs,
        out_specs=pl.BlockSpec((tm, D_MODEL), lambda i: (i, 0)),
        out_shape=jax.ShapeDtypeStruct((M, D_MODEL), F32),
        compiler_params=_params("parallel"),
        name=name,
    )(*row_args, *full_args)


def _ffn_kernel(x_ref, g_ref, wg_ref, wu_ref, wd_ref, fg_ref, o_ref, xn_s, acc_s, *, final_norm):
    f = pl.program_id(1)

    @pl.when(f == 0)
    def _():
        xn_s[...] = _rms(x_ref[...], g_ref[...]).astype(BF16)
        acc_s[...] = jnp.zeros(acc_s.shape, F32)

    xn = xn_s[...]
    gate = jnp.dot(xn, wg_ref[...], preferred_element_type=F32)
    up = jnp.dot(xn, wu_ref[...], preferred_element_type=F32)
    mid = (gate / (1.0 + jnp.exp(-gate)) * up).astype(BF16)
    acc_s[...] += jnp.dot(mid, wd_ref[...], preferred_element_type=F32)

    @pl.when(f == pl.num_programs(1) - 1)
    def _():
        y = x_ref[...] + acc_s[...]
        if final_norm:
            y = _rms(y, fg_ref[...])
        o_ref[...] = y


def ffn(x, g, wg, wu, wd, fg, final_norm, name):
    M, K = x.shape
    tm, tf = FFN_ROW_TILE, FFN_COL_TILE
    kern = functools.partial(_ffn_kernel, final_norm=final_norm)
    return pl.pallas_call(
        kern,
        grid=(M // tm, D_FF // tf),
        in_specs=[pl.BlockSpec((tm, K), lambda i, f: (i, 0)),
                  pl.BlockSpec((1, K), lambda i, f: (0, 0)),
                  pl.BlockSpec((K, tf), lambda i, f: (0, f)),
                  pl.BlockSpec((K, tf), lambda i, f: (0, f)),
                  pl.BlockSpec((tf, K), lambda i, f: (f, 0)),
                  pl.BlockSpec((1, K), lambda i, f: (0, 0))],
        out_specs=pl.BlockSpec((tm, K), lambda i, f: (i, 0)),
        out_shape=jax.ShapeDtypeStruct((M, K), F32),
        scratch_shapes=[pltpu.VMEM((tm, K), BF16), pltpu.VMEM((tm, K), F32)],
        compiler_params=_params("parallel", "arbitrary"),
        name=name,
    )(x, g.reshape(1, K), wg, wu, wd, fg.reshape(1, K))


def _trig_kernel(ang_ref, cos_ref, sin_ref):
    a = ang_ref[...]
    cos_ref[...] = jnp.cos(a)
    sin_ref[...] = jnp.sin(a)


def rope_tables(seq_len):
    half = D_ROPE // 2
    inv = ROPE_THETA ** (-jnp.arange(half, dtype=F32) / half)
    ang = jnp.arange(seq_len).astype(F32)[:, None] * inv[None, :]
    dense = ang.reshape(seq_len * half // LANE, LANE)
    spec = pl.BlockSpec(dense.shape, lambda: (0, 0))
    cos, sin = pl.pallas_call(
        _trig_kernel,
        in_specs=[spec],
        out_specs=[spec, spec],
        out_shape=[jax.ShapeDtypeStruct(dense.shape, F32)] * 2,
        name="rope_trig",
    )(dense)
    cos, sin = cos.reshape(seq_len, half), sin.reshape(seq_len, half)
    pad = LANE - D_NOPE - D_ROPE
    cos_l = jnp.concatenate([jnp.ones((seq_len, D_NOPE), F32), cos, cos, jnp.ones((seq_len, pad), F32)], axis=1)
    sin_l = jnp.concatenate([jnp.zeros((seq_len, D_NOPE), F32), sin, sin, jnp.zeros((seq_len, pad), F32)], axis=1)
    return cos_l, sin_l


def _prep_d_kernel(x_ref, g_ref, wa_ref, qn_ref, kvn_ref, wq_ref, wkv_ref, cos_ref, sin_ref,
                   q_ref, k_ref, v_ref, *, qscale):
    xn = _rms(x_ref[...], g_ref[...]).astype(BF16)
    lat = jnp.dot(xn, wa_ref[...], preferred_element_type=F32)
    cq = _rms(lat[:, :D_Q_LORA], qn_ref[...]).astype(BF16)
    ckv = _rms(lat[:, D_Q_LORA:D_Q_LORA + D_KV_LORA], kvn_ref[...]).astype(BF16)
    o_pe = D_Q_LORA + D_KV_LORA
    cos, sin = cos_ref[...], sin_ref[...]
    kpe = lat[:, o_pe:o_pe + LANE] * cos + lat[:, o_pe + LANE:o_pe + 2 * LANE] * sin
    qq = jnp.dot(cq, wq_ref[...], preferred_element_type=F32)
    kv = jnp.dot(ckv, wkv_ref[...], preferred_element_type=F32)
    kw = D_HEADS * LANE
    for h in range(D_HEADS):
        qh = qq[:, h * LANE:(h + 1) * LANE] * cos + qq[:, kw + h * LANE:kw + (h + 1) * LANE] * sin
        q_ref[:, h * LANE:(h + 1) * LANE] = (qh * qscale).astype(BF16)
        k_ref[:, h * LANE:(h + 1) * LANE] = (kv[:, h * LANE:(h + 1) * LANE] + kpe).astype(BF16)
    v_ref[...] = kv[:, kw:].astype(BF16)


def prep_d(x, g, wa, qn, kvn, wq, wkv, cos_l, sin_l, seq_len):
    M, K = x.shape
    tm = ROW_TILE
    nrow = seq_len // tm
    kw = D_HEADS * LANE
    kern = functools.partial(_prep_d_kernel, qscale=(D_NOPE + D_ROPE) ** -0.5 * LOG2E)
    full = lambda a: pl.BlockSpec(a.shape, lambda i: (0, 0))
    qn2, kvn2, g2 = qn.reshape(1, -1), kvn.reshape(1, -1), g.reshape(1, K)
    return pl.pallas_call(
        kern,
        grid=(M // tm,),
        in_specs=[pl.BlockSpec((tm, K), lambda i: (i, 0)), full(g2), full(wa), full(qn2), full(kvn2),
                  full(wq), full(wkv),
                  pl.BlockSpec((tm, LANE), lambda i: (i % nrow, 0)),
                  pl.BlockSpec((tm, LANE), lambda i: (i % nrow, 0))],
        out_specs=[pl.BlockSpec((tm, kw), lambda i: (i, 0)),
                   pl.BlockSpec((tm, kw), lambda i: (i, 0)),
                   pl.BlockSpec((tm, D_HEADS * D_V), lambda i: (i, 0))],
        out_shape=[jax.ShapeDtypeStruct((M, kw), BF16), jax.ShapeDtypeStruct((M, kw), BF16),
                   jax.ShapeDtypeStruct((M, D_HEADS * D_V), BF16)],
        compiler_params=_params("parallel"),
        name="prep_d",
    )(x, g2, wa, qn2, kvn2, wq, wkv, cos_l, sin_l)


B_HEAD_ORDER = (0, 4, 1, 5, 2, 6, 3, 7)


def _head_cols(order, base):
    return np.concatenate([np.arange(base + h * HEAD_DIM, base + (h + 1) * HEAD_DIM) for h in order])


def _to_tiles_t(t, tile):
    B, S, H, d = t.shape
    return t.reshape(B, S // tile, tile, H, d).transpose(0, 3, 1, 4, 2)


def _rot_partner_cols(w):
    half = D_ROPE // 2
    return jnp.concatenate([-w[..., half:], w[..., :half]], axis=-1)


def kernel(x, bias_table, attn_norm, ffn_norm, final_norm, ab_w_in, ab_lambda_q1, ab_lambda_k1,
           ab_lambda_q2, ab_lambda_k2, ab_subln, ab_sink, ab_w_o, cd_w_in, cd_q_norm, cd_w_q_b,
           cd_kv_norm, cd_w_kv_b, cd_w_o, ffn_w_gate, ffn_w_up, ffn_w_down):
    B, S, _ = x.shape
    M = B * S
    T = DENSE_TILE
    h = x.reshape(M, D_MODEL)
    qk_scale = HEAD_DIM ** -0.5 * LOG2E

    o3 = A_HEADS * (2 * A_QK_DIM + A_V_DIM)
    cols0 = np.concatenate([np.arange(o3), _head_cols(B_HEAD_ORDER, o3),
                            np.arange(o3 + B_HEADS * HEAD_DIM, AB_IN)])
    w0 = ab_w_in[0][:, cols0].astype(BF16)
    cs0 = np.ones((AB_IN,), np.float32)
    cs0[:A_HEADS * A_QK_DIM] = qk_scale
    cs0[o3:o3 + B_HEADS * HEAD_DIM] = qk_scale
    proj0 = norm_proj(h, attn_norm[0], w0, jnp.asarray(cs0), "proj0").reshape(B, S, AB_IN)

    qa = proj0[..., :A_HEADS * A_QK_DIM].reshape(B, S, A_HEADS, 2, HEAD_DIM)
    zeros = jnp.zeros_like(qa[..., 0, :])
    qa_z = jnp.stack([jnp.concatenate([qa[..., 0, :], zeros], axis=-1),
                      jnp.concatenate([zeros, qa[..., 1, :]], axis=-1)], axis=3)
    qat = qa_z.reshape(B, S // T, T, A_HEADS, 2, LANE).transpose(0, 3, 4, 1, 5, 2)
    va = proj0[..., 2 * A_HEADS * A_QK_DIM:o3].reshape(B, S, A_HEADS, A_V_DIM)
    va_ext = jnp.concatenate([va, jnp.ones((B, S, A_HEADS, 1), BF16),
                              jnp.zeros((B, S, A_HEADS, V_ROWS_A - A_V_DIM - 1), BF16)], axis=-1)
    vat = _to_tiles_t(va_ext, T)
    bias_a = bias_tiles(bias_table, nvar=2 * BIAS_REACH + 1, nheads=A_HEADS, head0=0, rows=T, cols=T,
                        off0=-BIAS_REACH * T, off_step=T, row_coef=1, col_coef=-1, dil=1,
                        half_window=None, name="bias_a")
    oa = attn_a(qat, proj0, vat, bias_a, ab_lambda_q1[0], ab_lambda_k1[0], ab_lambda_q2[0],
                ab_lambda_k2[0], ab_subln[0], 0.8 - 0.6 * math.exp(-0.3 * 0))

    tq_b = 256
    bias_b = bias_tiles(bias_table, nvar=3, nheads=B_HEADS, head0=A_HEADS, rows=tq_b,
                        cols=tq_b + 2 * B_HALF_WINDOW, off0=0, off_step=-B_HALF_WINDOW, row_coef=-1,
                        col_coef=1, dil=1, half_window=B_HALF_WINDOW, name="bias_b")
    qb0 = o3 // LANE
    (ob,) = banded(proj0, bias_b, ab_sink[0], dil=1, tq=tq_b, half_window=B_HALF_WINDOW, nqb=4, nkb=1,
                   q_idx=lambda r: qb0 // 4, k_idx=lambda r: qb0 + 4, v_idx=lambda r: qb0 + 5,
                   head_of=lambda c, half: c + 4 * half, out_dtype=BF16, has_lse=False, name="attn_b")

    wo = ab_w_o[0]
    wo_a = wo[:A_HEADS * A_V_DIM].astype(BF16)
    wo_b = wo[_head_cols(B_HEAD_ORDER, A_HEADS * A_V_DIM)].astype(BF16)
    h = _row_call(_out_proj_ab_kernel, [h, oa.reshape(M, -1), ob.reshape(M, -1)], [wo_a, wo_b], "out_proj0")
    h = ffn(h, ffn_norm[0], ffn_w_gate[0].astype(BF16), ffn_w_up[0].astype(BF16),
            ffn_w_down[0].astype(BF16), final_norm, False, "ffn0")

    w1 = cd_w_in[0]
    cs1 = np.ones((CD_C_IN,), np.float32)
    cs1[:C_HEADS * HEAD_DIM] = qk_scale
    proj1 = norm_proj(h, attn_norm[1], w1[:, :CD_C_IN].astype(BF16), jnp.asarray(cs1),
                      "proj1").reshape(B, S, CD_C_IN)

    nblk = CD_C_IN // LANE
    oc, lses = [], []
    for g, (window, dil) in enumerate(C_PATTERNS):
        hw = window // (2 * dil)
        tq_c = 128
        bias_c = bias_tiles(bias_table, nvar=3, nheads=C_HEADS_PER_GROUP, head0=g * C_HEADS_PER_GROUP,
                            rows=tq_c, cols=tq_c + 2 * hw, off0=0, off_step=-hw, row_coef=-1, col_coef=1,
                            dil=dil, half_window=hw, name=f"bias_c{g}")
        half_blocks = nblk // 2
        o_g, lse_g = banded(proj1, bias_c, None, dil=dil, tq=tq_c, half_window=hw, nqb=2, nkb=2,
                            q_idx=lambda r, g=g: r * half_blocks + g,
                            k_idx=lambda r, g=g: r * half_blocks + 3 + g,
                            v_idx=lambda r, g=g: r * half_blocks + 6 + g,
                            head_of=lambda c, half: 2 * c + half, out_dtype=F32, has_lse=True,
                            name=f"attn_c{g}")
        oc.append(o_g.reshape(M, -1))
        lses.append(lse_g.reshape(M, -1))

    o_q, o_kv = CD_C_IN + D_Q_LORA, CD_C_IN + D_Q_LORA + D_KV_LORA
    w_pe = w1[:, o_kv:]
    lane_pad = lambda w: jnp.pad(w, ((0, 0), (D_NOPE, LANE - D_NOPE - D_ROPE)))
    wa = jnp.concatenate([w1[:, CD_C_IN:o_kv], lane_pad(w_pe), lane_pad(_rot_partner_cols(w_pe))],
                         axis=1).astype(BF16)
    wq3 = cd_w_q_b[0].reshape(D_Q_LORA, D_HEADS, D_NOPE + D_ROPE)
    zpad = jnp.zeros((D_Q_LORA, D_HEADS, LANE - D_NOPE - D_ROPE), F32)
    wq_main = jnp.concatenate([wq3, zpad], axis=-1)
    wq_rot = jnp.concatenate([jnp.zeros_like(wq3[..., :D_NOPE]), _rot_partner_cols(wq3[..., D_NOPE:]), zpad],
                             axis=-1)
    wq = jnp.concatenate([wq_main.reshape(D_Q_LORA, -1), wq_rot.reshape(D_Q_LORA, -1)], axis=1).astype(BF16)
    wkv3 = cd_w_kv_b[0].reshape(D_KV_LORA, D_HEADS, D_NOPE + D_V)
    wk = jnp.pad(wkv3[..., :D_NOPE], ((0, 0), (0, 0), (0, LANE - D_NOPE))).reshape(D_KV_LORA, -1)
    wv = wkv3[..., D_NOPE:].reshape(D_KV_LORA, -1)
    wkv = jnp.concatenate([wk, wv], axis=1).astype(BF16)
    cos_l, sin_l = rope_tables(S)
    qd, kd, vd = prep_d(h, attn_norm[1], wa, cd_q_norm[0], cd_kv_norm[0], wq, wkv, cos_l, sin_l, S)

    qdt = _to_tiles_t(qd.reshape(B, S, D_HEADS, LANE), T)
    vd4 = vd.reshape(B, S, D_HEADS, D_V)
    vd_ext = jnp.concatenate([vd4, jnp.ones((B, S, D_HEADS, 1), BF16),
                              jnp.zeros((B, S, D_HEADS, V_ROWS_D - D_V - 1), BF16)], axis=-1)
    vdt = _to_tiles_t(vd_ext, T)
    od = attn_d(qdt, kd.reshape(B, S, D_HEADS * LANE), vdt)

    wo1 = cd_w_o[0]
    wo_c = wo1[:C_HEADS_PER_GROUP * HEAD_DIM].astype(BF16)
    wo_d = wo1[C_HEADS_PER_GROUP * HEAD_DIM:].astype(BF16)
    h = _row_call(_out_proj_cd_kernel, [h, *oc, *lses, od.reshape(M, -1)], [wo_c, wo_d], "out_proj1")
    h = ffn(h, ffn_norm[1], ffn_w_gate[1].astype(BF16), ffn_w_up[1].astype(BF16),
            ffn_w_down[1].astype(BF16), final_norm, True, "ffn1")
    return h.reshape(B, S, D_MODEL)
```

```python
import functools
import math

import numpy as np
import jax
import jax.numpy as jnp
from jax import lax
from jax.experimental import pallas as pl
from jax.experimental.pallas import tpu as pltpu

F32 = jnp.float32
BF16 = jnp.bfloat16

D_MODEL = 1024
HEAD_DIM = 64
EPS = 1e-6
NEG = -1e30
LOG2E = math.log2(math.e)
LN2 = math.log(2.0)

A_HEADS = 4
A_QK_DIM = 2 * HEAD_DIM
A_V_DIM = 2 * HEAD_DIM
B_HEADS = 8
B_KV_HEADS = 2
B_HALF_WINDOW = 128
C_PATTERNS = ((128, 1), (512, 4), (2048, 16))
C_HEADS_PER_GROUP = 4
C_HEADS = C_HEADS_PER_GROUP * len(C_PATTERNS)
D_HEADS = 12
D_Q_LORA = 384
D_KV_LORA = 256
D_NOPE = 64
D_ROPE = 32
D_V = 64
ROPE_THETA = 10000.0
NUM_BUCKETS = 32
MAX_DISTANCE = 1024
D_FF = 2816
AB_IN = 2304
CD_C_IN = 3 * C_HEADS * HEAD_DIM

LANE = 128
VMEM_LIMIT = 48 * 1024 * 1024

ROW_TILE = 512
FFN_ROW_TILE = 1024
FFN_COL_TILE = 256
DENSE_TILE = 512
QCOLS = 256
UNROLL = 4
BIAS_REACH = 3
V_ROWS_D = 80
V_ROWS_A = 144


def _bucket_thresholds():
    nb = NUM_BUCKETS // 2
    max_exact = nb // 2
    n = np.arange(1, 4 * MAX_DISTANCE)
    large = max_exact + (np.log(n.astype(np.float32) / np.float32(max_exact))
                         / np.float32(math.log(MAX_DISTANCE / max_exact))
                         * np.float32(nb - max_exact)).astype(np.int32)
    mag = np.where(n < max_exact, n, np.minimum(large, nb - 1))
    return tuple(int(n[np.argmax(mag >= k)]) for k in range(1, nb))


BUCKET_THRESHOLDS = _bucket_thresholds()
assert BUCKET_THRESHOLDS[-1] <= (BIAS_REACH - 1) * DENSE_TILE + 1


def _params(*sem):
    return pltpu.CompilerParams(dimension_semantics=sem, vmem_limit_bytes=VMEM_LIMIT)


def _rms(x, g):
    return x * lax.rsqrt(jnp.mean(x * x, axis=-1, keepdims=True) + EPS) * g


def _norm_proj_kernel(x_ref, g_ref, w_ref, cs_ref, o_ref):
    xn = _rms(x_ref[...], g_ref[...]).astype(BF16)
    y = jnp.dot(xn, w_ref[...], preferred_element_type=F32)
    o_ref[...] = (y * cs_ref[...]).astype(o_ref.dtype)


def norm_proj(x, g, w, colscale, name):
    M, K = x.shape
    N = w.shape[1]
    tm = ROW_TILE
    return pl.pallas_call(
        _norm_proj_kernel,
        grid=(M // tm,),
        in_specs=[pl.BlockSpec((tm, K), lambda i: (i, 0)),
                  pl.BlockSpec((1, K), lambda i: (0, 0)),
                  pl.BlockSpec((K, N), lambda i: (0, 0)),
                  pl.BlockSpec((1, N), lambda i: (0, 0))],
        out_specs=pl.BlockSpec((tm, N), lambda i: (i, 0)),
        out_shape=jax.ShapeDtypeStruct((M, N), BF16),
        compiler_params=_params("parallel"),
        name=name,
    )(x, g.reshape(1, K), w, colscale.reshape(1, N))


def _bias_kernel(tab_ref, o_ref, *, off0, off_step, row_coef, col_coef, dil, half_window, head0):
    v = pl.program_id(0)
    hcol = head0 + pl.program_id(1)
    R, C = o_ref.shape[-2:]
    row = lax.broadcasted_iota(jnp.int32, (R, C), 0)
    col = lax.broadcasted_iota(jnp.int32, (R, C), 1)
    rel = off0 + v * off_step + row_coef * row + col_coef * col
    dist = rel * dil
    n = jnp.abs(dist)
    nb = NUM_BUCKETS // 2
    vneg = jnp.full((R, C), tab_ref[0, hcol], F32)
    vpos = jnp.full((R, C), tab_ref[nb, hcol], F32)
    for k, thr in enumerate(BUCKET_THRESHOLDS, start=1):
        ge = n >= thr
        vneg = jnp.where(ge, tab_ref[k, hcol], vneg)
        vpos = jnp.where(ge, tab_ref[nb + k, hcol], vpos)
    val = jnp.where(dist > 0, vpos, vneg) * LOG2E
    if half_window is not None:
        val = jnp.where(jnp.abs(rel) <= half_window, val, NEG)
    o_ref[0, 0] = val


def bias_tiles(table, *, nvar, nheads, head0, rows, cols, off0, off_step, row_coef, col_coef,
               dil, half_window, name):
    kern = functools.partial(_bias_kernel, off0=off0, off_step=off_step, row_coef=row_coef,
                             col_coef=col_coef, dil=dil, half_window=half_window, head0=head0)
    return pl.pallas_call(
        kern,
        grid=(nvar, nheads),
        in_specs=[pl.BlockSpec(memory_space=pltpu.SMEM)],
        out_specs=pl.BlockSpec((1, 1, rows, cols), lambda v, h: (v, h, 0, 0)),
        out_shape=jax.ShapeDtypeStruct((nvar, nheads, rows, cols), F32),
        compiler_params=_params("parallel", "parallel"),
        name=name,
    )(table)


def _dense_pipeline(nk, tile, n_streams, score_fn, value_fn, m_s, acc_s, bufs):
    units = [(i, c * QCOLS) for i in range(n_streams) for c in range(tile // QCOLS)]

    def stage(kc, cur, nxt):
        for i, c0 in units:
            cols = slice(c0, c0 + QCOLS)
            if nxt is not None:
                s = score_fn(kc + 1, i, cols)
                nxt[0][i, :, cols] = s
                nxt[1][i, :, cols] = jnp.max(s, axis=0, keepdims=True)
            if cur is not None:
                m_old = m_s[i, :, cols]
                m_new = jnp.maximum(m_old, cur[1][i, :, cols])
                alpha = jnp.exp2(m_old - m_new)
                p = jnp.exp2((cur[0][i, :, cols] - m_new).astype(BF16))
                acc_s[i, :, cols] = (alpha * acc_s[i, :, cols]
                                     + jnp.dot(value_fn(kc, i), p, preferred_element_type=F32))
                m_s[i, :, cols] = m_new

    m_s[...] = jnp.full(m_s.shape, NEG, F32)
    acc_s[...] = jnp.zeros(acc_s.shape, F32)
    stage(-1, None, bufs[0])
    n_loop = (nk - 1) // UNROLL

    def body(j, carry):
        for u in range(UNROLL):
            stage(UNROLL * j + u, bufs[u % 2], bufs[(u + 1) % 2])
        return carry

    lax.fori_loop(0, n_loop, body, 0)
    for kc in range(n_loop * UNROLL, nk):
        stage(kc, bufs[kc % 2], bufs[(kc + 1) % 2] if kc < nk - 1 else None)


def _attn_a_kernel(q_ref, k_ref, v_ref, bias_ref, lq1_ref, lk1_ref, lq2_ref, lk2_ref, subln_ref, o_ref,
                   m_s, acc_s, s_a, s_b, cm_a, cm_b, *, tile, nk, lambda_init):
    qi = pl.program_id(2)

    def score_fn(kc, j, cols):
        kblk = k_ref[0, pl.ds(pl.multiple_of(kc * tile, tile), tile), :]
        bt = bias_ref[jnp.clip(kc - qi, -BIAS_REACH, BIAS_REACH) + BIAS_REACH, 0, :, cols]
        return jnp.dot(kblk, q_ref[0, 0, j, 0, :, cols], preferred_element_type=F32) + bt

    def value_fn(kc, j):
        return v_ref[0, 0, kc]

    _dense_pipeline(nk, tile, 2, score_fn, value_fn, m_s, acc_s, ((s_a, cm_a), (s_b, cm_b)))

    lam =(jnp.exp(jnp.sum(lq1_ref[...] * lk1_ref[...], axis=-1, keepdims=True))
           - jnp.exp(jnp.sum(lq2_ref[...] * lk2_ref[...], axis=-1, keepdims=True)) + lambda_init)
    a1, a2 = acc_s[0], acc_s[1]
    o = (a1[:A_V_DIM] / a1[A_V_DIM:A_V_DIM + 1]
         - lam * (a2[:A_V_DIM] / a2[A_V_DIM:A_V_DIM + 1]))
    ms = jnp.mean(o * o, axis=0, keepdims=True)
    y = o * lax.rsqrt(ms + EPS) * subln_ref[...] * (1.0 - lambda_init)
    o_ref[0] = y.T.astype(o_ref.dtype)


def attn_a(qt, proj, vt, bias, lq1, lk1, lq2, lk2, subln, lambda_init):
    B, S, _ = proj.shape
    T = DENSE_TILE
    nq = nk = S // T
    assert nk % 2 == 0 and nk >= 4
    k_block0 = (A_HEADS * A_QK_DIM) // LANE
    kern = functools.partial(_attn_a_kernel, tile=T, nk=nk, lambda_init=lambda_init)
    vec = lambda n: pl.BlockSpec((1, n), lambda b, h, i: (0, 0))
    return pl.pallas_call(
        kern,
        grid=(B, A_HEADS, nq),
        in_specs=[pl.BlockSpec((1, 1, 2, 1, LANE, T), lambda b, h, i: (b, h, 0, i, 0, 0)),
                  pl.BlockSpec((1, S, LANE), lambda b, h, i: (b, 0, k_block0 + h)),
                  pl.BlockSpec((1, 1, nk, V_ROWS_A, T), lambda b, h, i: (b, h, 0, 0, 0)),
                  pl.BlockSpec((2 * BIAS_REACH + 1, 1, T, T), lambda b, h, i: (0, h, 0, 0)),
                  vec(HEAD_DIM), vec(HEAD_DIM), vec(HEAD_DIM), vec(HEAD_DIM),
                  pl.BlockSpec((A_V_DIM, 1), lambda b, h, i: (0, 0))],
        out_specs=pl.BlockSpec((1, T, LANE), lambda b, h, i: (b, i, h)),
        out_shape=jax.ShapeDtypeStruct((B, S, A_HEADS * A_V_DIM), BF16),
        scratch_shapes=[pltpu.VMEM((2, 1, T), F32), pltpu.VMEM((2, V_ROWS_A, T), F32),
                        pltpu.VMEM((2, T, T), F32), pltpu.VMEM((2, T, T), F32),
                        pltpu.VMEM((2, 1, T), F32), pltpu.VMEM((2, 1, T), F32)],
        compiler_params=_params("parallel", "parallel", "arbitrary"),
        name="attn_a",
    )(qt, proj, vt, bias, lq1.reshape(1, -1), lk1.reshape(1, -1), lq2.reshape(1, -1),
      lk2.reshape(1, -1), subln.reshape(-1, 1))


def _attn_d_kernel(q_ref, k_ref, v_ref, o_ref, m_s, acc_s, s_a, s_b, cm_a, cm_b, *, tile, nk):
    def score_fn(kc, hh, cols):
        kblk = k_ref[0, pl.ds(pl.multiple_of(kc * tile, tile), tile), hh * LANE:(hh + 1) * LANE]
        return jnp.dot(kblk, q_ref[0, hh, 0, :, cols], preferred_element_type=F32)

    def value_fn(kc, hh):
        return v_ref[0, hh, kc]

    _dense_pipeline(nk, tile, 2, score_fn, value_fn, m_s, acc_s, ((s_a, cm_a), (s_b, cm_b)))
    outs = []
    for hh in range(2):
        acc = acc_s[hh]
        outs.append(acc[:D_V] / acc[D_V:D_V + 1])
    o_ref[0] = jnp.concatenate(outs, axis=0).T.astype(o_ref.dtype)


def attn_d(qt, k, vt):
    B, S, _ = k.shape
    T = DENSE_TILE
    nq = nk = S // T
    kern = functools.partial(_attn_d_kernel, tile=T, nk=nk)
    return pl.pallas_call(
        kern,
        grid=(B, D_HEADS // 2, nq),
        in_specs=[pl.BlockSpec((1, 2, 1, LANE, T), lambda b, h, i: (b, h, i, 0, 0)),
                  pl.BlockSpec((1, S, 2 * LANE), lambda b, h, i: (b, 0, h)),
                  pl.BlockSpec((1, 2, nk, V_ROWS_D, T), lambda b, h, i: (b, h, 0, 0, 0))],
        out_specs=pl.BlockSpec((1, T, LANE), lambda b, h, i: (b, i, h)),
        out_shape=jax.ShapeDtypeStruct((B, S, D_HEADS * D_V), BF16),
        scratch_shapes=[pltpu.VMEM((2, 1, T), F32), pltpu.VMEM((2, V_ROWS_D, T), F32),
                        pltpu.VMEM((2, T, T), F32), pltpu.VMEM((2, T, T), F32),
                        pltpu.VMEM((2, 1, T), F32), pltpu.VMEM((2, 1, T), F32)],
        compiler_params=_params("parallel", "parallel", "arbitrary"),
        name="attn_d",
    )(qt, k, vt)


def _banded_kernel(*refs, tq, span, half_window, seq_len, nqb, nkb, head_of, has_sink, has_lse):
    it = iter(refs)
    q_ref, k_ref, v_ref, bias_ref = next(it), next(it), next(it), next(it)
    sink_ref = next(it) if has_sink else None
    o_ref = next(it)
    lse_ref = next(it) if has_lse else None

    t = pl.program_id(2)
    start = jnp.clip(t * tq - half_window, 0, seq_len - span)
    start = pl.multiple_of(start, half_window)
    kwin = k_ref[0, pl.ds(start, span), :]
    vwin = v_ref[0, pl.ds(start, span), :]
    lane = lax.broadcasted_iota(jnp.int32, (tq, LANE), 1)
    low = lane < HEAD_DIM
    for c in range(nqb):
        q2 = q_ref[0, :, c * LANE:(c + 1) * LANE]
        kc = c if nkb == nqb else 0
        kb = kwin[:, kc * LANE:(kc + 1) * LANE]
        vb = vwin[:, kc * LANE:(kc + 1) * LANE]
        o_half, lse_half = [], []
        for half in range(2):
            hidx = head_of(c, half)
            qm = jnp.where(low if half == 0 else jnp.logical_not(low), q2, jnp.zeros_like(q2))
            s = lax.dot_general(qm, kb, (((1,), (1,)), ((), ())), preferred_element_type=F32)
            bt = bias_ref[0, hidx]
            s = jnp.where(bt > 0.5 * NEG, s + bt, NEG)
            m = jnp.max(s, axis=-1, keepdims=True)
            if has_sink:
                sk = sink_ref[:, hidx:hidx + 1] * LOG2E
                m = jnp.maximum(m, sk)
            e = jnp.exp2(s - m)
            denom = jnp.sum(e, axis=-1, keepdims=True)
            if has_sink:
                denom = denom + jnp.exp2(sk - m)
            o_half.append(jnp.dot(e.astype(BF16), vb, preferred_element_type=F32) / denom)
            if has_lse:
                lse_half.append(LN2 * m + jnp.log(denom))
        o_ref[0, :, c * LANE:(c + 1) * LANE] = jnp.where(low, o_half[0], o_half[1]).astype(o_ref.dtype)
        if has_lse:
            lse_ref[0, :, c * LANE:(c + 1) * LANE] = jnp.where(low, lse_half[0], lse_half[1])


def banded(proj, bias, sink, *, dil, tq, half_window, nqb, nkb, q_idx, k_idx, v_idx, head_of,
           out_dtype, has_lse, name):
    B, S, N = proj.shape
    L = S // dil
    span = tq + 2 * half_window
    nq = L // tq
    assert L % tq == 0 and L >= span and nq >= 2
    view = proj.reshape(B, L, dil * N)
    OW = nqb * LANE
    has_sink = sink is not None
    kern = functools.partial(_banded_kernel, tq=tq, span=span, half_window=half_window, seq_len=L,
                             nqb=nqb, nkb=nkb, head_of=head_of, has_sink=has_sink, has_lse=has_lse)
    nh = bias.shape[1]

    def variant(t):
        return jnp.where(t == 0, 0, jnp.where(t == nq - 1, 2, 1))

    in_specs = [pl.BlockSpec((1, tq, OW), lambda b, r, t: (b, t, q_idx(r))),
                pl.BlockSpec((1, L, nkb * LANE), lambda b, r, t: (b, 0, k_idx(r))),
                pl.BlockSpec((1, L, nkb * LANE), lambda b, r, t: (b, 0, v_idx(r))),
                pl.BlockSpec((1, nh, tq, span), lambda b, r, t: (variant(t), 0, 0, 0))]
    args = [view, view, view, bias]
    if has_sink:
        in_specs.append(pl.BlockSpec((1, sink.shape[-1]), lambda b, r, t: (0, 0)))
        args.append(sink.reshape(1, -1))
    out_spec = pl.BlockSpec((1, tq, OW), lambda b, r, t: (b, t, r))
    out_shapes = [jax.ShapeDtypeStruct((B, L, dil * OW), out_dtype)]
    out_specs = [out_spec]
    if has_lse:
        out_shapes.append(jax.ShapeDtypeStruct((B, L, dil * OW), F32))
        out_specs.append(out_spec)
    outs = pl.pallas_call(
        kern,
        grid=(B, dil, nq),
        in_specs=in_specs,
        out_specs=out_specs,
        out_shape=out_shapes,
        compiler_params=_params("parallel", "parallel", "arbitrary"),
        name=name,
    )(*args)
    return [o.reshape(B, S, OW) for o in outs]


def _out_proj_ab_kernel(h_ref, a_ref, b_ref, wa_ref, wb_ref, o_ref):
    mix = jnp.dot(a_ref[...], wa_ref[...], preferred_element_type=F32)
    mix = mix + jnp.dot(b_ref[...], wb_ref[...], preferred_element_type=F32)
    o_ref[...] = h_ref[...] + mix


def _out_proj_cd_kernel(h_ref, o0, o1, o2, s0, s1, s2, d_ref, wc_ref, wd_ref, o_ref):
    l0, l1, l2 = s0[...], s1[...], s2[...]
    mx = jnp.maximum(jnp.maximum(l0, l1), l2)
    e0, e1, e2 = jnp.exp(l0 - mx), jnp.exp(l1 - mx), jnp.exp(l2 - mx)
    oc = (e0 * o0[...] + e1 * o1[...] + e2 * o2[...]) / (e0 + e1 + e2)
    mix = jnp.dot(oc.astype(BF16), wc_ref[...], preferred_element_type=F32)
    mix = mix + jnp.dot(d_ref[...], wd_ref[...], preferred_element_type=F32)
    o_ref[...] = h_ref[...] + mix


def _row_call(kern, row_args, full_args, name):
    M = row_args[0].shape[0]
    tm = ROW_TILE
    in_specs = [pl.BlockSpec((tm, a.shape[1]), lambda i: (i, 0)) for a in row_args]
    in_specs += [pl.BlockSpec(a.shape, lambda i: (0, 0)) for a in full_args]
    return pl.pallas_call(
        kern,
        grid=(M // tm,),
        in_specs=in_specs,
        out_specs=pl.BlockSpec((tm, D_MODEL), lambda i: (i, 0)),
        out_shape=jax.ShapeDtypeStruct((M, D_MODEL), F32),
        compiler_params=_params("parallel"),
        name=name,
    )(*row_args, *full_args)


def _ffn_kernel(x_ref, g_ref, wg_ref, wu_ref, wd_ref, fg_ref, o_ref, xn_s, acc_s, *, final_norm):
    f = pl.program_id(1)

    @pl.when(f == 0)
    def _():
        xn_s[...] = _rms(x_ref[...], g_ref[...]).astype(BF16)
        acc_s[...] = jnp.zeros(acc_s.shape, F32)

    xn = xn_s[...]
    gate = jnp.dot(xn, wg_ref[...], preferred_element_type=F32)
    up = jnp.dot(xn, wu_ref[...], preferred_element_type=F32)
    mid = (gate / (1.0 + jnp.exp(-gate)) * up).astype(BF16)
    acc_s[...] += jnp.dot(mid, wd_ref[...], preferred_element_type=F32)

    @pl.when(f == pl.num_programs(1) - 1)
    def _():
        y = x_ref[...] + acc_s[...]
        if final_norm:
            y = _rms(y, fg_ref[...])
        o_ref[...] = y


def ffn(x, g, wg, wu, wd, fg, final_norm, name):
    M, K = x.shape
    tm, tf = FFN_ROW_TILE, FFN_COL_TILE
    kern = functools.partial(_ffn_kernel, final_norm=final_norm)
    return pl.pallas_call(
        kern,
        grid=(M // tm, D_FF // tf),
        in_specs=[pl.BlockSpec((tm, K), lambda i, f: (i, 0)),
                  pl.BlockSpec((1, K), lambda i, f: (0, 0)),
                  pl.BlockSpec((K, tf), lambda i, f: (0, f)),
                  pl.BlockSpec((K, tf), lambda i, f: (0, f)),
                  pl.BlockSpec((tf, K), lambda i, f: (f, 0)),
                  pl.BlockSpec((1, K), lambda i, f: (0, 0))],
        out_specs=pl.BlockSpec((tm, K), lambda i, f: (i, 0)),
        out_shape=jax.ShapeDtypeStruct((M, K), F32),
        scratch_shapes=[pltpu.VMEM((tm, K), BF16), pltpu.VMEM((tm, K), F32)],
        compiler_params=_params("parallel", "arbitrary"),
        name=name,
    )(x, g.reshape(1, K), wg, wu, wd, fg.reshape(1, K))


def _trig_kernel(ang_ref, cos_ref, sin_ref):
    a = ang_ref[...]
    cos_ref[...] = jnp.cos(a)
    sin_ref[...] = jnp.sin(a)


def rope_tables(seq_len):
    half = D_ROPE // 2
    inv = ROPE_THETA ** (-jnp.arange(half, dtype=F32) / half)
    ang = jnp.arange(seq_len).astype(F32)[:, None] * inv[None, :]
    dense = ang.reshape(seq_len * half // LANE, LANE)
    spec = pl.BlockSpec(dense.shape, lambda: (0, 0))
    cos, sin = pl.pallas_call(
        _trig_kernel,
        in_specs=[spec],
        out_specs=[spec, spec],
        out_shape=[jax.ShapeDtypeStruct(dense.shape, F32)] * 2,
        name="rope_trig",
    )(dense)
    cos, sin = cos.reshape(seq_len, half), sin.reshape(seq_len, half)
    pad = LANE - D_NOPE - D_ROPE
    cos_l = jnp.concatenate([jnp.ones((seq_len, D_NOPE), F32), cos, cos, jnp.ones((seq_len, pad), F32)], axis=1)
    sin_l = jnp.concatenate([jnp.zeros((seq_len, D_NOPE), F32), sin, sin, jnp.zeros((seq_len, pad), F32)], axis=1)
    return cos_l, sin_l


def _prep_d_kernel(x_ref, g_ref, wa_ref, qn_ref, kvn_ref, wq_ref, wkv_ref, cos_ref, sin_ref,
                   q_ref, k_ref, v_ref, *, qscale):
    xn = _rms(x_ref[...], g_ref[...]).astype(BF16)
    lat = jnp.dot(xn, wa_ref[...], preferred_element_type=F32)
    cq = _rms(lat[:, :D_Q_LORA], qn_ref[...]).astype(BF16)
    ckv = _rms(lat[:, D_Q_LORA:D_Q_LORA + D_KV_LORA], kvn_ref[...]).astype(BF16)
    o_pe = D_Q_LORA + D_KV_LORA
    cos, sin = cos_ref[...], sin_ref[...]
    kpe = lat[:, o_pe:o_pe + LANE] * cos + lat[:, o_pe + LANE:o_pe + 2 * LANE] * sin
    qq = jnp.dot(cq, wq_ref[...], preferred_element_type=F32)
    kv = jnp.dot(ckv, wkv_ref[...], preferred_element_type=F32)
    kw = D_HEADS * LANE
    for h in range(D_HEADS):
        qh = qq[:, h * LANE:(h + 1) * LANE] * cos + qq[:, kw + h * LANE:kw + (h + 1) * LANE] * sin
        q_ref[:, h * LANE:(h + 1) * LANE] = (qh * qscale).astype(BF16)
        k_ref[:, h * LANE:(h + 1) * LANE] = (kv[:, h * LANE:(h + 1) * LANE] + kpe).astype(BF16)
    v_ref[...] = kv[:, kw:].astype(BF16)


def prep_d(x, g, wa, qn, kvn, wq, wkv, cos_l, sin_l, seq_len):
    M, K = x.shape
    tm = ROW_TILE
    nrow = seq_len // tm
    kw = D_HEADS * LANE
    kern = functools.partial(_prep_d_kernel, qscale=(D_NOPE + D_ROPE) ** -0.5 * LOG2E)
    full = lambda a: pl.BlockSpec(a.shape, lambda i: (0, 0))
    qn2, kvn2, g2 = qn.reshape(1, -1), kvn.reshape(1, -1), g.reshape(1, K)
    return pl.pallas_call(
        kern,
        grid=(M // tm,),
        in_specs=[pl.BlockSpec((tm, K), lambda i: (i, 0)), full(g2), full(wa), full(qn2), full(kvn2),
                  full(wq), full(wkv),
                  pl.BlockSpec((tm, LANE), lambda i: (i % nrow, 0)),
                  pl.BlockSpec((tm, LANE), lambda i: (i % nrow, 0))],
        out_specs=[pl.BlockSpec((tm, kw), lambda i: (i, 0)),
                   pl.BlockSpec((tm, kw), lambda i: (i, 0)),
                   pl.BlockSpec((tm, D_HEADS * D_V), lambda i: (i, 0))],
        out_shape=[jax.ShapeDtypeStruct((M, kw), BF16), jax.ShapeDtypeStruct((M, kw), BF16),
                   jax.ShapeDtypeStruct((M, D_HEADS * D_V), BF16)],
        compiler_params=_params("parallel"),
        name="prep_d",
    )(x, g2, wa, qn2, kvn2, wq, wkv, cos_l, sin_l)


B_HEAD_ORDER = (0, 4, 1, 5, 2, 6, 3, 7)


def _head_cols(order, base):
    return np.concatenate([np.arange(base + h * HEAD_DIM, base + (h + 1) * HEAD_DIM) for h in order])


def _to_tiles_t(t, tile):
    B, S, H, d = t.shape
    return t.reshape(B, S // tile, tile, H, d).transpose(0, 3, 1, 4, 2)


def _rot_partner_cols(w):
    half = D_ROPE // 2
    return jnp.concatenate([-w[..., half:], w[..., :half]], axis=-1)


def kernel(x, bias_table, attn_norm, ffn_norm, final_norm, ab_w_in, ab_lambda_q1, ab_lambda_k1,
           ab_lambda_q2, ab_lambda_k2, ab_subln, ab_sink, ab_w_o, cd_w_in, cd_q_norm, cd_w_q_b,
           cd_kv_norm, cd_w_kv_b, cd_w_o, ffn_w_gate, ffn_w_up, ffn_w_down):
    B, S, _ = x.shape
    M = B * S
    T = DENSE_TILE
    h = x.reshape(M, D_MODEL)
    qk_scale = HEAD_DIM ** -0.5 * LOG2E

    o3 = A_HEADS * (2 * A_QK_DIM + A_V_DIM)
    cols0 = np.concatenate([np.arange(o3), _head_cols(B_HEAD_ORDER, o3),
                            np.arange(o3 + B_HEADS * HEAD_DIM, AB_IN)])
    w0 = ab_w_in[0][:, cols0].astype(BF16)
    cs0 = np.ones((AB_IN,), np.float32)
    cs0[:A_HEADS * A_QK_DIM] = qk_scale
    cs0[o3:o3 + B_HEADS * HEAD_DIM] = qk_scale
    proj0 = norm_proj(h, attn_norm[0], w0, jnp.asarray(cs0), "proj0").reshape(B, S, AB_IN)

    qa = proj0[..., :A_HEADS * A_QK_DIM].reshape(B, S, A_HEADS, 2, HEAD_DIM)
    zeros = jnp.zeros_like(qa[..., 0, :])
    qa_z = jnp.stack([jnp.concatenate([qa[..., 0, :], zeros], axis=-1),
                      jnp.concatenate([zeros, qa[..., 1, :]], axis=-1)], axis=3)
    qat = qa_z.reshape(B, S // T, T, A_HEADS, 2, LANE).transpose(0, 3, 4, 1, 5, 2)
    va = proj0[..., 2 * A_HEADS * A_QK_DIM:o3].reshape(B, S, A_HEADS, A_V_DIM)
    va_ext = jnp.concatenate([va, jnp.ones((B, S, A_HEADS, 1), BF16),
                              jnp.zeros((B, S, A_HEADS, V_ROWS_A - A_V_DIM - 1), BF16)], axis=-1)
    vat = _to_tiles_t(va_ext, T)
    bias_a = bias_tiles(bias_table, nvar=2 * BIAS_REACH + 1, nheads=A_HEADS, head0=0, rows=T, cols=T,
                        off0=-BIAS_REACH * T, off_step=T, row_coef=1, col_coef=-1, dil=1,
                        half_window=None, name="bias_a")
    oa = attn_a(qat, proj0, vat, bias_a, ab_lambda_q1[0], ab_lambda_k1[0], ab_lambda_q2[0],
                ab_lambda_k2[0], ab_subln[0], 0.8 - 0.6 * math.exp(-0.3 * 0))

    tq_b = 256
    bias_b = bias_tiles(bias_table, nvar=3, nheads=B_HEADS, head0=A_HEADS, rows=tq_b,
                        cols=tq_b + 2 * B_HALF_WINDOW, off0=0, off_step=-B_HALF_WINDOW, row_coef=-1,
                        col_coef=1, dil=1, half_window=B_HALF_WINDOW, name="bias_b")
    qb0 = o3 // LANE
    (ob,) = banded(proj0, bias_b, ab_sink[0], dil=1, tq=tq_b, half_window=B_HALF_WINDOW, nqb=4, nkb=1,
                   q_idx=lambda r: qb0 // 4, k_idx=lambda r: qb0 + 4, v_idx=lambda r: qb0 + 5,
                   head_of=lambda c, half: c + 4 * half, out_dtype=BF16, has_lse=False, name="attn_b")

    wo = ab_w_o[0]
    wo_a = wo[:A_HEADS * A_V_DIM].astype(BF16)
    wo_b = wo[_head_cols(B_HEAD_ORDER, A_HEADS * A_V_DIM)].astype(BF16)
    h = _row_call(_out_proj_ab_kernel, [h, oa.reshape(M, -1), ob.reshape(M, -1)], [wo_a, wo_b], "out_proj0")
    h = ffn(h, ffn_norm[0], ffn_w_gate[0].astype(BF16), ffn_w_up[0].astype(BF16),
            ffn_w_down[0].astype(BF16), final_norm, False, "ffn0")

    w1 = cd_w_in[0]
    cs1 = np.ones((CD_C_IN,), np.float32)
    cs1[:C_HEADS * HEAD_DIM] = qk_scale
    proj1 = norm_proj(h, attn_norm[1], w1[:, :CD_C_IN].astype(BF16), jnp.asarray(cs1),
                      "proj1").reshape(B, S, CD_C_IN)

    nblk = CD_C_IN // LANE
    oc, lses = [], []
    for g, (window, dil) in enumerate(C_PATTERNS):
        hw = window // (2 * dil)
        tq_c = 128
        bias_c = bias_tiles(bias_table, nvar=3, nheads=C_HEADS_PER_GROUP, head0=g * C_HEADS_PER_GROUP,
                            rows=tq_c, cols=tq_c + 2 * hw, off0=0, off_step=-hw, row_coef=-1, col_coef=1,
                            dil=dil, half_window=hw, name=f"bias_c{g}")
        half_blocks = nblk // 2
        o_g, lse_g = banded(proj1, bias_c, None, dil=dil, tq=tq_c, half_window=hw, nqb=2, nkb=2,
                            q_idx=lambda r, g=g: r * half_blocks + g,
                            k_idx=lambda r, g=g: r * half_blocks + 3 + g,
                            v_idx=lambda r, g=g: r * half_blocks + 6 + g,
                            head_of=lambda c, half: 2 * c + half, out_dtype=F32, has_lse=True,
                            name=f"attn_c{g}")
        oc.append(o_g.reshape(M, -1))
        lses.append(lse_g.reshape(M, -1))

    o_q, o_kv = CD_C_IN + D_Q_LORA, CD_C_IN + D_Q_LORA + D_KV_LORA
    w_pe = w1[:, o_kv:]
    lane_pad = lambda w: jnp.pad(w, ((0, 0), (D_NOPE, LANE - D_NOPE - D_ROPE)))
    wa = jnp.concatenate([w1[:, CD_C_IN:o_kv], lane_pad(w_pe), lane_pad(_rot_partner_cols(w_pe))],
                         axis=1).astype(BF16)
    wq3 = cd_w_q_b[0].reshape(D_Q_LORA, D_HEADS, D_NOPE + D_ROPE)
    zpad = jnp.zeros((D_Q_LORA, D_HEADS, LANE - D_NOPE - D_ROPE), F32)
    wq_main = jnp.concatenate([wq3, zpad], axis=-1)
    wq_rot = jnp.concatenate([jnp.zeros_like(wq3[..., :D_NOPE]), _rot_partner_cols(wq3[..., D_NOPE:]), zpad],
                             axis=-1)
    wq = jnp.concatenate([wq_main.reshape(D_Q_LORA, -1), wq_rot.reshape(D_Q_LORA, -1)], axis=1).astype(BF16)
    wkv3 = cd_w_kv_b[0].reshape(D_KV_LORA, D_HEADS, D_NOPE + D_V)
    wk = jnp.pad(wkv3[..., :D_NOPE], ((0, 0), (0, 0), (0, LANE - D_NOPE))).reshape(D_KV_LORA, -1)
    wv = wkv3[..., D_NOPE:].reshape(D_KV_LORA, -1)
    wkv = jnp.concatenate([wk, wv], axis=1).astype(BF16)
    cos_l, sin_l = rope_tables(S)
    qd, kd, vd = prep_d(h, attn_norm[1], wa, cd_q_norm[0], cd_kv_norm[0], wq, wkv, cos_l, sin_l, S)

    qdt = _to_tiles_t(qd.reshape(B, S, D_HEADS, LANE), T)
    vd4 = vd.reshape(B, S, D_HEADS, D_V)
    vd_ext = jnp.concatenate([vd4, jnp.ones((B, S, D_HEADS, 1), BF16),
                              jnp.zeros((B, S, D_HEADS, V_ROWS_D - D_V - 1), BF16)], axis=-1)
    vdt = _to_tiles_t(vd_ext, T)
    od = attn_d(qdt, kd.reshape(B, S, D_HEADS * LANE), vdt)

    wo1 = cd_w_o[0]
    wo_c = wo1[:C_HEADS_PER_GROUP * HEAD_DIM].astype(BF16)
    wo_d = wo1[C_HEADS_PER_GROUP * HEAD_DIM:].astype(BF16)
    h = _row_call(_out_proj_cd_kernel, [h, *oc, *lses, od.reshape(M, -1)], [wo_c, wo_d], "out_proj1")
    h = ffn(h, ffn_norm[1], ffn_w_gate[1].astype(BF16), ffn_w_up[1].astype(BF16),
            ffn_w_down[1].astype(BF16), final_norm, True, "ffn1")
    return h.reshape(B, S, D_MODEL)
```

```python
import functools
import math

import numpy as np
import jax
import jax.numpy as jnp
from jax import lax
from jax.experimental import pallas as pl
from jax.experimental.pallas import tpu as pltpu

F32 = jnp.float32
BF16 = jnp.bfloat16

D_MODEL = 1024
HEAD_DIM = 64
EPS = 1e-6
NEG = -1e30
LOG2E = math.log2(math.e)
LN2 = math.log(2.0)

A_HEADS = 4
A_QK_DIM = 2 * HEAD_DIM
A_V_DIM = 2 * HEAD_DIM
B_HEADS = 8
B_KV_HEADS = 2
B_HALF_WINDOW = 128
C_PATTERNS = ((128, 1), (512, 4), (2048, 16))
C_HEADS_PER_GROUP = 4
C_HEADS = C_HEADS_PER_GROUP * len(C_PATTERNS)
D_HEADS = 12
D_Q_LORA = 384
D_KV_LORA = 256
D_NOPE = 64
D_ROPE = 32
D_V = 64
ROPE_THETA = 10000.0
NUM_BUCKETS = 32
MAX_DISTANCE = 1024
D_FF = 2816
AB_IN = 2304
CD_C_IN = 3 * C_HEADS * HEAD_DIM

LANE = 128
VMEM_LIMIT = 48 * 1024 * 1024

ROW_TILE = 512
FFN_ROW_TILE = 1024
FFN_COL_TILE = 256
DENSE_TILE = 512
QCOLS = 256
UNROLL = 4
BAND_GROUP = 4
BIAS_REACH = 3
V_ROWS_D = 80
V_ROWS_A = 144


def _bucket_thresholds():
    nb = NUM_BUCKETS // 2
    max_exact = nb // 2
    n = np.arange(1, 4 * MAX_DISTANCE)
    large = max_exact + (np.log(n.astype(np.float32) / np.float32(max_exact))
                         / np.float32(math.log(MAX_DISTANCE / max_exact))
                         * np.float32(nb - max_exact)).astype(np.int32)
    mag = np.where(n < max_exact, n, np.minimum(large, nb - 1))
    return tuple(int(n[np.argmax(mag >= k)]) for k in range(1, nb))


BUCKET_THRESHOLDS = _bucket_thresholds()
assert BUCKET_THRESHOLDS[-1] <= (BIAS_REACH - 1) * DENSE_TILE + 1


def _params(*sem):
    return pltpu.CompilerParams(dimension_semantics=sem, vmem_limit_bytes=VMEM_LIMIT)


def _rms(x, g):
    return x * lax.rsqrt(jnp.mean(x * x, axis=-1, keepdims=True) + EPS) * g


def _norm_proj_kernel(x_ref, g_ref, w_ref, cs_ref, o_ref):
    xn = _rms(x_ref[...], g_ref[...]).astype(BF16)
    y = jnp.dot(xn, w_ref[...], preferred_element_type=F32)
    o_ref[...] = (y * cs_ref[...]).astype(o_ref.dtype)


def norm_proj(x, g, w, colscale, name):
    M, K = x.shape
    N = w.shape[1]
    tm = ROW_TILE
    return pl.pallas_call(
        _norm_proj_kernel,
        grid=(M // tm,),
        in_specs=[pl.BlockSpec((tm, K), lambda i: (i, 0)),
                  pl.BlockSpec((1, K), lambda i: (0, 0)),
                  pl.BlockSpec((K, N), lambda i: (0, 0)),
                  pl.BlockSpec((1, N), lambda i: (0, 0))],
        out_specs=pl.BlockSpec((tm, N), lambda i: (i, 0)),
        out_shape=jax.ShapeDtypeStruct((M, N), BF16),
        compiler_params=_params("parallel"),
        name=name,
    )(x, g.reshape(1, K), w, colscale.reshape(1, N))


def _ones_row_block(rows, cols):
    r = lax.broadcasted_iota(jnp.int32, (rows, cols), 0)
    return jnp.where(r == 0, 1.0, 0.0).astype(BF16)


def _proj0_kernel(x_ref, g_ref, w_ref, cs_ref, q_ref, v_ref, rest_ref):
    xn = _rms(x_ref[...], g_ref[...]).astype(BF16)
    y = jnp.dot(xn, w_ref[...], preferred_element_type=F32) * cs_ref[...]
    tm = y.shape[0]
    zeros = jnp.zeros((HEAD_DIM, tm), BF16)
    v0 = 2 * A_HEADS * A_QK_DIM
    for h in range(A_HEADS):
        qt = y[:, h * A_QK_DIM:(h + 1) * A_QK_DIM].T.astype(BF16)
        q_ref[0, h, 0, 0, :HEAD_DIM] = qt[:HEAD_DIM]
        q_ref[0, h, 0, 0, HEAD_DIM:] = zeros
        q_ref[0, h, 1, 0, :HEAD_DIM] = zeros
        q_ref[0, h, 1, 0, HEAD_DIM:] = qt[HEAD_DIM:]
        v_ref[0, h, 0, :A_V_DIM] = y[:, v0 + h * A_V_DIM:v0 + (h + 1) * A_V_DIM].T.astype(BF16)
        v_ref[0, h, 0, A_V_DIM:] = _ones_row_block(V_ROWS_A - A_V_DIM, tm)
    ka0, b0 = A_HEADS * A_QK_DIM, A_HEADS * (2 * A_QK_DIM + A_V_DIM)
    rest_ref[:, :ka0] = y[:, ka0:v0].astype(BF16)
    rest_ref[:, ka0:] = y[:, b0:].astype(BF16)


def proj0_call(x, g, w, colscale, batch, seq_len):
    M, K = x.shape
    N = w.shape[1]
    T = DENSE_TILE
    nrow = seq_len // T
    n_rest = N - A_HEADS * (A_QK_DIM + A_V_DIM)
    return pl.pallas_call(
        _proj0_kernel,
        grid=(M // T,),
        in_specs=[pl.BlockSpec((T, K), lambda i: (i, 0)),
                  pl.BlockSpec((1, K), lambda i: (0, 0)),
                  pl.BlockSpec((K, N), lambda i: (0, 0)),
                  pl.BlockSpec((1, N), lambda i: (0, 0))],
        out_specs=[pl.BlockSpec((1, A_HEADS, 2, 1, LANE, T), lambda i: (i // nrow, 0, 0, i % nrow, 0, 0)),
                   pl.BlockSpec((1, A_HEADS, 1, V_ROWS_A, T), lambda i: (i // nrow, 0, i % nrow, 0, 0)),
                   pl.BlockSpec((T, n_rest), lambda i: (i, 0))],
        out_shape=[jax.ShapeDtypeStruct((batch, A_HEADS, 2, nrow, LANE, T), BF16),
                   jax.ShapeDtypeStruct((batch, A_HEADS, nrow, V_ROWS_A, T), BF16),
                   jax.ShapeDtypeStruct((M, n_rest), BF16)],
        compiler_params=_params("parallel"),
        name="proj0",
    )(x, g.reshape(1, K), w, colscale.reshape(1, N))


def _bias_kernel(tab_ref, o_ref, *, off0, off_step, row_coef, col_coef, dil, half_window, head0):
    v = pl.program_id(0)
    hcol = head0 + pl.program_id(1)
    R, C = o_ref.shape[-2:]
    row = lax.broadcasted_iota(jnp.int32, (R, C), 0)
    col = lax.broadcasted_iota(jnp.int32, (R, C), 1)
    rel = off0 + v * off_step + row_coef * row + col_coef * col
    dist = rel * dil
    n = jnp.abs(dist)
    nb = NUM_BUCKETS // 2
    vneg = jnp.full((R, C), tab_ref[0, hcol], F32)
    vpos = jnp.full((R, C), tab_ref[nb, hcol], F32)
    for k, thr in enumerate(BUCKET_THRESHOLDS, start=1):
        ge = n >= thr
        vneg = jnp.where(ge, tab_ref[k, hcol], vneg)
        vpos = jnp.where(ge, tab_ref[nb + k, hcol], vpos)
    val = jnp.where(dist > 0, vpos, vneg) * LOG2E
    if half_window is not None:
        val = jnp.where(jnp.abs(rel) <= half_window, val, NEG)
    o_ref[0, 0] = val


def bias_tiles(table, *, nvar, nheads, head0, rows, cols, off0, off_step, row_coef, col_coef,
               dil, half_window, name):
    kern = functools.partial(_bias_kernel, off0=off0, off_step=off_step, row_coef=row_coef,
                             col_coef=col_coef, dil=dil, half_window=half_window, head0=head0)
    return pl.pallas_call(
        kern,
        grid=(nvar, nheads),
        in_specs=[pl.BlockSpec(memory_space=pltpu.SMEM)],
        out_specs=pl.BlockSpec((1, 1, rows, cols), lambda v, h: (v, h, 0, 0)),
        out_shape=jax.ShapeDtypeStruct((nvar, nheads, rows, cols), F32),
        compiler_params=_params("parallel", "parallel"),
        name=name,
    )(table)


def _dense_pipeline(nk, tile, n_streams, score_fn, value_fn, m_s, acc_s, bufs):
    units = [(i, c * QCOLS) for i in range(n_streams) for c in range(tile // QCOLS)]

    def stage(kc, cur, nxt):
        for i, c0 in units:
            cols = slice(c0, c0 + QCOLS)
            if nxt is not None:
                s = score_fn(kc + 1, i, cols)
                nxt[0][i, :, cols] = s
                nxt[1][i, :, cols] = jnp.max(s, axis=0, keepdims=True)
            if cur is not None:
                m_old = m_s[i, :, cols]
                m_new = jnp.maximum(m_old, cur[1][i, :, cols])
                alpha = jnp.exp2(m_old - m_new)
                p = jnp.exp2((cur[0][i, :, cols] - m_new).astype(BF16))
                acc_s[i, :, cols] = (alpha * acc_s[i, :, cols]
                                     + jnp.dot(value_fn(kc, i), p, preferred_element_type=F32))
                m_s[i, :, cols] = m_new

    m_s[...] = jnp.full(m_s.shape, NEG, F32)
    acc_s[...] = jnp.zeros(acc_s.shape, F32)
    stage(-1, None, bufs[0])
    n_loop = (nk - 1) // UNROLL

    def body(j, carry):
        for u in range(UNROLL):
            stage(UNROLL * j + u, bufs[u % 2], bufs[(u + 1) % 2])
        return carry

    lax.fori_loop(0, n_loop, body, 0)
    for kc in range(n_loop * UNROLL, nk):
        stage(kc, bufs[kc % 2], bufs[(kc + 1) % 2] if kc < nk - 1 else None)


def _attn_a_kernel(q_ref, k_ref, v_ref, bias_ref, lq1_ref, lk1_ref, lq2_ref, lk2_ref, subln_ref, o_ref,
                   m_s, acc_s, s_a, s_b, cm_a, cm_b, *, tile, nk, lambda_init):
    qi = pl.program_id(2)

    def score_fn(kc, j, cols):
        kblk = k_ref[0, pl.ds(pl.multiple_of(kc * tile, tile), tile), :]
        bt = bias_ref[jnp.clip(kc - qi, -BIAS_REACH, BIAS_REACH) + BIAS_REACH, 0, :, cols]
        return jnp.dot(kblk, q_ref[0, 0, j, 0, :, cols], preferred_element_type=F32) + bt

    def value_fn(kc, j):
        return v_ref[0, 0, kc]

    _dense_pipeline(nk, tile, 2, score_fn, value_fn, m_s, acc_s, ((s_a, cm_a), (s_b, cm_b)))

    lam =(jnp.exp(jnp.sum(lq1_ref[...] * lk1_ref[...], axis=-1, keepdims=True))
           - jnp.exp(jnp.sum(lq2_ref[...] * lk2_ref[...], axis=-1, keepdims=True)) + lambda_init)
    a1, a2 = acc_s[0], acc_s[1]
    o = (a1[:A_V_DIM] / a1[A_V_DIM:A_V_DIM + 1]
         - lam * (a2[:A_V_DIM] / a2[A_V_DIM:A_V_DIM + 1]))
    ms = jnp.mean(o * o, axis=0, keepdims=True)
    y = o * lax.rsqrt(ms + EPS) * subln_ref[...] * (1.0 - lambda_init)
    o_ref[0] = y.T.astype(o_ref.dtype)


def attn_a(qt, karr, vt, bias, lq1, lk1, lq2, lk2, subln, lambda_init):
    B, S, _ = karr.shape
    T = DENSE_TILE
    nq = nk = S // T
    assert nk % 2 == 0 and nk >= 4
    kern = functools.partial(_attn_a_kernel, tile=T, nk=nk, lambda_init=lambda_init)
    vec = lambda n: pl.BlockSpec((1, n), lambda b, h, i: (0, 0))
    return pl.pallas_call(
        kern,
        grid=(B, A_HEADS, nq),
        in_specs=[pl.BlockSpec((1, 1, 2, 1, LANE, T), lambda b, h, i: (b, h, 0, i, 0, 0)),
                  pl.BlockSpec((1, S, LANE), lambda b, h, i: (b, 0, h)),
                  pl.BlockSpec((1, 1, nk, V_ROWS_A, T), lambda b, h, i: (b, h, 0, 0, 0)),
                  pl.BlockSpec((2 * BIAS_REACH + 1, 1, T, T), lambda b, h, i: (0, h, 0, 0)),
                  vec(HEAD_DIM), vec(HEAD_DIM), vec(HEAD_DIM), vec(HEAD_DIM),
                  pl.BlockSpec((A_V_DIM, 1), lambda b, h, i: (0, 0))],
        out_specs=pl.BlockSpec((1, T, LANE), lambda b, h, i: (b, i, h)),
        out_shape=jax.ShapeDtypeStruct((B, S, A_HEADS * A_V_DIM), BF16),
        scratch_shapes=[pltpu.VMEM((2, 1, T), F32), pltpu.VMEM((2, V_ROWS_A, T), F32),
                        pltpu.VMEM((2, T, T), F32), pltpu.VMEM((2, T, T), F32),
                        pltpu.VMEM((2, 1, T), F32), pltpu.VMEM((2, 1, T), F32)],
        compiler_params=_params("parallel", "parallel", "arbitrary"),
        name="attn_a",
    )(qt, karr, vt, bias, lq1.reshape(1, -1), lk1.reshape(1, -1), lq2.reshape(1, -1),
      lk2.reshape(1, -1), subln.reshape(-1, 1))


def _attn_d_kernel(q_ref, k_ref, v_ref, o_ref, m_s, acc_s, s_a, s_b, cm_a, cm_b, *, tile, nk):
    def score_fn(kc, hh, cols):
        kblk = k_ref[0, pl.ds(pl.multiple_of(kc * tile, tile), tile), hh * LANE:(hh + 1) * LANE]
        return jnp.dot(kblk, q_ref[0, hh, 0, :, cols], preferred_element_type=F32)

    def value_fn(kc, hh):
        return v_ref[0, hh, kc]

    _dense_pipeline(nk, tile, 2, score_fn, value_fn, m_s, acc_s, ((s_a, cm_a), (s_b, cm_b)))
    outs = []
    for hh in range(2):
        acc = acc_s[hh]
        outs.append(acc[:D_V] / acc[D_V:D_V + 1])
    o_ref[0] = jnp.concatenate(outs, axis=0).T.astype(o_ref.dtype)


def attn_d(qt, k, vt):
    B, S, _ = k.shape
    T = DENSE_TILE
    nq = nk = S // T
    kern = functools.partial(_attn_d_kernel, tile=T, nk=nk)
    return pl.pallas_call(
        kern,
        grid=(B, D_HEADS // 2, nq),
        in_specs=[pl.BlockSpec((1, 2, 1, LANE, T), lambda b, h, i: (b, h, i, 0, 0)),
                  pl.BlockSpec((1, S, 2 * LANE), lambda b, h, i: (b, 0, h)),
                  pl.BlockSpec((1, 2, nk, V_ROWS_D, T), lambda b, h, i: (b, h, 0, 0, 0))],
        out_specs=pl.BlockSpec((1, T, LANE), lambda b, h, i: (b, i, h)),
        out_shape=jax.ShapeDtypeStruct((B, S, D_HEADS * D_V), BF16),
        scratch_shapes=[pltpu.VMEM((2, 1, T), F32), pltpu.VMEM((2, V_ROWS_D, T), F32),
                        pltpu.VMEM((2, T, T), F32), pltpu.VMEM((2, T, T), F32),
                        pltpu.VMEM((2, 1, T), F32), pltpu.VMEM((2, 1, T), F32)],
        compiler_params=_params("parallel", "parallel", "arbitrary"),
        name="attn_d",
    )(qt, k, vt)


def _banded_kernel(*refs, tq, group, span, half_window, seq_len, nqb, nkb, head_of, has_sink, has_lse):
    it = iter(refs)
    q_ref, k_ref, v_ref, bias_ref = next(it), next(it), next(it), next(it)
    sink_ref = next(it) if has_sink else None
    o_ref = next(it)
    lse_ref = next(it) if has_lse else None

    nq = seq_len // tq
    lane = lax.broadcasted_iota(jnp.int32, (tq, LANE), 1)
    low = lane < HEAD_DIM
    for g in range(group):
        t = pl.program_id(2) * group + g
        start = pl.multiple_of(jnp.clip(t * tq - half_window, 0, seq_len - span), half_window)
        variant = jnp.where(t == 0, 0, jnp.where(t == nq - 1, 2, 1))
        kwin = k_ref[0, pl.ds(start, span), :]
        vwin = v_ref[0, pl.ds(start, span), :]
        rows = slice(g * tq, (g + 1) * tq)
        for c in range(nqb):
            q2 = q_ref[0, rows, c * LANE:(c + 1) * LANE]
            kc = c if nkb == nqb else 0
            kb = kwin[:, kc * LANE:(kc + 1) * LANE]
            vb = vwin[:, kc * LANE:(kc + 1) * LANE]
            o_half, lse_half = [], []
            for half in range(2):
                hidx = head_of(c, half)
                qm = jnp.where(low if half == 0 else jnp.logical_not(low), q2, jnp.zeros_like(q2))
                s = lax.dot_general(qm, kb, (((1,), (1,)), ((), ())), preferred_element_type=F32)
                s = s + bias_ref[variant, hidx]
                m = jnp.max(s, axis=-1, keepdims=True)
                if has_sink:
                    sk = sink_ref[:, hidx:hidx + 1] * LOG2E
                    m = jnp.maximum(m, sk)
                e = jnp.exp2(s - m)
                denom = jnp.sum(e, axis=-1, keepdims=True)
                if has_sink:
                    denom = denom + jnp.exp2(sk - m)
                o_half.append(jnp.dot(e.astype(BF16), vb, preferred_element_type=F32) / denom)
                if has_lse:
                    lse_half.append(LN2 * m + jnp.log(denom))
            o_ref[0, rows, c * LANE:(c + 1) * LANE] = jnp.where(low, o_half[0], o_half[1]).astype(o_ref.dtype)
            if has_lse:
                lse_ref[0, rows, c * LANE:(c + 1) * LANE] = jnp.where(low, lse_half[0], lse_half[1])


def banded(proj, bias, sink, *, dil, tq, half_window, nqb, nkb, q_idx, k_idx, v_idx, head_of,
           out_dtype, has_lse, name):
    B, S, N = proj.shape
    L = S // dil
    span = tq + 2 * half_window
    nq = L // tq
    assert L % tq == 0 and L >= span and nq >= 2
    group = math.gcd(nq, BAND_GROUP)
    view = proj.reshape(B, L, dil * N)
    OW = nqb * LANE
    has_sink = sink is not None
    kern = functools.partial(_banded_kernel, tq=tq, group=group, span=span, half_window=half_window,
                             seq_len=L, nqb=nqb, nkb=nkb, head_of=head_of, has_sink=has_sink,
                             has_lse=has_lse)
    in_specs = [pl.BlockSpec((1, group * tq, OW), lambda b, r, t: (b, t, q_idx(r))),
                pl.BlockSpec((1, L, nkb * LANE), lambda b, r, t: (b, 0, k_idx(r))),
                pl.BlockSpec((1, L, nkb * LANE), lambda b, r, t: (b, 0, v_idx(r))),
                pl.BlockSpec(bias.shape, lambda b, r, t: (0, 0, 0, 0))]
    args = [view, view, view, bias]
    if has_sink:
        in_specs.append(pl.BlockSpec((1, sink.shape[-1]), lambda b, r, t: (0, 0)))
        args.append(sink.reshape(1, -1))
    out_spec = pl.BlockSpec((1, group * tq, OW), lambda b, r, t: (b, t, r))
    out_shapes = [jax.ShapeDtypeStruct((B, L, dil * OW), out_dtype)]
    out_specs = [out_spec]
    if has_lse:
        out_shapes.append(jax.ShapeDtypeStruct((B, L, dil * OW), F32))
        out_specs.append(out_spec)
    outs = pl.pallas_call(
        kern,
        grid=(B, dil, nq // group),
        in_specs=in_specs,
        out_specs=out_specs,
        out_shape=out_shapes,
        compiler_params=_params("parallel", "parallel", "arbitrary"),
        name=name,
    )(*args)
    return [o.reshape(B, S, OW) for o in outs]


def _out_proj_ab_kernel(h_ref, a_ref, b_ref, wa_ref, wb_ref, o_ref):
    mix = jnp.dot(a_ref[...], wa_ref[...], preferred_element_type=F32)
    mix = mix + jnp.dot(b_ref[...], wb_ref[...], preferred_element_type=F32)
    o_ref[...] = h_ref[...] + mix


def _out_proj_cd_kernel(h_ref, o0, o1, o2, s0, s1, s2, d_ref, wc_ref, wd_ref, o_ref):
    l0, l1, l2 = s0[...], s1[...], s2[...]
    mx = jnp.maximum(jnp.maximum(l0, l1), l2)
    e0, e1, e2 = jnp.exp(l0 - mx), jnp.exp(l1 - mx), jnp.exp(l2 - mx)
    oc = (e0 * o0[...] + e1 * o1[...] + e2 * o2[...]) / (e0 + e1 + e2)
    mix = jnp.dot(oc.astype(BF16), wc_ref[...], preferred_element_type=F32)
    mix = mix + jnp.dot(d_ref[...], wd_ref[...], preferred_element_type=F32)
    o_ref[...] = h_ref[...] + mix


def _row_call(kern, row_args, full_args, name):
    M = row_args[0].shape[0]
    tm = ROW_TILE
    in_specs = [pl.BlockSpec((tm, a.shape[1]), lambda i: (i, 0)) for a in row_args]
    in_specs += [pl.BlockSpec(a.shape, lambda i: (0, 0)) for a in full_args]
    return pl.pallas_call(
        kern,
        grid=(M // tm,),
        in_specs=in_specs,
        out_specs=pl.BlockSpec((tm, D_MODEL), lambda i: (i, 0)),
        out_shape=jax.ShapeDtypeStruct((M, D_MODEL), F32),
        compiler_params=_params("parallel"),
        name=name,
    )(*row_args, *full_args)


def _ffn_kernel(x_ref, g_ref, wg_ref, wu_ref, wd_ref, fg_ref, o_ref, xn_s, acc_s, *, final_norm):
    f = pl.program_id(1)

    @pl.when(f == 0)
    def _():
        xn_s[...] = _rms(x_ref[...], g_ref[...]).astype(BF16)
        acc_s[...] = jnp.zeros(acc_s.shape, F32)

    xn = xn_s[...]
    gate = jnp.dot(xn, wg_ref[...], preferred_element_type=F32)
    up = jnp.dot(xn, wu_ref[...], preferred_element_type=F32)
    mid = (gate / (1.0 + jnp.exp(-gate)) * up).astype(BF16)
    acc_s[...] += jnp.dot(mid, wd_ref[...], preferred_element_type=F32)

    @pl.when(f == pl.num_programs(1) - 1)
    def _():
        y = x_ref[...] + acc_s[...]
        if final_norm:
            y = _rms(y, fg_ref[...])
        o_ref[...] = y


def ffn(x, g, wg, wu, wd, fg, final_norm, name):
    M, K = x.shape
    tm, tf = FFN_ROW_TILE, FFN_COL_TILE
    kern = functools.partial(_ffn_kernel, final_norm=final_norm)
    return pl.pallas_call(
        kern,
        grid=(M // tm, D_FF // tf),
        in_specs=[pl.BlockSpec((tm, K), lambda i, f: (i, 0)),
                  pl.BlockSpec((1, K), lambda i, f: (0, 0)),
                  pl.BlockSpec((K, tf), lambda i, f: (0, f)),
                  pl.BlockSpec((K, tf), lambda i, f: (0, f)),
                  pl.BlockSpec((tf, K), lambda i, f: (f, 0)),
                  pl.BlockSpec((1, K), lambda i, f: (0, 0))],
        out_specs=pl.BlockSpec((tm, K), lambda i, f: (i, 0)),
        out_shape=jax.ShapeDtypeStruct((M, K), F32),
        scratch_shapes=[pltpu.VMEM((tm, K), BF16), pltpu.VMEM((tm, K), F32)],
        compiler_params=_params("parallel", "arbitrary"),
        name=name,
    )(x, g.reshape(1, K), wg, wu, wd, fg.reshape(1, K))


def _trig_kernel(ang_ref, cos_ref, sin_ref):
    a = ang_ref[...]
    cos_ref[...] = jnp.cos(a)
    sin_ref[...] = jnp.sin(a)


def rope_tables(seq_len):
    half = D_ROPE // 2
    inv = ROPE_THETA ** (-jnp.arange(half, dtype=F32) / half)
    ang = jnp.arange(seq_len).astype(F32)[:, None] * inv[None, :]
    dense = ang.reshape(seq_len * half // LANE, LANE)
    spec = pl.BlockSpec(dense.shape, lambda: (0, 0))
    cos, sin = pl.pallas_call(
        _trig_kernel,
        in_specs=[spec],
        out_specs=[spec, spec],
        out_shape=[jax.ShapeDtypeStruct(dense.shape, F32)] * 2,
        name="rope_trig",
    )(dense)
    cos, sin = cos.reshape(seq_len, half), sin.reshape(seq_len, half)
    pad = LANE - D_NOPE - D_ROPE
    cos_l = jnp.concatenate([jnp.ones((seq_len, D_NOPE), F32), cos, cos, jnp.ones((seq_len, pad), F32)], axis=1)
    sin_l = jnp.concatenate([jnp.zeros((seq_len, D_NOPE), F32), sin, sin, jnp.zeros((seq_len, pad), F32)], axis=1)
    return cos_l, sin_l


def _prep_d_kernel(x_ref, g_ref, wa_ref, qn_ref, kvn_ref, wq_ref, wkv_ref, cos_ref, sin_ref,
                   q_ref, k_ref, v_ref, *, qscale):
    xn = _rms(x_ref[...], g_ref[...]).astype(BF16)
    lat = jnp.dot(xn, wa_ref[...], preferred_element_type=F32)
    cq = _rms(lat[:, :D_Q_LORA], qn_ref[...]).astype(BF16)
    ckv = _rms(lat[:, D_Q_LORA:D_Q_LORA + D_KV_LORA], kvn_ref[...]).astype(BF16)
    o_pe = D_Q_LORA + D_KV_LORA
    cos, sin = cos_ref[...], sin_ref[...]
    kpe = lat[:, o_pe:o_pe + LANE] * cos + lat[:, o_pe + LANE:o_pe + 2 * LANE] * sin
    qq = jnp.dot(cq, wq_ref[...], preferred_element_type=F32)
    kv = jnp.dot(ckv, wkv_ref[...], preferred_element_type=F32)
    kw = D_HEADS * LANE
    ones_blk = _ones_row_block(V_ROWS_D - D_V, x_ref.shape[0])
    for h in range(D_HEADS):
        qh = qq[:, h * LANE:(h + 1) * LANE] * cos + qq[:, kw + h * LANE:kw + (h + 1) * LANE] * sin
        q_ref[0, h, 0] = (qh * qscale).T.astype(BF16)
        k_ref[:, h * LANE:(h + 1) * LANE] = (kv[:, h * LANE:(h + 1) * LANE] + kpe).astype(BF16)
    for j in range(D_HEADS // 2):
        vt = kv[:, kw + j * LANE:kw + (j + 1) * LANE].T.astype(BF16)
        for half in range(2):
            v_ref[0, 2 * j + half, 0, :D_V] = vt[half * D_V:(half + 1) * D_V]
            v_ref[0, 2 * j + half, 0, D_V:] = ones_blk


def prep_d(x, g, wa, qn, kvn, wq, wkv, cos_l, sin_l, batch, seq_len):
    M, K = x.shape
    T = DENSE_TILE
    nrow = seq_len // T
    kw = D_HEADS * LANE
    kern = functools.partial(_prep_d_kernel, qscale=(D_NOPE + D_ROPE) ** -0.5 * LOG2E)
    full = lambda a: pl.BlockSpec(a.shape, lambda i: (0, 0))
    qn2, kvn2, g2 = qn.reshape(1, -1), kvn.reshape(1, -1), g.reshape(1, K)
    return pl.pallas_call(
        kern,
        grid=(M // T,),
        in_specs=[pl.BlockSpec((T, K), lambda i: (i, 0)), full(g2), full(wa), full(qn2), full(kvn2),
                  full(wq), full(wkv),
                  pl.BlockSpec((T, LANE), lambda i: (i % nrow, 0)),
                  pl.BlockSpec((T, LANE), lambda i: (i % nrow, 0))],
        out_specs=[pl.BlockSpec((1, D_HEADS, 1, LANE, T), lambda i: (i // nrow, 0, i % nrow, 0, 0)),
                   pl.BlockSpec((T, kw), lambda i: (i, 0)),
                   pl.BlockSpec((1, D_HEADS, 1, V_ROWS_D, T), lambda i: (i // nrow, 0, i % nrow, 0, 0))],
        out_shape=[jax.ShapeDtypeStruct((batch, D_HEADS, nrow, LANE, T), BF16),
                   jax.ShapeDtypeStruct((M, kw), BF16),
                   jax.ShapeDtypeStruct((batch, D_HEADS, nrow, V_ROWS_D, T), BF16)],
        compiler_params=_params("parallel"),
        name="prep_d",
    )(x, g2, wa, qn2, kvn2, wq, wkv, cos_l, sin_l)


B_HEAD_ORDER = (0, 4, 1, 5, 2, 6, 3, 7)


def _head_cols(order, base):
    return np.concatenate([np.arange(base + h * HEAD_DIM, base + (h + 1) * HEAD_DIM) for h in order])


def _rot_partner_cols(w):
    half = D_ROPE // 2
    return jnp.concatenate([-w[..., half:], w[..., :half]], axis=-1)


def kernel(x, bias_table, attn_norm, ffn_norm, final_norm, ab_w_in, ab_lambda_q1, ab_lambda_k1,
           ab_lambda_q2, ab_lambda_k2, ab_subln, ab_sink, ab_w_o, cd_w_in, cd_q_norm, cd_w_q_b,
           cd_kv_norm, cd_w_kv_b, cd_w_o, ffn_w_gate, ffn_w_up, ffn_w_down):
    B, S, _ = x.shape
    M = B * S
    T = DENSE_TILE
    h = x.reshape(M, D_MODEL)
    qk_scale = HEAD_DIM ** -0.5 * LOG2E

    o3 = A_HEADS * (2 * A_QK_DIM + A_V_DIM)
    cols0 = np.concatenate([np.arange(o3), _head_cols(B_HEAD_ORDER, o3),
                            np.arange(o3 + B_HEADS * HEAD_DIM, AB_IN)])
    w0 = ab_w_in[0][:, cols0].astype(BF16)
    cs0 = np.ones((AB_IN,), np.float32)
    cs0[:A_HEADS * A_QK_DIM] = qk_scale
    cs0[o3:o3 + B_HEADS * HEAD_DIM] = qk_scale
    qat, vat, rest0 = proj0_call(h, attn_norm[0], w0, jnp.asarray(cs0), B, S)
    rest0 = rest0.reshape(B, S, -1)
    bias_a = bias_tiles(bias_table, nvar=2 * BIAS_REACH + 1, nheads=A_HEADS, head0=0, rows=T, cols=T,
                        off0=-BIAS_REACH * T, off_step=T, row_coef=1, col_coef=-1, dil=1,
                        half_window=None, name="bias_a")
    oa = attn_a(qat, rest0, vat, bias_a, ab_lambda_q1[0], ab_lambda_k1[0], ab_lambda_q2[0],
                ab_lambda_k2[0], ab_subln[0], 0.8 - 0.6 * math.exp(-0.3 * 0))

    tq_b = 256
    bias_b = bias_tiles(bias_table, nvar=3, nheads=B_HEADS, head0=A_HEADS, rows=tq_b,
                        cols=tq_b + 2 * B_HALF_WINDOW, off0=0, off_step=-B_HALF_WINDOW, row_coef=-1,
                        col_coef=1, dil=1, half_window=B_HALF_WINDOW, name="bias_b")
    qb0 = (A_HEADS * A_QK_DIM) // LANE
    (ob,) = banded(rest0, bias_b, ab_sink[0], dil=1, tq=tq_b, half_window=B_HALF_WINDOW, nqb=4, nkb=1,
                   q_idx=lambda r: qb0 // 4, k_idx=lambda r: qb0 + 4, v_idx=lambda r: qb0 + 5,
                   head_of=lambda c, half: c + 4 * half, out_dtype=BF16, has_lse=False, name="attn_b")

    wo = ab_w_o[0]
    wo_a = wo[:A_HEADS * A_V_DIM].astype(BF16)
    wo_b = wo[_head_cols(B_HEAD_ORDER, A_HEADS * A_V_DIM)].astype(BF16)
    h = _row_call(_out_proj_ab_kernel, [h, oa.reshape(M, -1), ob.reshape(M, -1)], [wo_a, wo_b], "out_proj0")
    h = ffn(h, ffn_norm[0], ffn_w_gate[0].astype(BF16), ffn_w_up[0].astype(BF16),
            ffn_w_down[0].astype(BF16), final_norm, False, "ffn0")

    w1 = cd_w_in[0]
    cs1 = np.ones((CD_C_IN,), np.float32)
    cs1[:C_HEADS * HEAD_DIM] = qk_scale
    proj1 = norm_proj(h, attn_norm[1], w1[:, :CD_C_IN].astype(BF16), jnp.asarray(cs1),
                      "proj1").reshape(B, S, CD_C_IN)

    nblk = CD_C_IN // LANE
    oc, lses = [], []
    for g, (window, dil) in enumerate(C_PATTERNS):
        hw = window // (2 * dil)
        tq_c = 128
        bias_c = bias_tiles(bias_table, nvar=3, nheads=C_HEADS_PER_GROUP, head0=g * C_HEADS_PER_GROUP,
                            rows=tq_c, cols=tq_c + 2 * hw, off0=0, off_step=-hw, row_coef=-1, col_coef=1,
                            dil=dil, half_window=hw, name=f"bias_c{g}")
        half_blocks = nblk // 2
        o_g, lse_g = banded(proj1, bias_c, None, dil=dil, tq=tq_c, half_window=hw, nqb=2, nkb=2,
                            q_idx=lambda r, g=g: r * half_blocks + g,
                            k_idx=lambda r, g=g: r * half_blocks + 3 + g,
                            v_idx=lambda r, g=g: r * half_blocks + 6 + g,
                            head_of=lambda c, half: 2 * c + half, out_dtype=F32, has_lse=True,
                            name=f"attn_c{g}")
        oc.append(o_g.reshape(M, -1))
        lses.append(lse_g.reshape(M, -1))

    o_q, o_kv = CD_C_IN + D_Q_LORA, CD_C_IN + D_Q_LORA + D_KV_LORA
    w_pe = w1[:, o_kv:]
    lane_pad = lambda w: jnp.pad(w, ((0, 0), (D_NOPE, LANE - D_NOPE - D_ROPE)))
    wa = jnp.concatenate([w1[:, CD_C_IN:o_kv], lane_pad(w_pe), lane_pad(_rot_partner_cols(w_pe))],
                         axis=1).astype(BF16)
    wq3 = cd_w_q_b[0].reshape(D_Q_LORA, D_HEADS, D_NOPE + D_ROPE)
    zpad = jnp.zeros((D_Q_LORA, D_HEADS, LANE - D_NOPE - D_ROPE), F32)
    wq_main = jnp.concatenate([wq3, zpad], axis=-1)
    wq_rot = jnp.concatenate([jnp.zeros_like(wq3[..., :D_NOPE]), _rot_partner_cols(wq3[..., D_NOPE:]), zpad],
                             axis=-1)
    wq = jnp.concatenate([wq_main.reshape(D_Q_LORA, -1), wq_rot.reshape(D_Q_LORA, -1)], axis=1).astype(BF16)
    wkv3 = cd_w_kv_b[0].reshape(D_KV_LORA, D_HEADS, D_NOPE + D_V)
    wk = jnp.pad(wkv3[..., :D_NOPE], ((0, 0), (0, 0), (0, LANE - D_NOPE))).reshape(D_KV_LORA, -1)
    wv = wkv3[..., D_NOPE:].reshape(D_KV_LORA, -1)
    wkv = jnp.concatenate([wk, wv], axis=1).astype(BF16)
    cos_l, sin_l = rope_tables(S)
    qdt, kd, vdt = prep_d(h, attn_norm[1], wa, cd_q_norm[0], cd_kv_norm[0], wq, wkv, cos_l, sin_l, B, S)
    od = attn_d(qdt, kd.reshape(B, S, D_HEADS * LANE), vdt)

    wo1 = cd_w_o[0]
    wo_c = wo1[:C_HEADS_PER_GROUP * HEAD_DIM].astype(BF16)
    wo_d = wo1[C_HEADS_PER_GROUP * HEAD_DIM:].astype(BF16)
    h = _row_call(_out_proj_cd_kernel, [h, *oc, *lses, od.reshape(M, -1)], [wo_c, wo_d], "out_proj1")
    h = ffn(h, ffn_norm[1], ffn_w_gate[1].astype(BF16), ffn_w_up[1].astype(BF16),
            ffn_w_down[1].astype(BF16), final_norm, True, "ffn1")
    return h.reshape(B, S, D_MODEL)
```

```python
import functools
import math

import numpy as np
import jax
import jax.numpy as jnp
from jax import lax
from jax.experimental import pallas as pl
from jax.experimental.pallas import tpu as pltpu

F32 = jnp.float32
BF16 = jnp.bfloat16

D_MODEL = 1024
HEAD_DIM = 64
EPS = 1e-6
NEG = -1e30
LOG2E = math.log2(math.e)
LN2 = math.log(2.0)

A_HEADS = 4
A_QK_DIM = 2 * HEAD_DIM
A_V_DIM = 2 * HEAD_DIM
B_HEADS = 8
B_KV_HEADS = 2
B_HALF_WINDOW = 128
C_PATTERNS = ((128, 1), (512, 4), (2048, 16))
C_HEADS_PER_GROUP = 4
C_HEADS = C_HEADS_PER_GROUP * len(C_PATTERNS)
D_HEADS = 12
D_Q_LORA = 384
D_KV_LORA = 256
D_NOPE = 64
D_ROPE = 32
D_V = 64
ROPE_THETA = 10000.0
NUM_BUCKETS = 32
MAX_DISTANCE = 1024
D_FF = 2816
AB_IN = 2304
CD_C_IN = 3 * C_HEADS * HEAD_DIM

LANE = 128
VMEM_LIMIT = 48 * 1024 * 1024

ROW_TILE = 512
FFN_ROW_TILE = 1024
FFN_COL_TILE = 256
DENSE_TILE = 512
QCOLS = 256
UNROLL = 4
BAND_GROUP = 4
BIAS_REACH = 3
V_ROWS_D = 80
V_ROWS_A = 144


def _bucket_thresholds():
    nb = NUM_BUCKETS // 2
    max_exact = nb // 2
    n = np.arange(1, 4 * MAX_DISTANCE)
    large = max_exact + (np.log(n.astype(np.float32) / np.float32(max_exact))
                         / np.float32(math.log(MAX_DISTANCE / max_exact))
                         * np.float32(nb - max_exact)).astype(np.int32)
    mag = np.where(n < max_exact, n, np.minimum(large, nb - 1))
    return tuple(int(n[np.argmax(mag >= k)]) for k in range(1, nb))


BUCKET_THRESHOLDS = _bucket_thresholds()
assert BUCKET_THRESHOLDS[-1] <= (BIAS_REACH - 1) * DENSE_TILE + 1


def _params(*sem):
    return pltpu.CompilerParams(dimension_semantics=sem, vmem_limit_bytes=VMEM_LIMIT)


def _rms(x, g):
    return x * lax.rsqrt(jnp.mean(x * x, axis=-1, keepdims=True) + EPS) * g


def _ones_row_block(rows, cols):
    r = lax.broadcasted_iota(jnp.int32, (rows, cols), 0)
    return jnp.where(r == 0, 1.0, 0.0).astype(BF16)


def _proj0_kernel(x_ref, g_ref, w_ref, cs_ref, q_ref, v_ref, rest_ref):
    xn = _rms(x_ref[...], g_ref[...]).astype(BF16)
    y = jnp.dot(xn, w_ref[...], preferred_element_type=F32) * cs_ref[...]
    tm = y.shape[0]
    zeros = jnp.zeros((HEAD_DIM, tm), BF16)
    v0 = 2 * A_HEADS * A_QK_DIM
    for h in range(A_HEADS):
        qt = y[:, h * A_QK_DIM:(h + 1) * A_QK_DIM].T.astype(BF16)
        q_ref[0, h, 0, 0, :HEAD_DIM] = qt[:HEAD_DIM]
        q_ref[0, h, 0, 0, HEAD_DIM:] = zeros
        q_ref[0, h, 1, 0, :HEAD_DIM] = zeros
        q_ref[0, h, 1, 0, HEAD_DIM:] = qt[HEAD_DIM:]
        v_ref[0, h, 0, :A_V_DIM] = y[:, v0 + h * A_V_DIM:v0 + (h + 1) * A_V_DIM].T.astype(BF16)
        v_ref[0, h, 0, A_V_DIM:] = _ones_row_block(V_ROWS_A - A_V_DIM, tm)
    ka0, b0 = A_HEADS * A_QK_DIM, A_HEADS * (2 * A_QK_DIM + A_V_DIM)
    rest_ref[:, :ka0] = y[:, ka0:v0].astype(BF16)
    rest_ref[:, ka0:] = y[:, b0:].astype(BF16)


def proj0_call(x, g, w, colscale, batch, seq_len):
    M, K = x.shape
    N = w.shape[1]
    T = DENSE_TILE
    nrow = seq_len // T
    n_rest = N - A_HEADS * (A_QK_DIM + A_V_DIM)
    return pl.pallas_call(
        _proj0_kernel,
        grid=(M // T,),
        in_specs=[pl.BlockSpec((T, K), lambda i: (i, 0)),
                  pl.BlockSpec((1, K), lambda i: (0, 0)),
                  pl.BlockSpec((K, N), lambda i: (0, 0)),
                  pl.BlockSpec((1, N), lambda i: (0, 0))],
        out_specs=[pl.BlockSpec((1, A_HEADS, 2, 1, LANE, T), lambda i: (i // nrow, 0, 0, i % nrow, 0, 0)),
                   pl.BlockSpec((1, A_HEADS, 1, V_ROWS_A, T), lambda i: (i // nrow, 0, i % nrow, 0, 0)),
                   pl.BlockSpec((T, n_rest), lambda i: (i, 0))],
        out_shape=[jax.ShapeDtypeStruct((batch, A_HEADS, 2, nrow, LANE, T), BF16),
                   jax.ShapeDtypeStruct((batch, A_HEADS, nrow, V_ROWS_A, T), BF16),
                   jax.ShapeDtypeStruct((M, n_rest), BF16)],
        compiler_params=_params("parallel"),
        name="proj0",
    )(x, g.reshape(1, K), w, colscale.reshape(1, N))


def _bias_kernel(tab_ref, o_ref, *, off0, off_step, row_coef, col_coef, dil, half_window, head0):
    v = pl.program_id(0)
    hcol = head0 + pl.program_id(1)
    R, C = o_ref.shape[-2:]
    row = lax.broadcasted_iota(jnp.int32, (R, C), 0)
    col = lax.broadcasted_iota(jnp.int32, (R, C), 1)
    rel = off0 + v * off_step + row_coef * row + col_coef * col
    dist = rel * dil
    n = jnp.abs(dist)
    nb = NUM_BUCKETS // 2
    vneg = jnp.full((R, C), tab_ref[0, hcol], F32)
    vpos = jnp.full((R, C), tab_ref[nb, hcol], F32)
    for k, thr in enumerate(BUCKET_THRESHOLDS, start=1):
        ge = n >= thr
        vneg = jnp.where(ge, tab_ref[k, hcol], vneg)
        vpos = jnp.where(ge, tab_ref[nb + k, hcol], vpos)
    val = jnp.where(dist > 0, vpos, vneg) * LOG2E
    if half_window is not None:
        val = jnp.where(jnp.abs(rel) <= half_window, val, NEG)
    o_ref[0, 0] = val


def bias_tiles(table, *, nvar, nheads, head0, rows, cols, off0, off_step, row_coef, col_coef,
               dil, half_window, name):
    kern = functools.partial(_bias_kernel, off0=off0, off_step=off_step, row_coef=row_coef,
                             col_coef=col_coef, dil=dil, half_window=half_window, head0=head0)
    return pl.pallas_call(
        kern,
        grid=(nvar, nheads),
        in_specs=[pl.BlockSpec(memory_space=pltpu.SMEM)],
        out_specs=pl.BlockSpec((1, 1, rows, cols), lambda v, h: (v, h, 0, 0)),
        out_shape=jax.ShapeDtypeStruct((nvar, nheads, rows, cols), F32),
        compiler_params=_params("parallel", "parallel"),
        name=name,
    )(table)


def _dense_pipeline(nk, tile, n_streams, score_fn, value_fn, m_s, acc_s, bufs):
    units = [(i, c * QCOLS) for i in range(n_streams) for c in range(tile // QCOLS)]

    def stage(kc, cur, nxt):
        for i, c0 in units:
            cols = slice(c0, c0 + QCOLS)
            if nxt is not None:
                s = score_fn(kc + 1, i, cols)
                nxt[0][i, :, cols] = s
                nxt[1][i, :, cols] = jnp.max(s, axis=0, keepdims=True)
            if cur is not None:
                m_old = m_s[i, :, cols]
                m_new = jnp.maximum(m_old, cur[1][i, :, cols])
                alpha = jnp.exp2(m_old - m_new)
                p = jnp.exp2((cur[0][i, :, cols] - m_new).astype(BF16))
                acc_s[i, :, cols] = (alpha * acc_s[i, :, cols]
                                     + jnp.dot(value_fn(kc, i), p, preferred_element_type=F32))
                m_s[i, :, cols] = m_new

    m_s[...] = jnp.full(m_s.shape, NEG, F32)
    acc_s[...] = jnp.zeros(acc_s.shape, F32)
    stage(-1, None, bufs[0])
    n_loop = (nk - 1) // UNROLL

    def body(j, carry):
        for u in range(UNROLL):
            stage(UNROLL * j + u, bufs[u % 2], bufs[(u + 1) % 2])
        return carry

    lax.fori_loop(0, n_loop, body, 0)
    for kc in range(n_loop * UNROLL, nk):
        stage(kc, bufs[kc % 2], bufs[(kc + 1) % 2] if kc < nk - 1 else None)


def _attn_a_kernel(q_ref, k_ref, v_ref, bias_ref, lq1_ref, lk1_ref, lq2_ref, lk2_ref, subln_ref, o_ref,
                   m_s, acc_s, s_a, s_b, cm_a, cm_b, *, tile, nk, lambda_init):
    qi = pl.program_id(2)

    def score_fn(kc, j, cols):
        kblk = k_ref[0, pl.ds(pl.multiple_of(kc * tile, tile), tile), :]
        bt = bias_ref[jnp.clip(kc - qi, -BIAS_REACH, BIAS_REACH) + BIAS_REACH, 0, :, cols]
        return jnp.dot(kblk, q_ref[0, 0, j, 0, :, cols], preferred_element_type=F32) + bt

    def value_fn(kc, j):
        return v_ref[0, 0, kc]

    _dense_pipeline(nk, tile, 2, score_fn, value_fn, m_s, acc_s, ((s_a, cm_a), (s_b, cm_b)))

    lam =(jnp.exp(jnp.sum(lq1_ref[...] * lk1_ref[...], axis=-1, keepdims=True))
           - jnp.exp(jnp.sum(lq2_ref[...] * lk2_ref[...], axis=-1, keepdims=True)) + lambda_init)
    a1, a2 = acc_s[0], acc_s[1]
    o = (a1[:A_V_DIM] / a1[A_V_DIM:A_V_DIM + 1]
         - lam * (a2[:A_V_DIM] / a2[A_V_DIM:A_V_DIM + 1]))
    ms = jnp.mean(o * o, axis=0, keepdims=True)
    y = o * lax.rsqrt(ms + EPS) * subln_ref[...] * (1.0 - lambda_init)
    o_ref[0] = y.T.astype(o_ref.dtype)


def attn_a(qt, karr, vt, bias, lq1, lk1, lq2, lk2, subln, lambda_init):
    B, S, _ = karr.shape
    T = DENSE_TILE
    nq = nk = S // T
    assert nk % 2 == 0 and nk >= 4
    kern = functools.partial(_attn_a_kernel, tile=T, nk=nk, lambda_init=lambda_init)
    vec = lambda n: pl.BlockSpec((1, n), lambda b, h, i: (0, 0))
    return pl.pallas_call(
        kern,
        grid=(B, A_HEADS, nq),
        in_specs=[pl.BlockSpec((1, 1, 2, 1, LANE, T), lambda b, h, i: (b, h, 0, i, 0, 0)),
                  pl.BlockSpec((1, S, LANE), lambda b, h, i: (b, 0, h)),
                  pl.BlockSpec((1, 1, nk, V_ROWS_A, T), lambda b, h, i: (b, h, 0, 0, 0)),
                  pl.BlockSpec((2 * BIAS_REACH + 1, 1, T, T), lambda b, h, i: (0, h, 0, 0)),
                  vec(HEAD_DIM), vec(HEAD_DIM), vec(HEAD_DIM), vec(HEAD_DIM),
                  pl.BlockSpec((A_V_DIM, 1), lambda b, h, i: (0, 0))],
        out_specs=pl.BlockSpec((1, T, LANE), lambda b, h, i: (b, i, h)),
        out_shape=jax.ShapeDtypeStruct((B, S, A_HEADS * A_V_DIM), BF16),
        scratch_shapes=[pltpu.VMEM((2, 1, T), F32), pltpu.VMEM((2, V_ROWS_A, T), F32),
                        pltpu.VMEM((2, T, T), F32), pltpu.VMEM((2, T, T), F32),
                        pltpu.VMEM((2, 1, T), F32), pltpu.VMEM((2, 1, T), F32)],
        compiler_params=_params("parallel", "parallel", "arbitrary"),
        name="attn_a",
    )(qt, karr, vt, bias, lq1.reshape(1, -1), lk1.reshape(1, -1), lq2.reshape(1, -1),
      lk2.reshape(1, -1), subln.reshape(-1, 1))


def _attn_d_kernel(q_ref, k_ref, v_ref, o_ref, m_s, acc_s, s_a, s_b, cm_a, cm_b, *, tile, nk):
    def score_fn(kc, hh, cols):
        kblk = k_ref[0, pl.ds(pl.multiple_of(kc * tile, tile), tile), hh * LANE:(hh + 1) * LANE]
        return jnp.dot(kblk, q_ref[0, hh, 0, :, cols], preferred_element_type=F32)

    def value_fn(kc, hh):
        return v_ref[0, hh, kc]

    _dense_pipeline(nk, tile, 2, score_fn, value_fn, m_s, acc_s, ((s_a, cm_a), (s_b, cm_b)))
    outs = []
    for hh in range(2):
        acc = acc_s[hh]
        outs.append(acc[:D_V] / acc[D_V:D_V + 1])
    o_ref[0] = jnp.concatenate(outs, axis=0).T.astype(o_ref.dtype)


def attn_d(qt, k, vt):
    B, S, _ = k.shape
    T = DENSE_TILE
    nq = nk = S // T
    kern = functools.partial(_attn_d_kernel, tile=T, nk=nk)
    return pl.pallas_call(
        kern,
        grid=(B, D_HEADS // 2, nq),
        in_specs=[pl.BlockSpec((1, 2, 1, LANE, T), lambda b, h, i: (b, h, i, 0, 0)),
                  pl.BlockSpec((1, S, 2 * LANE), lambda b, h, i: (b, 0, h)),
                  pl.BlockSpec((1, 2, nk, V_ROWS_D, T), lambda b, h, i: (b, h, 0, 0, 0))],
        out_specs=pl.BlockSpec((1, T, LANE), lambda b, h, i: (b, i, h)),
        out_shape=jax.ShapeDtypeStruct((B, S, D_HEADS * D_V), BF16),
        scratch_shapes=[pltpu.VMEM((2, 1, T), F32), pltpu.VMEM((2, V_ROWS_D, T), F32),
                        pltpu.VMEM((2, T, T), F32), pltpu.VMEM((2, T, T), F32),
                        pltpu.VMEM((2, 1, T), F32), pltpu.VMEM((2, 1, T), F32)],
        compiler_params=_params("parallel", "parallel", "arbitrary"),
        name="attn_d",
    )(qt, k, vt)


def _banded_kernel(*refs, tq, group, span, half_window, seq_len, nqb, nkb, head_of, has_sink, has_lse):
    it = iter(refs)
    q_ref, k_ref, v_ref, bias_ref = next(it), next(it), next(it), next(it)
    sink_ref = next(it) if has_sink else None
    o_ref = next(it)
    lse_ref = next(it) if has_lse else None

    nq = seq_len // tq
    lane = lax.broadcasted_iota(jnp.int32, (tq, LANE), 1)
    low = lane < HEAD_DIM
    for g in range(group):
        t = pl.program_id(2) * group + g
        start = pl.multiple_of(jnp.clip(t * tq - half_window, 0, seq_len - span), half_window)
        variant = jnp.where(t == 0, 0, jnp.where(t == nq - 1, 2, 1))
        kwin = k_ref[0, pl.ds(start, span), :]
        vwin = v_ref[0, pl.ds(start, span), :]
        rows = slice(g * tq, (g + 1) * tq)
        for c in range(nqb):
            q2 = q_ref[0, rows, c * LANE:(c + 1) * LANE]
            kc = c if nkb == nqb else 0
            kb = kwin[:, kc * LANE:(kc + 1) * LANE]
            vb = vwin[:, kc * LANE:(kc + 1) * LANE]
            o_half, lse_half = [], []
            for half in range(2):
                hidx = head_of(c, half)
                qm = jnp.where(low if half == 0 else jnp.logical_not(low), q2, jnp.zeros_like(q2))
                s = lax.dot_general(qm, kb, (((1,), (1,)), ((), ())), preferred_element_type=F32)
                s = s + bias_ref[variant, hidx]
                m = jnp.max(s, axis=-1, keepdims=True)
                if has_sink:
                    sk = sink_ref[:, hidx:hidx + 1] * LOG2E
                    m = jnp.maximum(m, sk)
                e = jnp.exp2(s - m)
                denom = jnp.sum(e, axis=-1, keepdims=True)
                if has_sink:
                    denom = denom + jnp.exp2(sk - m)
                o_half.append(jnp.dot(e.astype(BF16), vb, preferred_element_type=F32) / denom)
                if has_lse:
                    lse_half.append(LN2 * m + jnp.log(denom))
            o_ref[0, rows, c * LANE:(c + 1) * LANE] = jnp.where(low, o_half[0], o_half[1]).astype(o_ref.dtype)
            if has_lse:
                lse_ref[0, rows, c * LANE:(c + 1) * LANE] = jnp.where(low, lse_half[0], lse_half[1])


def banded(view, bias, sink, *, dil, tq, half_window, nqb, nkb, q_idx, k_idx, v_idx, head_of,
           out_dtype, has_lse, name):
    B, L, _ = view.shape
    span = tq + 2 * half_window
    nq = L // tq
    assert L % tq == 0 and L >= span and nq >= 2
    group = math.gcd(nq, BAND_GROUP)
    OW = nqb * LANE
    has_sink = sink is not None
    kern = functools.partial(_banded_kernel, tq=tq, group=group, span=span, half_window=half_window,
                             seq_len=L, nqb=nqb, nkb=nkb, head_of=head_of, has_sink=has_sink,
                             has_lse=has_lse)
    in_specs = [pl.BlockSpec((1, group * tq, OW), lambda b, r, t: (b, t, q_idx(r))),
                pl.BlockSpec((1, L, nkb * LANE), lambda b, r, t: (b, 0, k_idx(r))),
                pl.BlockSpec((1, L, nkb * LANE), lambda b, r, t: (b, 0, v_idx(r))),
                pl.BlockSpec(bias.shape, lambda b, r, t: (0, 0, 0, 0))]
    args = [view, view, view, bias]
    if has_sink:
        in_specs.append(pl.BlockSpec((1, sink.shape[-1]), lambda b, r, t: (0, 0)))
        args.append(sink.reshape(1, -1))
    out_spec = pl.BlockSpec((1, group * tq, OW), lambda b, r, t: (b, t, r))
    out_shapes = [jax.ShapeDtypeStruct((B, L, dil * OW), out_dtype)]
    out_specs = [out_spec]
    if has_lse:
        out_shapes.append(jax.ShapeDtypeStruct((B, L, dil * OW), F32))
        out_specs.append(out_spec)
    outs = pl.pallas_call(
        kern,
        grid=(B, dil, nq // group),
        in_specs=in_specs,
        out_specs=out_specs,
        out_shape=out_shapes,
        compiler_params=_params("parallel", "parallel", "arbitrary"),
        name=name,
    )(*args)
    return outs


def _out_proj_ab_kernel(h_ref, a_ref, b_ref, wa_ref, wb_ref, o_ref):
    mix = jnp.dot(a_ref[...], wa_ref[...], preferred_element_type=F32)
    mix = mix + jnp.dot(b_ref[...], wb_ref[...], preferred_element_type=F32)
    o_ref[...] = h_ref[...] + mix


def _out_proj_cd_kernel(h_ref, o0, o1, o2, s0, s1, s2, d_ref, wc_ref, wd_ref, o_ref, *scratch):
    tm = h_ref.shape[0]
    width = C_HEADS_PER_GROUP * HEAD_DIM
    spare = iter(scratch)

    def in_position_order(ref, dil):
        if dil == 1:
            return ref[0]
        t_s = next(spare)
        for r in range(dil):
            for j in range(width // LANE):
                t_s[j, pl.ds(r, tm // dil, stride=dil), :] = ref[0, :, r * width + j * LANE:r * width + (j + 1) * LANE]
        return jnp.concatenate([t_s[j] for j in range(width // LANE)], axis=1)

    dils = [d for _, d in C_PATTERNS]
    outs = [in_position_order(r, d) for r, d in zip((o0, o1, o2), dils)]
    lses = [in_position_order(r, d) for r, d in zip((s0, s1, s2), dils)]
    mx = jnp.maximum(jnp.maximum(lses[0], lses[1]), lses[2])
    es = [jnp.exp(l - mx) for l in lses]
    oc = (es[0] * outs[0] + es[1] * outs[1] + es[2] * outs[2]) / (es[0] + es[1] + es[2])
    mix = jnp.dot(oc.astype(BF16), wc_ref[...], preferred_element_type=F32)
    mix = mix + jnp.dot(d_ref[...], wd_ref[...], preferred_element_type=F32)
    o_ref[...] = h_ref[...] + mix


def out_proj_cd(h, oc, lses, od, wc, wd, batch, seq_len):
    M = h.shape[0]
    tm = ROW_TILE
    nrow = seq_len // tm
    width = C_HEADS_PER_GROUP * HEAD_DIM
    row = lambda a: pl.BlockSpec((tm, a.shape[1]), lambda i: (i, 0))
    full = lambda a: pl.BlockSpec(a.shape, lambda i: (0, 0))
    views = [pl.BlockSpec((1, tm // d, d * width), lambda i: (i // nrow, i % nrow, 0)) for _, d in C_PATTERNS]
    n_spare = 2 * sum(d > 1 for _, d in C_PATTERNS)
    return pl.pallas_call(
        _out_proj_cd_kernel,
        grid=(M // tm,),
        in_specs=[row(h), *views, *views, row(od), full(wc), full(wd)],
        out_specs=pl.BlockSpec((tm, D_MODEL), lambda i: (i, 0)),
        out_shape=jax.ShapeDtypeStruct((M, D_MODEL), F32),
        scratch_shapes=[pltpu.VMEM((width // LANE, tm, LANE), F32)] * n_spare,
        compiler_params=_params("parallel"),
        name="out_proj1",
    )(h, *oc, *lses, od, wc, wd)


def _proj1_kernel(x_ref, g_ref, w_ref, cs_ref, c0_ref, c1_ref, c2_ref, y_s):
    xn = _rms(x_ref[...], g_ref[...]).astype(BF16)
    y = jnp.dot(xn, w_ref[...], preferred_element_type=F32) * cs_ref[...]
    tm = y.shape[0]
    width = y.shape[1] // len(C_PATTERNS)
    c0_ref[0] = y[:, :width].astype(BF16)
    nblk = width // LANE
    for j in range(y_s.shape[0]):
        y_s[j] = y[:, width + j * LANE:width + (j + 1) * LANE]
    for g, ref in ((1, c1_ref), (2, c2_ref)):
        dil = C_PATTERNS[g][1]
        for r in range(dil):
            for j in range(nblk):
                rows = y_s[(g - 1) * nblk + j, pl.ds(r, tm // dil, stride=dil), :]
                ref[0, :, r * width + j * LANE:r * width + (j + 1) * LANE] = rows.astype(BF16)


def proj1_call(x, g, w, colscale, batch, seq_len):
    M, K = x.shape
    N = w.shape[1]
    tm = ROW_TILE
    nrow = seq_len // tm
    width = N // len(C_PATTERNS)
    assert [d for _, d in C_PATTERNS][0] == 1
    out_specs = [pl.BlockSpec((1, tm // d, d * width), lambda i: (i // nrow, i % nrow, 0)) for _, d in C_PATTERNS]
    out_shape = [jax.ShapeDtypeStruct((batch, seq_len // d, d * width), BF16) for _, d in C_PATTERNS]
    return pl.pallas_call(
        _proj1_kernel,
        grid=(M // tm,),
        in_specs=[pl.BlockSpec((tm, K), lambda i: (i, 0)),
                  pl.BlockSpec((1, K), lambda i: (0, 0)),
                  pl.BlockSpec((K, N), lambda i: (0, 0)),
                  pl.BlockSpec((1, N), lambda i: (0, 0))],
        out_specs=out_specs,
        out_shape=out_shape,
        scratch_shapes=[pltpu.VMEM(((N - width) // LANE, tm, LANE), F32)],
        compiler_params=_params("parallel"),
        name="proj1",
    )(x, g.reshape(1, K), w, colscale.reshape(1, N))


def _row_call(kern, row_args, full_args, name):
    M = row_args[0].shape[0]
    tm = ROW_TILE
    in_specs = [pl.BlockSpec((tm, a.shape[1]), lambda i: (i, 0)) for a in row_args]
    in_specs += [pl.BlockSpec(a.shape, lambda i: (0, 0)) for a in full_args]
    return pl.pallas_call(
        kern,
        grid=(M // tm,),
        in_specs=in_specs,
        out_specs=pl.BlockSpec((tm, D_MODEL), lambda i: (i, 0)),
        out_shape=jax.ShapeDtypeStruct((M, D_MODEL), F32),
        compiler_params=_params("parallel"),
        name=name,
    )(*row_args, *full_args)


def _ffn_kernel(x_ref, g_ref, wg_ref, wu_ref, wd_ref, fg_ref, o_ref, xn_s, acc_s, *, final_norm):
    f = pl.program_id(1)

    @pl.when(f == 0)
    def _():
        xn_s[...] = _rms(x_ref[...], g_ref[...]).astype(BF16)
        acc_s[...] = jnp.zeros(acc_s.shape, F32)

    xn = xn_s[...]
    gate = jnp.dot(xn, wg_ref[...], preferred_element_type=F32)
    up = jnp.dot(xn, wu_ref[...], preferred_element_type=F32)
    mid = (gate / (1.0 + jnp.exp(-gate)) * up).astype(BF16)
    acc_s[...] += jnp.dot(mid, wd_ref[...], preferred_element_type=F32)

    @pl.when(f == pl.num_programs(1) - 1)
    def _():
        y = x_ref[...] + acc_s[...]
        if final_norm:
            y = _rms(y, fg_ref[...])
        o_ref[...] = y


def ffn(x, g, wg, wu, wd, fg, final_norm, name):
    M, K = x.shape
    tm, tf = FFN_ROW_TILE, FFN_COL_TILE
    kern = functools.partial(_ffn_kernel, final_norm=final_norm)
    return pl.pallas_call(
        kern,
        grid=(M // tm, D_FF // tf),
        in_specs=[pl.BlockSpec((tm, K), lambda i, f: (i, 0)),
                  pl.BlockSpec((1, K), lambda i, f: (0, 0)),
                  pl.BlockSpec((K, tf), lambda i, f: (0, f)),
                  pl.BlockSpec((K, tf), lambda i, f: (0, f)),
                  pl.BlockSpec((tf, K), lambda i, f: (f, 0)),
                  pl.BlockSpec((1, K), lambda i, f: (0, 0))],
        out_specs=pl.BlockSpec((tm, K), lambda i, f: (i, 0)),
        out_shape=jax.ShapeDtypeStruct((M, K), F32),
        scratch_shapes=[pltpu.VMEM((tm, K), BF16), pltpu.VMEM((tm, K), F32)],
        compiler_params=_params("parallel", "arbitrary"),
        name=name,
    )(x, g.reshape(1, K), wg, wu, wd, fg.reshape(1, K))


def _trig_kernel(ang_ref, cos_ref, sin_ref):
    a = ang_ref[...]
    cos_ref[...] = jnp.cos(a)
    sin_ref[...] = jnp.sin(a)


def rope_tables(seq_len):
    half = D_ROPE // 2
    inv = ROPE_THETA ** (-jnp.arange(half, dtype=F32) / half)
    ang = jnp.arange(seq_len).astype(F32)[:, None] * inv[None, :]
    dense = ang.reshape(seq_len * half // LANE, LANE)
    spec = pl.BlockSpec(dense.shape, lambda: (0, 0))
    cos, sin = pl.pallas_call(
        _trig_kernel,
        in_specs=[spec],
        out_specs=[spec, spec],
        out_shape=[jax.ShapeDtypeStruct(dense.shape, F32)] * 2,
        name="rope_trig",
    )(dense)
    cos, sin = cos.reshape(seq_len, half), sin.reshape(seq_len, half)
    pad = LANE - D_NOPE - D_ROPE
    cos_l = jnp.concatenate([jnp.ones((seq_len, D_NOPE), F32), cos, cos, jnp.ones((seq_len, pad), F32)], axis=1)
    sin_l = jnp.concatenate([jnp.zeros((seq_len, D_NOPE), F32), sin, sin, jnp.zeros((seq_len, pad), F32)], axis=1)
    return cos_l, sin_l


def _prep_d_kernel(x_ref, g_ref, wa_ref, qn_ref, kvn_ref, wq_ref, wkv_ref, cos_ref, sin_ref,
                   q_ref, k_ref, v_ref, *, qscale):
    xn = _rms(x_ref[...], g_ref[...]).astype(BF16)
    lat = jnp.dot(xn, wa_ref[...], preferred_element_type=F32)
    cq = _rms(lat[:, :D_Q_LORA], qn_ref[...]).astype(BF16)
    ckv = _rms(lat[:, D_Q_LORA:D_Q_LORA + D_KV_LORA], kvn_ref[...]).astype(BF16)
    o_pe = D_Q_LORA + D_KV_LORA
    cos, sin = cos_ref[...], sin_ref[...]
    kpe = lat[:, o_pe:o_pe + LANE] * cos + lat[:, o_pe + LANE:o_pe + 2 * LANE] * sin
    qq = jnp.dot(cq, wq_ref[...], preferred_element_type=F32)
    kv = jnp.dot(ckv, wkv_ref[...], preferred_element_type=F32)
    kw = D_HEADS * LANE
    ones_blk = _ones_row_block(V_ROWS_D - D_V, x_ref.shape[0])
    for h in range(D_HEADS):
        qh = qq[:, h * LANE:(h + 1) * LANE] * cos + qq[:, kw + h * LANE:kw + (h + 1) * LANE] * sin
        q_ref[0, h, 0] = (qh * qscale).T.astype(BF16)
        k_ref[:, h * LANE:(h + 1) * LANE] = (kv[:, h * LANE:(h + 1) * LANE] + kpe).astype(BF16)
    for j in range(D_HEADS // 2):
        vt = kv[:, kw + j * LANE:kw + (j + 1) * LANE].T.astype(BF16)
        for half in range(2):
            v_ref[0, 2 * j + half, 0, :D_V] = vt[half * D_V:(half + 1) * D_V]
            v_ref[0, 2 * j + half, 0, D_V:] = ones_blk


def prep_d(x, g, wa, qn, kvn, wq, wkv, cos_l, sin_l, batch, seq_len):
    M, K = x.shape
    T = DENSE_TILE
    nrow = seq_len // T
    kw = D_HEADS * LANE
    kern = functools.partial(_prep_d_kernel, qscale=(D_NOPE + D_ROPE) ** -0.5 * LOG2E)
    full = lambda a: pl.BlockSpec(a.shape, lambda i: (0, 0))
    qn2, kvn2, g2 = qn.reshape(1, -1), kvn.reshape(1, -1), g.reshape(1, K)
    return pl.pallas_call(
        kern,
        grid=(M // T,),
        in_specs=[pl.BlockSpec((T, K), lambda i: (i, 0)), full(g2), full(wa), full(qn2), full(kvn2),
                  full(wq), full(wkv),
                  pl.BlockSpec((T, LANE), lambda i: (i % nrow, 0)),
                  pl.BlockSpec((T, LANE), lambda i: (i % nrow, 0))],
        out_specs=[pl.BlockSpec((1, D_HEADS, 1, LANE, T), lambda i: (i // nrow, 0, i % nrow, 0, 0)),
                   pl.BlockSpec((T, kw), lambda i: (i, 0)),
                   pl.BlockSpec((1, D_HEADS, 1, V_ROWS_D, T), lambda i: (i // nrow, 0, i % nrow, 0, 0))],
        out_shape=[jax.ShapeDtypeStruct((batch, D_HEADS, nrow, LANE, T), BF16),
                   jax.ShapeDtypeStruct((M, kw), BF16),
                   jax.ShapeDtypeStruct((batch, D_HEADS, nrow, V_ROWS_D, T), BF16)],
        compiler_params=_params("parallel"),
        name="prep_d",
    )(x, g2, wa, qn2, kvn2, wq, wkv, cos_l, sin_l)


B_HEAD_ORDER = (0, 4, 1, 5, 2, 6, 3, 7)


def _head_cols(order, base):
    return np.concatenate([np.arange(base + h * HEAD_DIM, base + (h + 1) * HEAD_DIM) for h in order])


def _rot_partner_cols(w):
    half = D_ROPE // 2
    return jnp.concatenate([-w[..., half:], w[..., :half]], axis=-1)


def kernel(x, bias_table, attn_norm, ffn_norm, final_norm, ab_w_in, ab_lambda_q1, ab_lambda_k1,
           ab_lambda_q2, ab_lambda_k2, ab_subln, ab_sink, ab_w_o, cd_w_in, cd_q_norm, cd_w_q_b,
           cd_kv_norm, cd_w_kv_b, cd_w_o, ffn_w_gate, ffn_w_up, ffn_w_down):
    B, S, _ = x.shape
    M = B * S
    T = DENSE_TILE
    h = x.reshape(M, D_MODEL)
    qk_scale = HEAD_DIM ** -0.5 * LOG2E

    o3 = A_HEADS * (2 * A_QK_DIM + A_V_DIM)
    cols0 = np.concatenate([np.arange(o3), _head_cols(B_HEAD_ORDER, o3),
                            np.arange(o3 + B_HEADS * HEAD_DIM, AB_IN)])
    w0 = ab_w_in[0][:, cols0].astype(BF16)
    cs0 = np.ones((AB_IN,), np.float32)
    cs0[:A_HEADS * A_QK_DIM] = qk_scale
    cs0[o3:o3 + B_HEADS * HEAD_DIM] = qk_scale
    qat, vat, rest0 = proj0_call(h, attn_norm[0], w0, jnp.asarray(cs0), B, S)
    rest0 = rest0.reshape(B, S, -1)
    bias_a = bias_tiles(bias_table, nvar=2 * BIAS_REACH + 1, nheads=A_HEADS, head0=0, rows=T, cols=T,
                        off0=-BIAS_REACH * T, off_step=T, row_coef=1, col_coef=-1, dil=1,
                        half_window=None, name="bias_a")
    oa = attn_a(qat, rest0, vat, bias_a, ab_lambda_q1[0], ab_lambda_k1[0], ab_lambda_q2[0],
                ab_lambda_k2[0], ab_subln[0], 0.8 - 0.6 * math.exp(-0.3 * 0))

    tq_b = 256
    bias_b = bias_tiles(bias_table, nvar=3, nheads=B_HEADS, head0=A_HEADS, rows=tq_b,
                        cols=tq_b + 2 * B_HALF_WINDOW, off0=0, off_step=-B_HALF_WINDOW, row_coef=-1,
                        col_coef=1, dil=1, half_window=B_HALF_WINDOW, name="bias_b")
    qb0 = (A_HEADS * A_QK_DIM) // LANE
    (ob,) = banded(rest0, bias_b, ab_sink[0], dil=1, tq=tq_b, half_window=B_HALF_WINDOW, nqb=4, nkb=1,
                   q_idx=lambda r: qb0 // 4, k_idx=lambda r: qb0 + 4, v_idx=lambda r: qb0 + 5,
                   head_of=lambda c, half: c + 4 * half, out_dtype=BF16, has_lse=False, name="attn_b")

    wo = ab_w_o[0]
    wo_a = wo[:A_HEADS * A_V_DIM].astype(BF16)
    wo_b = wo[_head_cols(B_HEAD_ORDER, A_HEADS * A_V_DIM)].astype(BF16)
    h = _row_call(_out_proj_ab_kernel, [h, oa.reshape(M, -1), ob.reshape(M, -1)], [wo_a, wo_b], "out_proj0")
    h = ffn(h, ffn_norm[0], ffn_w_gate[0].astype(BF16), ffn_w_up[0].astype(BF16),
            ffn_w_down[0].astype(BF16), final_norm, False, "ffn0")

    w1 = cd_w_in[0]
    gw = C_HEADS_PER_GROUP * HEAD_DIM
    cw = C_HEADS * HEAD_DIM
    cols1 = np.concatenate([np.arange(role * cw + g * gw, role * cw + (g + 1) * gw)
                            for g in range(len(C_PATTERNS)) for role in range(3)])
    cs1 = np.ones((CD_C_IN,), np.float32)
    for g in range(len(C_PATTERNS)):
        cs1[3 * g * gw:(3 * g + 1) * gw] = qk_scale
    c_views = proj1_call(h, attn_norm[1], w1[:, cols1].astype(BF16), jnp.asarray(cs1), B, S)

    oc, lses = [], []
    for g, (window, dil) in enumerate(C_PATTERNS):
        hw = window // (2 * dil)
        tq_c = 128
        bias_c = bias_tiles(bias_table, nvar=3, nheads=C_HEADS_PER_GROUP, head0=g * C_HEADS_PER_GROUP,
                            rows=tq_c, cols=tq_c + 2 * hw, off0=0, off_step=-hw, row_coef=-1, col_coef=1,
                            dil=dil, half_window=hw, name=f"bias_c{g}")
        o_g, lse_g = banded(c_views[g], bias_c, None, dil=dil, tq=tq_c, half_window=hw, nqb=2, nkb=2,
                            q_idx=lambda r: 3 * r, k_idx=lambda r: 3 * r + 1, v_idx=lambda r: 3 * r + 2,
                            head_of=lambda c, half: 2 * c + half, out_dtype=F32, has_lse=True,
                            name=f"attn_c{g}")
        oc.append(o_g)
        lses.append(lse_g)

    o_q, o_kv = CD_C_IN + D_Q_LORA, CD_C_IN + D_Q_LORA + D_KV_LORA
    w_pe = w1[:, o_kv:]
    lane_pad = lambda w: jnp.pad(w, ((0, 0), (D_NOPE, LANE - D_NOPE - D_ROPE)))
    wa = jnp.concatenate([w1[:, CD_C_IN:o_kv], lane_pad(w_pe), lane_pad(_rot_partner_cols(w_pe))],
                         axis=1).astype(BF16)
    wq3 = cd_w_q_b[0].reshape(D_Q_LORA, D_HEADS, D_NOPE + D_ROPE)
    zpad = jnp.zeros((D_Q_LORA, D_HEADS, LANE - D_NOPE - D_ROPE), F32)
    wq_main = jnp.concatenate([wq3, zpad], axis=-1)
    wq_rot = jnp.concatenate([jnp.zeros_like(wq3[..., :D_NOPE]), _rot_partner_cols(wq3[..., D_NOPE:]), zpad],
                             axis=-1)
    wq = jnp.concatenate([wq_main.reshape(D_Q_LORA, -1), wq_rot.reshape(D_Q_LORA, -1)], axis=1).astype(BF16)
    wkv3 = cd_w_kv_b[0].reshape(D_KV_LORA, D_HEADS, D_NOPE + D_V)
    wk = jnp.pad(wkv3[..., :D_NOPE], ((0, 0), (0, 0), (0, LANE - D_NOPE))).reshape(D_KV_LORA, -1)
    wv = wkv3[..., D_NOPE:].reshape(D_KV_LORA, -1)
    wkv = jnp.concatenate([wk, wv], axis=1).astype(BF16)
    cos_l, sin_l = rope_tables(S)
    qdt, kd, vdt = prep_d(h, attn_norm[1], wa, cd_q_norm[0], cd_kv_norm[0], wq, wkv, cos_l, sin_l, B, S)
    od = attn_d(qdt, kd.reshape(B, S, D_HEADS * LANE), vdt)

    wo1 = cd_w_o[0]
    wo_c = wo1[:C_HEADS_PER_GROUP * HEAD_DIM].astype(BF16)
    wo_d = wo1[C_HEADS_PER_GROUP * HEAD_DIM:].astype(BF16)
    h = out_proj_cd(h, oc, lses, od.reshape(M, -1), wo_c, wo_d, B, S)
    h = ffn(h, ffn_norm[1], ffn_w_gate[1].astype(BF16), ffn_w_up[1].astype(BF16),
            ffn_w_down[1].astype(BF16), final_norm, True, "ffn1")
    return h.reshape(B, S, D_MODEL)
```

```python
import functools
import math

import numpy as np
import jax
import jax.numpy as jnp
from jax import lax
from jax.experimental import pallas as pl
from jax.experimental.pallas import tpu as pltpu

F32 = jnp.float32
BF16 = jnp.bfloat16

D_MODEL = 1024
HEAD_DIM = 64
EPS = 1e-6
NEG = -1e30
LOG2E = math.log2(math.e)
LN2 = math.log(2.0)

A_HEADS = 4
A_QK_DIM = 2 * HEAD_DIM
A_V_DIM = 2 * HEAD_DIM
B_HEADS = 8
B_KV_HEADS = 2
B_HALF_WINDOW = 128
C_PATTERNS = ((128, 1), (512, 4), (2048, 16))
C_HEADS_PER_GROUP = 4
C_HEADS = C_HEADS_PER_GROUP * len(C_PATTERNS)
D_HEADS = 12
D_Q_LORA = 384
D_KV_LORA = 256
D_NOPE = 64
D_ROPE = 32
D_V = 64
ROPE_THETA = 10000.0
NUM_BUCKETS = 32
MAX_DISTANCE = 1024
D_FF = 2816
AB_IN = 2304
CD_C_IN = 3 * C_HEADS * HEAD_DIM

LANE = 128
VMEM_LIMIT = 48 * 1024 * 1024

ROW_TILE = 512
FFN_ROW_TILE = 1024
FFN_COL_TILE = 256
DENSE_TILE = 512
QCOLS = 256
UNROLL = 4
DENSE_QTILES = 2
BAND_GROUP = 4
BIAS_REACH = 3
V_ROWS_D = 80
V_ROWS_A = 144


def _bucket_thresholds():
    nb = NUM_BUCKETS // 2
    max_exact = nb // 2
    n = np.arange(1, 4 * MAX_DISTANCE)
    large = max_exact + (np.log(n.astype(np.float32) / np.float32(max_exact))
                         / np.float32(math.log(MAX_DISTANCE / max_exact))
                         * np.float32(nb - max_exact)).astype(np.int32)
    mag = np.where(n < max_exact, n, np.minimum(large, nb - 1))
    return tuple(int(n[np.argmax(mag >= k)]) for k in range(1, nb))


BUCKET_THRESHOLDS = _bucket_thresholds()
assert BUCKET_THRESHOLDS[-1] <= (BIAS_REACH - 1) * DENSE_TILE + 1


def _params(*sem):
    return pltpu.CompilerParams(dimension_semantics=sem, vmem_limit_bytes=VMEM_LIMIT)


def _rms(x, g):
    return x * lax.rsqrt(jnp.mean(x * x, axis=-1, keepdims=True) + EPS) * g


def _ones_row_block(rows, cols):
    r = lax.broadcasted_iota(jnp.int32, (rows, cols), 0)
    return jnp.where(r == 0, 1.0, 0.0).astype(BF16)


def _proj0_kernel(x_ref, g_ref, w_ref, cs_ref, q_ref, v_ref, rest_ref):
    xn = _rms(x_ref[...], g_ref[...]).astype(BF16)
    y = jnp.dot(xn, w_ref[...], preferred_element_type=F32) * cs_ref[...]
    tm = y.shape[0]
    zeros = jnp.zeros((HEAD_DIM, tm), BF16)
    v0 = 2 * A_HEADS * A_QK_DIM
    for h in range(A_HEADS):
        qt = y[:, h * A_QK_DIM:(h + 1) * A_QK_DIM].T.astype(BF16)
        q_ref[0, h, 0, 0, :HEAD_DIM] = qt[:HEAD_DIM]
        q_ref[0, h, 0, 0, HEAD_DIM:] = zeros
        q_ref[0, h, 1, 0, :HEAD_DIM] = zeros
        q_ref[0, h, 1, 0, HEAD_DIM:] = qt[HEAD_DIM:]
        v_ref[0, h, 0, :A_V_DIM] = y[:, v0 + h * A_V_DIM:v0 + (h + 1) * A_V_DIM].T.astype(BF16)
        v_ref[0, h, 0, A_V_DIM:] = _ones_row_block(V_ROWS_A - A_V_DIM, tm)
    ka0, b0 = A_HEADS * A_QK_DIM, A_HEADS * (2 * A_QK_DIM + A_V_DIM)
    rest_ref[:, :ka0] = y[:, ka0:v0].astype(BF16)
    rest_ref[:, ka0:] = y[:, b0:].astype(BF16)


def proj0_call(x, g, w, colscale, batch, seq_len):
    M, K = x.shape
    N = w.shape[1]
    T = DENSE_TILE
    nrow = seq_len // T
    n_rest = N - A_HEADS * (A_QK_DIM + A_V_DIM)
    return pl.pallas_call(
        _proj0_kernel,
        grid=(M // T,),
        in_specs=[pl.BlockSpec((T, K), lambda i: (i, 0)),
                  pl.BlockSpec((1, K), lambda i: (0, 0)),
                  pl.BlockSpec((K, N), lambda i: (0, 0)),
                  pl.BlockSpec((1, N), lambda i: (0, 0))],
        out_specs=[pl.BlockSpec((1, A_HEADS, 2, 1, LANE, T), lambda i: (i // nrow, 0, 0, i % nrow, 0, 0)),
                   pl.BlockSpec((1, A_HEADS, 1, V_ROWS_A, T), lambda i: (i // nrow, 0, i % nrow, 0, 0)),
                   pl.BlockSpec((T, n_rest), lambda i: (i, 0))],
        out_shape=[jax.ShapeDtypeStruct((batch, A_HEADS, 2, nrow, LANE, T), BF16),
                   jax.ShapeDtypeStruct((batch, A_HEADS, nrow, V_ROWS_A, T), BF16),
                   jax.ShapeDtypeStruct((M, n_rest), BF16)],
        compiler_params=_params("parallel"),
        name="proj0",
    )(x, g.reshape(1, K), w, colscale.reshape(1, N))


def _bias_kernel(tab_ref, o_ref, *, off0, off_step, row_coef, col_coef, dil, half_window, head0):
    v = pl.program_id(0)
    hcol = head0 + pl.program_id(1)
    R, C = o_ref.shape[-2:]
    row = lax.broadcasted_iota(jnp.int32, (R, C), 0)
    col = lax.broadcasted_iota(jnp.int32, (R, C), 1)
    rel = off0 + v * off_step + row_coef * row + col_coef * col
    dist = rel * dil
    n = jnp.abs(dist)
    nb = NUM_BUCKETS // 2
    vneg = jnp.full((R, C), tab_ref[0, hcol], F32)
    vpos = jnp.full((R, C), tab_ref[nb, hcol], F32)
    for k, thr in enumerate(BUCKET_THRESHOLDS, start=1):
        ge = n >= thr
        vneg = jnp.where(ge, tab_ref[k, hcol], vneg)
        vpos = jnp.where(ge, tab_ref[nb + k, hcol], vpos)
    val = jnp.where(dist > 0, vpos, vneg) * LOG2E
    if half_window is not None:
        val = jnp.where(jnp.abs(rel) <= half_window, val, NEG)
    o_ref[0, 0] = val


def bias_tiles(table, *, nvar, nheads, head0, rows, cols, off0, off_step, row_coef, col_coef,
               dil, half_window, name):
    kern = functools.partial(_bias_kernel, off0=off0, off_step=off_step, row_coef=row_coef,
                             col_coef=col_coef, dil=dil, half_window=half_window, head0=head0)
    return pl.pallas_call(
        kern,
        grid=(nvar, nheads),
        in_specs=[pl.BlockSpec(memory_space=pltpu.SMEM)],
        out_specs=pl.BlockSpec((1, 1, rows, cols), lambda v, h: (v, h, 0, 0)),
        out_shape=jax.ShapeDtypeStruct((nvar, nheads, rows, cols), F32),
        compiler_params=_params("parallel", "parallel"),
        name=name,
    )(table)


def _dense_pipeline(nk, tile, n_streams, score_fn, value_fn, m_s, acc_s, bufs):
    units = [(i, c * QCOLS) for i in range(n_streams) for c in range(tile // QCOLS)]

    def stage(kc, cur, nxt):
        for i, c0 in units:
            cols = slice(c0, c0 + QCOLS)
            if nxt is not None:
                s = score_fn(kc + 1, i, cols)
                nxt[0][i, :, cols] = s
                nxt[1][i, :, cols] = jnp.max(s, axis=0, keepdims=True)
            if cur is not None:
                m_old = m_s[i, :, cols]
                m_new = jnp.maximum(m_old, cur[1][i, :, cols])
                alpha = jnp.exp2(m_old - m_new)
                p = jnp.exp2((cur[0][i, :, cols] - m_new).astype(BF16))
                acc_s[i, :, cols] = (alpha * acc_s[i, :, cols]
                                     + jnp.dot(value_fn(kc, i), p, preferred_element_type=F32))
                m_s[i, :, cols] = m_new

    m_s[...] = jnp.full(m_s.shape, NEG, F32)
    acc_s[...] = jnp.zeros(acc_s.shape, F32)
    stage(-1, None, bufs[0])
    n_loop = (nk - 1) // UNROLL

    def body(j, carry):
        for u in range(UNROLL):
            stage(UNROLL * j + u, bufs[u % 2], bufs[(u + 1) % 2])
        return carry

    lax.fori_loop(0, n_loop, body, 0)
    for kc in range(n_loop * UNROLL, nk):
        stage(kc, bufs[kc % 2], bufs[(kc + 1) % 2] if kc < nk - 1 else None)


def _attn_a_kernel(q_ref, k_ref, v_ref, bias_ref, lq1_ref, lk1_ref, lq2_ref, lk2_ref, subln_ref, o_ref,
                   m_s, acc_s, s_a, s_b, cm_a, cm_b, *, tile, nk, qtiles, lambda_init):
    qi0 = pl.program_id(2) * qtiles

    def score_fn(kc, i, cols):
        j, qt = divmod(i, qtiles)
        kblk = k_ref[0, pl.ds(pl.multiple_of(kc * tile, tile), tile), :]
        bt = bias_ref[jnp.clip(kc - (qi0 + qt), -BIAS_REACH, BIAS_REACH) + BIAS_REACH, 0, :, cols]
        return jnp.dot(kblk, q_ref[0, 0, j, qt, :, cols], preferred_element_type=F32) + bt

    def value_fn(kc, i):
        return v_ref[0, 0, kc]

    _dense_pipeline(nk, tile, 2 * qtiles, score_fn, value_fn, m_s, acc_s, ((s_a, cm_a), (s_b, cm_b)))

    lam = (jnp.exp(jnp.sum(lq1_ref[...] * lk1_ref[...], axis=-1, keepdims=True))
           - jnp.exp(jnp.sum(lq2_ref[...] * lk2_ref[...], axis=-1, keepdims=True)) + lambda_init)
    for qt in range(qtiles):
        a1, a2 = acc_s[qt], acc_s[qtiles + qt]
        o = (a1[:A_V_DIM] / a1[A_V_DIM:A_V_DIM + 1]
             - lam * (a2[:A_V_DIM] / a2[A_V_DIM:A_V_DIM + 1]))
        ms = jnp.mean(o * o, axis=0, keepdims=True)
        y = o * lax.rsqrt(ms + EPS) * subln_ref[...] * (1.0 - lambda_init)
        o_ref[0, qt * tile:(qt + 1) * tile] = y.T.astype(o_ref.dtype)


def attn_a(qt, karr, vt, bias, lq1, lk1, lq2, lk2, subln, lambda_init):
    B, S, _ = karr.shape
    T = DENSE_TILE
    nq = nk = S // T
    assert nk % 2 == 0 and nk >= 4
    qtiles = DENSE_QTILES
    ns = 2 * qtiles
    kern = functools.partial(_attn_a_kernel, tile=T, nk=nk, qtiles=qtiles, lambda_init=lambda_init)
    vec = lambda n: pl.BlockSpec((1, n), lambda b, h, i: (0, 0))
    return pl.pallas_call(
        kern,
        grid=(B, A_HEADS, nq // qtiles),
        in_specs=[pl.BlockSpec((1, 1, 2, qtiles, LANE, T), lambda b, h, i: (b, h, 0, i, 0, 0)),
                  pl.BlockSpec((1, S, LANE), lambda b, h, i: (b, 0, h)),
                  pl.BlockSpec((1, 1, nk, V_ROWS_A, T), lambda b, h, i: (b, h, 0, 0, 0)),
                  pl.BlockSpec((2 * BIAS_REACH + 1, 1, T, T), lambda b, h, i: (0, h, 0, 0)),
                  vec(HEAD_DIM), vec(HEAD_DIM), vec(HEAD_DIM), vec(HEAD_DIM),
                  pl.BlockSpec((A_V_DIM, 1), lambda b, h, i: (0, 0))],
        out_specs=pl.BlockSpec((1, qtiles * T, LANE), lambda b, h, i: (b, i, h)),
        out_shape=jax.ShapeDtypeStruct((B, S, A_HEADS * A_V_DIM), BF16),
        scratch_shapes=[pltpu.VMEM((ns, 1, T), F32), pltpu.VMEM((ns, V_ROWS_A, T), F32),
                        pltpu.VMEM((ns, T, T), F32), pltpu.VMEM((ns, T, T), F32),
                        pltpu.VMEM((ns, 1, T), F32), pltpu.VMEM((ns, 1, T), F32)],
        compiler_params=_params("parallel", "parallel", "arbitrary"),
        name="attn_a",
    )(qt, karr, vt, bias, lq1.reshape(1, -1), lk1.reshape(1, -1), lq2.reshape(1, -1),
      lk2.reshape(1, -1), subln.reshape(-1, 1))


def _attn_d_kernel(q_ref, k_ref, v_ref, o_ref, m_s, acc_s, s_a, s_b, cm_a, cm_b, *, tile, nk, qtiles):
    def score_fn(kc, i, cols):
        hh, qt = divmod(i, qtiles)
        kblk = k_ref[0, pl.ds(pl.multiple_of(kc * tile, tile), tile), hh * LANE:(hh + 1) * LANE]
        return jnp.dot(kblk, q_ref[0, hh, qt, :, cols], preferred_element_type=F32)

    def value_fn(kc, i):
        return v_ref[0, i // qtiles, kc]

    _dense_pipeline(nk, tile, 2 * qtiles, score_fn, value_fn, m_s, acc_s, ((s_a, cm_a), (s_b, cm_b)))
    for qt in range(qtiles):
        outs = []
        for hh in range(2):
            acc = acc_s[hh * qtiles + qt]
            outs.append(acc[:D_V] / acc[D_V:D_V + 1])
        o_ref[0, qt * tile:(qt + 1) * tile] = jnp.concatenate(outs, axis=0).T.astype(o_ref.dtype)


def attn_d(qt, k, vt):
    B, S, _ = k.shape
    T = DENSE_TILE
    nq = nk = S // T
    qtiles = DENSE_QTILES
    ns = 2 * qtiles
    kern = functools.partial(_attn_d_kernel, tile=T, nk=nk, qtiles=qtiles)
    return pl.pallas_call(
        kern,
        grid=(B, D_HEADS // 2, nq // qtiles),
        in_specs=[pl.BlockSpec((1, 2, qtiles, LANE, T), lambda b, h, i: (b, h, i, 0, 0)),
                  pl.BlockSpec((1, S, 2 * LANE), lambda b, h, i: (b, 0, h)),
                  pl.BlockSpec((1, 2, nk, V_ROWS_D, T), lambda b, h, i: (b, h, 0, 0, 0))],
        out_specs=pl.BlockSpec((1, qtiles * T, LANE), lambda b, h, i: (b, i, h)),
        out_shape=jax.ShapeDtypeStruct((B, S, D_HEADS * D_V), BF16),
        scratch_shapes=[pltpu.VMEM((ns, 1, T), F32), pltpu.VMEM((ns, V_ROWS_D, T), F32),
                        pltpu.VMEM((ns, T, T), F32), pltpu.VMEM((ns, T, T), F32),
                        pltpu.VMEM((ns, 1, T), F32), pltpu.VMEM((ns, 1, T), F32)],
        compiler_params=_params("parallel", "parallel", "arbitrary"),
        name="attn_d",
    )(qt, k, vt)


def _banded_kernel(*refs, tq, group, span, half_window, seq_len, nqb, nkb, head_of, has_sink, has_lse):
    it = iter(refs)
    q_ref, k_ref, v_ref, bias_ref = next(it), next(it), next(it), next(it)
    sink_ref = next(it) if has_sink else None
    o_ref = next(it)
    lse_ref = next(it) if has_lse else None

    nq = seq_len // tq
    lane = lax.broadcasted_iota(jnp.int32, (tq, LANE), 1)
    low = lane < HEAD_DIM
    for g in range(group):
        t = pl.program_id(2) * group + g
        start = pl.multiple_of(jnp.clip(t * tq - half_window, 0, seq_len - span), half_window)
        variant = jnp.where(t == 0, 0, jnp.where(t == nq - 1, 2, 1))
        kwin = k_ref[0, pl.ds(start, span), :]
        vwin = v_ref[0, pl.ds(start, span), :]
        rows = slice(g * tq, (g + 1) * tq)
        for c in range(nqb):
            q2 = q_ref[0, rows, c * LANE:(c + 1) * LANE]
            kc = c if nkb == nqb else 0
            kb = kwin[:, kc * LANE:(kc + 1) * LANE]
            vb = vwin[:, kc * LANE:(kc + 1) * LANE]
            o_half, lse_half = [], []
            for half in range(2):
                hidx = head_of(c, half)
                qm = jnp.where(low if half == 0 else jnp.logical_not(low), q2, jnp.zeros_like(q2))
                s = lax.dot_general(qm, kb, (((1,), (1,)), ((), ())), preferred_element_type=F32)
                s = s + bias_ref[variant, hidx]
                m = jnp.max(s, axis=-1, keepdims=True)
                if has_sink:
                    sk = sink_ref[:, hidx:hidx + 1] * LOG2E
                    m = jnp.maximum(m, sk)
                e = jnp.exp2(s - m)
                denom = jnp.sum(e, axis=-1, keepdims=True)
                if has_sink:
                    denom = denom + jnp.exp2(sk - m)
                o_half.append(jnp.dot(e.astype(BF16), vb, preferred_element_type=F32) / denom)
                if has_lse:
                    lse_half.append(LN2 * m + jnp.log(denom))
            o_ref[0, rows, c * LANE:(c + 1) * LANE] = jnp.where(low, o_half[0], o_half[1]).astype(o_ref.dtype)
            if has_lse:
                lse_ref[0, rows, c * LANE:(c + 1) * LANE] = jnp.where(low, lse_half[0], lse_half[1])


def banded(view, bias, sink, *, dil, tq, half_window, nqb, nkb, q_idx, k_idx, v_idx, head_of,
           out_dtype, has_lse, name):
    B, L, _ = view.shape
    span = tq + 2 * half_window
    nq = L // tq
    assert L % tq == 0 and L >= span and nq >= 2
    group = math.gcd(nq, BAND_GROUP)
    OW = nqb * LANE
    has_sink = sink is not None
    kern = functools.partial(_banded_kernel, tq=tq, group=group, span=span, half_window=half_window,
                             seq_len=L, nqb=nqb, nkb=nkb, head_of=head_of, has_sink=has_sink,
                             has_lse=has_lse)
    in_specs = [pl.BlockSpec((1, group * tq, OW), lambda b, r, t: (b, t, q_idx(r))),
                pl.BlockSpec((1, L, nkb * LANE), lambda b, r, t: (b, 0, k_idx(r))),
                pl.BlockSpec((1, L, nkb * LANE), lambda b, r, t: (b, 0, v_idx(r))),
                pl.BlockSpec(bias.shape, lambda b, r, t: (0, 0, 0, 0))]
    args = [view, view, view, bias]
    if has_sink:
        in_specs.append(pl.BlockSpec((1, sink.shape[-1]), lambda b, r, t: (0, 0)))
        args.append(sink.reshape(1, -1))
    out_spec = pl.BlockSpec((1, group * tq, OW), lambda b, r, t: (b, t, r))
    out_shapes = [jax.ShapeDtypeStruct((B, L, dil * OW), out_dtype)]
    out_specs = [out_spec]
    if has_lse:
        out_shapes.append(jax.ShapeDtypeStruct((B, L, dil * OW), F32))
        out_specs.append(out_spec)
    outs = pl.pallas_call(
        kern,
        grid=(B, dil, nq // group),
        in_specs=in_specs,
        out_specs=out_specs,
        out_shape=out_shapes,
        compiler_params=_params("parallel", "parallel", "arbitrary"),
        name=name,
    )(*args)
    return outs


def _out_proj_ab_kernel(h_ref, a_ref, b_ref, wa_ref, wb_ref, o_ref):
    mix = jnp.dot(a_ref[...], wa_ref[...], preferred_element_type=F32)
    mix = mix + jnp.dot(b_ref[...], wb_ref[...], preferred_element_type=F32)
    o_ref[...] = h_ref[...] + mix


def _out_proj_cd_kernel(h_ref, o0, o1, o2, s0, s1, s2, d_ref, wc_ref, wd_ref, o_ref, *scratch):
    tm = h_ref.shape[0]
    width = C_HEADS_PER_GROUP * HEAD_DIM
    spare = iter(scratch)

    def in_position_order(ref, dil):
        if dil == 1:
            return ref[0]
        t_s = next(spare)
        for r in range(dil):
            for j in range(width // LANE):
                t_s[j, pl.ds(r, tm // dil, stride=dil), :] = ref[0, :, r * width + j * LANE:r * width + (j + 1) * LANE]
        return jnp.concatenate([t_s[j] for j in range(width // LANE)], axis=1)

    dils = [d for _, d in C_PATTERNS]
    outs = [in_position_order(r, d) for r, d in zip((o0, o1, o2), dils)]
    lses = [in_position_order(r, d) for r, d in zip((s0, s1, s2), dils)]
    mx = jnp.maximum(jnp.maximum(lses[0], lses[1]), lses[2])
    es = [jnp.exp(l - mx) for l in lses]
    oc = (es[0] * outs[0] + es[1] * outs[1] + es[2] * outs[2]) / (es[0] + es[1] + es[2])
    mix = jnp.dot(oc.astype(BF16), wc_ref[...], preferred_element_type=F32)
    mix = mix + jnp.dot(d_ref[...], wd_ref[...], preferred_element_type=F32)
    o_ref[...] = h_ref[...] + mix


def out_proj_cd(h, oc, lses, od, wc, wd, batch, seq_len):
    M = h.shape[0]
    tm = ROW_TILE
    nrow = seq_len // tm
    width = C_HEADS_PER_GROUP * HEAD_DIM
    row = lambda a: pl.BlockSpec((tm, a.shape[1]), lambda i: (i, 0))
    full = lambda a: pl.BlockSpec(a.shape, lambda i: (0, 0))
    views = [pl.BlockSpec((1, tm // d, d * width), lambda i: (i // nrow, i % nrow, 0)) for _, d in C_PATTERNS]
    n_spare = 2 * sum(d > 1 for _, d in C_PATTERNS)
    return pl.pallas_call(
        _out_proj_cd_kernel,
        grid=(M // tm,),
        in_specs=[row(h), *views, *views, row(od), full(wc), full(wd)],
        out_specs=pl.BlockSpec((tm, D_MODEL), lambda i: (i, 0)),
        out_shape=jax.ShapeDtypeStruct((M, D_MODEL), F32),
        scratch_shapes=[pltpu.VMEM((width // LANE, tm, LANE), F32)] * n_spare,
        compiler_params=_params("parallel"),
        name="out_proj1",
    )(h, *oc, *lses, od, wc, wd)


def _proj1_kernel(x_ref, g_ref, w_ref, cs_ref, c0_ref, c1_ref, c2_ref, y_s):
    xn = _rms(x_ref[...], g_ref[...]).astype(BF16)
    y = jnp.dot(xn, w_ref[...], preferred_element_type=F32) * cs_ref[...]
    tm = y.shape[0]
    width = y.shape[1] // len(C_PATTERNS)
    c0_ref[0] = y[:, :width].astype(BF16)
    nblk = width // LANE
    for j in range(y_s.shape[0]):
        y_s[j] = y[:, width + j * LANE:width + (j + 1) * LANE]
    for g, ref in ((1, c1_ref), (2, c2_ref)):
        dil = C_PATTERNS[g][1]
        for r in range(dil):
            for j in range(nblk):
                rows = y_s[(g - 1) * nblk + j, pl.ds(r, tm // dil, stride=dil), :]
                ref[0, :, r * width + j * LANE:r * width + (j + 1) * LANE] = rows.astype(BF16)


def proj1_call(x, g, w, colscale, batch, seq_len):
    M, K = x.shape
    N = w.shape[1]
    tm = ROW_TILE
    nrow = seq_len // tm
    width = N // len(C_PATTERNS)
    assert [d for _, d in C_PATTERNS][0] == 1
    out_specs = [pl.BlockSpec((1, tm // d, d * width), lambda i: (i // nrow, i % nrow, 0)) for _, d in C_PATTERNS]
    out_shape = [jax.ShapeDtypeStruct((batch, seq_len // d, d * width), BF16) for _, d in C_PATTERNS]
    return pl.pallas_call(
        _proj1_kernel,
        grid=(M // tm,),
        in_specs=[pl.BlockSpec((tm, K), lambda i: (i, 0)),
                  pl.BlockSpec((1, K), lambda i: (0, 0)),
                  pl.BlockSpec((K, N), lambda i: (0, 0)),
                  pl.BlockSpec((1, N), lambda i: (0, 0))],
        out_specs=out_specs,
        out_shape=out_shape,
        scratch_shapes=[pltpu.VMEM(((N - width) // LANE, tm, LANE), F32)],
        compiler_params=_params("parallel"),
        name="proj1",
    )(x, g.reshape(1, K), w, colscale.reshape(1, N))


def _row_call(kern, row_args, full_args, name):
    M = row_args[0].shape[0]
    tm = ROW_TILE
    in_specs = [pl.BlockSpec((tm, a.shape[1]), lambda i: (i, 0)) for a in row_args]
    in_specs += [pl.BlockSpec(a.shape, lambda i: (0, 0)) for a in full_args]
    return pl.pallas_call(
        kern,
        grid=(M // tm,),
        in_specs=in_specs,
        out_specs=pl.BlockSpec((tm, D_MODEL), lambda i: (i, 0)),
        out_shape=jax.ShapeDtypeStruct((M, D_MODEL), F32),
        compiler_params=_params("parallel"),
        name=name,
    )(*row_args, *full_args)


def _ffn_kernel(x_ref, g_ref, wg_ref, wu_ref, wd_ref, fg_ref, o_ref, xn_s, acc_s, *, final_norm):
    f = pl.program_id(1)

    @pl.when(f == 0)
    def _():
        xn_s[...] = _rms(x_ref[...], g_ref[...]).astype(BF16)
        acc_s[...] = jnp.zeros(acc_s.shape, F32)

    xn = xn_s[...]
    gate = jnp.dot(xn, wg_ref[...], preferred_element_type=F32)
    up = jnp.dot(xn, wu_ref[...], preferred_element_type=F32)
    mid = (gate / (1.0 + jnp.exp(-gate)) * up).astype(BF16)
    acc_s[...] += jnp.dot(mid, wd_ref[...], preferred_element_type=F32)

    @pl.when(f == pl.num_programs(1) - 1)
    def _():
        y = x_ref[...] + acc_s[...]
        if final_norm:
            y = _rms(y, fg_ref[...])
        o_ref[...] = y


def ffn(x, g, wg, wu, wd, fg, final_norm, name):
    M, K = x.shape
    tm, tf = FFN_ROW_TILE, FFN_COL_TILE
    kern = functools.partial(_ffn_kernel, final_norm=final_norm)
    return pl.pallas_call(
        kern,
        grid=(M // tm, D_FF // tf),
        in_specs=[pl.BlockSpec((tm, K), lambda i, f: (i, 0)),
                  pl.BlockSpec((1, K), lambda i, f: (0, 0)),
                  pl.BlockSpec((K, tf), lambda i, f: (0, f)),
                  pl.BlockSpec((K, tf), lambda i, f: (0, f)),
                  pl.BlockSpec((tf, K), lambda i, f: (f, 0)),
                  pl.BlockSpec((1, K), lambda i, f: (0, 0))],
        out_specs=pl.BlockSpec((tm, K), lambda i, f: (i, 0)),
        out_shape=jax.ShapeDtypeStruct((M, K), F32),
        scratch_shapes=[pltpu.VMEM((tm, K), BF16), pltpu.VMEM((tm, K), F32)],
        compiler_params=_params("parallel", "arbitrary"),
        name=name,
    )(x, g.reshape(1, K), wg, wu, wd, fg.reshape(1, K))


def _trig_kernel(ang_ref, cos_ref, sin_ref):
    a = ang_ref[...]
    cos_ref[...] = jnp.cos(a)
    sin_ref[...] = jnp.sin(a)


def rope_tables(seq_len):
    half = D_ROPE // 2
    inv = ROPE_THETA ** (-jnp.arange(half, dtype=F32) / half)
    ang = jnp.arange(seq_len).astype(F32)[:, None] * inv[None, :]
    dense = ang.reshape(seq_len * half // LANE, LANE)
    spec = pl.BlockSpec(dense.shape, lambda: (0, 0))
    cos, sin = pl.pallas_call(
        _trig_kernel,
        in_specs=[spec],
        out_specs=[spec, spec],
        out_shape=[jax.ShapeDtypeStruct(dense.shape, F32)] * 2,
        name="rope_trig",
    )(dense)
    cos, sin = cos.reshape(seq_len, half), sin.reshape(seq_len, half)
    pad = LANE - D_NOPE - D_ROPE
    cos_l = jnp.concatenate([jnp.ones((seq_len, D_NOPE), F32), cos, cos, jnp.ones((seq_len, pad), F32)], axis=1)
    sin_l = jnp.concatenate([jnp.zeros((seq_len, D_NOPE), F32), sin, sin, jnp.zeros((seq_len, pad), F32)], axis=1)
    return cos_l, sin_l


def _prep_d_kernel(x_ref, g_ref, wa_ref, qn_ref, kvn_ref, wq_ref, wkv_ref, cos_ref, sin_ref,
                   q_ref, k_ref, v_ref, *, qscale):
    xn = _rms(x_ref[...], g_ref[...]).astype(BF16)
    lat = jnp.dot(xn, wa_ref[...], preferred_element_type=F32)
    cq = _rms(lat[:, :D_Q_LORA], qn_ref[...]).astype(BF16)
    ckv = _rms(lat[:, D_Q_LORA:D_Q_LORA + D_KV_LORA], kvn_ref[...]).astype(BF16)
    o_pe = D_Q_LORA + D_KV_LORA
    cos, sin = cos_ref[...], sin_ref[...]
    kpe = lat[:, o_pe:o_pe + LANE] * cos + lat[:, o_pe + LANE:o_pe + 2 * LANE] * sin
    qq = jnp.dot(cq, wq_ref[...], preferred_element_type=F32)
    kv = jnp.dot(ckv, wkv_ref[...], preferred_element_type=F32)
    kw = D_HEADS * LANE
    ones_blk = _ones_row_block(V_ROWS_D - D_V, x_ref.shape[0])
    for h in range(D_HEADS):
        qh = qq[:, h * LANE:(h + 1) * LANE] * cos + qq[:, kw + h * LANE:kw + (h + 1) * LANE] * sin
        q_ref[0, h, 0] = (qh * qscale).T.astype(BF16)
        k_ref[:, h * LANE:(h + 1) * LANE] = (kv[:, h * LANE:(h + 1) * LANE] + kpe).astype(BF16)
    for j in range(D_HEADS // 2):
        vt = kv[:, kw + j * LANE:kw + (j + 1) * LANE].T.astype(BF16)
        for half in range(2):
            v_ref[0, 2 * j + half, 0, :D_V] = vt[half * D_V:(half + 1) * D_V]
            v_ref[0, 2 * j + half, 0, D_V:] = ones_blk


def prep_d(x, g, wa, qn, kvn, wq, wkv, cos_l, sin_l, batch, seq_len):
    M, K = x.shape
    T = DENSE_TILE
    nrow = seq_len // T
    kw = D_HEADS * LANE
    kern = functools.partial(_prep_d_kernel, qscale=(D_NOPE + D_ROPE) ** -0.5 * LOG2E)
    full = lambda a: pl.BlockSpec(a.shape, lambda i: (0, 0))
    qn2, kvn2, g2 = qn.reshape(1, -1), kvn.reshape(1, -1), g.reshape(1, K)
    return pl.pallas_call(
        kern,
        grid=(M // T,),
        in_specs=[pl.BlockSpec((T, K), lambda i: (i, 0)), full(g2), full(wa), full(qn2), full(kvn2),
                  full(wq), full(wkv),
                  pl.BlockSpec((T, LANE), lambda i: (i % nrow, 0)),
                  pl.BlockSpec((T, LANE), lambda i: (i % nrow, 0))],
        out_specs=[pl.BlockSpec((1, D_HEADS, 1, LANE, T), lambda i: (i // nrow, 0, i % nrow, 0, 0)),
                   pl.BlockSpec((T, kw), lambda i: (i, 0)),
                   pl.BlockSpec((1, D_HEADS, 1, V_ROWS_D, T), lambda i: (i // nrow, 0, i % nrow, 0, 0))],
        out_shape=[jax.ShapeDtypeStruct((batch, D_HEADS, nrow, LANE, T), BF16),
                   jax.ShapeDtypeStruct((M, kw), BF16),
                   jax.ShapeDtypeStruct((batch, D_HEADS, nrow, V_ROWS_D, T), BF16)],
        compiler_params=_params("parallel"),
        name="prep_d",
    )(x, g2, wa, qn2, kvn2, wq, wkv, cos_l, sin_l)


B_HEAD_ORDER = (0, 4, 1, 5, 2, 6, 3, 7)


def _head_cols(order, base):
    return np.concatenate([np.arange(base + h * HEAD_DIM, base + (h + 1) * HEAD_DIM) for h in order])


def _rot_partner_cols(w):
    half = D_ROPE // 2
    return jnp.concatenate([-w[..., half:], w[..., :half]], axis=-1)


def kernel(x, bias_table, attn_norm, ffn_norm, final_norm, ab_w_in, ab_lambda_q1, ab_lambda_k1,
           ab_lambda_q2, ab_lambda_k2, ab_subln, ab_sink, ab_w_o, cd_w_in, cd_q_norm, cd_w_q_b,
           cd_kv_norm, cd_w_kv_b, cd_w_o, ffn_w_gate, ffn_w_up, ffn_w_down):
    B, S, _ = x.shape
    M = B * S
    T = DENSE_TILE
    h = x.reshape(M, D_MODEL)
    qk_scale = HEAD_DIM ** -0.5 * LOG2E

    o3 = A_HEADS * (2 * A_QK_DIM + A_V_DIM)
    cols0 = np.concatenate([np.arange(o3), _head_cols(B_HEAD_ORDER, o3),
                            np.arange(o3 + B_HEADS * HEAD_DIM, AB_IN)])
    w0 = ab_w_in[0][:, cols0].astype(BF16)
    cs0 = np.ones((AB_IN,), np.float32)
    cs0[:A_HEADS * A_QK_DIM] = qk_scale
    cs0[o3:o3 + B_HEADS * HEAD_DIM] = qk_scale
    qat, vat, rest0 = proj0_call(h, attn_norm[0], w0, jnp.asarray(cs0), B, S)
    rest0 = rest0.reshape(B, S, -1)
    bias_a = bias_tiles(bias_table, nvar=2 * BIAS_REACH + 1, nheads=A_HEADS, head0=0, rows=T, cols=T,
                        off0=-BIAS_REACH * T, off_step=T, row_coef=1, col_coef=-1, dil=1,
                        half_window=None, name="bias_a")
    oa = attn_a(qat, rest0, vat, bias_a, ab_lambda_q1[0], ab_lambda_k1[0], ab_lambda_q2[0],
                ab_lambda_k2[0], ab_subln[0], 0.8 - 0.6 * math.exp(-0.3 * 0))

    tq_b = 256
    bias_b = bias_tiles(bias_table, nvar=3, nheads=B_HEADS, head0=A_HEADS, rows=tq_b,
                        cols=tq_b + 2 * B_HALF_WINDOW, off0=0, off_step=-B_HALF_WINDOW, row_coef=-1,
                        col_coef=1, dil=1, half_window=B_HALF_WINDOW, name="bias_b")
    qb0 = (A_HEADS * A_QK_DIM) // LANE
    (ob,) = banded(rest0, bias_b, ab_sink[0], dil=1, tq=tq_b, half_window=B_HALF_WINDOW, nqb=4, nkb=1,
                   q_idx=lambda r: qb0 // 4, k_idx=lambda r: qb0 + 4, v_idx=lambda r: qb0 + 5,
                   head_of=lambda c, half: c + 4 * half, out_dtype=BF16, has_lse=False, name="attn_b")

    wo = ab_w_o[0]
    wo_a = wo[:A_HEADS * A_V_DIM].astype(BF16)
    wo_b = wo[_head_cols(B_HEAD_ORDER, A_HEADS * A_V_DIM)].astype(BF16)
    h = _row_call(_out_proj_ab_kernel, [h, oa.reshape(M, -1), ob.reshape(M, -1)], [wo_a, wo_b], "out_proj0")
    h = ffn(h, ffn_norm[0], ffn_w_gate[0].astype(BF16), ffn_w_up[0].astype(BF16),
            ffn_w_down[0].astype(BF16), final_norm, False, "ffn0")

    w1 = cd_w_in[0]
    gw = C_HEADS_PER_GROUP * HEAD_DIM
    cw = C_HEADS * HEAD_DIM
    cols1 = np.concatenate([np.arange(role * cw + g * gw, role * cw + (g + 1) * gw)
                            for g in range(len(C_PATTERNS)) for role in range(3)])
    cs1 = np.ones((CD_C_IN,), np.float32)
    for g in range(len(C_PATTERNS)):
        cs1[3 * g * gw:(3 * g + 1) * gw] = qk_scale
    c_views = proj1_call(h, attn_norm[1], w1[:, cols1].astype(BF16), jnp.asarray(cs1), B, S)

    oc, lses = [], []
    for g, (window, dil) in enumerate(C_PATTERNS):
        hw = window // (2 * dil)
        tq_c = 128
        bias_c = bias_tiles(bias_table, nvar=3, nheads=C_HEADS_PER_GROUP, head0=g * C_HEADS_PER_GROUP,
                            rows=tq_c, cols=tq_c + 2 * hw, off0=0, off_step=-hw, row_coef=-1, col_coef=1,
                            dil=dil, half_window=hw, name=f"bias_c{g}")
        o_g, lse_g = banded(c_views[g], bias_c, None, dil=dil, tq=tq_c, half_window=hw, nqb=2, nkb=2,
                            q_idx=lambda r: 3 * r, k_idx=lambda r: 3 * r + 1, v_idx=lambda r: 3 * r + 2,
                            head_of=lambda c, half: 2 * c + half, out_dtype=F32, has_lse=True,
                            name=f"attn_c{g}")
        oc.append(o_g)
        lses.append(lse_g)

    o_q, o_kv = CD_C_IN + D_Q_LORA, CD_C_IN + D_Q_LORA + D_KV_LORA
    w_pe = w1[:, o_kv:]
    lane_pad = lambda w: jnp.pad(w, ((0, 0), (D_NOPE, LANE - D_NOPE - D_ROPE)))
    wa = jnp.concatenate([w1[:, CD_C_IN:o_kv], lane_pad(w_pe), lane_pad(_rot_partner_cols(w_pe))],
                         axis=1).astype(BF16)
    wq3 = cd_w_q_b[0].reshape(D_Q_LORA, D_HEADS, D_NOPE + D_ROPE)
    zpad = jnp.zeros((D_Q_LORA, D_HEADS, LANE - D_NOPE - D_ROPE), F32)
    wq_main = jnp.concatenate([wq3, zpad], axis=-1)
    wq_rot = jnp.concatenate([jnp.zeros_like(wq3[..., :D_NOPE]), _rot_partner_cols(wq3[..., D_NOPE:]), zpad],
                             axis=-1)
    wq = jnp.concatenate([wq_main.reshape(D_Q_LORA, -1), wq_rot.reshape(D_Q_LORA, -1)], axis=1).astype(BF16)
    wkv3 = cd_w_kv_b[0].reshape(D_KV_LORA, D_HEADS, D_NOPE + D_V)
    wk = jnp.pad(wkv3[..., :D_NOPE], ((0, 0), (0, 0), (0, LANE - D_NOPE))).reshape(D_KV_LORA, -1)
    wv = wkv3[..., D_NOPE:].reshape(D_KV_LORA, -1)
    wkv = jnp.concatenate([wk, wv], axis=1).astype(BF16)
    cos_l, sin_l = rope_tables(S)
    qdt, kd, vdt = prep_d(h, attn_norm[1], wa, cd_q_norm[0], cd_kv_norm[0], wq, wkv, cos_l, sin_l, B, S)
    od = attn_d(qdt, kd.reshape(B, S, D_HEADS * LANE), vdt)

    wo1 = cd_w_o[0]
    wo_c = wo1[:C_HEADS_PER_GROUP * HEAD_DIM].astype(BF16)
    wo_d = wo1[C_HEADS_PER_GROUP * HEAD_DIM:].astype(BF16)
    h = out_proj_cd(h, oc, lses, od.reshape(M, -1), wo_c, wo_d, B, S)
    h = ffn(h, ffn_norm[1], ffn_w_gate[1].astype(BF16), ffn_w_up[1].astype(BF16),
            ffn_w_down[1].astype(BF16), final_norm, True, "ffn1")
    return h.reshape(B, S, D_MODEL)
```

```python
import functools
import math

import numpy as np
import jax
import jax.numpy as jnp
from jax import lax
from jax.experimental import pallas as pl
from jax.experimental.pallas import tpu as pltpu

F32 = jnp.float32
BF16 = jnp.bfloat16

D_MODEL = 1024
HEAD_DIM = 64
EPS = 1e-6
NEG = -1e30
LOG2E = math.log2(math.e)
LN2 = math.log(2.0)

A_HEADS = 4
A_QK_DIM = 2 * HEAD_DIM
A_V_DIM = 2 * HEAD_DIM
B_HEADS = 8
B_KV_HEADS = 2
B_HALF_WINDOW = 128
C_PATTERNS = ((128, 1), (512, 4), (2048, 16))
C_HEADS_PER_GROUP = 4
C_HEADS = C_HEADS_PER_GROUP * len(C_PATTERNS)
D_HEADS = 12
D_Q_LORA = 384
D_KV_LORA = 256
D_NOPE = 64
D_ROPE = 32
D_V = 64
ROPE_THETA = 10000.0
NUM_BUCKETS = 32
MAX_DISTANCE = 1024
D_FF = 2816
AB_IN = 2304
CD_C_IN = 3 * C_HEADS * HEAD_DIM

LANE = 128
VMEM_LIMIT = 48 * 1024 * 1024

ROW_TILE = 512
FFN_ROW_TILE = 1024
FFN_COL_TILE = 256
DENSE_TILE = 512
QCOLS = 256
UNROLL = 4
DENSE_QTILES = 2
BAND_GROUP = 4
BIAS_REACH = 3
V_ROWS_D = 80
V_ROWS_A = 144


def _bucket_thresholds():
    nb = NUM_BUCKETS // 2
    max_exact = nb // 2
    n = np.arange(1, 4 * MAX_DISTANCE)
    large = max_exact + (np.log(n.astype(np.float32) / np.float32(max_exact))
                         / np.float32(math.log(MAX_DISTANCE / max_exact))
                         * np.float32(nb - max_exact)).astype(np.int32)
    mag = np.where(n < max_exact, n, np.minimum(large, nb - 1))
    return tuple(int(n[np.argmax(mag >= k)]) for k in range(1, nb))


BUCKET_THRESHOLDS = _bucket_thresholds()
assert BUCKET_THRESHOLDS[-1] <= (BIAS_REACH - 1) * DENSE_TILE + 1


def _params(*sem):
    return pltpu.CompilerParams(dimension_semantics=sem, vmem_limit_bytes=VMEM_LIMIT)


def _rms(x, g):
    return x * lax.rsqrt(jnp.mean(x * x, axis=-1, keepdims=True) + EPS) * g


def _ones_row_block(rows, cols):
    r = lax.broadcasted_iota(jnp.int32, (rows, cols), 0)
    return jnp.where(r == 0, 1.0, 0.0).astype(BF16)


def _store_masked_halves(q_ref, idx, qt):
    zeros = jnp.zeros((HEAD_DIM, qt.shape[1]), BF16)
    q_ref[idx + (0, 0, slice(None, HEAD_DIM))] = qt[:HEAD_DIM]
    q_ref[idx + (0, 0, slice(HEAD_DIM, None))] = zeros
    q_ref[idx + (1, 0, slice(None, HEAD_DIM))] = zeros
    q_ref[idx + (1, 0, slice(HEAD_DIM, None))] = qt[HEAD_DIM:]


def _proj0_kernel(x_ref, g_ref, w_ref, cs_ref, qa_ref, va_ref, qb_ref, vb_ref, k_ref):
    xn = _rms(x_ref[...], g_ref[...]).astype(BF16)
    y = jnp.dot(xn, w_ref[...], preferred_element_type=F32) * cs_ref[...]
    tm = y.shape[0]
    ka0 = A_HEADS * A_QK_DIM
    va0 = 2 * ka0
    qb0 = va0 + A_HEADS * A_V_DIM
    kb0 = qb0 + B_HEADS * HEAD_DIM
    vb0 = kb0 + B_KV_HEADS * HEAD_DIM
    for h in range(A_HEADS):
        _store_masked_halves(qa_ref, (0, h), y[:, h * A_QK_DIM:(h + 1) * A_QK_DIM].T.astype(BF16))
        va_ref[0, h, 0, :A_V_DIM] = y[:, va0 + h * A_V_DIM:va0 + (h + 1) * A_V_DIM].T.astype(BF16)
        va_ref[0, h, 0, A_V_DIM:] = _ones_row_block(V_ROWS_A - A_V_DIM, tm)
    for j in range(B_HEADS // 2):
        _store_masked_halves(qb_ref, (0, j), y[:, qb0 + j * LANE:qb0 + (j + 1) * LANE].T.astype(BF16))
    vbt = y[:, vb0:vb0 + LANE].T.astype(BF16)
    ones_blk = _ones_row_block(V_ROWS_D - HEAD_DIM, LANE)
    for g in range(B_KV_HEADS):
        for c in range(tm // LANE):
            vb_ref[0, g, c, :HEAD_DIM] = vbt[g * HEAD_DIM:(g + 1) * HEAD_DIM, c * LANE:(c + 1) * LANE]
            vb_ref[0, g, c, HEAD_DIM:] = ones_blk
    k_ref[:, :ka0] = y[:, ka0:va0].astype(BF16)
    k_ref[:, ka0:] = y[:, kb0:vb0].astype(BF16)


def proj0_call(x, g, w, colscale, batch, seq_len):
    M, K = x.shape
    N = w.shape[1]
    T = DENSE_TILE
    nrow = seq_len // T
    nkb = A_HEADS * A_QK_DIM + B_KV_HEADS * HEAD_DIM
    qspec = pl.BlockSpec((1, A_HEADS, 2, 1, LANE, T), lambda i: (i // nrow, 0, 0, i % nrow, 0, 0))
    qshape = jax.ShapeDtypeStruct((batch, A_HEADS, 2, nrow, LANE, T), BF16)
    return pl.pallas_call(
        _proj0_kernel,
        grid=(M // T,),
        in_specs=[pl.BlockSpec((T, K), lambda i: (i, 0)),
                  pl.BlockSpec((1, K), lambda i: (0, 0)),
                  pl.BlockSpec((K, N), lambda i: (0, 0)),
                  pl.BlockSpec((1, N), lambda i: (0, 0))],
        out_specs=[qspec,
                   pl.BlockSpec((1, A_HEADS, 1, V_ROWS_A, T), lambda i: (i // nrow, 0, i % nrow, 0, 0)),
                   qspec,
                   pl.BlockSpec((1, B_KV_HEADS, T // LANE, V_ROWS_D, LANE),
                                lambda i: (i // nrow, 0, i % nrow, 0, 0)),
                   pl.BlockSpec((T, nkb), lambda i: (i, 0))],
        out_shape=[qshape,
                   jax.ShapeDtypeStruct((batch, A_HEADS, nrow, V_ROWS_A, T), BF16),
                   qshape,
                   jax.ShapeDtypeStruct((batch, B_KV_HEADS, seq_len // LANE, V_ROWS_D, LANE), BF16),
                   jax.ShapeDtypeStruct((M, nkb), BF16)],
        compiler_params=_params("parallel"),
        name="proj0",
    )(x, g.reshape(1, K), w, colscale.reshape(1, N))


def _bias_kernel(tab_ref, o_ref, *, off0, off_step, row_coef, col_coef, dil, half_window, head0):
    v = pl.program_id(0)
    hcol = head0 + pl.program_id(1)
    R, C = o_ref.shape[-2:]
    row = lax.broadcasted_iota(jnp.int32, (R, C), 0)
    col = lax.broadcasted_iota(jnp.int32, (R, C), 1)
    rel = off0 + v * off_step + row_coef * row + col_coef * col
    dist = rel * dil
    n = jnp.abs(dist)
    nb = NUM_BUCKETS // 2
    vneg = jnp.full((R, C), tab_ref[0, hcol], F32)
    vpos = jnp.full((R, C), tab_ref[nb, hcol], F32)
    for k, thr in enumerate(BUCKET_THRESHOLDS, start=1):
        ge = n >= thr
        vneg = jnp.where(ge, tab_ref[k, hcol], vneg)
        vpos = jnp.where(ge, tab_ref[nb + k, hcol], vpos)
    val = jnp.where(dist > 0, vpos, vneg) * LOG2E
    if half_window is not None:
        val = jnp.where(jnp.abs(rel) <= half_window, val, NEG)
    o_ref[0, 0] = val


def bias_tiles(table, *, nvar, nheads, head0, rows, cols, off0, off_step, row_coef, col_coef,
               dil, half_window, name):
    kern = functools.partial(_bias_kernel, off0=off0, off_step=off_step, row_coef=row_coef,
                             col_coef=col_coef, dil=dil, half_window=half_window, head0=head0)
    return pl.pallas_call(
        kern,
        grid=(nvar, nheads),
        in_specs=[pl.BlockSpec(memory_space=pltpu.SMEM)],
        out_specs=pl.BlockSpec((1, 1, rows, cols), lambda v, h: (v, h, 0, 0)),
        out_shape=jax.ShapeDtypeStruct((nvar, nheads, rows, cols), F32),
        compiler_params=_params("parallel", "parallel"),
        name=name,
    )(table)


def _dense_pipeline(nk, tile, n_streams, score_fn, value_fn, m_s, acc_s, bufs):
    units = [(i, c * QCOLS) for i in range(n_streams) for c in range(tile // QCOLS)]

    def stage(kc, cur, nxt):
        for i, c0 in units:
            cols = slice(c0, c0 + QCOLS)
            if nxt is not None:
                s = score_fn(kc + 1, i, cols)
                nxt[0][i, :, cols] = s
                nxt[1][i, :, cols] = jnp.max(s, axis=0, keepdims=True)
            if cur is not None:
                m_old = m_s[i, :, cols]
                m_new = jnp.maximum(m_old, cur[1][i, :, cols])
                alpha = jnp.exp2(m_old - m_new)
                p = jnp.exp2((cur[0][i, :, cols] - m_new).astype(BF16))
                acc_s[i, :, cols] = (alpha * acc_s[i, :, cols]
                                     + jnp.dot(value_fn(kc, i), p, preferred_element_type=F32))
                m_s[i, :, cols] = m_new

    m_s[...] = jnp.full(m_s.shape, NEG, F32)
    acc_s[...] = jnp.zeros(acc_s.shape, F32)
    stage(-1, None, bufs[0])
    n_loop = (nk - 1) // UNROLL

    def body(j, carry):
        for u in range(UNROLL):
            stage(UNROLL * j + u, bufs[u % 2], bufs[(u + 1) % 2])
        return carry

    lax.fori_loop(0, n_loop, body, 0)
    for kc in range(n_loop * UNROLL, nk):
        stage(kc, bufs[kc % 2], bufs[(kc + 1) % 2] if kc < nk - 1 else None)


def _attn_a_kernel(q_ref, k_ref, v_ref, bias_ref, lq1_ref, lk1_ref, lq2_ref, lk2_ref, subln_ref, o_ref,
                   m_s, acc_s, s_a, s_b, cm_a, cm_b, *, tile, nk, qtiles, lambda_init):
    qi0 = pl.program_id(2) * qtiles

    def score_fn(kc, i, cols):
        j, qt = divmod(i, qtiles)
        kblk = k_ref[0, pl.ds(pl.multiple_of(kc * tile, tile), tile), :]
        bt = bias_ref[jnp.clip(kc - (qi0 + qt), -BIAS_REACH, BIAS_REACH) + BIAS_REACH, 0, :, cols]
        return jnp.dot(kblk, q_ref[0, 0, j, qt, :, cols], preferred_element_type=F32) + bt

    def value_fn(kc, i):
        return v_ref[0, 0, kc]

    _dense_pipeline(nk, tile, 2 * qtiles, score_fn, value_fn, m_s, acc_s, ((s_a, cm_a), (s_b, cm_b)))

    lam = (jnp.exp(jnp.sum(lq1_ref[...] * lk1_ref[...], axis=-1, keepdims=True))
           - jnp.exp(jnp.sum(lq2_ref[...] * lk2_ref[...], axis=-1, keepdims=True)) + lambda_init)
    for qt in range(qtiles):
        a1, a2 = acc_s[qt], acc_s[qtiles + qt]
        o = (a1[:A_V_DIM] / a1[A_V_DIM:A_V_DIM + 1]
             - lam * (a2[:A_V_DIM] / a2[A_V_DIM:A_V_DIM + 1]))
        ms = jnp.mean(o * o, axis=0, keepdims=True)
        y = o * lax.rsqrt(ms + EPS) * subln_ref[...] * (1.0 - lambda_init)
        o_ref[0, qt * tile:(qt + 1) * tile] = y.T.astype(o_ref.dtype)


def attn_a(qt, karr, vt, bias, lq1, lk1, lq2, lk2, subln, lambda_init):
    B, S, _ = karr.shape
    T = DENSE_TILE
    nq = nk = S // T
    assert nk % 2 == 0 and nk >= 4
    qtiles = DENSE_QTILES
    ns = 2 * qtiles
    kern = functools.partial(_attn_a_kernel, tile=T, nk=nk, qtiles=qtiles, lambda_init=lambda_init)
    vec = lambda n: pl.BlockSpec((1, n), lambda b, h, i: (0, 0))
    return pl.pallas_call(
        kern,
        grid=(B, A_HEADS, nq // qtiles),
        in_specs=[pl.BlockSpec((1, 1, 2, qtiles, LANE, T), lambda b, h, i: (b, h, 0, i, 0, 0)),
                  pl.BlockSpec((1, S, LANE), lambda b, h, i: (b, 0, h)),
                  pl.BlockSpec((1, 1, nk, V_ROWS_A, T), lambda b, h, i: (b, h, 0, 0, 0)),
                  pl.BlockSpec((2 * BIAS_REACH + 1, 1, T, T), lambda b, h, i: (0, h, 0, 0)),
                  vec(HEAD_DIM), vec(HEAD_DIM), vec(HEAD_DIM), vec(HEAD_DIM),
                  pl.BlockSpec((A_V_DIM, 1), lambda b, h, i: (0, 0))],
        out_specs=pl.BlockSpec((1, qtiles * T, LANE), lambda b, h, i: (b, i, h)),
        out_shape=jax.ShapeDtypeStruct((B, S, A_HEADS * A_V_DIM), BF16),
        scratch_shapes=[pltpu.VMEM((ns, 1, T), F32), pltpu.VMEM((ns, V_ROWS_A, T), F32),
                        pltpu.VMEM((ns, T, T), F32), pltpu.VMEM((ns, T, T), F32),
                        pltpu.VMEM((ns, 1, T), F32), pltpu.VMEM((ns, 1, T), F32)],
        compiler_params=_params("parallel", "parallel", "arbitrary"),
        name="attn_a",
    )(qt, karr, vt, bias, lq1.reshape(1, -1), lk1.reshape(1, -1), lq2.reshape(1, -1),
      lk2.reshape(1, -1), subln.reshape(-1, 1))


def _attn_d_kernel(q_ref, k_ref, v_ref, o_ref, m_s, acc_s, s_a, s_b, cm_a, cm_b, *, tile, nk, qtiles):
    def score_fn(kc, i, cols):
        hh, qt = divmod(i, qtiles)
        kblk = k_ref[0, pl.ds(pl.multiple_of(kc * tile, tile), tile), hh * LANE:(hh + 1) * LANE]
        return jnp.dot(kblk, q_ref[0, hh, qt, :, cols], preferred_element_type=F32)

    def value_fn(kc, i):
        return v_ref[0, i // qtiles, kc]

    _dense_pipeline(nk, tile, 2 * qtiles, score_fn, value_fn, m_s, acc_s, ((s_a, cm_a), (s_b, cm_b)))
    for qt in range(qtiles):
        outs = []
        for hh in range(2):
            acc = acc_s[hh * qtiles + qt]
            outs.append(acc[:D_V] / acc[D_V:D_V + 1])
        o_ref[0, qt * tile:(qt + 1) * tile] = jnp.concatenate(outs, axis=0).T.astype(o_ref.dtype)


def attn_d(qt, k, vt):
    B, S, _ = k.shape
    T = DENSE_TILE
    nq = nk = S // T
    qtiles = DENSE_QTILES
    ns = 2 * qtiles
    kern = functools.partial(_attn_d_kernel, tile=T, nk=nk, qtiles=qtiles)
    return pl.pallas_call(
        kern,
        grid=(B, D_HEADS // 2, nq // qtiles),
        in_specs=[pl.BlockSpec((1, 2, qtiles, LANE, T), lambda b, h, i: (b, h, i, 0, 0)),
                  pl.BlockSpec((1, S, 2 * LANE), lambda b, h, i: (b, 0, h)),
                  pl.BlockSpec((1, 2, nk, V_ROWS_D, T), lambda b, h, i: (b, h, 0, 0, 0))],
        out_specs=pl.BlockSpec((1, qtiles * T, LANE), lambda b, h, i: (b, i, h)),
        out_shape=jax.ShapeDtypeStruct((B, S, D_HEADS * D_V), BF16),
        scratch_shapes=[pltpu.VMEM((ns, 1, T), F32), pltpu.VMEM((ns, V_ROWS_D, T), F32),
                        pltpu.VMEM((ns, T, T), F32), pltpu.VMEM((ns, T, T), F32),
                        pltpu.VMEM((ns, 1, T), F32), pltpu.VMEM((ns, 1, T), F32)],
        compiler_params=_params("parallel", "parallel", "arbitrary"),
        name="attn_d",
    )(qt, k, vt)


def _attn_b_kernel(q_ref, k_ref, v_ref, bias_ref, sink_ref, o_ref, *, tile, seq_len):
    half_window = B_HALF_WINDOW
    span = QCOLS + 2 * half_window
    nchunk = span // LANE
    step = pl.program_id(1)

    def window(c):
        q0 = step * tile + c * QCOLS
        start = pl.multiple_of(jnp.clip(q0 - half_window, 0, seq_len - span), LANE)
        variant = jnp.where(q0 == 0, 0, jnp.where(q0 + QCOLS == seq_len, 2, 1))
        return start, variant

    def scores(task):
        c, j, g = task
        start, variant = window(c)
        kwin = k_ref[0, pl.ds(start, span), :]
        s = jnp.dot(kwin, q_ref[0, j, g, 0, :, c * QCOLS:(c + 1) * QCOLS], preferred_element_type=F32)
        return s + bias_ref[variant, j + (B_HEADS // 2) * g]

    def finish(task, s):
        c, j, g = task
        head = j + (B_HEADS // 2) * g
        start, _ = window(c)
        chunk0 = start // LANE
        vwin = jnp.concatenate([v_ref[0, g, chunk0 + n] for n in range(nchunk)], axis=1)
        sk = sink_ref[:, head:head + 1] * LOG2E
        m = jnp.maximum(jnp.max(s, axis=0, keepdims=True), sk)
        e = jnp.exp2((s - m).astype(BF16))
        ov = jnp.dot(vwin, e, preferred_element_type=F32)
        denom = ov[HEAD_DIM:HEAD_DIM + 1] + jnp.exp2(sk - m)
        return ov[:HEAD_DIM] / denom

    tasks = [(c, j, g) for c in range(tile // QCOLS) for j in range(B_HEADS // 2) for g in range(B_KV_HEADS)]
    s_next = scores(tasks[0])
    held = None
    for n, task in enumerate(tasks):
        s = s_next
        if n + 1 < len(tasks):
            s_next = scores(tasks[n + 1])
        o = finish(task, s)
        c, j, g = task
        if g == 0:
            held = o
            continue
        pair = jnp.concatenate([held, o], axis=0).T
        o_ref[0, c * QCOLS:(c + 1) * QCOLS, j * LANE:(j + 1) * LANE] = pair.astype(o_ref.dtype)


def attn_b(qt, karr, k_block, vt, bias, sink):
    B, S, _ = karr.shape
    T = DENSE_TILE
    return pl.pallas_call(
        functools.partial(_attn_b_kernel, tile=T, seq_len=S),
        grid=(B, S // T),
        in_specs=[pl.BlockSpec((1, B_HEADS // 2, 2, 1, LANE, T), lambda b, i: (b, 0, 0, i, 0, 0)),
                  pl.BlockSpec((1, S, LANE), lambda b, i: (b, 0, k_block)),
                  pl.BlockSpec((1, B_KV_HEADS, S // LANE, V_ROWS_D, LANE), lambda b, i: (b, 0, 0, 0, 0)),
                  pl.BlockSpec(bias.shape, lambda b, i: (0, 0, 0, 0)),
                  pl.BlockSpec((1, B_HEADS), lambda b, i: (0, 0))],
        out_specs=pl.BlockSpec((1, T, B_HEADS * HEAD_DIM), lambda b, i: (b, i, 0)),
        out_shape=jax.ShapeDtypeStruct((B, S, B_HEADS * HEAD_DIM), BF16),
        compiler_params=_params("parallel", "arbitrary"),
        name="attn_b",
    )(qt, karr, vt, bias, sink.reshape(1, -1))


def _banded_kernel(*refs, tq, group, span, half_window, seq_len, nqb, nkb, head_of, has_sink, has_lse):
    it = iter(refs)
    q_ref, k_ref, v_ref, bias_ref = next(it), next(it), next(it), next(it)
    sink_ref = next(it) if has_sink else None
    o_ref = next(it)
    lse_ref = next(it) if has_lse else None

    nq = seq_len // tq
    lane = lax.broadcasted_iota(jnp.int32, (tq, LANE), 1)
    low = lane < HEAD_DIM
    ones_col = jnp.where(lax.broadcasted_iota(jnp.int32, (span, LANE), 1) == 0, 1.0, 0.0).astype(BF16)

    def window(g):
        t = pl.program_id(2) * group + g
        start = pl.multiple_of(jnp.clip(t * tq - half_window, 0, seq_len - span), half_window)
        variant = jnp.where(t == 0, 0, jnp.where(t == nq - 1, 2, 1))
        return start, variant

    def scores(task):
        g, c, half = task
        start, variant = window(g)
        kc = c if nkb == nqb else 0
        q2 = q_ref[0, g * tq:(g + 1) * tq, c * LANE:(c + 1) * LANE]
        kb = k_ref[0, pl.ds(start, span), kc * LANE:(kc + 1) * LANE]
        qm = jnp.where(low if half == 0 else jnp.logical_not(low), q2, jnp.zeros_like(q2))
        s = lax.dot_general(qm, kb, (((1,), (1,)), ((), ())), preferred_element_type=F32)
        return s + bias_ref[variant, head_of(c, half)]

    def finish(task, s):
        g, c, half = task
        start, _ = window(g)
        kc = c if nkb == nqb else 0
        hidx = head_of(c, half)
        vb = v_ref[0, pl.ds(start, span), kc * LANE:(kc + 1) * LANE]
        m = jnp.max(s, axis=-1, keepdims=True)
        if has_sink:
            sk = sink_ref[:, hidx:hidx + 1] * LOG2E
            m = jnp.maximum(m, sk)
        e = jnp.exp2(s - m).astype(BF16)
        ov = jnp.dot(e, jnp.concatenate([vb, ones_col], axis=1), preferred_element_type=F32)
        denom = ov[:, LANE:LANE + 1]
        if has_sink:
            denom = denom + jnp.exp2(sk - m)
        return ov[:, :LANE] / denom, (LN2 * m + jnp.log(denom) if has_lse else None)

    tasks = [(g, c, half) for g in range(group) for c in range(nqb) for half in range(2)]
    s_next = scores(tasks[0])
    held = None
    for n, task in enumerate(tasks):
        s = s_next
        if n + 1 < len(tasks):
            s_next = scores(tasks[n + 1])
        o, lse = finish(task, s)
        g, c, half = task
        if half == 0:
            held = (o, lse)
            continue
        rows, cols = slice(g * tq, (g + 1) * tq), slice(c * LANE, (c + 1) * LANE)
        o_ref[0, rows, cols] = jnp.where(low, held[0], o).astype(o_ref.dtype)
        if has_lse:
            lse_ref[0, rows, cols] = jnp.where(low, held[1], lse)


def banded(view, bias, sink, *, dil, tq, half_window, nqb, nkb, q_idx, k_idx, v_idx, head_of,
           out_dtype, has_lse, name):
    B, L, _ = view.shape
    span = tq + 2 * half_window
    nq = L // tq
    assert L % tq == 0 and L >= span and nq >= 2
    group = math.gcd(nq, BAND_GROUP)
    OW = nqb * LANE
    has_sink = sink is not None
    kern = functools.partial(_banded_kernel, tq=tq, group=group, span=span, half_window=half_window,
                             seq_len=L, nqb=nqb, nkb=nkb, head_of=head_of, has_sink=has_sink,
                             has_lse=has_lse)
    in_specs = [pl.BlockSpec((1, group * tq, OW), lambda b, r, t: (b, t, q_idx(r))),
                pl.BlockSpec((1, L, nkb * LANE), lambda b, r, t: (b, 0, k_idx(r))),
                pl.BlockSpec((1, L, nkb * LANE), lambda b, r, t: (b, 0, v_idx(r))),
                pl.BlockSpec(bias.shape, lambda b, r, t: (0, 0, 0, 0))]
    args = [view, view, view, bias]
    if has_sink:
        in_specs.append(pl.BlockSpec((1, sink.shape[-1]), lambda b, r, t: (0, 0)))
        args.append(sink.reshape(1, -1))
    out_spec = pl.BlockSpec((1, group * tq, OW), lambda b, r, t: (b, t, r))
    out_shapes = [jax.ShapeDtypeStruct((B, L, dil * OW), out_dtype)]
    out_specs = [out_spec]
    if has_lse:
        out_shapes.append(jax.ShapeDtypeStruct((B, L, dil * OW), F32))
        out_specs.append(out_spec)
    outs = pl.pallas_call(
        kern,
        grid=(B, dil, nq // group),
        in_specs=in_specs,
        out_specs=out_specs,
        out_shape=out_shapes,
        compiler_params=_params("parallel", "parallel", "arbitrary"),
        name=name,
    )(*args)
    return outs


def _out_proj_ab_kernel(h_ref, a_ref, b_ref, wa_ref, wb_ref, o_ref):
    mix = jnp.dot(a_ref[...], wa_ref[...], preferred_element_type=F32)
    mix = mix + jnp.dot(b_ref[...], wb_ref[...], preferred_element_type=F32)
    o_ref[...] = h_ref[...] + mix


def _out_proj_cd_kernel(h_ref, o0, o1, o2, s0, s1, s2, d_ref, wc_ref, wd_ref, o_ref, *scratch):
    tm = h_ref.shape[0]
    width = C_HEADS_PER_GROUP * HEAD_DIM
    spare = iter(scratch)

    def in_position_order(ref, dil):
        if dil == 1:
            return ref[0]
        t_s = next(spare)
        for r in range(dil):
            for j in range(width // LANE):
                t_s[j, pl.ds(r, tm // dil, stride=dil), :] = ref[0, :, r * width + j * LANE:r * width + (j + 1) * LANE]
        return jnp.concatenate([t_s[j] for j in range(width // LANE)], axis=1)

    dils = [d for _, d in C_PATTERNS]
    outs = [in_position_order(r, d) for r, d in zip((o0, o1, o2), dils)]
    lses = [in_position_order(r, d) for r, d in zip((s0, s1, s2), dils)]
    mx = jnp.maximum(jnp.maximum(lses[0], lses[1]), lses[2])
    es = [jnp.exp(l - mx) for l in lses]
    oc = (es[0] * outs[0] + es[1] * outs[1] + es[2] * outs[2]) / (es[0] + es[1] + es[2])
    mix = jnp.dot(oc.astype(BF16), wc_ref[...], preferred_element_type=F32)
    mix = mix + jnp.dot(d_ref[...], wd_ref[...], preferred_element_type=F32)
    o_ref[...] = h_ref[...] + mix


def out_proj_cd(h, oc, lses, od, wc, wd, batch, seq_len):
    M = h.shape[0]
    tm = ROW_TILE
    nrow = seq_len // tm
    width = C_HEADS_PER_GROUP * HEAD_DIM
    row = lambda a: pl.BlockSpec((tm, a.shape[1]), lambda i: (i, 0))
    full = lambda a: pl.BlockSpec(a.shape, lambda i: (0, 0))
    views = [pl.BlockSpec((1, tm // d, d * width), lambda i: (i // nrow, i % nrow, 0)) for _, d in C_PATTERNS]
    n_spare = 2 * sum(d > 1 for _, d in C_PATTERNS)
    return pl.pallas_call(
        _out_proj_cd_kernel,
        grid=(M // tm,),
        in_specs=[row(h), *views, *views, row(od), full(wc), full(wd)],
        out_specs=pl.BlockSpec((tm, D_MODEL), lambda i: (i, 0)),
        out_shape=jax.ShapeDtypeStruct((M, D_MODEL), F32),
        scratch_shapes=[pltpu.VMEM((width // LANE, tm, LANE), F32)] * n_spare,
        compiler_params=_params("parallel"),
        name="out_proj1",
    )(h, *oc, *lses, od, wc, wd)


def _proj1_kernel(x_ref, g_ref, w_ref, cs_ref, c0_ref, c1_ref, c2_ref, y_s):
    xn = _rms(x_ref[...], g_ref[...]).astype(BF16)
    y = jnp.dot(xn, w_ref[...], preferred_element_type=F32) * cs_ref[...]
    tm = y.shape[0]
    width = y.shape[1] // len(C_PATTERNS)
    c0_ref[0] = y[:, :width].astype(BF16)
    nblk = width // LANE
    for j in range(y_s.shape[0]):
        y_s[j] = y[:, width + j * LANE:width + (j + 1) * LANE]
    for g, ref in ((1, c1_ref), (2, c2_ref)):
        dil = C_PATTERNS[g][1]
        for r in range(dil):
            for j in range(nblk):
                rows = y_s[(g - 1) * nblk + j, pl.ds(r, tm // dil, stride=dil), :]
                ref[0, :, r * width + j * LANE:r * width + (j + 1) * LANE] = rows.astype(BF16)


def proj1_call(x, g, w, colscale, batch, seq_len):
    M, K = x.shape
    N = w.shape[1]
    tm = ROW_TILE
    nrow = seq_len // tm
    width = N // len(C_PATTERNS)
    assert [d for _, d in C_PATTERNS][0] == 1
    out_specs = [pl.BlockSpec((1, tm // d, d * width), lambda i: (i // nrow, i % nrow, 0)) for _, d in C_PATTERNS]
    out_shape = [jax.ShapeDtypeStruct((batch, seq_len // d, d * width), BF16) for _, d in C_PATTERNS]
    return pl.pallas_call(
        _proj1_kernel,
        grid=(M // tm,),
        in_specs=[pl.BlockSpec((tm, K), lambda i: (i, 0)),
                  pl.BlockSpec((1, K), lambda i: (0, 0)),
                  pl.BlockSpec((K, N), lambda i: (0, 0)),
                  pl.BlockSpec((1, N), lambda i: (0, 0))],
        out_specs=out_specs,
        out_shape=out_shape,
        scratch_shapes=[pltpu.VMEM(((N - width) // LANE, tm, LANE), F32)],
        compiler_params=_params("parallel"),
        name="proj1",
    )(x, g.reshape(1, K), w, colscale.reshape(1, N))


def _row_call(kern, row_args, full_args, name):
    M = row_args[0].shape[0]
    tm = ROW_TILE
    in_specs = [pl.BlockSpec((tm, a.shape[1]), lambda i: (i, 0)) for a in row_args]
    in_specs += [pl.BlockSpec(a.shape, lambda i: (0, 0)) for a in full_args]
    return pl.pallas_call(
        kern,
        grid=(M // tm,),
        in_specs=in_specs,
        out_specs=pl.BlockSpec((tm, D_MODEL), lambda i: (i, 0)),
        out_shape=jax.ShapeDtypeStruct((M, D_MODEL), F32),
        compiler_params=_params("parallel"),
        name=name,
    )(*row_args, *full_args)


def _ffn_kernel(x_ref, g_ref, wg_ref, wu_ref, wd_ref, fg_ref, o_ref, xn_s, acc_s, *, final_norm):
    f = pl.program_id(1)

    @pl.when(f == 0)
    def _():
        xn_s[...] = _rms(x_ref[...], g_ref[...]).astype(BF16)
        acc_s[...] = jnp.zeros(acc_s.shape, F32)

    xn = xn_s[...]
    gate = jnp.dot(xn, wg_ref[...], preferred_element_type=F32)
    up = jnp.dot(xn, wu_ref[...], preferred_element_type=F32)
    mid = (gate / (1.0 + jnp.exp(-gate)) * up).astype(BF16)
    acc_s[...] += jnp.dot(mid, wd_ref[...], preferred_element_type=F32)

    @pl.when(f == pl.num_programs(1) - 1)
    def _():
        y = x_ref[...] + acc_s[...]
        if final_norm:
            y = _rms(y, fg_ref[...])
        o_ref[...] = y


def ffn(x, g, wg, wu, wd, fg, final_norm, name):
    M, K = x.shape
    tm, tf = FFN_ROW_TILE, FFN_COL_TILE
    kern = functools.partial(_ffn_kernel, final_norm=final_norm)
    return pl.pallas_call(
        kern,
        grid=(M // tm, D_FF // tf),
        in_specs=[pl.BlockSpec((tm, K), lambda i, f: (i, 0)),
                  pl.BlockSpec((1, K), lambda i, f: (0, 0)),
                  pl.BlockSpec((K, tf), lambda i, f: (0, f)),
                  pl.BlockSpec((K, tf), lambda i, f: (0, f)),
                  pl.BlockSpec((tf, K), lambda i, f: (f, 0)),
                  pl.BlockSpec((1, K), lambda i, f: (0, 0))],
        out_specs=pl.BlockSpec((tm, K), lambda i, f: (i, 0)),
        out_shape=jax.ShapeDtypeStruct((M, K), F32),
        scratch_shapes=[pltpu.VMEM((tm, K), BF16), pltpu.VMEM((tm, K), F32)],
        compiler_params=_params("parallel", "arbitrary"),
        name=name,
    )(x, g.reshape(1, K), wg, wu, wd, fg.reshape(1, K))


def _trig_kernel(ang_ref, cos_ref, sin_ref):
    a = ang_ref[...]
    cos_ref[...] = jnp.cos(a)
    sin_ref[...] = jnp.sin(a)


def rope_tables(seq_len):
    half = D_ROPE // 2
    inv = ROPE_THETA ** (-jnp.arange(half, dtype=F32) / half)
    ang = jnp.arange(seq_len).astype(F32)[:, None] * inv[None, :]
    dense = ang.reshape(seq_len * half // LANE, LANE)
    spec = pl.BlockSpec(dense.shape, lambda: (0, 0))
    cos, sin = pl.pallas_call(
        _trig_kernel,
        in_specs=[spec],
        out_specs=[spec, spec],
        out_shape=[jax.ShapeDtypeStruct(dense.shape, F32)] * 2,
        name="rope_trig",
    )(dense)
    cos, sin = cos.reshape(seq_len, half), sin.reshape(seq_len, half)
    pad = LANE - D_NOPE - D_ROPE
    cos_l = jnp.concatenate([jnp.ones((seq_len, D_NOPE), F32), cos, cos, jnp.ones((seq_len, pad), F32)], axis=1)
    sin_l = jnp.concatenate([jnp.zeros((seq_len, D_NOPE), F32), sin, sin, jnp.zeros((seq_len, pad), F32)], axis=1)
    return cos_l, sin_l


def _prep_d_kernel(x_ref, g_ref, wa_ref, qn_ref, kvn_ref, wq_ref, wkv_ref, cos_ref, sin_ref,
                   q_ref, k_ref, v_ref, *, qscale):
    xn = _rms(x_ref[...], g_ref[...]).astype(BF16)
    lat = jnp.dot(xn, wa_ref[...], preferred_element_type=F32)
    cq = _rms(lat[:, :D_Q_LORA], qn_ref[...]).astype(BF16)
    ckv = _rms(lat[:, D_Q_LORA:D_Q_LORA + D_KV_LORA], kvn_ref[...]).astype(BF16)
    o_pe = D_Q_LORA + D_KV_LORA
    cos, sin = cos_ref[...], sin_ref[...]
    kpe = lat[:, o_pe:o_pe + LANE] * cos + lat[:, o_pe + LANE:o_pe + 2 * LANE] * sin
    qq = jnp.dot(cq, wq_ref[...], preferred_element_type=F32)
    kv = jnp.dot(ckv, wkv_ref[...], preferred_element_type=F32)
    kw = D_HEADS * LANE
    ones_blk = _ones_row_block(V_ROWS_D - D_V, x_ref.shape[0])
    for h in range(D_HEADS):
        qh = qq[:, h * LANE:(h + 1) * LANE] * cos + qq[:, kw + h * LANE:kw + (h + 1) * LANE] * sin
        q_ref[0, h, 0] = (qh * qscale).T.astype(BF16)
        k_ref[:, h * LANE:(h + 1) * LANE] = (kv[:, h * LANE:(h + 1) * LANE] + kpe).astype(BF16)
    for j in range(D_HEADS // 2):
        vt = kv[:, kw + j * LANE:kw + (j + 1) * LANE].T.astype(BF16)
        for half in range(2):
            v_ref[0, 2 * j + half, 0, :D_V] = vt[half * D_V:(half + 1) * D_V]
            v_ref[0, 2 * j + half, 0, D_V:] = ones_blk


def prep_d(x, g, wa, qn, kvn, wq, wkv, cos_l, sin_l, batch, seq_len):
    M, K = x.shape
    T = DENSE_TILE
    nrow = seq_len // T
    kw = D_HEADS * LANE
    kern = functools.partial(_prep_d_kernel, qscale=(D_NOPE + D_ROPE) ** -0.5 * LOG2E)
    full = lambda a: pl.BlockSpec(a.shape, lambda i: (0, 0))
    qn2, kvn2, g2 = qn.reshape(1, -1), kvn.reshape(1, -1), g.reshape(1, K)
    return pl.pallas_call(
        kern,
        grid=(M // T,),
        in_specs=[pl.BlockSpec((T, K), lambda i: (i, 0)), full(g2), full(wa), full(qn2), full(kvn2),
                  full(wq), full(wkv),
                  pl.BlockSpec((T, LANE), lambda i: (i % nrow, 0)),
                  pl.BlockSpec((T, LANE), lambda i: (i % nrow, 0))],
        out_specs=[pl.BlockSpec((1, D_HEADS, 1, LANE, T), lambda i: (i // nrow, 0, i % nrow, 0, 0)),
                   pl.BlockSpec((T, kw), lambda i: (i, 0)),
                   pl.BlockSpec((1, D_HEADS, 1, V_ROWS_D, T), lambda i: (i // nrow, 0, i % nrow, 0, 0))],
        out_shape=[jax.ShapeDtypeStruct((batch, D_HEADS, nrow, LANE, T), BF16),
                   jax.ShapeDtypeStruct((M, kw), BF16),
                   jax.ShapeDtypeStruct((batch, D_HEADS, nrow, V_ROWS_D, T), BF16)],
        compiler_params=_params("parallel"),
        name="prep_d",
    )(x, g2, wa, qn2, kvn2, wq, wkv, cos_l, sin_l)


B_HEAD_ORDER = (0, 4, 1, 5, 2, 6, 3, 7)


def _head_cols(order, base):
    return np.concatenate([np.arange(base + h * HEAD_DIM, base + (h + 1) * HEAD_DIM) for h in order])


def _rot_partner_cols(w):
    half = D_ROPE // 2
    return jnp.concatenate([-w[..., half:], w[..., :half]], axis=-1)


def kernel(x, bias_table, attn_norm, ffn_norm, final_norm, ab_w_in, ab_lambda_q1, ab_lambda_k1,
           ab_lambda_q2, ab_lambda_k2, ab_subln, ab_sink, ab_w_o, cd_w_in, cd_q_norm, cd_w_q_b,
           cd_kv_norm, cd_w_kv_b, cd_w_o, ffn_w_gate, ffn_w_up, ffn_w_down):
    B, S, _ = x.shape
    M = B * S
    T = DENSE_TILE
    h = x.reshape(M, D_MODEL)
    qk_scale = HEAD_DIM ** -0.5 * LOG2E

    o3 = A_HEADS * (2 * A_QK_DIM + A_V_DIM)
    cols0 = np.concatenate([np.arange(o3), _head_cols(B_HEAD_ORDER, o3),
                            np.arange(o3 + B_HEADS * HEAD_DIM, AB_IN)])
    w0 = ab_w_in[0][:, cols0].astype(BF16)
    cs0 = np.ones((AB_IN,), np.float32)
    cs0[:A_HEADS * A_QK_DIM] = qk_scale
    cs0[o3:o3 + B_HEADS * HEAD_DIM] = qk_scale
    qat, vat, qbt, vbt, keys0 = proj0_call(h, attn_norm[0], w0, jnp.asarray(cs0), B, S)
    keys0 = keys0.reshape(B, S, -1)
    bias_a = bias_tiles(bias_table, nvar=2 * BIAS_REACH + 1, nheads=A_HEADS, head0=0, rows=T, cols=T,
                        off0=-BIAS_REACH * T, off_step=T, row_coef=1, col_coef=-1, dil=1,
                        half_window=None, name="bias_a")
    oa = attn_a(qat, keys0, vat, bias_a, ab_lambda_q1[0], ab_lambda_k1[0], ab_lambda_q2[0],
                ab_lambda_k2[0], ab_subln[0], 0.8 - 0.6 * math.exp(-0.3 * 0))

    bias_b = bias_tiles(bias_table, nvar=3, nheads=B_HEADS, head0=A_HEADS, rows=QCOLS + 2 * B_HALF_WINDOW,
                        cols=QCOLS, off0=0, off_step=-B_HALF_WINDOW, row_coef=1, col_coef=-1, dil=1,
                        half_window=B_HALF_WINDOW, name="bias_b")
    ob = attn_b(qbt, keys0, (A_HEADS * A_QK_DIM) // LANE, vbt, bias_b, ab_sink[0])

    wo = ab_w_o[0]
    wo_a = wo[:A_HEADS * A_V_DIM].astype(BF16)
    wo_b = wo[_head_cols(B_HEAD_ORDER, A_HEADS * A_V_DIM)].astype(BF16)
    h = _row_call(_out_proj_ab_kernel, [h, oa.reshape(M, -1), ob.reshape(M, -1)], [wo_a, wo_b], "out_proj0")
    h = ffn(h, ffn_norm[0], ffn_w_gate[0].astype(BF16), ffn_w_up[0].astype(BF16),
            ffn_w_down[0].astype(BF16), final_norm, False, "ffn0")

    w1 = cd_w_in[0]
    gw = C_HEADS_PER_GROUP * HEAD_DIM
    cw = C_HEADS * HEAD_DIM
    cols1 = np.concatenate([np.arange(role * cw + g * gw, role * cw + (g + 1) * gw)
                            for g in range(len(C_PATTERNS)) for role in range(3)])
    cs1 = np.ones((CD_C_IN,), np.float32)
    for g in range(len(C_PATTERNS)):
        cs1[3 * g * gw:(3 * g + 1) * gw] = qk_scale
    c_views = proj1_call(h, attn_norm[1], w1[:, cols1].astype(BF16), jnp.asarray(cs1), B, S)

    oc, lses = [], []
    for g, (window, dil) in enumerate(C_PATTERNS):
        hw = window // (2 * dil)
        tq_c = 128
        bias_c = bias_tiles(bias_table, nvar=3, nheads=C_HEADS_PER_GROUP, head0=g * C_HEADS_PER_GROUP,
                            rows=tq_c, cols=tq_c + 2 * hw, off0=0, off_step=-hw, row_coef=-1, col_coef=1,
                            dil=dil, half_window=hw, name=f"bias_c{g}")
        o_g, lse_g = banded(c_views[g], bias_c, None, dil=dil, tq=tq_c, half_window=hw, nqb=2, nkb=2,
                            q_idx=lambda r: 3 * r, k_idx=lambda r: 3 * r + 1, v_idx=lambda r: 3 * r + 2,
                            head_of=lambda c, half: 2 * c + half, out_dtype=F32, has_lse=True,
                            name=f"attn_c{g}")
        oc.append(o_g)
        lses.append(lse_g)

    o_q, o_kv = CD_C_IN + D_Q_LORA, CD_C_IN + D_Q_LORA + D_KV_LORA
    w_pe = w1[:, o_kv:]
    lane_pad = lambda w: jnp.pad(w, ((0, 0), (D_NOPE, LANE - D_NOPE - D_ROPE)))
    wa = jnp.concatenate([w1[:, CD_C_IN:o_kv], lane_pad(w_pe), lane_pad(_rot_partner_cols(w_pe))],
                         axis=1).astype(BF16)
    wq3 = cd_w_q_b[0].reshape(D_Q_LORA, D_HEADS, D_NOPE + D_ROPE)
    zpad = jnp.zeros((D_Q_LORA, D_HEADS, LANE - D_NOPE - D_ROPE), F32)
    wq_main = jnp.concatenate([wq3, zpad], axis=-1)
    wq_rot = jnp.concatenate([jnp.zeros_like(wq3[..., :D_NOPE]), _rot_partner_cols(wq3[..., D_NOPE:]), zpad],
                             axis=-1)
    wq = jnp.concatenate([wq_main.reshape(D_Q_LORA, -1), wq_rot.reshape(D_Q_LORA, -1)], axis=1).astype(BF16)
    wkv3 = cd_w_kv_b[0].reshape(D_KV_LORA, D_HEADS, D_NOPE + D_V)
    wk = jnp.pad(wkv3[..., :D_NOPE], ((0, 0), (0, 0), (0, LANE - D_NOPE))).reshape(D_KV_LORA, -1)
    wv = wkv3[..., D_NOPE:].reshape(D_KV_LORA, -1)
    wkv = jnp.concatenate([wk, wv], axis=1).astype(BF16)
    cos_l, sin_l = rope_tables(S)
    qdt, kd, vdt = prep_d(h, attn_norm[1], wa, cd_q_norm[0], cd_kv_norm[0], wq, wkv, cos_l, sin_l, B, S)
    od = attn_d(qdt, kd.reshape(B, S, D_HEADS * LANE), vdt)

    wo1 = cd_w_o[0]
    wo_c = wo1[:C_HEADS_PER_GROUP * HEAD_DIM].astype(BF16)
    wo_d = wo1[C_HEADS_PER_GROUP * HEAD_DIM:].astype(BF16)
    h = out_proj_cd(h, oc, lses, od.reshape(M, -1), wo_c, wo_d, B, S)
    h = ffn(h, ffn_norm[1], ffn_w_gate[1].astype(BF16), ffn_w_up[1].astype(BF16),
            ffn_w_down[1].astype(BF16), final_norm, True, "ffn1")
    return h.reshape(B, S, D_MODEL)
```

```python
import functools
import math

import numpy as np
import jax
import jax.numpy as jnp
from jax import lax
from jax.experimental import pallas as pl
from jax.experimental.pallas import tpu as pltpu

F32 = jnp.float32
BF16 = jnp.bfloat16

D_MODEL = 1024
HEAD_DIM = 64
EPS = 1e-6
NEG = -1e30
LOG2E = math.log2(math.e)
LN2 = math.log(2.0)

A_HEADS = 4
A_QK_DIM = 2 * HEAD_DIM
A_V_DIM = 2 * HEAD_DIM
B_HEADS = 8
B_KV_HEADS = 2
B_HALF_WINDOW = 128
C_PATTERNS = ((128, 1), (512, 4), (2048, 16))
C_HEADS_PER_GROUP = 4
C_HEADS = C_HEADS_PER_GROUP * len(C_PATTERNS)
D_HEADS = 12
D_Q_LORA = 384
D_KV_LORA = 256
D_NOPE = 64
D_ROPE = 32
D_V = 64
ROPE_THETA = 10000.0
NUM_BUCKETS = 32
MAX_DISTANCE = 1024
D_FF = 2816
AB_IN = 2304
CD_C_IN = 3 * C_HEADS * HEAD_DIM

LANE = 128
VMEM_LIMIT = 48 * 1024 * 1024

ROW_TILE = 512
FFN_ROW_TILE = 1024
FFN_COL_TILE = 256
DENSE_TILE = 512
QCOLS = 256
UNROLL = 4
DENSE_QTILES = 2
BAND_GROUP = 4
BIAS_REACH = 3
V_ROWS_D = 80
V_ROWS_A = 144


def _bucket_thresholds():
    nb = NUM_BUCKETS // 2
    max_exact = nb // 2
    n = np.arange(1, 4 * MAX_DISTANCE)
    large = max_exact + (np.log(n.astype(np.float32) / np.float32(max_exact))
                         / np.float32(math.log(MAX_DISTANCE / max_exact))
                         * np.float32(nb - max_exact)).astype(np.int32)
    mag = np.where(n < max_exact, n, np.minimum(large, nb - 1))
    return tuple(int(n[np.argmax(mag >= k)]) for k in range(1, nb))


BUCKET_THRESHOLDS = _bucket_thresholds()
assert BUCKET_THRESHOLDS[-1] <= (BIAS_REACH - 1) * DENSE_TILE + 1


def _params(*sem):
    return pltpu.CompilerParams(dimension_semantics=sem, vmem_limit_bytes=VMEM_LIMIT)


def _rms(x, g):
    return x * lax.rsqrt(jnp.mean(x * x, axis=-1, keepdims=True) + EPS) * g


def _ones_row_block(rows, cols):
    r = lax.broadcasted_iota(jnp.int32, (rows, cols), 0)
    return jnp.where(r == 0, 1.0, 0.0).astype(BF16)


def _store_masked_halves(q_ref, idx, qt):
    zeros = jnp.zeros((HEAD_DIM, qt.shape[1]), BF16)
    q_ref[idx + (0, 0, slice(None, HEAD_DIM))] = qt[:HEAD_DIM]
    q_ref[idx + (0, 0, slice(HEAD_DIM, None))] = zeros
    q_ref[idx + (1, 0, slice(None, HEAD_DIM))] = zeros
    q_ref[idx + (1, 0, slice(HEAD_DIM, None))] = qt[HEAD_DIM:]


def _proj0_kernel(x_ref, g_ref, w_ref, cs_ref, qa_ref, va_ref, qb_ref, vb_ref, k_ref):
    xn = _rms(x_ref[...], g_ref[...]).astype(BF16)
    y = jnp.dot(xn, w_ref[...], preferred_element_type=F32) * cs_ref[...]
    tm = y.shape[0]
    ka0 = A_HEADS * A_QK_DIM
    va0 = 2 * ka0
    qb0 = va0 + A_HEADS * A_V_DIM
    kb0 = qb0 + B_HEADS * HEAD_DIM
    vb0 = kb0 + B_KV_HEADS * HEAD_DIM
    for h in range(A_HEADS):
        _store_masked_halves(qa_ref, (0, h), y[:, h * A_QK_DIM:(h + 1) * A_QK_DIM].T.astype(BF16))
        va_ref[0, h, 0, :A_V_DIM] = y[:, va0 + h * A_V_DIM:va0 + (h + 1) * A_V_DIM].T.astype(BF16)
        va_ref[0, h, 0, A_V_DIM:] = _ones_row_block(V_ROWS_A - A_V_DIM, tm)
    for j in range(B_HEADS // 2):
        _store_masked_halves(qb_ref, (0, j), y[:, qb0 + j * LANE:qb0 + (j + 1) * LANE].T.astype(BF16))
    vbt = y[:, vb0:vb0 + LANE].T.astype(BF16)
    ones_blk = _ones_row_block(V_ROWS_D - HEAD_DIM, LANE)
    for g in range(B_KV_HEADS):
        for c in range(tm // LANE):
            vb_ref[0, g, c, :HEAD_DIM] = vbt[g * HEAD_DIM:(g + 1) * HEAD_DIM, c * LANE:(c + 1) * LANE]
            vb_ref[0, g, c, HEAD_DIM:] = ones_blk
    k_ref[:, :ka0] = y[:, ka0:va0].astype(BF16)
    k_ref[:, ka0:] = y[:, kb0:vb0].astype(BF16)


def proj0_call(x, g, w, colscale, batch, seq_len):
    M, K = x.shape
    N = w.shape[1]
    T = DENSE_TILE
    nrow = seq_len // T
    nkb = A_HEADS * A_QK_DIM + B_KV_HEADS * HEAD_DIM
    qspec = pl.BlockSpec((1, A_HEADS, 2, 1, LANE, T), lambda i: (i // nrow, 0, 0, i % nrow, 0, 0))
    qshape = jax.ShapeDtypeStruct((batch, A_HEADS, 2, nrow, LANE, T), BF16)
    return pl.pallas_call(
        _proj0_kernel,
        grid=(M // T,),
        in_specs=[pl.BlockSpec((T, K), lambda i: (i, 0)),
                  pl.BlockSpec((1, K), lambda i: (0, 0)),
                  pl.BlockSpec((K, N), lambda i: (0, 0)),
                  pl.BlockSpec((1, N), lambda i: (0, 0))],
        out_specs=[qspec,
                   pl.BlockSpec((1, A_HEADS, 1, V_ROWS_A, T), lambda i: (i // nrow, 0, i % nrow, 0, 0)),
                   qspec,
                   pl.BlockSpec((1, B_KV_HEADS, T // LANE, V_ROWS_D, LANE),
                                lambda i: (i // nrow, 0, i % nrow, 0, 0)),
                   pl.BlockSpec((T, nkb), lambda i: (i, 0))],
        out_shape=[qshape,
                   jax.ShapeDtypeStruct((batch, A_HEADS, nrow, V_ROWS_A, T), BF16),
                   qshape,
                   jax.ShapeDtypeStruct((batch, B_KV_HEADS, seq_len // LANE, V_ROWS_D, LANE), BF16),
                   jax.ShapeDtypeStruct((M, nkb), BF16)],
        compiler_params=_params("parallel"),
        name="proj0",
    )(x, g.reshape(1, K), w, colscale.reshape(1, N))


def _bias_kernel(tab_ref, o_ref, *, nvar, off0, off_step, row_coef, col_coef, dil, half_window, head0):
    hcol = head0 + pl.program_id(1)
    R, C = o_ref.shape[-2:]
    row = lax.broadcasted_iota(jnp.int32, (R, C), 0)
    col = lax.broadcasted_iota(jnp.int32, (R, C), 1)
    base = row_coef * row + col_coef * col
    span_lo = min(row_coef * (R - 1), 0) + min(col_coef * (C - 1), 0)
    span_hi = max(row_coef * (R - 1), 0) + max(col_coef * (C - 1), 0)
    nb = NUM_BUCKETS // 2

    def side(n, n_lo, n_hi, row0):
        val = jnp.full((R, C), tab_ref[row0 + sum(t <= n_lo for t in BUCKET_THRESHOLDS), hcol], F32)
        for k, thr in enumerate(BUCKET_THRESHOLDS, start=1):
            if n_lo < thr <= n_hi:
                val = jnp.where(n >= thr, tab_ref[row0 + k, hcol], val)
        return val

    for v in range(nvar):
        @pl.when(pl.program_id(0) == v)
        def _(v=v):
            off = off0 + v * off_step
            rel = off + base
            lo, hi = (off + span_lo) * dil, (off + span_hi) * dil
            dist = rel * dil
            n = jnp.abs(dist)
            if hi <= 0:
                val = side(n, -hi, -lo, 0)
            elif lo > 0:
                val = side(n, lo, hi, nb)
            else:
                val = jnp.where(dist > 0, side(n, 1, hi, nb), side(n, 0, -lo, 0))
            val = val * LOG2E
            if half_window is not None:
                val = jnp.where(jnp.abs(rel) <= half_window, val, NEG)
            o_ref[0, 0] = val


def bias_tiles(table, *, nvar, nheads, head0, rows, cols, off0, off_step, row_coef, col_coef,
               dil, half_window, name):
    kern = functools.partial(_bias_kernel, nvar=nvar, off0=off0, off_step=off_step, row_coef=row_coef,
                             col_coef=col_coef, dil=dil, half_window=half_window, head0=head0)
    return pl.pallas_call(
        kern,
        grid=(nvar, nheads),
        in_specs=[pl.BlockSpec(memory_space=pltpu.SMEM)],
        out_specs=pl.BlockSpec((1, 1, rows, cols), lambda v, h: (v, h, 0, 0)),
        out_shape=jax.ShapeDtypeStruct((nvar, nheads, rows, cols), F32),
        compiler_params=_params("parallel", "parallel"),
        name=name,
    )(table)


def _dense_pipeline(nk, tile, n_streams, score_fn, value_fn, m_s, acc_s, bufs):
    units = [(i, c * QCOLS) for i in range(n_streams) for c in range(tile // QCOLS)]

    def stage(kc, cur, nxt):
        for i, c0 in units:
            cols = slice(c0, c0 + QCOLS)
            if nxt is not None:
                s = score_fn(kc + 1, i, cols)
                nxt[0][i, :, cols] = s
                nxt[1][i, :, cols] = jnp.max(s, axis=0, keepdims=True)
            if cur is not None:
                m_old = m_s[i, :, cols]
                m_new = jnp.maximum(m_old, cur[1][i, :, cols])
                alpha = jnp.exp2(m_old - m_new)
                p = jnp.exp2((cur[0][i, :, cols] - m_new).astype(BF16))
                acc_s[i, :, cols] = (alpha * acc_s[i, :, cols]
                                     + jnp.dot(value_fn(kc, i), p, preferred_element_type=F32))
                m_s[i, :, cols] = m_new

    m_s[...] = jnp.full(m_s.shape, NEG, F32)
    acc_s[...] = jnp.zeros(acc_s.shape, F32)
    stage(-1, None, bufs[0])
    n_loop = (nk - 1) // UNROLL

    def body(j, carry):
        for u in range(UNROLL):
            stage(UNROLL * j + u, bufs[u % 2], bufs[(u + 1) % 2])
        return carry

    lax.fori_loop(0, n_loop, body, 0)
    for kc in range(n_loop * UNROLL, nk):
        stage(kc, bufs[kc % 2], bufs[(kc + 1) % 2] if kc < nk - 1 else None)


def _attn_a_kernel(q_ref, k_ref, v_ref, bias_ref, lq1_ref, lk1_ref, lq2_ref, lk2_ref, subln_ref, o_ref,
                   m_s, acc_s, s_a, s_b, cm_a, cm_b, *, tile, nk, qtiles, lambda_init):
    qi0 = pl.program_id(2) * qtiles

    def score_fn(kc, i, cols):
        j, qt = divmod(i, qtiles)
        kblk = k_ref[0, pl.ds(pl.multiple_of(kc * tile, tile), tile), :]
        bt = bias_ref[jnp.clip(kc - (qi0 + qt), -BIAS_REACH, BIAS_REACH) + BIAS_REACH, 0, :, cols]
        return jnp.dot(kblk, q_ref[0, 0, j, qt, :, cols], preferred_element_type=F32) + bt

    def value_fn(kc, i):
        return v_ref[0, 0, kc]

    _dense_pipeline(nk, tile, 2 * qtiles, score_fn, value_fn, m_s, acc_s, ((s_a, cm_a), (s_b, cm_b)))

    lam = (jnp.exp(jnp.sum(lq1_ref[...] * lk1_ref[...], axis=-1, keepdims=True))
           - jnp.exp(jnp.sum(lq2_ref[...] * lk2_ref[...], axis=-1, keepdims=True)) + lambda_init)
    for qt in range(qtiles):
        a1, a2 = acc_s[qt], acc_s[qtiles + qt]
        o = (a1[:A_V_DIM] / a1[A_V_DIM:A_V_DIM + 1]
             - lam * (a2[:A_V_DIM] / a2[A_V_DIM:A_V_DIM + 1]))
        ms = jnp.mean(o * o, axis=0, keepdims=True)
        y = o * lax.rsqrt(ms + EPS) * subln_ref[...] * (1.0 - lambda_init)
        o_ref[0, qt * tile:(qt + 1) * tile] = y.T.astype(o_ref.dtype)


def attn_a(qt, karr, vt, bias, lq1, lk1, lq2, lk2, subln, lambda_init):
    B, S, _ = karr.shape
    T = DENSE_TILE
    nq = nk = S // T
    assert nk % 2 == 0 and nk >= 4
    qtiles = DENSE_QTILES
    ns = 2 * qtiles
    kern = functools.partial(_attn_a_kernel, tile=T, nk=nk, qtiles=qtiles, lambda_init=lambda_init)
    vec = lambda n: pl.BlockSpec((1, n), lambda b, h, i: (0, 0))
    return pl.pallas_call(
        kern,
        grid=(B, A_HEADS, nq // qtiles),
        in_specs=[pl.BlockSpec((1, 1, 2, qtiles, LANE, T), lambda b, h, i: (b, h, 0, i, 0, 0)),
                  pl.BlockSpec((1, S, LANE), lambda b, h, i: (b, 0, h)),
                  pl.BlockSpec((1, 1, nk, V_ROWS_A, T), lambda b, h, i: (b, h, 0, 0, 0)),
                  pl.BlockSpec((2 * BIAS_REACH + 1, 1, T, T), lambda b, h, i: (0, h, 0, 0)),
                  vec(HEAD_DIM), vec(HEAD_DIM), vec(HEAD_DIM), vec(HEAD_DIM),
                  pl.BlockSpec((A_V_DIM, 1), lambda b, h, i: (0, 0))],
        out_specs=pl.BlockSpec((1, qtiles * T, LANE), lambda b, h, i: (b, i, h)),
        out_shape=jax.ShapeDtypeStruct((B, S, A_HEADS * A_V_DIM), BF16),
        scratch_shapes=[pltpu.VMEM((ns, 1, T), F32), pltpu.VMEM((ns, V_ROWS_A, T), F32),
                        pltpu.VMEM((ns, T, T), F32), pltpu.VMEM((ns, T, T), F32),
                        pltpu.VMEM((ns, 1, T), F32), pltpu.VMEM((ns, 1, T), F32)],
        compiler_params=_params("parallel", "parallel", "arbitrary"),
        name="attn_a",
    )(qt, karr, vt, bias, lq1.reshape(1, -1), lk1.reshape(1, -1), lq2.reshape(1, -1),
      lk2.reshape(1, -1), subln.reshape(-1, 1))


def _attn_d_kernel(q_ref, k_ref, v_ref, o_ref, m_s, acc_s, s_a, s_b, cm_a, cm_b, *, tile, nk, qtiles):
    def score_fn(kc, i, cols):
        hh, qt = divmod(i, qtiles)
        kblk = k_ref[0, pl.ds(pl.multiple_of(kc * tile, tile), tile), hh * LANE:(hh + 1) * LANE]
        return jnp.dot(kblk, q_ref[0, hh, qt, :, cols], preferred_element_type=F32)

    def value_fn(kc, i):
        return v_ref[0, i // qtiles, kc]

    _dense_pipeline(nk, tile, 2 * qtiles, score_fn, value_fn, m_s, acc_s, ((s_a, cm_a), (s_b, cm_b)))
    for qt in range(qtiles):
        outs = []
        for hh in range(2):
            acc = acc_s[hh * qtiles + qt]
            outs.append(acc[:D_V] / acc[D_V:D_V + 1])
        o_ref[0, qt * tile:(qt + 1) * tile] = jnp.concatenate(outs, axis=0).T.astype(o_ref.dtype)


def attn_d(qt, k, vt):
    B, S, _ = k.shape
    T = DENSE_TILE
    nq = nk = S // T
    qtiles = DENSE_QTILES
    ns = 2 * qtiles
    kern = functools.partial(_attn_d_kernel, tile=T, nk=nk, qtiles=qtiles)
    return pl.pallas_call(
        kern,
        grid=(B, D_HEADS // 2, nq // qtiles),
        in_specs=[pl.BlockSpec((1, 2, qtiles, LANE, T), lambda b, h, i: (b, h, i, 0, 0)),
                  pl.BlockSpec((1, S, 2 * LANE), lambda b, h, i: (b, 0, h)),
                  pl.BlockSpec((1, 2, nk, V_ROWS_D, T), lambda b, h, i: (b, h, 0, 0, 0))],
        out_specs=pl.BlockSpec((1, qtiles * T, LANE), lambda b, h, i: (b, i, h)),
        out_shape=jax.ShapeDtypeStruct((B, S, D_HEADS * D_V), BF16),
        scratch_shapes=[pltpu.VMEM((ns, 1, T), F32), pltpu.VMEM((ns, V_ROWS_D, T), F32),
                        pltpu.VMEM((ns, T, T), F32), pltpu.VMEM((ns, T, T), F32),
                        pltpu.VMEM((ns, 1, T), F32), pltpu.VMEM((ns, 1, T), F32)],
        compiler_params=_params("parallel", "parallel", "arbitrary"),
        name="attn_d",
    )(qt, k, vt)


def _attn_b_kernel(q_ref, k_ref, v_ref, bias_ref, sink_ref, o_ref, *, tile, seq_len):
    half_window = B_HALF_WINDOW
    span = QCOLS + 2 * half_window
    nchunk = span // LANE
    step = pl.program_id(1)

    def window(c):
        q0 = step * tile + c * QCOLS
        start = pl.multiple_of(jnp.clip(q0 - half_window, 0, seq_len - span), LANE)
        variant = jnp.where(q0 == 0, 0, jnp.where(q0 + QCOLS == seq_len, 2, 1))
        return start, variant

    def scores(task):
        c, j, g = task
        start, variant = window(c)
        kwin = k_ref[0, pl.ds(start, span), :]
        s = jnp.dot(kwin, q_ref[0, j, g, 0, :, c * QCOLS:(c + 1) * QCOLS], preferred_element_type=F32)
        return s + bias_ref[variant, j + (B_HEADS // 2) * g]

    def finish(task, s):
        c, j, g = task
        head = j + (B_HEADS // 2) * g
        start, _ = window(c)
        chunk0 = start // LANE
        vwin = jnp.concatenate([v_ref[0, g, chunk0 + n] for n in range(nchunk)], axis=1)
        sk = sink_ref[:, head:head + 1] * LOG2E
        m = jnp.maximum(jnp.max(s, axis=0, keepdims=True), sk)
        e = jnp.exp2((s - m).astype(BF16))
        ov = jnp.dot(vwin, e, preferred_element_type=F32)
        denom = ov[HEAD_DIM:HEAD_DIM + 1] + jnp.exp2(sk - m)
        return ov[:HEAD_DIM] / denom

    tasks = [(c, j, g) for c in range(tile // QCOLS) for j in range(B_HEADS // 2) for g in range(B_KV_HEADS)]
    s_next = scores(tasks[0])
    held = None
    for n, task in enumerate(tasks):
        s = s_next
        if n + 1 < len(tasks):
            s_next = scores(tasks[n + 1])
        o = finish(task, s)
        c, j, g = task
        if g == 0:
            held = o
            continue
        pair = jnp.concatenate([held, o], axis=0).T
        o_ref[0, c * QCOLS:(c + 1) * QCOLS, j * LANE:(j + 1) * LANE] = pair.astype(o_ref.dtype)


def attn_b(qt, karr, k_block, vt, bias, sink):
    B, S, _ = karr.shape
    T = DENSE_TILE
    return pl.pallas_call(
        functools.partial(_attn_b_kernel, tile=T, seq_len=S),
        grid=(B, S // T),
        in_specs=[pl.BlockSpec((1, B_HEADS // 2, 2, 1, LANE, T), lambda b, i: (b, 0, 0, i, 0, 0)),
                  pl.BlockSpec((1, S, LANE), lambda b, i: (b, 0, k_block)),
                  pl.BlockSpec((1, B_KV_HEADS, S // LANE, V_ROWS_D, LANE), lambda b, i: (b, 0, 0, 0, 0)),
                  pl.BlockSpec(bias.shape, lambda b, i: (0, 0, 0, 0)),
                  pl.BlockSpec((1, B_HEADS), lambda b, i: (0, 0))],
        out_specs=pl.BlockSpec((1, T, B_HEADS * HEAD_DIM), lambda b, i: (b, i, 0)),
        out_shape=jax.ShapeDtypeStruct((B, S, B_HEADS * HEAD_DIM), BF16),
        compiler_params=_params("parallel", "arbitrary"),
        name="attn_b",
    )(qt, karr, vt, bias, sink.reshape(1, -1))


def _banded_kernel(*refs, tq, group, span, half_window, seq_len, nqb, nkb, head_of, has_sink, has_lse):
    it = iter(refs)
    q_ref, k_ref, v_ref, bias_ref = next(it), next(it), next(it), next(it)
    sink_ref = next(it) if has_sink else None
    o_ref = next(it)
    lse_ref = next(it) if has_lse else None

    nq = seq_len // tq
    lane = lax.broadcasted_iota(jnp.int32, (tq, LANE), 1)
    low = lane < HEAD_DIM

    def window(g):
        t = pl.program_id(2) * group + g
        start = pl.multiple_of(jnp.clip(t * tq - half_window, 0, seq_len - span), half_window)
        variant = jnp.where(t == 0, 0, jnp.where(t == nq - 1, 2, 1))
        return start, variant

    def scores(task):
        g, c, half = task
        start, variant = window(g)
        kc = c if nkb == nqb else 0
        q2 = q_ref[0, g * tq:(g + 1) * tq, c * LANE:(c + 1) * LANE]
        kb = k_ref[0, pl.ds(start, span), kc * LANE:(kc + 1) * LANE]
        qm = jnp.where(low if half == 0 else jnp.logical_not(low), q2, jnp.zeros_like(q2))
        s = lax.dot_general(qm, kb, (((1,), (1,)), ((), ())), preferred_element_type=F32)
        return s + bias_ref[variant, head_of(c, half)]

    def finish(task, s):
        g, c, half = task
        start, _ = window(g)
        kc = c if nkb == nqb else 0
        hidx = head_of(c, half)
        vb = v_ref[0, pl.ds(start, span), kc * LANE:(kc + 1) * LANE]
        m = jnp.max(s, axis=-1, keepdims=True)
        if has_sink:
            sk = sink_ref[:, hidx:hidx + 1] * LOG2E
            m = jnp.maximum(m, sk)
        e = jnp.exp2(s - m)
        denom = jnp.sum(e, axis=-1, keepdims=True)
        if has_sink:
            denom = denom + jnp.exp2(sk - m)
        o = jnp.dot(e.astype(BF16), vb, preferred_element_type=F32) / denom
        return o, (LN2 * m + jnp.log(denom) if has_lse else None)

    tasks = [(g, c, half) for g in range(group) for c in range(nqb) for half in range(2)]
    s_next = scores(tasks[0])
    held = None
    for n, task in enumerate(tasks):
        s = s_next
        if n + 1 < len(tasks):
            s_next = scores(tasks[n + 1])
        o, lse = finish(task, s)
        g, c, half = task
        if half == 0:
            held = (o, lse)
            continue
        rows, cols = slice(g * tq, (g + 1) * tq), slice(c * LANE, (c + 1) * LANE)
        o_ref[0, rows, cols] = jnp.where(low, held[0], o).astype(o_ref.dtype)
        if has_lse:
            lse_ref[0, rows, cols] = jnp.where(low, held[1], lse)


def banded(view, bias, sink, *, dil, tq, half_window, nqb, nkb, q_idx, k_idx, v_idx, head_of,
           out_dtype, has_lse, name):
    B, L, _ = view.shape
    span = tq + 2 * half_window
    nq = L // tq
    assert L % tq == 0 and L >= span and nq >= 2
    group = math.gcd(nq, BAND_GROUP)
    OW = nqb * LANE
    has_sink = sink is not None
    kern = functools.partial(_banded_kernel, tq=tq, group=group, span=span, half_window=half_window,
                             seq_len=L, nqb=nqb, nkb=nkb, head_of=head_of, has_sink=has_sink,
                             has_lse=has_lse)
    in_specs = [pl.BlockSpec((1, group * tq, OW), lambda b, r, t: (b, t, q_idx(r))),
                pl.BlockSpec((1, L, nkb * LANE), lambda b, r, t: (b, 0, k_idx(r))),
                pl.BlockSpec((1, L, nkb * LANE), lambda b, r, t: (b, 0, v_idx(r))),
                pl.BlockSpec(bias.shape, lambda b, r, t: (0, 0, 0, 0))]
    args = [view, view, view, bias]
    if has_sink:
        in_specs.append(pl.BlockSpec((1, sink.shape[-1]), lambda b, r, t: (0, 0)))
        args.append(sink.reshape(1, -1))
    out_spec = pl.BlockSpec((1, group * tq, OW), lambda b, r, t: (b, t, r))
    out_shapes = [jax.ShapeDtypeStruct((B, L, dil * OW), out_dtype)]
    out_specs = [out_spec]
    if has_lse:
        out_shapes.append(jax.ShapeDtypeStruct((B, L, dil * OW), F32))
        out_specs.append(out_spec)
    outs = pl.pallas_call(
        kern,
        grid=(B, dil, nq // group),
        in_specs=in_specs,
        out_specs=out_specs,
        out_shape=out_shapes,
        compiler_params=_params("parallel", "parallel", "arbitrary"),
        name=name,
    )(*args)
    return outs


def _out_proj_cd_kernel(h_ref, o0, o1, o2, s0, s1, s2, d_ref, wc_ref, wd_ref, o_ref, *scratch):
    tm = h_ref.shape[0]
    width = C_HEADS_PER_GROUP * HEAD_DIM
    spare = iter(scratch)

    def in_position_order(ref, dil):
        if dil == 1:
            return ref[0]
        t_s = next(spare)
        for r in range(dil):
            for j in range(width // LANE):
                t_s[j, pl.ds(r, tm // dil, stride=dil), :] = ref[0, :, r * width + j * LANE:r * width + (j + 1) * LANE]
        return jnp.concatenate([t_s[j] for j in range(width // LANE)], axis=1)

    dils = [d for _, d in C_PATTERNS]
    outs = [in_position_order(r, d) for r, d in zip((o0, o1, o2), dils)]
    lses = [in_position_order(r, d) for r, d in zip((s0, s1, s2), dils)]
    mx = jnp.maximum(jnp.maximum(lses[0], lses[1]), lses[2])
    es = [jnp.exp(l - mx) for l in lses]
    oc = (es[0] * outs[0] + es[1] * outs[1] + es[2] * outs[2]) / (es[0] + es[1] + es[2])
    mix = jnp.dot(oc.astype(BF16), wc_ref[...], preferred_element_type=F32)
    mix = mix + jnp.dot(d_ref[...], wd_ref[...], preferred_element_type=F32)
    o_ref[...] = h_ref[...] + mix


def out_proj_cd(h, oc, lses, od, wc, wd, batch, seq_len):
    M = h.shape[0]
    tm = ROW_TILE
    nrow = seq_len // tm
    width = C_HEADS_PER_GROUP * HEAD_DIM
    row = lambda a: pl.BlockSpec((tm, a.shape[1]), lambda i: (i, 0))
    full = lambda a: pl.BlockSpec(a.shape, lambda i: (0, 0))
    views = [pl.BlockSpec((1, tm // d, d * width), lambda i: (i // nrow, i % nrow, 0)) for _, d in C_PATTERNS]
    n_spare = 2 * sum(d > 1 for _, d in C_PATTERNS)
    return pl.pallas_call(
        _out_proj_cd_kernel,
        grid=(M // tm,),
        in_specs=[row(h), *views, *views, row(od), full(wc), full(wd)],
        out_specs=pl.BlockSpec((tm, D_MODEL), lambda i: (i, 0)),
        out_shape=jax.ShapeDtypeStruct((M, D_MODEL), F32),
        scratch_shapes=[pltpu.VMEM((width // LANE, tm, LANE), F32)] * n_spare,
        compiler_params=_params("parallel"),
        name="out_proj1",
    )(h, *oc, *lses, od, wc, wd)


def _proj1_kernel(x_ref, g_ref, w_ref, cs_ref, c0_ref, c1_ref, c2_ref, y_s):
    xn = _rms(x_ref[...], g_ref[...]).astype(BF16)
    y = jnp.dot(xn, w_ref[...], preferred_element_type=F32) * cs_ref[...]
    tm = y.shape[0]
    width = y.shape[1] // len(C_PATTERNS)
    c0_ref[0] = y[:, :width].astype(BF16)
    nblk = width // LANE
    for j in range(y_s.shape[0]):
        y_s[j] = y[:, width + j * LANE:width + (j + 1) * LANE]
    for g, ref in ((1, c1_ref), (2, c2_ref)):
        dil = C_PATTERNS[g][1]
        for r in range(dil):
            for j in range(nblk):
                rows = y_s[(g - 1) * nblk + j, pl.ds(r, tm // dil, stride=dil), :]
                ref[0, :, r * width + j * LANE:r * width + (j + 1) * LANE] = rows.astype(BF16)


def proj1_call(x, g, w, colscale, batch, seq_len):
    M, K = x.shape
    N = w.shape[1]
    tm = ROW_TILE
    nrow = seq_len // tm
    width = N // len(C_PATTERNS)
    assert [d for _, d in C_PATTERNS][0] == 1
    out_specs = [pl.BlockSpec((1, tm // d, d * width), lambda i: (i // nrow, i % nrow, 0)) for _, d in C_PATTERNS]
    out_shape = [jax.ShapeDtypeStruct((batch, seq_len // d, d * width), BF16) for _, d in C_PATTERNS]
    return pl.pallas_call(
        _proj1_kernel,
        grid=(M // tm,),
        in_specs=[pl.BlockSpec((tm, K), lambda i: (i, 0)),
                  pl.BlockSpec((1, K), lambda i: (0, 0)),
                  pl.BlockSpec((K, N), lambda i: (0, 0)),
                  pl.BlockSpec((1, N), lambda i: (0, 0))],
        out_specs=out_specs,
        out_shape=out_shape,
        scratch_shapes=[pltpu.VMEM(((N - width) // LANE, tm, LANE), F32)],
        compiler_params=_params("parallel"),
        name="proj1",
    )(x, g.reshape(1, K), w, colscale.reshape(1, N))


def _ffn_kernel(*refs, n_mix, final_norm):
    x_ref, mix_refs = refs[0], refs[1:1 + 2 * n_mix]
    g_ref, wg_ref, wu_ref, wd_ref, fg_ref, o_ref, xn_s, acc_s, h_s = refs[1 + 2 * n_mix:]
    f = pl.program_id(1)

    @pl.when(f == 0)
    def _():
        h = x_ref[...]
        for a_ref, w_ref in zip(mix_refs[::2], mix_refs[1::2]):
            h = h + jnp.dot(a_ref[...], w_ref[...], preferred_element_type=F32)
        h_s[...] = h
        xn_s[...] = _rms(h, g_ref[...]).astype(BF16)
        acc_s[...] = jnp.zeros(acc_s.shape, F32)

    xn = xn_s[...]
    gate = jnp.dot(xn, wg_ref[...], preferred_element_type=F32)
    up = jnp.dot(xn, wu_ref[...], preferred_element_type=F32)
    mid = (gate / (1.0 + jnp.exp(-gate)) * up).astype(BF16)
    acc_s[...] += jnp.dot(mid, wd_ref[...], preferred_element_type=F32)

    @pl.when(f == pl.num_programs(1) - 1)
    def _():
        y = h_s[...] + acc_s[...]
        if final_norm:
            y = _rms(y, fg_ref[...])
        o_ref[...] = y


def ffn(x, mix, g, wg, wu, wd, fg, final_norm, name):
    M, K = x.shape
    tm, tf = FFN_ROW_TILE, FFN_COL_TILE
    kern = functools.partial(_ffn_kernel, n_mix=len(mix), final_norm=final_norm)
    mix_specs, mix_args = [], []
    for a, w in mix:
        mix_specs += [pl.BlockSpec((tm, a.shape[1]), lambda i, f: (i, 0)), pl.BlockSpec(w.shape, lambda i, f: (0, 0))]
        mix_args += [a, w]
    return pl.pallas_call(
        kern,
        grid=(M // tm, D_FF // tf),
        in_specs=[pl.BlockSpec((tm, K), lambda i, f: (i, 0)), *mix_specs,
                  pl.BlockSpec((1, K), lambda i, f: (0, 0)),
                  pl.BlockSpec((K, tf), lambda i, f: (0, f)),
                  pl.BlockSpec((K, tf), lambda i, f: (0, f)),
                  pl.BlockSpec((tf, K), lambda i, f: (f, 0)),
                  pl.BlockSpec((1, K), lambda i, f: (0, 0))],
        out_specs=pl.BlockSpec((tm, K), lambda i, f: (i, 0)),
        out_shape=jax.ShapeDtypeStruct((M, K), F32),
        scratch_shapes=[pltpu.VMEM((tm, K), BF16), pltpu.VMEM((tm, K), F32), pltpu.VMEM((tm, K), F32)],
        compiler_params=_params("parallel", "arbitrary"),
        name=name,
    )(x, *mix_args, g.reshape(1, K), wg, wu, wd, fg.reshape(1, K))


def _trig_kernel(ang_ref, cos_ref, sin_ref):
    a = ang_ref[...]
    cos_ref[...] = jnp.cos(a)
    sin_ref[...] = jnp.sin(a)


def rope_tables(seq_len):
    half = D_ROPE // 2
    inv = ROPE_THETA ** (-jnp.arange(half, dtype=F32) / half)
    ang = jnp.arange(seq_len).astype(F32)[:, None] * inv[None, :]
    dense = ang.reshape(seq_len * half // LANE, LANE)
    spec = pl.BlockSpec(dense.shape, lambda: (0, 0))
    cos, sin = pl.pallas_call(
        _trig_kernel,
        in_specs=[spec],
        out_specs=[spec, spec],
        out_shape=[jax.ShapeDtypeStruct(dense.shape, F32)] * 2,
        name="rope_trig",
    )(dense)
    cos, sin = cos.reshape(seq_len, half), sin.reshape(seq_len, half)
    pad = LANE - D_NOPE - D_ROPE
    cos_l = jnp.concatenate([jnp.ones((seq_len, D_NOPE), F32), cos, cos, jnp.ones((seq_len, pad), F32)], axis=1)
    sin_l = jnp.concatenate([jnp.zeros((seq_len, D_NOPE), F32), sin, sin, jnp.zeros((seq_len, pad), F32)], axis=1)
    return cos_l, sin_l


def _prep_d_kernel(x_ref, g_ref, wa_ref, qn_ref, kvn_ref, wq_ref, wkv_ref, cos_ref, sin_ref,
                   q_ref, k_ref, v_ref, *, qscale):
    xn = _rms(x_ref[...], g_ref[...]).astype(BF16)
    lat = jnp.dot(xn, wa_ref[...], preferred_element_type=F32)
    cq = _rms(lat[:, :D_Q_LORA], qn_ref[...]).astype(BF16)
    ckv = _rms(lat[:, D_Q_LORA:D_Q_LORA + D_KV_LORA], kvn_ref[...]).astype(BF16)
    o_pe = D_Q_LORA + D_KV_LORA
    cos, sin = cos_ref[...], sin_ref[...]
    kpe = lat[:, o_pe:o_pe + LANE] * cos + lat[:, o_pe + LANE:o_pe + 2 * LANE] * sin
    qq = jnp.dot(cq, wq_ref[...], preferred_element_type=F32)
    kv = jnp.dot(ckv, wkv_ref[...], preferred_element_type=F32)
    kw = D_HEADS * LANE
    ones_blk = _ones_row_block(V_ROWS_D - D_V, x_ref.shape[0])
    for h in range(D_HEADS):
        qh = qq[:, h * LANE:(h + 1) * LANE] * cos + qq[:, kw + h * LANE:kw + (h + 1) * LANE] * sin
        q_ref[0, h, 0] = (qh * qscale).T.astype(BF16)
        k_ref[:, h * LANE:(h + 1) * LANE] = (kv[:, h * LANE:(h + 1) * LANE] + kpe).astype(BF16)
    for j in range(D_HEADS // 2):
        vt = kv[:, kw + j * LANE:kw + (j + 1) * LANE].T.astype(BF16)
        for half in range(2):
            v_ref[0, 2 * j + half, 0, :D_V] = vt[half * D_V:(half + 1) * D_V]
            v_ref[0, 2 * j + half, 0, D_V:] = ones_blk


def prep_d(x, g, wa, qn, kvn, wq, wkv, cos_l, sin_l, batch, seq_len):
    M, K = x.shape
    T = DENSE_TILE
    nrow = seq_len // T
    kw = D_HEADS * LANE
    kern = functools.partial(_prep_d_kernel, qscale=(D_NOPE + D_ROPE) ** -0.5 * LOG2E)
    full = lambda a: pl.BlockSpec(a.shape, lambda i: (0, 0))
    qn2, kvn2, g2 = qn.reshape(1, -1), kvn.reshape(1, -1), g.reshape(1, K)
    return pl.pallas_call(
        kern,
        grid=(M // T,),
        in_specs=[pl.BlockSpec((T, K), lambda i: (i, 0)), full(g2), full(wa), full(qn2), full(kvn2),
                  full(wq), full(wkv),
                  pl.BlockSpec((T, LANE), lambda i: (i % nrow, 0)),
                  pl.BlockSpec((T, LANE), lambda i: (i % nrow, 0))],
        out_specs=[pl.BlockSpec((1, D_HEADS, 1, LANE, T), lambda i: (i // nrow, 0, i % nrow, 0, 0)),
                   pl.BlockSpec((T, kw), lambda i: (i, 0)),
                   pl.BlockSpec((1, D_HEADS, 1, V_ROWS_D, T), lambda i: (i // nrow, 0, i % nrow, 0, 0))],
        out_shape=[jax.ShapeDtypeStruct((batch, D_HEADS, nrow, LANE, T), BF16),
                   jax.ShapeDtypeStruct((M, kw), BF16),
                   jax.ShapeDtypeStruct((batch, D_HEADS, nrow, V_ROWS_D, T), BF16)],
        compiler_params=_params("parallel"),
        name="prep_d",
    )(x, g2, wa, qn2, kvn2, wq, wkv, cos_l, sin_l)


B_HEAD_ORDER = (0, 4, 1, 5, 2, 6, 3, 7)


def _head_cols(order, base):
    return np.concatenate([np.arange(base + h * HEAD_DIM, base + (h + 1) * HEAD_DIM) for h in order])


def _rot_partner_cols(w):
    half = D_ROPE // 2
    return jnp.concatenate([-w[..., half:], w[..., :half]], axis=-1)


def kernel(x, bias_table, attn_norm, ffn_norm, final_norm, ab_w_in, ab_lambda_q1, ab_lambda_k1,
           ab_lambda_q2, ab_lambda_k2, ab_subln, ab_sink, ab_w_o, cd_w_in, cd_q_norm, cd_w_q_b,
           cd_kv_norm, cd_w_kv_b, cd_w_o, ffn_w_gate, ffn_w_up, ffn_w_down):
    B, S, _ = x.shape
    M = B * S
    T = DENSE_TILE
    h = x.reshape(M, D_MODEL)
    qk_scale = HEAD_DIM ** -0.5 * LOG2E

    o3 = A_HEADS * (2 * A_QK_DIM + A_V_DIM)
    cols0 = np.concatenate([np.arange(o3), _head_cols(B_HEAD_ORDER, o3),
                            np.arange(o3 + B_HEADS * HEAD_DIM, AB_IN)])
    w0 = ab_w_in[0][:, cols0].astype(BF16)
    cs0 = np.ones((AB_IN,), np.float32)
    cs0[:A_HEADS * A_QK_DIM] = qk_scale
    cs0[o3:o3 + B_HEADS * HEAD_DIM] = qk_scale
    qat, vat, qbt, vbt, keys0 = proj0_call(h, attn_norm[0], w0, jnp.asarray(cs0), B, S)
    keys0 = keys0.reshape(B, S, -1)
    bias_a = bias_tiles(bias_table, nvar=2 * BIAS_REACH + 1, nheads=A_HEADS, head0=0, rows=T, cols=T,
                        off0=-BIAS_REACH * T, off_step=T, row_coef=1, col_coef=-1, dil=1,
                        half_window=None, name="bias_a")
    oa = attn_a(qat, keys0, vat, bias_a, ab_lambda_q1[0], ab_lambda_k1[0], ab_lambda_q2[0],
                ab_lambda_k2[0], ab_subln[0], 0.8 - 0.6 * math.exp(-0.3 * 0))

    bias_b = bias_tiles(bias_table, nvar=3, nheads=B_HEADS, head0=A_HEADS, rows=QCOLS + 2 * B_HALF_WINDOW,
                        cols=QCOLS, off0=0, off_step=-B_HALF_WINDOW, row_coef=1, col_coef=-1, dil=1,
                        half_window=B_HALF_WINDOW, name="bias_b")
    ob = attn_b(qbt, keys0, (A_HEADS * A_QK_DIM) // LANE, vbt, bias_b, ab_sink[0])

    wo = ab_w_o[0]
    wo_a = wo[:A_HEADS * A_V_DIM].astype(BF16)
    wo_b = wo[_head_cols(B_HEAD_ORDER, A_HEADS * A_V_DIM)].astype(BF16)
    h = ffn(h, [(oa.reshape(M, -1), wo_a), (ob.reshape(M, -1), wo_b)], ffn_norm[0], ffn_w_gate[0].astype(BF16), ffn_w_up[0].astype(BF16),
            ffn_w_down[0].astype(BF16), final_norm, False, "ffn0")

    w1 = cd_w_in[0]
    gw = C_HEADS_PER_GROUP * HEAD_DIM
    cw = C_HEADS * HEAD_DIM
    cols1 = np.concatenate([np.arange(role * cw + g * gw, role * cw + (g + 1) * gw)
                            for g in range(len(C_PATTERNS)) for role in range(3)])
    cs1 = np.ones((CD_C_IN,), np.float32)
    for g in range(len(C_PATTERNS)):
        cs1[3 * g * gw:(3 * g + 1) * gw] = qk_scale
    c_views = proj1_call(h, attn_norm[1], w1[:, cols1].astype(BF16), jnp.asarray(cs1), B, S)

    oc, lses = [], []
    for g, (window, dil) in enumerate(C_PATTERNS):
        hw = window // (2 * dil)
        tq_c = 128
        bias_c = bias_tiles(bias_table, nvar=3, nheads=C_HEADS_PER_GROUP, head0=g * C_HEADS_PER_GROUP,
                            rows=tq_c, cols=tq_c + 2 * hw, off0=0, off_step=-hw, row_coef=-1, col_coef=1,
                            dil=dil, half_window=hw, name=f"bias_c{g}")
        o_g, lse_g = banded(c_views[g], bias_c, None, dil=dil, tq=tq_c, half_window=hw, nqb=2, nkb=2,
                            q_idx=lambda r: 3 * r, k_idx=lambda r: 3 * r + 1, v_idx=lambda r: 3 * r + 2,
                            head_of=lambda c, half: 2 * c + half, out_dtype=F32, has_lse=True,
                            name=f"attn_c{g}")
        oc.append(o_g)
        lses.append(lse_g)

    o_q, o_kv = CD_C_IN + D_Q_LORA, CD_C_IN + D_Q_LORA + D_KV_LORA
    w_pe = w1[:, o_kv:]
    lane_pad = lambda w: jnp.pad(w, ((0, 0), (D_NOPE, LANE - D_NOPE - D_ROPE)))
    wa = jnp.concatenate([w1[:, CD_C_IN:o_kv], lane_pad(w_pe), lane_pad(_rot_partner_cols(w_pe))],
                         axis=1).astype(BF16)
    wq3 = cd_w_q_b[0].reshape(D_Q_LORA, D_HEADS, D_NOPE + D_ROPE)
    zpad = jnp.zeros((D_Q_LORA, D_HEADS, LANE - D_NOPE - D_ROPE), F32)
    wq_main = jnp.concatenate([wq3, zpad], axis=-1)
    wq_rot = jnp.concatenate([jnp.zeros_like(wq3[..., :D_NOPE]), _rot_partner_cols(wq3[..., D_NOPE:]), zpad],
                             axis=-1)
    wq = jnp.concatenate([wq_main.reshape(D_Q_LORA, -1), wq_rot.reshape(D_Q_LORA, -1)], axis=1).astype(BF16)
    wkv3 = cd_w_kv_b[0].reshape(D_KV_LORA, D_HEADS, D_NOPE + D_V)
    wk = jnp.pad(wkv3[..., :D_NOPE], ((0, 0), (0, 0), (0, LANE - D_NOPE))).reshape(D_KV_LORA, -1)
    wv = wkv3[..., D_NOPE:].reshape(D_KV_LORA, -1)
    wkv = jnp.concatenate([wk, wv], axis=1).astype(BF16)
    cos_l, sin_l = rope_tables(S)
    qdt, kd, vdt = prep_d(h, attn_norm[1], wa, cd_q_norm[0], cd_kv_norm[0], wq, wkv, cos_l, sin_l, B, S)
    od = attn_d(qdt, kd.reshape(B, S, D_HEADS * LANE), vdt)

    wo1 = cd_w_o[0]
    wo_c = wo1[:C_HEADS_PER_GROUP * HEAD_DIM].astype(BF16)
    wo_d = wo1[C_HEADS_PER_GROUP * HEAD_DIM:].astype(BF16)
    h = out_proj_cd(h, oc, lses, od.reshape(M, -1), wo_c, wo_d, B, S)
    h = ffn(h, [], ffn_norm[1], ffn_w_gate[1].astype(BF16), ffn_w_up[1].astype(BF16),
            ffn_w_down[1].astype(BF16), final_norm, True, "ffn1")
    return h.reshape(B, S, D_MODEL)
```

```python
import functools
import math

import numpy as np
import jax
import jax.numpy as jnp
from jax import lax
from jax.experimental import pallas as pl
from jax.experimental.pallas import tpu as pltpu

F32 = jnp.float32
BF16 = jnp.bfloat16

D_MODEL = 1024
HEAD_DIM = 64
EPS = 1e-6
NEG = -1e30
LOG2E = math.log2(math.e)
LN2 = math.log(2.0)

A_HEADS = 4
A_QK_DIM = 2 * HEAD_DIM
A_V_DIM = 2 * HEAD_DIM
B_HEADS = 8
B_KV_HEADS = 2
B_HALF_WINDOW = 128
C_PATTERNS = ((128, 1), (512, 4), (2048, 16))
C_HEADS_PER_GROUP = 4
C_HEADS = C_HEADS_PER_GROUP * len(C_PATTERNS)
D_HEADS = 12
D_Q_LORA = 384
D_KV_LORA = 256
D_NOPE = 64
D_ROPE = 32
D_V = 64
ROPE_THETA = 10000.0
NUM_BUCKETS = 32
MAX_DISTANCE = 1024
D_FF = 2816
AB_IN = 2304
CD_C_IN = 3 * C_HEADS * HEAD_DIM

LANE = 128
VMEM_LIMIT = 48 * 1024 * 1024

ROW_TILE = 512
FFN_ROW_TILE = 1024
FFN_COL_TILE = 256
DENSE_TILE = 512
QCOLS = 256
UNROLL = 4
DENSE_QTILES = 2
BAND_GROUP = 4
BIAS_REACH = 3
V_ROWS_D = 80
V_ROWS_A = 144


def _bucket_thresholds():
    nb = NUM_BUCKETS // 2
    max_exact = nb // 2
    n = np.arange(1, 4 * MAX_DISTANCE)
    large = max_exact + (np.log(n.astype(np.float32) / np.float32(max_exact))
                         / np.float32(math.log(MAX_DISTANCE / max_exact))
                         * np.float32(nb - max_exact)).astype(np.int32)
    mag = np.where(n < max_exact, n, np.minimum(large, nb - 1))
    return tuple(int(n[np.argmax(mag >= k)]) for k in range(1, nb))


BUCKET_THRESHOLDS = _bucket_thresholds()
assert BUCKET_THRESHOLDS[-1] <= (BIAS_REACH - 1) * DENSE_TILE + 1


def _params(*sem):
    return pltpu.CompilerParams(dimension_semantics=sem, vmem_limit_bytes=VMEM_LIMIT)


def _rms(x, g):
    return x * lax.rsqrt(jnp.mean(x * x, axis=-1, keepdims=True) + EPS) * g


def _ones_row_block(rows, cols):
    r = lax.broadcasted_iota(jnp.int32, (rows, cols), 0)
    return jnp.where(r == 0, 1.0, 0.0).astype(BF16)


def _store_masked_halves(q_ref, idx, qt):
    zeros = jnp.zeros((HEAD_DIM, qt.shape[1]), BF16)
    q_ref[idx + (0, 0, slice(None, HEAD_DIM))] = qt[:HEAD_DIM]
    q_ref[idx + (0, 0, slice(HEAD_DIM, None))] = zeros
    q_ref[idx + (1, 0, slice(None, HEAD_DIM))] = zeros
    q_ref[idx + (1, 0, slice(HEAD_DIM, None))] = qt[HEAD_DIM:]


def _proj0_kernel(x_ref, g_ref, w_ref, cs_ref, qa_ref, va_ref, qb_ref, vb_ref, k_ref):
    xn = _rms(x_ref[...], g_ref[...]).astype(BF16)
    y = jnp.dot(xn, w_ref[...], preferred_element_type=F32) * cs_ref[...]
    tm = y.shape[0]
    ka0 = A_HEADS * A_QK_DIM
    va0 = 2 * ka0
    qb0 = va0 + A_HEADS * A_V_DIM
    kb0 = qb0 + B_HEADS * HEAD_DIM
    vb0 = kb0 + B_KV_HEADS * HEAD_DIM
    for h in range(A_HEADS):
        _store_masked_halves(qa_ref, (0, h), y[:, h * A_QK_DIM:(h + 1) * A_QK_DIM].T.astype(BF16))
        va_ref[0, h, 0, :A_V_DIM] = y[:, va0 + h * A_V_DIM:va0 + (h + 1) * A_V_DIM].T.astype(BF16)
        va_ref[0, h, 0, A_V_DIM:] = _ones_row_block(V_ROWS_A - A_V_DIM, tm)
    for j in range(B_HEADS // 2):
        _store_masked_halves(qb_ref, (0, j), y[:, qb0 + j * LANE:qb0 + (j + 1) * LANE].T.astype(BF16))
    vbt = y[:, vb0:vb0 + LANE].T.astype(BF16)
    ones_blk = _ones_row_block(V_ROWS_D - HEAD_DIM, LANE)
    for g in range(B_KV_HEADS):
        for c in range(tm // LANE):
            vb_ref[0, g, c, :HEAD_DIM] = vbt[g * HEAD_DIM:(g + 1) * HEAD_DIM, c * LANE:(c + 1) * LANE]
            vb_ref[0, g, c, HEAD_DIM:] = ones_blk
    k_ref[:, :ka0] = y[:, ka0:va0].astype(BF16)
    k_ref[:, ka0:] = y[:, kb0:vb0].astype(BF16)


def proj0_call(x, g, w, colscale, batch, seq_len):
    M, K = x.shape
    N = w.shape[1]
    T = DENSE_TILE
    nrow = seq_len // T
    nkb = A_HEADS * A_QK_DIM + B_KV_HEADS * HEAD_DIM
    qspec = pl.BlockSpec((1, A_HEADS, 2, 1, LANE, T), lambda i: (i // nrow, 0, 0, i % nrow, 0, 0))
    qshape = jax.ShapeDtypeStruct((batch, A_HEADS, 2, nrow, LANE, T), BF16)
    return pl.pallas_call(
        _proj0_kernel,
        grid=(M // T,),
        in_specs=[pl.BlockSpec((T, K), lambda i: (i, 0)),
                  pl.BlockSpec((1, K), lambda i: (0, 0)),
                  pl.BlockSpec((K, N), lambda i: (0, 0)),
                  pl.BlockSpec((1, N), lambda i: (0, 0))],
        out_specs=[qspec,
                   pl.BlockSpec((1, A_HEADS, 1, V_ROWS_A, T), lambda i: (i // nrow, 0, i % nrow, 0, 0)),
                   qspec,
                   pl.BlockSpec((1, B_KV_HEADS, T // LANE, V_ROWS_D, LANE),
                                lambda i: (i // nrow, 0, i % nrow, 0, 0)),
                   pl.BlockSpec((T, nkb), lambda i: (i, 0))],
        out_shape=[qshape,
                   jax.ShapeDtypeStruct((batch, A_HEADS, nrow, V_ROWS_A, T), BF16),
                   qshape,
                   jax.ShapeDtypeStruct((batch, B_KV_HEADS, seq_len // LANE, V_ROWS_D, LANE), BF16),
                   jax.ShapeDtypeStruct((M, nkb), BF16)],
        compiler_params=_params("parallel"),
        name="proj0",
    )(x, g.reshape(1, K), w, colscale.reshape(1, N))


def _bias_kernel(tab_ref, o_ref, *, nvar, off0, off_step, row_coef, col_coef, dil, half_window, head0):
    hcol = head0 + pl.program_id(1)
    R, C = o_ref.shape[-2:]
    row = lax.broadcasted_iota(jnp.int32, (R, C), 0)
    col = lax.broadcasted_iota(jnp.int32, (R, C), 1)
    base = row_coef * row + col_coef * col
    span_lo = min(row_coef * (R - 1), 0) + min(col_coef * (C - 1), 0)
    span_hi = max(row_coef * (R - 1), 0) + max(col_coef * (C - 1), 0)
    nb = NUM_BUCKETS // 2

    def side(n, n_lo, n_hi, row0):
        val = jnp.full((R, C), tab_ref[row0 + sum(t <= n_lo for t in BUCKET_THRESHOLDS), hcol], F32)
        for k, thr in enumerate(BUCKET_THRESHOLDS, start=1):
            if n_lo < thr <= n_hi:
                val = jnp.where(n >= thr, tab_ref[row0 + k, hcol], val)
        return val

    for v in range(nvar):
        @pl.when(pl.program_id(0) == v)
        def _(v=v):
            off = off0 + v * off_step
            rel = off + base
            lo, hi = (off + span_lo) * dil, (off + span_hi) * dil
            dist = rel * dil
            n = jnp.abs(dist)
            if hi <= 0:
                val = side(n, -hi, -lo, 0)
            elif lo > 0:
                val = side(n, lo, hi, nb)
            else:
                val = jnp.where(dist > 0, side(n, 1, hi, nb), side(n, 0, -lo, 0))
            val = val * LOG2E
            if half_window is not None:
                val = jnp.where(jnp.abs(rel) <= half_window, val, NEG)
            o_ref[0, 0] = val


def bias_tiles(table, *, nvar, nheads, head0, rows, cols, off0, off_step, row_coef, col_coef,
               dil, half_window, name):
    kern = functools.partial(_bias_kernel, nvar=nvar, off0=off0, off_step=off_step, row_coef=row_coef,
                             col_coef=col_coef, dil=dil, half_window=half_window, head0=head0)
    return pl.pallas_call(
        kern,
        grid=(nvar, nheads),
        in_specs=[pl.BlockSpec(memory_space=pltpu.SMEM)],
        out_specs=pl.BlockSpec((1, 1, rows, cols), lambda v, h: (v, h, 0, 0)),
        out_shape=jax.ShapeDtypeStruct((nvar, nheads, rows, cols), F32),
        compiler_params=_params("parallel", "parallel"),
        name=name,
    )(table)


def _dense_pipeline(nk, tile, n_streams, score_fn, value_fn, m_s, acc_s, bufs):
    units = [(i, c * QCOLS) for i in range(n_streams) for c in range(tile // QCOLS)]

    def stage(kc, cur, nxt):
        for i, c0 in units:
            cols = slice(c0, c0 + QCOLS)
            if nxt is not None:
                s = score_fn(kc + 1, i, cols)
                nxt[0][i, :, cols] = s
                nxt[1][i, :, cols] = jnp.max(s, axis=0, keepdims=True)
            if cur is not None:
                m_old = m_s[i, :, cols]
                m_new = jnp.maximum(m_old, cur[1][i, :, cols])
                alpha = jnp.exp2(m_old - m_new)
                p = jnp.exp2((cur[0][i, :, cols] - m_new).astype(BF16))
                acc_s[i, :, cols] = (alpha * acc_s[i, :, cols]
                                     + jnp.dot(value_fn(kc, i), p, preferred_element_type=F32))
                m_s[i, :, cols] = m_new

    m_s[...] = jnp.full(m_s.shape, NEG, F32)
    acc_s[...] = jnp.zeros(acc_s.shape, F32)
    stage(-1, None, bufs[0])
    n_loop = (nk - 1) // UNROLL

    def body(j, carry):
        for u in range(UNROLL):
            stage(UNROLL * j + u, bufs[u % 2], bufs[(u + 1) % 2])
        return carry

    lax.fori_loop(0, n_loop, body, 0)
    for kc in range(n_loop * UNROLL, nk):
        stage(kc, bufs[kc % 2], bufs[(kc + 1) % 2] if kc < nk - 1 else None)


def _attn_a_kernel(q_ref, k_ref, v_ref, bias_ref, lq1_ref, lk1_ref, lq2_ref, lk2_ref, subln_ref, o_ref,
                   m_s, acc_s, s_a, s_b, cm_a, cm_b, *, tile, nk, qtiles, lambda_init):
    qi0 = pl.program_id(2) * qtiles

    def score_fn(kc, i, cols):
        j, qt = divmod(i, qtiles)
        kblk = k_ref[0, pl.ds(pl.multiple_of(kc * tile, tile), tile), :]
        bt = bias_ref[jnp.clip(kc - (qi0 + qt), -BIAS_REACH, BIAS_REACH) + BIAS_REACH, 0, :, cols]
        return jnp.dot(kblk, q_ref[0, 0, j, qt, :, cols], preferred_element_type=F32) + bt

    def value_fn(kc, i):
        return v_ref[0, 0, kc]

    _dense_pipeline(nk, tile, 2 * qtiles, score_fn, value_fn, m_s, acc_s, ((s_a, cm_a), (s_b, cm_b)))

    lam = (jnp.exp(jnp.sum(lq1_ref[...] * lk1_ref[...], axis=-1, keepdims=True))
           - jnp.exp(jnp.sum(lq2_ref[...] * lk2_ref[...], axis=-1, keepdims=True)) + lambda_init)
    for qt in range(qtiles):
        a1, a2 = acc_s[qt], acc_s[qtiles + qt]
        o = (a1[:A_V_DIM] / a1[A_V_DIM:A_V_DIM + 1]
             - lam * (a2[:A_V_DIM] / a2[A_V_DIM:A_V_DIM + 1]))
        ms = jnp.mean(o * o, axis=0, keepdims=True)
        y = o * lax.rsqrt(ms + EPS) * subln_ref[...] * (1.0 - lambda_init)
        o_ref[0, qt * tile:(qt + 1) * tile] = y.T.astype(o_ref.dtype)


def attn_a(qt, karr, vt, bias, lq1, lk1, lq2, lk2, subln, lambda_init):
    B, S, _ = karr.shape
    T = DENSE_TILE
    nq = nk = S // T
    assert nk % 2 == 0 and nk >= 4
    qtiles = DENSE_QTILES
    ns = 2 * qtiles
    kern = functools.partial(_attn_a_kernel, tile=T, nk=nk, qtiles=qtiles, lambda_init=lambda_init)
    vec = lambda n: pl.BlockSpec((1, n), lambda b, h, i: (0, 0))
    return pl.pallas_call(
        kern,
        grid=(B, A_HEADS, nq // qtiles),
        in_specs=[pl.BlockSpec((1, 1, 2, qtiles, LANE, T), lambda b, h, i: (b, h, 0, i, 0, 0)),
                  pl.BlockSpec((1, S, LANE), lambda b, h, i: (b, 0, h)),
                  pl.BlockSpec((1, 1, nk, V_ROWS_A, T), lambda b, h, i: (b, h, 0, 0, 0)),
                  pl.BlockSpec((2 * BIAS_REACH + 1, 1, T, T), lambda b, h, i: (0, h, 0, 0)),
                  vec(HEAD_DIM), vec(HEAD_DIM), vec(HEAD_DIM), vec(HEAD_DIM),
                  pl.BlockSpec((A_V_DIM, 1), lambda b, h, i: (0, 0))],
        out_specs=pl.BlockSpec((1, qtiles * T, LANE), lambda b, h, i: (b, i, h)),
        out_shape=jax.ShapeDtypeStruct((B, S, A_HEADS * A_V_DIM), BF16),
        scratch_shapes=[pltpu.VMEM((ns, 1, T), F32), pltpu.VMEM((ns, V_ROWS_A, T), F32),
                        pltpu.VMEM((ns, T, T), F32), pltpu.VMEM((ns, T, T), F32),
                        pltpu.VMEM((ns, 1, T), F32), pltpu.VMEM((ns, 1, T), F32)],
        compiler_params=_params("parallel", "parallel", "arbitrary"),
        name="attn_a",
    )(qt, karr, vt, bias, lq1.reshape(1, -1), lk1.reshape(1, -1), lq2.reshape(1, -1),
      lk2.reshape(1, -1), subln.reshape(-1, 1))


def _attn_d_kernel(q_ref, k_ref, v_ref, o_ref, m_s, acc_s, s_a, s_b, cm_a, cm_b, *, tile, nk, qtiles):
    def score_fn(kc, i, cols):
        hh, qt = divmod(i, qtiles)
        kblk = k_ref[0, pl.ds(pl.multiple_of(kc * tile, tile), tile), hh * LANE:(hh + 1) * LANE]
        return jnp.dot(kblk, q_ref[0, hh, qt, :, cols], preferred_element_type=F32)

    def value_fn(kc, i):
        return v_ref[0, i // qtiles, kc]

    _dense_pipeline(nk, tile, 2 * qtiles, score_fn, value_fn, m_s, acc_s, ((s_a, cm_a), (s_b, cm_b)))
    for qt in range(qtiles):
        outs = []
        for hh in range(2):
            acc = acc_s[hh * qtiles + qt]
            outs.append(acc[:D_V] / acc[D_V:D_V + 1])
        o_ref[0, qt * tile:(qt + 1) * tile] = jnp.concatenate(outs, axis=0).T.astype(o_ref.dtype)


def attn_d(qt, k, vt):
    B, S, _ = k.shape
    T = DENSE_TILE
    nq = nk = S // T
    qtiles = DENSE_QTILES
    ns = 2 * qtiles
    kern = functools.partial(_attn_d_kernel, tile=T, nk=nk, qtiles=qtiles)
    return pl.pallas_call(
        kern,
        grid=(B, D_HEADS // 2, nq // qtiles),
        in_specs=[pl.BlockSpec((1, 2, qtiles, LANE, T), lambda b, h, i: (b, h, i, 0, 0)),
                  pl.BlockSpec((1, S, 2 * LANE), lambda b, h, i: (b, 0, h)),
                  pl.BlockSpec((1, 2, nk, V_ROWS_D, T), lambda b, h, i: (b, h, 0, 0, 0))],
        out_specs=pl.BlockSpec((1, qtiles * T, LANE), lambda b, h, i: (b, i, h)),
        out_shape=jax.ShapeDtypeStruct((B, S, D_HEADS * D_V), BF16),
        scratch_shapes=[pltpu.VMEM((ns, 1, T), F32), pltpu.VMEM((ns, V_ROWS_D, T), F32),
                        pltpu.VMEM((ns, T, T), F32), pltpu.VMEM((ns, T, T), F32),
                        pltpu.VMEM((ns, 1, T), F32), pltpu.VMEM((ns, 1, T), F32)],
        compiler_params=_params("parallel", "parallel", "arbitrary"),
        name="attn_d",
    )(qt, k, vt)


def _attn_b_kernel(q_ref, k_ref, v_ref, bias_ref, sink_ref, o_ref, *, tile, seq_len):
    half_window = B_HALF_WINDOW
    span = QCOLS + 2 * half_window
    nchunk = span // LANE
    step = pl.program_id(1)

    def window(c):
        q0 = step * tile + c * QCOLS
        start = pl.multiple_of(jnp.clip(q0 - half_window, 0, seq_len - span), LANE)
        variant = jnp.where(q0 == 0, 0, jnp.where(q0 + QCOLS == seq_len, 2, 1))
        return start, variant

    def scores(task):
        c, j, g = task
        start, variant = window(c)
        kwin = k_ref[0, pl.ds(start, span), :]
        s = jnp.dot(kwin, q_ref[0, j, g, 0, :, c * QCOLS:(c + 1) * QCOLS], preferred_element_type=F32)
        return s + bias_ref[variant, j + (B_HEADS // 2) * g]

    def finish(task, s):
        c, j, g = task
        head = j + (B_HEADS // 2) * g
        start, _ = window(c)
        chunk0 = start // LANE
        vwin = jnp.concatenate([v_ref[0, g, chunk0 + n] for n in range(nchunk)], axis=1)
        sk = sink_ref[:, head:head + 1] * LOG2E
        m = jnp.maximum(jnp.max(s, axis=0, keepdims=True), sk)
        e = jnp.exp2((s - m).astype(BF16))
        ov = jnp.dot(vwin, e, preferred_element_type=F32)
        denom = ov[HEAD_DIM:HEAD_DIM + 1] + jnp.exp2(sk - m)
        return ov[:HEAD_DIM] / denom

    tasks = [(c, j, g) for c in range(tile // QCOLS) for j in range(B_HEADS // 2) for g in range(B_KV_HEADS)]
    s_next = scores(tasks[0])
    held = None
    for n, task in enumerate(tasks):
        s = s_next
        if n + 1 < len(tasks):
            s_next = scores(tasks[n + 1])
        o = finish(task, s)
        c, j, g = task
        if g == 0:
            held = o
            continue
        pair = jnp.concatenate([held, o], axis=0).T
        o_ref[0, c * QCOLS:(c + 1) * QCOLS, j * LANE:(j + 1) * LANE] = pair.astype(o_ref.dtype)


def attn_b(qt, karr, k_block, vt, bias, sink):
    B, S, _ = karr.shape
    T = DENSE_TILE
    return pl.pallas_call(
        functools.partial(_attn_b_kernel, tile=T, seq_len=S),
        grid=(B, S // T),
        in_specs=[pl.BlockSpec((1, B_HEADS // 2, 2, 1, LANE, T), lambda b, i: (b, 0, 0, i, 0, 0)),
                  pl.BlockSpec((1, S, LANE), lambda b, i: (b, 0, k_block)),
                  pl.BlockSpec((1, B_KV_HEADS, S // LANE, V_ROWS_D, LANE), lambda b, i: (b, 0, 0, 0, 0)),
                  pl.BlockSpec(bias.shape, lambda b, i: (0, 0, 0, 0)),
                  pl.BlockSpec((1, B_HEADS), lambda b, i: (0, 0))],
        out_specs=pl.BlockSpec((1, T, B_HEADS * HEAD_DIM), lambda b, i: (b, i, 0)),
        out_shape=jax.ShapeDtypeStruct((B, S, B_HEADS * HEAD_DIM), BF16),
        compiler_params=_params("parallel", "arbitrary"),
        name="attn_b",
    )(qt, karr, vt, bias, sink.reshape(1, -1))


def _banded_kernel(*refs, tq, group, span, half_window, seq_len, nqb, nkb, head_of, has_sink, has_lse):
    it = iter(refs)
    q_ref, k_ref, v_ref, bias_ref = next(it), next(it), next(it), next(it)
    sink_ref = next(it) if has_sink else None
    o_ref = next(it)
    lse_ref = next(it) if has_lse else None

    nq = seq_len // tq
    lane = lax.broadcasted_iota(jnp.int32, (tq, LANE), 1)
    low = lane < HEAD_DIM

    def window(g):
        t = pl.program_id(2) * group + g
        start = pl.multiple_of(jnp.clip(t * tq - half_window, 0, seq_len - span), half_window)
        variant = jnp.where(t == 0, 0, jnp.where(t == nq - 1, 2, 1))
        return start, variant

    def scores(task):
        g, c, half = task
        start, variant = window(g)
        kc = c if nkb == nqb else 0
        q2 = q_ref[0, g * tq:(g + 1) * tq, c * LANE:(c + 1) * LANE]
        kb = k_ref[0, pl.ds(start, span), kc * LANE:(kc + 1) * LANE]
        qm = jnp.where(low if half == 0 else jnp.logical_not(low), q2, jnp.zeros_like(q2))
        s = lax.dot_general(qm, kb, (((1,), (1,)), ((), ())), preferred_element_type=F32)
        return s + bias_ref[variant, head_of(c, half)]

    def finish(task, s):
        g, c, half = task
        start, _ = window(g)
        kc = c if nkb == nqb else 0
        hidx = head_of(c, half)
        vb = v_ref[0, pl.ds(start, span), kc * LANE:(kc + 1) * LANE]
        m = jnp.max(s, axis=-1, keepdims=True)
        if has_sink:
            sk = sink_ref[:, hidx:hidx + 1] * LOG2E
            m = jnp.maximum(m, sk)
        e = jnp.exp2(s - m)
        denom = jnp.sum(e, axis=-1, keepdims=True)
        if has_sink:
            denom = denom + jnp.exp2(sk - m)
        o = jnp.dot(e.astype(BF16), vb, preferred_element_type=F32) / denom
        return o, (LN2 * m + jnp.log(denom) if has_lse else None)

    tasks = [(g, c, half) for g in range(group) for c in range(nqb) for half in range(2)]
    s_next = scores(tasks[0])
    held = None
    for n, task in enumerate(tasks):
        s = s_next
        if n + 1 < len(tasks):
            s_next = scores(tasks[n + 1])
        o, lse = finish(task, s)
        g, c, half = task
        if half == 0:
            held = (o, lse)
            continue
        rows, cols = slice(g * tq, (g + 1) * tq), slice(c * LANE, (c + 1) * LANE)
        o_ref[0, rows, cols] = jnp.where(low, held[0], o).astype(o_ref.dtype)
        if has_lse:
            lse_ref[0, rows, cols] = jnp.where(low, held[1], lse)


def banded(view, bias, sink, *, dil, tq, half_window, nqb, nkb, q_idx, k_idx, v_idx, head_of,
           out_dtype, has_lse, name):
    B, L, _ = view.shape
    span = tq + 2 * half_window
    nq = L // tq
    assert L % tq == 0 and L >= span and nq >= 2
    group = math.gcd(nq, BAND_GROUP)
    OW = nqb * LANE
    has_sink = sink is not None
    kern = functools.partial(_banded_kernel, tq=tq, group=group, span=span, half_window=half_window,
                             seq_len=L, nqb=nqb, nkb=nkb, head_of=head_of, has_sink=has_sink,
                             has_lse=has_lse)
    in_specs = [pl.BlockSpec((1, group * tq, OW), lambda b, r, t: (b, t, q_idx(r))),
                pl.BlockSpec((1, L, nkb * LANE), lambda b, r, t: (b, 0, k_idx(r))),
                pl.BlockSpec((1, L, nkb * LANE), lambda b, r, t: (b, 0, v_idx(r))),
                pl.BlockSpec(bias.shape, lambda b, r, t: (0, 0, 0, 0))]
    args = [view, view, view, bias]
    if has_sink:
        in_specs.append(pl.BlockSpec((1, sink.shape[-1]), lambda b, r, t: (0, 0)))
        args.append(sink.reshape(1, -1))
    out_spec = pl.BlockSpec((1, group * tq, OW), lambda b, r, t: (b, t, r))
    out_shapes = [jax.ShapeDtypeStruct((B, L, dil * OW), out_dtype)]
    out_specs = [out_spec]
    if has_lse:
        out_shapes.append(jax.ShapeDtypeStruct((B, L, dil * OW), F32))
        out_specs.append(out_spec)
    outs = pl.pallas_call(
        kern,
        grid=(B, dil, nq // group),
        in_specs=in_specs,
        out_specs=out_specs,
        out_shape=out_shapes,
        compiler_params=_params("parallel", "parallel", "arbitrary"),
        name=name,
    )(*args)
    return outs


def _out_proj_cd_kernel(h_ref, o0, o1, o2, s0, s1, s2, d_ref, wc_ref, wd_ref, o_ref, *scratch):
    tm = h_ref.shape[0]
    width = C_HEADS_PER_GROUP * HEAD_DIM
    spare = iter(scratch)

    def in_position_order(ref, dil):
        if dil == 1:
            return ref[0]
        t_s = next(spare)
        for r in range(dil):
            for j in range(width // LANE):
                t_s[j, pl.ds(r, tm // dil, stride=dil), :] = ref[0, :, r * width + j * LANE:r * width + (j + 1) * LANE]
        return jnp.concatenate([t_s[j] for j in range(width // LANE)], axis=1)

    dils = [d for _, d in C_PATTERNS]
    outs = [in_position_order(r, d) for r, d in zip((o0, o1, o2), dils)]
    lses = [in_position_order(r, d) for r, d in zip((s0, s1, s2), dils)]
    mx = jnp.maximum(jnp.maximum(lses[0], lses[1]), lses[2])
    es = [jnp.exp(l - mx) for l in lses]
    oc = (es[0] * outs[0] + es[1] * outs[1] + es[2] * outs[2]) / (es[0] + es[1] + es[2])
    mix = jnp.dot(oc.astype(BF16), wc_ref[...], preferred_element_type=F32)
    mix = mix + jnp.dot(d_ref[...], wd_ref[...], preferred_element_type=F32)
    o_ref[...] = h_ref[...] + mix


def out_proj_cd(h, oc, lses, od, wc, wd, batch, seq_len):
    M = h.shape[0]
    tm = ROW_TILE
    nrow = seq_len // tm
    width = C_HEADS_PER_GROUP * HEAD_DIM
    row = lambda a: pl.BlockSpec((tm, a.shape[1]), lambda i: (i, 0))
    full = lambda a: pl.BlockSpec(a.shape, lambda i: (0, 0))
    views = [pl.BlockSpec((1, tm // d, d * width), lambda i: (i // nrow, i % nrow, 0)) for _, d in C_PATTERNS]
    n_spare = 2 * sum(d > 1 for _, d in C_PATTERNS)
    return pl.pallas_call(
        _out_proj_cd_kernel,
        grid=(M // tm,),
        in_specs=[row(h), *views, *views, row(od), full(wc), full(wd)],
        out_specs=pl.BlockSpec((tm, D_MODEL), lambda i: (i, 0)),
        out_shape=jax.ShapeDtypeStruct((M, D_MODEL), F32),
        scratch_shapes=[pltpu.VMEM((width // LANE, tm, LANE), F32)] * n_spare,
        compiler_params=_params("parallel"),
        name="out_proj1",
    )(h, *oc, *lses, od, wc, wd)


def _proj1_kernel(x_ref, g_ref, w_ref, cs_ref, c0_ref, c1_ref, c2_ref, y_s):
    xn = _rms(x_ref[...], g_ref[...]).astype(BF16)
    y = jnp.dot(xn, w_ref[...], preferred_element_type=F32) * cs_ref[...]
    tm = y.shape[0]
    width = y.shape[1] // len(C_PATTERNS)
    c0_ref[0] = y[:, :width].astype(BF16)
    nblk = width // LANE
    for j in range(y_s.shape[0]):
        y_s[j] = y[:, width + j * LANE:width + (j + 1) * LANE]
    for g, ref in ((1, c1_ref), (2, c2_ref)):
        dil = C_PATTERNS[g][1]
        for r in range(dil):
            for j in range(nblk):
                rows = y_s[(g - 1) * nblk + j, pl.ds(r, tm // dil, stride=dil), :]
                ref[0, :, r * width + j * LANE:r * width + (j + 1) * LANE] = rows.astype(BF16)


def proj1_call(x, g, w, colscale, batch, seq_len):
    M, K = x.shape
    N = w.shape[1]
    tm = ROW_TILE
    nrow = seq_len // tm
    width = N // len(C_PATTERNS)
    assert [d for _, d in C_PATTERNS][0] == 1
    out_specs = [pl.BlockSpec((1, tm // d, d * width), lambda i: (i // nrow, i % nrow, 0)) for _, d in C_PATTERNS]
    out_shape = [jax.ShapeDtypeStruct((batch, seq_len // d, d * width), BF16) for _, d in C_PATTERNS]
    return pl.pallas_call(
        _proj1_kernel,
        grid=(M // tm,),
        in_specs=[pl.BlockSpec((tm, K), lambda i: (i, 0)),
                  pl.BlockSpec((1, K), lambda i: (0, 0)),
                  pl.BlockSpec((K, N), lambda i: (0, 0)),
                  pl.BlockSpec((1, N), lambda i: (0, 0))],
        out_specs=out_specs,
        out_shape=out_shape,
        scratch_shapes=[pltpu.VMEM(((N - width) // LANE, tm, LANE), F32)],
        compiler_params=_params("parallel"),
        name="proj1",
    )(x, g.reshape(1, K), w, colscale.reshape(1, N))


def _ffn_kernel(*refs, n_mix, final_norm):
    x_ref, mix_refs = refs[0], refs[1:1 + 2 * n_mix]
    g_ref, wg_ref, wu_ref, wd_ref, fg_ref, o_ref, xn_s, acc_s, h_s = refs[1 + 2 * n_mix:]
    f = pl.program_id(1)

    @pl.when(f == 0)
    def _():
        h = x_ref[...]
        for a_ref, w_ref in zip(mix_refs[::2], mix_refs[1::2]):
            h = h + jnp.dot(a_ref[...], w_ref[...], preferred_element_type=F32)
        h_s[...] = h
        xn_s[...] = _rms(h, g_ref[...]).astype(BF16)
        acc_s[...] = jnp.zeros(acc_s.shape, F32)

    xn = xn_s[...]
    gate = jnp.dot(xn, wg_ref[0].astype(BF16), preferred_element_type=F32)
    up = jnp.dot(xn, wu_ref[0].astype(BF16), preferred_element_type=F32)
    mid = (gate / (1.0 + jnp.exp(-gate)) * up).astype(BF16)
    acc_s[...] += jnp.dot(mid, wd_ref[0].astype(BF16), preferred_element_type=F32)

    @pl.when(f == pl.num_programs(1) - 1)
    def _():
        y = h_s[...] + acc_s[...]
        if final_norm:
            y = _rms(y, fg_ref[...])
        o_ref[...] = y


def ffn(x, mix, g, wg, wu, wd, layer, fg, final_norm, name):
    M, K = x.shape
    tm, tf = FFN_ROW_TILE, FFN_COL_TILE
    kern = functools.partial(_ffn_kernel, n_mix=len(mix), final_norm=final_norm)
    mix_specs, mix_args = [], []
    for a, w in mix:
        mix_specs += [pl.BlockSpec((tm, a.shape[1]), lambda i, f: (i, 0)), pl.BlockSpec(w.shape, lambda i, f: (0, 0))]
        mix_args += [a, w]
    return pl.pallas_call(
        kern,
        grid=(M // tm, D_FF // tf),
        in_specs=[pl.BlockSpec((tm, K), lambda i, f: (i, 0)), *mix_specs,
                  pl.BlockSpec((1, K), lambda i, f: (0, 0)),
                  pl.BlockSpec((1, K, tf), lambda i, f: (layer, 0, f)),
                  pl.BlockSpec((1, K, tf), lambda i, f: (layer, 0, f)),
                  pl.BlockSpec((1, tf, K), lambda i, f: (layer, f, 0)),
                  pl.BlockSpec((1, K), lambda i, f: (0, 0))],
        out_specs=pl.BlockSpec((tm, K), lambda i, f: (i, 0)),
        out_shape=jax.ShapeDtypeStruct((M, K), F32),
        scratch_shapes=[pltpu.VMEM((tm, K), BF16), pltpu.VMEM((tm, K), F32), pltpu.VMEM((tm, K), F32)],
        compiler_params=_params("parallel", "arbitrary"),
        name=name,
    )(x, *mix_args, g.reshape(1, K), wg, wu, wd, fg.reshape(1, K))


def _trig_kernel(ang_ref, cos_ref, sin_ref):
    a = ang_ref[...]
    cos_ref[...] = jnp.cos(a)
    sin_ref[...] = jnp.sin(a)


def rope_tables(seq_len):
    half = D_ROPE // 2
    inv = ROPE_THETA ** (-jnp.arange(half, dtype=F32) / half)
    ang = jnp.arange(seq_len).astype(F32)[:, None] * inv[None, :]
    dense = ang.reshape(seq_len * half // LANE, LANE)
    spec = pl.BlockSpec(dense.shape, lambda: (0, 0))
    cos, sin = pl.pallas_call(
        _trig_kernel,
        in_specs=[spec],
        out_specs=[spec, spec],
        out_shape=[jax.ShapeDtypeStruct(dense.shape, F32)] * 2,
        name="rope_trig",
    )(dense)
    cos, sin = cos.reshape(seq_len, half), sin.reshape(seq_len, half)
    pad = LANE - D_NOPE - D_ROPE
    cos_l = jnp.concatenate([jnp.ones((seq_len, D_NOPE), F32), cos, cos, jnp.ones((seq_len, pad), F32)], axis=1)
    sin_l = jnp.concatenate([jnp.zeros((seq_len, D_NOPE), F32), sin, sin, jnp.zeros((seq_len, pad), F32)], axis=1)
    return cos_l, sin_l


def _prep_d_kernel(x_ref, g_ref, wa_ref, qn_ref, kvn_ref, wq_ref, wkv_ref, cos_ref, sin_ref,
                   q_ref, k_ref, v_ref, *, qscale):
    xn = _rms(x_ref[...], g_ref[...]).astype(BF16)
    lat = jnp.dot(xn, wa_ref[...], preferred_element_type=F32)
    cq = _rms(lat[:, :D_Q_LORA], qn_ref[...]).astype(BF16)
    ckv = _rms(lat[:, D_Q_LORA:D_Q_LORA + D_KV_LORA], kvn_ref[...]).astype(BF16)
    o_pe = D_Q_LORA + D_KV_LORA
    cos, sin = cos_ref[...], sin_ref[...]
    kpe = lat[:, o_pe:o_pe + LANE] * cos + lat[:, o_pe + LANE:o_pe + 2 * LANE] * sin
    qq = jnp.dot(cq, wq_ref[...], preferred_element_type=F32)
    kv = jnp.dot(ckv, wkv_ref[...], preferred_element_type=F32)
    kw = D_HEADS * LANE
    ones_blk = _ones_row_block(V_ROWS_D - D_V, x_ref.shape[0])
    for h in range(D_HEADS):
        qh = qq[:, h * LANE:(h + 1) * LANE] * cos + qq[:, kw + h * LANE:kw + (h + 1) * LANE] * sin
        q_ref[0, h, 0] = (qh * qscale).T.astype(BF16)
        k_ref[:, h * LANE:(h + 1) * LANE] = (kv[:, h * LANE:(h + 1) * LANE] + kpe).astype(BF16)
    for j in range(D_HEADS // 2):
        vt = kv[:, kw + j * LANE:kw + (j + 1) * LANE].T.astype(BF16)
        for half in range(2):
            v_ref[0, 2 * j + half, 0, :D_V] = vt[half * D_V:(half + 1) * D_V]
            v_ref[0, 2 * j + half, 0, D_V:] = ones_blk


def prep_d(x, g, wa, qn, kvn, wq, wkv, cos_l, sin_l, batch, seq_len):
    M, K = x.shape
    T = DENSE_TILE
    nrow = seq_len // T
    kw = D_HEADS * LANE
    kern = functools.partial(_prep_d_kernel, qscale=(D_NOPE + D_ROPE) ** -0.5 * LOG2E)
    full = lambda a: pl.BlockSpec(a.shape, lambda i: (0, 0))
    qn2, kvn2, g2 = qn.reshape(1, -1), kvn.reshape(1, -1), g.reshape(1, K)
    return pl.pallas_call(
        kern,
        grid=(M // T,),
        in_specs=[pl.BlockSpec((T, K), lambda i: (i, 0)), full(g2), full(wa), full(qn2), full(kvn2),
                  full(wq), full(wkv),
                  pl.BlockSpec((T, LANE), lambda i: (i % nrow, 0)),
                  pl.BlockSpec((T, LANE), lambda i: (i % nrow, 0))],
        out_specs=[pl.BlockSpec((1, D_HEADS, 1, LANE, T), lambda i: (i // nrow, 0, i % nrow, 0, 0)),
                   pl.BlockSpec((T, kw), lambda i: (i, 0)),
                   pl.BlockSpec((1, D_HEADS, 1, V_ROWS_D, T), lambda i: (i // nrow, 0, i % nrow, 0, 0))],
        out_shape=[jax.ShapeDtypeStruct((batch, D_HEADS, nrow, LANE, T), BF16),
                   jax.ShapeDtypeStruct((M, kw), BF16),
                   jax.ShapeDtypeStruct((batch, D_HEADS, nrow, V_ROWS_D, T), BF16)],
        compiler_params=_params("parallel"),
        name="prep_d",
    )(x, g2, wa, qn2, kvn2, wq, wkv, cos_l, sin_l)


B_HEAD_ORDER = (0, 4, 1, 5, 2, 6, 3, 7)


def _blocks(w, starts, width, axis):
    return jnp.concatenate([lax.slice_in_dim(w, s, s + width, axis=axis) for s in starts], axis=axis)


def _rot_partner_cols(w):
    half = D_ROPE // 2
    return jnp.concatenate([-w[..., half:], w[..., :half]], axis=-1)


def kernel(x, bias_table, attn_norm, ffn_norm, final_norm, ab_w_in, ab_lambda_q1, ab_lambda_k1,
           ab_lambda_q2, ab_lambda_k2, ab_subln, ab_sink, ab_w_o, cd_w_in, cd_q_norm, cd_w_q_b,
           cd_kv_norm, cd_w_kv_b, cd_w_o, ffn_w_gate, ffn_w_up, ffn_w_down):
    B, S, _ = x.shape
    M = B * S
    T = DENSE_TILE
    h = x.reshape(M, D_MODEL)
    qk_scale = HEAD_DIM ** -0.5 * LOG2E

    o3 = A_HEADS * (2 * A_QK_DIM + A_V_DIM)
    a0 = ab_w_in[0].astype(BF16)
    w0 = jnp.concatenate([a0[:, :o3], _blocks(a0, [o3 + hd * HEAD_DIM for hd in B_HEAD_ORDER], HEAD_DIM, 1),
                          a0[:, o3 + B_HEADS * HEAD_DIM:]], axis=1)
    cs0 = np.ones((AB_IN,), np.float32)
    cs0[:A_HEADS * A_QK_DIM] = qk_scale
    cs0[o3:o3 + B_HEADS * HEAD_DIM] = qk_scale
    qat, vat, qbt, vbt, keys0 = proj0_call(h, attn_norm[0], w0, jnp.asarray(cs0), B, S)
    keys0 = keys0.reshape(B, S, -1)
    bias_a = bias_tiles(bias_table, nvar=2 * BIAS_REACH + 1, nheads=A_HEADS, head0=0, rows=T, cols=T,
                        off0=-BIAS_REACH * T, off_step=T, row_coef=1, col_coef=-1, dil=1,
                        half_window=None, name="bias_a")
    oa = attn_a(qat, keys0, vat, bias_a, ab_lambda_q1[0], ab_lambda_k1[0], ab_lambda_q2[0],
                ab_lambda_k2[0], ab_subln[0], 0.8 - 0.6 * math.exp(-0.3 * 0))

    bias_b = bias_tiles(bias_table, nvar=3, nheads=B_HEADS, head0=A_HEADS, rows=QCOLS + 2 * B_HALF_WINDOW,
                        cols=QCOLS, off0=0, off_step=-B_HALF_WINDOW, row_coef=1, col_coef=-1, dil=1,
                        half_window=B_HALF_WINDOW, name="bias_b")
    ob = attn_b(qbt, keys0, (A_HEADS * A_QK_DIM) // LANE, vbt, bias_b, ab_sink[0])

    wo = ab_w_o[0].astype(BF16)
    wo_a = wo[:A_HEADS * A_V_DIM]
    wo_b = _blocks(wo, [A_HEADS * A_V_DIM + hd * HEAD_DIM for hd in B_HEAD_ORDER], HEAD_DIM, 0)
    h = ffn(h, [(oa.reshape(M, -1), wo_a), (ob.reshape(M, -1), wo_b)], ffn_norm[0], ffn_w_gate, ffn_w_up,
            ffn_w_down, 0, final_norm, False, "ffn0")

    w1 = cd_w_in[0]
    gw = C_HEADS_PER_GROUP * HEAD_DIM
    cw = C_HEADS * HEAD_DIM
    starts1 = [role * cw + g * gw for g in range(len(C_PATTERNS)) for role in range(3)]
    cs1 = np.ones((CD_C_IN,), np.float32)
    for g in range(len(C_PATTERNS)):
        cs1[3 * g * gw:(3 * g + 1) * gw] = qk_scale
    c_views = proj1_call(h, attn_norm[1], _blocks(w1.astype(BF16), starts1, gw, 1), jnp.asarray(cs1), B, S)

    oc, lses = [], []
    for g, (window, dil) in enumerate(C_PATTERNS):
        hw = window // (2 * dil)
        tq_c = 128
        bias_c = bias_tiles(bias_table, nvar=3, nheads=C_HEADS_PER_GROUP, head0=g * C_HEADS_PER_GROUP,
                            rows=tq_c, cols=tq_c + 2 * hw, off0=0, off_step=-hw, row_coef=-1, col_coef=1,
                            dil=dil, half_window=hw, name=f"bias_c{g}")
        o_g, lse_g = banded(c_views[g], bias_c, None, dil=dil, tq=tq_c, half_window=hw, nqb=2, nkb=2,
                            q_idx=lambda r: 3 * r, k_idx=lambda r: 3 * r + 1, v_idx=lambda r: 3 * r + 2,
                            head_of=lambda c, half: 2 * c + half, out_dtype=F32, has_lse=True,
                            name=f"attn_c{g}")
        oc.append(o_g)
        lses.append(lse_g)

    o_q, o_kv = CD_C_IN + D_Q_LORA, CD_C_IN + D_Q_LORA + D_KV_LORA
    w_pe = w1[:, o_kv:]
    lane_pad = lambda w: jnp.pad(w, ((0, 0), (D_NOPE, LANE - D_NOPE - D_ROPE)))
    wa = jnp.concatenate([w1[:, CD_C_IN:o_kv], lane_pad(w_pe), lane_pad(_rot_partner_cols(w_pe))],
                         axis=1).astype(BF16)
    wq3 = cd_w_q_b[0].reshape(D_Q_LORA, D_HEADS, D_NOPE + D_ROPE)
    zpad = jnp.zeros((D_Q_LORA, D_HEADS, LANE - D_NOPE - D_ROPE), F32)
    wq_main = jnp.concatenate([wq3, zpad], axis=-1)
    wq_rot = jnp.concatenate([jnp.zeros_like(wq3[..., :D_NOPE]), _rot_partner_cols(wq3[..., D_NOPE:]), zpad],
                             axis=-1)
    wq = jnp.concatenate([wq_main.reshape(D_Q_LORA, -1), wq_rot.reshape(D_Q_LORA, -1)], axis=1).astype(BF16)
    wkv3 = cd_w_kv_b[0].reshape(D_KV_LORA, D_HEADS, D_NOPE + D_V)
    wk = jnp.pad(wkv3[..., :D_NOPE], ((0, 0), (0, 0), (0, LANE - D_NOPE))).reshape(D_KV_LORA, -1)
    wv = wkv3[..., D_NOPE:].reshape(D_KV_LORA, -1)
    wkv = jnp.concatenate([wk, wv], axis=1).astype(BF16)
    cos_l, sin_l = rope_tables(S)
    qdt, kd, vdt = prep_d(h, attn_norm[1], wa, cd_q_norm[0], cd_kv_norm[0], wq, wkv, cos_l, sin_l, B, S)
    od = attn_d(qdt, kd.reshape(B, S, D_HEADS * LANE), vdt)

    wo1 = cd_w_o[0]
    wo_c = wo1[:C_HEADS_PER_GROUP * HEAD_DIM].astype(BF16)
    wo_d = wo1[C_HEADS_PER_GROUP * HEAD_DIM:].astype(BF16)
    h = out_proj_cd(h, oc, lses, od.reshape(M, -1), wo_c, wo_d, B, S)
    h = ffn(h, [], ffn_norm[1], ffn_w_gate, ffn_w_up, ffn_w_down, 1, final_norm, True, "ffn1")
    return h.reshape(B, S, D_MODEL)
```

```python
import functools
import math

import numpy as np
import jax
import jax.numpy as jnp
from jax import lax
from jax.experimental import pallas as pl
from jax.experimental.pallas import tpu as pltpu

F32 = jnp.float32
BF16 = jnp.bfloat16

D_MODEL = 1024
HEAD_DIM = 64
EPS = 1e-6
NEG = -1e30
LOG2E = math.log2(math.e)
LN2 = math.log(2.0)

A_HEADS = 4
A_QK_DIM = 2 * HEAD_DIM
A_V_DIM = 2 * HEAD_DIM
B_HEADS = 8
B_KV_HEADS = 2
B_HALF_WINDOW = 128
C_PATTERNS = ((128, 1), (512, 4), (2048, 16))
C_HEADS_PER_GROUP = 4
C_HEADS = C_HEADS_PER_GROUP * len(C_PATTERNS)
D_HEADS = 12
D_Q_LORA = 384
D_KV_LORA = 256
D_NOPE = 64
D_ROPE = 32
D_V = 64
ROPE_THETA = 10000.0
NUM_BUCKETS = 32
MAX_DISTANCE = 1024
D_FF = 2816
AB_IN = 2304
CD_C_IN = 3 * C_HEADS * HEAD_DIM

LANE = 128
VMEM_LIMIT = 48 * 1024 * 1024

ROW_TILE = 512
FFN_ROW_TILE = 1024
FFN_COL_TILE = 256
DENSE_TILE = 512
QCOLS = 256
UNROLL = 4
DENSE_QTILES = 2
BAND_GROUP = 4
BIAS_REACH = 3
V_ROWS_D = 80
V_ROWS_A = 144


def _bucket_thresholds():
    nb = NUM_BUCKETS // 2
    max_exact = nb // 2
    n = np.arange(1, 4 * MAX_DISTANCE)
    large = max_exact + (np.log(n.astype(np.float32) / np.float32(max_exact))
                         / np.float32(math.log(MAX_DISTANCE / max_exact))
                         * np.float32(nb - max_exact)).astype(np.int32)
    mag = np.where(n < max_exact, n, np.minimum(large, nb - 1))
    return tuple(int(n[np.argmax(mag >= k)]) for k in range(1, nb))


BUCKET_THRESHOLDS = _bucket_thresholds()
assert BUCKET_THRESHOLDS[-1] <= (BIAS_REACH - 1) * DENSE_TILE + 1


def _params(*sem):
    return pltpu.CompilerParams(dimension_semantics=sem, vmem_limit_bytes=VMEM_LIMIT)


def _rms(x, g):
    return x * lax.rsqrt(jnp.mean(x * x, axis=-1, keepdims=True) + EPS) * g


def _ones_row_block(rows, cols):
    r = lax.broadcasted_iota(jnp.int32, (rows, cols), 0)
    return jnp.where(r == 0, 1.0, 0.0).astype(BF16)


def _store_masked_halves(q_ref, idx, qt):
    zeros = jnp.zeros((HEAD_DIM, qt.shape[1]), BF16)
    q_ref[idx + (0, 0, slice(None, HEAD_DIM))] = qt[:HEAD_DIM]
    q_ref[idx + (0, 0, slice(HEAD_DIM, None))] = zeros
    q_ref[idx + (1, 0, slice(None, HEAD_DIM))] = zeros
    q_ref[idx + (1, 0, slice(HEAD_DIM, None))] = qt[HEAD_DIM:]


def _proj0_kernel(x_ref, g_ref, w_ref, cs_ref, qa_ref, va_ref, qb_ref, vb_ref, k_ref):
    xn = _rms(x_ref[...], g_ref[...]).astype(BF16)
    y = jnp.dot(xn, w_ref[...], preferred_element_type=F32) * cs_ref[...]
    tm = y.shape[0]
    ka0 = A_HEADS * A_QK_DIM
    va0 = 2 * ka0
    qb0 = va0 + A_HEADS * A_V_DIM
    kb0 = qb0 + B_HEADS * HEAD_DIM
    vb0 = kb0 + B_KV_HEADS * HEAD_DIM
    for h in range(A_HEADS):
        _store_masked_halves(qa_ref, (0, h), y[:, h * A_QK_DIM:(h + 1) * A_QK_DIM].T.astype(BF16))
        va_ref[0, h, 0, :A_V_DIM] = y[:, va0 + h * A_V_DIM:va0 + (h + 1) * A_V_DIM].T.astype(BF16)
        va_ref[0, h, 0, A_V_DIM:] = _ones_row_block(V_ROWS_A - A_V_DIM, tm)
    for j in range(B_HEADS // 2):
        _store_masked_halves(qb_ref, (0, j), y[:, qb0 + j * LANE:qb0 + (j + 1) * LANE].T.astype(BF16))
    vbt = y[:, vb0:vb0 + LANE].T.astype(BF16)
    ones_blk = _ones_row_block(V_ROWS_D - HEAD_DIM, LANE)
    for g in range(B_KV_HEADS):
        for c in range(tm // LANE):
            vb_ref[0, g, c, :HEAD_DIM] = vbt[g * HEAD_DIM:(g + 1) * HEAD_DIM, c * LANE:(c + 1) * LANE]
            vb_ref[0, g, c, HEAD_DIM:] = ones_blk
    k_ref[:, :ka0] = y[:, ka0:va0].astype(BF16)
    k_ref[:, ka0:] = y[:, kb0:vb0].astype(BF16)


def proj0_call(x, g, w, colscale, batch, seq_len):
    M, K = x.shape
    N = w.shape[1]
    T = DENSE_TILE
    nrow = seq_len // T
    nkb = A_HEADS * A_QK_DIM + B_KV_HEADS * HEAD_DIM
    qspec = pl.BlockSpec((1, A_HEADS, 2, 1, LANE, T), lambda i: (i // nrow, 0, 0, i % nrow, 0, 0))
    qshape = jax.ShapeDtypeStruct((batch, A_HEADS, 2, nrow, LANE, T), BF16)
    return pl.pallas_call(
        _proj0_kernel,
        grid=(M // T,),
        in_specs=[pl.BlockSpec((T, K), lambda i: (i, 0)),
                  pl.BlockSpec((1, K), lambda i: (0, 0)),
                  pl.BlockSpec((K, N), lambda i: (0, 0)),
                  pl.BlockSpec((1, N), lambda i: (0, 0))],
        out_specs=[qspec,
                   pl.BlockSpec((1, A_HEADS, 1, V_ROWS_A, T), lambda i: (i // nrow, 0, i % nrow, 0, 0)),
                   qspec,
                   pl.BlockSpec((1, B_KV_HEADS, T // LANE, V_ROWS_D, LANE),
                                lambda i: (i // nrow, 0, i % nrow, 0, 0)),
                   pl.BlockSpec((T, nkb), lambda i: (i, 0))],
        out_shape=[qshape,
                   jax.ShapeDtypeStruct((batch, A_HEADS, nrow, V_ROWS_A, T), BF16),
                   qshape,
                   jax.ShapeDtypeStruct((batch, B_KV_HEADS, seq_len // LANE, V_ROWS_D, LANE), BF16),
                   jax.ShapeDtypeStruct((M, nkb), BF16)],
        compiler_params=_params("parallel"),
        name="proj0",
    )(x, g.reshape(1, K), w, colscale.reshape(1, N))


def _bias_kernel(tab_ref, o_ref, *, nvar, off0, off_step, row_coef, col_coef, dil, half_window, head0):
    hcol = head0 + pl.program_id(1)
    R, C = o_ref.shape[-2:]
    row = lax.broadcasted_iota(jnp.int32, (R, C), 0)
    col = lax.broadcasted_iota(jnp.int32, (R, C), 1)
    base = row_coef * row + col_coef * col
    span_lo = min(row_coef * (R - 1), 0) + min(col_coef * (C - 1), 0)
    span_hi = max(row_coef * (R - 1), 0) + max(col_coef * (C - 1), 0)
    nb = NUM_BUCKETS // 2

    def side(n, n_lo, n_hi, row0):
        val = jnp.full((R, C), tab_ref[row0 + sum(t <= n_lo for t in BUCKET_THRESHOLDS), hcol], F32)
        for k, thr in enumerate(BUCKET_THRESHOLDS, start=1):
            if n_lo < thr <= n_hi:
                val = jnp.where(n >= thr, tab_ref[row0 + k, hcol], val)
        return val

    for v in range(nvar):
        @pl.when(pl.program_id(0) == v)
        def _(v=v):
            off = off0 + v * off_step
            rel = off + base
            lo, hi = (off + span_lo) * dil, (off + span_hi) * dil
            dist = rel * dil
            n = jnp.abs(dist)
            if hi <= 0:
                val = side(n, -hi, -lo, 0)
            elif lo > 0:
                val = side(n, lo, hi, nb)
            else:
                val = jnp.where(dist > 0, side(n, 1, hi, nb), side(n, 0, -lo, 0))
            val = val * LOG2E
            if half_window is not None:
                val = jnp.where(jnp.abs(rel) <= half_window, val, NEG)
            o_ref[0, 0] = val


def bias_tiles(table, *, nvar, nheads, head0, rows, cols, off0, off_step, row_coef, col_coef,
               dil, half_window, name):
    kern = functools.partial(_bias_kernel, nvar=nvar, off0=off0, off_step=off_step, row_coef=row_coef,
                             col_coef=col_coef, dil=dil, half_window=half_window, head0=head0)
    return pl.pallas_call(
        kern,
        grid=(nvar, nheads),
        in_specs=[pl.BlockSpec(memory_space=pltpu.SMEM)],
        out_specs=pl.BlockSpec((1, 1, rows, cols), lambda v, h: (v, h, 0, 0)),
        out_shape=jax.ShapeDtypeStruct((nvar, nheads, rows, cols), F32),
        compiler_params=_params("parallel", "parallel"),
        name=name,
    )(table)


def _dense_pipeline(nk, tile, n_streams, score_fn, value_fn, m_s, acc_s, bufs):
    units = [(i, c * QCOLS) for i in range(n_streams) for c in range(tile // QCOLS)]

    def produce(kc, unit, nxt):
        i, c0 = unit
        cols = slice(c0, c0 + QCOLS)
        s = score_fn(kc + 1, i, cols)
        nxt[0][i, :, cols] = s
        nxt[1][i, :, cols] = jnp.max(s, axis=0, keepdims=True)

    def consume(kc, unit, cur):
        i, c0 = unit
        cols = slice(c0, c0 + QCOLS)
        m_old = m_s[i, :, cols]
        m_new = jnp.maximum(m_old, cur[1][i, :, cols])
        alpha = jnp.exp2(m_old - m_new)
        p = jnp.exp2((cur[0][i, :, cols] - m_new).astype(BF16))
        acc_s[i, :, cols] = (alpha * acc_s[i, :, cols]
                             + jnp.dot(value_fn(kc, i), p, preferred_element_type=F32))
        m_s[i, :, cols] = m_new

    def stage(kc, cur, nxt):
        for unit in units:
            if nxt is not None:
                produce(kc, unit, nxt)
            if cur is not None:
                consume(kc, unit, cur)

    m_s[...] = jnp.full(m_s.shape, NEG, F32)
    acc_s[...] = jnp.zeros(acc_s.shape, F32)
    stage(-1, None, bufs[0])
    n_loop = (nk - 1) // UNROLL

    def body(j, carry):
        for u in range(UNROLL):
            stage(UNROLL * j + u, bufs[u % 2], bufs[(u + 1) % 2])
        return carry

    lax.fori_loop(0, n_loop, body, 0)
    for kc in range(n_loop * UNROLL, nk):
        stage(kc, bufs[kc % 2], bufs[(kc + 1) % 2] if kc < nk - 1 else None)


def _attn_a_kernel(q_ref, k_ref, v_ref, bias_ref, lq1_ref, lk1_ref, lq2_ref, lk2_ref, subln_ref, o_ref,
                   m_s, acc_s, s_a, s_b, cm_a, cm_b, *, tile, nk, qtiles, lambda_init):
    qi0 = pl.program_id(2) * qtiles

    def score_fn(kc, i, cols):
        j, qt = divmod(i, qtiles)
        kblk = k_ref[0, pl.ds(pl.multiple_of(kc * tile, tile), tile), :]
        bt = bias_ref[jnp.clip(kc - (qi0 + qt), -BIAS_REACH, BIAS_REACH) + BIAS_REACH, 0, :, cols]
        return jnp.dot(kblk, q_ref[0, 0, j, qt, :, cols], preferred_element_type=F32) + bt

    def value_fn(kc, i):
        return v_ref[0, 0, kc]

    _dense_pipeline(nk, tile, 2 * qtiles, score_fn, value_fn, m_s, acc_s, ((s_a, cm_a), (s_b, cm_b)))

    lam = (jnp.exp(jnp.sum(lq1_ref[...] * lk1_ref[...], axis=-1, keepdims=True))
           - jnp.exp(jnp.sum(lq2_ref[...] * lk2_ref[...], axis=-1, keepdims=True)) + lambda_init)
    for qt in range(qtiles):
        a1, a2 = acc_s[qt], acc_s[qtiles + qt]
        o = (a1[:A_V_DIM] / a1[A_V_DIM:A_V_DIM + 1]
             - lam * (a2[:A_V_DIM] / a2[A_V_DIM:A_V_DIM + 1]))
        ms = jnp.mean(o * o, axis=0, keepdims=True)
        y = o * lax.rsqrt(ms + EPS) * subln_ref[...] * (1.0 - lambda_init)
        o_ref[0, qt * tile:(qt + 1) * tile] = y.T.astype(o_ref.dtype)


def attn_a(qt, karr, vt, bias, lq1, lk1, lq2, lk2, subln, lambda_init):
    B, S, _ = karr.shape
    T = DENSE_TILE
    nq = nk = S // T
    assert nk % 2 == 0 and nk >= 4
    qtiles = DENSE_QTILES
    ns = 2 * qtiles
    kern = functools.partial(_attn_a_kernel, tile=T, nk=nk, qtiles=qtiles, lambda_init=lambda_init)
    vec = lambda n: pl.BlockSpec((1, n), lambda b, h, i: (0, 0))
    return pl.pallas_call(
        kern,
        grid=(B, A_HEADS, nq // qtiles),
        in_specs=[pl.BlockSpec((1, 1, 2, qtiles, LANE, T), lambda b, h, i: (b, h, 0, i, 0, 0)),
                  pl.BlockSpec((1, S, LANE), lambda b, h, i: (b, 0, h)),
                  pl.BlockSpec((1, 1, nk, V_ROWS_A, T), lambda b, h, i: (b, h, 0, 0, 0)),
                  pl.BlockSpec((2 * BIAS_REACH + 1, 1, T, T), lambda b, h, i: (0, h, 0, 0)),
                  vec(HEAD_DIM), vec(HEAD_DIM), vec(HEAD_DIM), vec(HEAD_DIM),
                  pl.BlockSpec((A_V_DIM, 1), lambda b, h, i: (0, 0))],
        out_specs=pl.BlockSpec((1, qtiles * T, LANE), lambda b, h, i: (b, i, h)),
        out_shape=jax.ShapeDtypeStruct((B, S, A_HEADS * A_V_DIM), BF16),
        scratch_shapes=[pltpu.VMEM((ns, 1, T), F32), pltpu.VMEM((ns, V_ROWS_A, T), F32),
                        pltpu.VMEM((ns, T, T), F32), pltpu.VMEM((ns, T, T), F32),
                        pltpu.VMEM((ns, 1, T), F32), pltpu.VMEM((ns, 1, T), F32)],
        compiler_params=_params("parallel", "parallel", "arbitrary"),
        name="attn_a",
    )(qt, karr, vt, bias, lq1.reshape(1, -1), lk1.reshape(1, -1), lq2.reshape(1, -1),
      lk2.reshape(1, -1), subln.reshape(-1, 1))


def _attn_d_kernel(q_ref, k_ref, v_ref, o_ref, m_s, acc_s, s_a, s_b, cm_a, cm_b, *, tile, nk, qtiles):
    def score_fn(kc, i, cols):
        hh, qt = divmod(i, qtiles)
        kblk = k_ref[0, pl.ds(pl.multiple_of(kc * tile, tile), tile), hh * LANE:(hh + 1) * LANE]
        return jnp.dot(kblk, q_ref[0, hh, qt, :, cols], preferred_element_type=F32)

    def value_fn(kc, i):
        return v_ref[0, i // qtiles, kc]

    _dense_pipeline(nk, tile, 2 * qtiles, score_fn, value_fn, m_s, acc_s, ((s_a, cm_a), (s_b, cm_b)))
    for qt in range(qtiles):
        outs = []
        for hh in range(2):
            acc = acc_s[hh * qtiles + qt]
            outs.append(acc[:D_V] / acc[D_V:D_V + 1])
        o_ref[0, qt * tile:(qt + 1) * tile] = jnp.concatenate(outs, axis=0).T.astype(o_ref.dtype)


def attn_d(qt, k, vt):
    B, S, _ = k.shape
    T = DENSE_TILE
    nq = nk = S // T
    qtiles = DENSE_QTILES
    ns = 2 * qtiles
    kern = functools.partial(_attn_d_kernel, tile=T, nk=nk, qtiles=qtiles)
    return pl.pallas_call(
        kern,
        grid=(B, D_HEADS // 2, nq // qtiles),
        in_specs=[pl.BlockSpec((1, 2, qtiles, LANE, T), lambda b, h, i: (b, h, i, 0, 0)),
                  pl.BlockSpec((1, S, 2 * LANE), lambda b, h, i: (b, 0, h)),
                  pl.BlockSpec((1, 2, nk, V_ROWS_D, T), lambda b, h, i: (b, h, 0, 0, 0))],
        out_specs=pl.BlockSpec((1, qtiles * T, LANE), lambda b, h, i: (b, i, h)),
        out_shape=jax.ShapeDtypeStruct((B, S, D_HEADS * D_V), BF16),
        scratch_shapes=[pltpu.VMEM((ns, 1, T), F32), pltpu.VMEM((ns, V_ROWS_D, T), F32),
                        pltpu.VMEM((ns, T, T), F32), pltpu.VMEM((ns, T, T), F32),
                        pltpu.VMEM((ns, 1, T), F32), pltpu.VMEM((ns, 1, T), F32)],
        compiler_params=_params("parallel", "parallel", "arbitrary"),
        name="attn_d",
    )(qt, k, vt)


def _attn_b_kernel(q_ref, k_ref, v_ref, bias_ref, sink_ref, o_ref, s_a, s_b, s_c, cm_a, cm_b, cm_c, *, tile, seq_len):
    half_window = B_HALF_WINDOW
    span = QCOLS + 2 * half_window
    nchunk = span // LANE
    step = pl.program_id(1)

    def window(c):
        q0 = step * tile + c * QCOLS
        start = pl.multiple_of(jnp.clip(q0 - half_window, 0, seq_len - span), LANE)
        variant = jnp.where(q0 == 0, 0, jnp.where(q0 + QCOLS == seq_len, 2, 1))
        return start, variant

    def scores(task):
        c, j, g = task
        start, variant = window(c)
        kwin = k_ref[0, pl.ds(start, span), :]
        s = jnp.dot(kwin, q_ref[0, j, g, 0, :, c * QCOLS:(c + 1) * QCOLS], preferred_element_type=F32)
        return s + bias_ref[variant, j + (B_HEADS // 2) * g]

    def finish(task, s_ref, cm_ref):
        c, j, g = task
        head = j + (B_HEADS // 2) * g
        start, _ = window(c)
        chunk0 = start // LANE
        vwin = jnp.concatenate([v_ref[0, g, chunk0 + n] for n in range(nchunk)], axis=1)
        sk = sink_ref[:, head:head + 1] * LOG2E
        m = jnp.maximum(cm_ref[...], sk)
        e = jnp.exp2((s_ref[...] - m).astype(BF16))
        ov = jnp.dot(vwin, e, preferred_element_type=F32)
        denom = ov[HEAD_DIM:HEAD_DIM + 1] + jnp.exp2(sk - m)
        return ov[:HEAD_DIM] / denom

    tasks = [(c, j, g) for c in range(tile // QCOLS) for j in range(B_HEADS // 2) for g in range(B_KV_HEADS)]
    bufs = ((s_a, cm_a), (s_b, cm_b), (s_c, cm_c))

    def produce(task, buf):
        s = scores(task)
        buf[0][...] = s
        buf[1][...] = jnp.max(s, axis=0, keepdims=True)

    produce(tasks[0], bufs[0])
    produce(tasks[1], bufs[1])
    held = None
    for n, task in enumerate(tasks):
        if n + 2 < len(tasks):
            produce(tasks[n + 2], bufs[(n + 2) % 3])
        o = finish(task, *bufs[n % 3])
        c, j, g = task
        if g == 0:
            held = o
            continue
        pair = jnp.concatenate([held, o], axis=0).T
        o_ref[0, c * QCOLS:(c + 1) * QCOLS, j * LANE:(j + 1) * LANE] = pair.astype(o_ref.dtype)


def attn_b(qt, karr, k_block, vt, bias, sink):
    B, S, _ = karr.shape
    T = DENSE_TILE
    span = QCOLS + 2 * B_HALF_WINDOW
    return pl.pallas_call(
        functools.partial(_attn_b_kernel, tile=T, seq_len=S),
        grid=(B, S // T),
        in_specs=[pl.BlockSpec((1, B_HEADS // 2, 2, 1, LANE, T), lambda b, i: (b, 0, 0, i, 0, 0)),
                  pl.BlockSpec((1, S, LANE), lambda b, i: (b, 0, k_block)),
                  pl.BlockSpec((1, B_KV_HEADS, S // LANE, V_ROWS_D, LANE), lambda b, i: (b, 0, 0, 0, 0)),
                  pl.BlockSpec(bias.shape, lambda b, i: (0, 0, 0, 0)),
                  pl.BlockSpec((1, B_HEADS), lambda b, i: (0, 0))],
        out_specs=pl.BlockSpec((1, T, B_HEADS * HEAD_DIM), lambda b, i: (b, i, 0)),
        out_shape=jax.ShapeDtypeStruct((B, S, B_HEADS * HEAD_DIM), BF16),
        scratch_shapes=[pltpu.VMEM((span, QCOLS), F32)] * 3 + [pltpu.VMEM((1, QCOLS), F32)] * 3,
        compiler_params=_params("parallel", "arbitrary"),
        name="attn_b",
    )(qt, karr, vt, bias, sink.reshape(1, -1))


def _banded_kernel(*refs, tq, group, span, half_window, seq_len, nqb, nkb, head_of, has_sink, has_lse):
    it = iter(refs)
    q_ref, k_ref, v_ref, bias_ref = next(it), next(it), next(it), next(it)
    sink_ref = next(it) if has_sink else None
    o_ref = next(it)
    lse_ref = next(it) if has_lse else None

    nq = seq_len // tq
    lane = lax.broadcasted_iota(jnp.int32, (tq, LANE), 1)
    low = lane < HEAD_DIM

    def window(g):
        t = pl.program_id(2) * group + g
        start = pl.multiple_of(jnp.clip(t * tq - half_window, 0, seq_len - span), half_window)
        variant = jnp.where(t == 0, 0, jnp.where(t == nq - 1, 2, 1))
        return start, variant

    def scores(task):
        g, c, half = task
        start, variant = window(g)
        kc = c if nkb == nqb else 0
        q2 = q_ref[0, g * tq:(g + 1) * tq, c * LANE:(c + 1) * LANE]
        kb = k_ref[0, pl.ds(start, span), kc * LANE:(kc + 1) * LANE]
        qm = jnp.where(low if half == 0 else jnp.logical_not(low), q2, jnp.zeros_like(q2))
        s = lax.dot_general(qm, kb, (((1,), (1,)), ((), ())), preferred_element_type=F32)
        return s + bias_ref[variant, head_of(c, half)]

    def finish(task, s):
        g, c, half = task
        start, _ = window(g)
        kc = c if nkb == nqb else 0
        hidx = head_of(c, half)
        vb = v_ref[0, pl.ds(start, span), kc * LANE:(kc + 1) * LANE]
        m = jnp.max(s, axis=-1, keepdims=True)
        if has_sink:
            sk = sink_ref[:, hidx:hidx + 1] * LOG2E
            m = jnp.maximum(m, sk)
        e = jnp.exp2(s - m)
        denom = jnp.sum(e, axis=-1, keepdims=True)
        if has_sink:
            denom = denom + jnp.exp2(sk - m)
        o = jnp.dot(e.astype(BF16), vb, preferred_element_type=F32) / denom
        return o, (LN2 * m + jnp.log(denom) if has_lse else None)

    tasks = [(g, c, half) for g in range(group) for c in range(nqb) for half in range(2)]
    s_next = scores(tasks[0])
    held = None
    for n, task in enumerate(tasks):
        s = s_next
        if n + 1 < len(tasks):
            s_next = scores(tasks[n + 1])
        o, lse = finish(task, s)
        g, c, half = task
        if half == 0:
            held = (o, lse)
            continue
        rows, cols = slice(g * tq, (g + 1) * tq), slice(c * LANE, (c + 1) * LANE)
        o_ref[0, rows, cols] = jnp.where(low, held[0], o).astype(o_ref.dtype)
        if has_lse:
            lse_ref[0, rows, cols] = jnp.where(low, held[1], lse)


def banded(view, bias, sink, *, dil, tq, half_window, nqb, nkb, q_idx, k_idx, v_idx, head_of,
           out_dtype, has_lse, name):
    B, L, _ = view.shape
    span = tq + 2 * half_window
    nq = L // tq
    assert L % tq == 0 and L >= span and nq >= 2
    group = math.gcd(nq, BAND_GROUP)
    OW = nqb * LANE
    has_sink = sink is not None
    kern = functools.partial(_banded_kernel, tq=tq, group=group, span=span, half_window=half_window,
                             seq_len=L, nqb=nqb, nkb=nkb, head_of=head_of, has_sink=has_sink,
                             has_lse=has_lse)
    in_specs = [pl.BlockSpec((1, group * tq, OW), lambda b, r, t: (b, t, q_idx(r))),
                pl.BlockSpec((1, L, nkb * LANE), lambda b, r, t: (b, 0, k_idx(r))),
                pl.BlockSpec((1, L, nkb * LANE), lambda b, r, t: (b, 0, v_idx(r))),
                pl.BlockSpec(bias.shape, lambda b, r, t: (0, 0, 0, 0))]
    args = [view, view, view, bias]
    if has_sink:
        in_specs.append(pl.BlockSpec((1, sink.shape[-1]), lambda b, r, t: (0, 0)))
        args.append(sink.reshape(1, -1))
    out_spec = pl.BlockSpec((1, group * tq, OW), lambda b, r, t: (b, t, r))
    out_shapes = [jax.ShapeDtypeStruct((B, L, dil * OW), out_dtype)]
    out_specs = [out_spec]
    if has_lse:
        out_shapes.append(jax.ShapeDtypeStruct((B, L, dil * OW), F32))
        out_specs.append(out_spec)
    outs = pl.pallas_call(
        kern,
        grid=(B, dil, nq // group),
        in_specs=in_specs,
        out_specs=out_specs,
        out_shape=out_shapes,
        compiler_params=_params("parallel", "parallel", "arbitrary"),
        name=name,
    )(*args)
    return outs


def _out_proj_cd_kernel(h_ref, o0, o1, o2, s0, s1, s2, d_ref, wc_ref, wd_ref, o_ref, *scratch):
    tm = h_ref.shape[0]
    width = C_HEADS_PER_GROUP * HEAD_DIM
    spare = iter(scratch)

    def in_position_order(ref, dil):
        if dil == 1:
            return ref[0]
        t_s = next(spare)
        for r in range(dil):
            for j in range(width // LANE):
                t_s[j, pl.ds(r, tm // dil, stride=dil), :] = ref[0, :, r * width + j * LANE:r * width + (j + 1) * LANE]
        return jnp.concatenate([t_s[j] for j in range(width // LANE)], axis=1)

    dils = [d for _, d in C_PATTERNS]
    outs = [in_position_order(r, d) for r, d in zip((o0, o1, o2), dils)]
    lses = [in_position_order(r, d) for r, d in zip((s0, s1, s2), dils)]
    mx = jnp.maximum(jnp.maximum(lses[0], lses[1]), lses[2])
    es = [jnp.exp(l - mx) for l in lses]
    oc = (es[0] * outs[0] + es[1] * outs[1] + es[2] * outs[2]) / (es[0] + es[1] + es[2])
    mix = jnp.dot(oc.astype(BF16), wc_ref[...], preferred_element_type=F32)
    mix = mix + jnp.dot(d_ref[...], wd_ref[...], preferred_element_type=F32)
    o_ref[...] = h_ref[...] + mix


def out_proj_cd(h, oc, lses, od, wc, wd, batch, seq_len):
    M = h.shape[0]
    tm = ROW_TILE
    nrow = seq_len // tm
    width = C_HEADS_PER_GROUP * HEAD_DIM
    row = lambda a: pl.BlockSpec((tm, a.shape[1]), lambda i: (i, 0))
    full = lambda a: pl.BlockSpec(a.shape, lambda i: (0, 0))
    views = [pl.BlockSpec((1, tm // d, d * width), lambda i: (i // nrow, i % nrow, 0)) for _, d in C_PATTERNS]
    n_spare = 2 * sum(d > 1 for _, d in C_PATTERNS)
    return pl.pallas_call(
        _out_proj_cd_kernel,
        grid=(M // tm,),
        in_specs=[row(h), *views, *views, row(od), full(wc), full(wd)],
        out_specs=pl.BlockSpec((tm, D_MODEL), lambda i: (i, 0)),
        out_shape=jax.ShapeDtypeStruct((M, D_MODEL), F32),
        scratch_shapes=[pltpu.VMEM((width // LANE, tm, LANE), F32)] * n_spare,
        compiler_params=_params("parallel"),
        name="out_proj1",
    )(h, *oc, *lses, od, wc, wd)


def _proj1_kernel(x_ref, g_ref, w_ref, cs_ref, c0_ref, c1_ref, c2_ref, y_s):
    xn = _rms(x_ref[...], g_ref[...]).astype(BF16)
    y = jnp.dot(xn, w_ref[...], preferred_element_type=F32) * cs_ref[...]
    tm = y.shape[0]
    width = y.shape[1] // len(C_PATTERNS)
    c0_ref[0] = y[:, :width].astype(BF16)
    nblk = width // LANE
    for j in range(y_s.shape[0]):
        y_s[j] = y[:, width + j * LANE:width + (j + 1) * LANE]
    for g, ref in ((1, c1_ref), (2, c2_ref)):
        dil = C_PATTERNS[g][1]
        for r in range(dil):
            for j in range(nblk):
                rows = y_s[(g - 1) * nblk + j, pl.ds(r, tm // dil, stride=dil), :]
                ref[0, :, r * width + j * LANE:r * width + (j + 1) * LANE] = rows.astype(BF16)


def proj1_call(x, g, w, colscale, batch, seq_len):
    M, K = x.shape
    N = w.shape[1]
    tm = ROW_TILE
    nrow = seq_len // tm
    width = N // len(C_PATTERNS)
    assert [d for _, d in C_PATTERNS][0] == 1
    out_specs = [pl.BlockSpec((1, tm // d, d * width), lambda i: (i // nrow, i % nrow, 0)) for _, d in C_PATTERNS]
    out_shape = [jax.ShapeDtypeStruct((batch, seq_len // d, d * width), BF16) for _, d in C_PATTERNS]
    return pl.pallas_call(
        _proj1_kernel,
        grid=(M // tm,),
        in_specs=[pl.BlockSpec((tm, K), lambda i: (i, 0)),
                  pl.BlockSpec((1, K), lambda i: (0, 0)),
                  pl.BlockSpec((K, N), lambda i: (0, 0)),
                  pl.BlockSpec((1, N), lambda i: (0, 0))],
        out_specs=out_specs,
        out_shape=out_shape,
        scratch_shapes=[pltpu.VMEM(((N - width) // LANE, tm, LANE), F32)],
        compiler_params=_params("parallel"),
        name="proj1",
    )(x, g.reshape(1, K), w, colscale.reshape(1, N))


def _ffn_kernel(*refs, n_mix, final_norm):
    x_ref, mix_refs = refs[0], refs[1:1 + 2 * n_mix]
    g_ref, wg_ref, wu_ref, wd_ref, fg_ref, o_ref, xn_s, acc_s, h_s = refs[1 + 2 * n_mix:]
    f = pl.program_id(1)

    @pl.when(f == 0)
    def _():
        h = x_ref[...]
        for a_ref, w_ref in zip(mix_refs[::2], mix_refs[1::2]):
            h = h + jnp.dot(a_ref[...], w_ref[...], preferred_element_type=F32)
        h_s[...] = h
        xn_s[...] = _rms(h, g_ref[...]).astype(BF16)
        acc_s[...] = jnp.zeros(acc_s.shape, F32)

    xn = xn_s[...]
    gate = jnp.dot(xn, wg_ref[0].astype(BF16), preferred_element_type=F32)
    up = jnp.dot(xn, wu_ref[0].astype(BF16), preferred_element_type=F32)
    mid = (gate / (1.0 + jnp.exp(-gate)) * up).astype(BF16)
    acc_s[...] += jnp.dot(mid, wd_ref[0].astype(BF16), preferred_element_type=F32)

    @pl.when(f == pl.num_programs(1) - 1)
    def _():
        y = h_s[...] + acc_s[...]
        if final_norm:
            y = _rms(y, fg_ref[...])
        o_ref[...] = y


def ffn(x, mix, g, wg, wu, wd, layer, fg, final_norm, name):
    M, K = x.shape
    tm, tf = FFN_ROW_TILE, FFN_COL_TILE
    kern = functools.partial(_ffn_kernel, n_mix=len(mix), final_norm=final_norm)
    mix_specs, mix_args = [], []
    for a, w in mix:
        mix_specs += [pl.BlockSpec((tm, a.shape[1]), lambda i, f: (i, 0)), pl.BlockSpec(w.shape, lambda i, f: (0, 0))]
        mix_args += [a, w]
    return pl.pallas_call(
        kern,
        grid=(M // tm, D_FF // tf),
        in_specs=[pl.BlockSpec((tm, K), lambda i, f: (i, 0)), *mix_specs,
                  pl.BlockSpec((1, K), lambda i, f: (0, 0)),
                  pl.BlockSpec((1, K, tf), lambda i, f: (layer, 0, f)),
                  pl.BlockSpec((1, K, tf), lambda i, f: (layer, 0, f)),
                  pl.BlockSpec((1, tf, K), lambda i, f: (layer, f, 0)),
                  pl.BlockSpec((1, K), lambda i, f: (0, 0))],
        out_specs=pl.BlockSpec((tm, K), lambda i, f: (i, 0)),
        out_shape=jax.ShapeDtypeStruct((M, K), F32),
        scratch_shapes=[pltpu.VMEM((tm, K), BF16), pltpu.VMEM((tm, K), F32), pltpu.VMEM((tm, K), F32)],
        compiler_params=_params("parallel", "arbitrary"),
        name=name,
    )(x, *mix_args, g.reshape(1, K), wg, wu, wd, fg.reshape(1, K))


def _trig_kernel(ang_ref, cos_ref, sin_ref):
    a = ang_ref[...]
    cos_ref[...] = jnp.cos(a)
    sin_ref[...] = jnp.sin(a)


def rope_tables(seq_len):
    half = D_ROPE // 2
    inv = ROPE_THETA ** (-jnp.arange(half, dtype=F32) / half)
    ang = jnp.arange(seq_len).astype(F32)[:, None] * inv[None, :]
    dense = ang.reshape(seq_len * half // LANE, LANE)
    spec = pl.BlockSpec(dense.shape, lambda: (0, 0))
    cos, sin = pl.pallas_call(
        _trig_kernel,
        in_specs=[spec],
        out_specs=[spec, spec],
        out_shape=[jax.ShapeDtypeStruct(dense.shape, F32)] * 2,
        name="rope_trig",
    )(dense)
    cos, sin = cos.reshape(seq_len, half), sin.reshape(seq_len, half)
    pad = LANE - D_NOPE - D_ROPE
    cos_l = jnp.concatenate([jnp.ones((seq_len, D_NOPE), F32), cos, cos, jnp.ones((seq_len, pad), F32)], axis=1)
    sin_l = jnp.concatenate([jnp.zeros((seq_len, D_NOPE), F32), sin, sin, jnp.zeros((seq_len, pad), F32)], axis=1)
    return cos_l, sin_l


def _prep_d_kernel(x_ref, g_ref, wa_ref, qn_ref, kvn_ref, wq_ref, wkv_ref, cos_ref, sin_ref,
                   q_ref, k_ref, v_ref, *, qscale):
    xn = _rms(x_ref[...], g_ref[...]).astype(BF16)
    lat = jnp.dot(xn, wa_ref[...], preferred_element_type=F32)
    cq = _rms(lat[:, :D_Q_LORA], qn_ref[...]).astype(BF16)
    ckv = _rms(lat[:, D_Q_LORA:D_Q_LORA + D_KV_LORA], kvn_ref[...]).astype(BF16)
    o_pe = D_Q_LORA + D_KV_LORA
    cos, sin = cos_ref[...], sin_ref[...]
    kpe = lat[:, o_pe:o_pe + LANE] * cos + lat[:, o_pe + LANE:o_pe + 2 * LANE] * sin
    qq = jnp.dot(cq, wq_ref[...], preferred_element_type=F32)
    kv = jnp.dot(ckv, wkv_ref[...], preferred_element_type=F32)
    kw = D_HEADS * LANE
    ones_blk = _ones_row_block(V_ROWS_D - D_V, x_ref.shape[0])
    for h in range(D_HEADS):
        qh = qq[:, h * LANE:(h + 1) * LANE] * cos + qq[:, kw + h * LANE:kw + (h + 1) * LANE] * sin
        q_ref[0, h, 0] = (qh * qscale).T.astype(BF16)
        k_ref[:, h * LANE:(h + 1) * LANE] = (kv[:, h * LANE:(h + 1) * LANE] + kpe).astype(BF16)
    for j in range(D_HEADS // 2):
        vt = kv[:, kw + j * LANE:kw + (j + 1) * LANE].T.astype(BF16)
        for half in range(2):
            v_ref[0, 2 * j + half, 0, :D_V] = vt[half * D_V:(half + 1) * D_V]
            v_ref[0, 2 * j + half, 0, D_V:] = ones_blk


def prep_d(x, g, wa, qn, kvn, wq, wkv, cos_l, sin_l, batch, seq_len):
    M, K = x.shape
    T = DENSE_TILE
    nrow = seq_len // T
    kw = D_HEADS * LANE
    kern = functools.partial(_prep_d_kernel, qscale=(D_NOPE + D_ROPE) ** -0.5 * LOG2E)
    full = lambda a: pl.BlockSpec(a.shape, lambda i: (0, 0))
    qn2, kvn2, g2 = qn.reshape(1, -1), kvn.reshape(1, -1), g.reshape(1, K)
    return pl.pallas_call(
        kern,
        grid=(M // T,),
        in_specs=[pl.BlockSpec((T, K), lambda i: (i, 0)), full(g2), full(wa), full(qn2), full(kvn2),
                  full(wq), full(wkv),
                  pl.BlockSpec((T, LANE), lambda i: (i % nrow, 0)),
                  pl.BlockSpec((T, LANE), lambda i: (i % nrow, 0))],
        out_specs=[pl.BlockSpec((1, D_HEADS, 1, LANE, T), lambda i: (i // nrow, 0, i % nrow, 0, 0)),
                   pl.BlockSpec((T, kw), lambda i: (i, 0)),
                   pl.BlockSpec((1, D_HEADS, 1, V_ROWS_D, T), lambda i: (i // nrow, 0, i % nrow, 0, 0))],
        out_shape=[jax.ShapeDtypeStruct((batch, D_HEADS, nrow, LANE, T), BF16),
                   jax.ShapeDtypeStruct((M, kw), BF16),
                   jax.ShapeDtypeStruct((batch, D_HEADS, nrow, V_ROWS_D, T), BF16)],
        compiler_params=_params("parallel"),
        name="prep_d",
    )(x, g2, wa, qn2, kvn2, wq, wkv, cos_l, sin_l)


B_HEAD_ORDER = (0, 4, 1, 5, 2, 6, 3, 7)


def _blocks(w, starts, width, axis):
    return jnp.concatenate([lax.slice_in_dim(w, s, s + width, axis=axis) for s in starts], axis=axis)


def _rot_partner_cols(w):
    half = D_ROPE // 2
    return jnp.concatenate([-w[..., half:], w[..., :half]], axis=-1)


def kernel(x, bias_table, attn_norm, ffn_norm, final_norm, ab_w_in, ab_lambda_q1, ab_lambda_k1,
           ab_lambda_q2, ab_lambda_k2, ab_subln, ab_sink, ab_w_o, cd_w_in, cd_q_norm, cd_w_q_b,
           cd_kv_norm, cd_w_kv_b, cd_w_o, ffn_w_gate, ffn_w_up, ffn_w_down):
    B, S, _ = x.shape
    M = B * S
    T = DENSE_TILE
    h = x.reshape(M, D_MODEL)
    qk_scale = HEAD_DIM ** -0.5 * LOG2E

    o3 = A_HEADS * (2 * A_QK_DIM + A_V_DIM)
    a0 = ab_w_in[0].astype(BF16)
    w0 = jnp.concatenate([a0[:, :o3], _blocks(a0, [o3 + hd * HEAD_DIM for hd in B_HEAD_ORDER], HEAD_DIM, 1),
                          a0[:, o3 + B_HEADS * HEAD_DIM:]], axis=1)
    cs0 = np.ones((AB_IN,), np.float32)
    cs0[:A_HEADS * A_QK_DIM] = qk_scale
    cs0[o3:o3 + B_HEADS * HEAD_DIM] = qk_scale
    qat, vat, qbt, vbt, keys0 = proj0_call(h, attn_norm[0], w0, jnp.asarray(cs0), B, S)
    keys0 = keys0.reshape(B, S, -1)
    bias_a = bias_tiles(bias_table, nvar=2 * BIAS_REACH + 1, nheads=A_HEADS, head0=0, rows=T, cols=T,
                        off0=-BIAS_REACH * T, off_step=T, row_coef=1, col_coef=-1, dil=1,
                        half_window=None, name="bias_a")
    oa = attn_a(qat, keys0, vat, bias_a, ab_lambda_q1[0], ab_lambda_k1[0], ab_lambda_q2[0],
                ab_lambda_k2[0], ab_subln[0], 0.8 - 0.6 * math.exp(-0.3 * 0))

    bias_b = bias_tiles(bias_table, nvar=3, nheads=B_HEADS, head0=A_HEADS, rows=QCOLS + 2 * B_HALF_WINDOW,
                        cols=QCOLS, off0=0, off_step=-B_HALF_WINDOW, row_coef=1, col_coef=-1, dil=1,
                        half_window=B_HALF_WINDOW, name="bias_b")
    ob = attn_b(qbt, keys0, (A_HEADS * A_QK_DIM) // LANE, vbt, bias_b, ab_sink[0])

    wo = ab_w_o[0].astype(BF16)
    wo_a = wo[:A_HEADS * A_V_DIM]
    wo_b = _blocks(wo, [A_HEADS * A_V_DIM + hd * HEAD_DIM for hd in B_HEAD_ORDER], HEAD_DIM, 0)
    h = ffn(h, [(oa.reshape(M, -1), wo_a), (ob.reshape(M, -1), wo_b)], ffn_norm[0], ffn_w_gate, ffn_w_up,
            ffn_w_down, 0, final_norm, False, "ffn0")

    w1 = cd_w_in[0]
    gw = C_HEADS_PER_GROUP * HEAD_DIM
    cw = C_HEADS * HEAD_DIM
    starts1 = [role * cw + g * gw for g in range(len(C_PATTERNS)) for role in range(3)]
    cs1 = np.ones((CD_C_IN,), np.float32)
    for g in range(len(C_PATTERNS)):
        cs1[3 * g * gw:(3 * g + 1) * gw] = qk_scale
    c_views = proj1_call(h, attn_norm[1], _blocks(w1.astype(BF16), starts1, gw, 1), jnp.asarray(cs1), B, S)

    oc, lses = [], []
    for g, (window, dil) in enumerate(C_PATTERNS):
        hw = window // (2 * dil)
        tq_c = 128
        bias_c = bias_tiles(bias_table, nvar=3, nheads=C_HEADS_PER_GROUP, head0=g * C_HEADS_PER_GROUP,
                            rows=tq_c, cols=tq_c + 2 * hw, off0=0, off_step=-hw, row_coef=-1, col_coef=1,
                            dil=dil, half_window=hw, name=f"bias_c{g}")
        o_g, lse_g = banded(c_views[g], bias_c, None, dil=dil, tq=tq_c, half_window=hw, nqb=2, nkb=2,
                            q_idx=lambda r: 3 * r, k_idx=lambda r: 3 * r + 1, v_idx=lambda r: 3 * r + 2,
                            head_of=lambda c, half: 2 * c + half, out_dtype=F32, has_lse=True,
                            name=f"attn_c{g}")
        oc.append(o_g)
        lses.append(lse_g)

    o_q, o_kv = CD_C_IN + D_Q_LORA, CD_C_IN + D_Q_LORA + D_KV_LORA
    w_pe = w1[:, o_kv:]
    lane_pad = lambda w: jnp.pad(w, ((0, 0), (D_NOPE, LANE - D_NOPE - D_ROPE)))
    wa = jnp.concatenate([w1[:, CD_C_IN:o_kv], lane_pad(w_pe), lane_pad(_rot_partner_cols(w_pe))],
                         axis=1).astype(BF16)
    wq3 = cd_w_q_b[0].reshape(D_Q_LORA, D_HEADS, D_NOPE + D_ROPE)
    zpad = jnp.zeros((D_Q_LORA, D_HEADS, LANE - D_NOPE - D_ROPE), F32)
    wq_main = jnp.concatenate([wq3, zpad], axis=-1)
    wq_rot = jnp.concatenate([jnp.zeros_like(wq3[..., :D_NOPE]), _rot_partner_cols(wq3[..., D_NOPE:]), zpad],
                             axis=-1)
    wq = jnp.concatenate([wq_main.reshape(D_Q_LORA, -1), wq_rot.reshape(D_Q_LORA, -1)], axis=1).astype(BF16)
    wkv3 = cd_w_kv_b[0].reshape(D_KV_LORA, D_HEADS, D_NOPE + D_V)
    wk = jnp.pad(wkv3[..., :D_NOPE], ((0, 0), (0, 0), (0, LANE - D_NOPE))).reshape(D_KV_LORA, -1)
    wv = wkv3[..., D_NOPE:].reshape(D_KV_LORA, -1)
    wkv = jnp.concatenate([wk, wv], axis=1).astype(BF16)
    cos_l, sin_l = rope_tables(S)
    qdt, kd, vdt = prep_d(h, attn_norm[1], wa, cd_q_norm[0], cd_kv_norm[0], wq, wkv, cos_l, sin_l, B, S)
    od = attn_d(qdt, kd.reshape(B, S, D_HEADS * LANE), vdt)

    wo1 = cd_w_o[0]
    wo_c = wo1[:C_HEADS_PER_GROUP * HEAD_DIM].astype(BF16)
    wo_d = wo1[C_HEADS_PER_GROUP * HEAD_DIM:].astype(BF16)
    h = out_proj_cd(h, oc, lses, od.reshape(M, -1), wo_c, wo_d, B, S)
    h = ffn(h, [], ffn_norm[1], ffn_w_gate, ffn_w_up, ffn_w_down, 1, final_norm, True, "ffn1")
    return h.reshape(B, S, D_MODEL)
```

```python
import functools
import math

import numpy as np
import jax
import jax.numpy as jnp
from jax import lax
from jax.experimental import pallas as pl
from jax.experimental.pallas import tpu as pltpu

F32 = jnp.float32
BF16 = jnp.bfloat16

D_MODEL = 1024
HEAD_DIM = 64
EPS = 1e-6
NEG = -1e30
LOG2E = math.log2(math.e)
LN2 = math.log(2.0)

A_HEADS = 4
A_QK_DIM = 2 * HEAD_DIM
A_V_DIM = 2 * HEAD_DIM
B_HEADS = 8
B_KV_HEADS = 2
B_HALF_WINDOW = 128
C_PATTERNS = ((128, 1), (512, 4), (2048, 16))
C_HEADS_PER_GROUP = 4
C_HEADS = C_HEADS_PER_GROUP * len(C_PATTERNS)
D_HEADS = 12
D_Q_LORA = 384
D_KV_LORA = 256
D_NOPE = 64
D_ROPE = 32
D_V = 64
ROPE_THETA = 10000.0
NUM_BUCKETS = 32
MAX_DISTANCE = 1024
D_FF = 2816
AB_IN = 2304
CD_C_IN = 3 * C_HEADS * HEAD_DIM

LANE = 128
VMEM_LIMIT = 48 * 1024 * 1024

ROW_TILE = 512
FFN_ROW_TILE = 1024
FFN_COL_TILE = 256
DENSE_TILE = 512
QCOLS = 256
UNROLL = 4
A_QTILES = 2
D_QTILES = 4
BAND_GROUP = 4
BIAS_REACH = 3
V_ROWS_D = 80
V_ROWS_A = 144


def _bucket_thresholds():
    nb = NUM_BUCKETS // 2
    max_exact = nb // 2
    n = np.arange(1, 4 * MAX_DISTANCE)
    large = max_exact + (np.log(n.astype(np.float32) / np.float32(max_exact))
                         / np.float32(math.log(MAX_DISTANCE / max_exact))
                         * np.float32(nb - max_exact)).astype(np.int32)
    mag = np.where(n < max_exact, n, np.minimum(large, nb - 1))
    return tuple(int(n[np.argmax(mag >= k)]) for k in range(1, nb))


BUCKET_THRESHOLDS = _bucket_thresholds()
assert BUCKET_THRESHOLDS[-1] <= (BIAS_REACH - 1) * DENSE_TILE + 1


def _params(*sem):
    return pltpu.CompilerParams(dimension_semantics=sem, vmem_limit_bytes=VMEM_LIMIT)


def _rms(x, g):
    return x * lax.rsqrt(jnp.mean(x * x, axis=-1, keepdims=True) + EPS) * g


def _ones_row_block(rows, cols):
    r = lax.broadcasted_iota(jnp.int32, (rows, cols), 0)
    return jnp.where(r == 0, 1.0, 0.0).astype(BF16)


def _store_masked_halves(q_ref, idx, qt):
    zeros = jnp.zeros((HEAD_DIM, qt.shape[1]), BF16)
    q_ref[idx + (0, 0, slice(None, HEAD_DIM))] = qt[:HEAD_DIM]
    q_ref[idx + (0, 0, slice(HEAD_DIM, None))] = zeros
    q_ref[idx + (1, 0, slice(None, HEAD_DIM))] = zeros
    q_ref[idx + (1, 0, slice(HEAD_DIM, None))] = qt[HEAD_DIM:]


def _proj0_kernel(x_ref, g_ref, w_ref, cs_ref, qa_ref, va_ref, qb_ref, vb_ref, k_ref):
    xn = _rms(x_ref[...], g_ref[...]).astype(BF16)
    y = jnp.dot(xn, w_ref[...], preferred_element_type=F32) * cs_ref[...]
    tm = y.shape[0]
    ka0 = A_HEADS * A_QK_DIM
    va0 = 2 * ka0
    qb0 = va0 + A_HEADS * A_V_DIM
    kb0 = qb0 + B_HEADS * HEAD_DIM
    vb0 = kb0 + B_KV_HEADS * HEAD_DIM
    for h in range(A_HEADS):
        _store_masked_halves(qa_ref, (0, h), y[:, h * A_QK_DIM:(h + 1) * A_QK_DIM].T.astype(BF16))
        va_ref[0, h, 0, :A_V_DIM] = y[:, va0 + h * A_V_DIM:va0 + (h + 1) * A_V_DIM].T.astype(BF16)
        va_ref[0, h, 0, A_V_DIM:] = _ones_row_block(V_ROWS_A - A_V_DIM, tm)
    for j in range(B_HEADS // 2):
        _store_masked_halves(qb_ref, (0, j), y[:, qb0 + j * LANE:qb0 + (j + 1) * LANE].T.astype(BF16))
    vbt = y[:, vb0:vb0 + LANE].T.astype(BF16)
    ones_blk = _ones_row_block(V_ROWS_D - HEAD_DIM, LANE)
    for g in range(B_KV_HEADS):
        for c in range(tm // LANE):
            vb_ref[0, g, c, :HEAD_DIM] = vbt[g * HEAD_DIM:(g + 1) * HEAD_DIM, c * LANE:(c + 1) * LANE]
            vb_ref[0, g, c, HEAD_DIM:] = ones_blk
    k_ref[:, :ka0] = y[:, ka0:va0].astype(BF16)
    k_ref[:, ka0:] = y[:, kb0:vb0].astype(BF16)


def proj0_call(x, g, w, colscale, batch, seq_len):
    M, K = x.shape
    N = w.shape[1]
    T = DENSE_TILE
    nrow = seq_len // T
    nkb = A_HEADS * A_QK_DIM + B_KV_HEADS * HEAD_DIM
    qspec = pl.BlockSpec((1, A_HEADS, 2, 1, LANE, T), lambda i: (i // nrow, 0, 0, i % nrow, 0, 0))
    qshape = jax.ShapeDtypeStruct((batch, A_HEADS, 2, nrow, LANE, T), BF16)
    return pl.pallas_call(
        _proj0_kernel,
        grid=(M // T,),
        in_specs=[pl.BlockSpec((T, K), lambda i: (i, 0)),
                  pl.BlockSpec((1, K), lambda i: (0, 0)),
                  pl.BlockSpec((K, N), lambda i: (0, 0)),
                  pl.BlockSpec((1, N), lambda i: (0, 0))],
        out_specs=[qspec,
                   pl.BlockSpec((1, A_HEADS, 1, V_ROWS_A, T), lambda i: (i // nrow, 0, i % nrow, 0, 0)),
                   qspec,
                   pl.BlockSpec((1, B_KV_HEADS, T // LANE, V_ROWS_D, LANE),
                                lambda i: (i // nrow, 0, i % nrow, 0, 0)),
                   pl.BlockSpec((T, nkb), lambda i: (i, 0))],
        out_shape=[qshape,
                   jax.ShapeDtypeStruct((batch, A_HEADS, nrow, V_ROWS_A, T), BF16),
                   qshape,
                   jax.ShapeDtypeStruct((batch, B_KV_HEADS, seq_len // LANE, V_ROWS_D, LANE), BF16),
                   jax.ShapeDtypeStruct((M, nkb), BF16)],
        compiler_params=_params("parallel"),
        name="proj0",
    )(x, g.reshape(1, K), w, colscale.reshape(1, N))


def _bias_kernel(tab_ref, o_ref, *, nvar, off0, off_step, row_coef, col_coef, dil, half_window, head0):
    hcol = head0 + pl.program_id(1)
    R, C = o_ref.shape[-2:]
    row = lax.broadcasted_iota(jnp.int32, (R, C), 0)
    col = lax.broadcasted_iota(jnp.int32, (R, C), 1)
    base = row_coef * row + col_coef * col
    span_lo = min(row_coef * (R - 1), 0) + min(col_coef * (C - 1), 0)
    span_hi = max(row_coef * (R - 1), 0) + max(col_coef * (C - 1), 0)
    nb = NUM_BUCKETS // 2

    def side(n, n_lo, n_hi, row0):
        val = jnp.full((R, C), tab_ref[row0 + sum(t <= n_lo for t in BUCKET_THRESHOLDS), hcol], F32)
        for k, thr in enumerate(BUCKET_THRESHOLDS, start=1):
            if n_lo < thr <= n_hi:
                val = jnp.where(n >= thr, tab_ref[row0 + k, hcol], val)
        return val

    for v in range(nvar):
        @pl.when(pl.program_id(0) == v)
        def _(v=v):
            off = off0 + v * off_step
            rel = off + base
            lo, hi = (off + span_lo) * dil, (off + span_hi) * dil
            dist = rel * dil
            n = jnp.abs(dist)
            if hi <= 0:
                val = side(n, -hi, -lo, 0)
            elif lo > 0:
                val = side(n, lo, hi, nb)
            else:
                val = jnp.where(dist > 0, side(n, 1, hi, nb), side(n, 0, -lo, 0))
            val = val * LOG2E
            if half_window is not None:
                val = jnp.where(jnp.abs(rel) <= half_window, val, NEG)
            o_ref[0, 0] = val


def bias_tiles(table, *, nvar, nheads, head0, rows, cols, off0, off_step, row_coef, col_coef,
               dil, half_window, name):
    kern = functools.partial(_bias_kernel, nvar=nvar, off0=off0, off_step=off_step, row_coef=row_coef,
                             col_coef=col_coef, dil=dil, half_window=half_window, head0=head0)
    return pl.pallas_call(
        kern,
        grid=(nvar, nheads),
        in_specs=[pl.BlockSpec(memory_space=pltpu.SMEM)],
        out_specs=pl.BlockSpec((1, 1, rows, cols), lambda v, h: (v, h, 0, 0)),
        out_shape=jax.ShapeDtypeStruct((nvar, nheads, rows, cols), F32),
        compiler_params=_params("parallel", "parallel"),
        name=name,
    )(table)


def _dense_pipeline(nk, tile, n_streams, score_fn, value_fn, m_s, acc_s, bufs):
    units = [(i, c * QCOLS) for i in range(n_streams) for c in range(tile // QCOLS)]

    def produce(kc, unit, nxt):
        i, c0 = unit
        cols = slice(c0, c0 + QCOLS)
        s = score_fn(kc + 1, i, cols)
        nxt[0][i, :, cols] = s
        nxt[1][i, :, cols] = jnp.max(s, axis=0, keepdims=True)

    def consume(kc, unit, cur):
        i, c0 = unit
        cols = slice(c0, c0 + QCOLS)
        m_old = m_s[i, :, cols]
        m_new = jnp.maximum(m_old, cur[1][i, :, cols])
        alpha = jnp.exp2(m_old - m_new)
        p = jnp.exp2((cur[0][i, :, cols] - m_new).astype(BF16))
        acc_s[i, :, cols] = (alpha * acc_s[i, :, cols]
                             + jnp.dot(value_fn(kc, i), p, preferred_element_type=F32))
        m_s[i, :, cols] = m_new

    def stage(kc, cur, nxt):
        for unit in units:
            if nxt is not None:
                produce(kc, unit, nxt)
            if cur is not None:
                consume(kc, unit, cur)

    m_s[...] = jnp.full(m_s.shape, NEG, F32)
    acc_s[...] = jnp.zeros(acc_s.shape, F32)
    stage(-1, None, bufs[0])
    n_loop = (nk - 1) // UNROLL

    def body(j, carry):
        for u in range(UNROLL):
            stage(UNROLL * j + u, bufs[u % 2], bufs[(u + 1) % 2])
        return carry

    lax.fori_loop(0, n_loop, body, 0)
    for kc in range(n_loop * UNROLL, nk):
        stage(kc, bufs[kc % 2], bufs[(kc + 1) % 2] if kc < nk - 1 else None)


def _attn_a_kernel(q_ref, k_ref, v_ref, bias_ref, lq1_ref, lk1_ref, lq2_ref, lk2_ref, subln_ref, o_ref,
                   m_s, acc_s, s_a, s_b, cm_a, cm_b, *, tile, nk, qtiles, lambda_init):
    qi0 = pl.program_id(2) * qtiles

    def score_fn(kc, i, cols):
        j, qt = divmod(i, qtiles)
        kblk = k_ref[0, pl.ds(pl.multiple_of(kc * tile, tile), tile), :]
        bt = bias_ref[jnp.clip(kc - (qi0 + qt), -BIAS_REACH, BIAS_REACH) + BIAS_REACH, 0, :, cols]
        return jnp.dot(kblk, q_ref[0, 0, j, qt, :, cols], preferred_element_type=F32) + bt

    def value_fn(kc, i):
        return v_ref[0, 0, kc]

    _dense_pipeline(nk, tile, 2 * qtiles, score_fn, value_fn, m_s, acc_s, ((s_a, cm_a), (s_b, cm_b)))

    lam = (jnp.exp(jnp.sum(lq1_ref[...] * lk1_ref[...], axis=-1, keepdims=True))
           - jnp.exp(jnp.sum(lq2_ref[...] * lk2_ref[...], axis=-1, keepdims=True)) + lambda_init)
    for qt in range(qtiles):
        a1, a2 = acc_s[qt], acc_s[qtiles + qt]
        o = (a1[:A_V_DIM] / a1[A_V_DIM:A_V_DIM + 1]
             - lam * (a2[:A_V_DIM] / a2[A_V_DIM:A_V_DIM + 1]))
        ms = jnp.mean(o * o, axis=0, keepdims=True)
        y = o * lax.rsqrt(ms + EPS) * subln_ref[...] * (1.0 - lambda_init)
        o_ref[0, qt * tile:(qt + 1) * tile] = y.T.astype(o_ref.dtype)


def attn_a(qt, karr, vt, bias, lq1, lk1, lq2, lk2, subln, lambda_init):
    B, S, _ = karr.shape
    T = DENSE_TILE
    nq = nk = S // T
    assert nk % 2 == 0 and nk >= 4
    qtiles = math.gcd(nq, A_QTILES)
    ns = 2 * qtiles
    kern = functools.partial(_attn_a_kernel, tile=T, nk=nk, qtiles=qtiles, lambda_init=lambda_init)
    vec = lambda n: pl.BlockSpec((1, n), lambda h, b, i: (0, 0))
    return pl.pallas_call(
        kern,
        grid=(A_HEADS, B, nq // qtiles),
        in_specs=[pl.BlockSpec((1, 1, 2, qtiles, LANE, T), lambda h, b, i: (b, h, 0, i, 0, 0)),
                  pl.BlockSpec((1, S, LANE), lambda h, b, i: (b, 0, h)),
                  pl.BlockSpec((1, 1, nk, V_ROWS_A, T), lambda h, b, i: (b, h, 0, 0, 0)),
                  pl.BlockSpec((2 * BIAS_REACH + 1, 1, T, T), lambda h, b, i: (0, h, 0, 0)),
                  vec(HEAD_DIM), vec(HEAD_DIM), vec(HEAD_DIM), vec(HEAD_DIM),
                  pl.BlockSpec((A_V_DIM, 1), lambda h, b, i: (0, 0))],
        out_specs=pl.BlockSpec((1, qtiles * T, LANE), lambda h, b, i: (b, i, h)),
        out_shape=jax.ShapeDtypeStruct((B, S, A_HEADS * A_V_DIM), BF16),
        scratch_shapes=[pltpu.VMEM((ns, 1, T), F32), pltpu.VMEM((ns, V_ROWS_A, T), F32),
                        pltpu.VMEM((ns, T, T), F32), pltpu.VMEM((ns, T, T), F32),
                        pltpu.VMEM((ns, 1, T), F32), pltpu.VMEM((ns, 1, T), F32)],
        compiler_params=_params("parallel", "parallel", "arbitrary"),
        name="attn_a",
    )(qt, karr, vt, bias, lq1.reshape(1, -1), lk1.reshape(1, -1), lq2.reshape(1, -1),
      lk2.reshape(1, -1), subln.reshape(-1, 1))


def _attn_d_kernel(q_ref, k_ref, v_ref, o_ref, m_s, acc_s, s_a, s_b, cm_a, cm_b, *, tile, nk, qtiles):
    def score_fn(kc, i, cols):
        hh, qt = divmod(i, qtiles)
        kblk = k_ref[0, pl.ds(pl.multiple_of(kc * tile, tile), tile), hh * LANE:(hh + 1) * LANE]
        return jnp.dot(kblk, q_ref[0, hh, qt, :, cols], preferred_element_type=F32)

    def value_fn(kc, i):
        return v_ref[0, i // qtiles, kc]

    _dense_pipeline(nk, tile, 2 * qtiles, score_fn, value_fn, m_s, acc_s, ((s_a, cm_a), (s_b, cm_b)))
    for qt in range(qtiles):
        outs = []
        for hh in range(2):
            acc = acc_s[hh * qtiles + qt]
            outs.append(acc[:D_V] / acc[D_V:D_V + 1])
        o_ref[0, qt * tile:(qt + 1) * tile] = jnp.concatenate(outs, axis=0).T.astype(o_ref.dtype)


def attn_d(qt, k, vt):
    B, S, _ = k.shape
    T = DENSE_TILE
    nq = nk = S // T
    qtiles = math.gcd(nq, D_QTILES)
    ns = 2 * qtiles
    kern = functools.partial(_attn_d_kernel, tile=T, nk=nk, qtiles=qtiles)
    return pl.pallas_call(
        kern,
        grid=(B, D_HEADS // 2, nq // qtiles),
        in_specs=[pl.BlockSpec((1, 2, qtiles, LANE, T), lambda b, h, i: (b, h, i, 0, 0)),
                  pl.BlockSpec((1, S, 2 * LANE), lambda b, h, i: (b, 0, h)),
                  pl.BlockSpec((1, 2, nk, V_ROWS_D, T), lambda b, h, i: (b, h, 0, 0, 0))],
        out_specs=pl.BlockSpec((1, qtiles * T, LANE), lambda b, h, i: (b, i, h)),
        out_shape=jax.ShapeDtypeStruct((B, S, D_HEADS * D_V), BF16),
        scratch_shapes=[pltpu.VMEM((ns, 1, T), F32), pltpu.VMEM((ns, V_ROWS_D, T), F32),
                        pltpu.VMEM((ns, T, T), F32), pltpu.VMEM((ns, T, T), F32),
                        pltpu.VMEM((ns, 1, T), F32), pltpu.VMEM((ns, 1, T), F32)],
        compiler_params=_params("parallel", "parallel", "arbitrary"),
        name="attn_d",
    )(qt, k, vt)


def _attn_b_kernel(q_ref, k_ref, v_ref, bias_ref, sink_ref, o_ref, s_a, s_b, s_c, cm_a, cm_b, cm_c, *, tile, seq_len):
    half_window = B_HALF_WINDOW
    span = QCOLS + 2 * half_window
    nchunk = span // LANE
    step = pl.program_id(1)

    def window(c):
        q0 = step * tile + c * QCOLS
        start = pl.multiple_of(jnp.clip(q0 - half_window, 0, seq_len - span), LANE)
        variant = jnp.where(q0 == 0, 0, jnp.where(q0 + QCOLS == seq_len, 2, 1))
        return start, variant

    def scores(task):
        c, j, g = task
        start, variant = window(c)
        kwin = k_ref[0, pl.ds(start, span), :]
        s = jnp.dot(kwin, q_ref[0, j, g, 0, :, c * QCOLS:(c + 1) * QCOLS], preferred_element_type=F32)
        return s + bias_ref[variant, j + (B_HEADS // 2) * g]

    def finish(task, s_ref, cm_ref):
        c, j, g = task
        head = j + (B_HEADS // 2) * g
        start, _ = window(c)
        chunk0 = start // LANE
        vwin = jnp.concatenate([v_ref[0, g, chunk0 + n] for n in range(nchunk)], axis=1)
        sk = sink_ref[:, head:head + 1] * LOG2E
        m = jnp.maximum(cm_ref[...], sk)
        e = jnp.exp2((s_ref[...] - m).astype(BF16))
        ov = jnp.dot(vwin, e, preferred_element_type=F32)
        denom = ov[HEAD_DIM:HEAD_DIM + 1] + jnp.exp2(sk - m)
        return ov[:HEAD_DIM] / denom

    tasks = [(c, j, g) for c in range(tile // QCOLS) for j in range(B_HEADS // 2) for g in range(B_KV_HEADS)]
    bufs = ((s_a, cm_a), (s_b, cm_b), (s_c, cm_c))

    def produce(task, buf):
        s = scores(task)
        buf[0][...] = s
        buf[1][...] = jnp.max(s, axis=0, keepdims=True)

    produce(tasks[0], bufs[0])
    produce(tasks[1], bufs[1])
    held = None
    for n, task in enumerate(tasks):
        if n + 2 < len(tasks):
            produce(tasks[n + 2], bufs[(n + 2) % 3])
        o = finish(task, *bufs[n % 3])
        c, j, g = task
        if g == 0:
            held = o
            continue
        pair = jnp.concatenate([held, o], axis=0).T
        o_ref[0, c * QCOLS:(c + 1) * QCOLS, j * LANE:(j + 1) * LANE] = pair.astype(o_ref.dtype)


def attn_b(qt, karr, k_block, vt, bias, sink):
    B, S, _ = karr.shape
    T = DENSE_TILE
    span = QCOLS + 2 * B_HALF_WINDOW
    return pl.pallas_call(
        functools.partial(_attn_b_kernel, tile=T, seq_len=S),
        grid=(B, S // T),
        in_specs=[pl.BlockSpec((1, B_HEADS // 2, 2, 1, LANE, T), lambda b, i: (b, 0, 0, i, 0, 0)),
                  pl.BlockSpec((1, S, LANE), lambda b, i: (b, 0, k_block)),
                  pl.BlockSpec((1, B_KV_HEADS, S // LANE, V_ROWS_D, LANE), lambda b, i: (b, 0, 0, 0, 0)),
                  pl.BlockSpec(bias.shape, lambda b, i: (0, 0, 0, 0)),
                  pl.BlockSpec((1, B_HEADS), lambda b, i: (0, 0))],
        out_specs=pl.BlockSpec((1, T, B_HEADS * HEAD_DIM), lambda b, i: (b, i, 0)),
        out_shape=jax.ShapeDtypeStruct((B, S, B_HEADS * HEAD_DIM), BF16),
        scratch_shapes=[pltpu.VMEM((span, QCOLS), F32)] * 3 + [pltpu.VMEM((1, QCOLS), F32)] * 3,
        compiler_params=_params("parallel", "arbitrary"),
        name="attn_b",
    )(qt, karr, vt, bias, sink.reshape(1, -1))


def _banded_kernel(*refs, tq, group, span, half_window, seq_len, nqb, nkb, head_of, has_sink, has_lse):
    it = iter(refs)
    q_ref, k_ref, v_ref, bias_ref = next(it), next(it), next(it), next(it)
    sink_ref = next(it) if has_sink else None
    o_ref = next(it)
    lse_ref = next(it) if has_lse else None

    nq = seq_len // tq
    lane = lax.broadcasted_iota(jnp.int32, (tq, LANE), 1)
    low = lane < HEAD_DIM

    def window(g):
        t = pl.program_id(2) * group + g
        start = pl.multiple_of(jnp.clip(t * tq - half_window, 0, seq_len - span), half_window)
        variant = jnp.where(t == 0, 0, jnp.where(t == nq - 1, 2, 1))
        return start, variant

    def scores(task):
        g, c, half = task
        start, variant = window(g)
        kc = c if nkb == nqb else 0
        q2 = q_ref[0, g * tq:(g + 1) * tq, c * LANE:(c + 1) * LANE]
        kb = k_ref[0, pl.ds(start, span), kc * LANE:(kc + 1) * LANE]
        qm = jnp.where(low if half == 0 else jnp.logical_not(low), q2, jnp.zeros_like(q2))
        s = lax.dot_general(qm, kb, (((1,), (1,)), ((), ())), preferred_element_type=F32)
        return s + bias_ref[variant, head_of(c, half)]

    def finish(task, s):
        g, c, half = task
        start, _ = window(g)
        kc = c if nkb == nqb else 0
        hidx = head_of(c, half)
        vb = v_ref[0, pl.ds(start, span), kc * LANE:(kc + 1) * LANE]
        m = jnp.max(s, axis=-1, keepdims=True)
        if has_sink:
            sk = sink_ref[:, hidx:hidx + 1] * LOG2E
            m = jnp.maximum(m, sk)
        e = jnp.exp2(s - m)
        denom = jnp.sum(e, axis=-1, keepdims=True)
        if has_sink:
            denom = denom + jnp.exp2(sk - m)
        o = jnp.dot(e.astype(BF16), vb, preferred_element_type=F32) / denom
        return o, (LN2 * m + jnp.log(denom) if has_lse else None)

    tasks = [(g, c, half) for g in range(group) for c in range(nqb) for half in range(2)]
    s_next = scores(tasks[0])
    held = None
    for n, task in enumerate(tasks):
        s = s_next
        if n + 1 < len(tasks):
            s_next = scores(tasks[n + 1])
        o, lse = finish(task, s)
        g, c, half = task
        if half == 0:
            held = (o, lse)
            continue
        rows, cols = slice(g * tq, (g + 1) * tq), slice(c * LANE, (c + 1) * LANE)
        o_ref[0, rows, cols] = jnp.where(low, held[0], o).astype(o_ref.dtype)
        if has_lse:
            lse_ref[0, rows, cols] = jnp.where(low, held[1], lse)


def banded(view, bias, sink, *, dil, tq, half_window, nqb, nkb, q_idx, k_idx, v_idx, head_of,
           out_dtype, has_lse, name):
    B, L, _ = view.shape
    span = tq + 2 * half_window
    nq = L // tq
    assert L % tq == 0 and L >= span and nq >= 2
    group = math.gcd(nq, BAND_GROUP)
    OW = nqb * LANE
    has_sink = sink is not None
    kern = functools.partial(_banded_kernel, tq=tq, group=group, span=span, half_window=half_window,
                             seq_len=L, nqb=nqb, nkb=nkb, head_of=head_of, has_sink=has_sink,
                             has_lse=has_lse)
    in_specs = [pl.BlockSpec((1, group * tq, OW), lambda b, r, t: (b, t, q_idx(r))),
                pl.BlockSpec((1, L, nkb * LANE), lambda b, r, t: (b, 0, k_idx(r))),
                pl.BlockSpec((1, L, nkb * LANE), lambda b, r, t: (b, 0, v_idx(r))),
                pl.BlockSpec(bias.shape, lambda b, r, t: (0, 0, 0, 0))]
    args = [view, view, view, bias]
    if has_sink:
        in_specs.append(pl.BlockSpec((1, sink.shape[-1]), lambda b, r, t: (0, 0)))
        args.append(sink.reshape(1, -1))
    out_spec = pl.BlockSpec((1, group * tq, OW), lambda b, r, t: (b, t, r))
    out_shapes = [jax.ShapeDtypeStruct((B, L, dil * OW), out_dtype)]
    out_specs = [out_spec]
    if has_lse:
        out_shapes.append(jax.ShapeDtypeStruct((B, L, dil * OW), F32))
        out_specs.append(out_spec)
    outs = pl.pallas_call(
        kern,
        grid=(B, dil, nq // group),
        in_specs=in_specs,
        out_specs=out_specs,
        out_shape=out_shapes,
        compiler_params=_params("parallel", "parallel", "arbitrary"),
        name=name,
    )(*args)
    return outs


def _out_proj_cd_kernel(h_ref, o0, o1, o2, s0, s1, s2, d_ref, wc_ref, wd_ref, o_ref, *scratch):
    tm = h_ref.shape[0]
    width = C_HEADS_PER_GROUP * HEAD_DIM
    spare = iter(scratch)

    def in_position_order(ref, dil):
        if dil == 1:
            return ref[0]
        t_s = next(spare)
        for r in range(dil):
            for j in range(width // LANE):
                t_s[j, pl.ds(r, tm // dil, stride=dil), :] = ref[0, :, r * width + j * LANE:r * width + (j + 1) * LANE]
        return jnp.concatenate([t_s[j] for j in range(width // LANE)], axis=1)

    dils = [d for _, d in C_PATTERNS]
    outs = [in_position_order(r, d) for r, d in zip((o0, o1, o2), dils)]
    lses = [in_position_order(r, d) for r, d in zip((s0, s1, s2), dils)]
    mx = jnp.maximum(jnp.maximum(lses[0], lses[1]), lses[2])
    es = [jnp.exp(l - mx) for l in lses]
    oc = (es[0] * outs[0] + es[1] * outs[1] + es[2] * outs[2]) / (es[0] + es[1] + es[2])
    mix = jnp.dot(oc.astype(BF16), wc_ref[...], preferred_element_type=F32)
    mix = mix + jnp.dot(d_ref[...], wd_ref[...], preferred_element_type=F32)
    o_ref[...] = h_ref[...] + mix


def out_proj_cd(h, oc, lses, od, wc, wd, batch, seq_len):
    M = h.shape[0]
    tm = ROW_TILE
    nrow = seq_len // tm
    width = C_HEADS_PER_GROUP * HEAD_DIM
    row = lambda a: pl.BlockSpec((tm, a.shape[1]), lambda i: (i, 0))
    full = lambda a: pl.BlockSpec(a.shape, lambda i: (0, 0))
    views = [pl.BlockSpec((1, tm // d, d * width), lambda i: (i // nrow, i % nrow, 0)) for _, d in C_PATTERNS]
    n_spare = 2 * sum(d > 1 for _, d in C_PATTERNS)
    return pl.pallas_call(
        _out_proj_cd_kernel,
        grid=(M // tm,),
        in_specs=[row(h), *views, *views, row(od), full(wc), full(wd)],
        out_specs=pl.BlockSpec((tm, D_MODEL), lambda i: (i, 0)),
        out_shape=jax.ShapeDtypeStruct((M, D_MODEL), F32),
        scratch_shapes=[pltpu.VMEM((width // LANE, tm, LANE), F32)] * n_spare,
        compiler_params=_params("parallel"),
        name="out_proj1",
    )(h, *oc, *lses, od, wc, wd)


def _proj1_kernel(x_ref, g_ref, w_ref, cs_ref, c0_ref, c1_ref, c2_ref, y_s):
    xn = _rms(x_ref[...], g_ref[...]).astype(BF16)
    y = jnp.dot(xn, w_ref[...], preferred_element_type=F32) * cs_ref[...]
    tm = y.shape[0]
    width = y.shape[1] // len(C_PATTERNS)
    c0_ref[0] = y[:, :width].astype(BF16)
    nblk = width // LANE
    for j in range(y_s.shape[0]):
        y_s[j] = y[:, width + j * LANE:width + (j + 1) * LANE]
    for g, ref in ((1, c1_ref), (2, c2_ref)):
        dil = C_PATTERNS[g][1]
        for r in range(dil):
            for j in range(nblk):
                rows = y_s[(g - 1) * nblk + j, pl.ds(r, tm // dil, stride=dil), :]
                ref[0, :, r * width + j * LANE:r * width + (j + 1) * LANE] = rows.astype(BF16)


def proj1_call(x, g, w, colscale, batch, seq_len):
    M, K = x.shape
    N = w.shape[1]
    tm = ROW_TILE
    nrow = seq_len // tm
    width = N // len(C_PATTERNS)
    assert [d for _, d in C_PATTERNS][0] == 1
    out_specs = [pl.BlockSpec((1, tm // d, d * width), lambda i: (i // nrow, i % nrow, 0)) for _, d in C_PATTERNS]
    out_shape = [jax.ShapeDtypeStruct((batch, seq_len // d, d * width), BF16) for _, d in C_PATTERNS]
    return pl.pallas_call(
        _proj1_kernel,
        grid=(M // tm,),
        in_specs=[pl.BlockSpec((tm, K), lambda i: (i, 0)),
                  pl.BlockSpec((1, K), lambda i: (0, 0)),
                  pl.BlockSpec((K, N), lambda i: (0, 0)),
                  pl.BlockSpec((1, N), lambda i: (0, 0))],
        out_specs=out_specs,
        out_shape=out_shape,
        scratch_shapes=[pltpu.VMEM(((N - width) // LANE, tm, LANE), F32)],
        compiler_params=_params("parallel"),
        name="proj1",
    )(x, g.reshape(1, K), w, colscale.reshape(1, N))


def _ffn_kernel(*refs, n_mix, final_norm):
    x_ref, mix_refs = refs[0], refs[1:1 + 2 * n_mix]
    g_ref, wg_ref, wu_ref, wd_ref, fg_ref, o_ref, xn_s, acc_s, h_s = refs[1 + 2 * n_mix:]
    f = pl.program_id(1)

    @pl.when(f == 0)
    def _():
        h = x_ref[...]
        for a_ref, w_ref in zip(mix_refs[::2], mix_refs[1::2]):
            h = h + jnp.dot(a_ref[...], w_ref[...], preferred_element_type=F32)
        h_s[...] = h
        xn_s[...] = _rms(h, g_ref[...]).astype(BF16)
        acc_s[...] = jnp.zeros(acc_s.shape, F32)

    xn = xn_s[...]
    gate = jnp.dot(xn, wg_ref[0].astype(BF16), preferred_element_type=F32)
    up = jnp.dot(xn, wu_ref[0].astype(BF16), preferred_element_type=F32)
    mid = (gate / (1.0 + jnp.exp(-gate)) * up).astype(BF16)
    acc_s[...] += jnp.dot(mid, wd_ref[0].astype(BF16), preferred_element_type=F32)

    @pl.when(f == pl.num_programs(1) - 1)
    def _():
        y = h_s[...] + acc_s[...]
        if final_norm:
            y = _rms(y, fg_ref[...])
        o_ref[...] = y


def ffn(x, mix, g, wg, wu, wd, layer, fg, final_norm, name):
    M, K = x.shape
    tm, tf = FFN_ROW_TILE, FFN_COL_TILE
    kern = functools.partial(_ffn_kernel, n_mix=len(mix), final_norm=final_norm)
    mix_specs, mix_args = [], []
    for a, w in mix:
        mix_specs += [pl.BlockSpec((tm, a.shape[1]), lambda i, f: (i, 0)), pl.BlockSpec(w.shape, lambda i, f: (0, 0))]
        mix_args += [a, w]
    return pl.pallas_call(
        kern,
        grid=(M // tm, D_FF // tf),
        in_specs=[pl.BlockSpec((tm, K), lambda i, f: (i, 0)), *mix_specs,
                  pl.BlockSpec((1, K), lambda i, f: (0, 0)),
                  pl.BlockSpec((1, K, tf), lambda i, f: (layer, 0, f)),
                  pl.BlockSpec((1, K, tf), lambda i, f: (layer, 0, f)),
                  pl.BlockSpec((1, tf, K), lambda i, f: (layer, f, 0)),
                  pl.BlockSpec((1, K), lambda i, f: (0, 0))],
        out_specs=pl.BlockSpec((tm, K), lambda i, f: (i, 0)),
        out_shape=jax.ShapeDtypeStruct((M, K), F32),
        scratch_shapes=[pltpu.VMEM((tm, K), BF16), pltpu.VMEM((tm, K), F32), pltpu.VMEM((tm, K), F32)],
        compiler_params=_params("parallel", "arbitrary"),
        name=name,
    )(x, *mix_args, g.reshape(1, K), wg, wu, wd, fg.reshape(1, K))


def _trig_kernel(ang_ref, cos_ref, sin_ref):
    a = ang_ref[...]
    cos_ref[...] = jnp.cos(a)
    sin_ref[...] = jnp.sin(a)


def rope_tables(seq_len):
    half = D_ROPE // 2
    inv = ROPE_THETA ** (-jnp.arange(half, dtype=F32) / half)
    ang = jnp.arange(seq_len).astype(F32)[:, None] * inv[None, :]
    dense = ang.reshape(seq_len * half // LANE, LANE)
    spec = pl.BlockSpec(dense.shape, lambda: (0, 0))
    cos, sin = pl.pallas_call(
        _trig_kernel,
        in_specs=[spec],
        out_specs=[spec, spec],
        out_shape=[jax.ShapeDtypeStruct(dense.shape, F32)] * 2,
        name="rope_trig",
    )(dense)
    cos, sin = cos.reshape(seq_len, half), sin.reshape(seq_len, half)
    pad = LANE - D_NOPE - D_ROPE
    cos_l = jnp.concatenate([jnp.ones((seq_len, D_NOPE), F32), cos, cos, jnp.ones((seq_len, pad), F32)], axis=1)
    sin_l = jnp.concatenate([jnp.zeros((seq_len, D_NOPE), F32), sin, sin, jnp.zeros((seq_len, pad), F32)], axis=1)
    return cos_l, sin_l


def _prep_d_kernel(x_ref, g_ref, wa_ref, qn_ref, kvn_ref, wq_ref, wkv_ref, cos_ref, sin_ref,
                   q_ref, k_ref, v_ref, *, qscale):
    xn = _rms(x_ref[...], g_ref[...]).astype(BF16)
    lat = jnp.dot(xn, wa_ref[...], preferred_element_type=F32)
    cq = _rms(lat[:, :D_Q_LORA], qn_ref[...]).astype(BF16)
    ckv = _rms(lat[:, D_Q_LORA:D_Q_LORA + D_KV_LORA], kvn_ref[...]).astype(BF16)
    o_pe = D_Q_LORA + D_KV_LORA
    cos, sin = cos_ref[...], sin_ref[...]
    kpe = lat[:, o_pe:o_pe + LANE] * cos + lat[:, o_pe + LANE:o_pe + 2 * LANE] * sin
    qq = jnp.dot(cq, wq_ref[...], preferred_element_type=F32)
    kv = jnp.dot(ckv, wkv_ref[...], preferred_element_type=F32)
    kw = D_HEADS * LANE
    ones_blk = _ones_row_block(V_ROWS_D - D_V, x_ref.shape[0])
    for h in range(D_HEADS):
        qh = qq[:, h * LANE:(h + 1) * LANE] * cos + qq[:, kw + h * LANE:kw + (h + 1) * LANE] * sin
        q_ref[0, h, 0] = (qh * qscale).T.astype(BF16)
        k_ref[:, h * LANE:(h + 1) * LANE] = (kv[:, h * LANE:(h + 1) * LANE] + kpe).astype(BF16)
    for j in range(D_HEADS // 2):
        vt = kv[:, kw + j * LANE:kw + (j + 1) * LANE].T.astype(BF16)
        for half in range(2):
            v_ref[0, 2 * j + half, 0, :D_V] = vt[half * D_V:(half + 1) * D_V]
            v_ref[0, 2 * j + half, 0, D_V:] = ones_blk


def prep_d(x, g, wa, qn, kvn, wq, wkv, cos_l, sin_l, batch, seq_len):
    M, K = x.shape
    T = DENSE_TILE
    nrow = seq_len // T
    kw = D_HEADS * LANE
    kern = functools.partial(_prep_d_kernel, qscale=(D_NOPE + D_ROPE) ** -0.5 * LOG2E)
    full = lambda a: pl.BlockSpec(a.shape, lambda i: (0, 0))
    qn2, kvn2, g2 = qn.reshape(1, -1), kvn.reshape(1, -1), g.reshape(1, K)
    return pl.pallas_call(
        kern,
        grid=(M // T,),
        in_specs=[pl.BlockSpec((T, K), lambda i: (i, 0)), full(g2), full(wa), full(qn2), full(kvn2),
                  full(wq), full(wkv),
                  pl.BlockSpec((T, LANE), lambda i: (i % nrow, 0)),
                  pl.BlockSpec((T, LANE), lambda i: (i % nrow, 0))],
        out_specs=[pl.BlockSpec((1, D_HEADS, 1, LANE, T), lambda i: (i // nrow, 0, i % nrow, 0, 0)),
                   pl.BlockSpec((T, kw), lambda i: (i, 0)),
                   pl.BlockSpec((1, D_HEADS, 1, V_ROWS_D, T), lambda i: (i // nrow, 0, i % nrow, 0, 0))],
        out_shape=[jax.ShapeDtypeStruct((batch, D_HEADS, nrow, LANE, T), BF16),
                   jax.ShapeDtypeStruct((M, kw), BF16),
                   jax.ShapeDtypeStruct((batch, D_HEADS, nrow, V_ROWS_D, T), BF16)],
        compiler_params=_params("parallel"),
        name="prep_d",
    )(x, g2, wa, qn2, kvn2, wq, wkv, cos_l, sin_l)


B_HEAD_ORDER = (0, 4, 1, 5, 2, 6, 3, 7)


def _blocks(w, starts, width, axis):
    return jnp.concatenate([lax.slice_in_dim(w, s, s + width, axis=axis) for s in starts], axis=axis)


def _rot_partner_cols(w):
    half = D_ROPE // 2
    return jnp.concatenate([-w[..., half:], w[..., :half]], axis=-1)


def kernel(x, bias_table, attn_norm, ffn_norm, final_norm, ab_w_in, ab_lambda_q1, ab_lambda_k1,
           ab_lambda_q2, ab_lambda_k2, ab_subln, ab_sink, ab_w_o, cd_w_in, cd_q_norm, cd_w_q_b,
           cd_kv_norm, cd_w_kv_b, cd_w_o, ffn_w_gate, ffn_w_up, ffn_w_down):
    B, S, _ = x.shape
    M = B * S
    T = DENSE_TILE
    h = x.reshape(M, D_MODEL)
    qk_scale = HEAD_DIM ** -0.5 * LOG2E

    o3 = A_HEADS * (2 * A_QK_DIM + A_V_DIM)
    a0 = ab_w_in[0].astype(BF16)
    w0 = jnp.concatenate([a0[:, :o3], _blocks(a0, [o3 + hd * HEAD_DIM for hd in B_HEAD_ORDER], HEAD_DIM, 1),
                          a0[:, o3 + B_HEADS * HEAD_DIM:]], axis=1)
    cs0 = np.ones((AB_IN,), np.float32)
    cs0[:A_HEADS * A_QK_DIM] = qk_scale
    cs0[o3:o3 + B_HEADS * HEAD_DIM] = qk_scale
    qat, vat, qbt, vbt, keys0 = proj0_call(h, attn_norm[0], w0, jnp.asarray(cs0), B, S)
    keys0 = keys0.reshape(B, S, -1)
    bias_a = bias_tiles(bias_table, nvar=2 * BIAS_REACH + 1, nheads=A_HEADS, head0=0, rows=T, cols=T,
                        off0=-BIAS_REACH * T, off_step=T, row_coef=1, col_coef=-1, dil=1,
                        half_window=None, name="bias_a")
    oa = attn_a(qat, keys0, vat, bias_a, ab_lambda_q1[0], ab_lambda_k1[0], ab_lambda_q2[0],
                ab_lambda_k2[0], ab_subln[0], 0.8 - 0.6 * math.exp(-0.3 * 0))

    bias_b = bias_tiles(bias_table, nvar=3, nheads=B_HEADS, head0=A_HEADS, rows=QCOLS + 2 * B_HALF_WINDOW,
                        cols=QCOLS, off0=0, off_step=-B_HALF_WINDOW, row_coef=1, col_coef=-1, dil=1,
                        half_window=B_HALF_WINDOW, name="bias_b")
    ob = attn_b(qbt, keys0, (A_HEADS * A_QK_DIM) // LANE, vbt, bias_b, ab_sink[0])

    wo = ab_w_o[0].astype(BF16)
    wo_a = wo[:A_HEADS * A_V_DIM]
    wo_b = _blocks(wo, [A_HEADS * A_V_DIM + hd * HEAD_DIM for hd in B_HEAD_ORDER], HEAD_DIM, 0)
    h = ffn(h, [(oa.reshape(M, -1), wo_a), (ob.reshape(M, -1), wo_b)], ffn_norm[0], ffn_w_gate, ffn_w_up,
            ffn_w_down, 0, final_norm, False, "ffn0")

    w1 = cd_w_in[0]
    gw = C_HEADS_PER_GROUP * HEAD_DIM
    cw = C_HEADS * HEAD_DIM
    starts1 = [role * cw + g * gw for g in range(len(C_PATTERNS)) for role in range(3)]
    cs1 = np.ones((CD_C_IN,), np.float32)
    for g in range(len(C_PATTERNS)):
        cs1[3 * g * gw:(3 * g + 1) * gw] = qk_scale
    c_views = proj1_call(h, attn_norm[1], _blocks(w1.astype(BF16), starts1, gw, 1), jnp.asarray(cs1), B, S)

    oc, lses = [], []
    for g, (window, dil) in enumerate(C_PATTERNS):
        hw = window // (2 * dil)
        tq_c = 128
        bias_c = bias_tiles(bias_table, nvar=3, nheads=C_HEADS_PER_GROUP, head0=g * C_HEADS_PER_GROUP,
                            rows=tq_c, cols=tq_c + 2 * hw, off0=0, off_step=-hw, row_coef=-1, col_coef=1,
                            dil=dil, half_window=hw, name=f"bias_c{g}")
        o_g, lse_g = banded(c_views[g], bias_c, None, dil=dil, tq=tq_c, half_window=hw, nqb=2, nkb=2,
                            q_idx=lambda r: 3 * r, k_idx=lambda r: 3 * r + 1, v_idx=lambda r: 3 * r + 2,
                            head_of=lambda c, half: 2 * c + half, out_dtype=F32, has_lse=True,
                            name=f"attn_c{g}")
        oc.append(o_g)
        lses.append(lse_g)

    o_q, o_kv = CD_C_IN + D_Q_LORA, CD_C_IN + D_Q_LORA + D_KV_LORA
    w_pe = w1[:, o_kv:]
    lane_pad = lambda w: jnp.pad(w, ((0, 0), (D_NOPE, LANE - D_NOPE - D_ROPE)))
    wa = jnp.concatenate([w1[:, CD_C_IN:o_kv], lane_pad(w_pe), lane_pad(_rot_partner_cols(w_pe))],
                         axis=1).astype(BF16)
    wq3 = cd_w_q_b[0].reshape(D_Q_LORA, D_HEADS, D_NOPE + D_ROPE)
    zpad = jnp.zeros((D_Q_LORA, D_HEADS, LANE - D_NOPE - D_ROPE), F32)
    wq_main = jnp.concatenate([wq3, zpad], axis=-1)
    wq_rot = jnp.concatenate([jnp.zeros_like(wq3[..., :D_NOPE]), _rot_partner_cols(wq3[..., D_NOPE:]), zpad],
                             axis=-1)
    wq = jnp.concatenate([wq_main.reshape(D_Q_LORA, -1), wq_rot.reshape(D_Q_LORA, -1)], axis=1).astype(BF16)
    wkv3 = cd_w_kv_b[0].reshape(D_KV_LORA, D_HEADS, D_NOPE + D_V)
    wk = jnp.pad(wkv3[..., :D_NOPE], ((0, 0), (0, 0), (0, LANE - D_NOPE))).reshape(D_KV_LORA, -1)
    wv = wkv3[..., D_NOPE:].reshape(D_KV_LORA, -1)
    wkv = jnp.concatenate([wk, wv], axis=1).astype(BF16)
    cos_l, sin_l = rope_tables(S)
    qdt, kd, vdt = prep_d(h, attn_norm[1], wa, cd_q_norm[0], cd_kv_norm[0], wq, wkv, cos_l, sin_l, B, S)
    od = attn_d(qdt, kd.reshape(B, S, D_HEADS * LANE), vdt)

    wo1 = cd_w_o[0]
    wo_c = wo1[:C_HEADS_PER_GROUP * HEAD_DIM].astype(BF16)
    wo_d = wo1[C_HEADS_PER_GROUP * HEAD_DIM:].astype(BF16)
    h = out_proj_cd(h, oc, lses, od.reshape(M, -1), wo_c, wo_d, B, S)
    h = ffn(h, [], ffn_norm[1], ffn_w_gate, ffn_w_up, ffn_w_down, 1, final_norm, True, "ffn1")
    return h.reshape(B, S, D_MODEL)
```

```python
import functools
import math

import numpy as np
import jax
import jax.numpy as jnp
from jax import lax
from jax.experimental import pallas as pl
from jax.experimental.pallas import tpu as pltpu

F32 = jnp.float32
BF16 = jnp.bfloat16

D_MODEL = 1024
HEAD_DIM = 64
EPS = 1e-6
NEG = -1e30
LOG2E = math.log2(math.e)
LN2 = math.log(2.0)

A_HEADS = 4
A_QK_DIM = 2 * HEAD_DIM
A_V_DIM = 2 * HEAD_DIM
B_HEADS = 8
B_KV_HEADS = 2
B_HALF_WINDOW = 128
C_PATTERNS = ((128, 1), (512, 4), (2048, 16))
C_HEADS_PER_GROUP = 4
C_HEADS = C_HEADS_PER_GROUP * len(C_PATTERNS)
D_HEADS = 12
D_Q_LORA = 384
D_KV_LORA = 256
D_NOPE = 64
D_ROPE = 32
D_V = 64
ROPE_THETA = 10000.0
NUM_BUCKETS = 32
MAX_DISTANCE = 1024
D_FF = 2816
AB_IN = 2304
CD_C_IN = 3 * C_HEADS * HEAD_DIM

LANE = 128
VMEM_LIMIT = 48 * 1024 * 1024

ROW_TILE = 512
FFN_ROW_TILE = 1024
FFN_COL_TILE = 256
DENSE_TILE = 512
QCOLS = 256
UNROLL = 4
A_QTILES = 4
D_QTILES = 4
BAND_GROUP = 4
BIAS_REACH = 3
V_ROWS_D = 80
V_ROWS_A = 144


def _bucket_thresholds():
    nb = NUM_BUCKETS // 2
    max_exact = nb // 2
    n = np.arange(1, 4 * MAX_DISTANCE)
    large = max_exact + (np.log(n.astype(np.float32) / np.float32(max_exact))
                         / np.float32(math.log(MAX_DISTANCE / max_exact))
                         * np.float32(nb - max_exact)).astype(np.int32)
    mag = np.where(n < max_exact, n, np.minimum(large, nb - 1))
    return tuple(int(n[np.argmax(mag >= k)]) for k in range(1, nb))


BUCKET_THRESHOLDS = _bucket_thresholds()
assert BUCKET_THRESHOLDS[-1] <= (BIAS_REACH - 1) * DENSE_TILE + 1


def _params(*sem):
    return pltpu.CompilerParams(dimension_semantics=sem, vmem_limit_bytes=VMEM_LIMIT)


def _rms(x, g):
    return x * lax.rsqrt(jnp.mean(x * x, axis=-1, keepdims=True) + EPS) * g


def _ones_row_block(rows, cols):
    r = lax.broadcasted_iota(jnp.int32, (rows, cols), 0)
    return jnp.where(r == 0, 1.0, 0.0).astype(BF16)


def _store_masked_halves(q_ref, idx, qt):
    zeros = jnp.zeros((HEAD_DIM, qt.shape[1]), BF16)
    q_ref[idx + (0, 0, slice(None, HEAD_DIM))] = qt[:HEAD_DIM]
    q_ref[idx + (0, 0, slice(HEAD_DIM, None))] = zeros
    q_ref[idx + (1, 0, slice(None, HEAD_DIM))] = zeros
    q_ref[idx + (1, 0, slice(HEAD_DIM, None))] = qt[HEAD_DIM:]


def _proj0_kernel(x_ref, g_ref, w_ref, cs_ref, qa_ref, va_ref, qb_ref, vb_ref, k_ref):
    xn = _rms(x_ref[...], g_ref[...]).astype(BF16)
    y = jnp.dot(xn, w_ref[...], preferred_element_type=F32) * cs_ref[...]
    tm = y.shape[0]
    ka0 = A_HEADS * A_QK_DIM
    va0 = 2 * ka0
    qb0 = va0 + A_HEADS * A_V_DIM
    kb0 = qb0 + B_HEADS * HEAD_DIM
    vb0 = kb0 + B_KV_HEADS * HEAD_DIM
    for h in range(A_HEADS):
        _store_masked_halves(qa_ref, (0, h), y[:, h * A_QK_DIM:(h + 1) * A_QK_DIM].T.astype(BF16))
        va_ref[0, h, 0, :A_V_DIM] = y[:, va0 + h * A_V_DIM:va0 + (h + 1) * A_V_DIM].T.astype(BF16)
        va_ref[0, h, 0, A_V_DIM:] = _ones_row_block(V_ROWS_A - A_V_DIM, tm)
    for j in range(B_HEADS // 2):
        _store_masked_halves(qb_ref, (0, j), y[:, qb0 + j * LANE:qb0 + (j + 1) * LANE].T.astype(BF16))
    vbt = y[:, vb0:vb0 + LANE].T.astype(BF16)
    ones_blk = _ones_row_block(V_ROWS_D - HEAD_DIM, LANE)
    for g in range(B_KV_HEADS):
        for c in range(tm // LANE):
            vb_ref[0, g, c, :HEAD_DIM] = vbt[g * HEAD_DIM:(g + 1) * HEAD_DIM, c * LANE:(c + 1) * LANE]
            vb_ref[0, g, c, HEAD_DIM:] = ones_blk
    k_ref[:, :ka0] = y[:, ka0:va0].astype(BF16)
    k_ref[:, ka0:] = y[:, kb0:vb0].astype(BF16)


def proj0_call(x, g, w, colscale, batch, seq_len):
    M, K = x.shape
    N = w.shape[1]
    T = DENSE_TILE
    nrow = seq_len // T
    nkb = A_HEADS * A_QK_DIM + B_KV_HEADS * HEAD_DIM
    qspec = pl.BlockSpec((1, A_HEADS, 2, 1, LANE, T), lambda i: (i // nrow, 0, 0, i % nrow, 0, 0))
    qshape = jax.ShapeDtypeStruct((batch, A_HEADS, 2, nrow, LANE, T), BF16)
    return pl.pallas_call(
        _proj0_kernel,
        grid=(M // T,),
        in_specs=[pl.BlockSpec((T, K), lambda i: (i, 0)),
                  pl.BlockSpec((1, K), lambda i: (0, 0)),
                  pl.BlockSpec((K, N), lambda i: (0, 0)),
                  pl.BlockSpec((1, N), lambda i: (0, 0))],
        out_specs=[qspec,
                   pl.BlockSpec((1, A_HEADS, 1, V_ROWS_A, T), lambda i: (i // nrow, 0, i % nrow, 0, 0)),
                   qspec,
                   pl.BlockSpec((1, B_KV_HEADS, T // LANE, V_ROWS_D, LANE),
                                lambda i: (i // nrow, 0, i % nrow, 0, 0)),
                   pl.BlockSpec((T, nkb), lambda i: (i, 0))],
        out_shape=[qshape,
                   jax.ShapeDtypeStruct((batch, A_HEADS, nrow, V_ROWS_A, T), BF16),
                   qshape,
                   jax.ShapeDtypeStruct((batch, B_KV_HEADS, seq_len // LANE, V_ROWS_D, LANE), BF16),
                   jax.ShapeDtypeStruct((M, nkb), BF16)],
        compiler_params=_params("parallel"),
        name="proj0",
    )(x, g.reshape(1, K), w, colscale.reshape(1, N))


def _bias_kernel(tab_ref, o_ref, *, nvar, off0, off_step, row_coef, col_coef, dil, half_window, head0):
    hcol = head0 + pl.program_id(1)
    R, C = o_ref.shape[-2:]
    row = lax.broadcasted_iota(jnp.int32, (R, C), 0)
    col = lax.broadcasted_iota(jnp.int32, (R, C), 1)
    base = row_coef * row + col_coef * col
    span_lo = min(row_coef * (R - 1), 0) + min(col_coef * (C - 1), 0)
    span_hi = max(row_coef * (R - 1), 0) + max(col_coef * (C - 1), 0)
    nb = NUM_BUCKETS // 2

    def side(n, n_lo, n_hi, row0):
        val = jnp.full((R, C), tab_ref[row0 + sum(t <= n_lo for t in BUCKET_THRESHOLDS), hcol], F32)
        for k, thr in enumerate(BUCKET_THRESHOLDS, start=1):
            if n_lo < thr <= n_hi:
                val = jnp.where(n >= thr, tab_ref[row0 + k, hcol], val)
        return val

    for v in range(nvar):
        @pl.when(pl.program_id(0) == v)
        def _(v=v):
            off = off0 + v * off_step
            rel = off + base
            lo, hi = (off + span_lo) * dil, (off + span_hi) * dil
            dist = rel * dil
            n = jnp.abs(dist)
            if hi <= 0:
                val = side(n, -hi, -lo, 0)
            elif lo > 0:
                val = side(n, lo, hi, nb)
            else:
                val = jnp.where(dist > 0, side(n, 1, hi, nb), side(n, 0, -lo, 0))
            val = val * LOG2E
            if half_window is not None:
                val = jnp.where(jnp.abs(rel) <= half_window, val, NEG)
            o_ref[0, 0] = val


def bias_tiles(table, *, nvar, nheads, head0, rows, cols, off0, off_step, row_coef, col_coef,
               dil, half_window, name):
    kern = functools.partial(_bias_kernel, nvar=nvar, off0=off0, off_step=off_step, row_coef=row_coef,
                             col_coef=col_coef, dil=dil, half_window=half_window, head0=head0)
    return pl.pallas_call(
        kern,
        grid=(nvar, nheads),
        in_specs=[pl.BlockSpec(memory_space=pltpu.SMEM)],
        out_specs=pl.BlockSpec((1, 1, rows, cols), lambda v, h: (v, h, 0, 0)),
        out_shape=jax.ShapeDtypeStruct((nvar, nheads, rows, cols), F32),
        compiler_params=_params("parallel", "parallel"),
        name=name,
    )(table)


def _dense_pipeline(nk, tile, n_streams, score_fn, value_fn, m_s, acc_s, bufs):
    units = [(i, c * QCOLS) for i in range(n_streams) for c in range(tile // QCOLS)]

    def produce(kc, unit, nxt):
        i, c0 = unit
        cols = slice(c0, c0 + QCOLS)
        s = score_fn(kc + 1, i, cols)
        nxt[0][i, :, cols] = s
        nxt[1][i, :, cols] = jnp.max(s, axis=0, keepdims=True)

    def consume(kc, unit, cur):
        i, c0 = unit
        cols = slice(c0, c0 + QCOLS)
        m_old = m_s[i, :, cols]
        m_new = jnp.maximum(m_old, cur[1][i, :, cols])
        alpha = jnp.exp2(m_old - m_new)
        p = jnp.exp2((cur[0][i, :, cols] - m_new).astype(BF16))
        acc_s[i, :, cols] = (alpha * acc_s[i, :, cols]
                             + jnp.dot(value_fn(kc, i), p, preferred_element_type=F32))
        m_s[i, :, cols] = m_new

    def stage(kc, cur, nxt):
        for unit in units:
            if nxt is not None:
                produce(kc, unit, nxt)
            if cur is not None:
                consume(kc, unit, cur)

    m_s[...] = jnp.full(m_s.shape, NEG, F32)
    acc_s[...] = jnp.zeros(acc_s.shape, F32)
    stage(-1, None, bufs[0])
    n_loop = (nk - 1) // UNROLL

    def body(j, carry):
        for u in range(UNROLL):
            stage(UNROLL * j + u, bufs[u % 2], bufs[(u + 1) % 2])
        return carry

    lax.fori_loop(0, n_loop, body, 0)
    for kc in range(n_loop * UNROLL, nk):
        stage(kc, bufs[kc % 2], bufs[(kc + 1) % 2] if kc < nk - 1 else None)


def _attn_a_kernel(q_ref, k_ref, v_ref, bias_ref, lq1_ref, lk1_ref, lq2_ref, lk2_ref, subln_ref, o_ref,
                   m_s, acc_s, s_a, s_b, cm_a, cm_b, *, tile, nk, qtiles, lambda_init):
    qi0 = pl.program_id(2) * qtiles

    def score_fn(kc, i, cols):
        j, qt = divmod(i, qtiles)
        kblk = k_ref[0, pl.ds(pl.multiple_of(kc * tile, tile), tile), :]
        bt = bias_ref[jnp.clip(kc - (qi0 + qt), -BIAS_REACH, BIAS_REACH) + BIAS_REACH, 0, :, cols]
        return jnp.dot(kblk, q_ref[0, 0, j, qt, :, cols], preferred_element_type=F32) + bt

    def value_fn(kc, i):
        return v_ref[0, 0, kc]

    _dense_pipeline(nk, tile, 2 * qtiles, score_fn, value_fn, m_s, acc_s, ((s_a, cm_a), (s_b, cm_b)))

    lam = (jnp.exp(jnp.sum(lq1_ref[...] * lk1_ref[...], axis=-1, keepdims=True))
           - jnp.exp(jnp.sum(lq2_ref[...] * lk2_ref[...], axis=-1, keepdims=True)) + lambda_init)
    for qt in range(qtiles):
        a1, a2 = acc_s[qt], acc_s[qtiles + qt]
        o = (a1[:A_V_DIM] / a1[A_V_DIM:A_V_DIM + 1]
             - lam * (a2[:A_V_DIM] / a2[A_V_DIM:A_V_DIM + 1]))
        ms = jnp.mean(o * o, axis=0, keepdims=True)
        y = o * lax.rsqrt(ms + EPS) * subln_ref[...] * (1.0 - lambda_init)
        o_ref[0, qt * tile:(qt + 1) * tile] = y.T.astype(o_ref.dtype)


def attn_a(qt, karr, vt, bias, lq1, lk1, lq2, lk2, subln, lambda_init):
    B, S, _ = karr.shape
    T = DENSE_TILE
    nq = nk = S // T
    assert nk % 2 == 0 and nk >= 4
    qtiles = math.gcd(nq, A_QTILES)
    ns = 2 * qtiles
    kern = functools.partial(_attn_a_kernel, tile=T, nk=nk, qtiles=qtiles, lambda_init=lambda_init)
    vec = lambda n: pl.BlockSpec((1, n), lambda h, b, i: (0, 0))
    return pl.pallas_call(
        kern,
        grid=(A_HEADS, B, nq // qtiles),
        in_specs=[pl.BlockSpec((1, 1, 2, qtiles, LANE, T), lambda h, b, i: (b, h, 0, i, 0, 0)),
                  pl.BlockSpec((1, S, LANE), lambda h, b, i: (b, 0, h)),
                  pl.BlockSpec((1, 1, nk, V_ROWS_A, T), lambda h, b, i: (b, h, 0, 0, 0)),
                  pl.BlockSpec((2 * BIAS_REACH + 1, 1, T, T), lambda h, b, i: (0, h, 0, 0),
                               pipeline_mode=pl.Buffered(1)),
                  vec(HEAD_DIM), vec(HEAD_DIM), vec(HEAD_DIM), vec(HEAD_DIM),
                  pl.BlockSpec((A_V_DIM, 1), lambda h, b, i: (0, 0))],
        out_specs=pl.BlockSpec((1, qtiles * T, LANE), lambda h, b, i: (b, i, h)),
        out_shape=jax.ShapeDtypeStruct((B, S, A_HEADS * A_V_DIM), BF16),
        scratch_shapes=[pltpu.VMEM((ns, 1, T), F32), pltpu.VMEM((ns, V_ROWS_A, T), F32),
                        pltpu.VMEM((ns, T, T), F32), pltpu.VMEM((ns, T, T), F32),
                        pltpu.VMEM((ns, 1, T), F32), pltpu.VMEM((ns, 1, T), F32)],
        compiler_params=_params("parallel", "parallel", "arbitrary"),
        name="attn_a",
    )(qt, karr, vt, bias, lq1.reshape(1, -1), lk1.reshape(1, -1), lq2.reshape(1, -1),
      lk2.reshape(1, -1), subln.reshape(-1, 1))


def _attn_d_kernel(q_ref, k_ref, v_ref, o_ref, m_s, acc_s, s_a, s_b, cm_a, cm_b, *, tile, nk, qtiles):
    def score_fn(kc, i, cols):
        hh, qt = divmod(i, qtiles)
        kblk = k_ref[0, pl.ds(pl.multiple_of(kc * tile, tile), tile), hh * LANE:(hh + 1) * LANE]
        return jnp.dot(kblk, q_ref[0, hh, qt, :, cols], preferred_element_type=F32)

    def value_fn(kc, i):
        return v_ref[0, i // qtiles, kc]

    _dense_pipeline(nk, tile, 2 * qtiles, score_fn, value_fn, m_s, acc_s, ((s_a, cm_a), (s_b, cm_b)))
    for qt in range(qtiles):
        outs = []
        for hh in range(2):
            acc = acc_s[hh * qtiles + qt]
            outs.append(acc[:D_V] / acc[D_V:D_V + 1])
        o_ref[0, qt * tile:(qt + 1) * tile] = jnp.concatenate(outs, axis=0).T.astype(o_ref.dtype)


def attn_d(qt, k, vt):
    B, S, _ = k.shape
    T = DENSE_TILE
    nq = nk = S // T
    qtiles = math.gcd(nq, D_QTILES)
    ns = 2 * qtiles
    kern = functools.partial(_attn_d_kernel, tile=T, nk=nk, qtiles=qtiles)
    return pl.pallas_call(
        kern,
        grid=(B, D_HEADS // 2, nq // qtiles),
        in_specs=[pl.BlockSpec((1, 2, qtiles, LANE, T), lambda b, h, i: (b, h, i, 0, 0)),
                  pl.BlockSpec((1, S, 2 * LANE), lambda b, h, i: (b, 0, h)),
                  pl.BlockSpec((1, 2, nk, V_ROWS_D, T), lambda b, h, i: (b, h, 0, 0, 0))],
        out_specs=pl.BlockSpec((1, qtiles * T, LANE), lambda b, h, i: (b, i, h)),
        out_shape=jax.ShapeDtypeStruct((B, S, D_HEADS * D_V), BF16),
        scratch_shapes=[pltpu.VMEM((ns, 1, T), F32), pltpu.VMEM((ns, V_ROWS_D, T), F32),
                        pltpu.VMEM((ns, T, T), F32), pltpu.VMEM((ns, T, T), F32),
                        pltpu.VMEM((ns, 1, T), F32), pltpu.VMEM((ns, 1, T), F32)],
        compiler_params=_params("parallel", "parallel", "arbitrary"),
        name="attn_d",
    )(qt, k, vt)


def _attn_b_kernel(q_ref, k_ref, v_ref, bias_ref, sink_ref, o_ref, s_a, s_b, s_c, cm_a, cm_b, cm_c, *, tile, seq_len):
    half_window = B_HALF_WINDOW
    span = QCOLS + 2 * half_window
    nchunk = span // LANE
    step = pl.program_id(1)

    def window(c):
        q0 = step * tile + c * QCOLS
        start = pl.multiple_of(jnp.clip(q0 - half_window, 0, seq_len - span), LANE)
        variant = jnp.where(q0 == 0, 0, jnp.where(q0 + QCOLS == seq_len, 2, 1))
        return start, variant

    def scores(task):
        c, j, g = task
        start, variant = window(c)
        kwin = k_ref[0, pl.ds(start, span), :]
        s = jnp.dot(kwin, q_ref[0, j, g, 0, :, c * QCOLS:(c + 1) * QCOLS], preferred_element_type=F32)
        return s + bias_ref[variant, j + (B_HEADS // 2) * g]

    def finish(task, s_ref, cm_ref):
        c, j, g = task
        head = j + (B_HEADS // 2) * g
        start, _ = window(c)
        chunk0 = start // LANE
        vwin = jnp.concatenate([v_ref[0, g, chunk0 + n] for n in range(nchunk)], axis=1)
        sk = sink_ref[:, head:head + 1] * LOG2E
        m = jnp.maximum(cm_ref[...], sk)
        e = jnp.exp2((s_ref[...] - m).astype(BF16))
        ov = jnp.dot(vwin, e, preferred_element_type=F32)
        denom = ov[HEAD_DIM:HEAD_DIM + 1] + jnp.exp2(sk - m)
        return ov[:HEAD_DIM] / denom

    tasks = [(c, j, g) for c in range(tile // QCOLS) for j in range(B_HEADS // 2) for g in range(B_KV_HEADS)]
    bufs = ((s_a, cm_a), (s_b, cm_b), (s_c, cm_c))

    def produce(task, buf):
        s = scores(task)
        buf[0][...] = s
        buf[1][...] = jnp.max(s, axis=0, keepdims=True)

    produce(tasks[0], bufs[0])
    produce(tasks[1], bufs[1])
    held = None
    for n, task in enumerate(tasks):
        if n + 2 < len(tasks):
            produce(tasks[n + 2], bufs[(n + 2) % 3])
        o = finish(task, *bufs[n % 3])
        c, j, g = task
        if g == 0:
            held = o
            continue
        pair = jnp.concatenate([held, o], axis=0).T
        o_ref[0, c * QCOLS:(c + 1) * QCOLS, j * LANE:(j + 1) * LANE] = pair.astype(o_ref.dtype)


def attn_b(qt, karr, k_block, vt, bias, sink):
    B, S, _ = karr.shape
    T = DENSE_TILE
    span = QCOLS + 2 * B_HALF_WINDOW
    return pl.pallas_call(
        functools.partial(_attn_b_kernel, tile=T, seq_len=S),
        grid=(B, S // T),
        in_specs=[pl.BlockSpec((1, B_HEADS // 2, 2, 1, LANE, T), lambda b, i: (b, 0, 0, i, 0, 0)),
                  pl.BlockSpec((1, S, LANE), lambda b, i: (b, 0, k_block)),
                  pl.BlockSpec((1, B_KV_HEADS, S // LANE, V_ROWS_D, LANE), lambda b, i: (b, 0, 0, 0, 0)),
                  pl.BlockSpec(bias.shape, lambda b, i: (0, 0, 0, 0)),
                  pl.BlockSpec((1, B_HEADS), lambda b, i: (0, 0))],
        out_specs=pl.BlockSpec((1, T, B_HEADS * HEAD_DIM), lambda b, i: (b, i, 0)),
        out_shape=jax.ShapeDtypeStruct((B, S, B_HEADS * HEAD_DIM), BF16),
        scratch_shapes=[pltpu.VMEM((span, QCOLS), F32)] * 3 + [pltpu.VMEM((1, QCOLS), F32)] * 3,
        compiler_params=_params("parallel", "arbitrary"),
        name="attn_b",
    )(qt, karr, vt, bias, sink.reshape(1, -1))


def _banded_kernel(*refs, tq, group, span, half_window, seq_len, nqb, nkb, head_of, has_sink, has_lse):
    it = iter(refs)
    q_ref, k_ref, v_ref, bias_ref = next(it), next(it), next(it), next(it)
    sink_ref = next(it) if has_sink else None
    o_ref = next(it)
    lse_ref = next(it) if has_lse else None

    nq = seq_len // tq
    lane = lax.broadcasted_iota(jnp.int32, (tq, LANE), 1)
    low = lane < HEAD_DIM

    def window(g):
        t = pl.program_id(2) * group + g
        start = pl.multiple_of(jnp.clip(t * tq - half_window, 0, seq_len - span), half_window)
        variant = jnp.where(t == 0, 0, jnp.where(t == nq - 1, 2, 1))
        return start, variant

    def scores(task):
        g, c, half = task
        start, variant = window(g)
        kc = c if nkb == nqb else 0
        q2 = q_ref[0, g * tq:(g + 1) * tq, c * LANE:(c + 1) * LANE]
        kb = k_ref[0, pl.ds(start, span), kc * LANE:(kc + 1) * LANE]
        qm = jnp.where(low if half == 0 else jnp.logical_not(low), q2, jnp.zeros_like(q2))
        s = lax.dot_general(qm, kb, (((1,), (1,)), ((), ())), preferred_element_type=F32)
        return s + bias_ref[variant, head_of(c, half)]

    def finish(task, s):
        g, c, half = task
        start, _ = window(g)
        kc = c if nkb == nqb else 0
        hidx = head_of(c, half)
        vb = v_ref[0, pl.ds(start, span), kc * LANE:(kc + 1) * LANE]
        m = jnp.max(s, axis=-1, keepdims=True)
        if has_sink:
            sk = sink_ref[:, hidx:hidx + 1] * LOG2E
            m = jnp.maximum(m, sk)
        e = jnp.exp2(s - m)
        denom = jnp.sum(e, axis=-1, keepdims=True)
        if has_sink:
            denom = denom + jnp.exp2(sk - m)
        o = jnp.dot(e.astype(BF16), vb, preferred_element_type=F32) / denom
        return o, (LN2 * m + jnp.log(denom) if has_lse else None)

    tasks = [(g, c, half) for g in range(group) for c in range(nqb) for half in range(2)]
    s_next = scores(tasks[0])
    held = None
    for n, task in enumerate(tasks):
        s = s_next
        if n + 1 < len(tasks):
            s_next = scores(tasks[n + 1])
        o, lse = finish(task, s)
        g, c, half = task
        if half == 0:
            held = (o, lse)
            continue
        rows, cols = slice(g * tq, (g + 1) * tq), slice(c * LANE, (c + 1) * LANE)
        o_ref[0, rows, cols] = jnp.where(low, held[0], o).astype(o_ref.dtype)
        if has_lse:
            lse_ref[0, rows, cols] = jnp.where(low, held[1], lse)


def banded(view, bias, sink, *, dil, tq, half_window, nqb, nkb, q_idx, k_idx, v_idx, head_of,
           out_dtype, has_lse, name):
    B, L, _ = view.shape
    span = tq + 2 * half_window
    nq = L // tq
    assert L % tq == 0 and L >= span and nq >= 2
    group = math.gcd(nq, BAND_GROUP)
    OW = nqb * LANE
    has_sink = sink is not None
    kern = functools.partial(_banded_kernel, tq=tq, group=group, span=span, half_window=half_window,
                             seq_len=L, nqb=nqb, nkb=nkb, head_of=head_of, has_sink=has_sink,
                             has_lse=has_lse)
    in_specs = [pl.BlockSpec((1, group * tq, OW), lambda b, r, t: (b, t, q_idx(r))),
                pl.BlockSpec((1, L, nkb * LANE), lambda b, r, t: (b, 0, k_idx(r))),
                pl.BlockSpec((1, L, nkb * LANE), lambda b, r, t: (b, 0, v_idx(r))),
                pl.BlockSpec(bias.shape, lambda b, r, t: (0, 0, 0, 0))]
    args = [view, view, view, bias]
    if has_sink:
        in_specs.append(pl.BlockSpec((1, sink.shape[-1]), lambda b, r, t: (0, 0)))
        args.append(sink.reshape(1, -1))
    out_spec = pl.BlockSpec((1, group * tq, OW), lambda b, r, t: (b, t, r))
    out_shapes = [jax.ShapeDtypeStruct((B, L, dil * OW), out_dtype)]
    out_specs = [out_spec]
    if has_lse:
        out_shapes.append(jax.ShapeDtypeStruct((B, L, dil * OW), F32))
        out_specs.append(out_spec)
    outs = pl.pallas_call(
        kern,
        grid=(B, dil, nq // group),
        in_specs=in_specs,
        out_specs=out_specs,
        out_shape=out_shapes,
        compiler_params=_params("parallel", "parallel", "arbitrary"),
        name=name,
    )(*args)
    return outs


def _out_proj_cd_kernel(h_ref, o0, o1, o2, s0, s1, s2, d_ref, wc_ref, wd_ref, o_ref, *scratch):
    tm = h_ref.shape[0]
    width = C_HEADS_PER_GROUP * HEAD_DIM
    spare = iter(scratch)

    def in_position_order(ref, dil):
        if dil == 1:
            return ref[0]
        t_s = next(spare)
        for r in range(dil):
            for j in range(width // LANE):
                t_s[j, pl.ds(r, tm // dil, stride=dil), :] = ref[0, :, r * width + j * LANE:r * width + (j + 1) * LANE]
        return jnp.concatenate([t_s[j] for j in range(width // LANE)], axis=1)

    dils = [d for _, d in C_PATTERNS]
    outs = [in_position_order(r, d) for r, d in zip((o0, o1, o2), dils)]
    lses = [in_position_order(r, d) for r, d in zip((s0, s1, s2), dils)]
    mx = jnp.maximum(jnp.maximum(lses[0], lses[1]), lses[2])
    es = [jnp.exp(l - mx) for l in lses]
    oc = (es[0] * outs[0] + es[1] * outs[1] + es[2] * outs[2]) / (es[0] + es[1] + es[2])
    mix = jnp.dot(oc.astype(BF16), wc_ref[...], preferred_element_type=F32)
    mix = mix + jnp.dot(d_ref[...], wd_ref[...], preferred_element_type=F32)
    o_ref[...] = h_ref[...] + mix


def out_proj_cd(h, oc, lses, od, wc, wd, batch, seq_len):
    M = h.shape[0]
    tm = ROW_TILE
    nrow = seq_len // tm
    width = C_HEADS_PER_GROUP * HEAD_DIM
    row = lambda a: pl.BlockSpec((tm, a.shape[1]), lambda i: (i, 0))
    full = lambda a: pl.BlockSpec(a.shape, lambda i: (0, 0))
    views = [pl.BlockSpec((1, tm // d, d * width), lambda i: (i // nrow, i % nrow, 0)) for _, d in C_PATTERNS]
    n_spare = 2 * sum(d > 1 for _, d in C_PATTERNS)
    return pl.pallas_call(
        _out_proj_cd_kernel,
        grid=(M // tm,),
        in_specs=[row(h), *views, *views, row(od), full(wc), full(wd)],
        out_specs=pl.BlockSpec((tm, D_MODEL), lambda i: (i, 0)),
        out_shape=jax.ShapeDtypeStruct((M, D_MODEL), F32),
        scratch_shapes=[pltpu.VMEM((width // LANE, tm, LANE), F32)] * n_spare,
        compiler_params=_params("parallel"),
        name="out_proj1",
    )(h, *oc, *lses, od, wc, wd)


def _proj1_kernel(x_ref, g_ref, w_ref, cs_ref, c0_ref, c1_ref, c2_ref, y_s):
    xn = _rms(x_ref[...], g_ref[...]).astype(BF16)
    y = jnp.dot(xn, w_ref[...], preferred_element_type=F32) * cs_ref[...]
    tm = y.shape[0]
    width = y.shape[1] // len(C_PATTERNS)
    c0_ref[0] = y[:, :width].astype(BF16)
    nblk = width // LANE
    for j in range(y_s.shape[0]):
        y_s[j] = y[:, width + j * LANE:width + (j + 1) * LANE]
    for g, ref in ((1, c1_ref), (2, c2_ref)):
        dil = C_PATTERNS[g][1]
        for r in range(dil):
            for j in range(nblk):
                rows = y_s[(g - 1) * nblk + j, pl.ds(r, tm // dil, stride=dil), :]
                ref[0, :, r * width + j * LANE:r * width + (j + 1) * LANE] = rows.astype(BF16)


def proj1_call(x, g, w, colscale, batch, seq_len):
    M, K = x.shape
    N = w.shape[1]
    tm = ROW_TILE
    nrow = seq_len // tm
    width = N // len(C_PATTERNS)
    assert [d for _, d in C_PATTERNS][0] == 1
    out_specs = [pl.BlockSpec((1, tm // d, d * width), lambda i: (i // nrow, i % nrow, 0)) for _, d in C_PATTERNS]
    out_shape = [jax.ShapeDtypeStruct((batch, seq_len // d, d * width), BF16) for _, d in C_PATTERNS]
    return pl.pallas_call(
        _proj1_kernel,
        grid=(M // tm,),
        in_specs=[pl.BlockSpec((tm, K), lambda i: (i, 0)),
                  pl.BlockSpec((1, K), lambda i: (0, 0)),
                  pl.BlockSpec((K, N), lambda i: (0, 0)),
                  pl.BlockSpec((1, N), lambda i: (0, 0))],
        out_specs=out_specs,
        out_shape=out_shape,
        scratch_shapes=[pltpu.VMEM(((N - width) // LANE, tm, LANE), F32)],
        compiler_params=_params("parallel"),
        name="proj1",
    )(x, g.reshape(1, K), w, colscale.reshape(1, N))


def _ffn_kernel(*refs, n_mix, final_norm):
    x_ref, mix_refs = refs[0], refs[1:1 + 2 * n_mix]
    g_ref, wg_ref, wu_ref, wd_ref, fg_ref, o_ref, xn_s, acc_s, h_s = refs[1 + 2 * n_mix:]
    f = pl.program_id(1)

    @pl.when(f == 0)
    def _():
        h = x_ref[...]
        for a_ref, w_ref in zip(mix_refs[::2], mix_refs[1::2]):
            h = h + jnp.dot(a_ref[...], w_ref[...], preferred_element_type=F32)
        h_s[...] = h
        xn_s[...] = _rms(h, g_ref[...]).astype(BF16)
        acc_s[...] = jnp.zeros(acc_s.shape, F32)

    xn = xn_s[...]
    gate = jnp.dot(xn, wg_ref[0].astype(BF16), preferred_element_type=F32)
    up = jnp.dot(xn, wu_ref[0].astype(BF16), preferred_element_type=F32)
    mid = (gate / (1.0 + jnp.exp(-gate)) * up).astype(BF16)
    acc_s[...] += jnp.dot(mid, wd_ref[0].astype(BF16), preferred_element_type=F32)

    @pl.when(f == pl.num_programs(1) - 1)
    def _():
        y = h_s[...] + acc_s[...]
        if final_norm:
            y = _rms(y, fg_ref[...])
        o_ref[...] = y


def ffn(x, mix, g, wg, wu, wd, layer, fg, final_norm, name):
    M, K = x.shape
    tm, tf = FFN_ROW_TILE, FFN_COL_TILE
    kern = functools.partial(_ffn_kernel, n_mix=len(mix), final_norm=final_norm)
    mix_specs, mix_args = [], []
    for a, w in mix:
        mix_specs += [pl.BlockSpec((tm, a.shape[1]), lambda i, f: (i, 0)), pl.BlockSpec(w.shape, lambda i, f: (0, 0))]
        mix_args += [a, w]
    return pl.pallas_call(
        kern,
        grid=(M // tm, D_FF // tf),
        in_specs=[pl.BlockSpec((tm, K), lambda i, f: (i, 0)), *mix_specs,
                  pl.BlockSpec((1, K), lambda i, f: (0, 0)),
                  pl.BlockSpec((1, K, tf), lambda i, f: (layer, 0, f)),
                  pl.BlockSpec((1, K, tf), lambda i, f: (layer, 0, f)),
                  pl.BlockSpec((1, tf, K), lambda i, f: (layer, f, 0)),
                  pl.BlockSpec((1, K), lambda i, f: (0, 0))],
        out_specs=pl.BlockSpec((tm, K), lambda i, f: (i, 0)),
        out_shape=jax.ShapeDtypeStruct((M, K), F32),
        scratch_shapes=[pltpu.VMEM((tm, K), BF16), pltpu.VMEM((tm, K), F32), pltpu.VMEM((tm, K), F32)],
        compiler_params=_params("parallel", "arbitrary"),
        name=name,
    )(x, *mix_args, g.reshape(1, K), wg, wu, wd, fg.reshape(1, K))


def _trig_kernel(ang_ref, cos_ref, sin_ref):
    a = ang_ref[...]
    cos_ref[...] = jnp.cos(a)
    sin_ref[...] = jnp.sin(a)


def rope_tables(seq_len):
    half = D_ROPE // 2
    inv = ROPE_THETA ** (-jnp.arange(half, dtype=F32) / half)
    ang = jnp.arange(seq_len).astype(F32)[:, None] * inv[None, :]
    dense = ang.reshape(seq_len * half // LANE, LANE)
    spec = pl.BlockSpec(dense.shape, lambda: (0, 0))
    cos, sin = pl.pallas_call(
        _trig_kernel,
        in_specs=[spec],
        out_specs=[spec, spec],
        out_shape=[jax.ShapeDtypeStruct(dense.shape, F32)] * 2,
        name="rope_trig",
    )(dense)
    cos, sin = cos.reshape(seq_len, half), sin.reshape(seq_len, half)
    pad = LANE - D_NOPE - D_ROPE
    cos_l = jnp.concatenate([jnp.ones((seq_len, D_NOPE), F32), cos, cos, jnp.ones((seq_len, pad), F32)], axis=1)
    sin_l = jnp.concatenate([jnp.zeros((seq_len, D_NOPE), F32), sin, sin, jnp.zeros((seq_len, pad), F32)], axis=1)
    return cos_l, sin_l


def _prep_d_kernel(x_ref, g_ref, wa_ref, qn_ref, kvn_ref, wq_ref, wkv_ref, cos_ref, sin_ref,
                   q_ref, k_ref, v_ref, *, qscale):
    xn = _rms(x_ref[...], g_ref[...]).astype(BF16)
    lat = jnp.dot(xn, wa_ref[...], preferred_element_type=F32)
    cq = _rms(lat[:, :D_Q_LORA], qn_ref[...]).astype(BF16)
    ckv = _rms(lat[:, D_Q_LORA:D_Q_LORA + D_KV_LORA], kvn_ref[...]).astype(BF16)
    o_pe = D_Q_LORA + D_KV_LORA
    cos, sin = cos_ref[...], sin_ref[...]
    kpe = lat[:, o_pe:o_pe + LANE] * cos + lat[:, o_pe + LANE:o_pe + 2 * LANE] * sin
    qq = jnp.dot(cq, wq_ref[...], preferred_element_type=F32)
    kv = jnp.dot(ckv, wkv_ref[...], preferred_element_type=F32)
    kw = D_HEADS * LANE
    ones_blk = _ones_row_block(V_ROWS_D - D_V, x_ref.shape[0])
    for h in range(D_HEADS):
        qh = qq[:, h * LANE:(h + 1) * LANE] * cos + qq[:, kw + h * LANE:kw + (h + 1) * LANE] * sin
        q_ref[0, h, 0] = (qh * qscale).T.astype(BF16)
        k_ref[:, h * LANE:(h + 1) * LANE] = (kv[:, h * LANE:(h + 1) * LANE] + kpe).astype(BF16)
    for j in range(D_HEADS // 2):
        vt = kv[:, kw + j * LANE:kw + (j + 1) * LANE].T.astype(BF16)
        for half in range(2):
            v_ref[0, 2 * j + half, 0, :D_V] = vt[half * D_V:(half + 1) * D_V]
            v_ref[0, 2 * j + half, 0, D_V:] = ones_blk


def prep_d(x, g, wa, qn, kvn, wq, wkv, cos_l, sin_l, batch, seq_len):
    M, K = x.shape
    T = DENSE_TILE
    nrow = seq_len // T
    kw = D_HEADS * LANE
    kern = functools.partial(_prep_d_kernel, qscale=(D_NOPE + D_ROPE) ** -0.5 * LOG2E)
    full = lambda a: pl.BlockSpec(a.shape, lambda i: (0, 0))
    qn2, kvn2, g2 = qn.reshape(1, -1), kvn.reshape(1, -1), g.reshape(1, K)
    return pl.pallas_call(
        kern,
        grid=(M // T,),
        in_specs=[pl.BlockSpec((T, K), lambda i: (i, 0)), full(g2), full(wa), full(qn2), full(kvn2),
                  full(wq), full(wkv),
                  pl.BlockSpec((T, LANE), lambda i: (i % nrow, 0)),
                  pl.BlockSpec((T, LANE), lambda i: (i % nrow, 0))],
        out_specs=[pl.BlockSpec((1, D_HEADS, 1, LANE, T), lambda i: (i // nrow, 0, i % nrow, 0, 0)),
                   pl.BlockSpec((T, kw), lambda i: (i, 0)),
                   pl.BlockSpec((1, D_HEADS, 1, V_ROWS_D, T), lambda i: (i // nrow, 0, i % nrow, 0, 0))],
        out_shape=[jax.ShapeDtypeStruct((batch, D_HEADS, nrow, LANE, T), BF16),
                   jax.ShapeDtypeStruct((M, kw), BF16),
                   jax.ShapeDtypeStruct((batch, D_HEADS, nrow, V_ROWS_D, T), BF16)],
        compiler_params=_params("parallel"),
        name="prep_d",
    )(x, g2, wa, qn2, kvn2, wq, wkv, cos_l, sin_l)


B_HEAD_ORDER = (0, 4, 1, 5, 2, 6, 3, 7)


def _blocks(w, starts, width, axis):
    return jnp.concatenate([lax.slice_in_dim(w, s, s + width, axis=axis) for s in starts], axis=axis)


def _rot_partner_cols(w):
    half = D_ROPE // 2
    return jnp.concatenate([-w[..., half:], w[..., :half]], axis=-1)


def kernel(x, bias_table, attn_norm, ffn_norm, final_norm, ab_w_in, ab_lambda_q1, ab_lambda_k1,
           ab_lambda_q2, ab_lambda_k2, ab_subln, ab_sink, ab_w_o, cd_w_in, cd_q_norm, cd_w_q_b,
           cd_kv_norm, cd_w_kv_b, cd_w_o, ffn_w_gate, ffn_w_up, ffn_w_down):
    B, S, _ = x.shape
    M = B * S
    T = DENSE_TILE
    h = x.reshape(M, D_MODEL)
    qk_scale = HEAD_DIM ** -0.5 * LOG2E

    o3 = A_HEADS * (2 * A_QK_DIM + A_V_DIM)
    a0 = ab_w_in[0].astype(BF16)
    w0 = jnp.concatenate([a0[:, :o3], _blocks(a0, [o3 + hd * HEAD_DIM for hd in B_HEAD_ORDER], HEAD_DIM, 1),
                          a0[:, o3 + B_HEADS * HEAD_DIM:]], axis=1)
    cs0 = np.ones((AB_IN,), np.float32)
    cs0[:A_HEADS * A_QK_DIM] = qk_scale
    cs0[o3:o3 + B_HEADS * HEAD_DIM] = qk_scale
    qat, vat, qbt, vbt, keys0 = proj0_call(h, attn_norm[0], w0, jnp.asarray(cs0), B, S)
    keys0 = keys0.reshape(B, S, -1)
    bias_a = bias_tiles(bias_table, nvar=2 * BIAS_REACH + 1, nheads=A_HEADS, head0=0, rows=T, cols=T,
                        off0=-BIAS_REACH * T, off_step=T, row_coef=1, col_coef=-1, dil=1,
                        half_window=None, name="bias_a")
    oa = attn_a(qat, keys0, vat, bias_a, ab_lambda_q1[0], ab_lambda_k1[0], ab_lambda_q2[0],
                ab_lambda_k2[0], ab_subln[0], 0.8 - 0.6 * math.exp(-0.3 * 0))

    bias_b = bias_tiles(bias_table, nvar=3, nheads=B_HEADS, head0=A_HEADS, rows=QCOLS + 2 * B_HALF_WINDOW,
                        cols=QCOLS, off0=0, off_step=-B_HALF_WINDOW, row_coef=1, col_coef=-1, dil=1,
                        half_window=B_HALF_WINDOW, name="bias_b")
    ob = attn_b(qbt, keys0, (A_HEADS * A_QK_DIM) // LANE, vbt, bias_b, ab_sink[0])

    wo = ab_w_o[0].astype(BF16)
    wo_a = wo[:A_HEADS * A_V_DIM]
    wo_b = _blocks(wo, [A_HEADS * A_V_DIM + hd * HEAD_DIM for hd in B_HEAD_ORDER], HEAD_DIM, 0)
    h = ffn(h, [(oa.reshape(M, -1), wo_a), (ob.reshape(M, -1), wo_b)], ffn_norm[0], ffn_w_gate, ffn_w_up,
            ffn_w_down, 0, final_norm, False, "ffn0")

    w1 = cd_w_in[0]
    gw = C_HEADS_PER_GROUP * HEAD_DIM
    cw = C_HEADS * HEAD_DIM
    starts1 = [role * cw + g * gw for g in range(len(C_PATTERNS)) for role in range(3)]
    cs1 = np.ones((CD_C_IN,), np.float32)
    for g in range(len(C_PATTERNS)):
        cs1[3 * g * gw:(3 * g + 1) * gw] = qk_scale
    c_views = proj1_call(h, attn_norm[1], _blocks(w1.astype(BF16), starts1, gw, 1), jnp.asarray(cs1), B, S)

    oc, lses = [], []
    for g, (window, dil) in enumerate(C_PATTERNS):
        hw = window // (2 * dil)
        tq_c = 128
        bias_c = bias_tiles(bias_table, nvar=3, nheads=C_HEADS_PER_GROUP, head0=g * C_HEADS_PER_GROUP,
                            rows=tq_c, cols=tq_c + 2 * hw, off0=0, off_step=-hw, row_coef=-1, col_coef=1,
                            dil=dil, half_window=hw, name=f"bias_c{g}")
        o_g, lse_g = banded(c_views[g], bias_c, None, dil=dil, tq=tq_c, half_window=hw, nqb=2, nkb=2,
                            q_idx=lambda r: 3 * r, k_idx=lambda r: 3 * r + 1, v_idx=lambda r: 3 * r + 2,
                            head_of=lambda c, half: 2 * c + half, out_dtype=F32, has_lse=True,
                            name=f"attn_c{g}")
        oc.append(o_g)
        lses.append(lse_g)

    o_q, o_kv = CD_C_IN + D_Q_LORA, CD_C_IN + D_Q_LORA + D_KV_LORA
    w_pe = w1[:, o_kv:]
    lane_pad = lambda w: jnp.pad(w, ((0, 0), (D_NOPE, LANE - D_NOPE - D_ROPE)))
    wa = jnp.concatenate([w1[:, CD_C_IN:o_kv], lane_pad(w_pe), lane_pad(_rot_partner_cols(w_pe))],
                         axis=1).astype(BF16)
    wq3 = cd_w_q_b[0].reshape(D_Q_LORA, D_HEADS, D_NOPE + D_ROPE)
    zpad = jnp.zeros((D_Q_LORA, D_HEADS, LANE - D_NOPE - D_ROPE), F32)
    wq_main = jnp.concatenate([wq3, zpad], axis=-1)
    wq_rot = jnp.concatenate([jnp.zeros_like(wq3[..., :D_NOPE]), _rot_partner_cols(wq3[..., D_NOPE:]), zpad],
                             axis=-1)
    wq = jnp.concatenate([wq_main.reshape(D_Q_LORA, -1), wq_rot.reshape(D_Q_LORA, -1)], axis=1).astype(BF16)
    wkv3 = cd_w_kv_b[0].reshape(D_KV_LORA, D_HEADS, D_NOPE + D_V)
    wk = jnp.pad(wkv3[..., :D_NOPE], ((0, 0), (0, 0), (0, LANE - D_NOPE))).reshape(D_KV_LORA, -1)
    wv = wkv3[..., D_NOPE:].reshape(D_KV_LORA, -1)
    wkv = jnp.concatenate([wk, wv], axis=1).astype(BF16)
    cos_l, sin_l = rope_tables(S)
    qdt, kd, vdt = prep_d(h, attn_norm[1], wa, cd_q_norm[0], cd_kv_norm[0], wq, wkv, cos_l, sin_l, B, S)
    od = attn_d(qdt, kd.reshape(B, S, D_HEADS * LANE), vdt)

    wo1 = cd_w_o[0]
    wo_c = wo1[:C_HEADS_PER_GROUP * HEAD_DIM].astype(BF16)
    wo_d = wo1[C_HEADS_PER_GROUP * HEAD_DIM:].astype(BF16)
    h = out_proj_cd(h, oc, lses, od.reshape(M, -1), wo_c, wo_d, B, S)
    h = ffn(h, [], ffn_norm[1], ffn_w_gate, ffn_w_up, ffn_w_down, 1, final_norm, True, "ffn1")
    return h.reshape(B, S, D_MODEL)
```

```python
import functools
import math

import numpy as np
import jax
import jax.numpy as jnp
from jax import lax
from jax.experimental import pallas as pl
from jax.experimental.pallas import tpu as pltpu

F32 = jnp.float32
BF16 = jnp.bfloat16

D_MODEL = 1024
HEAD_DIM = 64
EPS = 1e-6
NEG = -1e30
LOG2E = math.log2(math.e)
LN2 = math.log(2.0)

A_HEADS = 4
A_QK_DIM = 2 * HEAD_DIM
A_V_DIM = 2 * HEAD_DIM
B_HEADS = 8
B_KV_HEADS = 2
B_HALF_WINDOW = 128
C_PATTERNS = ((128, 1), (512, 4), (2048, 16))
C_HEADS_PER_GROUP = 4
C_HEADS = C_HEADS_PER_GROUP * len(C_PATTERNS)
D_HEADS = 12
D_Q_LORA = 384
D_KV_LORA = 256
D_NOPE = 64
D_ROPE = 32
D_V = 64
ROPE_THETA = 10000.0
NUM_BUCKETS = 32
MAX_DISTANCE = 1024
D_FF = 2816
AB_IN = 2304
CD_C_IN = 3 * C_HEADS * HEAD_DIM

LANE = 128
VMEM_LIMIT = 48 * 1024 * 1024

ROW_TILE = 512
FFN_ROW_TILE = 1024
FFN_COL_TILE = 256
DENSE_TILE = 512
QCOLS = 256
UNROLL = 4
DENSE_QTILES = 4
A_PASSES = 2
D_PASSES = 4
BAND_GROUP = 4
BIAS_REACH = 3
V_ROWS_D = 80
V_ROWS_A = 144


def _bucket_thresholds():
    nb = NUM_BUCKETS // 2
    max_exact = nb // 2
    n = np.arange(1, 4 * MAX_DISTANCE)
    large = max_exact + (np.log(n.astype(np.float32) / np.float32(max_exact))
                         / np.float32(math.log(MAX_DISTANCE / max_exact))
                         * np.float32(nb - max_exact)).astype(np.int32)
    mag = np.where(n < max_exact, n, np.minimum(large, nb - 1))
    return tuple(int(n[np.argmax(mag >= k)]) for k in range(1, nb))


BUCKET_THRESHOLDS = _bucket_thresholds()
assert BUCKET_THRESHOLDS[-1] <= (BIAS_REACH - 1) * DENSE_TILE + 1


def _params(*sem):
    return pltpu.CompilerParams(dimension_semantics=sem, vmem_limit_bytes=VMEM_LIMIT)


def _rms(x, g):
    return x * lax.rsqrt(jnp.mean(x * x, axis=-1, keepdims=True) + EPS) * g


def _ones_row_block(rows, cols):
    r = lax.broadcasted_iota(jnp.int32, (rows, cols), 0)
    return jnp.where(r == 0, 1.0, 0.0).astype(BF16)


def _store_masked_halves(q_ref, idx, qt):
    zeros = jnp.zeros((HEAD_DIM, qt.shape[1]), BF16)
    q_ref[idx + (0, 0, slice(None, HEAD_DIM))] = qt[:HEAD_DIM]
    q_ref[idx + (0, 0, slice(HEAD_DIM, None))] = zeros
    q_ref[idx + (1, 0, slice(None, HEAD_DIM))] = zeros
    q_ref[idx + (1, 0, slice(HEAD_DIM, None))] = qt[HEAD_DIM:]


def _proj0_kernel(x_ref, g_ref, w_ref, cs_ref, qa_ref, va_ref, qb_ref, vb_ref, k_ref):
    xn = _rms(x_ref[...], g_ref[...]).astype(BF16)
    y = jnp.dot(xn, w_ref[...], preferred_element_type=F32) * cs_ref[...]
    tm = y.shape[0]
    ka0 = A_HEADS * A_QK_DIM
    va0 = 2 * ka0
    qb0 = va0 + A_HEADS * A_V_DIM
    kb0 = qb0 + B_HEADS * HEAD_DIM
    vb0 = kb0 + B_KV_HEADS * HEAD_DIM
    for h in range(A_HEADS):
        _store_masked_halves(qa_ref, (0, h), y[:, h * A_QK_DIM:(h + 1) * A_QK_DIM].T.astype(BF16))
        va_ref[0, h, 0, :A_V_DIM] = y[:, va0 + h * A_V_DIM:va0 + (h + 1) * A_V_DIM].T.astype(BF16)
        va_ref[0, h, 0, A_V_DIM:] = _ones_row_block(V_ROWS_A - A_V_DIM, tm)
    for j in range(B_HEADS // 2):
        _store_masked_halves(qb_ref, (0, j), y[:, qb0 + j * LANE:qb0 + (j + 1) * LANE].T.astype(BF16))
    vbt = y[:, vb0:vb0 + LANE].T.astype(BF16)
    ones_blk = _ones_row_block(V_ROWS_D - HEAD_DIM, LANE)
    for g in range(B_KV_HEADS):
        for c in range(tm // LANE):
            vb_ref[0, g, c, :HEAD_DIM] = vbt[g * HEAD_DIM:(g + 1) * HEAD_DIM, c * LANE:(c + 1) * LANE]
            vb_ref[0, g, c, HEAD_DIM:] = ones_blk
    k_ref[:, :ka0] = y[:, ka0:va0].astype(BF16)
    k_ref[:, ka0:] = y[:, kb0:vb0].astype(BF16)


def proj0_call(x, g, w, colscale, batch, seq_len):
    M, K = x.shape
    N = w.shape[1]
    T = DENSE_TILE
    nrow = seq_len // T
    nkb = A_HEADS * A_QK_DIM + B_KV_HEADS * HEAD_DIM
    qspec = pl.BlockSpec((1, A_HEADS, 2, 1, LANE, T), lambda i: (i // nrow, 0, 0, i % nrow, 0, 0))
    qshape = jax.ShapeDtypeStruct((batch, A_HEADS, 2, nrow, LANE, T), BF16)
    return pl.pallas_call(
        _proj0_kernel,
        grid=(M // T,),
        in_specs=[pl.BlockSpec((T, K), lambda i: (i, 0)),
                  pl.BlockSpec((1, K), lambda i: (0, 0)),
                  pl.BlockSpec((K, N), lambda i: (0, 0)),
                  pl.BlockSpec((1, N), lambda i: (0, 0))],
        out_specs=[qspec,
                   pl.BlockSpec((1, A_HEADS, 1, V_ROWS_A, T), lambda i: (i // nrow, 0, i % nrow, 0, 0)),
                   qspec,
                   pl.BlockSpec((1, B_KV_HEADS, T // LANE, V_ROWS_D, LANE),
                                lambda i: (i // nrow, 0, i % nrow, 0, 0)),
                   pl.BlockSpec((T, nkb), lambda i: (i, 0))],
        out_shape=[qshape,
                   jax.ShapeDtypeStruct((batch, A_HEADS, nrow, V_ROWS_A, T), BF16),
                   qshape,
                   jax.ShapeDtypeStruct((batch, B_KV_HEADS, seq_len // LANE, V_ROWS_D, LANE), BF16),
                   jax.ShapeDtypeStruct((M, nkb), BF16)],
        compiler_params=_params("parallel"),
        name="proj0",
    )(x, g.reshape(1, K), w, colscale.reshape(1, N))


def _bias_kernel(tab_ref, o_ref, *, nvar, off0, off_step, row_coef, col_coef, dil, half_window, head0):
    hcol = head0 + pl.program_id(1)
    R, C = o_ref.shape[-2:]
    row = lax.broadcasted_iota(jnp.int32, (R, C), 0)
    col = lax.broadcasted_iota(jnp.int32, (R, C), 1)
    base = row_coef * row + col_coef * col
    span_lo = min(row_coef * (R - 1), 0) + min(col_coef * (C - 1), 0)
    span_hi = max(row_coef * (R - 1), 0) + max(col_coef * (C - 1), 0)
    nb = NUM_BUCKETS // 2

    def side(n, n_lo, n_hi, row0):
        val = jnp.full((R, C), tab_ref[row0 + sum(t <= n_lo for t in BUCKET_THRESHOLDS), hcol], F32)
        for k, thr in enumerate(BUCKET_THRESHOLDS, start=1):
            if n_lo < thr <= n_hi:
                val = jnp.where(n >= thr, tab_ref[row0 + k, hcol], val)
        return val

    for v in range(nvar):
        @pl.when(pl.program_id(0) == v)
        def _(v=v):
            off = off0 + v * off_step
            rel = off + base
            lo, hi = (off + span_lo) * dil, (off + span_hi) * dil
            dist = rel * dil
            n = jnp.abs(dist)
            if hi <= 0:
                val = side(n, -hi, -lo, 0)
            elif lo > 0:
                val = side(n, lo, hi, nb)
            else:
                val = jnp.where(dist > 0, side(n, 1, hi, nb), side(n, 0, -lo, 0))
            val = val * LOG2E
            if half_window is not None:
                val = jnp.where(jnp.abs(rel) <= half_window, val, NEG)
            o_ref[0, 0] = val


def bias_tiles(table, *, nvar, nheads, head0, rows, cols, off0, off_step, row_coef, col_coef,
               dil, half_window, name):
    kern = functools.partial(_bias_kernel, nvar=nvar, off0=off0, off_step=off_step, row_coef=row_coef,
                             col_coef=col_coef, dil=dil, half_window=half_window, head0=head0)
    return pl.pallas_call(
        kern,
        grid=(nvar, nheads),
        in_specs=[pl.BlockSpec(memory_space=pltpu.SMEM)],
        out_specs=pl.BlockSpec((1, 1, rows, cols), lambda v, h: (v, h, 0, 0)),
        out_shape=jax.ShapeDtypeStruct((nvar, nheads, rows, cols), F32),
        compiler_params=_params("parallel", "parallel"),
        name=name,
    )(table)


def _dense_pipeline(nk, tile, n_streams, score_fn, value_fn, m_s, acc_s, bufs):
    units = [(i, c * QCOLS) for i in range(n_streams) for c in range(tile // QCOLS)]

    def produce(kc, unit, nxt):
        i, c0 = unit
        cols = slice(c0, c0 + QCOLS)
        s = score_fn(kc + 1, i, cols)
        nxt[0][i, :, cols] = s
        nxt[1][i, :, cols] = jnp.max(s, axis=0, keepdims=True)

    def consume(kc, unit, cur):
        i, c0 = unit
        cols = slice(c0, c0 + QCOLS)
        m_old = m_s[i, :, cols]
        m_new = jnp.maximum(m_old, cur[1][i, :, cols])
        alpha = jnp.exp2(m_old - m_new)
        p = jnp.exp2((cur[0][i, :, cols] - m_new).astype(BF16))
        acc_s[i, :, cols] = (alpha * acc_s[i, :, cols]
                             + jnp.dot(value_fn(kc, i), p, preferred_element_type=F32))
        m_s[i, :, cols] = m_new

    def stage(kc, cur, nxt):
        for unit in units:
            if nxt is not None:
                produce(kc, unit, nxt)
            if cur is not None:
                consume(kc, unit, cur)

    m_s[...] = jnp.full(m_s.shape, NEG, F32)
    acc_s[...] = jnp.zeros(acc_s.shape, F32)
    stage(-1, None, bufs[0])
    n_loop = (nk - 1) // UNROLL

    def body(j, carry):
        for u in range(UNROLL):
            stage(UNROLL * j + u, bufs[u % 2], bufs[(u + 1) % 2])
        return carry

    lax.fori_loop(0, n_loop, body, 0)
    for kc in range(n_loop * UNROLL, nk):
        stage(kc, bufs[kc % 2], bufs[(kc + 1) % 2] if kc < nk - 1 else None)


def _attn_a_kernel(q_ref, k_ref, v_ref, bias_ref, lq1_ref, lk1_ref, lq2_ref, lk2_ref, subln_ref, o_ref,
                   m_s, acc_s, s_a, s_b, cm_a, cm_b, *, tile, nk, qtiles, npass, lambda_init):
    lam = (jnp.exp(jnp.sum(lq1_ref[...] * lk1_ref[...], axis=-1, keepdims=True))
           - jnp.exp(jnp.sum(lq2_ref[...] * lk2_ref[...], axis=-1, keepdims=True)) + lambda_init)

    def one_pass(p, carry):
        q0 = p * qtiles
        qi0 = pl.program_id(2) * (qtiles * npass) + q0

        def score_fn(kc, i, cols):
            j, qt = divmod(i, qtiles)
            kblk = k_ref[0, pl.ds(pl.multiple_of(kc * tile, tile), tile), :]
            bt = bias_ref[jnp.clip(kc - (qi0 + qt), -BIAS_REACH, BIAS_REACH) + BIAS_REACH, 0, :, cols]
            return jnp.dot(kblk, q_ref[0, 0, j, q0 + qt, :, cols], preferred_element_type=F32) + bt

        def value_fn(kc, i):
            return v_ref[0, 0, kc]

        _dense_pipeline(nk, tile, 2 * qtiles, score_fn, value_fn, m_s, acc_s, ((s_a, cm_a), (s_b, cm_b)))
        for qt in range(qtiles):
            a1, a2 = acc_s[qt], acc_s[qtiles + qt]
            o = (a1[:A_V_DIM] / a1[A_V_DIM:A_V_DIM + 1]
                 - lam * (a2[:A_V_DIM] / a2[A_V_DIM:A_V_DIM + 1]))
            ms = jnp.mean(o * o, axis=0, keepdims=True)
            y = o * lax.rsqrt(ms + EPS) * subln_ref[...] * (1.0 - lambda_init)
            rows = pl.ds(pl.multiple_of((q0 + qt) * tile, tile), tile)
            o_ref[0, rows] = y.T.astype(o_ref.dtype)
        return carry

    lax.fori_loop(0, npass, one_pass, 0)


def attn_a(qt, karr, vt, bias, lq1, lk1, lq2, lk2, subln, lambda_init):
    B, S, _ = karr.shape
    T = DENSE_TILE
    nq = nk = S // T
    assert nk % 2 == 0 and nk >= 4
    qtiles = math.gcd(nq, DENSE_QTILES)
    npass = math.gcd(nq // qtiles, A_PASSES)
    ns = 2 * qtiles
    kern = functools.partial(_attn_a_kernel, tile=T, nk=nk, qtiles=qtiles, npass=npass, lambda_init=lambda_init)
    vec = lambda n: pl.BlockSpec((1, n), lambda h, b, i: (0, 0))
    return pl.pallas_call(
        kern,
        grid=(A_HEADS, B, nq // (qtiles * npass)),
        in_specs=[pl.BlockSpec((1, 1, 2, qtiles * npass, LANE, T), lambda h, b, i: (b, h, 0, i, 0, 0)),
                  pl.BlockSpec((1, S, LANE), lambda h, b, i: (b, 0, h)),
                  pl.BlockSpec((1, 1, nk, V_ROWS_A, T), lambda h, b, i: (b, h, 0, 0, 0)),
                  pl.BlockSpec((2 * BIAS_REACH + 1, 1, T, T), lambda h, b, i: (0, h, 0, 0),
                               pipeline_mode=pl.Buffered(1)),
                  vec(HEAD_DIM), vec(HEAD_DIM), vec(HEAD_DIM), vec(HEAD_DIM),
                  pl.BlockSpec((A_V_DIM, 1), lambda h, b, i: (0, 0))],
        out_specs=pl.BlockSpec((1, qtiles * npass * T, LANE), lambda h, b, i: (b, i, h)),
        out_shape=jax.ShapeDtypeStruct((B, S, A_HEADS * A_V_DIM), BF16),
        scratch_shapes=[pltpu.VMEM((ns, 1, T), F32), pltpu.VMEM((ns, V_ROWS_A, T), F32),
                        pltpu.VMEM((ns, T, T), F32), pltpu.VMEM((ns, T, T), F32),
                        pltpu.VMEM((ns, 1, T), F32), pltpu.VMEM((ns, 1, T), F32)],
        compiler_params=_params("parallel", "parallel", "arbitrary"),
        name="attn_a",
    )(qt, karr, vt, bias, lq1.reshape(1, -1), lk1.reshape(1, -1), lq2.reshape(1, -1),
      lk2.reshape(1, -1), subln.reshape(-1, 1))


def _attn_d_kernel(q_ref, k_ref, v_ref, o_ref, m_s, acc_s, s_a, s_b, cm_a, cm_b, *, tile, nk, qtiles, npass):
    def one_pass(p, carry):
        q0 = p * qtiles

        def score_fn(kc, i, cols):
            hh, qt = divmod(i, qtiles)
            kblk = k_ref[0, pl.ds(pl.multiple_of(kc * tile, tile), tile), hh * LANE:(hh + 1) * LANE]
            return jnp.dot(kblk, q_ref[0, hh, q0 + qt, :, cols], preferred_element_type=F32)

        def value_fn(kc, i):
            return v_ref[0, i // qtiles, kc]

        _dense_pipeline(nk, tile, 2 * qtiles, score_fn, value_fn, m_s, acc_s, ((s_a, cm_a), (s_b, cm_b)))
        for qt in range(qtiles):
            outs = []
            for hh in range(2):
                acc = acc_s[hh * qtiles + qt]
                outs.append(acc[:D_V] / acc[D_V:D_V + 1])
            rows = pl.ds(pl.multiple_of((q0 + qt) * tile, tile), tile)
            o_ref[0, rows] = jnp.concatenate(outs, axis=0).T.astype(o_ref.dtype)
        return carry

    lax.fori_loop(0, npass, one_pass, 0)


def attn_d(qt, k, vt):
    B, S, _ = k.shape
    T = DENSE_TILE
    nq = nk = S // T
    qtiles = math.gcd(nq, DENSE_QTILES)
    npass = math.gcd(nq // qtiles, D_PASSES)
    ns = 2 * qtiles
    kern = functools.partial(_attn_d_kernel, tile=T, nk=nk, qtiles=qtiles, npass=npass)
    return pl.pallas_call(
        kern,
        grid=(B, D_HEADS // 2, nq // (qtiles * npass)),
        in_specs=[pl.BlockSpec((1, 2, qtiles * npass, LANE, T), lambda b, h, i: (b, h, i, 0, 0)),
                  pl.BlockSpec((1, S, 2 * LANE), lambda b, h, i: (b, 0, h)),
                  pl.BlockSpec((1, 2, nk, V_ROWS_D, T), lambda b, h, i: (b, h, 0, 0, 0))],
        out_specs=pl.BlockSpec((1, qtiles * npass * T, LANE), lambda b, h, i: (b, i, h)),
        out_shape=jax.ShapeDtypeStruct((B, S, D_HEADS * D_V), BF16),
        scratch_shapes=[pltpu.VMEM((ns, 1, T), F32), pltpu.VMEM((ns, V_ROWS_D, T), F32),
                        pltpu.VMEM((ns, T, T), F32), pltpu.VMEM((ns, T, T), F32),
                        pltpu.VMEM((ns, 1, T), F32), pltpu.VMEM((ns, 1, T), F32)],
        compiler_params=_params("parallel", "parallel", "arbitrary"),
        name="attn_d",
    )(qt, k, vt)


def _attn_b_kernel(q_ref, k_ref, v_ref, bias_ref, sink_ref, o_ref, s_a, s_b, s_c, cm_a, cm_b, cm_c, *, tile, seq_len):
    half_window = B_HALF_WINDOW
    span = QCOLS + 2 * half_window
    nchunk = span // LANE
    step = pl.program_id(1)

    def window(c):
        q0 = step * tile + c * QCOLS
        start = pl.multiple_of(jnp.clip(q0 - half_window, 0, seq_len - span), LANE)
        variant = jnp.where(q0 == 0, 0, jnp.where(q0 + QCOLS == seq_len, 2, 1))
        return start, variant

    def scores(task):
        c, j, g = task
        start, variant = window(c)
        kwin = k_ref[0, pl.ds(start, span), :]
        s = jnp.dot(kwin, q_ref[0, j, g, 0, :, c * QCOLS:(c + 1) * QCOLS], preferred_element_type=F32)
        return s + bias_ref[variant, j + (B_HEADS // 2) * g]

    def finish(task, s_ref, cm_ref):
        c, j, g = task
        head = j + (B_HEADS // 2) * g
        start, _ = window(c)
        chunk0 = start // LANE
        vwin = jnp.concatenate([v_ref[0, g, chunk0 + n] for n in range(nchunk)], axis=1)
        sk = sink_ref[:, head:head + 1] * LOG2E
        m = jnp.maximum(cm_ref[...], sk)
        e = jnp.exp2((s_ref[...] - m).astype(BF16))
        ov = jnp.dot(vwin, e, preferred_element_type=F32)
        denom = ov[HEAD_DIM:HEAD_DIM + 1] + jnp.exp2(sk - m)
        return ov[:HEAD_DIM] / denom

    tasks = [(c, j, g) for c in range(tile // QCOLS) for j in range(B_HEADS // 2) for g in range(B_KV_HEADS)]
    bufs = ((s_a, cm_a), (s_b, cm_b), (s_c, cm_c))

    def produce(task, buf):
        s = scores(task)
        buf[0][...] = s
        buf[1][...] = jnp.max(s, axis=0, keepdims=True)

    produce(tasks[0], bufs[0])
    produce(tasks[1], bufs[1])
    held = None
    for n, task in enumerate(tasks):
        if n + 2 < len(tasks):
            produce(tasks[n + 2], bufs[(n + 2) % 3])
        o = finish(task, *bufs[n % 3])
        c, j, g = task
        if g == 0:
            held = o
            continue
        pair = jnp.concatenate([held, o], axis=0).T
        o_ref[0, c * QCOLS:(c + 1) * QCOLS, j * LANE:(j + 1) * LANE] = pair.astype(o_ref.dtype)


def attn_b(qt, karr, k_block, vt, bias, sink):
    B, S, _ = karr.shape
    T = DENSE_TILE
    span = QCOLS + 2 * B_HALF_WINDOW
    return pl.pallas_call(
        functools.partial(_attn_b_kernel, tile=T, seq_len=S),
        grid=(B, S // T),
        in_specs=[pl.BlockSpec((1, B_HEADS // 2, 2, 1, LANE, T), lambda b, i: (b, 0, 0, i, 0, 0)),
                  pl.BlockSpec((1, S, LANE), lambda b, i: (b, 0, k_block)),
                  pl.BlockSpec((1, B_KV_HEADS, S // LANE, V_ROWS_D, LANE), lambda b, i: (b, 0, 0, 0, 0)),
                  pl.BlockSpec(bias.shape, lambda b, i: (0, 0, 0, 0)),
                  pl.BlockSpec((1, B_HEADS), lambda b, i: (0, 0))],
        out_specs=pl.BlockSpec((1, T, B_HEADS * HEAD_DIM), lambda b, i: (b, i, 0)),
        out_shape=jax.ShapeDtypeStruct((B, S, B_HEADS * HEAD_DIM), BF16),
        scratch_shapes=[pltpu.VMEM((span, QCOLS), F32)] * 3 + [pltpu.VMEM((1, QCOLS), F32)] * 3,
        compiler_params=_params("parallel", "arbitrary"),
        name="attn_b",
    )(qt, karr, vt, bias, sink.reshape(1, -1))


def _banded_kernel(*refs, tq, group, span, half_window, seq_len, nqb, nkb, head_of, has_sink, has_lse):
    it = iter(refs)
    q_ref, k_ref, v_ref, bias_ref = next(it), next(it), next(it), next(it)
    sink_ref = next(it) if has_sink else None
    o_ref = next(it)
    lse_ref = next(it) if has_lse else None

    nq = seq_len // tq
    lane = lax.broadcasted_iota(jnp.int32, (tq, LANE), 1)
    low = lane < HEAD_DIM

    def window(g):
        t = pl.program_id(2) * group + g
        start = pl.multiple_of(jnp.clip(t * tq - half_window, 0, seq_len - span), half_window)
        variant = jnp.where(t == 0, 0, jnp.where(t == nq - 1, 2, 1))
        return start, variant

    def scores(task):
        g, c, half = task
        start, variant = window(g)
        kc = c if nkb == nqb else 0
        q2 = q_ref[0, g * tq:(g + 1) * tq, c * LANE:(c + 1) * LANE]
        kb = k_ref[0, pl.ds(start, span), kc * LANE:(kc + 1) * LANE]
        qm = jnp.where(low if half == 0 else jnp.logical_not(low), q2, jnp.zeros_like(q2))
        s = lax.dot_general(qm, kb, (((1,), (1,)), ((), ())), preferred_element_type=F32)
        return s + bias_ref[variant, head_of(c, half)]

    def finish(task, s):
        g, c, half = task
        start, _ = window(g)
        kc = c if nkb == nqb else 0
        hidx = head_of(c, half)
        vb = v_ref[0, pl.ds(start, span), kc * LANE:(kc + 1) * LANE]
        m = jnp.max(s, axis=-1, keepdims=True)
        if has_sink:
            sk = sink_ref[:, hidx:hidx + 1] * LOG2E
            m = jnp.maximum(m, sk)
        e = jnp.exp2(s - m)
        denom = jnp.sum(e, axis=-1, keepdims=True)
        if has_sink:
            denom = denom + jnp.exp2(sk - m)
        o = jnp.dot(e.astype(BF16), vb, preferred_element_type=F32) / denom
        return o, (LN2 * m + jnp.log(denom) if has_lse else None)

    tasks = [(g, c, half) for g in range(group) for c in range(nqb) for half in range(2)]
    s_next = scores(tasks[0])
    held = None
    for n, task in enumerate(tasks):
        s = s_next
        if n + 1 < len(tasks):
            s_next = scores(tasks[n + 1])
        o, lse = finish(task, s)
        g, c, half = task
        if half == 0:
            held = (o, lse)
            continue
        rows, cols = slice(g * tq, (g + 1) * tq), slice(c * LANE, (c + 1) * LANE)
        o_ref[0, rows, cols] = jnp.where(low, held[0], o).astype(o_ref.dtype)
        if has_lse:
            lse_ref[0, rows, cols] = jnp.where(low, held[1], lse)


def banded(view, bias, sink, *, dil, tq, half_window, nqb, nkb, q_idx, k_idx, v_idx, head_of,
           out_dtype, has_lse, name):
    B, L, _ = view.shape
    span = tq + 2 * half_window
    nq = L // tq
    assert L % tq == 0 and L >= span and nq >= 2
    group = math.gcd(nq, BAND_GROUP)
    OW = nqb * LANE
    has_sink = sink is not None
    kern = functools.partial(_banded_kernel, tq=tq, group=group, span=span, half_window=half_window,
                             seq_len=L, nqb=nqb, nkb=nkb, head_of=head_of, has_sink=has_sink,
                             has_lse=has_lse)
    in_specs = [pl.BlockSpec((1, group * tq, OW), lambda b, r, t: (b, t, q_idx(r))),
                pl.BlockSpec((1, L, nkb * LANE), lambda b, r, t: (b, 0, k_idx(r))),
                pl.BlockSpec((1, L, nkb * LANE), lambda b, r, t: (b, 0, v_idx(r))),
                pl.BlockSpec(bias.shape, lambda b, r, t: (0, 0, 0, 0))]
    args = [view, view, view, bias]
    if has_sink:
        in_specs.append(pl.BlockSpec((1, sink.shape[-1]), lambda b, r, t: (0, 0)))
        args.append(sink.reshape(1, -1))
    out_spec = pl.BlockSpec((1, group * tq, OW), lambda b, r, t: (b, t, r))
    out_shapes = [jax.ShapeDtypeStruct((B, L, dil * OW), out_dtype)]
    out_specs = [out_spec]
    if has_lse:
        out_shapes.append(jax.ShapeDtypeStruct((B, L, dil * OW), F32))
        out_specs.append(out_spec)
    outs = pl.pallas_call(
        kern,
        grid=(B, dil, nq // group),
        in_specs=in_specs,
        out_specs=out_specs,
        out_shape=out_shapes,
        compiler_params=_params("parallel", "parallel", "arbitrary"),
        name=name,
    )(*args)
    return outs


def _out_proj_cd_kernel(h_ref, o0, o1, o2, s0, s1, s2, d_ref, wc_ref, wd_ref, o_ref, *scratch):
    tm = h_ref.shape[0]
    width = C_HEADS_PER_GROUP * HEAD_DIM
    spare = iter(scratch)

    def in_position_order(ref, dil):
        if dil == 1:
            return ref[0]
        t_s = next(spare)
        for r in range(dil):
            for j in range(width // LANE):
                t_s[j, pl.ds(r, tm // dil, stride=dil), :] = ref[0, :, r * width + j * LANE:r * width + (j + 1) * LANE]
        return jnp.concatenate([t_s[j] for j in range(width // LANE)], axis=1)

    dils = [d for _, d in C_PATTERNS]
    outs = [in_position_order(r, d) for r, d in zip((o0, o1, o2), dils)]
    lses = [in_position_order(r, d) for r, d in zip((s0, s1, s2), dils)]
    mx = jnp.maximum(jnp.maximum(lses[0], lses[1]), lses[2])
    es = [jnp.exp(l - mx) for l in lses]
    oc = (es[0] * outs[0] + es[1] * outs[1] + es[2] * outs[2]) / (es[0] + es[1] + es[2])
    mix = jnp.dot(oc.astype(BF16), wc_ref[...], preferred_element_type=F32)
    mix = mix + jnp.dot(d_ref[...], wd_ref[...], preferred_element_type=F32)
    o_ref[...] = h_ref[...] + mix


def out_proj_cd(h, oc, lses, od, wc, wd, batch, seq_len):
    M = h.shape[0]
    tm = ROW_TILE
    nrow = seq_len // tm
    width = C_HEADS_PER_GROUP * HEAD_DIM
    row = lambda a: pl.BlockSpec((tm, a.shape[1]), lambda i: (i, 0))
    full = lambda a: pl.BlockSpec(a.shape, lambda i: (0, 0))
    views = [pl.BlockSpec((1, tm // d, d * width), lambda i: (i // nrow, i % nrow, 0)) for _, d in C_PATTERNS]
    n_spare = 2 * sum(d > 1 for _, d in C_PATTERNS)
    return pl.pallas_call(
        _out_proj_cd_kernel,
        grid=(M // tm,),
        in_specs=[row(h), *views, *views, row(od), full(wc), full(wd)],
        out_specs=pl.BlockSpec((tm, D_MODEL), lambda i: (i, 0)),
        out_shape=jax.ShapeDtypeStruct((M, D_MODEL), F32),
        scratch_shapes=[pltpu.VMEM((width // LANE, tm, LANE), F32)] * n_spare,
        compiler_params=_params("parallel"),
        name="out_proj1",
    )(h, *oc, *lses, od, wc, wd)


def _proj1_kernel(x_ref, g_ref, w_ref, cs_ref, c0_ref, c1_ref, c2_ref, y_s):
    xn = _rms(x_ref[...], g_ref[...]).astype(BF16)
    y = jnp.dot(xn, w_ref[...], preferred_element_type=F32) * cs_ref[...]
    tm = y.shape[0]
    width = y.shape[1] // len(C_PATTERNS)
    c0_ref[0] = y[:, :width].astype(BF16)
    nblk = width // LANE
    for j in range(y_s.shape[0]):
        y_s[j] = y[:, width + j * LANE:width + (j + 1) * LANE]
    for g, ref in ((1, c1_ref), (2, c2_ref)):
        dil = C_PATTERNS[g][1]
        for r in range(dil):
            for j in range(nblk):
                rows = y_s[(g - 1) * nblk + j, pl.ds(r, tm // dil, stride=dil), :]
                ref[0, :, r * width + j * LANE:r * width + (j + 1) * LANE] = rows.astype(BF16)


def proj1_call(x, g, w, colscale, batch, seq_len):
    M, K = x.shape
    N = w.shape[1]
    tm = ROW_TILE
    nrow = seq_len // tm
    width = N // len(C_PATTERNS)
    assert [d for _, d in C_PATTERNS][0] == 1
    out_specs = [pl.BlockSpec((1, tm // d, d * width), lambda i: (i // nrow, i % nrow, 0)) for _, d in C_PATTERNS]
    out_shape = [jax.ShapeDtypeStruct((batch, seq_len // d, d * width), BF16) for _, d in C_PATTERNS]
    return pl.pallas_call(
        _proj1_kernel,
        grid=(M // tm,),
        in_specs=[pl.BlockSpec((tm, K), lambda i: (i, 0)),
                  pl.BlockSpec((1, K), lambda i: (0, 0)),
                  pl.BlockSpec((K, N), lambda i: (0, 0)),
                  pl.BlockSpec((1, N), lambda i: (0, 0))],
        out_specs=out_specs,
        out_shape=out_shape,
        scratch_shapes=[pltpu.VMEM(((N - width) // LANE, tm, LANE), F32)],
        compiler_params=_params("parallel"),
        name="proj1",
    )(x, g.reshape(1, K), w, colscale.reshape(1, N))


def _ffn_kernel(*refs, n_mix, final_norm):
    x_ref, mix_refs = refs[0], refs[1:1 + 2 * n_mix]
    g_ref, wg_ref, wu_ref, wd_ref, fg_ref, o_ref, xn_s, acc_s, h_s = refs[1 + 2 * n_mix:]
    f = pl.program_id(1)

    @pl.when(f == 0)
    def _():
        h = x_ref[...]
        for a_ref, w_ref in zip(mix_refs[::2], mix_refs[1::2]):
            h = h + jnp.dot(a_ref[...], w_ref[...], preferred_element_type=F32)
        h_s[...] = h
        xn_s[...] = _rms(h, g_ref[...]).astype(BF16)
        acc_s[...] = jnp.zeros(acc_s.shape, F32)

    xn = xn_s[...]
    gate = jnp.dot(xn, wg_ref[0].astype(BF16), preferred_element_type=F32)
    up = jnp.dot(xn, wu_ref[0].astype(BF16), preferred_element_type=F32)
    mid = (gate / (1.0 + jnp.exp(-gate)) * up).astype(BF16)
    acc_s[...] += jnp.dot(mid, wd_ref[0].astype(BF16), preferred_element_type=F32)

    @pl.when(f == pl.num_programs(1) - 1)
    def _():
        y = h_s[...] + acc_s[...]
        if final_norm:
            y = _rms(y, fg_ref[...])
        o_ref[...] = y


def ffn(x, mix, g, wg, wu, wd, layer, fg, final_norm, name):
    M, K = x.shape
    tm, tf = FFN_ROW_TILE, FFN_COL_TILE
    kern = functools.partial(_ffn_kernel, n_mix=len(mix), final_norm=final_norm)
    mix_specs, mix_args = [], []
    for a, w in mix:
        mix_specs += [pl.BlockSpec((tm, a.shape[1]), lambda i, f: (i, 0)), pl.BlockSpec(w.shape, lambda i, f: (0, 0))]
        mix_args += [a, w]
    return pl.pallas_call(
        kern,
        grid=(M // tm, D_FF // tf),
        in_specs=[pl.BlockSpec((tm, K), lambda i, f: (i, 0)), *mix_specs,
                  pl.BlockSpec((1, K), lambda i, f: (0, 0)),
                  pl.BlockSpec((1, K, tf), lambda i, f: (layer, 0, f)),
                  pl.BlockSpec((1, K, tf), lambda i, f: (layer, 0, f)),
                  pl.BlockSpec((1, tf, K), lambda i, f: (layer, f, 0)),
                  pl.BlockSpec((1, K), lambda i, f: (0, 0))],
        out_specs=pl.BlockSpec((tm, K), lambda i, f: (i, 0)),
        out_shape=jax.ShapeDtypeStruct((M, K), F32),
        scratch_shapes=[pltpu.VMEM((tm, K), BF16), pltpu.VMEM((tm, K), F32), pltpu.VMEM((tm, K), F32)],
        compiler_params=_params("parallel", "arbitrary"),
        name=name,
    )(x, *mix_args, g.reshape(1, K), wg, wu, wd, fg.reshape(1, K))


def _trig_kernel(ang_ref, cos_ref, sin_ref):
    a = ang_ref[...]
    cos_ref[...] = jnp.cos(a)
    sin_ref[...] = jnp.sin(a)


def rope_tables(seq_len):
    half = D_ROPE // 2
    inv = ROPE_THETA ** (-jnp.arange(half, dtype=F32) / half)
    ang = jnp.arange(seq_len).astype(F32)[:, None] * inv[None, :]
    dense = ang.reshape(seq_len * half // LANE, LANE)
    spec = pl.BlockSpec(dense.shape, lambda: (0, 0))
    cos, sin = pl.pallas_call(
        _trig_kernel,
        in_specs=[spec],
        out_specs=[spec, spec],
        out_shape=[jax.ShapeDtypeStruct(dense.shape, F32)] * 2,
        name="rope_trig",
    )(dense)
    cos, sin = cos.reshape(seq_len, half), sin.reshape(seq_len, half)
    pad = LANE - D_NOPE - D_ROPE
    cos_l = jnp.concatenate([jnp.ones((seq_len, D_NOPE), F32), cos, cos, jnp.ones((seq_len, pad), F32)], axis=1)
    sin_l = jnp.concatenate([jnp.zeros((seq_len, D_NOPE), F32), sin, sin, jnp.zeros((seq_len, pad), F32)], axis=1)
    return cos_l, sin_l


def _prep_d_kernel(x_ref, g_ref, wa_ref, qn_ref, kvn_ref, wq_ref, wkv_ref, cos_ref, sin_ref,
                   q_ref, k_ref, v_ref, *, qscale):
    xn = _rms(x_ref[...], g_ref[...]).astype(BF16)
    lat = jnp.dot(xn, wa_ref[...], preferred_element_type=F32)
    cq = _rms(lat[:, :D_Q_LORA], qn_ref[...]).astype(BF16)
    ckv = _rms(lat[:, D_Q_LORA:D_Q_LORA + D_KV_LORA], kvn_ref[...]).astype(BF16)
    o_pe = D_Q_LORA + D_KV_LORA
    cos, sin = cos_ref[...], sin_ref[...]
    kpe = lat[:, o_pe:o_pe + LANE] * cos + lat[:, o_pe + LANE:o_pe + 2 * LANE] * sin
    qq = jnp.dot(cq, wq_ref[...], preferred_element_type=F32)
    kv = jnp.dot(ckv, wkv_ref[...], preferred_element_type=F32)
    kw = D_HEADS * LANE
    ones_blk = _ones_row_block(V_ROWS_D - D_V, x_ref.shape[0])
    for h in range(D_HEADS):
        qh = qq[:, h * LANE:(h + 1) * LANE] * cos + qq[:, kw + h * LANE:kw + (h + 1) * LANE] * sin
        q_ref[0, h, 0] = (qh * qscale).T.astype(BF16)
        k_ref[:, h * LANE:(h + 1) * LANE] = (kv[:, h * LANE:(h + 1) * LANE] + kpe).astype(BF16)
    for j in range(D_HEADS // 2):
        vt = kv[:, kw + j * LANE:kw + (j + 1) * LANE].T.astype(BF16)
        for half in range(2):
            v_ref[0, 2 * j + half, 0, :D_V] = vt[half * D_V:(half + 1) * D_V]
            v_ref[0, 2 * j + half, 0, D_V:] = ones_blk


def prep_d(x, g, wa, qn, kvn, wq, wkv, cos_l, sin_l, batch, seq_len):
    M, K = x.shape
    T = DENSE_TILE
    nrow = seq_len // T
    kw = D_HEADS * LANE
    kern = functools.partial(_prep_d_kernel, qscale=(D_NOPE + D_ROPE) ** -0.5 * LOG2E)
    full = lambda a: pl.BlockSpec(a.shape, lambda i: (0, 0))
    qn2, kvn2, g2 = qn.reshape(1, -1), kvn.reshape(1, -1), g.reshape(1, K)
    return pl.pallas_call(
        kern,
        grid=(M // T,),
        in_specs=[pl.BlockSpec((T, K), lambda i: (i, 0)), full(g2), full(wa), full(qn2), full(kvn2),
                  full(wq), full(wkv),
                  pl.BlockSpec((T, LANE), lambda i: (i % nrow, 0)),
                  pl.BlockSpec((T, LANE), lambda i: (i % nrow, 0))],
        out_specs=[pl.BlockSpec((1, D_HEADS, 1, LANE, T), lambda i: (i // nrow, 0, i % nrow, 0, 0)),
                   pl.BlockSpec((T, kw), lambda i: (i, 0)),
                   pl.BlockSpec((1, D_HEADS, 1, V_ROWS_D, T), lambda i: (i // nrow, 0, i % nrow, 0, 0))],
        out_shape=[jax.ShapeDtypeStruct((batch, D_HEADS, nrow, LANE, T), BF16),
                   jax.ShapeDtypeStruct((M, kw), BF16),
                   jax.ShapeDtypeStruct((batch, D_HEADS, nrow, V_ROWS_D, T), BF16)],
        compiler_params=_params("parallel"),
        name="prep_d",
    )(x, g2, wa, qn2, kvn2, wq, wkv, cos_l, sin_l)


B_HEAD_ORDER = (0, 4, 1, 5, 2, 6, 3, 7)


def _blocks(w, starts, width, axis):
    return jnp.concatenate([lax.slice_in_dim(w, s, s + width, axis=axis) for s in starts], axis=axis)


def _rot_partner_cols(w):
    half = D_ROPE // 2
    return jnp.concatenate([-w[..., half:], w[..., :half]], axis=-1)


def kernel(x, bias_table, attn_norm, ffn_norm, final_norm, ab_w_in, ab_lambda_q1, ab_lambda_k1,
           ab_lambda_q2, ab_lambda_k2, ab_subln, ab_sink, ab_w_o, cd_w_in, cd_q_norm, cd_w_q_b,
           cd_kv_norm, cd_w_kv_b, cd_w_o, ffn_w_gate, ffn_w_up, ffn_w_down):
    B, S, _ = x.shape
    M = B * S
    T = DENSE_TILE
    h = x.reshape(M, D_MODEL)
    qk_scale = HEAD_DIM ** -0.5 * LOG2E

    o3 = A_HEADS * (2 * A_QK_DIM + A_V_DIM)
    a0 = ab_w_in[0].astype(BF16)
    w0 = jnp.concatenate([a0[:, :o3], _blocks(a0, [o3 + hd * HEAD_DIM for hd in B_HEAD_ORDER], HEAD_DIM, 1),
                          a0[:, o3 + B_HEADS * HEAD_DIM:]], axis=1)
    cs0 = np.ones((AB_IN,), np.float32)
    cs0[:A_HEADS * A_QK_DIM] = qk_scale
    cs0[o3:o3 + B_HEADS * HEAD_DIM] = qk_scale
    qat, vat, qbt, vbt, keys0 = proj0_call(h, attn_norm[0], w0, jnp.asarray(cs0), B, S)
    keys0 = keys0.reshape(B, S, -1)
    bias_a = bias_tiles(bias_table, nvar=2 * BIAS_REACH + 1, nheads=A_HEADS, head0=0, rows=T, cols=T,
                        off0=-BIAS_REACH * T, off_step=T, row_coef=1, col_coef=-1, dil=1,
                        half_window=None, name="bias_a")
    oa = attn_a(qat, keys0, vat, bias_a, ab_lambda_q1[0], ab_lambda_k1[0], ab_lambda_q2[0],
                ab_lambda_k2[0], ab_subln[0], 0.8 - 0.6 * math.exp(-0.3 * 0))

    bias_b = bias_tiles(bias_table, nvar=3, nheads=B_HEADS, head0=A_HEADS, rows=QCOLS + 2 * B_HALF_WINDOW,
                        cols=QCOLS, off0=0, off_step=-B_HALF_WINDOW, row_coef=1, col_coef=-1, dil=1,
                        half_window=B_HALF_WINDOW, name="bias_b")
    ob = attn_b(qbt, keys0, (A_HEADS * A_QK_DIM) // LANE, vbt, bias_b, ab_sink[0])

    wo = ab_w_o[0].astype(BF16)
    wo_a = wo[:A_HEADS * A_V_DIM]
    wo_b = _blocks(wo, [A_HEADS * A_V_DIM + hd * HEAD_DIM for hd in B_HEAD_ORDER], HEAD_DIM, 0)
    h = ffn(h, [(oa.reshape(M, -1), wo_a), (ob.reshape(M, -1), wo_b)], ffn_norm[0], ffn_w_gate, ffn_w_up,
            ffn_w_down, 0, final_norm, False, "ffn0")

    w1 = cd_w_in[0]
    gw = C_HEADS_PER_GROUP * HEAD_DIM
    cw = C_HEADS * HEAD_DIM
    starts1 = [role * cw + g * gw for g in range(len(C_PATTERNS)) for role in range(3)]
    cs1 = np.ones((CD_C_IN,), np.float32)
    for g in range(len(C_PATTERNS)):
        cs1[3 * g * gw:(3 * g + 1) * gw] = qk_scale
    c_views = proj1_call(h, attn_norm[1], _blocks(w1.astype(BF16), starts1, gw, 1), jnp.asarray(cs1), B, S)

    oc, lses = [], []
    for g, (window, dil) in enumerate(C_PATTERNS):
        hw = window // (2 * dil)
        tq_c = 128
        bias_c = bias_tiles(bias_table, nvar=3, nheads=C_HEADS_PER_GROUP, head0=g * C_HEADS_PER_GROUP,
                            rows=tq_c, cols=tq_c + 2 * hw, off0=0, off_step=-hw, row_coef=-1, col_coef=1,
                            dil=dil, half_window=hw, name=f"bias_c{g}")
        o_g, lse_g = banded(c_views[g], bias_c, None, dil=dil, tq=tq_c, half_window=hw, nqb=2, nkb=2,
                            q_idx=lambda r: 3 * r, k_idx=lambda r: 3 * r + 1, v_idx=lambda r: 3 * r + 2,
                            head_of=lambda c, half: 2 * c + half, out_dtype=F32, has_lse=True,
                            name=f"attn_c{g}")
        oc.append(o_g)
        lses.append(lse_g)

    o_q, o_kv = CD_C_IN + D_Q_LORA, CD_C_IN + D_Q_LORA + D_KV_LORA
    w_pe = w1[:, o_kv:]
    lane_pad = lambda w: jnp.pad(w, ((0, 0), (D_NOPE, LANE - D_NOPE - D_ROPE)))
    wa = jnp.concatenate([w1[:, CD_C_IN:o_kv], lane_pad(w_pe), lane_pad(_rot_partner_cols(w_pe))],
                         axis=1).astype(BF16)
    wq3 = cd_w_q_b[0].reshape(D_Q_LORA, D_HEADS, D_NOPE + D_ROPE)
    zpad = jnp.zeros((D_Q_LORA, D_HEADS, LANE - D_NOPE - D_ROPE), F32)
    wq_main = jnp.concatenate([wq3, zpad], axis=-1)
    wq_rot = jnp.concatenate([jnp.zeros_like(wq3[..., :D_NOPE]), _rot_partner_cols(wq3[..., D_NOPE:]), zpad],
                             axis=-1)
    wq = jnp.concatenate([wq_main.reshape(D_Q_LORA, -1), wq_rot.reshape(D_Q_LORA, -1)], axis=1).astype(BF16)
    wkv3 = cd_w_kv_b[0].reshape(D_KV_LORA, D_HEADS, D_NOPE + D_V)
    wk = jnp.pad(wkv3[..., :D_NOPE], ((0, 0), (0, 0), (0, LANE - D_NOPE))).reshape(D_KV_LORA, -1)
    wv = wkv3[..., D_NOPE:].reshape(D_KV_LORA, -1)
    wkv = jnp.concatenate([wk, wv], axis=1).astype(BF16)
    cos_l, sin_l = rope_tables(S)
    qdt, kd, vdt = prep_d(h, attn_norm[1], wa, cd_q_norm[0], cd_kv_norm[0], wq, wkv, cos_l, sin_l, B, S)
    od = attn_d(qdt, kd.reshape(B, S, D_HEADS * LANE), vdt)

    wo1 = cd_w_o[0]
    wo_c = wo1[:C_HEADS_PER_GROUP * HEAD_DIM].astype(BF16)
    wo_d = wo1[C_HEADS_PER_GROUP * HEAD_DIM:].astype(BF16)
    h = out_proj_cd(h, oc, lses, od.reshape(M, -1), wo_c, wo_d, B, S)
    h = ffn(h, [], ffn_norm[1], ffn_w_gate, ffn_w_up, ffn_w_down, 1, final_norm, True, "ffn1")
    return h.reshape(B, S, D_MODEL)
```

```python
import functools
import math

import numpy as np
import jax
import jax.numpy as jnp
from jax import lax
from jax.experimental import pallas as pl
from jax.experimental.pallas import tpu as pltpu

F32 = jnp.float32
BF16 = jnp.bfloat16

D_MODEL = 1024
HEAD_DIM = 64
EPS = 1e-6
NEG = -1e30
LOG2E = math.log2(math.e)
LN2 = math.log(2.0)

A_HEADS = 4
A_QK_DIM = 2 * HEAD_DIM
A_V_DIM = 2 * HEAD_DIM
B_HEADS = 8
B_KV_HEADS = 2
B_HALF_WINDOW = 128
C_PATTERNS = ((128, 1), (512, 4), (2048, 16))
C_HEADS_PER_GROUP = 4
C_HEADS = C_HEADS_PER_GROUP * len(C_PATTERNS)
D_HEADS = 12
D_Q_LORA = 384
D_KV_LORA = 256
D_NOPE = 64
D_ROPE = 32
D_V = 64
ROPE_THETA = 10000.0
NUM_BUCKETS = 32
MAX_DISTANCE = 1024
D_FF = 2816
AB_IN = 2304
CD_C_IN = 3 * C_HEADS * HEAD_DIM

LANE = 128
VMEM_LIMIT = 48 * 1024 * 1024

ROW_TILE = 512
FFN_ROW_TILE = 1024
FFN_COL_TILE = 256
DENSE_TILE = 512
QCOLS = 256
UNROLL = 4
DENSE_QTILES = 4
A_PASSES = 2
D_PASSES = 4
BAND_GROUP = 8
BIAS_REACH = 3
V_ROWS_D = 80
V_ROWS_A = 144


def _bucket_thresholds():
    nb = NUM_BUCKETS // 2
    max_exact = nb // 2
    n = np.arange(1, 4 * MAX_DISTANCE)
    large = max_exact + (np.log(n.astype(np.float32) / np.float32(max_exact))
                         / np.float32(math.log(MAX_DISTANCE / max_exact))
                         * np.float32(nb - max_exact)).astype(np.int32)
    mag = np.where(n < max_exact, n, np.minimum(large, nb - 1))
    return tuple(int(n[np.argmax(mag >= k)]) for k in range(1, nb))


BUCKET_THRESHOLDS = _bucket_thresholds()
assert BUCKET_THRESHOLDS[-1] <= (BIAS_REACH - 1) * DENSE_TILE + 1


def _params(*sem):
    return pltpu.CompilerParams(dimension_semantics=sem, vmem_limit_bytes=VMEM_LIMIT)


def _rms(x, g):
    return x * lax.rsqrt(jnp.mean(x * x, axis=-1, keepdims=True) + EPS) * g


def _ones_row_block(rows, cols):
    r = lax.broadcasted_iota(jnp.int32, (rows, cols), 0)
    return jnp.where(r == 0, 1.0, 0.0).astype(BF16)


def _store_masked_halves(q_ref, idx, qt):
    zeros = jnp.zeros((HEAD_DIM, qt.shape[1]), BF16)
    q_ref[idx + (0, 0, slice(None, HEAD_DIM))] = qt[:HEAD_DIM]
    q_ref[idx + (0, 0, slice(HEAD_DIM, None))] = zeros
    q_ref[idx + (1, 0, slice(None, HEAD_DIM))] = zeros
    q_ref[idx + (1, 0, slice(HEAD_DIM, None))] = qt[HEAD_DIM:]


def _proj0_kernel(x_ref, g_ref, w_ref, cs_ref, qa_ref, va_ref, qb_ref, vb_ref, k_ref):
    xn = _rms(x_ref[...], g_ref[...]).astype(BF16)
    y = jnp.dot(xn, w_ref[...], preferred_element_type=F32) * cs_ref[...]
    tm = y.shape[0]
    ka0 = A_HEADS * A_QK_DIM
    va0 = 2 * ka0
    qb0 = va0 + A_HEADS * A_V_DIM
    kb0 = qb0 + B_HEADS * HEAD_DIM
    vb0 = kb0 + B_KV_HEADS * HEAD_DIM
    for h in range(A_HEADS):
        _store_masked_halves(qa_ref, (0, h), y[:, h * A_QK_DIM:(h + 1) * A_QK_DIM].T.astype(BF16))
        va_ref[0, h, 0, :A_V_DIM] = y[:, va0 + h * A_V_DIM:va0 + (h + 1) * A_V_DIM].T.astype(BF16)
        va_ref[0, h, 0, A_V_DIM:] = _ones_row_block(V_ROWS_A - A_V_DIM, tm)
    for j in range(B_HEADS // 2):
        _store_masked_halves(qb_ref, (0, j), y[:, qb0 + j * LANE:qb0 + (j + 1) * LANE].T.astype(BF16))
    vbt = y[:, vb0:vb0 + LANE].T.astype(BF16)
    ones_blk = _ones_row_block(V_ROWS_D - HEAD_DIM, LANE)
    for g in range(B_KV_HEADS):
        for c in range(tm // LANE):
            vb_ref[0, g, c, :HEAD_DIM] = vbt[g * HEAD_DIM:(g + 1) * HEAD_DIM, c * LANE:(c + 1) * LANE]
            vb_ref[0, g, c, HEAD_DIM:] = ones_blk
    k_ref[:, :ka0] = y[:, ka0:va0].astype(BF16)
    k_ref[:, ka0:] = y[:, kb0:vb0].astype(BF16)


def proj0_call(x, g, w, colscale, batch, seq_len):
    M, K = x.shape
    N = w.shape[1]
    T = DENSE_TILE
    nrow = seq_len // T
    nkb = A_HEADS * A_QK_DIM + B_KV_HEADS * HEAD_DIM
    qspec = pl.BlockSpec((1, A_HEADS, 2, 1, LANE, T), lambda i: (i // nrow, 0, 0, i % nrow, 0, 0))
    qshape = jax.ShapeDtypeStruct((batch, A_HEADS, 2, nrow, LANE, T), BF16)
    return pl.pallas_call(
        _proj0_kernel,
        grid=(M // T,),
        in_specs=[pl.BlockSpec((T, K), lambda i: (i, 0)),
                  pl.BlockSpec((1, K), lambda i: (0, 0)),
                  pl.BlockSpec((K, N), lambda i: (0, 0)),
                  pl.BlockSpec((1, N), lambda i: (0, 0))],
        out_specs=[qspec,
                   pl.BlockSpec((1, A_HEADS, 1, V_ROWS_A, T), lambda i: (i // nrow, 0, i % nrow, 0, 0)),
                   qspec,
                   pl.BlockSpec((1, B_KV_HEADS, T // LANE, V_ROWS_D, LANE),
                                lambda i: (i // nrow, 0, i % nrow, 0, 0)),
                   pl.BlockSpec((T, nkb), lambda i: (i, 0))],
        out_shape=[qshape,
                   jax.ShapeDtypeStruct((batch, A_HEADS, nrow, V_ROWS_A, T), BF16),
                   qshape,
                   jax.ShapeDtypeStruct((batch, B_KV_HEADS, seq_len // LANE, V_ROWS_D, LANE), BF16),
                   jax.ShapeDtypeStruct((M, nkb), BF16)],
        compiler_params=_params("parallel"),
        name="proj0",
    )(x, g.reshape(1, K), w, colscale.reshape(1, N))


def _bias_kernel(tab_ref, o_ref, *, nvar, off0, off_step, row_coef, col_coef, dil, half_window, head0):
    hcol = head0 + pl.program_id(1)
    R, C = o_ref.shape[-2:]
    row = lax.broadcasted_iota(jnp.int32, (R, C), 0)
    col = lax.broadcasted_iota(jnp.int32, (R, C), 1)
    base = row_coef * row + col_coef * col
    span_lo = min(row_coef * (R - 1), 0) + min(col_coef * (C - 1), 0)
    span_hi = max(row_coef * (R - 1), 0) + max(col_coef * (C - 1), 0)
    nb = NUM_BUCKETS // 2

    def side(n, n_lo, n_hi, row0):
        val = jnp.full((R, C), tab_ref[row0 + sum(t <= n_lo for t in BUCKET_THRESHOLDS), hcol], F32)
        for k, thr in enumerate(BUCKET_THRESHOLDS, start=1):
            if n_lo < thr <= n_hi:
                val = jnp.where(n >= thr, tab_ref[row0 + k, hcol], val)
        return val

    for v in range(nvar):
        @pl.when(pl.program_id(0) == v)
        def _(v=v):
            off = off0 + v * off_step
            rel = off + base
            lo, hi = (off + span_lo) * dil, (off + span_hi) * dil
            dist = rel * dil
            n = jnp.abs(dist)
            if hi <= 0:
                val = side(n, -hi, -lo, 0)
            elif lo > 0:
                val = side(n, lo, hi, nb)
            else:
                val = jnp.where(dist > 0, side(n, 1, hi, nb), side(n, 0, -lo, 0))
            val = val * LOG2E
            if half_window is not None:
                val = jnp.where(jnp.abs(rel) <= half_window, val, NEG)
            o_ref[0, 0] = val


def bias_tiles(table, *, nvar, nheads, head0, rows, cols, off0, off_step, row_coef, col_coef,
               dil, half_window, name):
    kern = functools.partial(_bias_kernel, nvar=nvar, off0=off0, off_step=off_step, row_coef=row_coef,
                             col_coef=col_coef, dil=dil, half_window=half_window, head0=head0)
    return pl.pallas_call(
        kern,
        grid=(nvar, nheads),
        in_specs=[pl.BlockSpec(memory_space=pltpu.SMEM)],
        out_specs=pl.BlockSpec((1, 1, rows, cols), lambda v, h: (v, h, 0, 0)),
        out_shape=jax.ShapeDtypeStruct((nvar, nheads, rows, cols), F32),
        compiler_params=_params("parallel", "parallel"),
        name=name,
    )(table)


def _dense_pipeline(nk, tile, n_streams, score_fn, value_fn, m_s, acc_s, bufs):
    units = [(i, c * QCOLS) for i in range(n_streams) for c in range(tile // QCOLS)]

    def produce(kc, unit, nxt):
        i, c0 = unit
        cols = slice(c0, c0 + QCOLS)
        s = score_fn(kc + 1, i, cols)
        nxt[0][i, :, cols] = s
        nxt[1][i, :, cols] = jnp.max(s, axis=0, keepdims=True)

    def consume(kc, unit, cur):
        i, c0 = unit
        cols = slice(c0, c0 + QCOLS)
        m_old = m_s[i, :, cols]
        m_new = jnp.maximum(m_old, cur[1][i, :, cols])
        alpha = jnp.exp2(m_old - m_new)
        p = jnp.exp2((cur[0][i, :, cols] - m_new).astype(BF16))
        acc_s[i, :, cols] = (alpha * acc_s[i, :, cols]
                             + jnp.dot(value_fn(kc, i), p, preferred_element_type=F32))
        m_s[i, :, cols] = m_new

    def stage(kc, cur, nxt):
        for unit in units:
            if nxt is not None:
                produce(kc, unit, nxt)
            if cur is not None:
                consume(kc, unit, cur)

    m_s[...] = jnp.full(m_s.shape, NEG, F32)
    acc_s[...] = jnp.zeros(acc_s.shape, F32)
    stage(-1, None, bufs[0])
    n_loop = (nk - 1) // UNROLL

    def body(j, carry):
        for u in range(UNROLL):
            stage(UNROLL * j + u, bufs[u % 2], bufs[(u + 1) % 2])
        return carry

    lax.fori_loop(0, n_loop, body, 0)
    for kc in range(n_loop * UNROLL, nk):
        stage(kc, bufs[kc % 2], bufs[(kc + 1) % 2] if kc < nk - 1 else None)


def _attn_a_kernel(q_ref, k_ref, v_ref, bias_ref, lq1_ref, lk1_ref, lq2_ref, lk2_ref, subln_ref, o_ref,
                   m_s, acc_s, s_a, s_b, cm_a, cm_b, *, tile, nk, qtiles, npass, lambda_init):
    lam = (jnp.exp(jnp.sum(lq1_ref[...] * lk1_ref[...], axis=-1, keepdims=True))
           - jnp.exp(jnp.sum(lq2_ref[...] * lk2_ref[...], axis=-1, keepdims=True)) + lambda_init)

    def one_pass(p, carry):
        q0 = p * qtiles
        qi0 = pl.program_id(2) * (qtiles * npass) + q0

        def score_fn(kc, i, cols):
            j, qt = divmod(i, qtiles)
            kblk = k_ref[0, pl.ds(pl.multiple_of(kc * tile, tile), tile), :]
            bt = bias_ref[jnp.clip(kc - (qi0 + qt), -BIAS_REACH, BIAS_REACH) + BIAS_REACH, 0, :, cols]
            return jnp.dot(kblk, q_ref[0, 0, j, q0 + qt, :, cols], preferred_element_type=F32) + bt

        def value_fn(kc, i):
            return v_ref[0, 0, kc]

        _dense_pipeline(nk, tile, 2 * qtiles, score_fn, value_fn, m_s, acc_s, ((s_a, cm_a), (s_b, cm_b)))
        for qt in range(qtiles):
            a1, a2 = acc_s[qt], acc_s[qtiles + qt]
            o = (a1[:A_V_DIM] / a1[A_V_DIM:A_V_DIM + 1]
                 - lam * (a2[:A_V_DIM] / a2[A_V_DIM:A_V_DIM + 1]))
            ms = jnp.mean(o * o, axis=0, keepdims=True)
            y = o * lax.rsqrt(ms + EPS) * subln_ref[...] * (1.0 - lambda_init)
            rows = pl.ds(pl.multiple_of((q0 + qt) * tile, tile), tile)
            o_ref[0, rows] = y.T.astype(o_ref.dtype)
        return carry

    lax.fori_loop(0, npass, one_pass, 0)


def attn_a(qt, karr, vt, bias, lq1, lk1, lq2, lk2, subln, lambda_init):
    B, S, _ = karr.shape
    T = DENSE_TILE
    nq = nk = S // T
    assert nk % 2 == 0 and nk >= 4
    qtiles = math.gcd(nq, DENSE_QTILES)
    npass = math.gcd(nq // qtiles, A_PASSES)
    ns = 2 * qtiles
    kern = functools.partial(_attn_a_kernel, tile=T, nk=nk, qtiles=qtiles, npass=npass, lambda_init=lambda_init)
    vec = lambda n: pl.BlockSpec((1, n), lambda h, b, i: (0, 0))
    return pl.pallas_call(
        kern,
        grid=(A_HEADS, B, nq // (qtiles * npass)),
        in_specs=[pl.BlockSpec((1, 1, 2, qtiles * npass, LANE, T), lambda h, b, i: (b, h, 0, i, 0, 0)),
                  pl.BlockSpec((1, S, LANE), lambda h, b, i: (b, 0, h)),
                  pl.BlockSpec((1, 1, nk, V_ROWS_A, T), lambda h, b, i: (b, h, 0, 0, 0)),
                  pl.BlockSpec((2 * BIAS_REACH + 1, 1, T, T), lambda h, b, i: (0, h, 0, 0),
                               pipeline_mode=pl.Buffered(1)),
                  vec(HEAD_DIM), vec(HEAD_DIM), vec(HEAD_DIM), vec(HEAD_DIM),
                  pl.BlockSpec((A_V_DIM, 1), lambda h, b, i: (0, 0))],
        out_specs=pl.BlockSpec((1, qtiles * npass * T, LANE), lambda h, b, i: (b, i, h)),
        out_shape=jax.ShapeDtypeStruct((B, S, A_HEADS * A_V_DIM), BF16),
        scratch_shapes=[pltpu.VMEM((ns, 1, T), F32), pltpu.VMEM((ns, V_ROWS_A, T), F32),
                        pltpu.VMEM((ns, T, T), F32), pltpu.VMEM((ns, T, T), F32),
                        pltpu.VMEM((ns, 1, T), F32), pltpu.VMEM((ns, 1, T), F32)],
        compiler_params=_params("parallel", "parallel", "arbitrary"),
        name="attn_a",
    )(qt, karr, vt, bias, lq1.reshape(1, -1), lk1.reshape(1, -1), lq2.reshape(1, -1),
      lk2.reshape(1, -1), subln.reshape(-1, 1))


def _attn_d_kernel(q_ref, k_ref, v_ref, o_ref, m_s, acc_s, s_a, s_b, cm_a, cm_b, *, tile, nk, qtiles, npass):
    def one_pass(p, carry):
        q0 = p * qtiles

        def score_fn(kc, i, cols):
            hh, qt = divmod(i, qtiles)
            kblk = k_ref[0, pl.ds(pl.multiple_of(kc * tile, tile), tile), hh * LANE:(hh + 1) * LANE]
            return jnp.dot(kblk, q_ref[0, hh, q0 + qt, :, cols], preferred_element_type=F32)

        def value_fn(kc, i):
            return v_ref[0, i // qtiles, kc]

        _dense_pipeline(nk, tile, 2 * qtiles, score_fn, value_fn, m_s, acc_s, ((s_a, cm_a), (s_b, cm_b)))
        for qt in range(qtiles):
            outs = []
            for hh in range(2):
                acc = acc_s[hh * qtiles + qt]
                outs.append(acc[:D_V] / acc[D_V:D_V + 1])
            rows = pl.ds(pl.multiple_of((q0 + qt) * tile, tile), tile)
            o_ref[0, rows] = jnp.concatenate(outs, axis=0).T.astype(o_ref.dtype)
        return carry

    lax.fori_loop(0, npass, one_pass, 0)


def attn_d(qt, k, vt):
    B, S, _ = k.shape
    T = DENSE_TILE
    nq = nk = S // T
    qtiles = math.gcd(nq, DENSE_QTILES)
    npass = math.gcd(nq // qtiles, D_PASSES)
    ns = 2 * qtiles
    kern = functools.partial(_attn_d_kernel, tile=T, nk=nk, qtiles=qtiles, npass=npass)
    return pl.pallas_call(
        kern,
        grid=(B, D_HEADS // 2, nq // (qtiles * npass)),
        in_specs=[pl.BlockSpec((1, 2, qtiles * npass, LANE, T), lambda b, h, i: (b, h, i, 0, 0)),
                  pl.BlockSpec((1, S, 2 * LANE), lambda b, h, i: (b, 0, h)),
                  pl.BlockSpec((1, 2, nk, V_ROWS_D, T), lambda b, h, i: (b, h, 0, 0, 0))],
        out_specs=pl.BlockSpec((1, qtiles * npass * T, LANE), lambda b, h, i: (b, i, h)),
        out_shape=jax.ShapeDtypeStruct((B, S, D_HEADS * D_V), BF16),
        scratch_shapes=[pltpu.VMEM((ns, 1, T), F32), pltpu.VMEM((ns, V_ROWS_D, T), F32),
                        pltpu.VMEM((ns, T, T), F32), pltpu.VMEM((ns, T, T), F32),
                        pltpu.VMEM((ns, 1, T), F32), pltpu.VMEM((ns, 1, T), F32)],
        compiler_params=_params("parallel", "parallel", "arbitrary"),
        name="attn_d",
    )(qt, k, vt)


def _attn_b_kernel(q_ref, k_ref, v_ref, bias_ref, sink_ref, o_ref, s_a, s_b, s_c, cm_a, cm_b, cm_c, *, tile, seq_len):
    half_window = B_HALF_WINDOW
    span = QCOLS + 2 * half_window
    nchunk = span // LANE
    step = pl.program_id(1)

    def window(c):
        q0 = step * tile + c * QCOLS
        start = pl.multiple_of(jnp.clip(q0 - half_window, 0, seq_len - span), LANE)
        variant = jnp.where(q0 == 0, 0, jnp.where(q0 + QCOLS == seq_len, 2, 1))
        return start, variant

    def scores(task):
        c, j, g = task
        start, variant = window(c)
        kwin = k_ref[0, pl.ds(start, span), :]
        s = jnp.dot(kwin, q_ref[0, j, g, 0, :, c * QCOLS:(c + 1) * QCOLS], preferred_element_type=F32)
        return s + bias_ref[variant, j + (B_HEADS // 2) * g]

    def finish(task, s_ref, cm_ref):
        c, j, g = task
        head = j + (B_HEADS // 2) * g
        start, _ = window(c)
        chunk0 = start // LANE
        vwin = jnp.concatenate([v_ref[0, g, chunk0 + n] for n in range(nchunk)], axis=1)
        sk = sink_ref[:, head:head + 1] * LOG2E
        m = jnp.maximum(cm_ref[...], sk)
        e = jnp.exp2((s_ref[...] - m).astype(BF16))
        ov = jnp.dot(vwin, e, preferred_element_type=F32)
        denom = ov[HEAD_DIM:HEAD_DIM + 1] + jnp.exp2(sk - m)
        return ov[:HEAD_DIM] / denom

    tasks = [(c, j, g) for c in range(tile // QCOLS) for j in range(B_HEADS // 2) for g in range(B_KV_HEADS)]
    bufs = ((s_a, cm_a), (s_b, cm_b), (s_c, cm_c))

    def produce(task, buf):
        s = scores(task)
        buf[0][...] = s
        buf[1][...] = jnp.max(s, axis=0, keepdims=True)

    produce(tasks[0], bufs[0])
    produce(tasks[1], bufs[1])
    held = None
    for n, task in enumerate(tasks):
        if n + 2 < len(tasks):
            produce(tasks[n + 2], bufs[(n + 2) % 3])
        o = finish(task, *bufs[n % 3])
        c, j, g = task
        if g == 0:
            held = o
            continue
        pair = jnp.concatenate([held, o], axis=0).T
        o_ref[0, c * QCOLS:(c + 1) * QCOLS, j * LANE:(j + 1) * LANE] = pair.astype(o_ref.dtype)


def attn_b(qt, karr, k_block, vt, bias, sink):
    B, S, _ = karr.shape
    T = DENSE_TILE
    span = QCOLS + 2 * B_HALF_WINDOW
    return pl.pallas_call(
        functools.partial(_attn_b_kernel, tile=T, seq_len=S),
        grid=(B, S // T),
        in_specs=[pl.BlockSpec((1, B_HEADS // 2, 2, 1, LANE, T), lambda b, i: (b, 0, 0, i, 0, 0)),
                  pl.BlockSpec((1, S, LANE), lambda b, i: (b, 0, k_block)),
                  pl.BlockSpec((1, B_KV_HEADS, S // LANE, V_ROWS_D, LANE), lambda b, i: (b, 0, 0, 0, 0)),
                  pl.BlockSpec(bias.shape, lambda b, i: (0, 0, 0, 0)),
                  pl.BlockSpec((1, B_HEADS), lambda b, i: (0, 0))],
        out_specs=pl.BlockSpec((1, T, B_HEADS * HEAD_DIM), lambda b, i: (b, i, 0)),
        out_shape=jax.ShapeDtypeStruct((B, S, B_HEADS * HEAD_DIM), BF16),
        scratch_shapes=[pltpu.VMEM((span, QCOLS), F32)] * 3 + [pltpu.VMEM((1, QCOLS), F32)] * 3,
        compiler_params=_params("parallel", "arbitrary"),
        name="attn_b",
    )(qt, karr, vt, bias, sink.reshape(1, -1))


def _banded_kernel(*refs, tq, group, span, half_window, seq_len, nqb, nkb, head_of, has_sink, has_lse):
    it = iter(refs)
    q_ref, k_ref, v_ref, bias_ref = next(it), next(it), next(it), next(it)
    sink_ref = next(it) if has_sink else None
    o_ref = next(it)
    lse_ref = next(it) if has_lse else None

    nq = seq_len // tq
    lane = lax.broadcasted_iota(jnp.int32, (tq, LANE), 1)
    low = lane < HEAD_DIM

    def window(g):
        t = pl.program_id(2) * group + g
        start = pl.multiple_of(jnp.clip(t * tq - half_window, 0, seq_len - span), half_window)
        variant = jnp.where(t == 0, 0, jnp.where(t == nq - 1, 2, 1))
        return start, variant

    def scores(task):
        g, c, half = task
        start, variant = window(g)
        kc = c if nkb == nqb else 0
        q2 = q_ref[0, g * tq:(g + 1) * tq, c * LANE:(c + 1) * LANE]
        kb = k_ref[0, pl.ds(start, span), kc * LANE:(kc + 1) * LANE]
        qm = jnp.where(low if half == 0 else jnp.logical_not(low), q2, jnp.zeros_like(q2))
        s = lax.dot_general(qm, kb, (((1,), (1,)), ((), ())), preferred_element_type=F32)
        return s + bias_ref[variant, head_of(c, half)]

    def finish(task, s):
        g, c, half = task
        start, _ = window(g)
        kc = c if nkb == nqb else 0
        hidx = head_of(c, half)
        vb = v_ref[0, pl.ds(start, span), kc * LANE:(kc + 1) * LANE]
        m = jnp.max(s, axis=-1, keepdims=True)
        if has_sink:
            sk = sink_ref[:, hidx:hidx + 1] * LOG2E
            m = jnp.maximum(m, sk)
        e = jnp.exp2(s - m)
        denom = jnp.sum(e, axis=-1, keepdims=True)
        if has_sink:
            denom = denom + jnp.exp2(sk - m)
        o = jnp.dot(e.astype(BF16), vb, preferred_element_type=F32) / denom
        return o, (LN2 * m + jnp.log(denom) if has_lse else None)

    tasks = [(g, c, half) for g in range(group) for c in range(nqb) for half in range(2)]
    s_next = scores(tasks[0])
    held = None
    for n, task in enumerate(tasks):
        s = s_next
        if n + 1 < len(tasks):
            s_next = scores(tasks[n + 1])
        o, lse = finish(task, s)
        g, c, half = task
        if half == 0:
            held = (o, lse)
            continue
        rows, cols = slice(g * tq, (g + 1) * tq), slice(c * LANE, (c + 1) * LANE)
        o_ref[0, rows, cols] = jnp.where(low, held[0], o).astype(o_ref.dtype)
        if has_lse:
            lse_ref[0, rows, cols] = jnp.where(low, held[1], lse)


def banded(view, bias, sink, *, dil, tq, half_window, nqb, nkb, q_idx, k_idx, v_idx, head_of,
           out_dtype, has_lse, name):
    B, L, _ = view.shape
    span = tq + 2 * half_window
    nq = L // tq
    assert L % tq == 0 and L >= span and nq >= 2
    group = math.gcd(nq, BAND_GROUP)
    OW = nqb * LANE
    has_sink = sink is not None
    kern = functools.partial(_banded_kernel, tq=tq, group=group, span=span, half_window=half_window,
                             seq_len=L, nqb=nqb, nkb=nkb, head_of=head_of, has_sink=has_sink,
                             has_lse=has_lse)
    in_specs = [pl.BlockSpec((1, group * tq, OW), lambda b, r, t: (b, t, q_idx(r))),
                pl.BlockSpec((1, L, nkb * LANE), lambda b, r, t: (b, 0, k_idx(r))),
                pl.BlockSpec((1, L, nkb * LANE), lambda b, r, t: (b, 0, v_idx(r))),
                pl.BlockSpec(bias.shape, lambda b, r, t: (0, 0, 0, 0))]
    args = [view, view, view, bias]
    if has_sink:
        in_specs.append(pl.BlockSpec((1, sink.shape[-1]), lambda b, r, t: (0, 0)))
        args.append(sink.reshape(1, -1))
    out_spec = pl.BlockSpec((1, group * tq, OW), lambda b, r, t: (b, t, r))
    out_shapes = [jax.ShapeDtypeStruct((B, L, dil * OW), out_dtype)]
    out_specs = [out_spec]
    if has_lse:
        out_shapes.append(jax.ShapeDtypeStruct((B, L, dil * OW), F32))
        out_specs.append(out_spec)
    outs = pl.pallas_call(
        kern,
        grid=(B, dil, nq // group),
        in_specs=in_specs,
        out_specs=out_specs,
        out_shape=out_shapes,
        compiler_params=_params("parallel", "parallel", "arbitrary"),
        name=name,
    )(*args)
    return outs


def _out_proj_cd_kernel(h_ref, o0, o1, o2, s0, s1, s2, d_ref, wc_ref, wd_ref, o_ref, *scratch):
    tm = h_ref.shape[0]
    width = C_HEADS_PER_GROUP * HEAD_DIM
    spare = iter(scratch)

    def in_position_order(ref, dil):
        if dil == 1:
            return ref[0]
        t_s = next(spare)
        for r in range(dil):
            for j in range(width // LANE):
                t_s[j, pl.ds(r, tm // dil, stride=dil), :] = ref[0, :, r * width + j * LANE:r * width + (j + 1) * LANE]
        return jnp.concatenate([t_s[j] for j in range(width // LANE)], axis=1)

    dils = [d for _, d in C_PATTERNS]
    outs = [in_position_order(r, d) for r, d in zip((o0, o1, o2), dils)]
    lses = [in_position_order(r, d) for r, d in zip((s0, s1, s2), dils)]
    mx = jnp.maximum(jnp.maximum(lses[0], lses[1]), lses[2])
    es = [jnp.exp(l - mx) for l in lses]
    oc = (es[0] * outs[0] + es[1] * outs[1] + es[2] * outs[2]) / (es[0] + es[1] + es[2])
    mix = jnp.dot(oc.astype(BF16), wc_ref[...], preferred_element_type=F32)
    mix = mix + jnp.dot(d_ref[...], wd_ref[...], preferred_element_type=F32)
    o_ref[...] = h_ref[...] + mix


def out_proj_cd(h, oc, lses, od, wc, wd, batch, seq_len):
    M = h.shape[0]
    tm = ROW_TILE
    nrow = seq_len // tm
    width = C_HEADS_PER_GROUP * HEAD_DIM
    row = lambda a: pl.BlockSpec((tm, a.shape[1]), lambda i: (i, 0))
    full = lambda a: pl.BlockSpec(a.shape, lambda i: (0, 0))
    views = [pl.BlockSpec((1, tm // d, d * width), lambda i: (i // nrow, i % nrow, 0)) for _, d in C_PATTERNS]
    n_spare = 2 * sum(d > 1 for _, d in C_PATTERNS)
    return pl.pallas_call(
        _out_proj_cd_kernel,
        grid=(M // tm,),
        in_specs=[row(h), *views, *views, row(od), full(wc), full(wd)],
        out_specs=pl.BlockSpec((tm, D_MODEL), lambda i: (i, 0)),
        out_shape=jax.ShapeDtypeStruct((M, D_MODEL), F32),
        scratch_shapes=[pltpu.VMEM((width // LANE, tm, LANE), F32)] * n_spare,
        compiler_params=_params("parallel"),
        name="out_proj1",
    )(h, *oc, *lses, od, wc, wd)


def _proj1_kernel(x_ref, g_ref, w_ref, cs_ref, c0_ref, c1_ref, c2_ref, y_s):
    xn = _rms(x_ref[...], g_ref[...]).astype(BF16)
    y = jnp.dot(xn, w_ref[...], preferred_element_type=F32) * cs_ref[...]
    tm = y.shape[0]
    width = y.shape[1] // len(C_PATTERNS)
    c0_ref[0] = y[:, :width].astype(BF16)
    nblk = width // LANE
    for j in range(y_s.shape[0]):
        y_s[j] = y[:, width + j * LANE:width + (j + 1) * LANE]
    for g, ref in ((1, c1_ref), (2, c2_ref)):
        dil = C_PATTERNS[g][1]
        for r in range(dil):
            for j in range(nblk):
                rows = y_s[(g - 1) * nblk + j, pl.ds(r, tm // dil, stride=dil), :]
                ref[0, :, r * width + j * LANE:r * width + (j + 1) * LANE] = rows.astype(BF16)


def proj1_call(x, g, w, colscale, batch, seq_len):
    M, K = x.shape
    N = w.shape[1]
    tm = ROW_TILE
    nrow = seq_len // tm
    width = N // len(C_PATTERNS)
    assert [d for _, d in C_PATTERNS][0] == 1
    out_specs = [pl.BlockSpec((1, tm // d, d * width), lambda i: (i // nrow, i % nrow, 0)) for _, d in C_PATTERNS]
    out_shape = [jax.ShapeDtypeStruct((batch, seq_len // d, d * width), BF16) for _, d in C_PATTERNS]
    return pl.pallas_call(
        _proj1_kernel,
        grid=(M // tm,),
        in_specs=[pl.BlockSpec((tm, K), lambda i: (i, 0)),
                  pl.BlockSpec((1, K), lambda i: (0, 0)),
                  pl.BlockSpec((K, N), lambda i: (0, 0)),
                  pl.BlockSpec((1, N), lambda i: (0, 0))],
        out_specs=out_specs,
        out_shape=out_shape,
        scratch_shapes=[pltpu.VMEM(((N - width) // LANE, tm, LANE), F32)],
        compiler_params=_params("parallel"),
        name="proj1",
    )(x, g.reshape(1, K), w, colscale.reshape(1, N))


def _ffn_kernel(*refs, n_mix, final_norm):
    x_ref, mix_refs = refs[0], refs[1:1 + 2 * n_mix]
    g_ref, wg_ref, wu_ref, wd_ref, fg_ref, o_ref, xn_s, acc_s, h_s = refs[1 + 2 * n_mix:]
    f = pl.program_id(1)

    @pl.when(f == 0)
    def _():
        h = x_ref[...]
        for a_ref, w_ref in zip(mix_refs[::2], mix_refs[1::2]):
            h = h + jnp.dot(a_ref[...], w_ref[...], preferred_element_type=F32)
        h_s[...] = h
        xn_s[...] = _rms(h, g_ref[...]).astype(BF16)

    xn = xn_s[...]
    gate = jnp.dot(xn, wg_ref[0].astype(BF16), preferred_element_type=F32)
    up = jnp.dot(xn, wu_ref[0].astype(BF16), preferred_element_type=F32)
    mid = (gate / (1.0 + jnp.exp(-gate)) * up).astype(BF16)
    wd = wd_ref[0].astype(BF16)
    last = pl.num_programs(1) - 1

    @pl.when(f == 0)
    def _():
        acc_s[...] = jnp.dot(mid, wd, preferred_element_type=F32)

    @pl.when(jnp.logical_and(f > 0, f < last))
    def _():
        acc_s[...] += jnp.dot(mid, wd, preferred_element_type=F32)

    @pl.when(f == last)
    def _():
        y = h_s[...] + acc_s[...] + jnp.dot(mid, wd, preferred_element_type=F32)
        if final_norm:
            y = _rms(y, fg_ref[...])
        o_ref[...] = y


def ffn(x, mix, g, wg, wu, wd, layer, fg, final_norm, name):
    M, K = x.shape
    tm, tf = FFN_ROW_TILE, FFN_COL_TILE
    kern = functools.partial(_ffn_kernel, n_mix=len(mix), final_norm=final_norm)
    mix_specs, mix_args = [], []
    for a, w in mix:
        mix_specs += [pl.BlockSpec((tm, a.shape[1]), lambda i, f: (i, 0)), pl.BlockSpec(w.shape, lambda i, f: (0, 0))]
        mix_args += [a, w]
    return pl.pallas_call(
        kern,
        grid=(M // tm, D_FF // tf),
        in_specs=[pl.BlockSpec((tm, K), lambda i, f: (i, 0)), *mix_specs,
                  pl.BlockSpec((1, K), lambda i, f: (0, 0)),
                  pl.BlockSpec((1, K, tf), lambda i, f: (layer, 0, f)),
                  pl.BlockSpec((1, K, tf), lambda i, f: (layer, 0, f)),
                  pl.BlockSpec((1, tf, K), lambda i, f: (layer, f, 0)),
                  pl.BlockSpec((1, K), lambda i, f: (0, 0))],
        out_specs=pl.BlockSpec((tm, K), lambda i, f: (i, 0)),
        out_shape=jax.ShapeDtypeStruct((M, K), F32),
        scratch_shapes=[pltpu.VMEM((tm, K), BF16), pltpu.VMEM((tm, K), F32), pltpu.VMEM((tm, K), F32)],
        compiler_params=_params("parallel", "arbitrary"),
        name=name,
    )(x, *mix_args, g.reshape(1, K), wg, wu, wd, fg.reshape(1, K))


def _trig_kernel(ang_ref, cos_ref, sin_ref):
    a = ang_ref[...]
    cos_ref[...] = jnp.cos(a)
    sin_ref[...] = jnp.sin(a)


def rope_tables(seq_len):
    half = D_ROPE // 2
    inv = ROPE_THETA ** (-jnp.arange(half, dtype=F32) / half)
    ang = jnp.arange(seq_len).astype(F32)[:, None] * inv[None, :]
    dense = ang.reshape(seq_len * half // LANE, LANE)
    spec = pl.BlockSpec(dense.shape, lambda: (0, 0))
    cos, sin = pl.pallas_call(
        _trig_kernel,
        in_specs=[spec],
        out_specs=[spec, spec],
        out_shape=[jax.ShapeDtypeStruct(dense.shape, F32)] * 2,
        name="rope_trig",
    )(dense)
    cos, sin = cos.reshape(seq_len, half), sin.reshape(seq_len, half)
    pad = LANE - D_NOPE - D_ROPE
    cos_l = jnp.concatenate([jnp.ones((seq_len, D_NOPE), F32), cos, cos, jnp.ones((seq_len, pad), F32)], axis=1)
    sin_l = jnp.concatenate([jnp.zeros((seq_len, D_NOPE), F32), sin, sin, jnp.zeros((seq_len, pad), F32)], axis=1)
    return cos_l, sin_l


def _prep_d_kernel(x_ref, g_ref, wa_ref, qn_ref, kvn_ref, wq_ref, wkv_ref, cos_ref, sin_ref,
                   q_ref, k_ref, v_ref, *, qscale):
    xn = _rms(x_ref[...], g_ref[...]).astype(BF16)
    lat = jnp.dot(xn, wa_ref[...], preferred_element_type=F32)
    cq = _rms(lat[:, :D_Q_LORA], qn_ref[...]).astype(BF16)
    ckv = _rms(lat[:, D_Q_LORA:D_Q_LORA + D_KV_LORA], kvn_ref[...]).astype(BF16)
    o_pe = D_Q_LORA + D_KV_LORA
    cos, sin = cos_ref[...], sin_ref[...]
    kpe = lat[:, o_pe:o_pe + LANE] * cos + lat[:, o_pe + LANE:o_pe + 2 * LANE] * sin
    qq = jnp.dot(cq, wq_ref[...], preferred_element_type=F32)
    kv = jnp.dot(ckv, wkv_ref[...], preferred_element_type=F32)
    kw = D_HEADS * LANE
    ones_blk = _ones_row_block(V_ROWS_D - D_V, x_ref.shape[0])
    for h in range(D_HEADS):
        qh = qq[:, h * LANE:(h + 1) * LANE] * cos + qq[:, kw + h * LANE:kw + (h + 1) * LANE] * sin
        q_ref[0, h, 0] = (qh * qscale).T.astype(BF16)
        k_ref[:, h * LANE:(h + 1) * LANE] = (kv[:, h * LANE:(h + 1) * LANE] + kpe).astype(BF16)
    for j in range(D_HEADS // 2):
        vt = kv[:, kw + j * LANE:kw + (j + 1) * LANE].T.astype(BF16)
        for half in range(2):
            v_ref[0, 2 * j + half, 0, :D_V] = vt[half * D_V:(half + 1) * D_V]
            v_ref[0, 2 * j + half, 0, D_V:] = ones_blk


def prep_d(x, g, wa, qn, kvn, wq, wkv, cos_l, sin_l, batch, seq_len):
    M, K = x.shape
    T = DENSE_TILE
    nrow = seq_len // T
    kw = D_HEADS * LANE
    kern = functools.partial(_prep_d_kernel, qscale=(D_NOPE + D_ROPE) ** -0.5 * LOG2E)
    full = lambda a: pl.BlockSpec(a.shape, lambda i: (0, 0))
    qn2, kvn2, g2 = qn.reshape(1, -1), kvn.reshape(1, -1), g.reshape(1, K)
    return pl.pallas_call(
        kern,
        grid=(M // T,),
        in_specs=[pl.BlockSpec((T, K), lambda i: (i, 0)), full(g2), full(wa), full(qn2), full(kvn2),
                  full(wq), full(wkv),
                  pl.BlockSpec((T, LANE), lambda i: (i % nrow, 0)),
                  pl.BlockSpec((T, LANE), lambda i: (i % nrow, 0))],
        out_specs=[pl.BlockSpec((1, D_HEADS, 1, LANE, T), lambda i: (i // nrow, 0, i % nrow, 0, 0)),
                   pl.BlockSpec((T, kw), lambda i: (i, 0)),
                   pl.BlockSpec((1, D_HEADS, 1, V_ROWS_D, T), lambda i: (i // nrow, 0, i % nrow, 0, 0))],
        out_shape=[jax.ShapeDtypeStruct((batch, D_HEADS, nrow, LANE, T), BF16),
                   jax.ShapeDtypeStruct((M, kw), BF16),
                   jax.ShapeDtypeStruct((batch, D_HEADS, nrow, V_ROWS_D, T), BF16)],
        compiler_params=_params("parallel"),
        name="prep_d",
    )(x, g2, wa, qn2, kvn2, wq, wkv, cos_l, sin_l)


B_HEAD_ORDER = (0, 4, 1, 5, 2, 6, 3, 7)


def _blocks(w, starts, width, axis):
    return jnp.concatenate([lax.slice_in_dim(w, s, s + width, axis=axis) for s in starts], axis=axis)


def _rot_partner_cols(w):
    half = D_ROPE // 2
    return jnp.concatenate([-w[..., half:], w[..., :half]], axis=-1)


def kernel(x, bias_table, attn_norm, ffn_norm, final_norm, ab_w_in, ab_lambda_q1, ab_lambda_k1,
           ab_lambda_q2, ab_lambda_k2, ab_subln, ab_sink, ab_w_o, cd_w_in, cd_q_norm, cd_w_q_b,
           cd_kv_norm, cd_w_kv_b, cd_w_o, ffn_w_gate, ffn_w_up, ffn_w_down):
    B, S, _ = x.shape
    M = B * S
    T = DENSE_TILE
    h = x.reshape(M, D_MODEL)
    qk_scale = HEAD_DIM ** -0.5 * LOG2E

    o3 = A_HEADS * (2 * A_QK_DIM + A_V_DIM)
    a0 = ab_w_in[0].astype(BF16)
    w0 = jnp.concatenate([a0[:, :o3], _blocks(a0, [o3 + hd * HEAD_DIM for hd in B_HEAD_ORDER], HEAD_DIM, 1),
                          a0[:, o3 + B_HEADS * HEAD_DIM:]], axis=1)
    cs0 = np.ones((AB_IN,), np.float32)
    cs0[:A_HEADS * A_QK_DIM] = qk_scale
    cs0[o3:o3 + B_HEADS * HEAD_DIM] = qk_scale
    qat, vat, qbt, vbt, keys0 = proj0_call(h, attn_norm[0], w0, jnp.asarray(cs0), B, S)
    keys0 = keys0.reshape(B, S, -1)
    bias_a = bias_tiles(bias_table, nvar=2 * BIAS_REACH + 1, nheads=A_HEADS, head0=0, rows=T, cols=T,
                        off0=-BIAS_REACH * T, off_step=T, row_coef=1, col_coef=-1, dil=1,
                        half_window=None, name="bias_a")
    oa = attn_a(qat, keys0, vat, bias_a, ab_lambda_q1[0], ab_lambda_k1[0], ab_lambda_q2[0],
                ab_lambda_k2[0], ab_subln[0], 0.8 - 0.6 * math.exp(-0.3 * 0))

    bias_b = bias_tiles(bias_table, nvar=3, nheads=B_HEADS, head0=A_HEADS, rows=QCOLS + 2 * B_HALF_WINDOW,
                        cols=QCOLS, off0=0, off_step=-B_HALF_WINDOW, row_coef=1, col_coef=-1, dil=1,
                        half_window=B_HALF_WINDOW, name="bias_b")
    ob = attn_b(qbt, keys0, (A_HEADS * A_QK_DIM) // LANE, vbt, bias_b, ab_sink[0])

    wo = ab_w_o[0].astype(BF16)
    wo_a = wo[:A_HEADS * A_V_DIM]
    wo_b = _blocks(wo, [A_HEADS * A_V_DIM + hd * HEAD_DIM for hd in B_HEAD_ORDER], HEAD_DIM, 0)
    h = ffn(h, [(oa.reshape(M, -1), wo_a), (ob.reshape(M, -1), wo_b)], ffn_norm[0], ffn_w_gate, ffn_w_up,
            ffn_w_down, 0, final_norm, False, "ffn0")

    w1 = cd_w_in[0]
    gw = C_HEADS_PER_GROUP * HEAD_DIM
    cw = C_HEADS * HEAD_DIM
    starts1 = [role * cw + g * gw for g in range(len(C_PATTERNS)) for role in range(3)]
    cs1 = np.ones((CD_C_IN,), np.float32)
    for g in range(len(C_PATTERNS)):
        cs1[3 * g * gw:(3 * g + 1) * gw] = qk_scale
    c_views = proj1_call(h, attn_norm[1], _blocks(w1.astype(BF16), starts1, gw, 1), jnp.asarray(cs1), B, S)

    oc, lses = [], []
    for g, (window, dil) in enumerate(C_PATTERNS):
        hw = window // (2 * dil)
        tq_c = 128
        bias_c = bias_tiles(bias_table, nvar=3, nheads=C_HEADS_PER_GROUP, head0=g * C_HEADS_PER_GROUP,
                            rows=tq_c, cols=tq_c + 2 * hw, off0=0, off_step=-hw, row_coef=-1, col_coef=1,
                            dil=dil, half_window=hw, name=f"bias_c{g}")
        o_g, lse_g = banded(c_views[g], bias_c, None, dil=dil, tq=tq_c, half_window=hw, nqb=2, nkb=2,
                            q_idx=lambda r: 3 * r, k_idx=lambda r: 3 * r + 1, v_idx=lambda r: 3 * r + 2,
                            head_of=lambda c, half: 2 * c + half, out_dtype=F32, has_lse=True,
                            name=f"attn_c{g}")
        oc.append(o_g)
        lses.append(lse_g)

    o_q, o_kv = CD_C_IN + D_Q_LORA, CD_C_IN + D_Q_LORA + D_KV_LORA
    w_pe = w1[:, o_kv:]
    lane_pad = lambda w: jnp.pad(w, ((0, 0), (D_NOPE, LANE - D_NOPE - D_ROPE)))
    wa = jnp.concatenate([w1[:, CD_C_IN:o_kv], lane_pad(w_pe), lane_pad(_rot_partner_cols(w_pe))],
                         axis=1).astype(BF16)
    wq3 = cd_w_q_b[0].reshape(D_Q_LORA, D_HEADS, D_NOPE + D_ROPE)
    zpad = jnp.zeros((D_Q_LORA, D_HEADS, LANE - D_NOPE - D_ROPE), F32)
    wq_main = jnp.concatenate([wq3, zpad], axis=-1)
    wq_rot = jnp.concatenate([jnp.zeros_like(wq3[..., :D_NOPE]), _rot_partner_cols(wq3[..., D_NOPE:]), zpad],
                             axis=-1)
    wq = jnp.concatenate([wq_main.reshape(D_Q_LORA, -1), wq_rot.reshape(D_Q_LORA, -1)], axis=1).astype(BF16)
    wkv3 = cd_w_kv_b[0].reshape(D_KV_LORA, D_HEADS, D_NOPE + D_V)
    wk = jnp.pad(wkv3[..., :D_NOPE], ((0, 0), (0, 0), (0, LANE - D_NOPE))).reshape(D_KV_LORA, -1)
    wv = wkv3[..., D_NOPE:].reshape(D_KV_LORA, -1)
    wkv = jnp.concatenate([wk, wv], axis=1).astype(BF16)
    cos_l, sin_l = rope_tables(S)
    qdt, kd, vdt = prep_d(h, attn_norm[1], wa, cd_q_norm[0], cd_kv_norm[0], wq, wkv, cos_l, sin_l, B, S)
    od = attn_d(qdt, kd.reshape(B, S, D_HEADS * LANE), vdt)

    wo1 = cd_w_o[0]
    wo_c = wo1[:C_HEADS_PER_GROUP * HEAD_DIM].astype(BF16)
    wo_d = wo1[C_HEADS_PER_GROUP * HEAD_DIM:].astype(BF16)
    h = out_proj_cd(h, oc, lses, od.reshape(M, -1), wo_c, wo_d, B, S)
    h = ffn(h, [], ffn_norm[1], ffn_w_gate, ffn_w_up, ffn_w_down, 1, final_norm, True, "ffn1")
    return h.reshape(B, S, D_MODEL)
```

```python
import functools
import math

import numpy as np
import jax
import jax.numpy as jnp
from jax import lax
from jax.experimental import pallas as pl
from jax.experimental.pallas import tpu as pltpu

F32 = jnp.float32
BF16 = jnp.bfloat16

D_MODEL = 1024
HEAD_DIM = 64
EPS = 1e-6
NEG = -1e30
LOG2E = math.log2(math.e)
LN2 = math.log(2.0)

A_HEADS = 4
A_QK_DIM = 2 * HEAD_DIM
A_V_DIM = 2 * HEAD_DIM
B_HEADS = 8
B_KV_HEADS = 2
B_HALF_WINDOW = 128
C_PATTERNS = ((128, 1), (512, 4), (2048, 16))
C_HEADS_PER_GROUP = 4
C_HEADS = C_HEADS_PER_GROUP * len(C_PATTERNS)
D_HEADS = 12
D_Q_LORA = 384
D_KV_LORA = 256
D_NOPE = 64
D_ROPE = 32
D_V = 64
ROPE_THETA = 10000.0
NUM_BUCKETS = 32
MAX_DISTANCE = 1024
D_FF = 2816
AB_IN = 2304
CD_C_IN = 3 * C_HEADS * HEAD_DIM

LANE = 128
VMEM_LIMIT = 48 * 1024 * 1024

ROW_TILE = 512
FFN_ROW_TILE = 1024
FFN_COL_TILE = 256
DENSE_TILE = 512
QCOLS = 256
UNROLL = 4
DENSE_QTILES = 4
A_PASSES = 2
D_PASSES = 4
BAND_GROUP = 8
BIAS_REACH = 3
V_ROWS_D = 80
V_ROWS_A = 144


def _bucket_thresholds():
    nb = NUM_BUCKETS // 2
    max_exact = nb // 2
    n = np.arange(1, 4 * MAX_DISTANCE)
    large = max_exact + (np.log(n.astype(np.float32) / np.float32(max_exact))
                         / np.float32(math.log(MAX_DISTANCE / max_exact))
                         * np.float32(nb - max_exact)).astype(np.int32)
    mag = np.where(n < max_exact, n, np.minimum(large, nb - 1))
    return tuple(int(n[np.argmax(mag >= k)]) for k in range(1, nb))


BUCKET_THRESHOLDS = _bucket_thresholds()
assert BUCKET_THRESHOLDS[-1] <= (BIAS_REACH - 1) * DENSE_TILE + 1


def _params(*sem):
    return pltpu.CompilerParams(dimension_semantics=sem, vmem_limit_bytes=VMEM_LIMIT)


def _rms(x, g):
    return x * lax.rsqrt(jnp.mean(x * x, axis=-1, keepdims=True) + EPS) * g


def _ones_row_block(rows, cols):
    r = lax.broadcasted_iota(jnp.int32, (rows, cols), 0)
    return jnp.where(r == 0, 1.0, 0.0).astype(BF16)


def _store_masked_halves(q_ref, idx, qt):
    zeros = jnp.zeros((HEAD_DIM, qt.shape[1]), BF16)
    q_ref[idx + (0, 0, slice(None, HEAD_DIM))] = qt[:HEAD_DIM]
    q_ref[idx + (0, 0, slice(HEAD_DIM, None))] = zeros
    q_ref[idx + (1, 0, slice(None, HEAD_DIM))] = zeros
    q_ref[idx + (1, 0, slice(HEAD_DIM, None))] = qt[HEAD_DIM:]


def _proj0_kernel(x_ref, g_ref, w_ref, cs_ref, qa_ref, va_ref, qb_ref, vb_ref, k_ref):
    xn = _rms(x_ref[...], g_ref[...]).astype(BF16)
    y = jnp.dot(xn, w_ref[...], preferred_element_type=F32) * cs_ref[...]
    tm = y.shape[0]
    ka0 = A_HEADS * A_QK_DIM
    va0 = 2 * ka0
    qb0 = va0 + A_HEADS * A_V_DIM
    kb0 = qb0 + B_HEADS * HEAD_DIM
    vb0 = kb0 + B_KV_HEADS * HEAD_DIM
    for h in range(A_HEADS):
        _store_masked_halves(qa_ref, (0, h), y[:, h * A_QK_DIM:(h + 1) * A_QK_DIM].T.astype(BF16))
        va_ref[0, h, 0, :A_V_DIM] = y[:, va0 + h * A_V_DIM:va0 + (h + 1) * A_V_DIM].T.astype(BF16)
        va_ref[0, h, 0, A_V_DIM:] = _ones_row_block(V_ROWS_A - A_V_DIM, tm)
    for j in range(B_HEADS // 2):
        _store_masked_halves(qb_ref, (0, j), y[:, qb0 + j * LANE:qb0 + (j + 1) * LANE].T.astype(BF16))
    vbt = y[:, vb0:vb0 + LANE].T.astype(BF16)
    ones_blk = _ones_row_block(V_ROWS_D - HEAD_DIM, LANE)
    for g in range(B_KV_HEADS):
        for c in range(tm // LANE):
            vb_ref[0, g, c, :HEAD_DIM] = vbt[g * HEAD_DIM:(g + 1) * HEAD_DIM, c * LANE:(c + 1) * LANE]
            vb_ref[0, g, c, HEAD_DIM:] = ones_blk
    k_ref[:, :ka0] = y[:, ka0:va0].astype(BF16)
    k_ref[:, ka0:] = y[:, kb0:vb0].astype(BF16)


def proj0_call(x, g, w, colscale, batch, seq_len):
    M, K = x.shape
    N = w.shape[1]
    T = DENSE_TILE
    nrow = seq_len // T
    nkb = A_HEADS * A_QK_DIM + B_KV_HEADS * HEAD_DIM
    qspec = pl.BlockSpec((1, A_HEADS, 2, 1, LANE, T), lambda i: (i // nrow, 0, 0, i % nrow, 0, 0))
    qshape = jax.ShapeDtypeStruct((batch, A_HEADS, 2, nrow, LANE, T), BF16)
    return pl.pallas_call(
        _proj0_kernel,
        grid=(M // T,),
        in_specs=[pl.BlockSpec((T, K), lambda i: (i, 0)),
                  pl.BlockSpec((1, K), lambda i: (0, 0)),
                  pl.BlockSpec((K, N), lambda i: (0, 0)),
                  pl.BlockSpec((1, N), lambda i: (0, 0))],
        out_specs=[qspec,
                   pl.BlockSpec((1, A_HEADS, 1, V_ROWS_A, T), lambda i: (i // nrow, 0, i % nrow, 0, 0)),
                   qspec,
                   pl.BlockSpec((1, B_KV_HEADS, T // LANE, V_ROWS_D, LANE),
                                lambda i: (i // nrow, 0, i % nrow, 0, 0)),
                   pl.BlockSpec((T, nkb), lambda i: (i, 0))],
        out_shape=[qshape,
                   jax.ShapeDtypeStruct((batch, A_HEADS, nrow, V_ROWS_A, T), BF16),
                   qshape,
                   jax.ShapeDtypeStruct((batch, B_KV_HEADS, seq_len // LANE, V_ROWS_D, LANE), BF16),
                   jax.ShapeDtypeStruct((M, nkb), BF16)],
        compiler_params=_params("parallel"),
        name="proj0",
    )(x, g.reshape(1, K), w, colscale.reshape(1, N))


def _bias_kernel(tab_ref, o_ref, *, nvar, off0, off_step, row_coef, col_coef, dil, half_window, head0):
    hcol = head0 + pl.program_id(1)
    R, C = o_ref.shape[-2:]
    row = lax.broadcasted_iota(jnp.int32, (R, C), 0)
    col = lax.broadcasted_iota(jnp.int32, (R, C), 1)
    base = row_coef * row + col_coef * col
    span_lo = min(row_coef * (R - 1), 0) + min(col_coef * (C - 1), 0)
    span_hi = max(row_coef * (R - 1), 0) + max(col_coef * (C - 1), 0)
    nb = NUM_BUCKETS // 2

    def side(n, n_lo, n_hi, row0):
        val = jnp.full((R, C), tab_ref[row0 + sum(t <= n_lo for t in BUCKET_THRESHOLDS), hcol], F32)
        for k, thr in enumerate(BUCKET_THRESHOLDS, start=1):
            if n_lo < thr <= n_hi:
                val = jnp.where(n >= thr, tab_ref[row0 + k, hcol], val)
        return val

    for v in range(nvar):
        @pl.when(pl.program_id(0) == v)
        def _(v=v):
            off = off0 + v * off_step
            rel = off + base
            lo, hi = (off + span_lo) * dil, (off + span_hi) * dil
            dist = rel * dil
            n = jnp.abs(dist)
            if hi <= 0:
                val = side(n, -hi, -lo, 0)
            elif lo > 0:
                val = side(n, lo, hi, nb)
            else:
                val = jnp.where(dist > 0, side(n, 1, hi, nb), side(n, 0, -lo, 0))
            val = val * LOG2E
            if half_window is not None:
                val = jnp.where(jnp.abs(rel) <= half_window, val, NEG)
            o_ref[0, 0] = val


def bias_tiles(table, *, nvar, nheads, head0, rows, cols, off0, off_step, row_coef, col_coef,
               dil, half_window, name):
    kern = functools.partial(_bias_kernel, nvar=nvar, off0=off0, off_step=off_step, row_coef=row_coef,
                             col_coef=col_coef, dil=dil, half_window=half_window, head0=head0)
    return pl.pallas_call(
        kern,
        grid=(nvar, nheads),
        in_specs=[pl.BlockSpec(memory_space=pltpu.SMEM)],
        out_specs=pl.BlockSpec((1, 1, rows, cols), lambda v, h: (v, h, 0, 0)),
        out_shape=jax.ShapeDtypeStruct((nvar, nheads, rows, cols), F32),
        compiler_params=_params("parallel", "parallel"),
        name=name,
    )(table)


def _dense_pipeline(nk, tile, n_streams, score_fn, value_fn, m_s, acc_s, bufs):
    units = [(i, c * QCOLS) for i in range(n_streams) for c in range(tile // QCOLS)]

    def produce(kc, unit, nxt):
        i, c0 = unit
        cols = slice(c0, c0 + QCOLS)
        s = score_fn(kc + 1, i, cols)
        nxt[0][i, :, cols] = s
        nxt[1][i, :, cols] = jnp.max(s, axis=0, keepdims=True)

    def consume(kc, unit, cur):
        i, c0 = unit
        cols = slice(c0, c0 + QCOLS)
        m_old = m_s[i, :, cols]
        m_new = jnp.maximum(m_old, cur[1][i, :, cols])
        alpha = jnp.exp2(m_old - m_new)
        p = jnp.exp2((cur[0][i, :, cols] - m_new).astype(BF16))
        acc_s[i, :, cols] = (alpha * acc_s[i, :, cols]
                             + jnp.dot(value_fn(kc, i), p, preferred_element_type=F32))
        m_s[i, :, cols] = m_new

    def stage(kc, cur, nxt):
        for unit in units:
            if nxt is not None:
                produce(kc, unit, nxt)
            if cur is not None:
                consume(kc, unit, cur)

    m_s[...] = jnp.full(m_s.shape, NEG, F32)
    acc_s[...] = jnp.zeros(acc_s.shape, F32)
    stage(-1, None, bufs[0])
    n_loop = (nk - 1) // UNROLL

    def body(j, carry):
        for u in range(UNROLL):
            stage(UNROLL * j + u, bufs[u % 2], bufs[(u + 1) % 2])
        return carry

    lax.fori_loop(0, n_loop, body, 0)
    for kc in range(n_loop * UNROLL, nk):
        stage(kc, bufs[kc % 2], bufs[(kc + 1) % 2] if kc < nk - 1 else None)


def _attn_a_kernel(q_ref, k_ref, v_ref, bias_ref, lq1_ref, lk1_ref, lq2_ref, lk2_ref, subln_ref, o_ref,
                   m_s, acc_s, s_a, s_b, cm_a, cm_b, *, tile, nk, qtiles, npass, lambda_init):
    lam = (jnp.exp(jnp.sum(lq1_ref[...] * lk1_ref[...], axis=-1, keepdims=True))
           - jnp.exp(jnp.sum(lq2_ref[...] * lk2_ref[...], axis=-1, keepdims=True)) + lambda_init)

    def one_pass(p, carry):
        q0 = p * qtiles
        qi0 = pl.program_id(2) * (qtiles * npass) + q0

        def score_fn(kc, i, cols):
            j, qt = divmod(i, qtiles)
            kblk = k_ref[0, pl.ds(pl.multiple_of(kc * tile, tile), tile), :]
            bt = bias_ref[jnp.clip(kc - (qi0 + qt), -BIAS_REACH, BIAS_REACH) + BIAS_REACH, 0, :, cols]
            return jnp.dot(kblk, q_ref[0, 0, j, q0 + qt, :, cols], preferred_element_type=F32) + bt

        def value_fn(kc, i):
            return v_ref[0, 0, kc]

        _dense_pipeline(nk, tile, 2 * qtiles, score_fn, value_fn, m_s, acc_s, ((s_a, cm_a), (s_b, cm_b)))
        for qt in range(qtiles):
            a1, a2 = acc_s[qt], acc_s[qtiles + qt]
            o = (a1[:A_V_DIM] / a1[A_V_DIM:A_V_DIM + 1]
                 - lam * (a2[:A_V_DIM] / a2[A_V_DIM:A_V_DIM + 1]))
            ms = jnp.mean(o * o, axis=0, keepdims=True)
            y = o * lax.rsqrt(ms + EPS) * subln_ref[...] * (1.0 - lambda_init)
            rows = pl.ds(pl.multiple_of((q0 + qt) * tile, tile), tile)
            o_ref[0, rows] = y.T.astype(o_ref.dtype)
        return carry

    lax.fori_loop(0, npass, one_pass, 0)


def attn_a(qt, karr, vt, bias, lq1, lk1, lq2, lk2, subln, lambda_init):
    B, S, _ = karr.shape
    T = DENSE_TILE
    nq = nk = S // T
    assert nk % 2 == 0 and nk >= 4
    qtiles = math.gcd(nq, DENSE_QTILES)
    npass = math.gcd(nq // qtiles, A_PASSES)
    ns = 2 * qtiles
    kern = functools.partial(_attn_a_kernel, tile=T, nk=nk, qtiles=qtiles, npass=npass, lambda_init=lambda_init)
    vec = lambda n: pl.BlockSpec((1, n), lambda h, b, i: (0, 0))
    return pl.pallas_call(
        kern,
        grid=(A_HEADS, B, nq // (qtiles * npass)),
        in_specs=[pl.BlockSpec((1, 1, 2, qtiles * npass, LANE, T), lambda h, b, i: (b, h, 0, i, 0, 0)),
                  pl.BlockSpec((1, S, LANE), lambda h, b, i: (b, 0, h)),
                  pl.BlockSpec((1, 1, nk, V_ROWS_A, T), lambda h, b, i: (b, h, 0, 0, 0)),
                  pl.BlockSpec((2 * BIAS_REACH + 1, 1, T, T), lambda h, b, i: (0, h, 0, 0),
                               pipeline_mode=pl.Buffered(1)),
                  vec(HEAD_DIM), vec(HEAD_DIM), vec(HEAD_DIM), vec(HEAD_DIM),
                  pl.BlockSpec((A_V_DIM, 1), lambda h, b, i: (0, 0))],
        out_specs=pl.BlockSpec((1, qtiles * npass * T, LANE), lambda h, b, i: (b, i, h)),
        out_shape=jax.ShapeDtypeStruct((B, S, A_HEADS * A_V_DIM), BF16),
        scratch_shapes=[pltpu.VMEM((ns, 1, T), F32), pltpu.VMEM((ns, V_ROWS_A, T), F32),
                        pltpu.VMEM((ns, T, T), F32), pltpu.VMEM((ns, T, T), F32),
                        pltpu.VMEM((ns, 1, T), F32), pltpu.VMEM((ns, 1, T), F32)],
        compiler_params=_params("parallel", "parallel", "arbitrary"),
        name="attn_a",
    )(qt, karr, vt, bias, lq1.reshape(1, -1), lk1.reshape(1, -1), lq2.reshape(1, -1),
      lk2.reshape(1, -1), subln.reshape(-1, 1))


def _attn_d_kernel(q_ref, k_ref, v_ref, o_ref, m_s, acc_s, s_a, s_b, cm_a, cm_b, *, tile, nk, qtiles, npass):
    def one_pass(p, carry):
        q0 = p * qtiles

        def score_fn(kc, i, cols):
            hh, qt = divmod(i, qtiles)
            kblk = k_ref[0, pl.ds(pl.multiple_of(kc * tile, tile), tile), hh * LANE:(hh + 1) * LANE]
            return jnp.dot(kblk, q_ref[0, hh, q0 + qt, :, cols], preferred_element_type=F32)

        def value_fn(kc, i):
            return v_ref[0, i // qtiles, kc]

        _dense_pipeline(nk, tile, 2 * qtiles, score_fn, value_fn, m_s, acc_s, ((s_a, cm_a), (s_b, cm_b)))
        for qt in range(qtiles):
            outs = []
            for hh in range(2):
                acc = acc_s[hh * qtiles + qt]
                outs.append(acc[:D_V] / acc[D_V:D_V + 1])
            rows = pl.ds(pl.multiple_of((q0 + qt) * tile, tile), tile)
            o_ref[0, rows] = jnp.concatenate(outs, axis=0).T.astype(o_ref.dtype)
        return carry

    lax.fori_loop(0, npass, one_pass, 0)


def attn_d(qt, k, vt):
    B, S, _ = k.shape
    T = DENSE_TILE
    nq = nk = S // T
    qtiles = math.gcd(nq, DENSE_QTILES)
    npass = math.gcd(nq // qtiles, D_PASSES)
    ns = 2 * qtiles
    kern = functools.partial(_attn_d_kernel, tile=T, nk=nk, qtiles=qtiles, npass=npass)
    return pl.pallas_call(
        kern,
        grid=(B, D_HEADS // 2, nq // (qtiles * npass)),
        in_specs=[pl.BlockSpec((1, 2, qtiles * npass, LANE, T), lambda b, h, i: (b, h, i, 0, 0)),
                  pl.BlockSpec((1, S, 2 * LANE), lambda b, h, i: (b, 0, h)),
                  pl.BlockSpec((1, 2, nk, V_ROWS_D, T), lambda b, h, i: (b, h, 0, 0, 0))],
        out_specs=pl.BlockSpec((1, qtiles * npass * T, LANE), lambda b, h, i: (b, i, h)),
        out_shape=jax.ShapeDtypeStruct((B, S, D_HEADS * D_V), BF16),
        scratch_shapes=[pltpu.VMEM((ns, 1, T), F32), pltpu.VMEM((ns, V_ROWS_D, T), F32),
                        pltpu.VMEM((ns, T, T), F32), pltpu.VMEM((ns, T, T), F32),
                        pltpu.VMEM((ns, 1, T), F32), pltpu.VMEM((ns, 1, T), F32)],
        compiler_params=_params("parallel", "parallel", "arbitrary"),
        name="attn_d",
    )(qt, k, vt)


def _attn_b_kernel(q_ref, k_ref, v_ref, bias_ref, sink_ref, o_ref, s_a, s_b, s_c, cm_a, cm_b, cm_c, *, tile, seq_len):
    half_window = B_HALF_WINDOW
    span = QCOLS + 2 * half_window
    nchunk = span // LANE
    step = pl.program_id(1)

    def window(c):
        q0 = step * tile + c * QCOLS
        start = pl.multiple_of(jnp.clip(q0 - half_window, 0, seq_len - span), LANE)
        variant = jnp.where(q0 == 0, 0, jnp.where(q0 + QCOLS == seq_len, 2, 1))
        return start, variant

    def scores(task):
        c, j, g = task
        start, variant = window(c)
        kwin = k_ref[0, pl.ds(start, span), :]
        s = jnp.dot(kwin, q_ref[0, j, g, 0, :, c * QCOLS:(c + 1) * QCOLS], preferred_element_type=F32)
        return s + bias_ref[variant, j + (B_HEADS // 2) * g]

    def finish(task, s_ref, cm_ref):
        c, j, g = task
        head = j + (B_HEADS // 2) * g
        start, _ = window(c)
        chunk0 = start // LANE
        vwin = jnp.concatenate([v_ref[0, g, chunk0 + n] for n in range(nchunk)], axis=1)
        sk = sink_ref[:, head:head + 1] * LOG2E
        m = jnp.maximum(cm_ref[...], sk)
        e = jnp.exp2((s_ref[...] - m).astype(BF16))
        ov = jnp.dot(vwin, e, preferred_element_type=F32)
        denom = ov[HEAD_DIM:HEAD_DIM + 1] + jnp.exp2(sk - m)
        return ov[:HEAD_DIM] / denom

    tasks = [(c, j, g) for c in range(tile // QCOLS) for j in range(B_HEADS // 2) for g in range(B_KV_HEADS)]
    bufs = ((s_a, cm_a), (s_b, cm_b), (s_c, cm_c))

    def produce(task, buf):
        s = scores(task)
        buf[0][...] = s
        buf[1][...] = jnp.max(s, axis=0, keepdims=True)

    produce(tasks[0], bufs[0])
    produce(tasks[1], bufs[1])
    held = None
    for n, task in enumerate(tasks):
        if n + 2 < len(tasks):
            produce(tasks[n + 2], bufs[(n + 2) % 3])
        o = finish(task, *bufs[n % 3])
        c, j, g = task
        if g == 0:
            held = o
            continue
        pair = jnp.concatenate([held, o], axis=0).T
        o_ref[0, c * QCOLS:(c + 1) * QCOLS, j * LANE:(j + 1) * LANE] = pair.astype(o_ref.dtype)


def attn_b(qt, karr, k_block, vt, bias, sink):
    B, S, _ = karr.shape
    T = DENSE_TILE
    span = QCOLS + 2 * B_HALF_WINDOW
    return pl.pallas_call(
        functools.partial(_attn_b_kernel, tile=T, seq_len=S),
        grid=(B, S // T),
        in_specs=[pl.BlockSpec((1, B_HEADS // 2, 2, 1, LANE, T), lambda b, i: (b, 0, 0, i, 0, 0)),
                  pl.BlockSpec((1, S, LANE), lambda b, i: (b, 0, k_block)),
                  pl.BlockSpec((1, B_KV_HEADS, S // LANE, V_ROWS_D, LANE), lambda b, i: (b, 0, 0, 0, 0)),
                  pl.BlockSpec(bias.shape, lambda b, i: (0, 0, 0, 0)),
                  pl.BlockSpec((1, B_HEADS), lambda b, i: (0, 0))],
        out_specs=pl.BlockSpec((1, T, B_HEADS * HEAD_DIM), lambda b, i: (b, i, 0)),
        out_shape=jax.ShapeDtypeStruct((B, S, B_HEADS * HEAD_DIM), BF16),
        scratch_shapes=[pltpu.VMEM((span, QCOLS), F32)] * 3 + [pltpu.VMEM((1, QCOLS), F32)] * 3,
        compiler_params=_params("parallel", "arbitrary"),
        name="attn_b",
    )(qt, karr, vt, bias, sink.reshape(1, -1))


def _attn_c_kernel(q_ref, k_ref, v_ref, bias_ref, o_ref, lse_ref, *, tq, group, span, half_window, seq_len):
    nq = seq_len // tq
    nblk = q_ref.shape[-1] // LANE
    lane = lax.broadcasted_iota(jnp.int32, (tq, LANE), 1)
    low = lane < HEAD_DIM

    def window(g):
        t = pl.program_id(2) * group + g
        start = pl.multiple_of(jnp.clip(t * tq - half_window, 0, seq_len - span), half_window)
        variant = jnp.where(t == 0, 0, jnp.where(t == nq - 1, 2, 1))
        return start, variant

    def scores(task):
        g, c, half = task
        start, variant = window(g)
        q2 = q_ref[0, g * tq:(g + 1) * tq, c * LANE:(c + 1) * LANE]
        kb = k_ref[0, pl.ds(start, span), c * LANE:(c + 1) * LANE]
        qm = jnp.where(low if half == 0 else jnp.logical_not(low), q2, jnp.zeros_like(q2))
        s = lax.dot_general(qm, kb, (((1,), (1,)), ((), ())), preferred_element_type=F32)
        return s + bias_ref[variant, 2 * c + half]

    def finish(task, s):
        g, c, half = task
        start, _ = window(g)
        vb = v_ref[0, pl.ds(start, span), c * LANE:(c + 1) * LANE]
        m = jnp.max(s, axis=-1, keepdims=True)
        e = jnp.exp2(s - m)
        denom = jnp.sum(e, axis=-1, keepdims=True)
        o = jnp.dot(e.astype(BF16), vb, preferred_element_type=F32) / denom
        return o, LN2 * m + jnp.log(denom)

    tasks = [(g, c, half) for g in range(group) for c in range(nblk) for half in range(2)]
    s_next = scores(tasks[0])
    held = None
    for n, task in enumerate(tasks):
        s = s_next
        if n + 1 < len(tasks):
            s_next = scores(tasks[n + 1])
        o, lse = finish(task, s)
        g, c, half = task
        if half == 0:
            held = (o, lse)
            continue
        rows, cols = slice(g * tq, (g + 1) * tq), slice(c * LANE, (c + 1) * LANE)
        o_ref[0, rows, cols] = jnp.where(low, held[0], o)
        lse_ref[0, rows, cols] = jnp.where(low, held[1], lse)


def attn_c(view, bias, *, dil, tq, half_window, name):
    B, L, _ = view.shape
    width = C_HEADS_PER_GROUP * HEAD_DIM
    span = tq + 2 * half_window
    nq = L // tq
    assert L % tq == 0 and L >= span and nq >= 2
    group = math.gcd(nq, BAND_GROUP)
    kern = functools.partial(_attn_c_kernel, tq=tq, group=group, span=span, half_window=half_window, seq_len=L)
    out_spec = pl.BlockSpec((1, group * tq, width), lambda b, r, t: (b, t, r))
    out_shape = jax.ShapeDtypeStruct((B, L, dil * width), F32)
    return pl.pallas_call(
        kern,
        grid=(B, dil, nq // group),
        in_specs=[pl.BlockSpec((1, group * tq, width), lambda b, r, t: (b, t, 3 * r)),
                  pl.BlockSpec((1, L, width), lambda b, r, t: (b, 0, 3 * r + 1)),
                  pl.BlockSpec((1, L, width), lambda b, r, t: (b, 0, 3 * r + 2)),
                  pl.BlockSpec(bias.shape, lambda b, r, t: (0, 0, 0, 0))],
        out_specs=[out_spec, out_spec],
        out_shape=[out_shape, out_shape],
        compiler_params=_params("parallel", "parallel", "arbitrary"),
        name=name,
    )(view, view, view, bias)


def _out_proj_cd_kernel(h_ref, o0, o1, o2, s0, s1, s2, d_ref, wc_ref, wd_ref, o_ref, *scratch):
    tm = h_ref.shape[0]
    width = C_HEADS_PER_GROUP * HEAD_DIM
    spare = iter(scratch)

    def in_position_order(ref, dil):
        if dil == 1:
            return ref[0]
        t_s = next(spare)
        for r in range(dil):
            for j in range(width // LANE):
                t_s[j, pl.ds(r, tm // dil, stride=dil), :] = ref[0, :, r * width + j * LANE:r * width + (j + 1) * LANE]
        return jnp.concatenate([t_s[j] for j in range(width // LANE)], axis=1)

    dils = [d for _, d in C_PATTERNS]
    outs = [in_position_order(r, d) for r, d in zip((o0, o1, o2), dils)]
    lses = [in_position_order(r, d) for r, d in zip((s0, s1, s2), dils)]
    mx = jnp.maximum(jnp.maximum(lses[0], lses[1]), lses[2])
    es = [jnp.exp(l - mx) for l in lses]
    oc = (es[0] * outs[0] + es[1] * outs[1] + es[2] * outs[2]) / (es[0] + es[1] + es[2])
    mix = jnp.dot(oc.astype(BF16), wc_ref[...], preferred_element_type=F32)
    mix = mix + jnp.dot(d_ref[...], wd_ref[...], preferred_element_type=F32)
    o_ref[...] = h_ref[...] + mix


def out_proj_cd(h, oc, lses, od, wc, wd, batch, seq_len):
    M = h.shape[0]
    tm = ROW_TILE
    nrow = seq_len // tm
    width = C_HEADS_PER_GROUP * HEAD_DIM
    row = lambda a: pl.BlockSpec((tm, a.shape[1]), lambda i: (i, 0))
    full = lambda a: pl.BlockSpec(a.shape, lambda i: (0, 0))
    views = [pl.BlockSpec((1, tm // d, d * width), lambda i: (i // nrow, i % nrow, 0)) for _, d in C_PATTERNS]
    n_spare = 2 * sum(d > 1 for _, d in C_PATTERNS)
    return pl.pallas_call(
        _out_proj_cd_kernel,
        grid=(M // tm,),
        in_specs=[row(h), *views, *views, row(od), full(wc), full(wd)],
        out_specs=pl.BlockSpec((tm, D_MODEL), lambda i: (i, 0)),
        out_shape=jax.ShapeDtypeStruct((M, D_MODEL), F32),
        scratch_shapes=[pltpu.VMEM((width // LANE, tm, LANE), F32)] * n_spare,
        compiler_params=_params("parallel"),
        name="out_proj1",
    )(h, *oc, *lses, od, wc, wd)


def _proj1_kernel(x_ref, g_ref, w_ref, cs_ref, c0_ref, c1_ref, c2_ref, y_s):
    xn = _rms(x_ref[...], g_ref[...]).astype(BF16)
    y = jnp.dot(xn, w_ref[...], preferred_element_type=F32) * cs_ref[...]
    tm = y.shape[0]
    width = y.shape[1] // len(C_PATTERNS)
    c0_ref[0] = y[:, :width].astype(BF16)
    nblk = width // LANE
    for j in range(y_s.shape[0]):
        y_s[j] = y[:, width + j * LANE:width + (j + 1) * LANE]
    for g, ref in ((1, c1_ref), (2, c2_ref)):
        dil = C_PATTERNS[g][1]
        for r in range(dil):
            for j in range(nblk):
                rows = y_s[(g - 1) * nblk + j, pl.ds(r, tm // dil, stride=dil), :]
                ref[0, :, r * width + j * LANE:r * width + (j + 1) * LANE] = rows.astype(BF16)


def proj1_call(x, g, w, colscale, batch, seq_len):
    M, K = x.shape
    N = w.shape[1]
    tm = ROW_TILE
    nrow = seq_len // tm
    width = N // len(C_PATTERNS)
    assert [d for _, d in C_PATTERNS][0] == 1
    out_specs = [pl.BlockSpec((1, tm // d, d * width), lambda i: (i // nrow, i % nrow, 0)) for _, d in C_PATTERNS]
    out_shape = [jax.ShapeDtypeStruct((batch, seq_len // d, d * width), BF16) for _, d in C_PATTERNS]
    return pl.pallas_call(
        _proj1_kernel,
        grid=(M // tm,),
        in_specs=[pl.BlockSpec((tm, K), lambda i: (i, 0)),
                  pl.BlockSpec((1, K), lambda i: (0, 0)),
                  pl.BlockSpec((K, N), lambda i: (0, 0)),
                  pl.BlockSpec((1, N), lambda i: (0, 0))],
        out_specs=out_specs,
        out_shape=out_shape,
        scratch_shapes=[pltpu.VMEM(((N - width) // LANE, tm, LANE), F32)],
        compiler_params=_params("parallel"),
        name="proj1",
    )(x, g.reshape(1, K), w, colscale.reshape(1, N))


def _ffn_kernel(*refs, n_mix, final_norm):
    x_ref, mix_refs = refs[0], refs[1:1 + 2 * n_mix]
    g_ref, wg_ref, wu_ref, wd_ref, fg_ref, o_ref, xn_s, acc_s, h_s = refs[1 + 2 * n_mix:]
    f = pl.program_id(1)

    @pl.when(f == 0)
    def _():
        h = x_ref[...]
        for a_ref, w_ref in zip(mix_refs[::2], mix_refs[1::2]):
            h = h + jnp.dot(a_ref[...], w_ref[...], preferred_element_type=F32)
        h_s[...] = h
        xn_s[...] = _rms(h, g_ref[...]).astype(BF16)
        acc_s[...] = jnp.zeros(acc_s.shape, F32)

    xn = xn_s[...]
    gate = jnp.dot(xn, wg_ref[0].astype(BF16), preferred_element_type=F32)
    up = jnp.dot(xn, wu_ref[0].astype(BF16), preferred_element_type=F32)
    mid = (gate / (1.0 + jnp.exp(-gate)) * up).astype(BF16)
    acc_s[...] += jnp.dot(mid, wd_ref[0].astype(BF16), preferred_element_type=F32)

    @pl.when(f == pl.num_programs(1) - 1)
    def _():
        y = h_s[...] + acc_s[...]
        if final_norm:
            y = _rms(y, fg_ref[...])
        o_ref[...] = y


def ffn(x, mix, g, wg, wu, wd, layer, fg, final_norm, name):
    M, K = x.shape
    tm, tf = FFN_ROW_TILE, FFN_COL_TILE
    kern = functools.partial(_ffn_kernel, n_mix=len(mix), final_norm=final_norm)
    mix_specs, mix_args = [], []
    for a, w in mix:
        mix_specs += [pl.BlockSpec((tm, a.shape[1]), lambda i, f: (i, 0)), pl.BlockSpec(w.shape, lambda i, f: (0, 0))]
        mix_args += [a, w]
    return pl.pallas_call(
        kern,
        grid=(M // tm, D_FF // tf),
        in_specs=[pl.BlockSpec((tm, K), lambda i, f: (i, 0)), *mix_specs,
                  pl.BlockSpec((1, K), lambda i, f: (0, 0)),
                  pl.BlockSpec((1, K, tf), lambda i, f: (layer, 0, f)),
                  pl.BlockSpec((1, K, tf), lambda i, f: (layer, 0, f)),
                  pl.BlockSpec((1, tf, K), lambda i, f: (layer, f, 0)),
                  pl.BlockSpec((1, K), lambda i, f: (0, 0))],
        out_specs=pl.BlockSpec((tm, K), lambda i, f: (i, 0)),
        out_shape=jax.ShapeDtypeStruct((M, K), F32),
        scratch_shapes=[pltpu.VMEM((tm, K), BF16), pltpu.VMEM((tm, K), F32), pltpu.VMEM((tm, K), F32)],
        compiler_params=_params("parallel", "arbitrary"),
        name=name,
    )(x, *mix_args, g.reshape(1, K), wg, wu, wd, fg.reshape(1, K))


def _trig_kernel(ang_ref, cos_ref, sin_ref):
    a = ang_ref[...]
    cos_ref[...] = jnp.cos(a)
    sin_ref[...] = jnp.sin(a)


def rope_tables(seq_len):
    half = D_ROPE // 2
    inv = ROPE_THETA ** (-jnp.arange(half, dtype=F32) / half)
    ang = jnp.arange(seq_len).astype(F32)[:, None] * inv[None, :]
    dense = ang.reshape(seq_len * half // LANE, LANE)
    spec = pl.BlockSpec(dense.shape, lambda: (0, 0))
    cos, sin = pl.pallas_call(
        _trig_kernel,
        in_specs=[spec],
        out_specs=[spec, spec],
        out_shape=[jax.ShapeDtypeStruct(dense.shape, F32)] * 2,
        name="rope_trig",
    )(dense)
    cos, sin = cos.reshape(seq_len, half), sin.reshape(seq_len, half)
    pad = LANE - D_NOPE - D_ROPE
    cos_l = jnp.concatenate([jnp.ones((seq_len, D_NOPE), F32), cos, cos, jnp.ones((seq_len, pad), F32)], axis=1)
    sin_l = jnp.concatenate([jnp.zeros((seq_len, D_NOPE), F32), sin, sin, jnp.zeros((seq_len, pad), F32)], axis=1)
    return cos_l, sin_l


def _prep_d_kernel(x_ref, g_ref, wa_ref, qn_ref, kvn_ref, wq_ref, wkv_ref, cos_ref, sin_ref,
                   q_ref, k_ref, v_ref, *, qscale):
    xn = _rms(x_ref[...], g_ref[...]).astype(BF16)
    lat = jnp.dot(xn, wa_ref[...], preferred_element_type=F32)
    cq = _rms(lat[:, :D_Q_LORA], qn_ref[...]).astype(BF16)
    ckv = _rms(lat[:, D_Q_LORA:D_Q_LORA + D_KV_LORA], kvn_ref[...]).astype(BF16)
    o_pe = D_Q_LORA + D_KV_LORA
    cos, sin = cos_ref[...], sin_ref[...]
    kpe = lat[:, o_pe:o_pe + LANE] * cos + lat[:, o_pe + LANE:o_pe + 2 * LANE] * sin
    qq = jnp.dot(cq, wq_ref[...], preferred_element_type=F32)
    kv = jnp.dot(ckv, wkv_ref[...], preferred_element_type=F32)
    kw = D_HEADS * LANE
    ones_blk = _ones_row_block(V_ROWS_D - D_V, x_ref.shape[0])
    for h in range(D_HEADS):
        qh = qq[:, h * LANE:(h + 1) * LANE] * cos + qq[:, kw + h * LANE:kw + (h + 1) * LANE] * sin
        q_ref[0, h, 0] = (qh * qscale).T.astype(BF16)
        k_ref[:, h * LANE:(h + 1) * LANE] = (kv[:, h * LANE:(h + 1) * LANE] + kpe).astype(BF16)
    for j in range(D_HEADS // 2):
        vt = kv[:, kw + j * LANE:kw + (j + 1) * LANE].T.astype(BF16)
        for half in range(2):
            v_ref[0, 2 * j + half, 0, :D_V] = vt[half * D_V:(half + 1) * D_V]
            v_ref[0, 2 * j + half, 0, D_V:] = ones_blk


def prep_d(x, g, wa, qn, kvn, wq, wkv, cos_l, sin_l, batch, seq_len):
    M, K = x.shape
    T = DENSE_TILE
    nrow = seq_len // T
    kw = D_HEADS * LANE
    kern = functools.partial(_prep_d_kernel, qscale=(D_NOPE + D_ROPE) ** -0.5 * LOG2E)
    full = lambda a: pl.BlockSpec(a.shape, lambda i: (0, 0))
    qn2, kvn2, g2 = qn.reshape(1, -1), kvn.reshape(1, -1), g.reshape(1, K)
    return pl.pallas_call(
        kern,
        grid=(M // T,),
        in_specs=[pl.BlockSpec((T, K), lambda i: (i, 0)), full(g2), full(wa), full(qn2), full(kvn2),
                  full(wq), full(wkv),
                  pl.BlockSpec((T, LANE), lambda i: (i % nrow, 0)),
                  pl.BlockSpec((T, LANE), lambda i: (i % nrow, 0))],
        out_specs=[pl.BlockSpec((1, D_HEADS, 1, LANE, T), lambda i: (i // nrow, 0, i % nrow, 0, 0)),
                   pl.BlockSpec((T, kw), lambda i: (i, 0)),
                   pl.BlockSpec((1, D_HEADS, 1, V_ROWS_D, T), lambda i: (i // nrow, 0, i % nrow, 0, 0))],
        out_shape=[jax.ShapeDtypeStruct((batch, D_HEADS, nrow, LANE, T), BF16),
                   jax.ShapeDtypeStruct((M, kw), BF16),
                   jax.ShapeDtypeStruct((batch, D_HEADS, nrow, V_ROWS_D, T), BF16)],
        compiler_params=_params("parallel"),
        name="prep_d",
    )(x, g2, wa, qn2, kvn2, wq, wkv, cos_l, sin_l)


B_HEAD_ORDER = (0, 4, 1, 5, 2, 6, 3, 7)


def _blocks(w, starts, width, axis):
    return jnp.concatenate([lax.slice_in_dim(w, s, s + width, axis=axis) for s in starts], axis=axis)


def _rot_partner_cols(w):
    half = D_ROPE // 2
    return jnp.concatenate([-w[..., half:], w[..., :half]], axis=-1)


def kernel(x, bias_table, attn_norm, ffn_norm, final_norm, ab_w_in, ab_lambda_q1, ab_lambda_k1,
           ab_lambda_q2, ab_lambda_k2, ab_subln, ab_sink, ab_w_o, cd_w_in, cd_q_norm, cd_w_q_b,
           cd_kv_norm, cd_w_kv_b, cd_w_o, ffn_w_gate, ffn_w_up, ffn_w_down):
    B, S, _ = x.shape
    M = B * S
    T = DENSE_TILE
    h = x.reshape(M, D_MODEL)
    qk_scale = HEAD_DIM ** -0.5 * LOG2E

    o3 = A_HEADS * (2 * A_QK_DIM + A_V_DIM)
    a0 = ab_w_in[0].astype(BF16)
    w0 = jnp.concatenate([a0[:, :o3], _blocks(a0, [o3 + hd * HEAD_DIM for hd in B_HEAD_ORDER], HEAD_DIM, 1),
                          a0[:, o3 + B_HEADS * HEAD_DIM:]], axis=1)
    cs0 = np.ones((AB_IN,), np.float32)
    cs0[:A_HEADS * A_QK_DIM] = qk_scale
    cs0[o3:o3 + B_HEADS * HEAD_DIM] = qk_scale
    qat, vat, qbt, vbt, keys0 = proj0_call(h, attn_norm[0], w0, jnp.asarray(cs0), B, S)
    keys0 = keys0.reshape(B, S, -1)
    bias_a = bias_tiles(bias_table, nvar=2 * BIAS_REACH + 1, nheads=A_HEADS, head0=0, rows=T, cols=T,
                        off0=-BIAS_REACH * T, off_step=T, row_coef=1, col_coef=-1, dil=1,
                        half_window=None, name="bias_a")
    oa = attn_a(qat, keys0, vat, bias_a, ab_lambda_q1[0], ab_lambda_k1[0], ab_lambda_q2[0],
                ab_lambda_k2[0], ab_subln[0], 0.8 - 0.6 * math.exp(-0.3 * 0))

    bias_b = bias_tiles(bias_table, nvar=3, nheads=B_HEADS, head0=A_HEADS, rows=QCOLS + 2 * B_HALF_WINDOW,
                        cols=QCOLS, off0=0, off_step=-B_HALF_WINDOW, row_coef=1, col_coef=-1, dil=1,
                        half_window=B_HALF_WINDOW, name="bias_b")
    ob = attn_b(qbt, keys0, (A_HEADS * A_QK_DIM) // LANE, vbt, bias_b, ab_sink[0])

    wo = ab_w_o[0].astype(BF16)
    wo_a = wo[:A_HEADS * A_V_DIM]
    wo_b = _blocks(wo, [A_HEADS * A_V_DIM + hd * HEAD_DIM for hd in B_HEAD_ORDER], HEAD_DIM, 0)
    h = ffn(h, [(oa.reshape(M, -1), wo_a), (ob.reshape(M, -1), wo_b)], ffn_norm[0], ffn_w_gate, ffn_w_up,
            ffn_w_down, 0, final_norm, False, "ffn0")

    w1 = cd_w_in[0]
    gw = C_HEADS_PER_GROUP * HEAD_DIM
    cw = C_HEADS * HEAD_DIM
    starts1 = [role * cw + g * gw for g in range(len(C_PATTERNS)) for role in range(3)]
    cs1 = np.ones((CD_C_IN,), np.float32)
    for g in range(len(C_PATTERNS)):
        cs1[3 * g * gw:(3 * g + 1) * gw] = qk_scale
    c_views = proj1_call(h, attn_norm[1], _blocks(w1.astype(BF16), starts1, gw, 1), jnp.asarray(cs1), B, S)

    oc, lses = [], []
    for g, (window, dil) in enumerate(C_PATTERNS):
        hw = window // (2 * dil)
        tq_c = 128
        bias_c = bias_tiles(bias_table, nvar=3, nheads=C_HEADS_PER_GROUP, head0=g * C_HEADS_PER_GROUP,
                            rows=tq_c, cols=tq_c + 2 * hw, off0=0, off_step=-hw, row_coef=-1, col_coef=1,
                            dil=dil, half_window=hw, name=f"bias_c{g}")
        o_g, lse_g = attn_c(c_views[g], bias_c, dil=dil, tq=tq_c, half_window=hw, name=f"attn_c{g}")
        oc.append(o_g)
        lses.append(lse_g)

    o_kv = CD_C_IN + D_Q_LORA + D_KV_LORA
    w_pe = w1[:, o_kv:]
    lane_pad = lambda w: jnp.pad(w, ((0, 0), (D_NOPE, LANE - D_NOPE - D_ROPE)))
    wa = jnp.concatenate([w1[:, CD_C_IN:o_kv], lane_pad(w_pe), lane_pad(_rot_partner_cols(w_pe))],
                         axis=1).astype(BF16)
    wq3 = cd_w_q_b[0].reshape(D_Q_LORA, D_HEADS, D_NOPE + D_ROPE)
    zpad = jnp.zeros((D_Q_LORA, D_HEADS, LANE - D_NOPE - D_ROPE), F32)
    wq_main = jnp.concatenate([wq3, zpad], axis=-1)
    wq_rot = jnp.concatenate([jnp.zeros_like(wq3[..., :D_NOPE]), _rot_partner_cols(wq3[..., D_NOPE:]), zpad],
                             axis=-1)
    wq = jnp.concatenate([wq_main.reshape(D_Q_LORA, -1), wq_rot.reshape(D_Q_LORA, -1)], axis=1).astype(BF16)
    wkv3 = cd_w_kv_b[0].reshape(D_KV_LORA, D_HEADS, D_NOPE + D_V)
    wk = jnp.pad(wkv3[..., :D_NOPE], ((0, 0), (0, 0), (0, LANE - D_NOPE))).reshape(D_KV_LORA, -1)
    wv = wkv3[..., D_NOPE:].reshape(D_KV_LORA, -1)
    wkv = jnp.concatenate([wk, wv], axis=1).astype(BF16)
    cos_l, sin_l = rope_tables(S)
    qdt, kd, vdt = prep_d(h, attn_norm[1], wa, cd_q_norm[0], cd_kv_norm[0], wq, wkv, cos_l, sin_l, B, S)
    od = attn_d(qdt, kd.reshape(B, S, D_HEADS * LANE), vdt)

    wo1 = cd_w_o[0]
    wo_c = wo1[:C_HEADS_PER_GROUP * HEAD_DIM].astype(BF16)
    wo_d = wo1[C_HEADS_PER_GROUP * HEAD_DIM:].astype(BF16)
    h = out_proj_cd(h, oc, lses, od.reshape(M, -1), wo_c, wo_d, B, S)
    h = ffn(h, [], ffn_norm[1], ffn_w_gate, ffn_w_up, ffn_w_down, 1, final_norm, True, "ffn1")
    return h.reshape(B, S, D_MODEL)
```

```python
import functools
import math

import numpy as np
import jax
import jax.numpy as jnp
from jax import lax
from jax.experimental import pallas as pl
from jax.experimental.pallas import tpu as pltpu

F32 = jnp.float32
BF16 = jnp.bfloat16

D_MODEL = 1024
HEAD_DIM = 64
EPS = 1e-6
NEG = -1e30
LOG2E = math.log2(math.e)
LN2 = math.log(2.0)

A_HEADS = 4
A_QK_DIM = 2 * HEAD_DIM
A_V_DIM = 2 * HEAD_DIM
B_HEADS = 8
B_KV_HEADS = 2
B_HALF_WINDOW = 128
C_PATTERNS = ((128, 1), (512, 4), (2048, 16))
C_HEADS_PER_GROUP = 4
C_HEADS = C_HEADS_PER_GROUP * len(C_PATTERNS)
D_HEADS = 12
D_Q_LORA = 384
D_KV_LORA = 256
D_NOPE = 64
D_ROPE = 32
D_V = 64
ROPE_THETA = 10000.0
NUM_BUCKETS = 32
MAX_DISTANCE = 1024
D_FF = 2816
AB_IN = 2304
CD_C_IN = 3 * C_HEADS * HEAD_DIM

LANE = 128
VMEM_LIMIT = 48 * 1024 * 1024

ROW_TILE = 512
FFN_ROW_TILE = 1024
FFN_COL_TILE = 256
DENSE_TILE = 512
QCOLS = 256
UNROLL = 4
DENSE_QTILES = 4
A_PASSES = 2
D_PASSES = 4
BAND_GROUP = 8
BIAS_REACH = 3
V_ROWS_D = 80
V_ROWS_A = 144


def _bucket_thresholds():
    nb = NUM_BUCKETS // 2
    max_exact = nb // 2
    n = np.arange(1, 4 * MAX_DISTANCE)
    large = max_exact + (np.log(n.astype(np.float32) / np.float32(max_exact))
                         / np.float32(math.log(MAX_DISTANCE / max_exact))
                         * np.float32(nb - max_exact)).astype(np.int32)
    mag = np.where(n < max_exact, n, np.minimum(large, nb - 1))
    return tuple(int(n[np.argmax(mag >= k)]) for k in range(1, nb))


BUCKET_THRESHOLDS = _bucket_thresholds()
assert BUCKET_THRESHOLDS[-1] <= (BIAS_REACH - 1) * DENSE_TILE + 1


def _params(*sem):
    return pltpu.CompilerParams(dimension_semantics=sem, vmem_limit_bytes=VMEM_LIMIT)


def _rms(x, g):
    return x * lax.rsqrt(jnp.mean(x * x, axis=-1, keepdims=True) + EPS) * g


def _ones_row_block(rows, cols):
    r = lax.broadcasted_iota(jnp.int32, (rows, cols), 0)
    return jnp.where(r == 0, 1.0, 0.0).astype(BF16)


def _store_masked_halves(q_ref, idx, qt):
    zeros = jnp.zeros((HEAD_DIM, qt.shape[1]), BF16)
    q_ref[idx + (0, 0, slice(None, HEAD_DIM))] = qt[:HEAD_DIM]
    q_ref[idx + (0, 0, slice(HEAD_DIM, None))] = zeros
    q_ref[idx + (1, 0, slice(None, HEAD_DIM))] = zeros
    q_ref[idx + (1, 0, slice(HEAD_DIM, None))] = qt[HEAD_DIM:]


def _proj0_kernel(x_ref, g_ref, w_ref, cs_ref, qa_ref, va_ref, qb_ref, vb_ref, k_ref):
    xn = _rms(x_ref[...], g_ref[...]).astype(BF16)
    y = jnp.dot(xn, w_ref[...], preferred_element_type=F32) * cs_ref[...]
    tm = y.shape[0]
    ka0 = A_HEADS * A_QK_DIM
    va0 = 2 * ka0
    qb0 = va0 + A_HEADS * A_V_DIM
    kb0 = qb0 + B_HEADS * HEAD_DIM
    vb0 = kb0 + B_KV_HEADS * HEAD_DIM
    for h in range(A_HEADS):
        _store_masked_halves(qa_ref, (0, h), y[:, h * A_QK_DIM:(h + 1) * A_QK_DIM].T.astype(BF16))
        va_ref[0, h, 0, :A_V_DIM] = y[:, va0 + h * A_V_DIM:va0 + (h + 1) * A_V_DIM].T.astype(BF16)
        va_ref[0, h, 0, A_V_DIM:] = _ones_row_block(V_ROWS_A - A_V_DIM, tm)
    for j in range(B_HEADS // 2):
        _store_masked_halves(qb_ref, (0, j), y[:, qb0 + j * LANE:qb0 + (j + 1) * LANE].T.astype(BF16))
    vbt = y[:, vb0:vb0 + LANE].T.astype(BF16)
    ones_blk = _ones_row_block(V_ROWS_D - HEAD_DIM, LANE)
    for g in range(B_KV_HEADS):
        for c in range(tm // LANE):
            vb_ref[0, g, c, :HEAD_DIM] = vbt[g * HEAD_DIM:(g + 1) * HEAD_DIM, c * LANE:(c + 1) * LANE]
            vb_ref[0, g, c, HEAD_DIM:] = ones_blk
    k_ref[:, :ka0] = y[:, ka0:va0].astype(BF16)
    k_ref[:, ka0:] = y[:, kb0:vb0].astype(BF16)


def proj0_call(x, g, w, colscale, batch, seq_len):
    M, K = x.shape
    N = w.shape[1]
    T = DENSE_TILE
    nrow = seq_len // T
    nkb = A_HEADS * A_QK_DIM + B_KV_HEADS * HEAD_DIM
    qspec = pl.BlockSpec((1, A_HEADS, 2, 1, LANE, T), lambda i: (i // nrow, 0, 0, i % nrow, 0, 0))
    qshape = jax.ShapeDtypeStruct((batch, A_HEADS, 2, nrow, LANE, T), BF16)
    return pl.pallas_call(
        _proj0_kernel,
        grid=(M // T,),
        in_specs=[pl.BlockSpec((T, K), lambda i: (i, 0)),
                  pl.BlockSpec((1, K), lambda i: (0, 0)),
                  pl.BlockSpec((K, N), lambda i: (0, 0)),
                  pl.BlockSpec((1, N), lambda i: (0, 0))],
        out_specs=[qspec,
                   pl.BlockSpec((1, A_HEADS, 1, V_ROWS_A, T), lambda i: (i // nrow, 0, i % nrow, 0, 0)),
                   qspec,
                   pl.BlockSpec((1, B_KV_HEADS, T // LANE, V_ROWS_D, LANE),
                                lambda i: (i // nrow, 0, i % nrow, 0, 0)),
                   pl.BlockSpec((T, nkb), lambda i: (i, 0))],
        out_shape=[qshape,
                   jax.ShapeDtypeStruct((batch, A_HEADS, nrow, V_ROWS_A, T), BF16),
                   qshape,
                   jax.ShapeDtypeStruct((batch, B_KV_HEADS, seq_len // LANE, V_ROWS_D, LANE), BF16),
                   jax.ShapeDtypeStruct((M, nkb), BF16)],
        compiler_params=_params("parallel"),
        name="proj0",
    )(x, g.reshape(1, K), w, colscale.reshape(1, N))


def _bias_kernel(tab_ref, o_ref, *, nvar, off0, off_step, row_coef, col_coef, dil, half_window, head0):
    hcol = head0 + pl.program_id(1)
    R, C = o_ref.shape[-2:]
    row = lax.broadcasted_iota(jnp.int32, (R, C), 0)
    col = lax.broadcasted_iota(jnp.int32, (R, C), 1)
    base = row_coef * row + col_coef * col
    span_lo = min(row_coef * (R - 1), 0) + min(col_coef * (C - 1), 0)
    span_hi = max(row_coef * (R - 1), 0) + max(col_coef * (C - 1), 0)
    nb = NUM_BUCKETS // 2

    def side(n, n_lo, n_hi, row0):
        val = jnp.full((R, C), tab_ref[row0 + sum(t <= n_lo for t in BUCKET_THRESHOLDS), hcol], F32)
        for k, thr in enumerate(BUCKET_THRESHOLDS, start=1):
            if n_lo < thr <= n_hi:
                val = jnp.where(n >= thr, tab_ref[row0 + k, hcol], val)
        return val

    for v in range(nvar):
        @pl.when(pl.program_id(0) == v)
        def _(v=v):
            off = off0 + v * off_step
            rel = off + base
            lo, hi = (off + span_lo) * dil, (off + span_hi) * dil
            dist = rel * dil
            n = jnp.abs(dist)
            if hi <= 0:
                val = side(n, -hi, -lo, 0)
            elif lo > 0:
                val = side(n, lo, hi, nb)
            else:
                val = jnp.where(dist > 0, side(n, 1, hi, nb), side(n, 0, -lo, 0))
            val = val * LOG2E
            if half_window is not None:
                val = jnp.where(jnp.abs(rel) <= half_window, val, NEG)
            o_ref[0, 0] = val


def bias_tiles(table, *, nvar, nheads, head0, rows, cols, off0, off_step, row_coef, col_coef,
               dil, half_window, name):
    kern = functools.partial(_bias_kernel, nvar=nvar, off0=off0, off_step=off_step, row_coef=row_coef,
                             col_coef=col_coef, dil=dil, half_window=half_window, head0=head0)
    return pl.pallas_call(
        kern,
        grid=(nvar, nheads),
        in_specs=[pl.BlockSpec(memory_space=pltpu.SMEM)],
        out_specs=pl.BlockSpec((1, 1, rows, cols), lambda v, h: (v, h, 0, 0)),
        out_shape=jax.ShapeDtypeStruct((nvar, nheads, rows, cols), F32),
        compiler_params=_params("parallel", "parallel"),
        name=name,
    )(table)


def _dense_pipeline(nk, tile, n_streams, score_fn, value_fn, m_s, acc_s, bufs):
    UCOLS = tile
    units = [(i, c * UCOLS) for i in range(n_streams) for c in range(tile // UCOLS)]

    def produce(kc, unit, nxt):
        i, c0 = unit
        cols = slice(c0, c0 + UCOLS)
        s = score_fn(kc + 1, i, cols)
        nxt[0][i, :, cols] = s
        nxt[1][i, :, cols] = jnp.max(s, axis=0, keepdims=True)

    def consume(kc, unit, cur):
        i, c0 = unit
        cols = slice(c0, c0 + UCOLS)
        m_old = m_s[i, :, cols]
        m_new = jnp.maximum(m_old, cur[1][i, :, cols])
        alpha = jnp.exp2(m_old - m_new)
        p = jnp.exp2((cur[0][i, :, cols] - m_new).astype(BF16))
        acc_s[i, :, cols] = (alpha * acc_s[i, :, cols]
                             + jnp.dot(value_fn(kc, i), p, preferred_element_type=F32))
        m_s[i, :, cols] = m_new

    def stage(kc, cur, nxt):
        for unit in units:
            if nxt is not None:
                produce(kc, unit, nxt)
            if cur is not None:
                consume(kc, unit, cur)

    m_s[...] = jnp.full(m_s.shape, NEG, F32)
    acc_s[...] = jnp.zeros(acc_s.shape, F32)
    stage(-1, None, bufs[0])
    n_loop = (nk - 1) // UNROLL

    def body(j, carry):
        for u in range(UNROLL):
            stage(UNROLL * j + u, bufs[u % 2], bufs[(u + 1) % 2])
        return carry

    lax.fori_loop(0, n_loop, body, 0)
    for kc in range(n_loop * UNROLL, nk):
        stage(kc, bufs[kc % 2], bufs[(kc + 1) % 2] if kc < nk - 1 else None)


def _attn_a_kernel(q_ref, k_ref, v_ref, bias_ref, lq1_ref, lk1_ref, lq2_ref, lk2_ref, subln_ref, o_ref,
                   m_s, acc_s, s_a, s_b, cm_a, cm_b, *, tile, nk, qtiles, npass, lambda_init):
    lam = (jnp.exp(jnp.sum(lq1_ref[...] * lk1_ref[...], axis=-1, keepdims=True))
           - jnp.exp(jnp.sum(lq2_ref[...] * lk2_ref[...], axis=-1, keepdims=True)) + lambda_init)

    def one_pass(p, carry):
        q0 = p * qtiles
        qi0 = pl.program_id(2) * (qtiles * npass) + q0

        def score_fn(kc, i, cols):
            j, qt = divmod(i, qtiles)
            kblk = k_ref[0, pl.ds(pl.multiple_of(kc * tile, tile), tile), :]
            bt = bias_ref[jnp.clip(kc - (qi0 + qt), -BIAS_REACH, BIAS_REACH) + BIAS_REACH, 0, :, cols]
            return jnp.dot(kblk, q_ref[0, 0, j, q0 + qt, :, cols], preferred_element_type=F32) + bt

        def value_fn(kc, i):
            return v_ref[0, 0, kc]

        _dense_pipeline(nk, tile, 2 * qtiles, score_fn, value_fn, m_s, acc_s, ((s_a, cm_a), (s_b, cm_b)))
        for qt in range(qtiles):
            a1, a2 = acc_s[qt], acc_s[qtiles + qt]
            o = (a1[:A_V_DIM] / a1[A_V_DIM:A_V_DIM + 1]
                 - lam * (a2[:A_V_DIM] / a2[A_V_DIM:A_V_DIM + 1]))
            ms = jnp.mean(o * o, axis=0, keepdims=True)
            y = o * lax.rsqrt(ms + EPS) * subln_ref[...] * (1.0 - lambda_init)
            rows = pl.ds(pl.multiple_of((q0 + qt) * tile, tile), tile)
            o_ref[0, rows] = y.T.astype(o_ref.dtype)
        return carry

    lax.fori_loop(0, npass, one_pass, 0)


def attn_a(qt, karr, vt, bias, lq1, lk1, lq2, lk2, subln, lambda_init):
    B, S, _ = karr.shape
    T = DENSE_TILE
    nq = nk = S // T
    assert nk % 2 == 0 and nk >= 4
    qtiles = math.gcd(nq, DENSE_QTILES)
    npass = math.gcd(nq // qtiles, A_PASSES)
    ns = 2 * qtiles
    kern = functools.partial(_attn_a_kernel, tile=T, nk=nk, qtiles=qtiles, npass=npass, lambda_init=lambda_init)
    vec = lambda n: pl.BlockSpec((1, n), lambda h, b, i: (0, 0))
    return pl.pallas_call(
        kern,
        grid=(A_HEADS, B, nq // (qtiles * npass)),
        in_specs=[pl.BlockSpec((1, 1, 2, qtiles * npass, LANE, T), lambda h, b, i: (b, h, 0, i, 0, 0)),
                  pl.BlockSpec((1, S, LANE), lambda h, b, i: (b, 0, h)),
                  pl.BlockSpec((1, 1, nk, V_ROWS_A, T), lambda h, b, i: (b, h, 0, 0, 0)),
                  pl.BlockSpec((2 * BIAS_REACH + 1, 1, T, T), lambda h, b, i: (0, h, 0, 0),
                               pipeline_mode=pl.Buffered(1)),
                  vec(HEAD_DIM), vec(HEAD_DIM), vec(HEAD_DIM), vec(HEAD_DIM),
                  pl.BlockSpec((A_V_DIM, 1), lambda h, b, i: (0, 0))],
        out_specs=pl.BlockSpec((1, qtiles * npass * T, LANE), lambda h, b, i: (b, i, h)),
        out_shape=jax.ShapeDtypeStruct((B, S, A_HEADS * A_V_DIM), BF16),
        scratch_shapes=[pltpu.VMEM((ns, 1, T), F32), pltpu.VMEM((ns, V_ROWS_A, T), F32),
                        pltpu.VMEM((ns, T, T), F32), pltpu.VMEM((ns, T, T), F32),
                        pltpu.VMEM((ns, 1, T), F32), pltpu.VMEM((ns, 1, T), F32)],
        compiler_params=_params("parallel", "parallel", "arbitrary"),
        name="attn_a",
    )(qt, karr, vt, bias, lq1.reshape(1, -1), lk1.reshape(1, -1), lq2.reshape(1, -1),
      lk2.reshape(1, -1), subln.reshape(-1, 1))


def _attn_d_kernel(q_ref, k_ref, v_ref, o_ref, m_s, acc_s, s_a, s_b, cm_a, cm_b, *, tile, nk, qtiles, npass):
    def one_pass(p, carry):
        q0 = p * qtiles

        def score_fn(kc, i, cols):
            hh, qt = divmod(i, qtiles)
            kblk = k_ref[0, pl.ds(pl.multiple_of(kc * tile, tile), tile), hh * LANE:(hh + 1) * LANE]
            return jnp.dot(kblk, q_ref[0, hh, q0 + qt, :, cols], preferred_element_type=F32)

        def value_fn(kc, i):
            return v_ref[0, i // qtiles, kc]

        _dense_pipeline(nk, tile, 2 * qtiles, score_fn, value_fn, m_s, acc_s, ((s_a, cm_a), (s_b, cm_b)))
        for qt in range(qtiles):
            outs = []
            for hh in range(2):
                acc = acc_s[hh * qtiles + qt]
                outs.append(acc[:D_V] / acc[D_V:D_V + 1])
            rows = pl.ds(pl.multiple_of((q0 + qt) * tile, tile), tile)
            o_ref[0, rows] = jnp.concatenate(outs, axis=0).T.astype(o_ref.dtype)
        return carry

    lax.fori_loop(0, npass, one_pass, 0)


def attn_d(qt, k, vt):
    B, S, _ = k.shape
    T = DENSE_TILE
    nq = nk = S // T
    qtiles = math.gcd(nq, DENSE_QTILES)
    npass = math.gcd(nq // qtiles, D_PASSES)
    ns = 2 * qtiles
    kern = functools.partial(_attn_d_kernel, tile=T, nk=nk, qtiles=qtiles, npass=npass)
    return pl.pallas_call(
        kern,
        grid=(B, D_HEADS // 2, nq // (qtiles * npass)),
        in_specs=[pl.BlockSpec((1, 2, qtiles * npass, LANE, T), lambda b, h, i: (b, h, i, 0, 0)),
                  pl.BlockSpec((1, S, 2 * LANE), lambda b, h, i: (b, 0, h)),
                  pl.BlockSpec((1, 2, nk, V_ROWS_D, T), lambda b, h, i: (b, h, 0, 0, 0))],
        out_specs=pl.BlockSpec((1, qtiles * npass * T, LANE), lambda b, h, i: (b, i, h)),
        out_shape=jax.ShapeDtypeStruct((B, S, D_HEADS * D_V), BF16),
        scratch_shapes=[pltpu.VMEM((ns, 1, T), F32), pltpu.VMEM((ns, V_ROWS_D, T), F32),
                        pltpu.VMEM((ns, T, T), F32), pltpu.VMEM((ns, T, T), F32),
                        pltpu.VMEM((ns, 1, T), F32), pltpu.VMEM((ns, 1, T), F32)],
        compiler_params=_params("parallel", "parallel", "arbitrary"),
        name="attn_d",
    )(qt, k, vt)


def _attn_b_kernel(q_ref, k_ref, v_ref, bias_ref, sink_ref, o_ref, s_a, s_b, s_c, cm_a, cm_b, cm_c, *, tile, seq_len):
    half_window = B_HALF_WINDOW
    span = QCOLS + 2 * half_window
    nchunk = span // LANE
    step = pl.program_id(1)

    def window(c):
        q0 = step * tile + c * QCOLS
        start = pl.multiple_of(jnp.clip(q0 - half_window, 0, seq_len - span), LANE)
        variant = jnp.where(q0 == 0, 0, jnp.where(q0 + QCOLS == seq_len, 2, 1))
        return start, variant

    def scores(task):
        c, j, g = task
        start, variant = window(c)
        kwin = k_ref[0, pl.ds(start, span), :]
        s = jnp.dot(kwin, q_ref[0, j, g, 0, :, c * QCOLS:(c + 1) * QCOLS], preferred_element_type=F32)
        return s + bias_ref[variant, j + (B_HEADS // 2) * g]

    def finish(task, s_ref, cm_ref):
        c, j, g = task
        head = j + (B_HEADS // 2) * g
        start, _ = window(c)
        chunk0 = start // LANE
        vwin = jnp.concatenate([v_ref[0, g, chunk0 + n] for n in range(nchunk)], axis=1)
        sk = sink_ref[:, head:head + 1] * LOG2E
        m = jnp.maximum(cm_ref[...], sk)
        e = jnp.exp2((s_ref[...] - m).astype(BF16))
        ov = jnp.dot(vwin, e, preferred_element_type=F32)
        denom = ov[HEAD_DIM:HEAD_DIM + 1] + jnp.exp2(sk - m)
        return ov[:HEAD_DIM] / denom

    tasks = [(c, j, g) for c in range(tile // QCOLS) for j in range(B_HEADS // 2) for g in range(B_KV_HEADS)]
    bufs = ((s_a, cm_a), (s_b, cm_b), (s_c, cm_c))

    def produce(task, buf):
        s = scores(task)
        buf[0][...] = s
        buf[1][...] = jnp.max(s, axis=0, keepdims=True)

    produce(tasks[0], bufs[0])
    produce(tasks[1], bufs[1])
    held = None
    for n, task in enumerate(tasks):
        if n + 2 < len(tasks):
            produce(tasks[n + 2], bufs[(n + 2) % 3])
        o = finish(task, *bufs[n % 3])
        c, j, g = task
        if g == 0:
            held = o
            continue
        pair = jnp.concatenate([held, o], axis=0).T
        o_ref[0, c * QCOLS:(c + 1) * QCOLS, j * LANE:(j + 1) * LANE] = pair.astype(o_ref.dtype)


def attn_b(qt, karr, k_block, vt, bias, sink):
    B, S, _ = karr.shape
    T = DENSE_TILE
    span = QCOLS + 2 * B_HALF_WINDOW
    return pl.pallas_call(
        functools.partial(_attn_b_kernel, tile=T, seq_len=S),
        grid=(B, S // T),
        in_specs=[pl.BlockSpec((1, B_HEADS // 2, 2, 1, LANE, T), lambda b, i: (b, 0, 0, i, 0, 0)),
                  pl.BlockSpec((1, S, LANE), lambda b, i: (b, 0, k_block)),
                  pl.BlockSpec((1, B_KV_HEADS, S // LANE, V_ROWS_D, LANE), lambda b, i: (b, 0, 0, 0, 0)),
                  pl.BlockSpec(bias.shape, lambda b, i: (0, 0, 0, 0)),
                  pl.BlockSpec((1, B_HEADS), lambda b, i: (0, 0))],
        out_specs=pl.BlockSpec((1, T, B_HEADS * HEAD_DIM), lambda b, i: (b, i, 0)),
        out_shape=jax.ShapeDtypeStruct((B, S, B_HEADS * HEAD_DIM), BF16),
        scratch_shapes=[pltpu.VMEM((span, QCOLS), F32)] * 3 + [pltpu.VMEM((1, QCOLS), F32)] * 3,
        compiler_params=_params("parallel", "arbitrary"),
        name="attn_b",
    )(qt, karr, vt, bias, sink.reshape(1, -1))


def _attn_c_kernel(q_ref, k_ref, v_ref, bias_ref, o_ref, lse_ref, *, tq, group, span, half_window, seq_len):
    nq = seq_len // tq
    nblk = q_ref.shape[-1] // LANE
    lane = lax.broadcasted_iota(jnp.int32, (tq, LANE), 1)
    low = lane < HEAD_DIM

    def window(g):
        t = pl.program_id(2) * group + g
        start = pl.multiple_of(jnp.clip(t * tq - half_window, 0, seq_len - span), half_window)
        variant = jnp.where(t == 0, 0, jnp.where(t == nq - 1, 2, 1))
        return start, variant

    def scores(task):
        g, c, half = task
        start, variant = window(g)
        q2 = q_ref[0, g * tq:(g + 1) * tq, c * LANE:(c + 1) * LANE]
        kb = k_ref[0, pl.ds(start, span), c * LANE:(c + 1) * LANE]
        qm = jnp.where(low if half == 0 else jnp.logical_not(low), q2, jnp.zeros_like(q2))
        s = lax.dot_general(qm, kb, (((1,), (1,)), ((), ())), preferred_element_type=F32)
        return s + bias_ref[variant, 2 * c + half]

    def finish(task, s):
        g, c, half = task
        start, _ = window(g)
        vb = v_ref[0, pl.ds(start, span), c * LANE:(c + 1) * LANE]
        m = jnp.max(s, axis=-1, keepdims=True)
        e = jnp.exp2(s - m)
        denom = jnp.sum(e, axis=-1, keepdims=True)
        o = jnp.dot(e.astype(BF16), vb, preferred_element_type=F32) / denom
        return o, LN2 * m + jnp.log(denom)

    tasks = [(g, c, half) for g in range(group) for c in range(nblk) for half in range(2)]
    s_next = scores(tasks[0])
    held = None
    for n, task in enumerate(tasks):
        s = s_next
        if n + 1 < len(tasks):
            s_next = scores(tasks[n + 1])
        o, lse = finish(task, s)
        g, c, half = task
        if half == 0:
            held = (o, lse)
            continue
        rows, cols = slice(g * tq, (g + 1) * tq), slice(c * LANE, (c + 1) * LANE)
        o_ref[0, rows, cols] = jnp.where(low, held[0], o)
        lse_ref[0, rows, cols] = jnp.where(low, held[1], lse)


def attn_c(view, bias, *, dil, tq, half_window, name):
    B, L, _ = view.shape
    width = C_HEADS_PER_GROUP * HEAD_DIM
    span = tq + 2 * half_window
    nq = L // tq
    assert L % tq == 0 and L >= span and nq >= 2
    group = math.gcd(nq, BAND_GROUP)
    kern = functools.partial(_attn_c_kernel, tq=tq, group=group, span=span, half_window=half_window, seq_len=L)
    out_spec = pl.BlockSpec((1, group * tq, width), lambda b, r, t: (b, t, r))
    out_shape = jax.ShapeDtypeStruct((B, L, dil * width), F32)
    return pl.pallas_call(
        kern,
        grid=(B, dil, nq // group),
        in_specs=[pl.BlockSpec((1, group * tq, width), lambda b, r, t: (b, t, 3 * r)),
                  pl.BlockSpec((1, L, width), lambda b, r, t: (b, 0, 3 * r + 1)),
                  pl.BlockSpec((1, L, width), lambda b, r, t: (b, 0, 3 * r + 2)),
                  pl.BlockSpec(bias.shape, lambda b, r, t: (0, 0, 0, 0))],
        out_specs=[out_spec, out_spec],
        out_shape=[out_shape, out_shape],
        compiler_params=_params("parallel", "parallel", "arbitrary"),
        name=name,
    )(view, view, view, bias)


def _out_proj_cd_kernel(h_ref, o0, o1, o2, s0, s1, s2, d_ref, wc_ref, wd_ref, o_ref, *scratch):
    tm = h_ref.shape[0]
    width = C_HEADS_PER_GROUP * HEAD_DIM
    spare = iter(scratch)

    def in_position_order(ref, dil):
        if dil == 1:
            return ref[0]
        t_s = next(spare)
        for r in range(dil):
            for j in range(width // LANE):
                t_s[j, pl.ds(r, tm // dil, stride=dil), :] = ref[0, :, r * width + j * LANE:r * width + (j + 1) * LANE]
        return jnp.concatenate([t_s[j] for j in range(width // LANE)], axis=1)

    dils = [d for _, d in C_PATTERNS]
    outs = [in_position_order(r, d) for r, d in zip((o0, o1, o2), dils)]
    lses = [in_position_order(r, d) for r, d in zip((s0, s1, s2), dils)]
    mx = jnp.maximum(jnp.maximum(lses[0], lses[1]), lses[2])
    es = [jnp.exp(l - mx) for l in lses]
    oc = (es[0] * outs[0] + es[1] * outs[1] + es[2] * outs[2]) / (es[0] + es[1] + es[2])
    mix = jnp.dot(oc.astype(BF16), wc_ref[...], preferred_element_type=F32)
    mix = mix + jnp.dot(d_ref[...], wd_ref[...], preferred_element_type=F32)
    o_ref[...] = h_ref[...] + mix


def out_proj_cd(h, oc, lses, od, wc, wd, batch, seq_len):
    M = h.shape[0]
    tm = ROW_TILE
    nrow = seq_len // tm
    width = C_HEADS_PER_GROUP * HEAD_DIM
    row = lambda a: pl.BlockSpec((tm, a.shape[1]), lambda i: (i, 0))
    full = lambda a: pl.BlockSpec(a.shape, lambda i: (0, 0))
    views = [pl.BlockSpec((1, tm // d, d * width), lambda i: (i // nrow, i % nrow, 0)) for _, d in C_PATTERNS]
    n_spare = 2 * sum(d > 1 for _, d in C_PATTERNS)
    return pl.pallas_call(
        _out_proj_cd_kernel,
        grid=(M // tm,),
        in_specs=[row(h), *views, *views, row(od), full(wc), full(wd)],
        out_specs=pl.BlockSpec((tm, D_MODEL), lambda i: (i, 0)),
        out_shape=jax.ShapeDtypeStruct((M, D_MODEL), F32),
        scratch_shapes=[pltpu.VMEM((width // LANE, tm, LANE), F32)] * n_spare,
        compiler_params=_params("parallel"),
        name="out_proj1",
    )(h, *oc, *lses, od, wc, wd)


def _proj1_kernel(x_ref, g_ref, w_ref, cs_ref, c0_ref, c1_ref, c2_ref, y_s):
    xn = _rms(x_ref[...], g_ref[...]).astype(BF16)
    y = jnp.dot(xn, w_ref[...], preferred_element_type=F32) * cs_ref[...]
    tm = y.shape[0]
    width = y.shape[1] // len(C_PATTERNS)
    c0_ref[0] = y[:, :width].astype(BF16)
    nblk = width // LANE
    for j in range(y_s.shape[0]):
        y_s[j] = y[:, width + j * LANE:width + (j + 1) * LANE]
    for g, ref in ((1, c1_ref), (2, c2_ref)):
        dil = C_PATTERNS[g][1]
        for r in range(dil):
            for j in range(nblk):
                rows = y_s[(g - 1) * nblk + j, pl.ds(r, tm // dil, stride=dil), :]
                ref[0, :, r * width + j * LANE:r * width + (j + 1) * LANE] = rows.astype(BF16)


def proj1_call(x, g, w, colscale, batch, seq_len):
    M, K = x.shape
    N = w.shape[1]
    tm = ROW_TILE
    nrow = seq_len // tm
    width = N // len(C_PATTERNS)
    assert [d for _, d in C_PATTERNS][0] == 1
    out_specs = [pl.BlockSpec((1, tm // d, d * width), lambda i: (i // nrow, i % nrow, 0)) for _, d in C_PATTERNS]
    out_shape = [jax.ShapeDtypeStruct((batch, seq_len // d, d * width), BF16) for _, d in C_PATTERNS]
    return pl.pallas_call(
        _proj1_kernel,
        grid=(M // tm,),
        in_specs=[pl.BlockSpec((tm, K), lambda i: (i, 0)),
                  pl.BlockSpec((1, K), lambda i: (0, 0)),
                  pl.BlockSpec((K, N), lambda i: (0, 0)),
                  pl.BlockSpec((1, N), lambda i: (0, 0))],
        out_specs=out_specs,
        out_shape=out_shape,
        scratch_shapes=[pltpu.VMEM(((N - width) // LANE, tm, LANE), F32)],
        compiler_params=_params("parallel"),
        name="proj1",
    )(x, g.reshape(1, K), w, colscale.reshape(1, N))


def _ffn_kernel(*refs, n_mix, final_norm):
    x_ref, mix_refs = refs[0], refs[1:1 + 2 * n_mix]
    g_ref, wg_ref, wu_ref, wd_ref, fg_ref, o_ref, xn_s, acc_s, h_s = refs[1 + 2 * n_mix:]
    f = pl.program_id(1)

    @pl.when(f == 0)
    def _():
        h = x_ref[...]
        for a_ref, w_ref in zip(mix_refs[::2], mix_refs[1::2]):
            h = h + jnp.dot(a_ref[...], w_ref[...], preferred_element_type=F32)
        h_s[...] = h
        xn_s[...] = _rms(h, g_ref[...]).astype(BF16)
        acc_s[...] = jnp.zeros(acc_s.shape, F32)

    xn = xn_s[...]
    gate = jnp.dot(xn, wg_ref[0].astype(BF16), preferred_element_type=F32)
    up = jnp.dot(xn, wu_ref[0].astype(BF16), preferred_element_type=F32)
    mid = (gate / (1.0 + jnp.exp(-gate)) * up).astype(BF16)
    acc_s[...] += jnp.dot(mid, wd_ref[0].astype(BF16), preferred_element_type=F32)

    @pl.when(f == pl.num_programs(1) - 1)
    def _():
        y = h_s[...] + acc_s[...]
        if final_norm:
            y = _rms(y, fg_ref[...])
        o_ref[...] = y


def ffn(x, mix, g, wg, wu, wd, layer, fg, final_norm, name):
    M, K = x.shape
    tm, tf = FFN_ROW_TILE, FFN_COL_TILE
    kern = functools.partial(_ffn_kernel, n_mix=len(mix), final_norm=final_norm)
    mix_specs, mix_args = [], []
    for a, w in mix:
        mix_specs += [pl.BlockSpec((tm, a.shape[1]), lambda i, f: (i, 0)), pl.BlockSpec(w.shape, lambda i, f: (0, 0))]
        mix_args += [a, w]
    return pl.pallas_call(
        kern,
        grid=(M // tm, D_FF // tf),
        in_specs=[pl.BlockSpec((tm, K), lambda i, f: (i, 0)), *mix_specs,
                  pl.BlockSpec((1, K), lambda i, f: (0, 0)),
                  pl.BlockSpec((1, K, tf), lambda i, f: (layer, 0, f)),
                  pl.BlockSpec((1, K, tf), lambda i, f: (layer, 0, f)),
                  pl.BlockSpec((1, tf, K), lambda i, f: (layer, f, 0)),
                  pl.BlockSpec((1, K), lambda i, f: (0, 0))],
        out_specs=pl.BlockSpec((tm, K), lambda i, f: (i, 0)),
        out_shape=jax.ShapeDtypeStruct((M, K), F32),
        scratch_shapes=[pltpu.VMEM((tm, K), BF16), pltpu.VMEM((tm, K), F32), pltpu.VMEM((tm, K), F32)],
        compiler_params=_params("parallel", "arbitrary"),
        name=name,
    )(x, *mix_args, g.reshape(1, K), wg, wu, wd, fg.reshape(1, K))


def _trig_kernel(ang_ref, cos_ref, sin_ref):
    a = ang_ref[...]
    cos_ref[...] = jnp.cos(a)
    sin_ref[...] = jnp.sin(a)


def rope_tables(seq_len):
    half = D_ROPE // 2
    inv = ROPE_THETA ** (-jnp.arange(half, dtype=F32) / half)
    ang = jnp.arange(seq_len).astype(F32)[:, None] * inv[None, :]
    dense = ang.reshape(seq_len * half // LANE, LANE)
    spec = pl.BlockSpec(dense.shape, lambda: (0, 0))
    cos, sin = pl.pallas_call(
        _trig_kernel,
        in_specs=[spec],
        out_specs=[spec, spec],
        out_shape=[jax.ShapeDtypeStruct(dense.shape, F32)] * 2,
        name="rope_trig",
    )(dense)
    cos, sin = cos.reshape(seq_len, half), sin.reshape(seq_len, half)
    pad = LANE - D_NOPE - D_ROPE
    cos_l = jnp.concatenate([jnp.ones((seq_len, D_NOPE), F32), cos, cos, jnp.ones((seq_len, pad), F32)], axis=1)
    sin_l = jnp.concatenate([jnp.zeros((seq_len, D_NOPE), F32), sin, sin, jnp.zeros((seq_len, pad), F32)], axis=1)
    return cos_l, sin_l


def _prep_d_kernel(x_ref, g_ref, wa_ref, qn_ref, kvn_ref, wq_ref, wkv_ref, cos_ref, sin_ref,
                   q_ref, k_ref, v_ref, *, qscale):
    xn = _rms(x_ref[...], g_ref[...]).astype(BF16)
    lat = jnp.dot(xn, wa_ref[...], preferred_element_type=F32)
    cq = _rms(lat[:, :D_Q_LORA], qn_ref[...]).astype(BF16)
    ckv = _rms(lat[:, D_Q_LORA:D_Q_LORA + D_KV_LORA], kvn_ref[...]).astype(BF16)
    o_pe = D_Q_LORA + D_KV_LORA
    cos, sin = cos_ref[...], sin_ref[...]
    kpe = lat[:, o_pe:o_pe + LANE] * cos + lat[:, o_pe + LANE:o_pe + 2 * LANE] * sin
    qq = jnp.dot(cq, wq_ref[...], preferred_element_type=F32)
    kv = jnp.dot(ckv, wkv_ref[...], preferred_element_type=F32)
    kw = D_HEADS * LANE
    ones_blk = _ones_row_block(V_ROWS_D - D_V, x_ref.shape[0])
    for h in range(D_HEADS):
        qh = qq[:, h * LANE:(h + 1) * LANE] * cos + qq[:, kw + h * LANE:kw + (h + 1) * LANE] * sin
        q_ref[0, h, 0] = (qh * qscale).T.astype(BF16)
        k_ref[:, h * LANE:(h + 1) * LANE] = (kv[:, h * LANE:(h + 1) * LANE] + kpe).astype(BF16)
    for j in range(D_HEADS // 2):
        vt = kv[:, kw + j * LANE:kw + (j + 1) * LANE].T.astype(BF16)
        for half in range(2):
            v_ref[0, 2 * j + half, 0, :D_V] = vt[half * D_V:(half + 1) * D_V]
            v_ref[0, 2 * j + half, 0, D_V:] = ones_blk


def prep_d(x, g, wa, qn, kvn, wq, wkv, cos_l, sin_l, batch, seq_len):
    M, K = x.shape
    T = DENSE_TILE
    nrow = seq_len // T
    kw = D_HEADS * LANE
    kern = functools.partial(_prep_d_kernel, qscale=(D_NOPE + D_ROPE) ** -0.5 * LOG2E)
    full = lambda a: pl.BlockSpec(a.shape, lambda i: (0, 0))
    qn2, kvn2, g2 = qn.reshape(1, -1), kvn.reshape(1, -1), g.reshape(1, K)
    return pl.pallas_call(
        kern,
        grid=(M // T,),
        in_specs=[pl.BlockSpec((T, K), lambda i: (i, 0)), full(g2), full(wa), full(qn2), full(kvn2),
                  full(wq), full(wkv),
                  pl.BlockSpec((T, LANE), lambda i: (i % nrow, 0)),
                  pl.BlockSpec((T, LANE), lambda i: (i % nrow, 0))],
        out_specs=[pl.BlockSpec((1, D_HEADS, 1, LANE, T), lambda i: (i // nrow, 0, i % nrow, 0, 0)),
                   pl.BlockSpec((T, kw), lambda i: (i, 0)),
                   pl.BlockSpec((1, D_HEADS, 1, V_ROWS_D, T), lambda i: (i // nrow, 0, i % nrow, 0, 0))],
        out_shape=[jax.ShapeDtypeStruct((batch, D_HEADS, nrow, LANE, T), BF16),
                   jax.ShapeDtypeStruct((M, kw), BF16),
                   jax.ShapeDtypeStruct((batch, D_HEADS, nrow, V_ROWS_D, T), BF16)],
        compiler_params=_params("parallel"),
        name="prep_d",
    )(x, g2, wa, qn2, kvn2, wq, wkv, cos_l, sin_l)


B_HEAD_ORDER = (0, 4, 1, 5, 2, 6, 3, 7)


def _blocks(w, starts, width, axis):
    return jnp.concatenate([lax.slice_in_dim(w, s, s + width, axis=axis) for s in starts], axis=axis)


def _rot_partner_cols(w):
    half = D_ROPE // 2
    return jnp.concatenate([-w[..., half:], w[..., :half]], axis=-1)


def kernel(x, bias_table, attn_norm, ffn_norm, final_norm, ab_w_in, ab_lambda_q1, ab_lambda_k1,
           ab_lambda_q2, ab_lambda_k2, ab_subln, ab_sink, ab_w_o, cd_w_in, cd_q_norm, cd_w_q_b,
           cd_kv_norm, cd_w_kv_b, cd_w_o, ffn_w_gate, ffn_w_up, ffn_w_down):
    B, S, _ = x.shape
    M = B * S
    T = DENSE_TILE
    h = x.reshape(M, D_MODEL)
    qk_scale = HEAD_DIM ** -0.5 * LOG2E

    o3 = A_HEADS * (2 * A_QK_DIM + A_V_DIM)
    a0 = ab_w_in[0].astype(BF16)
    w0 = jnp.concatenate([a0[:, :o3], _blocks(a0, [o3 + hd * HEAD_DIM for hd in B_HEAD_ORDER], HEAD_DIM, 1),
                          a0[:, o3 + B_HEADS * HEAD_DIM:]], axis=1)
    cs0 = np.ones((AB_IN,), np.float32)
    cs0[:A_HEADS * A_QK_DIM] = qk_scale
    cs0[o3:o3 + B_HEADS * HEAD_DIM] = qk_scale
    qat, vat, qbt, vbt, keys0 = proj0_call(h, attn_norm[0], w0, jnp.asarray(cs0), B, S)
    keys0 = keys0.reshape(B, S, -1)
    bias_a = bias_tiles(bias_table, nvar=2 * BIAS_REACH + 1, nheads=A_HEADS, head0=0, rows=T, cols=T,
                        off0=-BIAS_REACH * T, off_step=T, row_coef=1, col_coef=-1, dil=1,
                        half_window=None, name="bias_a")
    oa = attn_a(qat, keys0, vat, bias_a, ab_lambda_q1[0], ab_lambda_k1[0], ab_lambda_q2[0],
                ab_lambda_k2[0], ab_subln[0], 0.8 - 0.6 * math.exp(-0.3 * 0))

    bias_b = bias_tiles(bias_table, nvar=3, nheads=B_HEADS, head0=A_HEADS, rows=QCOLS + 2 * B_HALF_WINDOW,
                        cols=QCOLS, off0=0, off_step=-B_HALF_WINDOW, row_coef=1, col_coef=-1, dil=1,
                        half_window=B_HALF_WINDOW, name="bias_b")
    ob = attn_b(qbt, keys0, (A_HEADS * A_QK_DIM) // LANE, vbt, bias_b, ab_sink[0])

    wo = ab_w_o[0].astype(BF16)
    wo_a = wo[:A_HEADS * A_V_DIM]
    wo_b = _blocks(wo, [A_HEADS * A_V_DIM + hd * HEAD_DIM for hd in B_HEAD_ORDER], HEAD_DIM, 0)
    h = ffn(h, [(oa.reshape(M, -1), wo_a), (ob.reshape(M, -1), wo_b)], ffn_norm[0], ffn_w_gate, ffn_w_up,
            ffn_w_down, 0, final_norm, False, "ffn0")

    w1 = cd_w_in[0]
    gw = C_HEADS_PER_GROUP * HEAD_DIM
    cw = C_HEADS * HEAD_DIM
    starts1 = [role * cw + g * gw for g in range(len(C_PATTERNS)) for role in range(3)]
    cs1 = np.ones((CD_C_IN,), np.float32)
    for g in range(len(C_PATTERNS)):
        cs1[3 * g * gw:(3 * g + 1) * gw] = qk_scale
    c_views = proj1_call(h, attn_norm[1], _blocks(w1.astype(BF16), starts1, gw, 1), jnp.asarray(cs1), B, S)

    oc, lses = [], []
    for g, (window, dil) in enumerate(C_PATTERNS):
        hw = window // (2 * dil)
        tq_c = 128
        bias_c = bias_tiles(bias_table, nvar=3, nheads=C_HEADS_PER_GROUP, head0=g * C_HEADS_PER_GROUP,
                            rows=tq_c, cols=tq_c + 2 * hw, off0=0, off_step=-hw, row_coef=-1, col_coef=1,
                            dil=dil, half_window=hw, name=f"bias_c{g}")
        o_g, lse_g = attn_c(c_views[g], bias_c, dil=dil, tq=tq_c, half_window=hw, name=f"attn_c{g}")
        oc.append(o_g)
        lses.append(lse_g)

    o_kv = CD_C_IN + D_Q_LORA + D_KV_LORA
    w_pe = w1[:, o_kv:]
    lane_pad = lambda w: jnp.pad(w, ((0, 0), (D_NOPE, LANE - D_NOPE - D_ROPE)))
    wa = jnp.concatenate([w1[:, CD_C_IN:o_kv], lane_pad(w_pe), lane_pad(_rot_partner_cols(w_pe))],
                         axis=1).astype(BF16)
    wq3 = cd_w_q_b[0].reshape(D_Q_LORA, D_HEADS, D_NOPE + D_ROPE)
    zpad = jnp.zeros((D_Q_LORA, D_HEADS, LANE - D_NOPE - D_ROPE), F32)
    wq_main = jnp.concatenate([wq3, zpad], axis=-1)
    wq_rot = jnp.concatenate([jnp.zeros_like(wq3[..., :D_NOPE]), _rot_partner_cols(wq3[..., D_NOPE:]), zpad],
                             axis=-1)
    wq = jnp.concatenate([wq_main.reshape(D_Q_LORA, -1), wq_rot.reshape(D_Q_LORA, -1)], axis=1).astype(BF16)
    wkv3 = cd_w_kv_b[0].reshape(D_KV_LORA, D_HEADS, D_NOPE + D_V)
    wk = jnp.pad(wkv3[..., :D_NOPE], ((0, 0), (0, 0), (0, LANE - D_NOPE))).reshape(D_KV_LORA, -1)
    wv = wkv3[..., D_NOPE:].reshape(D_KV_LORA, -1)
    wkv = jnp.concatenate([wk, wv], axis=1).astype(BF16)
    cos_l, sin_l = rope_tables(S)
    qdt, kd, vdt = prep_d(h, attn_norm[1], wa, cd_q_norm[0], cd_kv_norm[0], wq, wkv, cos_l, sin_l, B, S)
    od = attn_d(qdt, kd.reshape(B, S, D_HEADS * LANE), vdt)

    wo1 = cd_w_o[0]
    wo_c = wo1[:C_HEADS_PER_GROUP * HEAD_DIM].astype(BF16)
    wo_d = wo1[C_HEADS_PER_GROUP * HEAD_DIM:].astype(BF16)
    h = out_proj_cd(h, oc, lses, od.reshape(M, -1), wo_c, wo_d, B, S)
    h = ffn(h, [], ffn_norm[1], ffn_w_gate, ffn_w_up, ffn_w_down, 1, final_norm, True, "ffn1")
    return h.reshape(B, S, D_MODEL)
```

```python
import functools
import math

import numpy as np
import jax
import jax.numpy as jnp
from jax import lax
from jax.experimental import pallas as pl
from jax.experimental.pallas import tpu as pltpu

F32 = jnp.float32
BF16 = jnp.bfloat16

D_MODEL = 1024
HEAD_DIM = 64
EPS = 1e-6
NEG = -1e30
LOG2E = math.log2(math.e)
LN2 = math.log(2.0)

A_HEADS = 4
A_QK_DIM = 2 * HEAD_DIM
A_V_DIM = 2 * HEAD_DIM
B_HEADS = 8
B_KV_HEADS = 2
B_HALF_WINDOW = 128
C_PATTERNS = ((128, 1), (512, 4), (2048, 16))
C_HEADS_PER_GROUP = 4
C_HEADS = C_HEADS_PER_GROUP * len(C_PATTERNS)
D_HEADS = 12
D_Q_LORA = 384
D_KV_LORA = 256
D_NOPE = 64
D_ROPE = 32
D_V = 64
ROPE_THETA = 10000.0
NUM_BUCKETS = 32
MAX_DISTANCE = 1024
D_FF = 2816
AB_IN = 2304
CD_C_IN = 3 * C_HEADS * HEAD_DIM

LANE = 128
VMEM_LIMIT = 48 * 1024 * 1024

ROW_TILE = 512
FFN_ROW_TILE = 1024
FFN_COL_TILE = 256
DENSE_TILE = 512
QCOLS = 256
UNROLL = 2
DENSE_QTILES = 4
A_PASSES = 2
D_PASSES = 4
BAND_GROUP = 8
BIAS_REACH = 3
V_ROWS_D = 80
V_ROWS_A = 144


def _bucket_thresholds():
    nb = NUM_BUCKETS // 2
    max_exact = nb // 2
    n = np.arange(1, 4 * MAX_DISTANCE)
    large = max_exact + (np.log(n.astype(np.float32) / np.float32(max_exact))
                         / np.float32(math.log(MAX_DISTANCE / max_exact))
                         * np.float32(nb - max_exact)).astype(np.int32)
    mag = np.where(n < max_exact, n, np.minimum(large, nb - 1))
    return tuple(int(n[np.argmax(mag >= k)]) for k in range(1, nb))


BUCKET_THRESHOLDS = _bucket_thresholds()
assert BUCKET_THRESHOLDS[-1] <= (BIAS_REACH - 1) * DENSE_TILE + 1


def _params(*sem):
    return pltpu.CompilerParams(dimension_semantics=sem, vmem_limit_bytes=VMEM_LIMIT)


def _rms(x, g):
    return x * lax.rsqrt(jnp.mean(x * x, axis=-1, keepdims=True) + EPS) * g


def _ones_row_block(rows, cols):
    r = lax.broadcasted_iota(jnp.int32, (rows, cols), 0)
    return jnp.where(r == 0, 1.0, 0.0).astype(BF16)


def _store_masked_halves(q_ref, idx, qt):
    zeros = jnp.zeros((HEAD_DIM, qt.shape[1]), BF16)
    q_ref[idx + (0, 0, slice(None, HEAD_DIM))] = qt[:HEAD_DIM]
    q_ref[idx + (0, 0, slice(HEAD_DIM, None))] = zeros
    q_ref[idx + (1, 0, slice(None, HEAD_DIM))] = zeros
    q_ref[idx + (1, 0, slice(HEAD_DIM, None))] = qt[HEAD_DIM:]


def _proj0_kernel(x_ref, g_ref, w_ref, cs_ref, qa_ref, va_ref, qb_ref, vb_ref, k_ref):
    xn = _rms(x_ref[...], g_ref[...]).astype(BF16)
    y = jnp.dot(xn, w_ref[...], preferred_element_type=F32) * cs_ref[...]
    tm = y.shape[0]
    ka0 = A_HEADS * A_QK_DIM
    va0 = 2 * ka0
    qb0 = va0 + A_HEADS * A_V_DIM
    kb0 = qb0 + B_HEADS * HEAD_DIM
    vb0 = kb0 + B_KV_HEADS * HEAD_DIM
    for h in range(A_HEADS):
        _store_masked_halves(qa_ref, (0, h), y[:, h * A_QK_DIM:(h + 1) * A_QK_DIM].T.astype(BF16))
        va_ref[0, h, 0, :A_V_DIM] = y[:, va0 + h * A_V_DIM:va0 + (h + 1) * A_V_DIM].T.astype(BF16)
        va_ref[0, h, 0, A_V_DIM:] = _ones_row_block(V_ROWS_A - A_V_DIM, tm)
    for j in range(B_HEADS // 2):
        _store_masked_halves(qb_ref, (0, j), y[:, qb0 + j * LANE:qb0 + (j + 1) * LANE].T.astype(BF16))
    vbt = y[:, vb0:vb0 + LANE].T.astype(BF16)
    ones_blk = _ones_row_block(V_ROWS_D - HEAD_DIM, LANE)
    for g in range(B_KV_HEADS):
        for c in range(tm // LANE):
            vb_ref[0, g, c, :HEAD_DIM] = vbt[g * HEAD_DIM:(g + 1) * HEAD_DIM, c * LANE:(c + 1) * LANE]
            vb_ref[0, g, c, HEAD_DIM:] = ones_blk
    k_ref[:, :ka0] = y[:, ka0:va0].astype(BF16)
    k_ref[:, ka0:] = y[:, kb0:vb0].astype(BF16)


def proj0_call(x, g, w, colscale, batch, seq_len):
    M, K = x.shape
    N = w.shape[1]
    T = DENSE_TILE
    nrow = seq_len // T
    nkb = A_HEADS * A_QK_DIM + B_KV_HEADS * HEAD_DIM
    qspec = pl.BlockSpec((1, A_HEADS, 2, 1, LANE, T), lambda i: (i // nrow, 0, 0, i % nrow, 0, 0))
    qshape = jax.ShapeDtypeStruct((batch, A_HEADS, 2, nrow, LANE, T), BF16)
    return pl.pallas_call(
        _proj0_kernel,
        grid=(M // T,),
        in_specs=[pl.BlockSpec((T, K), lambda i: (i, 0)),
                  pl.BlockSpec((1, K), lambda i: (0, 0)),
                  pl.BlockSpec((K, N), lambda i: (0, 0)),
                  pl.BlockSpec((1, N), lambda i: (0, 0))],
        out_specs=[qspec,
                   pl.BlockSpec((1, A_HEADS, 1, V_ROWS_A, T), lambda i: (i // nrow, 0, i % nrow, 0, 0)),
                   qspec,
                   pl.BlockSpec((1, B_KV_HEADS, T // LANE, V_ROWS_D, LANE),
                                lambda i: (i // nrow, 0, i % nrow, 0, 0)),
                   pl.BlockSpec((T, nkb), lambda i: (i, 0))],
        out_shape=[qshape,
                   jax.ShapeDtypeStruct((batch, A_HEADS, nrow, V_ROWS_A, T), BF16),
                   qshape,
                   jax.ShapeDtypeStruct((batch, B_KV_HEADS, seq_len // LANE, V_ROWS_D, LANE), BF16),
                   jax.ShapeDtypeStruct((M, nkb), BF16)],
        compiler_params=_params("parallel"),
        name="proj0",
    )(x, g.reshape(1, K), w, colscale.reshape(1, N))


def _bias_kernel(tab_ref, o_ref, *, nvar, off0, off_step, row_coef, col_coef, dil, half_window, head0):
    hcol = head0 + pl.program_id(1)
    R, C = o_ref.shape[-2:]
    row = lax.broadcasted_iota(jnp.int32, (R, C), 0)
    col = lax.broadcasted_iota(jnp.int32, (R, C), 1)
    base = row_coef * row + col_coef * col
    span_lo = min(row_coef * (R - 1), 0) + min(col_coef * (C - 1), 0)
    span_hi = max(row_coef * (R - 1), 0) + max(col_coef * (C - 1), 0)
    nb = NUM_BUCKETS // 2

    def side(n, n_lo, n_hi, row0):
        val = jnp.full((R, C), tab_ref[row0 + sum(t <= n_lo for t in BUCKET_THRESHOLDS), hcol], F32)
        for k, thr in enumerate(BUCKET_THRESHOLDS, start=1):
            if n_lo < thr <= n_hi:
                val = jnp.where(n >= thr, tab_ref[row0 + k, hcol], val)
        return val

    for v in range(nvar):
        @pl.when(pl.program_id(0) == v)
        def _(v=v):
            off = off0 + v * off_step
            rel = off + base
            lo, hi = (off + span_lo) * dil, (off + span_hi) * dil
            dist = rel * dil
            n = jnp.abs(dist)
            if hi <= 0:
                val = side(n, -hi, -lo, 0)
            elif lo > 0:
                val = side(n, lo, hi, nb)
            else:
                val = jnp.where(dist > 0, side(n, 1, hi, nb), side(n, 0, -lo, 0))
            val = val * LOG2E
            if half_window is not None:
                val = jnp.where(jnp.abs(rel) <= half_window, val, NEG)
            o_ref[0, 0] = val


def bias_tiles(table, *, nvar, nheads, head0, rows, cols, off0, off_step, row_coef, col_coef,
               dil, half_window, name):
    kern = functools.partial(_bias_kernel, nvar=nvar, off0=off0, off_step=off_step, row_coef=row_coef,
                             col_coef=col_coef, dil=dil, half_window=half_window, head0=head0)
    return pl.pallas_call(
        kern,
        grid=(nvar, nheads),
        in_specs=[pl.BlockSpec(memory_space=pltpu.SMEM)],
        out_specs=pl.BlockSpec((1, 1, rows, cols), lambda v, h: (v, h, 0, 0)),
        out_shape=jax.ShapeDtypeStruct((nvar, nheads, rows, cols), F32),
        compiler_params=_params("parallel", "parallel"),
        name=name,
    )(table)


def _dense_pipeline(nk, tile, n_streams, score_fn, value_fn, m_s, acc_s, bufs):
    units = [(i, c * QCOLS) for i in range(n_streams) for c in range(tile // QCOLS)]

    def produce(kc, unit, nxt):
        i, c0 = unit
        cols = slice(c0, c0 + QCOLS)
        s = score_fn(kc + 1, i, cols)
        nxt[0][i, :, cols] = s
        nxt[1][i, :, cols] = jnp.max(s, axis=0, keepdims=True)

    def consume(kc, unit, cur):
        i, c0 = unit
        cols = slice(c0, c0 + QCOLS)
        m_old = m_s[i, :, cols]
        m_new = jnp.maximum(m_old, cur[1][i, :, cols])
        alpha = jnp.exp2(m_old - m_new)
        p = jnp.exp2((cur[0][i, :, cols] - m_new).astype(BF16))
        acc_s[i, :, cols] = (alpha * acc_s[i, :, cols]
                             + jnp.dot(value_fn(kc, i), p, preferred_element_type=F32))
        m_s[i, :, cols] = m_new

    def stage(kc, cur, nxt):
        for unit in units:
            if nxt is not None:
                produce(kc, unit, nxt)
            if cur is not None:
                consume(kc, unit, cur)

    m_s[...] = jnp.full(m_s.shape, NEG, F32)
    acc_s[...] = jnp.zeros(acc_s.shape, F32)
    stage(-1, None, bufs[0])
    n_loop = (nk - 1) // UNROLL

    def body(j, carry):
        for u in range(UNROLL):
            stage(UNROLL * j + u, bufs[u % 2], bufs[(u + 1) % 2])
        return carry

    lax.fori_loop(0, n_loop, body, 0)
    for kc in range(n_loop * UNROLL, nk):
        stage(kc, bufs[kc % 2], bufs[(kc + 1) % 2] if kc < nk - 1 else None)


def _attn_a_kernel(q_ref, k_ref, v_ref, bias_ref, lq1_ref, lk1_ref, lq2_ref, lk2_ref, subln_ref, o_ref,
                   m_s, acc_s, s_a, s_b, cm_a, cm_b, *, tile, nk, qtiles, npass, lambda_init):
    lam = (jnp.exp(jnp.sum(lq1_ref[...] * lk1_ref[...], axis=-1, keepdims=True))
           - jnp.exp(jnp.sum(lq2_ref[...] * lk2_ref[...], axis=-1, keepdims=True)) + lambda_init)

    def one_pass(p, carry):
        q0 = p * qtiles
        qi0 = pl.program_id(2) * (qtiles * npass) + q0

        def score_fn(kc, i, cols):
            j, qt = divmod(i, qtiles)
            kblk = k_ref[0, pl.ds(pl.multiple_of(kc * tile, tile), tile), :]
            bt = bias_ref[jnp.clip(kc - (qi0 + qt), -BIAS_REACH, BIAS_REACH) + BIAS_REACH, 0, :, cols]
            return jnp.dot(kblk, q_ref[0, 0, j, q0 + qt, :, cols], preferred_element_type=F32) + bt

        def value_fn(kc, i):
            return v_ref[0, 0, kc]

        _dense_pipeline(nk, tile, 2 * qtiles, score_fn, value_fn, m_s, acc_s, ((s_a, cm_a), (s_b, cm_b)))
        for qt in range(qtiles):
            a1, a2 = acc_s[qt], acc_s[qtiles + qt]
            o = (a1[:A_V_DIM] / a1[A_V_DIM:A_V_DIM + 1]
                 - lam * (a2[:A_V_DIM] / a2[A_V_DIM:A_V_DIM + 1]))
            ms = jnp.mean(o * o, axis=0, keepdims=True)
            y = o * lax.rsqrt(ms + EPS) * subln_ref[...] * (1.0 - lambda_init)
            rows = pl.ds(pl.multiple_of((q0 + qt) * tile, tile), tile)
            o_ref[0, rows] = y.T.astype(o_ref.dtype)
        return carry

    lax.fori_loop(0, npass, one_pass, 0)


def attn_a(qt, karr, vt, bias, lq1, lk1, lq2, lk2, subln, lambda_init):
    B, S, _ = karr.shape
    T = DENSE_TILE
    nq = nk = S // T
    assert nk % 2 == 0 and nk >= 4
    qtiles = math.gcd(nq, DENSE_QTILES)
    npass = math.gcd(nq // qtiles, A_PASSES)
    ns = 2 * qtiles
    kern = functools.partial(_attn_a_kernel, tile=T, nk=nk, qtiles=qtiles, npass=npass, lambda_init=lambda_init)
    vec = lambda n: pl.BlockSpec((1, n), lambda h, b, i: (0, 0))
    return pl.pallas_call(
        kern,
        grid=(A_HEADS, B, nq // (qtiles * npass)),
        in_specs=[pl.BlockSpec((1, 1, 2, qtiles * npass, LANE, T), lambda h, b, i: (b, h, 0, i, 0, 0)),
                  pl.BlockSpec((1, S, LANE), lambda h, b, i: (b, 0, h)),
                  pl.BlockSpec((1, 1, nk, V_ROWS_A, T), lambda h, b, i: (b, h, 0, 0, 0)),
                  pl.BlockSpec((2 * BIAS_REACH + 1, 1, T, T), lambda h, b, i: (0, h, 0, 0),
                               pipeline_mode=pl.Buffered(1)),
                  vec(HEAD_DIM), vec(HEAD_DIM), vec(HEAD_DIM), vec(HEAD_DIM),
                  pl.BlockSpec((A_V_DIM, 1), lambda h, b, i: (0, 0))],
        out_specs=pl.BlockSpec((1, qtiles * npass * T, LANE), lambda h, b, i: (b, i, h)),
        out_shape=jax.ShapeDtypeStruct((B, S, A_HEADS * A_V_DIM), BF16),
        scratch_shapes=[pltpu.VMEM((ns, 1, T), F32), pltpu.VMEM((ns, V_ROWS_A, T), F32),
                        pltpu.VMEM((ns, T, T), F32), pltpu.VMEM((ns, T, T), F32),
                        pltpu.VMEM((ns, 1, T), F32), pltpu.VMEM((ns, 1, T), F32)],
        compiler_params=_params("parallel", "parallel", "arbitrary"),
        name="attn_a",
    )(qt, karr, vt, bias, lq1.reshape(1, -1), lk1.reshape(1, -1), lq2.reshape(1, -1),
      lk2.reshape(1, -1), subln.reshape(-1, 1))


def _attn_d_kernel(q_ref, k_ref, v_ref, o_ref, m_s, acc_s, s_a, s_b, cm_a, cm_b, *, tile, nk, qtiles, npass):
    def one_pass(p, carry):
        q0 = p * qtiles

        def score_fn(kc, i, cols):
            hh, qt = divmod(i, qtiles)
            kblk = k_ref[0, pl.ds(pl.multiple_of(kc * tile, tile), tile), hh * LANE:(hh + 1) * LANE]
            return jnp.dot(kblk, q_ref[0, hh, q0 + qt, :, cols], preferred_element_type=F32)

        def value_fn(kc, i):
            return v_ref[0, i // qtiles, kc]

        _dense_pipeline(nk, tile, 2 * qtiles, score_fn, value_fn, m_s, acc_s, ((s_a, cm_a), (s_b, cm_b)))
        for qt in range(qtiles):
            outs = []
            for hh in range(2):
                acc = acc_s[hh * qtiles + qt]
                outs.append(acc[:D_V] / acc[D_V:D_V + 1])
            rows = pl.ds(pl.multiple_of((q0 + qt) * tile, tile), tile)
            o_ref[0, rows] = jnp.concatenate(outs, axis=0).T.astype(o_ref.dtype)
        return carry

    lax.fori_loop(0, npass, one_pass, 0)


def attn_d(qt, k, vt):
    B, S, _ = k.shape
    T = DENSE_TILE
    nq = nk = S // T
    qtiles = math.gcd(nq, DENSE_QTILES)
    npass = math.gcd(nq // qtiles, D_PASSES)
    ns = 2 * qtiles
    kern = functools.partial(_attn_d_kernel, tile=T, nk=nk, qtiles=qtiles, npass=npass)
    return pl.pallas_call(
        kern,
        grid=(B, D_HEADS // 2, nq // (qtiles * npass)),
        in_specs=[pl.BlockSpec((1, 2, qtiles * npass, LANE, T), lambda b, h, i: (b, h, i, 0, 0)),
                  pl.BlockSpec((1, S, 2 * LANE), lambda b, h, i: (b, 0, h)),
                  pl.BlockSpec((1, 2, nk, V_ROWS_D, T), lambda b, h, i: (b, h, 0, 0, 0))],
        out_specs=pl.BlockSpec((1, qtiles * npass * T, LANE), lambda b, h, i: (b, i, h)),
        out_shape=jax.ShapeDtypeStruct((B, S, D_HEADS * D_V), BF16),
        scratch_shapes=[pltpu.VMEM((ns, 1, T), F32), pltpu.VMEM((ns, V_ROWS_D, T), F32),
                        pltpu.VMEM((ns, T, T), F32), pltpu.VMEM((ns, T, T), F32),
                        pltpu.VMEM((ns, 1, T), F32), pltpu.VMEM((ns, 1, T), F32)],
        compiler_params=_params("parallel", "parallel", "arbitrary"),
        name="attn_d",
    )(qt, k, vt)


def _attn_b_kernel(q_ref, k_ref, v_ref, bias_ref, sink_ref, o_ref, s_a, s_b, s_c, cm_a, cm_b, cm_c, *, tile, seq_len):
    half_window = B_HALF_WINDOW
    span = QCOLS + 2 * half_window
    nchunk = span // LANE
    step = pl.program_id(1)

    def window(c):
        q0 = step * tile + c * QCOLS
        start = pl.multiple_of(jnp.clip(q0 - half_window, 0, seq_len - span), LANE)
        variant = jnp.where(q0 == 0, 0, jnp.where(q0 + QCOLS == seq_len, 2, 1))
        return start, variant

    def scores(task):
        c, j, g = task
        start, variant = window(c)
        kwin = k_ref[0, pl.ds(start, span), :]
        s = jnp.dot(kwin, q_ref[0, j, g, 0, :, c * QCOLS:(c + 1) * QCOLS], preferred_element_type=F32)
        return s + bias_ref[variant, j + (B_HEADS // 2) * g]

    def finish(task, s_ref, cm_ref):
        c, j, g = task
        head = j + (B_HEADS // 2) * g
        start, _ = window(c)
        chunk0 = start // LANE
        vwin = jnp.concatenate([v_ref[0, g, chunk0 + n] for n in range(nchunk)], axis=1)
        sk = sink_ref[:, head:head + 1] * LOG2E
        m = jnp.maximum(cm_ref[...], sk)
        e = jnp.exp2((s_ref[...] - m).astype(BF16))
        ov = jnp.dot(vwin, e, preferred_element_type=F32)
        denom = ov[HEAD_DIM:HEAD_DIM + 1] + jnp.exp2(sk - m)
        return ov[:HEAD_DIM] / denom

    tasks = [(c, j, g) for c in range(tile // QCOLS) for j in range(B_HEADS // 2) for g in range(B_KV_HEADS)]
    bufs = ((s_a, cm_a), (s_b, cm_b), (s_c, cm_c))

    def produce(task, buf):
        s = scores(task)
        buf[0][...] = s
        buf[1][...] = jnp.max(s, axis=0, keepdims=True)

    produce(tasks[0], bufs[0])
    produce(tasks[1], bufs[1])
    held = None
    for n, task in enumerate(tasks):
        if n + 2 < len(tasks):
            produce(tasks[n + 2], bufs[(n + 2) % 3])
        o = finish(task, *bufs[n % 3])
        c, j, g = task
        if g == 0:
            held = o
            continue
        pair = jnp.concatenate([held, o], axis=0).T
        o_ref[0, c * QCOLS:(c + 1) * QCOLS, j * LANE:(j + 1) * LANE] = pair.astype(o_ref.dtype)


def attn_b(qt, karr, k_block, vt, bias, sink):
    B, S, _ = karr.shape
    T = DENSE_TILE
    span = QCOLS + 2 * B_HALF_WINDOW
    return pl.pallas_call(
        functools.partial(_attn_b_kernel, tile=T, seq_len=S),
        grid=(B, S // T),
        in_specs=[pl.BlockSpec((1, B_HEADS // 2, 2, 1, LANE, T), lambda b, i: (b, 0, 0, i, 0, 0)),
                  pl.BlockSpec((1, S, LANE), lambda b, i: (b, 0, k_block)),
                  pl.BlockSpec((1, B_KV_HEADS, S // LANE, V_ROWS_D, LANE), lambda b, i: (b, 0, 0, 0, 0)),
                  pl.BlockSpec(bias.shape, lambda b, i: (0, 0, 0, 0)),
                  pl.BlockSpec((1, B_HEADS), lambda b, i: (0, 0))],
        out_specs=pl.BlockSpec((1, T, B_HEADS * HEAD_DIM), lambda b, i: (b, i, 0)),
        out_shape=jax.ShapeDtypeStruct((B, S, B_HEADS * HEAD_DIM), BF16),
        scratch_shapes=[pltpu.VMEM((span, QCOLS), F32)] * 3 + [pltpu.VMEM((1, QCOLS), F32)] * 3,
        compiler_params=_params("parallel", "arbitrary"),
        name="attn_b",
    )(qt, karr, vt, bias, sink.reshape(1, -1))


def _attn_c_kernel(q_ref, k_ref, v_ref, bias_ref, o_ref, lse_ref, *, tq, group, span, half_window, seq_len):
    nq = seq_len // tq
    nblk = q_ref.shape[-1] // LANE
    lane = lax.broadcasted_iota(jnp.int32, (tq, LANE), 1)
    low = lane < HEAD_DIM

    def window(g):
        t = pl.program_id(2) * group + g
        start = pl.multiple_of(jnp.clip(t * tq - half_window, 0, seq_len - span), half_window)
        variant = jnp.where(t == 0, 0, jnp.where(t == nq - 1, 2, 1))
        return start, variant

    def scores(task):
        g, c, half = task
        start, variant = window(g)
        q2 = q_ref[0, g * tq:(g + 1) * tq, c * LANE:(c + 1) * LANE]
        kb = k_ref[0, pl.ds(start, span), c * LANE:(c + 1) * LANE]
        qm = jnp.where(low if half == 0 else jnp.logical_not(low), q2, jnp.zeros_like(q2))
        s = lax.dot_general(qm, kb, (((1,), (1,)), ((), ())), preferred_element_type=F32)
        return s + bias_ref[variant, 2 * c + half]

    def finish(task, s):
        g, c, half = task
        start, _ = window(g)
        vb = v_ref[0, pl.ds(start, span), c * LANE:(c + 1) * LANE]
        m = jnp.max(s, axis=-1, keepdims=True)
        e = jnp.exp2(s - m)
        denom = jnp.sum(e, axis=-1, keepdims=True)
        o = jnp.dot(e.astype(BF16), vb, preferred_element_type=F32) / denom
        return o, LN2 * m + jnp.log(denom)

    tasks = [(g, c, half) for g in range(group) for c in range(nblk) for half in range(2)]
    s_next = scores(tasks[0])
    held = None
    for n, task in enumerate(tasks):
        s = s_next
        if n + 1 < len(tasks):
            s_next = scores(tasks[n + 1])
        o, lse = finish(task, s)
        g, c, half = task
        if half == 0:
            held = (o, lse)
            continue
        rows, cols = slice(g * tq, (g + 1) * tq), slice(c * LANE, (c + 1) * LANE)
        o_ref[0, rows, cols] = jnp.where(low, held[0], o)
        lse_ref[0, rows, cols] = jnp.where(low, held[1], lse)


def attn_c(view, bias, *, dil, tq, half_window, name):
    B, L, _ = view.shape
    width = C_HEADS_PER_GROUP * HEAD_DIM
    span = tq + 2 * half_window
    nq = L // tq
    assert L % tq == 0 and L >= span and nq >= 2
    group = math.gcd(nq, BAND_GROUP)
    kern = functools.partial(_attn_c_kernel, tq=tq, group=group, span=span, half_window=half_window, seq_len=L)
    out_spec = pl.BlockSpec((1, group * tq, width), lambda b, r, t: (b, t, r))
    out_shape = jax.ShapeDtypeStruct((B, L, dil * width), F32)
    return pl.pallas_call(
        kern,
        grid=(B, dil, nq // group),
        in_specs=[pl.BlockSpec((1, group * tq, width), lambda b, r, t: (b, t, 3 * r)),
                  pl.BlockSpec((1, L, width), lambda b, r, t: (b, 0, 3 * r + 1)),
                  pl.BlockSpec((1, L, width), lambda b, r, t: (b, 0, 3 * r + 2)),
                  pl.BlockSpec(bias.shape, lambda b, r, t: (0, 0, 0, 0))],
        out_specs=[out_spec, out_spec],
        out_shape=[out_shape, out_shape],
        compiler_params=_params("parallel", "parallel", "arbitrary"),
        name=name,
    )(view, view, view, bias)


def _out_proj_cd_kernel(h_ref, o0, o1, o2, s0, s1, s2, d_ref, wc_ref, wd_ref, o_ref, *scratch):
    tm = h_ref.shape[0]
    width = C_HEADS_PER_GROUP * HEAD_DIM
    spare = iter(scratch)

    def in_position_order(ref, dil):
        if dil == 1:
            return ref[0]
        t_s = next(spare)
        for r in range(dil):
            for j in range(width // LANE):
                t_s[j, pl.ds(r, tm // dil, stride=dil), :] = ref[0, :, r * width + j * LANE:r * width + (j + 1) * LANE]
        return jnp.concatenate([t_s[j] for j in range(width // LANE)], axis=1)

    dils = [d for _, d in C_PATTERNS]
    outs = [in_position_order(r, d) for r, d in zip((o0, o1, o2), dils)]
    lses = [in_position_order(r, d) for r, d in zip((s0, s1, s2), dils)]
    mx = jnp.maximum(jnp.maximum(lses[0], lses[1]), lses[2])
    es = [jnp.exp(l - mx) for l in lses]
    oc = (es[0] * outs[0] + es[1] * outs[1] + es[2] * outs[2]) / (es[0] + es[1] + es[2])
    mix = jnp.dot(oc.astype(BF16), wc_ref[...], preferred_element_type=F32)
    mix = mix + jnp.dot(d_ref[...], wd_ref[...], preferred_element_type=F32)
    o_ref[...] = h_ref[...] + mix


def out_proj_cd(h, oc, lses, od, wc, wd, batch, seq_len):
    M = h.shape[0]
    tm = ROW_TILE
    nrow = seq_len // tm
    width = C_HEADS_PER_GROUP * HEAD_DIM
    row = lambda a: pl.BlockSpec((tm, a.shape[1]), lambda i: (i, 0))
    full = lambda a: pl.BlockSpec(a.shape, lambda i: (0, 0))
    views = [pl.BlockSpec((1, tm // d, d * width), lambda i: (i // nrow, i % nrow, 0)) for _, d in C_PATTERNS]
    n_spare = 2 * sum(d > 1 for _, d in C_PATTERNS)
    return pl.pallas_call(
        _out_proj_cd_kernel,
        grid=(M // tm,),
        in_specs=[row(h), *views, *views, row(od), full(wc), full(wd)],
        out_specs=pl.BlockSpec((tm, D_MODEL), lambda i: (i, 0)),
        out_shape=jax.ShapeDtypeStruct((M, D_MODEL), F32),
        scratch_shapes=[pltpu.VMEM((width // LANE, tm, LANE), F32)] * n_spare,
        compiler_params=_params("parallel"),
        name="out_proj1",
    )(h, *oc, *lses, od, wc, wd)


def _proj1_kernel(x_ref, g_ref, w_ref, cs_ref, c0_ref, c1_ref, c2_ref, y_s):
    xn = _rms(x_ref[...], g_ref[...]).astype(BF16)
    y = jnp.dot(xn, w_ref[...], preferred_element_type=F32) * cs_ref[...]
    tm = y.shape[0]
    width = y.shape[1] // len(C_PATTERNS)
    c0_ref[0] = y[:, :width].astype(BF16)
    nblk = width // LANE
    for j in range(y_s.shape[0]):
        y_s[j] = y[:, width + j * LANE:width + (j + 1) * LANE]
    for g, ref in ((1, c1_ref), (2, c2_ref)):
        dil = C_PATTERNS[g][1]
        for r in range(dil):
            for j in range(nblk):
                rows = y_s[(g - 1) * nblk + j, pl.ds(r, tm // dil, stride=dil), :]
                ref[0, :, r * width + j * LANE:r * width + (j + 1) * LANE] = rows.astype(BF16)


def proj1_call(x, g, w, colscale, batch, seq_len):
    M, K = x.shape
    N = w.shape[1]
    tm = ROW_TILE
    nrow = seq_len // tm
    width = N // len(C_PATTERNS)
    assert [d for _, d in C_PATTERNS][0] == 1
    out_specs = [pl.BlockSpec((1, tm // d, d * width), lambda i: (i // nrow, i % nrow, 0)) for _, d in C_PATTERNS]
    out_shape = [jax.ShapeDtypeStruct((batch, seq_len // d, d * width), BF16) for _, d in C_PATTERNS]
    return pl.pallas_call(
        _proj1_kernel,
        grid=(M // tm,),
        in_specs=[pl.BlockSpec((tm, K), lambda i: (i, 0)),
                  pl.BlockSpec((1, K), lambda i: (0, 0)),
                  pl.BlockSpec((K, N), lambda i: (0, 0)),
                  pl.BlockSpec((1, N), lambda i: (0, 0))],
        out_specs=out_specs,
        out_shape=out_shape,
        scratch_shapes=[pltpu.VMEM(((N - width) // LANE, tm, LANE), F32)],
        compiler_params=_params("parallel"),
        name="proj1",
    )(x, g.reshape(1, K), w, colscale.reshape(1, N))


def _ffn_kernel(*refs, n_mix, final_norm):
    x_ref, mix_refs = refs[0], refs[1:1 + 2 * n_mix]
    g_ref, wg_ref, wu_ref, wd_ref, fg_ref, o_ref, xn_s, acc_s, h_s = refs[1 + 2 * n_mix:]
    f = pl.program_id(1)

    @pl.when(f == 0)
    def _():
        h = x_ref[...]
        for a_ref, w_ref in zip(mix_refs[::2], mix_refs[1::2]):
            h = h + jnp.dot(a_ref[...], w_ref[...], preferred_element_type=F32)
        h_s[...] = h
        xn_s[...] = _rms(h, g_ref[...]).astype(BF16)
        acc_s[...] = jnp.zeros(acc_s.shape, F32)

    xn = xn_s[...]
    gate = jnp.dot(xn, wg_ref[0].astype(BF16), preferred_element_type=F32)
    up = jnp.dot(xn, wu_ref[0].astype(BF16), preferred_element_type=F32)
    mid = (gate / (1.0 + jnp.exp(-gate)) * up).astype(BF16)
    acc_s[...] += jnp.dot(mid, wd_ref[0].astype(BF16), preferred_element_type=F32)

    @pl.when(f == pl.num_programs(1) - 1)
    def _():
        y = h_s[...] + acc_s[...]
        if final_norm:
            y = _rms(y, fg_ref[...])
        o_ref[...] = y


def ffn(x, mix, g, wg, wu, wd, layer, fg, final_norm, name):
    M, K = x.shape
    tm, tf = FFN_ROW_TILE, FFN_COL_TILE
    kern = functools.partial(_ffn_kernel, n_mix=len(mix), final_norm=final_norm)
    mix_specs, mix_args = [], []
    for a, w in mix:
        mix_specs += [pl.BlockSpec((tm, a.shape[1]), lambda i, f: (i, 0)), pl.BlockSpec(w.shape, lambda i, f: (0, 0))]
        mix_args += [a, w]
    return pl.pallas_call(
        kern,
        grid=(M // tm, D_FF // tf),
        in_specs=[pl.BlockSpec((tm, K), lambda i, f: (i, 0)), *mix_specs,
                  pl.BlockSpec((1, K), lambda i, f: (0, 0)),
                  pl.BlockSpec((1, K, tf), lambda i, f: (layer, 0, f)),
                  pl.BlockSpec((1, K, tf), lambda i, f: (layer, 0, f)),
                  pl.BlockSpec((1, tf, K), lambda i, f: (layer, f, 0)),
                  pl.BlockSpec((1, K), lambda i, f: (0, 0))],
        out_specs=pl.BlockSpec((tm, K), lambda i, f: (i, 0)),
        out_shape=jax.ShapeDtypeStruct((M, K), F32),
        scratch_shapes=[pltpu.VMEM((tm, K), BF16), pltpu.VMEM((tm, K), F32), pltpu.VMEM((tm, K), F32)],
        compiler_params=_params("parallel", "arbitrary"),
        name=name,
    )(x, *mix_args, g.reshape(1, K), wg, wu, wd, fg.reshape(1, K))


def _trig_kernel(ang_ref, cos_ref, sin_ref):
    a = ang_ref[...]
    cos_ref[...] = jnp.cos(a)
    sin_ref[...] = jnp.sin(a)


def rope_tables(seq_len):
    half = D_ROPE // 2
    inv = ROPE_THETA ** (-jnp.arange(half, dtype=F32) / half)
    ang = jnp.arange(seq_len).astype(F32)[:, None] * inv[None, :]
    dense = ang.reshape(seq_len * half // LANE, LANE)
    spec = pl.BlockSpec(dense.shape, lambda: (0, 0))
    cos, sin = pl.pallas_call(
        _trig_kernel,
        in_specs=[spec],
        out_specs=[spec, spec],
        out_shape=[jax.ShapeDtypeStruct(dense.shape, F32)] * 2,
        name="rope_trig",
    )(dense)
    cos, sin = cos.reshape(seq_len, half), sin.reshape(seq_len, half)
    pad = LANE - D_NOPE - D_ROPE
    cos_l = jnp.concatenate([jnp.ones((seq_len, D_NOPE), F32), cos, cos, jnp.ones((seq_len, pad), F32)], axis=1)
    sin_l = jnp.concatenate([jnp.zeros((seq_len, D_NOPE), F32), sin, sin, jnp.zeros((seq_len, pad), F32)], axis=1)
    return cos_l, sin_l


def _prep_d_kernel(x_ref, g_ref, wa_ref, qn_ref, kvn_ref, wq_ref, wkv_ref, cos_ref, sin_ref,
                   q_ref, k_ref, v_ref, *, qscale):
    xn = _rms(x_ref[...], g_ref[...]).astype(BF16)
    lat = jnp.dot(xn, wa_ref[...], preferred_element_type=F32)
    cq = _rms(lat[:, :D_Q_LORA], qn_ref[...]).astype(BF16)
    ckv = _rms(lat[:, D_Q_LORA:D_Q_LORA + D_KV_LORA], kvn_ref[...]).astype(BF16)
    o_pe = D_Q_LORA + D_KV_LORA
    cos, sin = cos_ref[...], sin_ref[...]
    kpe = lat[:, o_pe:o_pe + LANE] * cos + lat[:, o_pe + LANE:o_pe + 2 * LANE] * sin
    qq = jnp.dot(cq, wq_ref[...], preferred_element_type=F32)
    kv = jnp.dot(ckv, wkv_ref[...], preferred_element_type=F32)
    kw = D_HEADS * LANE
    ones_blk = _ones_row_block(V_ROWS_D - D_V, x_ref.shape[0])
    for h in range(D_HEADS):
        qh = qq[:, h * LANE:(h + 1) * LANE] * cos + qq[:, kw + h * LANE:kw + (h + 1) * LANE] * sin
        q_ref[0, h, 0] = (qh * qscale).T.astype(BF16)
        k_ref[:, h * LANE:(h + 1) * LANE] = (kv[:, h * LANE:(h + 1) * LANE] + kpe).astype(BF16)
    for j in range(D_HEADS // 2):
        vt = kv[:, kw + j * LANE:kw + (j + 1) * LANE].T.astype(BF16)
        for half in range(2):
            v_ref[0, 2 * j + half, 0, :D_V] = vt[half * D_V:(half + 1) * D_V]
            v_ref[0, 2 * j + half, 0, D_V:] = ones_blk


def prep_d(x, g, wa, qn, kvn, wq, wkv, cos_l, sin_l, batch, seq_len):
    M, K = x.shape
    T = DENSE_TILE
    nrow = seq_len // T
    kw = D_HEADS * LANE
    kern = functools.partial(_prep_d_kernel, qscale=(D_NOPE + D_ROPE) ** -0.5 * LOG2E)
    full = lambda a: pl.BlockSpec(a.shape, lambda i: (0, 0))
    qn2, kvn2, g2 = qn.reshape(1, -1), kvn.reshape(1, -1), g.reshape(1, K)
    return pl.pallas_call(
        kern,
        grid=(M // T,),
        in_specs=[pl.BlockSpec((T, K), lambda i: (i, 0)), full(g2), full(wa), full(qn2), full(kvn2),
                  full(wq), full(wkv),
                  pl.BlockSpec((T, LANE), lambda i: (i % nrow, 0)),
                  pl.BlockSpec((T, LANE), lambda i: (i % nrow, 0))],
        out_specs=[pl.BlockSpec((1, D_HEADS, 1, LANE, T), lambda i: (i // nrow, 0, i % nrow, 0, 0)),
                   pl.BlockSpec((T, kw), lambda i: (i, 0)),
                   pl.BlockSpec((1, D_HEADS, 1, V_ROWS_D, T), lambda i: (i // nrow, 0, i % nrow, 0, 0))],
        out_shape=[jax.ShapeDtypeStruct((batch, D_HEADS, nrow, LANE, T), BF16),
                   jax.ShapeDtypeStruct((M, kw), BF16),
                   jax.ShapeDtypeStruct((batch, D_HEADS, nrow, V_ROWS_D, T), BF16)],
        compiler_params=_params("parallel"),
        name="prep_d",
    )(x, g2, wa, qn2, kvn2, wq, wkv, cos_l, sin_l)


B_HEAD_ORDER = (0, 4, 1, 5, 2, 6, 3, 7)


def _blocks(w, starts, width, axis):
    return jnp.concatenate([lax.slice_in_dim(w, s, s + width, axis=axis) for s in starts], axis=axis)


def _rot_partner_cols(w):
    half = D_ROPE // 2
    return jnp.concatenate([-w[..., half:], w[..., :half]], axis=-1)


def kernel(x, bias_table, attn_norm, ffn_norm, final_norm, ab_w_in, ab_lambda_q1, ab_lambda_k1,
           ab_lambda_q2, ab_lambda_k2, ab_subln, ab_sink, ab_w_o, cd_w_in, cd_q_norm, cd_w_q_b,
           cd_kv_norm, cd_w_kv_b, cd_w_o, ffn_w_gate, ffn_w_up, ffn_w_down):
    B, S, _ = x.shape
    M = B * S
    T = DENSE_TILE
    h = x.reshape(M, D_MODEL)
    qk_scale = HEAD_DIM ** -0.5 * LOG2E

    o3 = A_HEADS * (2 * A_QK_DIM + A_V_DIM)
    a0 = ab_w_in[0].astype(BF16)
    w0 = jnp.concatenate([a0[:, :o3], _blocks(a0, [o3 + hd * HEAD_DIM for hd in B_HEAD_ORDER], HEAD_DIM, 1),
                          a0[:, o3 + B_HEADS * HEAD_DIM:]], axis=1)
    cs0 = np.ones((AB_IN,), np.float32)
    cs0[:A_HEADS * A_QK_DIM] = qk_scale
    cs0[o3:o3 + B_HEADS * HEAD_DIM] = qk_scale
    qat, vat, qbt, vbt, keys0 = proj0_call(h, attn_norm[0], w0, jnp.asarray(cs0), B, S)
    keys0 = keys0.reshape(B, S, -1)
    bias_a = bias_tiles(bias_table, nvar=2 * BIAS_REACH + 1, nheads=A_HEADS, head0=0, rows=T, cols=T,
                        off0=-BIAS_REACH * T, off_step=T, row_coef=1, col_coef=-1, dil=1,
                        half_window=None, name="bias_a")
    oa = attn_a(qat, keys0, vat, bias_a, ab_lambda_q1[0], ab_lambda_k1[0], ab_lambda_q2[0],
                ab_lambda_k2[0], ab_subln[0], 0.8 - 0.6 * math.exp(-0.3 * 0))

    bias_b = bias_tiles(bias_table, nvar=3, nheads=B_HEADS, head0=A_HEADS, rows=QCOLS + 2 * B_HALF_WINDOW,
                        cols=QCOLS, off0=0, off_step=-B_HALF_WINDOW, row_coef=1, col_coef=-1, dil=1,
                        half_window=B_HALF_WINDOW, name="bias_b")
    ob = attn_b(qbt, keys0, (A_HEADS * A_QK_DIM) // LANE, vbt, bias_b, ab_sink[0])

    wo = ab_w_o[0].astype(BF16)
    wo_a = wo[:A_HEADS * A_V_DIM]
    wo_b = _blocks(wo, [A_HEADS * A_V_DIM + hd * HEAD_DIM for hd in B_HEAD_ORDER], HEAD_DIM, 0)
    h = ffn(h, [(oa.reshape(M, -1), wo_a), (ob.reshape(M, -1), wo_b)], ffn_norm[0], ffn_w_gate, ffn_w_up,
            ffn_w_down, 0, final_norm, False, "ffn0")

    w1 = cd_w_in[0]
    gw = C_HEADS_PER_GROUP * HEAD_DIM
    cw = C_HEADS * HEAD_DIM
    starts1 = [role * cw + g * gw for g in range(len(C_PATTERNS)) for role in range(3)]
    cs1 = np.ones((CD_C_IN,), np.float32)
    for g in range(len(C_PATTERNS)):
        cs1[3 * g * gw:(3 * g + 1) * gw] = qk_scale
    c_views = proj1_call(h, attn_norm[1], _blocks(w1.astype(BF16), starts1, gw, 1), jnp.asarray(cs1), B, S)

    oc, lses = [], []
    for g, (window, dil) in enumerate(C_PATTERNS):
        hw = window // (2 * dil)
        tq_c = 128
        bias_c = bias_tiles(bias_table, nvar=3, nheads=C_HEADS_PER_GROUP, head0=g * C_HEADS_PER_GROUP,
                            rows=tq_c, cols=tq_c + 2 * hw, off0=0, off_step=-hw, row_coef=-1, col_coef=1,
                            dil=dil, half_window=hw, name=f"bias_c{g}")
        o_g, lse_g = attn_c(c_views[g], bias_c, dil=dil, tq=tq_c, half_window=hw, name=f"attn_c{g}")
        oc.append(o_g)
        lses.append(lse_g)

    o_kv = CD_C_IN + D_Q_LORA + D_KV_LORA
    w_pe = w1[:, o_kv:]
    lane_pad = lambda w: jnp.pad(w, ((0, 0), (D_NOPE, LANE - D_NOPE - D_ROPE)))
    wa = jnp.concatenate([w1[:, CD_C_IN:o_kv], lane_pad(w_pe), lane_pad(_rot_partner_cols(w_pe))],
                         axis=1).astype(BF16)
    wq3 = cd_w_q_b[0].reshape(D_Q_LORA, D_HEADS, D_NOPE + D_ROPE)
    zpad = jnp.zeros((D_Q_LORA, D_HEADS, LANE - D_NOPE - D_ROPE), F32)
    wq_main = jnp.concatenate([wq3, zpad], axis=-1)
    wq_rot = jnp.concatenate([jnp.zeros_like(wq3[..., :D_NOPE]), _rot_partner_cols(wq3[..., D_NOPE:]), zpad],
                             axis=-1)
    wq = jnp.concatenate([wq_main.reshape(D_Q_LORA, -1), wq_rot.reshape(D_Q_LORA, -1)], axis=1).astype(BF16)
    wkv3 = cd_w_kv_b[0].reshape(D_KV_LORA, D_HEADS, D_NOPE + D_V)
    wk = jnp.pad(wkv3[..., :D_NOPE], ((0, 0), (0, 0), (0, LANE - D_NOPE))).reshape(D_KV_LORA, -1)
    wv = wkv3[..., D_NOPE:].reshape(D_KV_LORA, -1)
    wkv = jnp.concatenate([wk, wv], axis=1).astype(BF16)
    cos_l, sin_l = rope_tables(S)
    qdt, kd, vdt = prep_d(h, attn_norm[1], wa, cd_q_norm[0], cd_kv_norm[0], wq, wkv, cos_l, sin_l, B, S)
    od = attn_d(qdt, kd.reshape(B, S, D_HEADS * LANE), vdt)

    wo1 = cd_w_o[0]
    wo_c = wo1[:C_HEADS_PER_GROUP * HEAD_DIM].astype(BF16)
    wo_d = wo1[C_HEADS_PER_GROUP * HEAD_DIM:].astype(BF16)
    h = out_proj_cd(h, oc, lses, od.reshape(M, -1), wo_c, wo_d, B, S)
    h = ffn(h, [], ffn_norm[1], ffn_w_gate, ffn_w_up, ffn_w_down, 1, final_norm, True, "ffn1")
    return h.reshape(B, S, D_MODEL)
```

```python
import functools
import math

import numpy as np
import jax
import jax.numpy as jnp
from jax import lax
from jax.experimental import pallas as pl
from jax.experimental.pallas import tpu as pltpu

F32 = jnp.float32
BF16 = jnp.bfloat16

D_MODEL = 1024
HEAD_DIM = 64
EPS = 1e-6
NEG = -1e30
LOG2E = math.log2(math.e)
LN2 = math.log(2.0)

A_HEADS = 4
A_QK_DIM = 2 * HEAD_DIM
A_V_DIM = 2 * HEAD_DIM
B_HEADS = 8
B_KV_HEADS = 2
B_HALF_WINDOW = 128
C_PATTERNS = ((128, 1), (512, 4), (2048, 16))
C_HEADS_PER_GROUP = 4
C_HEADS = C_HEADS_PER_GROUP * len(C_PATTERNS)
D_HEADS = 12
D_Q_LORA = 384
D_KV_LORA = 256
D_NOPE = 64
D_ROPE = 32
D_V = 64
ROPE_THETA = 10000.0
NUM_BUCKETS = 32
MAX_DISTANCE = 1024
D_FF = 2816
AB_IN = 2304
CD_C_IN = 3 * C_HEADS * HEAD_DIM

LANE = 128
VMEM_LIMIT = 48 * 1024 * 1024

ROW_TILE = 512
FFN_ROW_TILE = 1024
FFN_COL_TILE = 256
DENSE_TILE = 512
QCOLS = 256
UNROLL = 6
DENSE_QTILES = 4
A_PASSES = 2
D_PASSES = 4
BAND_GROUP = 8
BIAS_REACH = 3
V_ROWS_D = 80
V_ROWS_A = 144


def _bucket_thresholds():
    nb = NUM_BUCKETS // 2
    max_exact = nb // 2
    n = np.arange(1, 4 * MAX_DISTANCE)
    large = max_exact + (np.log(n.astype(np.float32) / np.float32(max_exact))
                         / np.float32(math.log(MAX_DISTANCE / max_exact))
                         * np.float32(nb - max_exact)).astype(np.int32)
    mag = np.where(n < max_exact, n, np.minimum(large, nb - 1))
    return tuple(int(n[np.argmax(mag >= k)]) for k in range(1, nb))


BUCKET_THRESHOLDS = _bucket_thresholds()
assert BUCKET_THRESHOLDS[-1] <= (BIAS_REACH - 1) * DENSE_TILE + 1


def _params(*sem):
    return pltpu.CompilerParams(dimension_semantics=sem, vmem_limit_bytes=VMEM_LIMIT)


def _rms(x, g):
    return x * lax.rsqrt(jnp.mean(x * x, axis=-1, keepdims=True) + EPS) * g


def _ones_row_block(rows, cols):
    r = lax.broadcasted_iota(jnp.int32, (rows, cols), 0)
    return jnp.where(r == 0, 1.0, 0.0).astype(BF16)


def _store_masked_halves(q_ref, idx, qt):
    zeros = jnp.zeros((HEAD_DIM, qt.shape[1]), BF16)
    q_ref[idx + (0, 0, slice(None, HEAD_DIM))] = qt[:HEAD_DIM]
    q_ref[idx + (0, 0, slice(HEAD_DIM, None))] = zeros
    q_ref[idx + (1, 0, slice(None, HEAD_DIM))] = zeros
    q_ref[idx + (1, 0, slice(HEAD_DIM, None))] = qt[HEAD_DIM:]


def _proj0_kernel(x_ref, g_ref, w_ref, cs_ref, qa_ref, va_ref, qb_ref, vb_ref, k_ref):
    xn = _rms(x_ref[...], g_ref[...]).astype(BF16)
    y = jnp.dot(xn, w_ref[...], preferred_element_type=F32) * cs_ref[...]
    tm = y.shape[0]
    ka0 = A_HEADS * A_QK_DIM
    va0 = 2 * ka0
    qb0 = va0 + A_HEADS * A_V_DIM
    kb0 = qb0 + B_HEADS * HEAD_DIM
    vb0 = kb0 + B_KV_HEADS * HEAD_DIM
    for h in range(A_HEADS):
        _store_masked_halves(qa_ref, (0, h), y[:, h * A_QK_DIM:(h + 1) * A_QK_DIM].T.astype(BF16))
        va_ref[0, h, 0, :A_V_DIM] = y[:, va0 + h * A_V_DIM:va0 + (h + 1) * A_V_DIM].T.astype(BF16)
        va_ref[0, h, 0, A_V_DIM:] = _ones_row_block(V_ROWS_A - A_V_DIM, tm)
    for j in range(B_HEADS // 2):
        _store_masked_halves(qb_ref, (0, j), y[:, qb0 + j * LANE:qb0 + (j + 1) * LANE].T.astype(BF16))
    vbt = y[:, vb0:vb0 + LANE].T.astype(BF16)
    ones_blk = _ones_row_block(V_ROWS_D - HEAD_DIM, LANE)
    for g in range(B_KV_HEADS):
        for c in range(tm // LANE):
            vb_ref[0, g, c, :HEAD_DIM] = vbt[g * HEAD_DIM:(g + 1) * HEAD_DIM, c * LANE:(c + 1) * LANE]
            vb_ref[0, g, c, HEAD_DIM:] = ones_blk
    k_ref[:, :ka0] = y[:, ka0:va0].astype(BF16)
    k_ref[:, ka0:] = y[:, kb0:vb0].astype(BF16)


def proj0_call(x, g, w, colscale, batch, seq_len):
    M, K = x.shape
    N = w.shape[1]
    T = DENSE_TILE
    nrow = seq_len // T
    nkb = A_HEADS * A_QK_DIM + B_KV_HEADS * HEAD_DIM
    qspec = pl.BlockSpec((1, A_HEADS, 2, 1, LANE, T), lambda i: (i // nrow, 0, 0, i % nrow, 0, 0))
    qshape = jax.ShapeDtypeStruct((batch, A_HEADS, 2, nrow, LANE, T), BF16)
    return pl.pallas_call(
        _proj0_kernel,
        grid=(M // T,),
        in_specs=[pl.BlockSpec((T, K), lambda i: (i, 0)),
                  pl.BlockSpec((1, K), lambda i: (0, 0)),
                  pl.BlockSpec((K, N), lambda i: (0, 0)),
                  pl.BlockSpec((1, N), lambda i: (0, 0))],
        out_specs=[qspec,
                   pl.BlockSpec((1, A_HEADS, 1, V_ROWS_A, T), lambda i: (i // nrow, 0, i % nrow, 0, 0)),
                   qspec,
                   pl.BlockSpec((1, B_KV_HEADS, T // LANE, V_ROWS_D, LANE),
                                lambda i: (i // nrow, 0, i % nrow, 0, 0)),
                   pl.BlockSpec((T, nkb), lambda i: (i, 0))],
        out_shape=[qshape,
                   jax.ShapeDtypeStruct((batch, A_HEADS, nrow, V_ROWS_A, T), BF16),
                   qshape,
                   jax.ShapeDtypeStruct((batch, B_KV_HEADS, seq_len // LANE, V_ROWS_D, LANE), BF16),
                   jax.ShapeDtypeStruct((M, nkb), BF16)],
        compiler_params=_params("parallel"),
        name="proj0",
    )(x, g.reshape(1, K), w, colscale.reshape(1, N))


def _bias_kernel(tab_ref, o_ref, *, nvar, off0, off_step, row_coef, col_coef, dil, half_window, head0):
    hcol = head0 + pl.program_id(1)
    R, C = o_ref.shape[-2:]
    row = lax.broadcasted_iota(jnp.int32, (R, C), 0)
    col = lax.broadcasted_iota(jnp.int32, (R, C), 1)
    base = row_coef * row + col_coef * col
    span_lo = min(row_coef * (R - 1), 0) + min(col_coef * (C - 1), 0)
    span_hi = max(row_coef * (R - 1), 0) + max(col_coef * (C - 1), 0)
    nb = NUM_BUCKETS // 2

    def side(n, n_lo, n_hi, row0):
        val = jnp.full((R, C), tab_ref[row0 + sum(t <= n_lo for t in BUCKET_THRESHOLDS), hcol], F32)
        for k, thr in enumerate(BUCKET_THRESHOLDS, start=1):
            if n_lo < thr <= n_hi:
                val = jnp.where(n >= thr, tab_ref[row0 + k, hcol], val)
        return val

    for v in range(nvar):
        @pl.when(pl.program_id(0) == v)
        def _(v=v):
            off = off0 + v * off_step
            rel = off + base
            lo, hi = (off + span_lo) * dil, (off + span_hi) * dil
            dist = rel * dil
            n = jnp.abs(dist)
            if hi <= 0:
                val = side(n, -hi, -lo, 0)
            elif lo > 0:
                val = side(n, lo, hi, nb)
            else:
                val = jnp.where(dist > 0, side(n, 1, hi, nb), side(n, 0, -lo, 0))
            val = val * LOG2E
            if half_window is not None:
                val = jnp.where(jnp.abs(rel) <= half_window, val, NEG)
            o_ref[0, 0] = val


def bias_tiles(table, *, nvar, nheads, head0, rows, cols, off0, off_step, row_coef, col_coef,
               dil, half_window, name):
    kern = functools.partial(_bias_kernel, nvar=nvar, off0=off0, off_step=off_step, row_coef=row_coef,
                             col_coef=col_coef, dil=dil, half_window=half_window, head0=head0)
    return pl.pallas_call(
        kern,
        grid=(nvar, nheads),
        in_specs=[pl.BlockSpec(memory_space=pltpu.SMEM)],
        out_specs=pl.BlockSpec((1, 1, rows, cols), lambda v, h: (v, h, 0, 0)),
        out_shape=jax.ShapeDtypeStruct((nvar, nheads, rows, cols), F32),
        compiler_params=_params("parallel", "parallel"),
        name=name,
    )(table)


def _dense_pipeline(nk, tile, n_streams, score_fn, value_fn, m_s, acc_s, bufs):
    units = [(i, c * QCOLS) for i in range(n_streams) for c in range(tile // QCOLS)]

    def produce(kc, unit, nxt):
        i, c0 = unit
        cols = slice(c0, c0 + QCOLS)
        s = score_fn(kc + 1, i, cols)
        nxt[0][i, :, cols] = s
        nxt[1][i, :, cols] = jnp.max(s, axis=0, keepdims=True)

    def consume(kc, unit, cur):
        i, c0 = unit
        cols = slice(c0, c0 + QCOLS)
        m_old = m_s[i, :, cols]
        m_new = jnp.maximum(m_old, cur[1][i, :, cols])
        alpha = jnp.exp2(m_old - m_new)
        p = jnp.exp2((cur[0][i, :, cols] - m_new).astype(BF16))
        acc_s[i, :, cols] = (alpha * acc_s[i, :, cols]
                             + jnp.dot(value_fn(kc, i), p, preferred_element_type=F32))
        m_s[i, :, cols] = m_new

    def stage(kc, cur, nxt):
        for unit in units:
            if nxt is not None:
                produce(kc, unit, nxt)
            if cur is not None:
                consume(kc, unit, cur)

    m_s[...] = jnp.full(m_s.shape, NEG, F32)
    acc_s[...] = jnp.zeros(acc_s.shape, F32)
    stage(-1, None, bufs[0])
    n_loop = (nk - 1) // UNROLL

    def body(j, carry):
        for u in range(UNROLL):
            stage(UNROLL * j + u, bufs[u % 2], bufs[(u + 1) % 2])
        return carry

    lax.fori_loop(0, n_loop, body, 0)
    for kc in range(n_loop * UNROLL, nk):
        stage(kc, bufs[kc % 2], bufs[(kc + 1) % 2] if kc < nk - 1 else None)


def _attn_a_kernel(q_ref, k_ref, v_ref, bias_ref, lq1_ref, lk1_ref, lq2_ref, lk2_ref, subln_ref, o_ref,
                   m_s, acc_s, s_a, s_b, cm_a, cm_b, *, tile, nk, qtiles, npass, lambda_init):
    lam = (jnp.exp(jnp.sum(lq1_ref[...] * lk1_ref[...], axis=-1, keepdims=True))
           - jnp.exp(jnp.sum(lq2_ref[...] * lk2_ref[...], axis=-1, keepdims=True)) + lambda_init)

    def one_pass(p, carry):
        q0 = p * qtiles
        qi0 = pl.program_id(2) * (qtiles * npass) + q0

        def score_fn(kc, i, cols):
            j, qt = divmod(i, qtiles)
            kblk = k_ref[0, pl.ds(pl.multiple_of(kc * tile, tile), tile), :]
            bt = bias_ref[jnp.clip(kc - (qi0 + qt), -BIAS_REACH, BIAS_REACH) + BIAS_REACH, 0, :, cols]
            return jnp.dot(kblk, q_ref[0, 0, j, q0 + qt, :, cols], preferred_element_type=F32) + bt

        def value_fn(kc, i):
            return v_ref[0, 0, kc]

        _dense_pipeline(nk, tile, 2 * qtiles, score_fn, value_fn, m_s, acc_s, ((s_a, cm_a), (s_b, cm_b)))
        for qt in range(qtiles):
            a1, a2 = acc_s[qt], acc_s[qtiles + qt]
            o = (a1[:A_V_DIM] / a1[A_V_DIM:A_V_DIM + 1]
                 - lam * (a2[:A_V_DIM] / a2[A_V_DIM:A_V_DIM + 1]))
            ms = jnp.mean(o * o, axis=0, keepdims=True)
            y = o * lax.rsqrt(ms + EPS) * subln_ref[...] * (1.0 - lambda_init)
            rows = pl.ds(pl.multiple_of((q0 + qt) * tile, tile), tile)
            o_ref[0, rows] = y.T.astype(o_ref.dtype)
        return carry

    lax.fori_loop(0, npass, one_pass, 0)


def attn_a(qt, karr, vt, bias, lq1, lk1, lq2, lk2, subln, lambda_init):
    B, S, _ = karr.shape
    T = DENSE_TILE
    nq = nk = S // T
    assert nk % 2 == 0 and nk >= 4
    qtiles = math.gcd(nq, DENSE_QTILES)
    npass = math.gcd(nq // qtiles, A_PASSES)
    ns = 2 * qtiles
    kern = functools.partial(_attn_a_kernel, tile=T, nk=nk, qtiles=qtiles, npass=npass, lambda_init=lambda_init)
    vec = lambda n: pl.BlockSpec((1, n), lambda h, b, i: (0, 0))
    return pl.pallas_call(
        kern,
        grid=(A_HEADS, B, nq // (qtiles * npass)),
        in_specs=[pl.BlockSpec((1, 1, 2, qtiles * npass, LANE, T), lambda h, b, i: (b, h, 0, i, 0, 0)),
                  pl.BlockSpec((1, S, LANE), lambda h, b, i: (b, 0, h)),
                  pl.BlockSpec((1, 1, nk, V_ROWS_A, T), lambda h, b, i: (b, h, 0, 0, 0)),
                  pl.BlockSpec((2 * BIAS_REACH + 1, 1, T, T), lambda h, b, i: (0, h, 0, 0),
                               pipeline_mode=pl.Buffered(1)),
                  vec(HEAD_DIM), vec(HEAD_DIM), vec(HEAD_DIM), vec(HEAD_DIM),
                  pl.BlockSpec((A_V_DIM, 1), lambda h, b, i: (0, 0))],
        out_specs=pl.BlockSpec((1, qtiles * npass * T, LANE), lambda h, b, i: (b, i, h)),
        out_shape=jax.ShapeDtypeStruct((B, S, A_HEADS * A_V_DIM), BF16),
        scratch_shapes=[pltpu.VMEM((ns, 1, T), F32), pltpu.VMEM((ns, V_ROWS_A, T), F32),
                        pltpu.VMEM((ns, T, T), F32), pltpu.VMEM((ns, T, T), F32),
                        pltpu.VMEM((ns, 1, T), F32), pltpu.VMEM((ns, 1, T), F32)],
        compiler_params=_params("parallel", "parallel", "arbitrary"),
        name="attn_a",
    )(qt, karr, vt, bias, lq1.reshape(1, -1), lk1.reshape(1, -1), lq2.reshape(1, -1),
      lk2.reshape(1, -1), subln.reshape(-1, 1))


def _attn_d_kernel(q_ref, k_ref, v_ref, o_ref, m_s, acc_s, s_a, s_b, cm_a, cm_b, *, tile, nk, qtiles, npass):
    def one_pass(p, carry):
        q0 = p * qtiles

        def score_fn(kc, i, cols):
            hh, qt = divmod(i, qtiles)
            kblk = k_ref[0, pl.ds(pl.multiple_of(kc * tile, tile), tile), hh * LANE:(hh + 1) * LANE]
            return jnp.dot(kblk, q_ref[0, hh, q0 + qt, :, cols], preferred_element_type=F32)

        def value_fn(kc, i):
            return v_ref[0, i // qtiles, kc]

        _dense_pipeline(nk, tile, 2 * qtiles, score_fn, value_fn, m_s, acc_s, ((s_a, cm_a), (s_b, cm_b)))
        for qt in range(qtiles):
            outs = []
            for hh in range(2):
                acc = acc_s[hh * qtiles + qt]
                outs.append(acc[:D_V] / acc[D_V:D_V + 1])
            rows = pl.ds(pl.multiple_of((q0 + qt) * tile, tile), tile)
            o_ref[0, rows] = jnp.concatenate(outs, axis=0).T.astype(o_ref.dtype)
        return carry

    lax.fori_loop(0, npass, one_pass, 0)


def attn_d(qt, k, vt):
    B, S, _ = k.shape
    T = DENSE_TILE
    nq = nk = S // T
    qtiles = math.gcd(nq, DENSE_QTILES)
    npass = math.gcd(nq // qtiles, D_PASSES)
    ns = 2 * qtiles
    kern = functools.partial(_attn_d_kernel, tile=T, nk=nk, qtiles=qtiles, npass=npass)
    return pl.pallas_call(
        kern,
        grid=(B, D_HEADS // 2, nq // (qtiles * npass)),
        in_specs=[pl.BlockSpec((1, 2, qtiles * npass, LANE, T), lambda b, h, i: (b, h, i, 0, 0)),
                  pl.BlockSpec((1, S, 2 * LANE), lambda b, h, i: (b, 0, h)),
                  pl.BlockSpec((1, 2, nk, V_ROWS_D, T), lambda b, h, i: (b, h, 0, 0, 0))],
        out_specs=pl.BlockSpec((1, qtiles * npass * T, LANE), lambda b, h, i: (b, i, h)),
        out_shape=jax.ShapeDtypeStruct((B, S, D_HEADS * D_V), BF16),
        scratch_shapes=[pltpu.VMEM((ns, 1, T), F32), pltpu.VMEM((ns, V_ROWS_D, T), F32),
                        pltpu.VMEM((ns, T, T), F32), pltpu.VMEM((ns, T, T), F32),
                        pltpu.VMEM((ns, 1, T), F32), pltpu.VMEM((ns, 1, T), F32)],
        compiler_params=_params("parallel", "parallel", "arbitrary"),
        name="attn_d",
    )(qt, k, vt)


def _attn_b_kernel(q_ref, k_ref, v_ref, bias_ref, sink_ref, o_ref, s_a, s_b, s_c, cm_a, cm_b, cm_c, *, tile, seq_len):
    half_window = B_HALF_WINDOW
    span = QCOLS + 2 * half_window
    nchunk = span // LANE
    step = pl.program_id(1)

    def window(c):
        q0 = step * tile + c * QCOLS
        start = pl.multiple_of(jnp.clip(q0 - half_window, 0, seq_len - span), LANE)
        variant = jnp.where(q0 == 0, 0, jnp.where(q0 + QCOLS == seq_len, 2, 1))
        return start, variant

    def scores(task):
        c, j, g = task
        start, variant = window(c)
        kwin = k_ref[0, pl.ds(start, span), :]
        s = jnp.dot(kwin, q_ref[0, j, g, 0, :, c * QCOLS:(c + 1) * QCOLS], preferred_element_type=F32)
        return s + bias_ref[variant, j + (B_HEADS // 2) * g]

    def finish(task, s_ref, cm_ref):
        c, j, g = task
        head = j + (B_HEADS // 2) * g
        start, _ = window(c)
        chunk0 = start // LANE
        vwin = jnp.concatenate([v_ref[0, g, chunk0 + n] for n in range(nchunk)], axis=1)
        sk = sink_ref[:, head:head + 1] * LOG2E
        m = jnp.maximum(cm_ref[...], sk)
        e = jnp.exp2((s_ref[...] - m).astype(BF16))
        ov = jnp.dot(vwin, e, preferred_element_type=F32)
        denom = ov[HEAD_DIM:HEAD_DIM + 1] + jnp.exp2(sk - m)
        return ov[:HEAD_DIM] / denom

    tasks = [(c, j, g) for c in range(tile // QCOLS) for j in range(B_HEADS // 2) for g in range(B_KV_HEADS)]
    bufs = ((s_a, cm_a), (s_b, cm_b), (s_c, cm_c))

    def produce(task, buf):
        s = scores(task)
        buf[0][...] = s
        buf[1][...] = jnp.max(s, axis=0, keepdims=True)

    produce(tasks[0], bufs[0])
    produce(tasks[1], bufs[1])
    held = None
    for n, task in enumerate(tasks):
        if n + 2 < len(tasks):
            produce(tasks[n + 2], bufs[(n + 2) % 3])
        o = finish(task, *bufs[n % 3])
        c, j, g = task
        if g == 0:
            held = o
            continue
        pair = jnp.concatenate([held, o], axis=0).T
        o_ref[0, c * QCOLS:(c + 1) * QCOLS, j * LANE:(j + 1) * LANE] = pair.astype(o_ref.dtype)


def attn_b(qt, karr, k_block, vt, bias, sink):
    B, S, _ = karr.shape
    T = DENSE_TILE
    span = QCOLS + 2 * B_HALF_WINDOW
    return pl.pallas_call(
        functools.partial(_attn_b_kernel, tile=T, seq_len=S),
        grid=(B, S // T),
        in_specs=[pl.BlockSpec((1, B_HEADS // 2, 2, 1, LANE, T), lambda b, i: (b, 0, 0, i, 0, 0)),
                  pl.BlockSpec((1, S, LANE), lambda b, i: (b, 0, k_block)),
                  pl.BlockSpec((1, B_KV_HEADS, S // LANE, V_ROWS_D, LANE), lambda b, i: (b, 0, 0, 0, 0)),
                  pl.BlockSpec(bias.shape, lambda b, i: (0, 0, 0, 0)),
                  pl.BlockSpec((1, B_HEADS), lambda b, i: (0, 0))],
        out_specs=pl.BlockSpec((1, T, B_HEADS * HEAD_DIM), lambda b, i: (b, i, 0)),
        out_shape=jax.ShapeDtypeStruct((B, S, B_HEADS * HEAD_DIM), BF16),
        scratch_shapes=[pltpu.VMEM((span, QCOLS), F32)] * 3 + [pltpu.VMEM((1, QCOLS), F32)] * 3,
        compiler_params=_params("parallel", "arbitrary"),
        name="attn_b",
    )(qt, karr, vt, bias, sink.reshape(1, -1))


def _attn_c_kernel(q_ref, k_ref, v_ref, bias_ref, o_ref, lse_ref, *, tq, group, span, half_window, seq_len):
    nq = seq_len // tq
    nblk = q_ref.shape[-1] // LANE
    lane = lax.broadcasted_iota(jnp.int32, (tq, LANE), 1)
    low = lane < HEAD_DIM

    def window(g):
        t = pl.program_id(2) * group + g
        start = pl.multiple_of(jnp.clip(t * tq - half_window, 0, seq_len - span), half_window)
        variant = jnp.where(t == 0, 0, jnp.where(t == nq - 1, 2, 1))
        return start, variant

    def scores(task):
        g, c, half = task
        start, variant = window(g)
        q2 = q_ref[0, g * tq:(g + 1) * tq, c * LANE:(c + 1) * LANE]
        kb = k_ref[0, pl.ds(start, span), c * LANE:(c + 1) * LANE]
        qm = jnp.where(low if half == 0 else jnp.logical_not(low), q2, jnp.zeros_like(q2))
        s = lax.dot_general(qm, kb, (((1,), (1,)), ((), ())), preferred_element_type=F32)
        return s + bias_ref[variant, 2 * c + half]

    def finish(task, s):
        g, c, half = task
        start, _ = window(g)
        vb = v_ref[0, pl.ds(start, span), c * LANE:(c + 1) * LANE]
        m = jnp.max(s, axis=-1, keepdims=True)
        e = jnp.exp2(s - m)
        denom = jnp.sum(e, axis=-1, keepdims=True)
        o = jnp.dot(e.astype(BF16), vb, preferred_element_type=F32) / denom
        return o, LN2 * m + jnp.log(denom)

    tasks = [(g, c, half) for g in range(group) for c in range(nblk) for half in range(2)]
    s_next = scores(tasks[0])
    held = None
    for n, task in enumerate(tasks):
        s = s_next
        if n + 1 < len(tasks):
            s_next = scores(tasks[n + 1])
        o, lse = finish(task, s)
        g, c, half = task
        if half == 0:
            held = (o, lse)
            continue
        rows, cols = slice(g * tq, (g + 1) * tq), slice(c * LANE, (c + 1) * LANE)
        o_ref[0, rows, cols] = jnp.where(low, held[0], o)
        lse_ref[0, rows, cols] = jnp.where(low, held[1], lse)


def attn_c(view, bias, *, dil, tq, half_window, name):
    B, L, _ = view.shape
    width = C_HEADS_PER_GROUP * HEAD_DIM
    span = tq + 2 * half_window
    nq = L // tq
    assert L % tq == 0 and L >= span and nq >= 2
    group = math.gcd(nq, BAND_GROUP)
    kern = functools.partial(_attn_c_kernel, tq=tq, group=group, span=span, half_window=half_window, seq_len=L)
    out_spec = pl.BlockSpec((1, group * tq, width), lambda b, r, t: (b, t, r))
    out_shape = jax.ShapeDtypeStruct((B, L, dil * width), F32)
    return pl.pallas_call(
        kern,
        grid=(B, dil, nq // group),
        in_specs=[pl.BlockSpec((1, group * tq, width), lambda b, r, t: (b, t, 3 * r)),
                  pl.BlockSpec((1, L, width), lambda b, r, t: (b, 0, 3 * r + 1)),
                  pl.BlockSpec((1, L, width), lambda b, r, t: (b, 0, 3 * r + 2)),
                  pl.BlockSpec(bias.shape, lambda b, r, t: (0, 0, 0, 0))],
        out_specs=[out_spec, out_spec],
        out_shape=[out_shape, out_shape],
        compiler_params=_params("parallel", "parallel", "arbitrary"),
        name=name,
    )(view, view, view, bias)


def _out_proj_cd_kernel(h_ref, o0, o1, o2, s0, s1, s2, d_ref, wc_ref, wd_ref, o_ref, *scratch):
    tm = h_ref.shape[0]
    width = C_HEADS_PER_GROUP * HEAD_DIM
    spare = iter(scratch)

    def in_position_order(ref, dil):
        if dil == 1:
            return ref[0]
        t_s = next(spare)
        for r in range(dil):
            for j in range(width // LANE):
                t_s[j, pl.ds(r, tm // dil, stride=dil), :] = ref[0, :, r * width + j * LANE:r * width + (j + 1) * LANE]
        return jnp.concatenate([t_s[j] for j in range(width // LANE)], axis=1)

    dils = [d for _, d in C_PATTERNS]
    outs = [in_position_order(r, d) for r, d in zip((o0, o1, o2), dils)]
    lses = [in_position_order(r, d) for r, d in zip((s0, s1, s2), dils)]
    mx = jnp.maximum(jnp.maximum(lses[0], lses[1]), lses[2])
    es = [jnp.exp(l - mx) for l in lses]
    oc = (es[0] * outs[0] + es[1] * outs[1] + es[2] * outs[2]) / (es[0] + es[1] + es[2])
    mix = jnp.dot(oc.astype(BF16), wc_ref[...], preferred_element_type=F32)
    mix = mix + jnp.dot(d_ref[...], wd_ref[...], preferred_element_type=F32)
    o_ref[...] = h_ref[...] + mix


def out_proj_cd(h, oc, lses, od, wc, wd, batch, seq_len):
    M = h.shape[0]
    tm = ROW_TILE
    nrow = seq_len // tm
    width = C_HEADS_PER_GROUP * HEAD_DIM
    row = lambda a: pl.BlockSpec((tm, a.shape[1]), lambda i: (i, 0))
    full = lambda a: pl.BlockSpec(a.shape, lambda i: (0, 0))
    views = [pl.BlockSpec((1, tm // d, d * width), lambda i: (i // nrow, i % nrow, 0)) for _, d in C_PATTERNS]
    n_spare = 2 * sum(d > 1 for _, d in C_PATTERNS)
    return pl.pallas_call(
        _out_proj_cd_kernel,
        grid=(M // tm,),
        in_specs=[row(h), *views, *views, row(od), full(wc), full(wd)],
        out_specs=pl.BlockSpec((tm, D_MODEL), lambda i: (i, 0)),
        out_shape=jax.ShapeDtypeStruct((M, D_MODEL), F32),
        scratch_shapes=[pltpu.VMEM((width // LANE, tm, LANE), F32)] * n_spare,
        compiler_params=_params("parallel"),
        name="out_proj1",
    )(h, *oc, *lses, od, wc, wd)


def _proj1_kernel(x_ref, g_ref, w_ref, cs_ref, c0_ref, c1_ref, c2_ref, y_s):
    xn = _rms(x_ref[...], g_ref[...]).astype(BF16)
    y = jnp.dot(xn, w_ref[...], preferred_element_type=F32) * cs_ref[...]
    tm = y.shape[0]
    width = y.shape[1] // len(C_PATTERNS)
    c0_ref[0] = y[:, :width].astype(BF16)
    nblk = width // LANE
    for j in range(y_s.shape[0]):
        y_s[j] = y[:, width + j * LANE:width + (j + 1) * LANE]
    for g, ref in ((1, c1_ref), (2, c2_ref)):
        dil = C_PATTERNS[g][1]
        for r in range(dil):
            for j in range(nblk):
                rows = y_s[(g - 1) * nblk + j, pl.ds(r, tm // dil, stride=dil), :]
                ref[0, :, r * width + j * LANE:r * width + (j + 1) * LANE] = rows.astype(BF16)


def proj1_call(x, g, w, colscale, batch, seq_len):
    M, K = x.shape
    N = w.shape[1]
    tm = ROW_TILE
    nrow = seq_len // tm
    width = N // len(C_PATTERNS)
    assert [d for _, d in C_PATTERNS][0] == 1
    out_specs = [pl.BlockSpec((1, tm // d, d * width), lambda i: (i // nrow, i % nrow, 0)) for _, d in C_PATTERNS]
    out_shape = [jax.ShapeDtypeStruct((batch, seq_len // d, d * width), BF16) for _, d in C_PATTERNS]
    return pl.pallas_call(
        _proj1_kernel,
        grid=(M // tm,),
        in_specs=[pl.BlockSpec((tm, K), lambda i: (i, 0)),
                  pl.BlockSpec((1, K), lambda i: (0, 0)),
                  pl.BlockSpec((K, N), lambda i: (0, 0)),
                  pl.BlockSpec((1, N), lambda i: (0, 0))],
        out_specs=out_specs,
        out_shape=out_shape,
        scratch_shapes=[pltpu.VMEM(((N - width) // LANE, tm, LANE), F32)],
        compiler_params=_params("parallel"),
        name="proj1",
    )(x, g.reshape(1, K), w, colscale.reshape(1, N))


def _ffn_kernel(*refs, n_mix, final_norm):
    x_ref, mix_refs = refs[0], refs[1:1 + 2 * n_mix]
    g_ref, wg_ref, wu_ref, wd_ref, fg_ref, o_ref, xn_s, acc_s, h_s = refs[1 + 2 * n_mix:]
    f = pl.program_id(1)

    @pl.when(f == 0)
    def _():
        h = x_ref[...]
        for a_ref, w_ref in zip(mix_refs[::2], mix_refs[1::2]):
            h = h + jnp.dot(a_ref[...], w_ref[...], preferred_element_type=F32)
        h_s[...] = h
        xn_s[...] = _rms(h, g_ref[...]).astype(BF16)
        acc_s[...] = jnp.zeros(acc_s.shape, F32)

    xn = xn_s[...]
    gate = jnp.dot(xn, wg_ref[0].astype(BF16), preferred_element_type=F32)
    up = jnp.dot(xn, wu_ref[0].astype(BF16), preferred_element_type=F32)
    mid = (gate / (1.0 + jnp.exp(-gate)) * up).astype(BF16)
    acc_s[...] += jnp.dot(mid, wd_ref[0].astype(BF16), preferred_element_type=F32)

    @pl.when(f == pl.num_programs(1) - 1)
    def _():
        y = h_s[...] + acc_s[...]
        if final_norm:
            y = _rms(y, fg_ref[...])
        o_ref[...] = y


def ffn(x, mix, g, wg, wu, wd, layer, fg, final_norm, name):
    M, K = x.shape
    tm, tf = FFN_ROW_TILE, FFN_COL_TILE
    kern = functools.partial(_ffn_kernel, n_mix=len(mix), final_norm=final_norm)
    mix_specs, mix_args = [], []
    for a, w in mix:
        mix_specs += [pl.BlockSpec((tm, a.shape[1]), lambda i, f: (i, 0)), pl.BlockSpec(w.shape, lambda i, f: (0, 0))]
        mix_args += [a, w]
    return pl.pallas_call(
        kern,
        grid=(M // tm, D_FF // tf),
        in_specs=[pl.BlockSpec((tm, K), lambda i, f: (i, 0)), *mix_specs,
                  pl.BlockSpec((1, K), lambda i, f: (0, 0)),
                  pl.BlockSpec((1, K, tf), lambda i, f: (layer, 0, f)),
                  pl.BlockSpec((1, K, tf), lambda i, f: (layer, 0, f)),
                  pl.BlockSpec((1, tf, K), lambda i, f: (layer, f, 0)),
                  pl.BlockSpec((1, K), lambda i, f: (0, 0))],
        out_specs=pl.BlockSpec((tm, K), lambda i, f: (i, 0)),
        out_shape=jax.ShapeDtypeStruct((M, K), F32),
        scratch_shapes=[pltpu.VMEM((tm, K), BF16), pltpu.VMEM((tm, K), F32), pltpu.VMEM((tm, K), F32)],
        compiler_params=_params("parallel", "arbitrary"),
        name=name,
    )(x, *mix_args, g.reshape(1, K), wg, wu, wd, fg.reshape(1, K))


def _trig_kernel(ang_ref, cos_ref, sin_ref):
    a = ang_ref[...]
    cos_ref[...] = jnp.cos(a)
    sin_ref[...] = jnp.sin(a)


def rope_tables(seq_len):
    half = D_ROPE // 2
    inv = ROPE_THETA ** (-jnp.arange(half, dtype=F32) / half)
    ang = jnp.arange(seq_len).astype(F32)[:, None] * inv[None, :]
    dense = ang.reshape(seq_len * half // LANE, LANE)
    spec = pl.BlockSpec(dense.shape, lambda: (0, 0))
    cos, sin = pl.pallas_call(
        _trig_kernel,
        in_specs=[spec],
        out_specs=[spec, spec],
        out_shape=[jax.ShapeDtypeStruct(dense.shape, F32)] * 2,
        name="rope_trig",
    )(dense)
    cos, sin = cos.reshape(seq_len, half), sin.reshape(seq_len, half)
    pad = LANE - D_NOPE - D_ROPE
    cos_l = jnp.concatenate([jnp.ones((seq_len, D_NOPE), F32), cos, cos, jnp.ones((seq_len, pad), F32)], axis=1)
    sin_l = jnp.concatenate([jnp.zeros((seq_len, D_NOPE), F32), sin, sin, jnp.zeros((seq_len, pad), F32)], axis=1)
    return cos_l, sin_l


def _prep_d_kernel(x_ref, g_ref, wa_ref, qn_ref, kvn_ref, wq_ref, wkv_ref, cos_ref, sin_ref,
                   q_ref, k_ref, v_ref, *, qscale):
    xn = _rms(x_ref[...], g_ref[...]).astype(BF16)
    lat = jnp.dot(xn, wa_ref[...], preferred_element_type=F32)
    cq = _rms(lat[:, :D_Q_LORA], qn_ref[...]).astype(BF16)
    ckv = _rms(lat[:, D_Q_LORA:D_Q_LORA + D_KV_LORA], kvn_ref[...]).astype(BF16)
    o_pe = D_Q_LORA + D_KV_LORA
    cos, sin = cos_ref[...], sin_ref[...]
    kpe = lat[:, o_pe:o_pe + LANE] * cos + lat[:, o_pe + LANE:o_pe + 2 * LANE] * sin
    qq = jnp.dot(cq, wq_ref[...], preferred_element_type=F32)
    kv = jnp.dot(ckv, wkv_ref[...], preferred_element_type=F32)
    kw = D_HEADS * LANE
    ones_blk = _ones_row_block(V_ROWS_D - D_V, x_ref.shape[0])
    for h in range(D_HEADS):
        qh = qq[:, h * LANE:(h + 1) * LANE] * cos + qq[:, kw + h * LANE:kw + (h + 1) * LANE] * sin
        q_ref[0, h, 0] = (qh * qscale).T.astype(BF16)
        k_ref[:, h * LANE:(h + 1) * LANE] = (kv[:, h * LANE:(h + 1) * LANE] + kpe).astype(BF16)
    for j in range(D_HEADS // 2):
        vt = kv[:, kw + j * LANE:kw + (j + 1) * LANE].T.astype(BF16)
        for half in range(2):
            v_ref[0, 2 * j + half, 0, :D_V] = vt[half * D_V:(half + 1) * D_V]
            v_ref[0, 2 * j + half, 0, D_V:] = ones_blk


def prep_d(x, g, wa, qn, kvn, wq, wkv, cos_l, sin_l, batch, seq_len):
    M, K = x.shape
    T = DENSE_TILE
    nrow = seq_len // T
    kw = D_HEADS * LANE
    kern = functools.partial(_prep_d_kernel, qscale=(D_NOPE + D_ROPE) ** -0.5 * LOG2E)
    full = lambda a: pl.BlockSpec(a.shape, lambda i: (0, 0))
    qn2, kvn2, g2 = qn.reshape(1, -1), kvn.reshape(1, -1), g.reshape(1, K)
    return pl.pallas_call(
        kern,
        grid=(M // T,),
        in_specs=[pl.BlockSpec((T, K), lambda i: (i, 0)), full(g2), full(wa), full(qn2), full(kvn2),
                  full(wq), full(wkv),
                  pl.BlockSpec((T, LANE), lambda i: (i % nrow, 0)),
                  pl.BlockSpec((T, LANE), lambda i: (i % nrow, 0))],
        out_specs=[pl.BlockSpec((1, D_HEADS, 1, LANE, T), lambda i: (i // nrow, 0, i % nrow, 0, 0)),
                   pl.BlockSpec((T, kw), lambda i: (i, 0)),
                   pl.BlockSpec((1, D_HEADS, 1, V_ROWS_D, T), lambda i: (i // nrow, 0, i % nrow, 0, 0))],
        out_shape=[jax.ShapeDtypeStruct((batch, D_HEADS, nrow, LANE, T), BF16),
                   jax.ShapeDtypeStruct((M, kw), BF16),
                   jax.ShapeDtypeStruct((batch, D_HEADS, nrow, V_ROWS_D, T), BF16)],
        compiler_params=_params("parallel"),
        name="prep_d",
    )(x, g2, wa, qn2, kvn2, wq, wkv, cos_l, sin_l)


B_HEAD_ORDER = (0, 4, 1, 5, 2, 6, 3, 7)


def _blocks(w, starts, width, axis):
    return jnp.concatenate([lax.slice_in_dim(w, s, s + width, axis=axis) for s in starts], axis=axis)


def _rot_partner_cols(w):
    half = D_ROPE // 2
    return jnp.concatenate([-w[..., half:], w[..., :half]], axis=-1)


def kernel(x, bias_table, attn_norm, ffn_norm, final_norm, ab_w_in, ab_lambda_q1, ab_lambda_k1,
           ab_lambda_q2, ab_lambda_k2, ab_subln, ab_sink, ab_w_o, cd_w_in, cd_q_norm, cd_w_q_b,
           cd_kv_norm, cd_w_kv_b, cd_w_o, ffn_w_gate, ffn_w_up, ffn_w_down):
    B, S, _ = x.shape
    M = B * S
    T = DENSE_TILE
    h = x.reshape(M, D_MODEL)
    qk_scale = HEAD_DIM ** -0.5 * LOG2E

    o3 = A_HEADS * (2 * A_QK_DIM + A_V_DIM)
    a0 = ab_w_in[0].astype(BF16)
    w0 = jnp.concatenate([a0[:, :o3], _blocks(a0, [o3 + hd * HEAD_DIM for hd in B_HEAD_ORDER], HEAD_DIM, 1),
                          a0[:, o3 + B_HEADS * HEAD_DIM:]], axis=1)
    cs0 = np.ones((AB_IN,), np.float32)
    cs0[:A_HEADS * A_QK_DIM] = qk_scale
    cs0[o3:o3 + B_HEADS * HEAD_DIM] = qk_scale
    qat, vat, qbt, vbt, keys0 = proj0_call(h, attn_norm[0], w0, jnp.asarray(cs0), B, S)
    keys0 = keys0.reshape(B, S, -1)
    bias_a = bias_tiles(bias_table, nvar=2 * BIAS_REACH + 1, nheads=A_HEADS, head0=0, rows=T, cols=T,
                        off0=-BIAS_REACH * T, off_step=T, row_coef=1, col_coef=-1, dil=1,
                        half_window=None, name="bias_a")
    oa = attn_a(qat, keys0, vat, bias_a, ab_lambda_q1[0], ab_lambda_k1[0], ab_lambda_q2[0],
                ab_lambda_k2[0], ab_subln[0], 0.8 - 0.6 * math.exp(-0.3 * 0))

    bias_b = bias_tiles(bias_table, nvar=3, nheads=B_HEADS, head0=A_HEADS, rows=QCOLS + 2 * B_HALF_WINDOW,
                        cols=QCOLS, off0=0, off_step=-B_HALF_WINDOW, row_coef=1, col_coef=-1, dil=1,
                        half_window=B_HALF_WINDOW, name="bias_b")
    ob = attn_b(qbt, keys0, (A_HEADS * A_QK_DIM) // LANE, vbt, bias_b, ab_sink[0])

    wo = ab_w_o[0].astype(BF16)
    wo_a = wo[:A_HEADS * A_V_DIM]
    wo_b = _blocks(wo, [A_HEADS * A_V_DIM + hd * HEAD_DIM for hd in B_HEAD_ORDER], HEAD_DIM, 0)
    h = ffn(h, [(oa.reshape(M, -1), wo_a), (ob.reshape(M, -1), wo_b)], ffn_norm[0], ffn_w_gate, ffn_w_up,
            ffn_w_down, 0, final_norm, False, "ffn0")

    w1 = cd_w_in[0]
    gw = C_HEADS_PER_GROUP * HEAD_DIM
    cw = C_HEADS * HEAD_DIM
    starts1 = [role * cw + g * gw for g in range(len(C_PATTERNS)) for role in range(3)]
    cs1 = np.ones((CD_C_IN,), np.float32)
    for g in range(len(C_PATTERNS)):
        cs1[3 * g * gw:(3 * g + 1) * gw] = qk_scale
    c_views = proj1_call(h, attn_norm[1], _blocks(w1.astype(BF16), starts1, gw, 1), jnp.asarray(cs1), B, S)

    oc, lses = [], []
    for g, (window, dil) in enumerate(C_PATTERNS):
        hw = window // (2 * dil)
        tq_c = 128
        bias_c = bias_tiles(bias_table, nvar=3, nheads=C_HEADS_PER_GROUP, head0=g * C_HEADS_PER_GROUP,
                            rows=tq_c, cols=tq_c + 2 * hw, off0=0, off_step=-hw, row_coef=-1, col_coef=1,
                            dil=dil, half_window=hw, name=f"bias_c{g}")
        o_g, lse_g = attn_c(c_views[g], bias_c, dil=dil, tq=tq_c, half_window=hw, name=f"attn_c{g}")
        oc.append(o_g)
        lses.append(lse_g)

    o_kv = CD_C_IN + D_Q_LORA + D_KV_LORA
    w_pe = w1[:, o_kv:]
    lane_pad = lambda w: jnp.pad(w, ((0, 0), (D_NOPE, LANE - D_NOPE - D_ROPE)))
    wa = jnp.concatenate([w1[:, CD_C_IN:o_kv], lane_pad(w_pe), lane_pad(_rot_partner_cols(w_pe))],
                         axis=1).astype(BF16)
    wq3 = cd_w_q_b[0].reshape(D_Q_LORA, D_HEADS, D_NOPE + D_ROPE)
    zpad = jnp.zeros((D_Q_LORA, D_HEADS, LANE - D_NOPE - D_ROPE), F32)
    wq_main = jnp.concatenate([wq3, zpad], axis=-1)
    wq_rot = jnp.concatenate([jnp.zeros_like(wq3[..., :D_NOPE]), _rot_partner_cols(wq3[..., D_NOPE:]), zpad],
                             axis=-1)
    wq = jnp.concatenate([wq_main.reshape(D_Q_LORA, -1), wq_rot.reshape(D_Q_LORA, -1)], axis=1).astype(BF16)
    wkv3 = cd_w_kv_b[0].reshape(D_KV_LORA, D_HEADS, D_NOPE + D_V)
    wk = jnp.pad(wkv3[..., :D_NOPE], ((0, 0), (0, 0), (0, LANE - D_NOPE))).reshape(D_KV_LORA, -1)
    wv = wkv3[..., D_NOPE:].reshape(D_KV_LORA, -1)
    wkv = jnp.concatenate([wk, wv], axis=1).astype(BF16)
    cos_l, sin_l = rope_tables(S)
    qdt, kd, vdt = prep_d(h, attn_norm[1], wa, cd_q_norm[0], cd_kv_norm[0], wq, wkv, cos_l, sin_l, B, S)
    od = attn_d(qdt, kd.reshape(B, S, D_HEADS * LANE), vdt)

    wo1 = cd_w_o[0]
    wo_c = wo1[:C_HEADS_PER_GROUP * HEAD_DIM].astype(BF16)
    wo_d = wo1[C_HEADS_PER_GROUP * HEAD_DIM:].astype(BF16)
    h = out_proj_cd(h, oc, lses, od.reshape(M, -1), wo_c, wo_d, B, S)
    h = ffn(h, [], ffn_norm[1], ffn_w_gate, ffn_w_up, ffn_w_down, 1, final_norm, True, "ffn1")
    return h.reshape(B, S, D_MODEL)
```

```python
import functools
import math

import numpy as np
import jax
import jax.numpy as jnp
from jax import lax
from jax.experimental import pallas as pl
from jax.experimental.pallas import tpu as pltpu

F32 = jnp.float32
BF16 = jnp.bfloat16

D_MODEL = 1024
HEAD_DIM = 64
EPS = 1e-6
NEG = -1e30
LOG2E = math.log2(math.e)
LN2 = math.log(2.0)

A_HEADS = 4
A_QK_DIM = 2 * HEAD_DIM
A_V_DIM = 2 * HEAD_DIM
B_HEADS = 8
B_KV_HEADS = 2
B_HALF_WINDOW = 128
C_PATTERNS = ((128, 1), (512, 4), (2048, 16))
C_HEADS_PER_GROUP = 4
C_HEADS = C_HEADS_PER_GROUP * len(C_PATTERNS)
D_HEADS = 12
D_Q_LORA = 384
D_KV_LORA = 256
D_NOPE = 64
D_ROPE = 32
D_V = 64
ROPE_THETA = 10000.0
NUM_BUCKETS = 32
MAX_DISTANCE = 1024
D_FF = 2816
AB_IN = 2304
CD_C_IN = 3 * C_HEADS * HEAD_DIM

LANE = 128
VMEM_LIMIT = 48 * 1024 * 1024

ROW_TILE = 512
FFN_ROW_TILE = 1024
FFN_COL_TILE = 256
DENSE_TILE = 512
QCOLS = 256
UNROLL = 14
DENSE_QTILES = 4
A_PASSES = 2
D_PASSES = 4
BAND_GROUP = 8
BIAS_REACH = 3
V_ROWS_D = 80
V_ROWS_A = 144


def _bucket_thresholds():
    nb = NUM_BUCKETS // 2
    max_exact = nb // 2
    n = np.arange(1, 4 * MAX_DISTANCE)
    large = max_exact + (np.log(n.astype(np.float32) / np.float32(max_exact))
                         / np.float32(math.log(MAX_DISTANCE / max_exact))
                         * np.float32(nb - max_exact)).astype(np.int32)
    mag = np.where(n < max_exact, n, np.minimum(large, nb - 1))
    return tuple(int(n[np.argmax(mag >= k)]) for k in range(1, nb))


BUCKET_THRESHOLDS = _bucket_thresholds()
assert BUCKET_THRESHOLDS[-1] <= (BIAS_REACH - 1) * DENSE_TILE + 1


def _params(*sem):
    return pltpu.CompilerParams(dimension_semantics=sem, vmem_limit_bytes=VMEM_LIMIT)


def _rms(x, g):
    return x * lax.rsqrt(jnp.mean(x * x, axis=-1, keepdims=True) + EPS) * g


def _ones_row_block(rows, cols):
    r = lax.broadcasted_iota(jnp.int32, (rows, cols), 0)
    return jnp.where(r == 0, 1.0, 0.0).astype(BF16)


def _store_masked_halves(q_ref, idx, qt):
    zeros = jnp.zeros((HEAD_DIM, qt.shape[1]), BF16)
    q_ref[idx + (0, 0, slice(None, HEAD_DIM))] = qt[:HEAD_DIM]
    q_ref[idx + (0, 0, slice(HEAD_DIM, None))] = zeros
    q_ref[idx + (1, 0, slice(None, HEAD_DIM))] = zeros
    q_ref[idx + (1, 0, slice(HEAD_DIM, None))] = qt[HEAD_DIM:]


def _proj0_kernel(x_ref, g_ref, w_ref, cs_ref, qa_ref, va_ref, qb_ref, vb_ref, k_ref):
    xn = _rms(x_ref[...], g_ref[...]).astype(BF16)
    y = jnp.dot(xn, w_ref[...], preferred_element_type=F32) * cs_ref[...]
    tm = y.shape[0]
    ka0 = A_HEADS * A_QK_DIM
    va0 = 2 * ka0
    qb0 = va0 + A_HEADS * A_V_DIM
    kb0 = qb0 + B_HEADS * HEAD_DIM
    vb0 = kb0 + B_KV_HEADS * HEAD_DIM
    for h in range(A_HEADS):
        _store_masked_halves(qa_ref, (0, h), y[:, h * A_QK_DIM:(h + 1) * A_QK_DIM].T.astype(BF16))
        va_ref[0, h, 0, :A_V_DIM] = y[:, va0 + h * A_V_DIM:va0 + (h + 1) * A_V_DIM].T.astype(BF16)
        va_ref[0, h, 0, A_V_DIM:] = _ones_row_block(V_ROWS_A - A_V_DIM, tm)
    for j in range(B_HEADS // 2):
        _store_masked_halves(qb_ref, (0, j), y[:, qb0 + j * LANE:qb0 + (j + 1) * LANE].T.astype(BF16))
    vbt = y[:, vb0:vb0 + LANE].T.astype(BF16)
    ones_blk = _ones_row_block(V_ROWS_D - HEAD_DIM, LANE)
    for g in range(B_KV_HEADS):
        for c in range(tm // LANE):
            vb_ref[0, g, c, :HEAD_DIM] = vbt[g * HEAD_DIM:(g + 1) * HEAD_DIM, c * LANE:(c + 1) * LANE]
            vb_ref[0, g, c, HEAD_DIM:] = ones_blk
    k_ref[:, :ka0] = y[:, ka0:va0].astype(BF16)
    k_ref[:, ka0:] = y[:, kb0:vb0].astype(BF16)


def proj0_call(x, g, w, colscale, batch, seq_len):
    M, K = x.shape
    N = w.shape[1]
    T = DENSE_TILE
    nrow = seq_len // T
    nkb = A_HEADS * A_QK_DIM + B_KV_HEADS * HEAD_DIM
    qspec = pl.BlockSpec((1, A_HEADS, 2, 1, LANE, T), lambda i: (i // nrow, 0, 0, i % nrow, 0, 0))
    qshape = jax.ShapeDtypeStruct((batch, A_HEADS, 2, nrow, LANE, T), BF16)
    return pl.pallas_call(
        _proj0_kernel,
        grid=(M // T,),
        in_specs=[pl.BlockSpec((T, K), lambda i: (i, 0)),
                  pl.BlockSpec((1, K), lambda i: (0, 0)),
                  pl.BlockSpec((K, N), lambda i: (0, 0)),
                  pl.BlockSpec((1, N), lambda i: (0, 0))],
        out_specs=[qspec,
                   pl.BlockSpec((1, A_HEADS, 1, V_ROWS_A, T), lambda i: (i // nrow, 0, i % nrow, 0, 0)),
                   qspec,
                   pl.BlockSpec((1, B_KV_HEADS, T // LANE, V_ROWS_D, LANE),
                                lambda i: (i // nrow, 0, i % nrow, 0, 0)),
                   pl.BlockSpec((T, nkb), lambda i: (i, 0))],
        out_shape=[qshape,
                   jax.ShapeDtypeStruct((batch, A_HEADS, nrow, V_ROWS_A, T), BF16),
                   qshape,
                   jax.ShapeDtypeStruct((batch, B_KV_HEADS, seq_len // LANE, V_ROWS_D, LANE), BF16),
                   jax.ShapeDtypeStruct((M, nkb), BF16)],
        compiler_params=_params("parallel"),
        name="proj0",
    )(x, g.reshape(1, K), w, colscale.reshape(1, N))


def _bias_kernel(tab_ref, o_ref, *, nvar, off0, off_step, row_coef, col_coef, dil, half_window, head0):
    hcol = head0 + pl.program_id(1)
    R, C = o_ref.shape[-2:]
    row = lax.broadcasted_iota(jnp.int32, (R, C), 0)
    col = lax.broadcasted_iota(jnp.int32, (R, C), 1)
    base = row_coef * row + col_coef * col
    span_lo = min(row_coef * (R - 1), 0) + min(col_coef * (C - 1), 0)
    span_hi = max(row_coef * (R - 1), 0) + max(col_coef * (C - 1), 0)
    nb = NUM_BUCKETS // 2

    def side(n, n_lo, n_hi, row0):
        val = jnp.full((R, C), tab_ref[row0 + sum(t <= n_lo for t in BUCKET_THRESHOLDS), hcol], F32)
        for k, thr in enumerate(BUCKET_THRESHOLDS, start=1):
            if n_lo < thr <= n_hi:
                val = jnp.where(n >= thr, tab_ref[row0 + k, hcol], val)
        return val

    for v in range(nvar):
        @pl.when(pl.program_id(0) == v)
        def _(v=v):
            off = off0 + v * off_step
            rel = off + base
            lo, hi = (off + span_lo) * dil, (off + span_hi) * dil
            dist = rel * dil
            n = jnp.abs(dist)
            if hi <= 0:
                val = side(n, -hi, -lo, 0)
            elif lo > 0:
                val = side(n, lo, hi, nb)
            else:
                val = jnp.where(dist > 0, side(n, 1, hi, nb), side(n, 0, -lo, 0))
            val = val * LOG2E
            if half_window is not None:
                val = jnp.where(jnp.abs(rel) <= half_window, val, NEG)
            o_ref[0, 0] = val


def bias_tiles(table, *, nvar, nheads, head0, rows, cols, off0, off_step, row_coef, col_coef,
               dil, half_window, name):
    kern = functools.partial(_bias_kernel, nvar=nvar, off0=off0, off_step=off_step, row_coef=row_coef,
                             col_coef=col_coef, dil=dil, half_window=half_window, head0=head0)
    return pl.pallas_call(
        kern,
        grid=(nvar, nheads),
        in_specs=[pl.BlockSpec(memory_space=pltpu.SMEM)],
        out_specs=pl.BlockSpec((1, 1, rows, cols), lambda v, h: (v, h, 0, 0)),
        out_shape=jax.ShapeDtypeStruct((nvar, nheads, rows, cols), F32),
        compiler_params=_params("parallel", "parallel"),
        name=name,
    )(table)


def _dense_pipeline(nk, tile, n_streams, score_fn, value_fn, m_s, acc_s, bufs):
    units = [(i, c * QCOLS) for i in range(n_streams) for c in range(tile // QCOLS)]

    def produce(kc, unit, nxt):
        i, c0 = unit
        cols = slice(c0, c0 + QCOLS)
        s = score_fn(kc + 1, i, cols)
        nxt[0][i, :, cols] = s
        nxt[1][i, :, cols] = jnp.max(s, axis=0, keepdims=True)

    def consume(kc, unit, cur):
        i, c0 = unit
        cols = slice(c0, c0 + QCOLS)
        m_old = m_s[i, :, cols]
        m_new = jnp.maximum(m_old, cur[1][i, :, cols])
        alpha = jnp.exp2(m_old - m_new)
        p = jnp.exp2((cur[0][i, :, cols] - m_new).astype(BF16))
        acc_s[i, :, cols] = (alpha * acc_s[i, :, cols]
                             + jnp.dot(value_fn(kc, i), p, preferred_element_type=F32))
        m_s[i, :, cols] = m_new

    def stage(kc, cur, nxt):
        for unit in units:
            if nxt is not None:
                produce(kc, unit, nxt)
            if cur is not None:
                consume(kc, unit, cur)

    m_s[...] = jnp.full(m_s.shape, NEG, F32)
    acc_s[...] = jnp.zeros(acc_s.shape, F32)
    stage(-1, None, bufs[0])
    n_loop = (nk - 1) // UNROLL

    def body(j, carry):
        for u in range(UNROLL):
            stage(UNROLL * j + u, bufs[u % 2], bufs[(u + 1) % 2])
        return carry

    lax.fori_loop(0, n_loop, body, 0)
    for kc in range(n_loop * UNROLL, nk):
        stage(kc, bufs[kc % 2], bufs[(kc + 1) % 2] if kc < nk - 1 else None)


def _attn_a_kernel(q_ref, k_ref, v_ref, bias_ref, lq1_ref, lk1_ref, lq2_ref, lk2_ref, subln_ref, o_ref,
                   m_s, acc_s, s_a, s_b, cm_a, cm_b, *, tile, nk, qtiles, npass, lambda_init):
    lam = (jnp.exp(jnp.sum(lq1_ref[...] * lk1_ref[...], axis=-1, keepdims=True))
           - jnp.exp(jnp.sum(lq2_ref[...] * lk2_ref[...], axis=-1, keepdims=True)) + lambda_init)

    def one_pass(p, carry):
        q0 = p * qtiles
        qi0 = pl.program_id(2) * (qtiles * npass) + q0

        def score_fn(kc, i, cols):
            j, qt = divmod(i, qtiles)
            kblk = k_ref[0, pl.ds(pl.multiple_of(kc * tile, tile), tile), :]
            bt = bias_ref[jnp.clip(kc - (qi0 + qt), -BIAS_REACH, BIAS_REACH) + BIAS_REACH, 0, :, cols]
            return jnp.dot(kblk, q_ref[0, 0, j, q0 + qt, :, cols], preferred_element_type=F32) + bt

        def value_fn(kc, i):
            return v_ref[0, 0, kc]

        _dense_pipeline(nk, tile, 2 * qtiles, score_fn, value_fn, m_s, acc_s, ((s_a, cm_a), (s_b, cm_b)))
        for qt in range(qtiles):
            a1, a2 = acc_s[qt], acc_s[qtiles + qt]
            o = (a1[:A_V_DIM] / a1[A_V_DIM:A_V_DIM + 1]
                 - lam * (a2[:A_V_DIM] / a2[A_V_DIM:A_V_DIM + 1]))
            ms = jnp.mean(o * o, axis=0, keepdims=True)
            y = o * lax.rsqrt(ms + EPS) * subln_ref[...] * (1.0 - lambda_init)
            rows = pl.ds(pl.multiple_of((q0 + qt) * tile, tile), tile)
            o_ref[0, rows] = y.T.astype(o_ref.dtype)
        return carry

    lax.fori_loop(0, npass, one_pass, 0)


def attn_a(qt, karr, vt, bias, lq1, lk1, lq2, lk2, subln, lambda_init):
    B, S, _ = karr.shape
    T = DENSE_TILE
    nq = nk = S // T
    assert nk % 2 == 0 and nk >= 4
    qtiles = math.gcd(nq, DENSE_QTILES)
    npass = math.gcd(nq // qtiles, A_PASSES)
    ns = 2 * qtiles
    kern = functools.partial(_attn_a_kernel, tile=T, nk=nk, qtiles=qtiles, npass=npass, lambda_init=lambda_init)
    vec = lambda n: pl.BlockSpec((1, n), lambda h, b, i: (0, 0))
    return pl.pallas_call(
        kern,
        grid=(A_HEADS, B, nq // (qtiles * npass)),
        in_specs=[pl.BlockSpec((1, 1, 2, qtiles * npass, LANE, T), lambda h, b, i: (b, h, 0, i, 0, 0)),
                  pl.BlockSpec((1, S, LANE), lambda h, b, i: (b, 0, h)),
                  pl.BlockSpec((1, 1, nk, V_ROWS_A, T), lambda h, b, i: (b, h, 0, 0, 0)),
                  pl.BlockSpec((2 * BIAS_REACH + 1, 1, T, T), lambda h, b, i: (0, h, 0, 0),
                               pipeline_mode=pl.Buffered(1)),
                  vec(HEAD_DIM), vec(HEAD_DIM), vec(HEAD_DIM), vec(HEAD_DIM),
                  pl.BlockSpec((A_V_DIM, 1), lambda h, b, i: (0, 0))],
        out_specs=pl.BlockSpec((1, qtiles * npass * T, LANE), lambda h, b, i: (b, i, h)),
        out_shape=jax.ShapeDtypeStruct((B, S, A_HEADS * A_V_DIM), BF16),
        scratch_shapes=[pltpu.VMEM((ns, 1, T), F32), pltpu.VMEM((ns, V_ROWS_A, T), F32),
                        pltpu.VMEM((ns, T, T), F32), pltpu.VMEM((ns, T, T), F32),
                        pltpu.VMEM((ns, 1, T), F32), pltpu.VMEM((ns, 1, T), F32)],
        compiler_params=_params("parallel", "parallel", "arbitrary"),
        name="attn_a",
    )(qt, karr, vt, bias, lq1.reshape(1, -1), lk1.reshape(1, -1), lq2.reshape(1, -1),
      lk2.reshape(1, -1), subln.reshape(-1, 1))


def _attn_d_kernel(q_ref, k_ref, v_ref, o_ref, m_s, acc_s, s_a, s_b, cm_a, cm_b, *, tile, nk, qtiles, npass):
    def one_pass(p, carry):
        q0 = p * qtiles

        def score_fn(kc, i, cols):
            hh, qt = divmod(i, qtiles)
            kblk = k_ref[0, pl.ds(pl.multiple_of(kc * tile, tile), tile), hh * LANE:(hh + 1) * LANE]
            return jnp.dot(kblk, q_ref[0, hh, q0 + qt, :, cols], preferred_element_type=F32)

        def value_fn(kc, i):
            return v_ref[0, i // qtiles, kc]

        _dense_pipeline(nk, tile, 2 * qtiles, score_fn, value_fn, m_s, acc_s, ((s_a, cm_a), (s_b, cm_b)))
        for qt in range(qtiles):
            outs = []
            for hh in range(2):
                acc = acc_s[hh * qtiles + qt]
                outs.append(acc[:D_V] / acc[D_V:D_V + 1])
            rows = pl.ds(pl.multiple_of((q0 + qt) * tile, tile), tile)
            o_ref[0, rows] = jnp.concatenate(outs, axis=0).T.astype(o_ref.dtype)
        return carry

    lax.fori_loop(0, npass, one_pass, 0)


def attn_d(qt, k, vt):
    B, S, _ = k.shape
    T = DENSE_TILE
    nq = nk = S // T
    qtiles = math.gcd(nq, DENSE_QTILES)
    npass = math.gcd(nq // qtiles, D_PASSES)
    ns = 2 * qtiles
    kern = functools.partial(_attn_d_kernel, tile=T, nk=nk, qtiles=qtiles, npass=npass)
    return pl.pallas_call(
        kern,
        grid=(B, D_HEADS // 2, nq // (qtiles * npass)),
        in_specs=[pl.BlockSpec((1, 2, qtiles * npass, LANE, T), lambda b, h, i: (b, h, i, 0, 0)),
                  pl.BlockSpec((1, S, 2 * LANE), lambda b, h, i: (b, 0, h)),
                  pl.BlockSpec((1, 2, nk, V_ROWS_D, T), lambda b, h, i: (b, h, 0, 0, 0))],
        out_specs=pl.BlockSpec((1, qtiles * npass * T, LANE), lambda b, h, i: (b, i, h)),
        out_shape=jax.ShapeDtypeStruct((B, S, D_HEADS * D_V), BF16),
        scratch_shapes=[pltpu.VMEM((ns, 1, T), F32), pltpu.VMEM((ns, V_ROWS_D, T), F32),
                        pltpu.VMEM((ns, T, T), F32), pltpu.VMEM((ns, T, T), F32),
                        pltpu.VMEM((ns, 1, T), F32), pltpu.VMEM((ns, 1, T), F32)],
        compiler_params=_params("parallel", "parallel", "arbitrary"),
        name="attn_d",
    )(qt, k, vt)


def _attn_b_kernel(q_ref, k_ref, v_ref, bias_ref, sink_ref, o_ref, s_a, s_b, s_c, cm_a, cm_b, cm_c, *, tile, seq_len):
    half_window = B_HALF_WINDOW
    span = QCOLS + 2 * half_window
    nchunk = span // LANE
    step = pl.program_id(1)

    def window(c):
        q0 = step * tile + c * QCOLS
        start = pl.multiple_of(jnp.clip(q0 - half_window, 0, seq_len - span), LANE)
        variant = jnp.where(q0 == 0, 0, jnp.where(q0 + QCOLS == seq_len, 2, 1))
        return start, variant

    def scores(task):
        c, j, g = task
        start, variant = window(c)
        kwin = k_ref[0, pl.ds(start, span), :]
        s = jnp.dot(kwin, q_ref[0, j, g, 0, :, c * QCOLS:(c + 1) * QCOLS], preferred_element_type=F32)
        return s + bias_ref[variant, j + (B_HEADS // 2) * g]

    def finish(task, s_ref, cm_ref):
        c, j, g = task
        head = j + (B_HEADS // 2) * g
        start, _ = window(c)
        chunk0 = start // LANE
        vwin = jnp.concatenate([v_ref[0, g, chunk0 + n] for n in range(nchunk)], axis=1)
        sk = sink_ref[:, head:head + 1] * LOG2E
        m = jnp.maximum(cm_ref[...], sk)
        e = jnp.exp2((s_ref[...] - m).astype(BF16))
        ov = jnp.dot(vwin, e, preferred_element_type=F32)
        denom = ov[HEAD_DIM:HEAD_DIM + 1] + jnp.exp2(sk - m)
        return ov[:HEAD_DIM] / denom

    tasks = [(c, j, g) for c in range(tile // QCOLS) for j in range(B_HEADS // 2) for g in range(B_KV_HEADS)]
    bufs = ((s_a, cm_a), (s_b, cm_b), (s_c, cm_c))

    def produce(task, buf):
        s = scores(task)
        buf[0][...] = s
        buf[1][...] = jnp.max(s, axis=0, keepdims=True)

    produce(tasks[0], bufs[0])
    produce(tasks[1], bufs[1])
    held = None
    for n, task in enumerate(tasks):
        if n + 2 < len(tasks):
            produce(tasks[n + 2], bufs[(n + 2) % 3])
        o = finish(task, *bufs[n % 3])
        c, j, g = task
        if g == 0:
            held = o
            continue
        pair = jnp.concatenate([held, o], axis=0).T
        o_ref[0, c * QCOLS:(c + 1) * QCOLS, j * LANE:(j + 1) * LANE] = pair.astype(o_ref.dtype)


def attn_b(qt, karr, k_block, vt, bias, sink):
    B, S, _ = karr.shape
    T = DENSE_TILE
    span = QCOLS + 2 * B_HALF_WINDOW
    return pl.pallas_call(
        functools.partial(_attn_b_kernel, tile=T, seq_len=S),
        grid=(B, S // T),
        in_specs=[pl.BlockSpec((1, B_HEADS // 2, 2, 1, LANE, T), lambda b, i: (b, 0, 0, i, 0, 0)),
                  pl.BlockSpec((1, S, LANE), lambda b, i: (b, 0, k_block)),
                  pl.BlockSpec((1, B_KV_HEADS, S // LANE, V_ROWS_D, LANE), lambda b, i: (b, 0, 0, 0, 0)),
                  pl.BlockSpec(bias.shape, lambda b, i: (0, 0, 0, 0)),
                  pl.BlockSpec((1, B_HEADS), lambda b, i: (0, 0))],
        out_specs=pl.BlockSpec((1, T, B_HEADS * HEAD_DIM), lambda b, i: (b, i, 0)),
        out_shape=jax.ShapeDtypeStruct((B, S, B_HEADS * HEAD_DIM), BF16),
        scratch_shapes=[pltpu.VMEM((span, QCOLS), F32)] * 3 + [pltpu.VMEM((1, QCOLS), F32)] * 3,
        compiler_params=_params("parallel", "arbitrary"),
        name="attn_b",
    )(qt, karr, vt, bias, sink.reshape(1, -1))


def _attn_c_kernel(q_ref, k_ref, v_ref, bias_ref, o_ref, lse_ref, *, tq, group, span, half_window, seq_len):
    nq = seq_len // tq
    nblk = q_ref.shape[-1] // LANE
    lane = lax.broadcasted_iota(jnp.int32, (tq, LANE), 1)
    low = lane < HEAD_DIM

    def window(g):
        t = pl.program_id(2) * group + g
        start = pl.multiple_of(jnp.clip(t * tq - half_window, 0, seq_len - span), half_window)
        variant = jnp.where(t == 0, 0, jnp.where(t == nq - 1, 2, 1))
        return start, variant

    def scores(task):
        g, c, half = task
        start, variant = window(g)
        q2 = q_ref[0, g * tq:(g + 1) * tq, c * LANE:(c + 1) * LANE]
        kb = k_ref[0, pl.ds(start, span), c * LANE:(c + 1) * LANE]
        qm = jnp.where(low if half == 0 else jnp.logical_not(low), q2, jnp.zeros_like(q2))
        s = lax.dot_general(qm, kb, (((1,), (1,)), ((), ())), preferred_element_type=F32)
        return s + bias_ref[variant, 2 * c + half]

    def finish(task, s):
        g, c, half = task
        start, _ = window(g)
        vb = v_ref[0, pl.ds(start, span), c * LANE:(c + 1) * LANE]
        m = jnp.max(s, axis=-1, keepdims=True)
        e = jnp.exp2(s - m)
        denom = jnp.sum(e, axis=-1, keepdims=True)
        o = jnp.dot(e.astype(BF16), vb, preferred_element_type=F32) / denom
        return o, LN2 * m + jnp.log(denom)

    tasks = [(g, c, half) for g in range(group) for c in range(nblk) for half in range(2)]
    s_next = scores(tasks[0])
    held = None
    for n, task in enumerate(tasks):
        s = s_next
        if n + 1 < len(tasks):
            s_next = scores(tasks[n + 1])
        o, lse = finish(task, s)
        g, c, half = task
        if half == 0:
            held = (o, lse)
            continue
        rows, cols = slice(g * tq, (g + 1) * tq), slice(c * LANE, (c + 1) * LANE)
        o_ref[0, rows, cols] = jnp.where(low, held[0], o)
        lse_ref[0, rows, cols] = jnp.where(low, held[1], lse)


def attn_c(view, bias, *, dil, tq, half_window, name):
    B, L, _ = view.shape
    width = C_HEADS_PER_GROUP * HEAD_DIM
    span = tq + 2 * half_window
    nq = L // tq
    assert L % tq == 0 and L >= span and nq >= 2
    group = math.gcd(nq, BAND_GROUP)
    kern = functools.partial(_attn_c_kernel, tq=tq, group=group, span=span, half_window=half_window, seq_len=L)
    out_spec = pl.BlockSpec((1, group * tq, width), lambda b, r, t: (b, t, r))
    out_shape = jax.ShapeDtypeStruct((B, L, dil * width), F32)
    return pl.pallas_call(
        kern,
        grid=(B, dil, nq // group),
        in_specs=[pl.BlockSpec((1, group * tq, width), lambda b, r, t: (b, t, 3 * r)),
                  pl.BlockSpec((1, L, width), lambda b, r, t: (b, 0, 3 * r + 1)),
                  pl.BlockSpec((1, L, width), lambda b, r, t: (b, 0, 3 * r + 2)),
                  pl.BlockSpec(bias.shape, lambda b, r, t: (0, 0, 0, 0))],
        out_specs=[out_spec, out_spec],
        out_shape=[out_shape, out_shape],
        compiler_params=_params("parallel", "parallel", "arbitrary"),
        name=name,
    )(view, view, view, bias)


def _out_proj_cd_kernel(h_ref, o0, o1, o2, s0, s1, s2, d_ref, wc_ref, wd_ref, o_ref, *scratch):
    tm = h_ref.shape[0]
    width = C_HEADS_PER_GROUP * HEAD_DIM
    spare = iter(scratch)

    def in_position_order(ref, dil):
        if dil == 1:
            return ref[0]
        t_s = next(spare)
        for r in range(dil):
            for j in range(width // LANE):
                t_s[j, pl.ds(r, tm // dil, stride=dil), :] = ref[0, :, r * width + j * LANE:r * width + (j + 1) * LANE]
        return jnp.concatenate([t_s[j] for j in range(width // LANE)], axis=1)

    dils = [d for _, d in C_PATTERNS]
    outs = [in_position_order(r, d) for r, d in zip((o0, o1, o2), dils)]
    lses = [in_position_order(r, d) for r, d in zip((s0, s1, s2), dils)]
    mx = jnp.maximum(jnp.maximum(lses[0], lses[1]), lses[2])
    es = [jnp.exp(l - mx) for l in lses]
    oc = (es[0] * outs[0] + es[1] * outs[1] + es[2] * outs[2]) / (es[0] + es[1] + es[2])
    mix = jnp.dot(oc.astype(BF16), wc_ref[...], preferred_element_type=F32)
    mix = mix + jnp.dot(d_ref[...], wd_ref[...], preferred_element_type=F32)
    o_ref[...] = h_ref[...] + mix


def out_proj_cd(h, oc, lses, od, wc, wd, batch, seq_len):
    M = h.shape[0]
    tm = ROW_TILE
    nrow = seq_len // tm
    width = C_HEADS_PER_GROUP * HEAD_DIM
    row = lambda a: pl.BlockSpec((tm, a.shape[1]), lambda i: (i, 0))
    full = lambda a: pl.BlockSpec(a.shape, lambda i: (0, 0))
    views = [pl.BlockSpec((1, tm // d, d * width), lambda i: (i // nrow, i % nrow, 0)) for _, d in C_PATTERNS]
    n_spare = 2 * sum(d > 1 for _, d in C_PATTERNS)
    return pl.pallas_call(
        _out_proj_cd_kernel,
        grid=(M // tm,),
        in_specs=[row(h), *views, *views, row(od), full(wc), full(wd)],
        out_specs=pl.BlockSpec((tm, D_MODEL), lambda i: (i, 0)),
        out_shape=jax.ShapeDtypeStruct((M, D_MODEL), F32),
        scratch_shapes=[pltpu.VMEM((width // LANE, tm, LANE), F32)] * n_spare,
        compiler_params=_params("parallel"),
        name="out_proj1",
    )(h, *oc, *lses, od, wc, wd)


def _proj1_kernel(x_ref, g_ref, w_ref, cs_ref, c0_ref, c1_ref, c2_ref, y_s):
    xn = _rms(x_ref[...], g_ref[...]).astype(BF16)
    y = jnp.dot(xn, w_ref[...], preferred_element_type=F32) * cs_ref[...]
    tm = y.shape[0]
    width = y.shape[1] // len(C_PATTERNS)
    c0_ref[0] = y[:, :width].astype(BF16)
    nblk = width // LANE
    for j in range(y_s.shape[0]):
        y_s[j] = y[:, width + j * LANE:width + (j + 1) * LANE]
    for g, ref in ((1, c1_ref), (2, c2_ref)):
        dil = C_PATTERNS[g][1]
        for r in range(dil):
            for j in range(nblk):
                rows = y_s[(g - 1) * nblk + j, pl.ds(r, tm // dil, stride=dil), :]
                ref[0, :, r * width + j * LANE:r * width + (j + 1) * LANE] = rows.astype(BF16)


def proj1_call(x, g, w, colscale, batch, seq_len):
    M, K = x.shape
    N = w.shape[1]
    tm = ROW_TILE
    nrow = seq_len // tm
    width = N // len(C_PATTERNS)
    assert [d for _, d in C_PATTERNS][0] == 1
    out_specs = [pl.BlockSpec((1, tm // d, d * width), lambda i: (i // nrow, i % nrow, 0)) for _, d in C_PATTERNS]
    out_shape = [jax.ShapeDtypeStruct((batch, seq_len // d, d * width), BF16) for _, d in C_PATTERNS]
    return pl.pallas_call(
        _proj1_kernel,
        grid=(M // tm,),
        in_specs=[pl.BlockSpec((tm, K), lambda i: (i, 0)),
                  pl.BlockSpec((1, K), lambda i: (0, 0)),
                  pl.BlockSpec((K, N), lambda i: (0, 0)),
                  pl.BlockSpec((1, N), lambda i: (0, 0))],
        out_specs=out_specs,
        out_shape=out_shape,
        scratch_shapes=[pltpu.VMEM(((N - width) // LANE, tm, LANE), F32)],
        compiler_params=_params("parallel"),
        name="proj1",
    )(x, g.reshape(1, K), w, colscale.reshape(1, N))


def _ffn_kernel(*refs, n_mix, final_norm):
    x_ref, mix_refs = refs[0], refs[1:1 + 2 * n_mix]
    g_ref, wg_ref, wu_ref, wd_ref, fg_ref, o_ref, xn_s, acc_s, h_s = refs[1 + 2 * n_mix:]
    f = pl.program_id(1)

    @pl.when(f == 0)
    def _():
        h = x_ref[...]
        for a_ref, w_ref in zip(mix_refs[::2], mix_refs[1::2]):
            h = h + jnp.dot(a_ref[...], w_ref[...], preferred_element_type=F32)
        h_s[...] = h
        xn_s[...] = _rms(h, g_ref[...]).astype(BF16)
        acc_s[...] = jnp.zeros(acc_s.shape, F32)

    xn = xn_s[...]
    gate = jnp.dot(xn, wg_ref[0].astype(BF16), preferred_element_type=F32)
    up = jnp.dot(xn, wu_ref[0].astype(BF16), preferred_element_type=F32)
    mid = (gate / (1.0 + jnp.exp(-gate)) * up).astype(BF16)
    acc_s[...] += jnp.dot(mid, wd_ref[0].astype(BF16), preferred_element_type=F32)

    @pl.when(f == pl.num_programs(1) - 1)
    def _():
        y = h_s[...] + acc_s[...]
        if final_norm:
            y = _rms(y, fg_ref[...])
        o_ref[...] = y


def ffn(x, mix, g, wg, wu, wd, layer, fg, final_norm, name):
    M, K = x.shape
    tm, tf = FFN_ROW_TILE, FFN_COL_TILE
    kern = functools.partial(_ffn_kernel, n_mix=len(mix), final_norm=final_norm)
    mix_specs, mix_args = [], []
    for a, w in mix:
        mix_specs += [pl.BlockSpec((tm, a.shape[1]), lambda i, f: (i, 0)), pl.BlockSpec(w.shape, lambda i, f: (0, 0))]
        mix_args += [a, w]
    return pl.pallas_call(
        kern,
        grid=(M // tm, D_FF // tf),
        in_specs=[pl.BlockSpec((tm, K), lambda i, f: (i, 0)), *mix_specs,
                  pl.BlockSpec((1, K), lambda i, f: (0, 0)),
                  pl.BlockSpec((1, K, tf), lambda i, f: (layer, 0, f)),
                  pl.BlockSpec((1, K, tf), lambda i, f: (layer, 0, f)),
                  pl.BlockSpec((1, tf, K), lambda i, f: (layer, f, 0)),
                  pl.BlockSpec((1, K), lambda i, f: (0, 0))],
        out_specs=pl.BlockSpec((tm, K), lambda i, f: (i, 0)),
        out_shape=jax.ShapeDtypeStruct((M, K), F32),
        scratch_shapes=[pltpu.VMEM((tm, K), BF16), pltpu.VMEM((tm, K), F32), pltpu.VMEM((tm, K), F32)],
        compiler_params=_params("parallel", "arbitrary"),
        name=name,
    )(x, *mix_args, g.reshape(1, K), wg, wu, wd, fg.reshape(1, K))


def _trig_kernel(ang_ref, cos_ref, sin_ref):
    a = ang_ref[...]
    cos_ref[...] = jnp.cos(a)
    sin_ref[...] = jnp.sin(a)


def rope_tables(seq_len):
    half = D_ROPE // 2
    inv = ROPE_THETA ** (-jnp.arange(half, dtype=F32) / half)
    ang = jnp.arange(seq_len).astype(F32)[:, None] * inv[None, :]
    dense = ang.reshape(seq_len * half // LANE, LANE)
    spec = pl.BlockSpec(dense.shape, lambda: (0, 0))
    cos, sin = pl.pallas_call(
        _trig_kernel,
        in_specs=[spec],
        out_specs=[spec, spec],
        out_shape=[jax.ShapeDtypeStruct(dense.shape, F32)] * 2,
        name="rope_trig",
    )(dense)
    cos, sin = cos.reshape(seq_len, half), sin.reshape(seq_len, half)
    pad = LANE - D_NOPE - D_ROPE
    cos_l = jnp.concatenate([jnp.ones((seq_len, D_NOPE), F32), cos, cos, jnp.ones((seq_len, pad), F32)], axis=1)
    sin_l = jnp.concatenate([jnp.zeros((seq_len, D_NOPE), F32), sin, sin, jnp.zeros((seq_len, pad), F32)], axis=1)
    return cos_l, sin_l


def _prep_d_kernel(x_ref, g_ref, wa_ref, qn_ref, kvn_ref, wq_ref, wkv_ref, cos_ref, sin_ref,
                   q_ref, k_ref, v_ref, *, qscale):
    xn = _rms(x_ref[...], g_ref[...]).astype(BF16)
    lat = jnp.dot(xn, wa_ref[...], preferred_element_type=F32)
    cq = _rms(lat[:, :D_Q_LORA], qn_ref[...]).astype(BF16)
    ckv = _rms(lat[:, D_Q_LORA:D_Q_LORA + D_KV_LORA], kvn_ref[...]).astype(BF16)
    o_pe = D_Q_LORA + D_KV_LORA
    cos, sin = cos_ref[...], sin_ref[...]
    kpe = lat[:, o_pe:o_pe + LANE] * cos + lat[:, o_pe + LANE:o_pe + 2 * LANE] * sin
    qq = jnp.dot(cq, wq_ref[...], preferred_element_type=F32)
    kv = jnp.dot(ckv, wkv_ref[...], preferred_element_type=F32)
    kw = D_HEADS * LANE
    ones_blk = _ones_row_block(V_ROWS_D - D_V, x_ref.shape[0])
    for h in range(D_HEADS):
        qh = qq[:, h * LANE:(h + 1) * LANE] * cos + qq[:, kw + h * LANE:kw + (h + 1) * LANE] * sin
        q_ref[0, h, 0] = (qh * qscale).T.astype(BF16)
        k_ref[:, h * LANE:(h + 1) * LANE] = (kv[:, h * LANE:(h + 1) * LANE] + kpe).astype(BF16)
    for j in range(D_HEADS // 2):
        vt = kv[:, kw + j * LANE:kw + (j + 1) * LANE].T.astype(BF16)
        for half in range(2):
            v_ref[0, 2 * j + half, 0, :D_V] = vt[half * D_V:(half + 1) * D_V]
            v_ref[0, 2 * j + half, 0, D_V:] = ones_blk


def prep_d(x, g, wa, qn, kvn, wq, wkv, cos_l, sin_l, batch, seq_len):
    M, K = x.shape
    T = DENSE_TILE
    nrow = seq_len // T
    kw = D_HEADS * LANE
    kern = functools.partial(_prep_d_kernel, qscale=(D_NOPE + D_ROPE) ** -0.5 * LOG2E)
    full = lambda a: pl.BlockSpec(a.shape, lambda i: (0, 0))
    qn2, kvn2, g2 = qn.reshape(1, -1), kvn.reshape(1, -1), g.reshape(1, K)
    return pl.pallas_call(
        kern,
        grid=(M // T,),
        in_specs=[pl.BlockSpec((T, K), lambda i: (i, 0)), full(g2), full(wa), full(qn2), full(kvn2),
                  full(wq), full(wkv),
                  pl.BlockSpec((T, LANE), lambda i: (i % nrow, 0)),
                  pl.BlockSpec((T, LANE), lambda i: (i % nrow, 0))],
        out_specs=[pl.BlockSpec((1, D_HEADS, 1, LANE, T), lambda i: (i // nrow, 0, i % nrow, 0, 0)),
                   pl.BlockSpec((T, kw), lambda i: (i, 0)),
                   pl.BlockSpec((1, D_HEADS, 1, V_ROWS_D, T), lambda i: (i // nrow, 0, i % nrow, 0, 0))],
        out_shape=[jax.ShapeDtypeStruct((batch, D_HEADS, nrow, LANE, T), BF16),
                   jax.ShapeDtypeStruct((M, kw), BF16),
                   jax.ShapeDtypeStruct((batch, D_HEADS, nrow, V_ROWS_D, T), BF16)],
        compiler_params=_params("parallel"),
        name="prep_d",
    )(x, g2, wa, qn2, kvn2, wq, wkv, cos_l, sin_l)


B_HEAD_ORDER = (0, 4, 1, 5, 2, 6, 3, 7)


def _blocks(w, starts, width, axis):
    return jnp.concatenate([lax.slice_in_dim(w, s, s + width, axis=axis) for s in starts], axis=axis)


def _rot_partner_cols(w):
    half = D_ROPE // 2
    return jnp.concatenate([-w[..., half:], w[..., :half]], axis=-1)


def kernel(x, bias_table, attn_norm, ffn_norm, final_norm, ab_w_in, ab_lambda_q1, ab_lambda_k1,
           ab_lambda_q2, ab_lambda_k2, ab_subln, ab_sink, ab_w_o, cd_w_in, cd_q_norm, cd_w_q_b,
           cd_kv_norm, cd_w_kv_b, cd_w_o, ffn_w_gate, ffn_w_up, ffn_w_down):
    B, S, _ = x.shape
    M = B * S
    T = DENSE_TILE
    h = x.reshape(M, D_MODEL)
    qk_scale = HEAD_DIM ** -0.5 * LOG2E

    o3 = A_HEADS * (2 * A_QK_DIM + A_V_DIM)
    a0 = ab_w_in[0].astype(BF16)
    w0 = jnp.concatenate([a0[:, :o3], _blocks(a0, [o3 + hd * HEAD_DIM for hd in B_HEAD_ORDER], HEAD_DIM, 1),
                          a0[:, o3 + B_HEADS * HEAD_DIM:]], axis=1)
    cs0 = np.ones((AB_IN,), np.float32)
    cs0[:A_HEADS * A_QK_DIM] = qk_scale
    cs0[o3:o3 + B_HEADS * HEAD_DIM] = qk_scale
    qat, vat, qbt, vbt, keys0 = proj0_call(h, attn_norm[0], w0, jnp.asarray(cs0), B, S)
    keys0 = keys0.reshape(B, S, -1)
    bias_a = bias_tiles(bias_table, nvar=2 * BIAS_REACH + 1, nheads=A_HEADS, head0=0, rows=T, cols=T,
                        off0=-BIAS_REACH * T, off_step=T, row_coef=1, col_coef=-1, dil=1,
                        half_window=None, name="bias_a")
    oa = attn_a(qat, keys0, vat, bias_a, ab_lambda_q1[0], ab_lambda_k1[0], ab_lambda_q2[0],
                ab_lambda_k2[0], ab_subln[0], 0.8 - 0.6 * math.exp(-0.3 * 0))

    bias_b = bias_tiles(bias_table, nvar=3, nheads=B_HEADS, head0=A_HEADS, rows=QCOLS + 2 * B_HALF_WINDOW,
                        cols=QCOLS, off0=0, off_step=-B_HALF_WINDOW, row_coef=1, col_coef=-1, dil=1,
                        half_window=B_HALF_WINDOW, name="bias_b")
    ob = attn_b(qbt, keys0, (A_HEADS * A_QK_DIM) // LANE, vbt, bias_b, ab_sink[0])

    wo = ab_w_o[0].astype(BF16)
    wo_a = wo[:A_HEADS * A_V_DIM]
    wo_b = _blocks(wo, [A_HEADS * A_V_DIM + hd * HEAD_DIM for hd in B_HEAD_ORDER], HEAD_DIM, 0)
    h = ffn(h, [(oa.reshape(M, -1), wo_a), (ob.reshape(M, -1), wo_b)], ffn_norm[0], ffn_w_gate, ffn_w_up,
            ffn_w_down, 0, final_norm, False, "ffn0")

    w1 = cd_w_in[0]
    gw = C_HEADS_PER_GROUP * HEAD_DIM
    cw = C_HEADS * HEAD_DIM
    starts1 = [role * cw + g * gw for g in range(len(C_PATTERNS)) for role in range(3)]
    cs1 = np.ones((CD_C_IN,), np.float32)
    for g in range(len(C_PATTERNS)):
        cs1[3 * g * gw:(3 * g + 1) * gw] = qk_scale
    c_views = proj1_call(h, attn_norm[1], _blocks(w1.astype(BF16), starts1, gw, 1), jnp.asarray(cs1), B, S)

    oc, lses = [], []
    for g, (window, dil) in enumerate(C_PATTERNS):
        hw = window // (2 * dil)
        tq_c = 128
        bias_c = bias_tiles(bias_table, nvar=3, nheads=C_HEADS_PER_GROUP, head0=g * C_HEADS_PER_GROUP,
                            rows=tq_c, cols=tq_c + 2 * hw, off0=0, off_step=-hw, row_coef=-1, col_coef=1,
                            dil=dil, half_window=hw, name=f"bias_c{g}")
        o_g, lse_g = attn_c(c_views[g], bias_c, dil=dil, tq=tq_c, half_window=hw, name=f"attn_c{g}")
        oc.append(o_g)
        lses.append(lse_g)

    o_kv = CD_C_IN + D_Q_LORA + D_KV_LORA
    w_pe = w1[:, o_kv:]
    lane_pad = lambda w: jnp.pad(w, ((0, 0), (D_NOPE, LANE - D_NOPE - D_ROPE)))
    wa = jnp.concatenate([w1[:, CD_C_IN:o_kv], lane_pad(w_pe), lane_pad(_rot_partner_cols(w_pe))],
                         axis=1).astype(BF16)
    wq3 = cd_w_q_b[0].reshape(D_Q_LORA, D_HEADS, D_NOPE + D_ROPE)
    zpad = jnp.zeros((D_Q_LORA, D_HEADS, LANE - D_NOPE - D_ROPE), F32)
    wq_main = jnp.concatenate([wq3, zpad], axis=-1)
    wq_rot = jnp.concatenate([jnp.zeros_like(wq3[..., :D_NOPE]), _rot_partner_cols(wq3[..., D_NOPE:]), zpad],
                             axis=-1)
    wq = jnp.concatenate([wq_main.reshape(D_Q_LORA, -1), wq_rot.reshape(D_Q_LORA, -1)], axis=1).astype(BF16)
    wkv3 = cd_w_kv_b[0].reshape(D_KV_LORA, D_HEADS, D_NOPE + D_V)
    wk = jnp.pad(wkv3[..., :D_NOPE], ((0, 0), (0, 0), (0, LANE - D_NOPE))).reshape(D_KV_LORA, -1)
    wv = wkv3[..., D_NOPE:].reshape(D_KV_LORA, -1)
    wkv = jnp.concatenate([wk, wv], axis=1).astype(BF16)
    cos_l, sin_l = rope_tables(S)
    qdt, kd, vdt = prep_d(h, attn_norm[1], wa, cd_q_norm[0], cd_kv_norm[0], wq, wkv, cos_l, sin_l, B, S)
    od = attn_d(qdt, kd.reshape(B, S, D_HEADS * LANE), vdt)

    wo1 = cd_w_o[0]
    wo_c = wo1[:C_HEADS_PER_GROUP * HEAD_DIM].astype(BF16)
    wo_d = wo1[C_HEADS_PER_GROUP * HEAD_DIM:].astype(BF16)
    h = out_proj_cd(h, oc, lses, od.reshape(M, -1), wo_c, wo_d, B, S)
    h = ffn(h, [], ffn_norm[1], ffn_w_gate, ffn_w_up, ffn_w_down, 1, final_norm, True, "ffn1")
    return h.reshape(B, S, D_MODEL)
```

```python
import functools
import math

import numpy as np
import jax
import jax.numpy as jnp
from jax import lax
from jax.experimental import pallas as pl
from jax.experimental.pallas import tpu as pltpu

F32 = jnp.float32
BF16 = jnp.bfloat16

D_MODEL = 1024
HEAD_DIM = 64
EPS = 1e-6
NEG = -1e30
LOG2E = math.log2(math.e)
LN2 = math.log(2.0)

A_HEADS = 4
A_QK_DIM = 2 * HEAD_DIM
A_V_DIM = 2 * HEAD_DIM
B_HEADS = 8
B_KV_HEADS = 2
B_HALF_WINDOW = 128
C_PATTERNS = ((128, 1), (512, 4), (2048, 16))
C_HEADS_PER_GROUP = 4
C_HEADS = C_HEADS_PER_GROUP * len(C_PATTERNS)
D_HEADS = 12
D_Q_LORA = 384
D_KV_LORA = 256
D_NOPE = 64
D_ROPE = 32
D_V = 64
ROPE_THETA = 10000.0
NUM_BUCKETS = 32
MAX_DISTANCE = 1024
D_FF = 2816
AB_IN = 2304
CD_C_IN = 3 * C_HEADS * HEAD_DIM

LANE = 128
VMEM_LIMIT = 48 * 1024 * 1024

ROW_TILE = 512
FFN_ROW_TILE = 1024
FFN_COL_TILE = 256
DENSE_TILE = 512
QCOLS = 256
UNROLL = 6
DENSE_QTILES = 4
A_PASSES = 2
D_PASSES = 4
BAND_GROUP = 8
BIAS_REACH = 3
V_ROWS_D = 80
V_ROWS_A = 144


def _bucket_thresholds():
    nb = NUM_BUCKETS // 2
    max_exact = nb // 2
    n = np.arange(1, 4 * MAX_DISTANCE)
    large = max_exact + (np.log(n.astype(np.float32) / np.float32(max_exact))
                         / np.float32(math.log(MAX_DISTANCE / max_exact))
                         * np.float32(nb - max_exact)).astype(np.int32)
    mag = np.where(n < max_exact, n, np.minimum(large, nb - 1))
    return tuple(int(n[np.argmax(mag >= k)]) for k in range(1, nb))


BUCKET_THRESHOLDS = _bucket_thresholds()
assert BUCKET_THRESHOLDS[-1] <= (BIAS_REACH - 1) * DENSE_TILE + 1


def _params(*sem):
    return pltpu.CompilerParams(dimension_semantics=sem, vmem_limit_bytes=VMEM_LIMIT)


def _rms(x, g):
    return x * lax.rsqrt(jnp.mean(x * x, axis=-1, keepdims=True) + EPS) * g


def _ones_row_block(rows, cols):
    r = lax.broadcasted_iota(jnp.int32, (rows, cols), 0)
    return jnp.where(r == 0, 1.0, 0.0).astype(BF16)


def _store_masked_halves(q_ref, idx, qt):
    zeros = jnp.zeros((HEAD_DIM, qt.shape[1]), BF16)
    q_ref[idx + (0, 0, slice(None, HEAD_DIM))] = qt[:HEAD_DIM]
    q_ref[idx + (0, 0, slice(HEAD_DIM, None))] = zeros
    q_ref[idx + (1, 0, slice(None, HEAD_DIM))] = zeros
    q_ref[idx + (1, 0, slice(HEAD_DIM, None))] = qt[HEAD_DIM:]


def _proj0_kernel(x_ref, g_ref, w_ref, cs_ref, qa_ref, va_ref, qb_ref, vb_ref, k_ref):
    xn = _rms(x_ref[...], g_ref[...]).astype(BF16)
    y = jnp.dot(xn, w_ref[...], preferred_element_type=F32) * cs_ref[...]
    tm = y.shape[0]
    ka0 = A_HEADS * A_QK_DIM
    va0 = 2 * ka0
    qb0 = va0 + A_HEADS * A_V_DIM
    kb0 = qb0 + B_HEADS * HEAD_DIM
    vb0 = kb0 + B_KV_HEADS * HEAD_DIM
    for h in range(A_HEADS):
        _store_masked_halves(qa_ref, (0, h), y[:, h * A_QK_DIM:(h + 1) * A_QK_DIM].T.astype(BF16))
        va_ref[0, h, 0, :A_V_DIM] = y[:, va0 + h * A_V_DIM:va0 + (h + 1) * A_V_DIM].T.astype(BF16)
        va_ref[0, h, 0, A_V_DIM:] = _ones_row_block(V_ROWS_A - A_V_DIM, tm)
    for j in range(B_HEADS // 2):
        _store_masked_halves(qb_ref, (0, j), y[:, qb0 + j * LANE:qb0 + (j + 1) * LANE].T.astype(BF16))
    vbt = y[:, vb0:vb0 + LANE].T.astype(BF16)
    ones_blk = _ones_row_block(V_ROWS_D - HEAD_DIM, LANE)
    for g in range(B_KV_HEADS):
        for c in range(tm // LANE):
            vb_ref[0, g, c, :HEAD_DIM] = vbt[g * HEAD_DIM:(g + 1) * HEAD_DIM, c * LANE:(c + 1) * LANE]
            vb_ref[0, g, c, HEAD_DIM:] = ones_blk
    k_ref[:, :ka0] = y[:, ka0:va0].astype(BF16)
    k_ref[:, ka0:] = y[:, kb0:vb0].astype(BF16)


def proj0_call(x, g, w, colscale, batch, seq_len):
    M, K = x.shape
    N = w.shape[1]
    T = DENSE_TILE
    nrow = seq_len // T
    nkb = A_HEADS * A_QK_DIM + B_KV_HEADS * HEAD_DIM
    qspec = pl.BlockSpec((1, A_HEADS, 2, 1, LANE, T), lambda i: (i // nrow, 0, 0, i % nrow, 0, 0))
    qshape = jax.ShapeDtypeStruct((batch, A_HEADS, 2, nrow, LANE, T), BF16)
    return pl.pallas_call(
        _proj0_kernel,
        grid=(M // T,),
        in_specs=[pl.BlockSpec((T, K), lambda i: (i, 0)),
                  pl.BlockSpec((1, K), lambda i: (0, 0)),
                  pl.BlockSpec((K, N), lambda i: (0, 0)),
                  pl.BlockSpec((1, N), lambda i: (0, 0))],
        out_specs=[qspec,
                   pl.BlockSpec((1, A_HEADS, 1, V_ROWS_A, T), lambda i: (i // nrow, 0, i % nrow, 0, 0)),
                   qspec,
                   pl.BlockSpec((1, B_KV_HEADS, T // LANE, V_ROWS_D, LANE),
                                lambda i: (i // nrow, 0, i % nrow, 0, 0)),
                   pl.BlockSpec((T, nkb), lambda i: (i, 0))],
        out_shape=[qshape,
                   jax.ShapeDtypeStruct((batch, A_HEADS, nrow, V_ROWS_A, T), BF16),
                   qshape,
                   jax.ShapeDtypeStruct((batch, B_KV_HEADS, seq_len // LANE, V_ROWS_D, LANE), BF16),
                   jax.ShapeDtypeStruct((M, nkb), BF16)],
        compiler_params=_params("parallel"),
        name="proj0",
    )(x, g.reshape(1, K), w, colscale.reshape(1, N))


def _bias_kernel(tab_ref, o_ref, *, nvar, off0, off_step, row_coef, col_coef, dil, half_window, head0):
    hcol = head0 + pl.program_id(1)
    R, C = o_ref.shape[-2:]
    row = lax.broadcasted_iota(jnp.int32, (R, C), 0)
    col = lax.broadcasted_iota(jnp.int32, (R, C), 1)
    base = row_coef * row + col_coef * col
    span_lo = min(row_coef * (R - 1), 0) + min(col_coef * (C - 1), 0)
    span_hi = max(row_coef * (R - 1), 0) + max(col_coef * (C - 1), 0)
    nb = NUM_BUCKETS // 2

    def side(n, n_lo, n_hi, row0):
        val = jnp.full((R, C), tab_ref[row0 + sum(t <= n_lo for t in BUCKET_THRESHOLDS), hcol], F32)
        for k, thr in enumerate(BUCKET_THRESHOLDS, start=1):
            if n_lo < thr <= n_hi:
                val = jnp.where(n >= thr, tab_ref[row0 + k, hcol], val)
        return val

    for v in range(nvar):
        @pl.when(pl.program_id(0) == v)
        def _(v=v):
            off = off0 + v * off_step
            rel = off + base
            lo, hi = (off + span_lo) * dil, (off + span_hi) * dil
            dist = rel * dil
            n = jnp.abs(dist)
            if hi <= 0:
                val = side(n, -hi, -lo, 0)
            elif lo > 0:
                val = side(n, lo, hi, nb)
            else:
                val = jnp.where(dist > 0, side(n, 1, hi, nb), side(n, 0, -lo, 0))
            val = val * LOG2E
            if half_window is not None:
                val = jnp.where(jnp.abs(rel) <= half_window, val, NEG)
            o_ref[0, 0] = val


def bias_tiles(table, *, nvar, nheads, head0, rows, cols, off0, off_step, row_coef, col_coef,
               dil, half_window, name):
    kern = functools.partial(_bias_kernel, nvar=nvar, off0=off0, off_step=off_step, row_coef=row_coef,
                             col_coef=col_coef, dil=dil, half_window=half_window, head0=head0)
    return pl.pallas_call(
        kern,
        grid=(nvar, nheads),
        in_specs=[pl.BlockSpec(memory_space=pltpu.SMEM)],
        out_specs=pl.BlockSpec((1, 1, rows, cols), lambda v, h: (v, h, 0, 0)),
        out_shape=jax.ShapeDtypeStruct((nvar, nheads, rows, cols), F32),
        compiler_params=_params("parallel", "parallel"),
        name=name,
    )(table)


def _dense_pipeline(nk, tile, n_streams, score_fn, value_fn, m_s, acc_s, bufs):
    units = [(i, c * QCOLS) for i in range(n_streams) for c in range(tile // QCOLS)]

    def produce(kc, unit, nxt):
        i, c0 = unit
        cols = slice(c0, c0 + QCOLS)
        s = score_fn(kc + 1, i, cols)
        nxt[0][i, :, cols] = s
        nxt[1][i, :, cols] = jnp.max(s, axis=0, keepdims=True)

    def consume(kc, unit, cur):
        i, c0 = unit
        cols = slice(c0, c0 + QCOLS)
        m_old = m_s[i, :, cols]
        m_new = jnp.maximum(m_old, cur[1][i, :, cols])
        alpha = jnp.exp2(m_old - m_new)
        p = jnp.exp2((cur[0][i, :, cols] - m_new).astype(BF16))
        acc_s[i, :, cols] = (alpha * acc_s[i, :, cols]
                             + jnp.dot(value_fn(kc, i), p, preferred_element_type=F32))
        m_s[i, :, cols] = m_new

    def stage(kc, cur, nxt):
        for unit in units:
            if nxt is not None:
                produce(kc, unit, nxt)
            if cur is not None:
                consume(kc, unit, cur)

    m_s[...] = jnp.full(m_s.shape, NEG, F32)
    acc_s[...] = jnp.zeros(acc_s.shape, F32)
    stage(-1, None, bufs[0])
    n_loop = (nk - 1) // UNROLL

    def body(j, carry):
        for u in range(UNROLL):
            stage(UNROLL * j + u, bufs[u % 2], bufs[(u + 1) % 2])
        return carry

    lax.fori_loop(0, n_loop, body, 0)
    for kc in range(n_loop * UNROLL, nk):
        stage(kc, bufs[kc % 2], bufs[(kc + 1) % 2] if kc < nk - 1 else None)


def _attn_a_kernel(q_ref, k_ref, v_ref, bias_ref, lq1_ref, lk1_ref, lq2_ref, lk2_ref, subln_ref, o_ref,
                   m_s, acc_s, s_a, s_b, cm_a, cm_b, *, tile, nk, qtiles, npass, lambda_init):
    lam = (jnp.exp(jnp.sum(lq1_ref[...] * lk1_ref[...], axis=-1, keepdims=True))
           - jnp.exp(jnp.sum(lq2_ref[...] * lk2_ref[...], axis=-1, keepdims=True)) + lambda_init)

    def one_pass(p, carry):
        q0 = p * qtiles
        qi0 = pl.program_id(2) * (qtiles * npass) + q0

        def score_fn(kc, i, cols):
            j, qt = divmod(i, qtiles)
            kblk = k_ref[0, pl.ds(pl.multiple_of(kc * tile, tile), tile), :]
            bt = bias_ref[jnp.clip(kc - (qi0 + qt), -BIAS_REACH, BIAS_REACH) + BIAS_REACH, 0, :, cols]
            return jnp.dot(kblk, q_ref[0, 0, j, q0 + qt, :, cols], preferred_element_type=F32) + bt

        def value_fn(kc, i):
            return v_ref[0, 0, kc]

        _dense_pipeline(nk, tile, 2 * qtiles, score_fn, value_fn, m_s, acc_s, ((s_a, cm_a), (s_b, cm_b)))
        for qt in range(qtiles):
            a1, a2 = acc_s[qt], acc_s[qtiles + qt]
            o = (a1[:A_V_DIM] / a1[A_V_DIM:A_V_DIM + 1]
                 - lam * (a2[:A_V_DIM] / a2[A_V_DIM:A_V_DIM + 1]))
            ms = jnp.mean(o * o, axis=0, keepdims=True)
            y = o * lax.rsqrt(ms + EPS) * subln_ref[...] * (1.0 - lambda_init)
            rows = pl.ds(pl.multiple_of((q0 + qt) * tile, tile), tile)
            o_ref[0, rows] = y.T.astype(o_ref.dtype)
        return carry

    lax.fori_loop(0, npass, one_pass, 0)


def attn_a(qt, karr, vt, bias, lq1, lk1, lq2, lk2, subln, lambda_init):
    B, S, _ = karr.shape
    T = DENSE_TILE
    nq = nk = S // T
    assert nk % 2 == 0 and nk >= 4
    qtiles = math.gcd(nq, DENSE_QTILES)
    npass = math.gcd(nq // qtiles, A_PASSES)
    ns = 2 * qtiles
    kern = functools.partial(_attn_a_kernel, tile=T, nk=nk, qtiles=qtiles, npass=npass, lambda_init=lambda_init)
    vec = lambda n: pl.BlockSpec((1, n), lambda h, b, i: (0, 0))
    return pl.pallas_call(
        kern,
        grid=(A_HEADS, B, nq // (qtiles * npass)),
        in_specs=[pl.BlockSpec((1, 1, 2, qtiles * npass, LANE, T), lambda h, b, i: (b, h, 0, i, 0, 0)),
                  pl.BlockSpec((1, S, LANE), lambda h, b, i: (b, 0, h)),
                  pl.BlockSpec((1, 1, nk, V_ROWS_A, T), lambda h, b, i: (b, h, 0, 0, 0)),
                  pl.BlockSpec((2 * BIAS_REACH + 1, 1, T, T), lambda h, b, i: (0, h, 0, 0),
                               pipeline_mode=pl.Buffered(1)),
                  vec(HEAD_DIM), vec(HEAD_DIM), vec(HEAD_DIM), vec(HEAD_DIM),
                  pl.BlockSpec((A_V_DIM, 1), lambda h, b, i: (0, 0))],
        out_specs=pl.BlockSpec((1, qtiles * npass * T, LANE), lambda h, b, i: (b, i, h)),
        out_shape=jax.ShapeDtypeStruct((B, S, A_HEADS * A_V_DIM), BF16),
        scratch_shapes=[pltpu.VMEM((ns, 1, T), F32), pltpu.VMEM((ns, V_ROWS_A, T), F32),
                        pltpu.VMEM((ns, T, T), F32), pltpu.VMEM((ns, T, T), F32),
                        pltpu.VMEM((ns, 1, T), F32), pltpu.VMEM((ns, 1, T), F32)],
        compiler_params=_params("parallel", "parallel", "arbitrary"),
        name="attn_a",
    )(qt, karr, vt, bias, lq1.reshape(1, -1), lk1.reshape(1, -1), lq2.reshape(1, -1),
      lk2.reshape(1, -1), subln.reshape(-1, 1))


def _attn_d_kernel(q_ref, k_ref, v_ref, o_ref, m_s, acc_s, s_a, s_b, cm_a, cm_b, *, tile, nk, qtiles, npass):
    def one_pass(p, carry):
        q0 = p * qtiles

        def score_fn(kc, i, cols):
            hh, qt = divmod(i, qtiles)
            kblk = k_ref[0, pl.ds(pl.multiple_of(kc * tile, tile), tile), hh * LANE:(hh + 1) * LANE]
            return jnp.dot(kblk, q_ref[0, hh, q0 + qt, :, cols], preferred_element_type=F32)

        def value_fn(kc, i):
            return v_ref[0, i // qtiles, kc]

        _dense_pipeline(nk, tile, 2 * qtiles, score_fn, value_fn, m_s, acc_s, ((s_a, cm_a), (s_b, cm_b)))
        for qt in range(qtiles):
            outs = []
            for hh in range(2):
                acc = acc_s[hh * qtiles + qt]
                outs.append(acc[:D_V] / acc[D_V:D_V + 1])
            rows = pl.ds(pl.multiple_of((q0 + qt) * tile, tile), tile)
            o_ref[0, rows] = jnp.concatenate(outs, axis=0).T.astype(o_ref.dtype)
        return carry

    lax.fori_loop(0, npass, one_pass, 0)


def attn_d(qt, k, vt):
    B, S, _ = k.shape
    T = DENSE_TILE
    nq = nk = S // T
    qtiles = math.gcd(nq, DENSE_QTILES)
    npass = math.gcd(nq // qtiles, D_PASSES)
    ns = 2 * qtiles
    kern = functools.partial(_attn_d_kernel, tile=T, nk=nk, qtiles=qtiles, npass=npass)
    return pl.pallas_call(
        kern,
        grid=(B, D_HEADS // 2, nq // (qtiles * npass)),
        in_specs=[pl.BlockSpec((1, 2, qtiles * npass, LANE, T), lambda b, h, i: (b, h, i, 0, 0)),
                  pl.BlockSpec((1, S, 2 * LANE), lambda b, h, i: (b, 0, h)),
                  pl.BlockSpec((1, 2, nk, V_ROWS_D, T), lambda b, h, i: (b, h, 0, 0, 0))],
        out_specs=pl.BlockSpec((1, qtiles * npass * T, LANE), lambda b, h, i: (b, i, h)),
        out_shape=jax.ShapeDtypeStruct((B, S, D_HEADS * D_V), BF16),
        scratch_shapes=[pltpu.VMEM((ns, 1, T), F32), pltpu.VMEM((ns, V_ROWS_D, T), F32),
                        pltpu.VMEM((ns, T, T), F32), pltpu.VMEM((ns, T, T), F32),
                        pltpu.VMEM((ns, 1, T), F32), pltpu.VMEM((ns, 1, T), F32)],
        compiler_params=_params("parallel", "parallel", "arbitrary"),
        name="attn_d",
    )(qt, k, vt)


def _attn_b_kernel(q_ref, k_ref, v_ref, bias_ref, sink_ref, o_ref, s_a, s_b, s_c, cm_a, cm_b, cm_c, *, tile, seq_len):
    half_window = B_HALF_WINDOW
    span = QCOLS + 2 * half_window
    nchunk = span // LANE
    step = pl.program_id(1)

    def window(c):
        q0 = step * tile + c * QCOLS
        start = pl.multiple_of(jnp.clip(q0 - half_window, 0, seq_len - span), LANE)
        variant = jnp.where(q0 == 0, 0, jnp.where(q0 + QCOLS == seq_len, 2, 1))
        return start, variant

    def scores(task):
        c, j, g = task
        start, variant = window(c)
        kwin = k_ref[0, pl.ds(start, span), :]
        s = jnp.dot(kwin, q_ref[0, j, g, 0, :, c * QCOLS:(c + 1) * QCOLS], preferred_element_type=F32)
        return s + bias_ref[variant, j + (B_HEADS // 2) * g]

    def finish(task, s_ref, cm_ref):
        c, j, g = task
        head = j + (B_HEADS // 2) * g
        start, _ = window(c)
        chunk0 = start // LANE
        vwin = jnp.concatenate([v_ref[0, g, chunk0 + n] for n in range(nchunk)], axis=1)
        sk = sink_ref[:, head:head + 1] * LOG2E
        m = jnp.maximum(cm_ref[...], sk)
        e = jnp.exp2((s_ref[...] - m).astype(BF16))
        ov = jnp.dot(vwin, e, preferred_element_type=F32)
        denom = ov[HEAD_DIM:HEAD_DIM + 1] + jnp.exp2(sk - m)
        return ov[:HEAD_DIM] / denom

    tasks = [(c, j, g) for c in range(tile // QCOLS) for j in range(B_HEADS // 2) for g in range(B_KV_HEADS)]
    bufs = ((s_a, cm_a), (s_b, cm_b), (s_c, cm_c))

    def produce(task, buf):
        s = scores(task)
        buf[0][...] = s
        buf[1][...] = jnp.max(s, axis=0, keepdims=True)

    produce(tasks[0], bufs[0])
    produce(tasks[1], bufs[1])
    held = None
    for n, task in enumerate(tasks):
        if n + 2 < len(tasks):
            produce(tasks[n + 2], bufs[(n + 2) % 3])
        o = finish(task, *bufs[n % 3])
        c, j, g = task
        if g == 0:
            held = o
            continue
        pair = jnp.concatenate([held, o], axis=0).T
        o_ref[0, c * QCOLS:(c + 1) * QCOLS, j * LANE:(j + 1) * LANE] = pair.astype(o_ref.dtype)


def attn_b(qt, karr, k_block, vt, bias, sink):
    B, S, _ = karr.shape
    T = DENSE_TILE
    span = QCOLS + 2 * B_HALF_WINDOW
    return pl.pallas_call(
        functools.partial(_attn_b_kernel, tile=T, seq_len=S),
        grid=(B, S // T),
        in_specs=[pl.BlockSpec((1, B_HEADS // 2, 2, 1, LANE, T), lambda b, i: (b, 0, 0, i, 0, 0)),
                  pl.BlockSpec((1, S, LANE), lambda b, i: (b, 0, k_block)),
                  pl.BlockSpec((1, B_KV_HEADS, S // LANE, V_ROWS_D, LANE), lambda b, i: (b, 0, 0, 0, 0)),
                  pl.BlockSpec(bias.shape, lambda b, i: (0, 0, 0, 0)),
                  pl.BlockSpec((1, B_HEADS), lambda b, i: (0, 0))],
        out_specs=pl.BlockSpec((1, T, B_HEADS * HEAD_DIM), lambda b, i: (b, i, 0)),
        out_shape=jax.ShapeDtypeStruct((B, S, B_HEADS * HEAD_DIM), BF16),
        scratch_shapes=[pltpu.VMEM((span, QCOLS), F32)] * 3 + [pltpu.VMEM((1, QCOLS), F32)] * 3,
        compiler_params=_params("parallel", "arbitrary"),
        name="attn_b",
    )(qt, karr, vt, bias, sink.reshape(1, -1))


def _attn_c_kernel(q_ref, k_ref, v_ref, bias_ref, o_ref, lse_ref, *, tq, group, span, half_window, seq_len):
    nq = seq_len // tq
    nblk = q_ref.shape[-1] // LANE
    lane = lax.broadcasted_iota(jnp.int32, (tq, LANE), 1)
    low = lane < HEAD_DIM

    def window(g):
        t = pl.program_id(2) * group + g
        start = pl.multiple_of(jnp.clip(t * tq - half_window, 0, seq_len - span), half_window)
        variant = jnp.where(t == 0, 0, jnp.where(t == nq - 1, 2, 1))
        return start, variant

    def scores(task):
        g, c, half = task
        start, variant = window(g)
        q2 = q_ref[0, g * tq:(g + 1) * tq, c * LANE:(c + 1) * LANE]
        kb = k_ref[0, pl.ds(start, span), c * LANE:(c + 1) * LANE]
        qm = jnp.where(low if half == 0 else jnp.logical_not(low), q2, jnp.zeros_like(q2))
        s = lax.dot_general(qm, kb, (((1,), (1,)), ((), ())), preferred_element_type=F32)
        return s + bias_ref[variant, 2 * c + half]

    def finish(task, s):
        g, c, half = task
        start, _ = window(g)
        vb = v_ref[0, pl.ds(start, span), c * LANE:(c + 1) * LANE]
        m = jnp.max(s, axis=-1, keepdims=True)
        e = jnp.exp2(s - m)
        denom = jnp.sum(e, axis=-1, keepdims=True)
        o = jnp.dot(e.astype(BF16), vb, preferred_element_type=F32) / denom
        return o, LN2 * m + jnp.log(denom)

    tasks = [(g, c, half) for g in range(group) for c in range(nblk) for half in range(2)]
    s_next = scores(tasks[0])
    held = None
    for n, task in enumerate(tasks):
        s = s_next
        if n + 1 < len(tasks):
            s_next = scores(tasks[n + 1])
        o, lse = finish(task, s)
        g, c, half = task
        if half == 0:
            held = (o, lse)
            continue
        rows, cols = slice(g * tq, (g + 1) * tq), slice(c * LANE, (c + 1) * LANE)
        o_ref[0, rows, cols] = jnp.where(low, held[0], o)
        lse_ref[0, rows, cols] = jnp.where(low, held[1], lse)


def attn_c(view, bias, *, dil, tq, half_window, name):
    B, L, _ = view.shape
    width = C_HEADS_PER_GROUP * HEAD_DIM
    span = tq + 2 * half_window
    nq = L // tq
    assert L % tq == 0 and L >= span and nq >= 2
    group = math.gcd(nq, BAND_GROUP)
    kern = functools.partial(_attn_c_kernel, tq=tq, group=group, span=span, half_window=half_window, seq_len=L)
    out_spec = pl.BlockSpec((1, group * tq, width), lambda b, r, t: (b, t, r))
    out_shape = jax.ShapeDtypeStruct((B, L, dil * width), F32)
    return pl.pallas_call(
        kern,
        grid=(B, dil, nq // group),
        in_specs=[pl.BlockSpec((1, group * tq, width), lambda b, r, t: (b, t, 3 * r)),
                  pl.BlockSpec((1, L, width), lambda b, r, t: (b, 0, 3 * r + 1)),
                  pl.BlockSpec((1, L, width), lambda b, r, t: (b, 0, 3 * r + 2)),
                  pl.BlockSpec(bias.shape, lambda b, r, t: (0, 0, 0, 0))],
        out_specs=[out_spec, out_spec],
        out_shape=[out_shape, out_shape],
        compiler_params=_params("parallel", "parallel", "arbitrary"),
        name=name,
    )(view, view, view, bias)


def _mix_c_kernel(o0, o1, o2, s0, s1, s2, o_ref, *scratch):
    tm = o_ref.shape[0]
    width = C_HEADS_PER_GROUP * HEAD_DIM
    spare = iter(scratch)

    def in_position_order(ref, dil):
        if dil == 1:
            return ref[0]
        t_s = next(spare)
        for r in range(dil):
            for j in range(width // LANE):
                t_s[j, pl.ds(r, tm // dil, stride=dil), :] = ref[0, :, r * width + j * LANE:r * width + (j + 1) * LANE]
        return jnp.concatenate([t_s[j] for j in range(width // LANE)], axis=1)

    dils = [d for _, d in C_PATTERNS]
    outs = [in_position_order(r, d) for r, d in zip((o0, o1, o2), dils)]
    lses = [in_position_order(r, d) for r, d in zip((s0, s1, s2), dils)]
    mx = jnp.maximum(jnp.maximum(lses[0], lses[1]), lses[2])
    es = [jnp.exp(l - mx) for l in lses]
    oc = (es[0] * outs[0] + es[1] * outs[1] + es[2] * outs[2]) / (es[0] + es[1] + es[2])
    o_ref[...] = oc.astype(o_ref.dtype)


def mix_c(oc, lses, batch, seq_len):
    M = batch * seq_len
    tm = ROW_TILE
    nrow = seq_len // tm
    width = C_HEADS_PER_GROUP * HEAD_DIM
    views = [pl.BlockSpec((1, tm // d, d * width), lambda i: (i // nrow, i % nrow, 0)) for _, d in C_PATTERNS]
    n_spare = 2 * sum(d > 1 for _, d in C_PATTERNS)
    return pl.pallas_call(
        _mix_c_kernel,
        grid=(M // tm,),
        in_specs=[*views, *views],
        out_specs=pl.BlockSpec((tm, width), lambda i: (i, 0)),
        out_shape=jax.ShapeDtypeStruct((M, width), BF16),
        scratch_shapes=[pltpu.VMEM((width // LANE, tm, LANE), F32)] * n_spare,
        compiler_params=_params("parallel"),
        name="mix_c",
    )(*oc, *lses)


def _proj1_kernel(x_ref, g_ref, w_ref, cs_ref, c0_ref, c1_ref, c2_ref, y_s):
    xn = _rms(x_ref[...], g_ref[...]).astype(BF16)
    y = jnp.dot(xn, w_ref[...], preferred_element_type=F32) * cs_ref[...]
    tm = y.shape[0]
    width = y.shape[1] // len(C_PATTERNS)
    c0_ref[0] = y[:, :width].astype(BF16)
    nblk = width // LANE
    for j in range(y_s.shape[0]):
        y_s[j] = y[:, width + j * LANE:width + (j + 1) * LANE]
    for g, ref in ((1, c1_ref), (2, c2_ref)):
        dil = C_PATTERNS[g][1]
        for r in range(dil):
            for j in range(nblk):
                rows = y_s[(g - 1) * nblk + j, pl.ds(r, tm // dil, stride=dil), :]
                ref[0, :, r * width + j * LANE:r * width + (j + 1) * LANE] = rows.astype(BF16)


def proj1_call(x, g, w, colscale, batch, seq_len):
    M, K = x.shape
    N = w.shape[1]
    tm = ROW_TILE
    nrow = seq_len // tm
    width = N // len(C_PATTERNS)
    assert [d for _, d in C_PATTERNS][0] == 1
    out_specs = [pl.BlockSpec((1, tm // d, d * width), lambda i: (i // nrow, i % nrow, 0)) for _, d in C_PATTERNS]
    out_shape = [jax.ShapeDtypeStruct((batch, seq_len // d, d * width), BF16) for _, d in C_PATTERNS]
    return pl.pallas_call(
        _proj1_kernel,
        grid=(M // tm,),
        in_specs=[pl.BlockSpec((tm, K), lambda i: (i, 0)),
                  pl.BlockSpec((1, K), lambda i: (0, 0)),
                  pl.BlockSpec((K, N), lambda i: (0, 0)),
                  pl.BlockSpec((1, N), lambda i: (0, 0))],
        out_specs=out_specs,
        out_shape=out_shape,
        scratch_shapes=[pltpu.VMEM(((N - width) // LANE, tm, LANE), F32)],
        compiler_params=_params("parallel"),
        name="proj1",
    )(x, g.reshape(1, K), w, colscale.reshape(1, N))


def _ffn_kernel(*refs, n_mix, final_norm):
    x_ref, mix_refs = refs[0], refs[1:1 + 2 * n_mix]
    g_ref, wg_ref, wu_ref, wd_ref, fg_ref, o_ref, xn_s, acc_s, h_s = refs[1 + 2 * n_mix:]
    f = pl.program_id(1)

    @pl.when(f == 0)
    def _():
        h = x_ref[...]
        for a_ref, w_ref in zip(mix_refs[::2], mix_refs[1::2]):
            h = h + jnp.dot(a_ref[...], w_ref[...], preferred_element_type=F32)
        h_s[...] = h
        xn_s[...] = _rms(h, g_ref[...]).astype(BF16)
        acc_s[...] = jnp.zeros(acc_s.shape, F32)

    xn = xn_s[...]
    gate = jnp.dot(xn, wg_ref[0].astype(BF16), preferred_element_type=F32)
    up = jnp.dot(xn, wu_ref[0].astype(BF16), preferred_element_type=F32)
    mid = (gate / (1.0 + jnp.exp(-gate)) * up).astype(BF16)
    acc_s[...] += jnp.dot(mid, wd_ref[0].astype(BF16), preferred_element_type=F32)

    @pl.when(f == pl.num_programs(1) - 1)
    def _():
        y = h_s[...] + acc_s[...]
        if final_norm:
            y = _rms(y, fg_ref[...])
        o_ref[...] = y


def ffn(x, mix, g, wg, wu, wd, layer, fg, final_norm, name):
    M, K = x.shape
    tm, tf = FFN_ROW_TILE, FFN_COL_TILE
    kern = functools.partial(_ffn_kernel, n_mix=len(mix), final_norm=final_norm)
    mix_specs, mix_args = [], []
    for a, w in mix:
        mix_specs += [pl.BlockSpec((tm, a.shape[1]), lambda i, f: (i, 0)), pl.BlockSpec(w.shape, lambda i, f: (0, 0))]
        mix_args += [a, w]
    return pl.pallas_call(
        kern,
        grid=(M // tm, D_FF // tf),
        in_specs=[pl.BlockSpec((tm, K), lambda i, f: (i, 0)), *mix_specs,
                  pl.BlockSpec((1, K), lambda i, f: (0, 0)),
                  pl.BlockSpec((1, K, tf), lambda i, f: (layer, 0, f)),
                  pl.BlockSpec((1, K, tf), lambda i, f: (layer, 0, f)),
                  pl.BlockSpec((1, tf, K), lambda i, f: (layer, f, 0)),
                  pl.BlockSpec((1, K), lambda i, f: (0, 0))],
        out_specs=pl.BlockSpec((tm, K), lambda i, f: (i, 0)),
        out_shape=jax.ShapeDtypeStruct((M, K), F32),
        scratch_shapes=[pltpu.VMEM((tm, K), BF16), pltpu.VMEM((tm, K), F32), pltpu.VMEM((tm, K), F32)],
        compiler_params=_params("parallel", "arbitrary"),
        name=name,
    )(x, *mix_args, g.reshape(1, K), wg, wu, wd, fg.reshape(1, K))


def _trig_kernel(ang_ref, cos_ref, sin_ref):
    a = ang_ref[...]
    cos_ref[...] = jnp.cos(a)
    sin_ref[...] = jnp.sin(a)


def rope_tables(seq_len):
    half = D_ROPE // 2
    inv = ROPE_THETA ** (-jnp.arange(half, dtype=F32) / half)
    ang = jnp.arange(seq_len).astype(F32)[:, None] * inv[None, :]
    dense = ang.reshape(seq_len * half // LANE, LANE)
    spec = pl.BlockSpec(dense.shape, lambda: (0, 0))
    cos, sin = pl.pallas_call(
        _trig_kernel,
        in_specs=[spec],
        out_specs=[spec, spec],
        out_shape=[jax.ShapeDtypeStruct(dense.shape, F32)] * 2,
        name="rope_trig",
    )(dense)
    cos, sin = cos.reshape(seq_len, half), sin.reshape(seq_len, half)
    pad = LANE - D_NOPE - D_ROPE
    cos_l = jnp.concatenate([jnp.ones((seq_len, D_NOPE), F32), cos, cos, jnp.ones((seq_len, pad), F32)], axis=1)
    sin_l = jnp.concatenate([jnp.zeros((seq_len, D_NOPE), F32), sin, sin, jnp.zeros((seq_len, pad), F32)], axis=1)
    return cos_l, sin_l


def _prep_d_kernel(x_ref, g_ref, wa_ref, qn_ref, kvn_ref, wq_ref, wkv_ref, cos_ref, sin_ref,
                   q_ref, k_ref, v_ref, *, qscale):
    xn = _rms(x_ref[...], g_ref[...]).astype(BF16)
    lat = jnp.dot(xn, wa_ref[...], preferred_element_type=F32)
    cq = _rms(lat[:, :D_Q_LORA], qn_ref[...]).astype(BF16)
    ckv = _rms(lat[:, D_Q_LORA:D_Q_LORA + D_KV_LORA], kvn_ref[...]).astype(BF16)
    o_pe = D_Q_LORA + D_KV_LORA
    cos, sin = cos_ref[...], sin_ref[...]
    kpe = lat[:, o_pe:o_pe + LANE] * cos + lat[:, o_pe + LANE:o_pe + 2 * LANE] * sin
    qq = jnp.dot(cq, wq_ref[...], preferred_element_type=F32)
    kv = jnp.dot(ckv, wkv_ref[...], preferred_element_type=F32)
    kw = D_HEADS * LANE
    ones_blk = _ones_row_block(V_ROWS_D - D_V, x_ref.shape[0])
    for h in range(D_HEADS):
        qh = qq[:, h * LANE:(h + 1) * LANE] * cos + qq[:, kw + h * LANE:kw + (h + 1) * LANE] * sin
        q_ref[0, h, 0] = (qh * qscale).T.astype(BF16)
        k_ref[:, h * LANE:(h + 1) * LANE] = (kv[:, h * LANE:(h + 1) * LANE] + kpe).astype(BF16)
    for j in range(D_HEADS // 2):
        vt = kv[:, kw + j * LANE:kw + (j + 1) * LANE].T.astype(BF16)
        for half in range(2):
            v_ref[0, 2 * j + half, 0, :D_V] = vt[half * D_V:(half + 1) * D_V]
            v_ref[0, 2 * j + half, 0, D_V:] = ones_blk


def prep_d(x, g, wa, qn, kvn, wq, wkv, cos_l, sin_l, batch, seq_len):
    M, K = x.shape
    T = DENSE_TILE
    nrow = seq_len // T
    kw = D_HEADS * LANE
    kern = functools.partial(_prep_d_kernel, qscale=(D_NOPE + D_ROPE) ** -0.5 * LOG2E)
    full = lambda a: pl.BlockSpec(a.shape, lambda i: (0, 0))
    qn2, kvn2, g2 = qn.reshape(1, -1), kvn.reshape(1, -1), g.reshape(1, K)
    return pl.pallas_call(
        kern,
        grid=(M // T,),
        in_specs=[pl.BlockSpec((T, K), lambda i: (i, 0)), full(g2), full(wa), full(qn2), full(kvn2),
                  full(wq), full(wkv),
                  pl.BlockSpec((T, LANE), lambda i: (i % nrow, 0)),
                  pl.BlockSpec((T, LANE), lambda i: (i % nrow, 0))],
        out_specs=[pl.BlockSpec((1, D_HEADS, 1, LANE, T), lambda i: (i // nrow, 0, i % nrow, 0, 0)),
                   pl.BlockSpec((T, kw), lambda i: (i, 0)),
                   pl.BlockSpec((1, D_HEADS, 1, V_ROWS_D, T), lambda i: (i // nrow, 0, i % nrow, 0, 0))],
        out_shape=[jax.ShapeDtypeStruct((batch, D_HEADS, nrow, LANE, T), BF16),
                   jax.ShapeDtypeStruct((M, kw), BF16),
                   jax.ShapeDtypeStruct((batch, D_HEADS, nrow, V_ROWS_D, T), BF16)],
        compiler_params=_params("parallel"),
        name="prep_d",
    )(x, g2, wa, qn2, kvn2, wq, wkv, cos_l, sin_l)


B_HEAD_ORDER = (0, 4, 1, 5, 2, 6, 3, 7)


def _blocks(w, starts, width, axis):
    return jnp.concatenate([lax.slice_in_dim(w, s, s + width, axis=axis) for s in starts], axis=axis)


def _rot_partner_cols(w):
    half = D_ROPE // 2
    return jnp.concatenate([-w[..., half:], w[..., :half]], axis=-1)


def kernel(x, bias_table, attn_norm, ffn_norm, final_norm, ab_w_in, ab_lambda_q1, ab_lambda_k1,
           ab_lambda_q2, ab_lambda_k2, ab_subln, ab_sink, ab_w_o, cd_w_in, cd_q_norm, cd_w_q_b,
           cd_kv_norm, cd_w_kv_b, cd_w_o, ffn_w_gate, ffn_w_up, ffn_w_down):
    B, S, _ = x.shape
    M = B * S
    T = DENSE_TILE
    h = x.reshape(M, D_MODEL)
    qk_scale = HEAD_DIM ** -0.5 * LOG2E

    o3 = A_HEADS * (2 * A_QK_DIM + A_V_DIM)
    a0 = ab_w_in[0].astype(BF16)
    w0 = jnp.concatenate([a0[:, :o3], _blocks(a0, [o3 + hd * HEAD_DIM for hd in B_HEAD_ORDER], HEAD_DIM, 1),
                          a0[:, o3 + B_HEADS * HEAD_DIM:]], axis=1)
    cs0 = np.ones((AB_IN,), np.float32)
    cs0[:A_HEADS * A_QK_DIM] = qk_scale
    cs0[o3:o3 + B_HEADS * HEAD_DIM] = qk_scale
    qat, vat, qbt, vbt, keys0 = proj0_call(h, attn_norm[0], w0, jnp.asarray(cs0), B, S)
    keys0 = keys0.reshape(B, S, -1)
    bias_a = bias_tiles(bias_table, nvar=2 * BIAS_REACH + 1, nheads=A_HEADS, head0=0, rows=T, cols=T,
                        off0=-BIAS_REACH * T, off_step=T, row_coef=1, col_coef=-1, dil=1,
                        half_window=None, name="bias_a")
    oa = attn_a(qat, keys0, vat, bias_a, ab_lambda_q1[0], ab_lambda_k1[0], ab_lambda_q2[0],
                ab_lambda_k2[0], ab_subln[0], 0.8 - 0.6 * math.exp(-0.3 * 0))

    bias_b = bias_tiles(bias_table, nvar=3, nheads=B_HEADS, head0=A_HEADS, rows=QCOLS + 2 * B_HALF_WINDOW,
                        cols=QCOLS, off0=0, off_step=-B_HALF_WINDOW, row_coef=1, col_coef=-1, dil=1,
                        half_window=B_HALF_WINDOW, name="bias_b")
    ob = attn_b(qbt, keys0, (A_HEADS * A_QK_DIM) // LANE, vbt, bias_b, ab_sink[0])

    wo = ab_w_o[0].astype(BF16)
    wo_a = wo[:A_HEADS * A_V_DIM]
    wo_b = _blocks(wo, [A_HEADS * A_V_DIM + hd * HEAD_DIM for hd in B_HEAD_ORDER], HEAD_DIM, 0)
    h = ffn(h, [(oa.reshape(M, -1), wo_a), (ob.reshape(M, -1), wo_b)], ffn_norm[0], ffn_w_gate, ffn_w_up,
            ffn_w_down, 0, final_norm, False, "ffn0")

    w1 = cd_w_in[0]
    gw = C_HEADS_PER_GROUP * HEAD_DIM
    cw = C_HEADS * HEAD_DIM
    starts1 = [role * cw + g * gw for g in range(len(C_PATTERNS)) for role in range(3)]
    cs1 = np.ones((CD_C_IN,), np.float32)
    for g in range(len(C_PATTERNS)):
        cs1[3 * g * gw:(3 * g + 1) * gw] = qk_scale
    c_views = proj1_call(h, attn_norm[1], _blocks(w1.astype(BF16), starts1, gw, 1), jnp.asarray(cs1), B, S)

    oc, lses = [], []
    for g, (window, dil) in enumerate(C_PATTERNS):
        hw = window // (2 * dil)
        tq_c = 128
        bias_c = bias_tiles(bias_table, nvar=3, nheads=C_HEADS_PER_GROUP, head0=g * C_HEADS_PER_GROUP,
                            rows=tq_c, cols=tq_c + 2 * hw, off0=0, off_step=-hw, row_coef=-1, col_coef=1,
                            dil=dil, half_window=hw, name=f"bias_c{g}")
        o_g, lse_g = attn_c(c_views[g], bias_c, dil=dil, tq=tq_c, half_window=hw, name=f"attn_c{g}")
        oc.append(o_g)
        lses.append(lse_g)

    o_kv = CD_C_IN + D_Q_LORA + D_KV_LORA
    w_pe = w1[:, o_kv:]
    lane_pad = lambda w: jnp.pad(w, ((0, 0), (D_NOPE, LANE - D_NOPE - D_ROPE)))
    wa = jnp.concatenate([w1[:, CD_C_IN:o_kv], lane_pad(w_pe), lane_pad(_rot_partner_cols(w_pe))],
                         axis=1).astype(BF16)
    wq3 = cd_w_q_b[0].reshape(D_Q_LORA, D_HEADS, D_NOPE + D_ROPE)
    zpad = jnp.zeros((D_Q_LORA, D_HEADS, LANE - D_NOPE - D_ROPE), F32)
    wq_main = jnp.concatenate([wq3, zpad], axis=-1)
    wq_rot = jnp.concatenate([jnp.zeros_like(wq3[..., :D_NOPE]), _rot_partner_cols(wq3[..., D_NOPE:]), zpad],
                             axis=-1)
    wq = jnp.concatenate([wq_main.reshape(D_Q_LORA, -1), wq_rot.reshape(D_Q_LORA, -1)], axis=1).astype(BF16)
    wkv3 = cd_w_kv_b[0].reshape(D_KV_LORA, D_HEADS, D_NOPE + D_V)
    wk = jnp.pad(wkv3[..., :D_NOPE], ((0, 0), (0, 0), (0, LANE - D_NOPE))).reshape(D_KV_LORA, -1)
    wv = wkv3[..., D_NOPE:].reshape(D_KV_LORA, -1)
    wkv = jnp.concatenate([wk, wv], axis=1).astype(BF16)
    cos_l, sin_l = rope_tables(S)
    qdt, kd, vdt = prep_d(h, attn_norm[1], wa, cd_q_norm[0], cd_kv_norm[0], wq, wkv, cos_l, sin_l, B, S)
    od = attn_d(qdt, kd.reshape(B, S, D_HEADS * LANE), vdt)

    wo1 = cd_w_o[0]
    wo_c = wo1[:C_HEADS_PER_GROUP * HEAD_DIM].astype(BF16)
    wo_d = wo1[C_HEADS_PER_GROUP * HEAD_DIM:].astype(BF16)
    h = ffn(h, [(mix_c(oc, lses, B, S), wo_c), (od.reshape(M, -1), wo_d)], ffn_norm[1], ffn_w_gate, ffn_w_up,
            ffn_w_down, 1, final_norm, True, "ffn1")
    return h.reshape(B, S, D_MODEL)
```
